```python
import jax, jax.numpy as jnp
from jax import lax
import numpy as np

D_MODEL = 2048
BATCH = 8
SEQ = 8192
DEPTH = 1

HEAD_DIM = 128
N_HEADS_DIL = 8
N_HEADS_FOX = 8
D_DIL = N_HEADS_DIL * HEAD_DIM
D_FOX = N_HEADS_FOX * HEAD_DIM
DIL_PATTERNS = ((128, 1), (512, 4), (2048, 16))
MAX_WINDOW = 2048
Q_BLOCK = 128
ROPE_THETA = 500000.0
ROPE_DIM = HEAD_DIM // 4
D_FF = 5632
NORM_EPS = 1e-6
IN_SPLITS = (D_DIL, D_DIL, D_DIL, D_FOX, D_FOX, D_FOX, N_HEADS_FOX, D_MODEL, D_MODEL)
IN_COLS = sum(IN_SPLITS)

kernel_name = "hybrid_dilated_fox_gated_macaron"


def rmsnorm(x, g):
    xf = x.astype(jnp.float32)
    y = xf * lax.rsqrt(jnp.mean(xf * xf, axis=-1, keepdims=True) + NORM_EPS)
    return (y * g.astype(jnp.float32)).astype(x.dtype)


def swiglu(x, w_gate, w_up, w_down):
    return (jax.nn.silu(x @ w_gate) * (x @ w_up)) @ w_down


def split_heads(t, n_heads):
    b, s, _ = t.shape
    return t.reshape(b, s, n_heads, HEAD_DIM).transpose(0, 2, 1, 3)


def merge_heads(t):
    b, h, s, d = t.shape
    return t.transpose(0, 2, 1, 3).reshape(b, s, h * d)


def partial_rope(t):
    s = t.shape[2]
    half = ROPE_DIM // 2
    pos = jnp.arange(s, dtype=jnp.float32)
    inv_freq = ROPE_THETA ** (-jnp.arange(0, ROPE_DIM, 2, dtype=jnp.float32) / ROPE_DIM)
    ang = pos[:, None] * inv_freq[None, :]
    cos, sin = jnp.cos(ang), jnp.sin(ang)
    rot = t[..., :ROPE_DIM].astype(jnp.float32)
    x1, x2 = rot[..., :half], rot[..., half:]
    rotated = jnp.concatenate([x1 * cos - x2 * sin, x2 * cos + x1 * sin], axis=-1)
    return jnp.concatenate([rotated.astype(t.dtype), t[..., ROPE_DIM:]], axis=-1)


def dilated_attention(q, k, v):
    b, h, s, dh = q.shape
    n_blocks = s // Q_BLOCK
    pad = ((0, 0), (0, 0), (MAX_WINDOW, 0), (0, 0))
    kp = jnp.pad(k, pad)
    vp = jnp.pad(v, pad)
    qb = q.reshape(b, h, n_blocks, Q_BLOCK, dh).transpose(2, 0, 1, 3, 4)

    def block(args):
        qblk, blk = args
        t0 = blk * Q_BLOCK
        nums, dens, maxs = [], [], []
        for w, d in DIL_PATTERNS:
            length = w + Q_BLOCK
            rows = length // d
            nq = Q_BLOCK // d
            start = t0 + MAX_WINDOW - w
            kr = lax.dynamic_slice_in_dim(kp, start, length, axis=2).reshape(b, h, rows, d, dh)
            vr = lax.dynamic_slice_in_dim(vp, start, length, axis=2).reshape(b, h, rows, d, dh)
            qr = qblk.reshape(b, h, nq, d, dh)
            sc = jnp.einsum('bhicd,bhrcd->bhcir', qr, kr, preferred_element_type=jnp.float32)
            r = jnp.arange(rows)
            ip = jnp.arange(nq)
            cc = jnp.arange(d)
            band = (r[None, :] >= ip[:, None]) & (r[None, :] <= ip[:, None] + w // d)
            valid = (t0 - w + r[None, :] * d + cc[:, None]) >= 0
            mask = band[None, :, :] & valid[:, None, :]
            sc = jnp.where(mask, sc, -jnp.inf)
            m = jnp.max(sc, axis=-1, keepdims=True)
            p = jnp.exp(sc - m)
            den = jnp.sum(p, axis=-1)
            num = jnp.einsum('bhcir,bhrcd->bhicd', p, vr.astype(jnp.float32))
            nums.append(num.reshape(b, h, Q_BLOCK, dh))
            dens.append(den.transpose(0, 1, 3, 2).reshape(b, h, Q_BLOCK))
            maxs.append(m[..., 0].transpose(0, 1, 3, 2).reshape(b, h, Q_BLOCK))
        num = jnp.stack(nums)
        den = jnp.stack(dens)
        mx = jnp.stack(maxs)
        wgt = jnp.exp(mx - jnp.max(mx, axis=0, keepdims=True))
        out = jnp.sum(num * wgt[..., None], axis=0) / jnp.sum(den * wgt, axis=0)[..., None]
        return out.astype(q.dtype)

    out = lax.map(block, (qb, jnp.arange(n_blocks)))
    return out.transpose(1, 2, 0, 3, 4).reshape(b, h, s, dh)


def forgetting_attention(q, k, v, log_f):
    b, h, s, dh = q.shape
    n_blocks = s // Q_BLOCK
    c = jnp.cumsum(log_f, axis=-1)
    qb = q.reshape(b, h, n_blocks, Q_BLOCK, dh).transpose(2, 0, 1, 3, 4)
    cb = c.reshape(b, h, n_blocks, Q_BLOCK).transpose(2, 0, 1, 3)
    kpos = jnp.arange(s)

    def block(args):
        qblk, cq, blk = args
        sc = jnp.einsum('bhqd,bhkd->bhqk', qblk, k, preferred_element_type=jnp.float32)
        sc = sc + cq[..., :, None] - c[:, :, None, :]
        qpos = blk * Q_BLOCK + jnp.arange(Q_BLOCK)
        sc = jnp.where(kpos[None, :] <= qpos[:, None], sc, -jnp.inf)
        p = jax.nn.softmax(sc, axis=-1)
        return jnp.einsum('bhqk,bhkd->bhqd', p.astype(v.dtype), v)

    out = lax.map(block, (qb, cb, jnp.arange(n_blocks)))
    return out.transpose(1, 2, 0, 3, 4).reshape(b, h, s, dh)


def _fwd_setup_inputs(seed: int = 0) -> dict:
    key = jax.random.key(seed)
    ks = jax.random.split(key, 20)
    f32 = jnp.float32

    def w(k, shape, fan_in):
        return jax.random.normal(k, shape, f32) * fan_in ** -0.5

    def gain(k):
        return 1.0 + 0.02 * jax.random.normal(k, (DEPTH, D_MODEL), f32)

    return {
        "x": jax.random.normal(ks[0], (BATCH, SEQ, D_MODEL), f32),
        "ffn1_norm": gain(ks[1]),
        "ffn1_w_gate": w(ks[2], (DEPTH, D_MODEL, D_FF), D_MODEL),
        "ffn1_w_up": w(ks[3], (DEPTH, D_MODEL, D_FF), D_MODEL),
        "ffn1_w_down": w(ks[4], (DEPTH, D_FF, D_MODEL), D_FF),
        "mix_norm": gain(ks[5]),
        "w_in": w(ks[6], (DEPTH, D_MODEL, IN_COLS), D_MODEL),
        "b_forget": 4.0 + jax.random.normal(ks[7], (DEPTH, N_HEADS_FOX), f32),
        "b_gate_dil": 0.02 * jax.random.normal(ks[8], (DEPTH, D_MODEL), f32),
        "b_gate_fox": 0.02 * jax.random.normal(ks[9], (DEPTH, D_MODEL), f32),
        "w_proj_dil": w(ks[10], (DEPTH, D_DIL, D_MODEL), D_DIL),
        "w_proj_fox": w(ks[11], (DEPTH, D_FOX, D_MODEL), D_FOX),
        "w_out": w(ks[12], (DEPTH, D_MODEL, D_MODEL), D_MODEL),
        "ffn2_norm": gain(ks[13]),
        "ffn2_w_gate": w(ks[14], (DEPTH, D_MODEL, D_FF), D_MODEL),
        "ffn2_w_up": w(ks[15], (DEPTH, D_MODEL, D_FF), D_MODEL),
        "ffn2_w_down": w(ks[16], (DEPTH, D_FF, D_MODEL), D_FF),
        "final_norm": 1.0 + 0.02 * jax.random.normal(ks[17], (D_MODEL,), f32),
    }


def _fwd_reference(x, ffn1_norm, ffn1_w_gate, ffn1_w_up, ffn1_w_down, mix_norm, w_in,
              b_forget, b_gate_dil, b_gate_fox, w_proj_dil, w_proj_fox, w_out,
              ffn2_norm, ffn2_w_gate, ffn2_w_up, ffn2_w_down, final_norm):
    split_points = list(np.cumsum(IN_SPLITS)[:-1])
    scale = HEAD_DIM ** -0.5
    for l in range(DEPTH):
        x = x + 0.5 * swiglu(rmsnorm(x, ffn1_norm[l]), ffn1_w_gate[l], ffn1_w_up[l], ffn1_w_down[l])

        h = rmsnorm(x, mix_norm[l])
        proj = h @ w_in[l]
        q_d, k_d, v_d, q_f, k_f, v_f, f_logit, g_d, g_f = jnp.split(proj, split_points, axis=-1)

        qa = partial_rope(split_heads(q_d, N_HEADS_DIL)) * scale
        ka = partial_rope(split_heads(k_d, N_HEADS_DIL))
        va = split_heads(v_d, N_HEADS_DIL)
        y_dil = merge_heads(dilated_attention(qa, ka, va))

        log_f = jax.nn.log_sigmoid((f_logit + b_forget[l]).astype(jnp.float32))
        log_f = log_f.transpose(0, 2, 1)
        qb_ = split_heads(q_f, N_HEADS_FOX) * scale
        kb_ = split_heads(k_f, N_HEADS_FOX)
        vb_ = split_heads(v_f, N_HEADS_FOX)
        y_fox = merge_heads(forgetting_attention(qb_, kb_, vb_, log_f))

        merged = (jax.nn.sigmoid(g_d + b_gate_dil[l]) * (y_dil @ w_proj_dil[l])
                  + jax.nn.sigmoid(g_f + b_gate_fox[l]) * (y_fox @ w_proj_fox[l]))
        x = x + merged @ w_out[l]

        x = x + 0.5 * swiglu(rmsnorm(x, ffn2_norm[l]), ffn2_w_gate[l], ffn2_w_up[l], ffn2_w_down[l])
    return rmsnorm(x, final_norm)


import jax as _jax
import jax.numpy as _jnp

TWIN_FORMAT = 'train_step'
FWD_PARAMS = ['x', 'ffn1_norm', 'ffn1_w_gate', 'ffn1_w_up', 'ffn1_w_down', 'mix_norm', 'w_in', 'b_forget', 'b_gate_dil', 'b_gate_fox', 'w_proj_dil', 'w_proj_fox', 'w_out', 'ffn2_norm', 'ffn2_w_gate', 'ffn2_w_up', 'ffn2_w_down', 'final_norm']
TWIN_WEIGHTS = ['ffn1_norm', 'ffn1_w_gate', 'ffn1_w_up', 'ffn1_w_down', 'mix_norm', 'w_in', 'b_forget', 'b_gate_dil', 'b_gate_fox', 'w_proj_dil', 'w_proj_fox', 'w_out', 'ffn2_norm', 'ffn2_w_gate', 'ffn2_w_up', 'ffn2_w_down', 'final_norm']
TWIN_DIFF_INPUT = 'x'
TWIN_INPUTS = ['x', 'ffn1_norm', 'ffn1_w_gate', 'ffn1_w_up', 'ffn1_w_down', 'mix_norm', 'w_in', 'b_forget', 'b_gate_dil', 'b_gate_fox', 'w_proj_dil', 'w_proj_fox', 'w_out', 'ffn2_norm', 'ffn2_w_gate', 'ffn2_w_up', 'ffn2_w_down', 'final_norm', 'loss_target', 'm_ffn1_norm', 'm_ffn1_w_gate', 'm_ffn1_w_up', 'm_ffn1_w_down', 'm_mix_norm', 'm_w_in', 'm_b_forget', 'm_b_gate_dil', 'm_b_gate_fox', 'm_w_proj_dil', 'm_w_proj_fox', 'm_w_out', 'm_ffn2_norm', 'm_ffn2_w_gate', 'm_ffn2_w_up', 'm_ffn2_w_down', 'm_final_norm', 'v_ffn1_norm', 'v_ffn1_w_gate', 'v_ffn1_w_up', 'v_ffn1_w_down', 'v_mix_norm', 'v_w_in', 'v_b_forget', 'v_b_gate_dil', 'v_b_gate_fox', 'v_w_proj_dil', 'v_w_proj_fox', 'v_w_out', 'v_ffn2_norm', 'v_ffn2_w_gate', 'v_ffn2_w_up', 'v_ffn2_w_down', 'v_final_norm']
TWIN_OUTPUTS = ['loss', 'grad_x', 'grad_ffn1_norm', 'grad_ffn1_w_gate', 'grad_ffn1_w_up', 'grad_ffn1_w_down', 'grad_mix_norm', 'grad_w_in', 'grad_b_forget', 'grad_b_gate_dil', 'grad_b_gate_fox', 'grad_w_proj_dil', 'grad_w_proj_fox', 'grad_w_out', 'grad_ffn2_norm', 'grad_ffn2_w_gate', 'grad_ffn2_w_up', 'grad_ffn2_w_down', 'grad_final_norm', 'delta_ffn1_norm', 'delta_ffn1_w_gate', 'delta_ffn1_w_up', 'delta_ffn1_w_down', 'delta_mix_norm', 'delta_w_in', 'delta_b_forget', 'delta_b_gate_dil', 'delta_b_gate_fox', 'delta_w_proj_dil', 'delta_w_proj_fox', 'delta_w_out', 'delta_ffn2_norm', 'delta_ffn2_w_gate', 'delta_ffn2_w_up', 'delta_ffn2_w_down', 'delta_final_norm', 'new_m_ffn1_norm', 'new_m_ffn1_w_gate', 'new_m_ffn1_w_up', 'new_m_ffn1_w_down', 'new_m_mix_norm', 'new_m_w_in', 'new_m_b_forget', 'new_m_b_gate_dil', 'new_m_b_gate_fox', 'new_m_w_proj_dil', 'new_m_w_proj_fox', 'new_m_w_out', 'new_m_ffn2_norm', 'new_m_ffn2_w_gate', 'new_m_ffn2_w_up', 'new_m_ffn2_w_down', 'new_m_final_norm', 'new_v_ffn1_norm', 'new_v_ffn1_w_gate', 'new_v_ffn1_w_up', 'new_v_ffn1_w_down', 'new_v_mix_norm', 'new_v_w_in', 'new_v_b_forget', 'new_v_b_gate_dil', 'new_v_b_gate_fox', 'new_v_w_proj_dil', 'new_v_w_proj_fox', 'new_v_w_out', 'new_v_ffn2_norm', 'new_v_ffn2_w_gate', 'new_v_ffn2_w_up', 'new_v_ffn2_w_down', 'new_v_final_norm']
TWIN_LEAF_KINDS = {'loss': 'loss', 'grad_x': 'grad_x', 'grad_ffn1_norm': 'grad_w', 'grad_ffn1_w_gate': 'grad_w', 'grad_ffn1_w_up': 'grad_w', 'grad_ffn1_w_down': 'grad_w', 'grad_mix_norm': 'grad_w', 'grad_w_in': 'grad_w', 'grad_b_forget': 'grad_w', 'grad_b_gate_dil': 'grad_w', 'grad_b_gate_fox': 'grad_w', 'grad_w_proj_dil': 'grad_w', 'grad_w_proj_fox': 'grad_w', 'grad_w_out': 'grad_w', 'grad_ffn2_norm': 'grad_w', 'grad_ffn2_w_gate': 'grad_w', 'grad_ffn2_w_up': 'grad_w', 'grad_ffn2_w_down': 'grad_w', 'grad_final_norm': 'grad_w', 'delta_ffn1_norm': 'delta_w', 'delta_ffn1_w_gate': 'delta_w', 'delta_ffn1_w_up': 'delta_w', 'delta_ffn1_w_down': 'delta_w', 'delta_mix_norm': 'delta_w', 'delta_w_in': 'delta_w', 'delta_b_forget': 'delta_w', 'delta_b_gate_dil': 'delta_w', 'delta_b_gate_fox': 'delta_w', 'delta_w_proj_dil': 'delta_w', 'delta_w_proj_fox': 'delta_w', 'delta_w_out': 'delta_w', 'delta_ffn2_norm': 'delta_w', 'delta_ffn2_w_gate': 'delta_w', 'delta_ffn2_w_up': 'delta_w', 'delta_ffn2_w_down': 'delta_w', 'delta_final_norm': 'delta_w', 'new_m_ffn1_norm': 'new_m', 'new_m_ffn1_w_gate': 'new_m', 'new_m_ffn1_w_up': 'new_m', 'new_m_ffn1_w_down': 'new_m', 'new_m_mix_norm': 'new_m', 'new_m_w_in': 'new_m', 'new_m_b_forget': 'new_m', 'new_m_b_gate_dil': 'new_m', 'new_m_b_gate_fox': 'new_m', 'new_m_w_proj_dil': 'new_m', 'new_m_w_proj_fox': 'new_m', 'new_m_w_out': 'new_m', 'new_m_ffn2_norm': 'new_m', 'new_m_ffn2_w_gate': 'new_m', 'new_m_ffn2_w_up': 'new_m', 'new_m_ffn2_w_down': 'new_m', 'new_m_final_norm': 'new_m', 'new_v_ffn1_norm': 'new_v', 'new_v_ffn1_w_gate': 'new_v', 'new_v_ffn1_w_up': 'new_v', 'new_v_ffn1_w_down': 'new_v', 'new_v_mix_norm': 'new_v', 'new_v_w_in': 'new_v', 'new_v_b_forget': 'new_v', 'new_v_b_gate_dil': 'new_v', 'new_v_b_gate_fox': 'new_v', 'new_v_w_proj_dil': 'new_v', 'new_v_w_proj_fox': 'new_v', 'new_v_w_out': 'new_v', 'new_v_ffn2_norm': 'new_v', 'new_v_ffn2_w_gate': 'new_v', 'new_v_ffn2_w_up': 'new_v', 'new_v_ffn2_w_down': 'new_v', 'new_v_final_norm': 'new_v'}


def _forward(args):
    return _fwd_reference(*[args[k] for k in FWD_PARAMS])


def _output_shape():
    def fwd():
        inp = _fwd_setup_inputs(0)
        return _fwd_reference(*[inp[k] for k in FWD_PARAMS])
    out = _jax.eval_shape(fwd)
    return out.shape, out.dtype

N_MICROBATCH = 1
ADAM_LR = 0.001
ADAM_B1 = 0.9
ADAM_B2 = 0.999
ADAM_EPS = 1e-08
ADAM_WD = 0.01
ADAM_STEP = 10
PER_EXAMPLE_BATCH_AXIS = {'x': 0, 'loss_target': 0}
SHARED_INPUTS = []
_WEIGHT_DTYPES = {'ffn1_norm': _jnp.float32, 'ffn1_w_gate': _jnp.float32, 'ffn1_w_up': _jnp.float32, 'ffn1_w_down': _jnp.float32, 'mix_norm': _jnp.float32, 'w_in': _jnp.float32, 'b_forget': _jnp.float32, 'b_gate_dil': _jnp.float32, 'b_gate_fox': _jnp.float32, 'w_proj_dil': _jnp.float32, 'w_proj_fox': _jnp.float32, 'w_out': _jnp.float32, 'ffn2_norm': _jnp.float32, 'ffn2_w_gate': _jnp.float32, 'ffn2_w_up': _jnp.float32, 'ffn2_w_down': _jnp.float32, 'final_norm': _jnp.float32}
MOMENT_SCALE = {'ffn1_norm': 5.514681e-02, 'ffn1_w_gate': 2.375039e-02, 'ffn1_w_up': 2.300070e-02, 'ffn1_w_down': 3.815062e-02, 'mix_norm': 3.925439e-02, 'w_in': 1.764804e-02, 'b_forget': 3.019563e-01, 'b_gate_dil': 5.906258e-03, 'b_gate_fox': 7.831327e-03, 'w_proj_dil': 1.489196e-02, 'w_proj_fox': 2.005206e-02, 'w_out': 2.500225e-02, 'ffn2_norm': 4.922803e-02, 'ffn2_w_gate': 2.138374e-02, 'ffn2_w_up': 2.071044e-02, 'ffn2_w_down': 3.435943e-02, 'final_norm': 3.196659e+01}


def _to_microbatches(a, axis):
    t = _jnp.moveaxis(a, axis, 0)
    t = t.reshape((N_MICROBATCH, t.shape[0] // N_MICROBATCH) + t.shape[1:])
    return _jnp.moveaxis(t, 1, axis + 1)


def setup_inputs(seed: int = 0) -> dict:
    inp = _fwd_setup_inputs(seed)
    key = _jax.random.fold_in(_jax.random.key(seed), 7919)
    shape, _ = _output_shape()
    out = dict(inp)
    out["loss_target"] = _jax.random.normal(_jax.random.fold_in(key, 0), shape, _jnp.float32)
    for i, name in enumerate(TWIN_WEIGHTS):
        w = inp[name].astype(_jnp.float32)
        if MOMENT_SCALE is None:
            s = _jnp.sqrt(_jnp.mean(_jnp.square(w)) + 1e-30)
        else:
            s = MOMENT_SCALE[name]
        km, kv = _jax.random.split(_jax.random.fold_in(key, i + 1))
        out[name] = w
        out["m_" + name] = s * _jax.random.normal(km, w.shape, _jnp.float32)
        out["v_" + name] = (s * s) * _jax.random.uniform(kv, w.shape, _jnp.float32, 0.5, 1.5)
    if N_MICROBATCH > 1:
        for name, axis in PER_EXAMPLE_BATCH_AXIS.items():
            out[name] = _to_microbatches(out[name], axis)
    return {'x': out['x'], 'ffn1_norm': out['ffn1_norm'], 'ffn1_w_gate': out['ffn1_w_gate'], 'ffn1_w_up': out['ffn1_w_up'], 'ffn1_w_down': out['ffn1_w_down'], 'mix_norm': out['mix_norm'], 'w_in': out['w_in'], 'b_forget': out['b_forget'], 'b_gate_dil': out['b_gate_dil'], 'b_gate_fox': out['b_gate_fox'], 'w_proj_dil': out['w_proj_dil'], 'w_proj_fox': out['w_proj_fox'], 'w_out': out['w_out'], 'ffn2_norm': out['ffn2_norm'], 'ffn2_w_gate': out['ffn2_w_gate'], 'ffn2_w_up': out['ffn2_w_up'], 'ffn2_w_down': out['ffn2_w_down'], 'final_norm': out['final_norm'], 'loss_target': out['loss_target'], 'm_ffn1_norm': out['m_ffn1_norm'], 'm_ffn1_w_gate': out['m_ffn1_w_gate'], 'm_ffn1_w_up': out['m_ffn1_w_up'], 'm_ffn1_w_down': out['m_ffn1_w_down'], 'm_mix_norm': out['m_mix_norm'], 'm_w_in': out['m_w_in'], 'm_b_forget': out['m_b_forget'], 'm_b_gate_dil': out['m_b_gate_dil'], 'm_b_gate_fox': out['m_b_gate_fox'], 'm_w_proj_dil': out['m_w_proj_dil'], 'm_w_proj_fox': out['m_w_proj_fox'], 'm_w_out': out['m_w_out'], 'm_ffn2_norm': out['m_ffn2_norm'], 'm_ffn2_w_gate': out['m_ffn2_w_gate'], 'm_ffn2_w_up': out['m_ffn2_w_up'], 'm_ffn2_w_down': out['m_ffn2_w_down'], 'm_final_norm': out['m_final_norm'], 'v_ffn1_norm': out['v_ffn1_norm'], 'v_ffn1_w_gate': out['v_ffn1_w_gate'], 'v_ffn1_w_up': out['v_ffn1_w_up'], 'v_ffn1_w_down': out['v_ffn1_w_down'], 'v_mix_norm': out['v_mix_norm'], 'v_w_in': out['v_w_in'], 'v_b_forget': out['v_b_forget'], 'v_b_gate_dil': out['v_b_gate_dil'], 'v_b_gate_fox': out['v_b_gate_fox'], 'v_w_proj_dil': out['v_w_proj_dil'], 'v_w_proj_fox': out['v_w_proj_fox'], 'v_w_out': out['v_w_out'], 'v_ffn2_norm': out['v_ffn2_norm'], 'v_ffn2_w_gate': out['v_ffn2_w_gate'], 'v_ffn2_w_up': out['v_ffn2_w_up'], 'v_ffn2_w_down': out['v_ffn2_w_down'], 'v_final_norm': out['v_final_norm']}


def _loss(weights, diff, rest, loss_target):
    with _jax.named_scope("forward"):
        args = {**rest, TWIN_DIFF_INPUT: diff, **{k: w.astype(_WEIGHT_DTYPES[k]) for k, w in weights.items()}}
        y = _forward(args)
    with _jax.named_scope("loss_head"):
        err = _jnp.square(y.astype(_jnp.float32) - loss_target)
        return 0.5 * _jnp.sum(_jnp.mean(err, axis=-1)) if err.ndim else 0.5 * err


def _adamw(w, g, m, v):
    m = ADAM_B1 * m + (1.0 - ADAM_B1) * g
    v = ADAM_B2 * v + (1.0 - ADAM_B2) * _jnp.square(g)
    m_hat = m / (1.0 - ADAM_B1 ** ADAM_STEP)
    v_hat = v / (1.0 - ADAM_B2 ** ADAM_STEP)
    delta = -ADAM_LR * (m_hat / (_jnp.sqrt(v_hat) + ADAM_EPS) + ADAM_WD * w)
    return delta, m, v


def reference(x, ffn1_norm, ffn1_w_gate, ffn1_w_up, ffn1_w_down, mix_norm, w_in, b_forget, b_gate_dil, b_gate_fox, w_proj_dil, w_proj_fox, w_out, ffn2_norm, ffn2_w_gate, ffn2_w_up, ffn2_w_down, final_norm, loss_target, m_ffn1_norm, m_ffn1_w_gate, m_ffn1_w_up, m_ffn1_w_down, m_mix_norm, m_w_in, m_b_forget, m_b_gate_dil, m_b_gate_fox, m_w_proj_dil, m_w_proj_fox, m_w_out, m_ffn2_norm, m_ffn2_w_gate, m_ffn2_w_up, m_ffn2_w_down, m_final_norm, v_ffn1_norm, v_ffn1_w_gate, v_ffn1_w_up, v_ffn1_w_down, v_mix_norm, v_w_in, v_b_forget, v_b_gate_dil, v_b_gate_fox, v_w_proj_dil, v_w_proj_fox, v_w_out, v_ffn2_norm, v_ffn2_w_gate, v_ffn2_w_up, v_ffn2_w_down, v_final_norm):
    given = dict(x=x, ffn1_norm=ffn1_norm, ffn1_w_gate=ffn1_w_gate, ffn1_w_up=ffn1_w_up, ffn1_w_down=ffn1_w_down, mix_norm=mix_norm, w_in=w_in, b_forget=b_forget, b_gate_dil=b_gate_dil, b_gate_fox=b_gate_fox, w_proj_dil=w_proj_dil, w_proj_fox=w_proj_fox, w_out=w_out, ffn2_norm=ffn2_norm, ffn2_w_gate=ffn2_w_gate, ffn2_w_up=ffn2_w_up, ffn2_w_down=ffn2_w_down, final_norm=final_norm, loss_target=loss_target, m_ffn1_norm=m_ffn1_norm, m_ffn1_w_gate=m_ffn1_w_gate, m_ffn1_w_up=m_ffn1_w_up, m_ffn1_w_down=m_ffn1_w_down, m_mix_norm=m_mix_norm, m_w_in=m_w_in, m_b_forget=m_b_forget, m_b_gate_dil=m_b_gate_dil, m_b_gate_fox=m_b_gate_fox, m_w_proj_dil=m_w_proj_dil, m_w_proj_fox=m_w_proj_fox, m_w_out=m_w_out, m_ffn2_norm=m_ffn2_norm, m_ffn2_w_gate=m_ffn2_w_gate, m_ffn2_w_up=m_ffn2_w_up, m_ffn2_w_down=m_ffn2_w_down, m_final_norm=m_final_norm, v_ffn1_norm=v_ffn1_norm, v_ffn1_w_gate=v_ffn1_w_gate, v_ffn1_w_up=v_ffn1_w_up, v_ffn1_w_down=v_ffn1_w_down, v_mix_norm=v_mix_norm, v_w_in=v_w_in, v_b_forget=v_b_forget, v_b_gate_dil=v_b_gate_dil, v_b_gate_fox=v_b_gate_fox, v_w_proj_dil=v_w_proj_dil, v_w_proj_fox=v_w_proj_fox, v_w_out=v_w_out, v_ffn2_norm=v_ffn2_norm, v_ffn2_w_gate=v_ffn2_w_gate, v_ffn2_w_up=v_ffn2_w_up, v_ffn2_w_down=v_ffn2_w_down, v_final_norm=v_final_norm)
    weights = {n: given[n] for n in TWIN_WEIGHTS}
    shared = {n: given[n] for n in SHARED_INPUTS}
    per_example = {n: given[n] for n in ['x']}
    grad_fn = _jax.value_and_grad(_loss, argnums=(0, 1))

    def one_microbatch(ex, loss_target):
        ex = dict(ex)
        diff = ex.pop(TWIN_DIFF_INPUT)
        return grad_fn(weights, diff, {**shared, **ex}, loss_target)

    if N_MICROBATCH == 1:
        loss, (grad_w, grad_x) = one_microbatch(per_example, given["loss_target"])
    else:
        def body(carry, xs):
            loss_sum, grad_sum = carry
            l_k, (gw_k, gx_k) = one_microbatch(xs[0], xs[1])
            with _jax.named_scope("update"):
                return (loss_sum + l_k, _jax.tree.map(_jnp.add, grad_sum, gw_k)), gx_k

        init = (_jnp.zeros((), _jnp.float32), _jax.tree.map(_jnp.zeros_like, weights))
        (loss, grad_w), grad_x = _jax.lax.scan(body, init, (per_example, given["loss_target"]))
    with _jax.named_scope("update"):
        delta_w, new_m, new_v = {}, {}, {}
        for n in TWIN_WEIGHTS:
            delta_w[n], new_m[n], new_v[n] = _adamw(weights[n], grad_w[n], given["m_" + n], given["v_" + n])
    return (loss, grad_x, *[grad_w[n] for n in TWIN_WEIGHTS], *[delta_w[n] for n in TWIN_WEIGHTS],
            *[new_m[n] for n in TWIN_WEIGHTS], *[new_v[n] for n in TWIN_WEIGHTS])
```

```python
import functools

import numpy as np
import jax
import jax.numpy as jnp
from jax import lax
from jax.experimental import pallas as pl
from jax.experimental.pallas import tpu as pltpu

BF = jnp.bfloat16
F32 = jnp.float32
MESH = pl.DeviceIdType.MESH
N_DEV = 8

HEAD_DIM = 128
ROPE_DIM = HEAD_DIM // 4
ROPE_HALF = ROPE_DIM // 2
ROPE_THETA = 500000.0
NORM_EPS = 1e-6
DIL_PATTERNS = ((128, 1), (512, 4), (2048, 16))
MAX_WINDOW = 2048
LANE = 128
NEG = -1e30

ADAM_LR = 0.001
ADAM_B1 = 0.9
ADAM_B2 = 0.999
ADAM_EPS = 1e-08
ADAM_WD = 0.01
ADAM_STEP = 10

VMEM_LIMIT_BYTES = 56 * 1024 * 1024
ANY = pl.BlockSpec(memory_space=pl.ANY)

NN = (((1,), (0,)), ((), ()))
NT = (((1,), (1,)), ((), ()))
TN = (((0,), (0,)), ((), ()))


def _dot(a, b, dn=NN):
    return lax.dot_general(a, b, dn, preferred_element_type=F32)


def _sig(x):
    return 1.0 / (1.0 + jnp.exp(-x))


def _tile(n, pref, align):
    best = None
    t = align
    while t <= min(n, pref):
        if n % t == 0:
            best = t
        t += align
    return n if best is None else best


def _params():
    return pltpu.CompilerParams(vmem_limit_bytes=VMEM_LIMIT_BYTES)


def _peers():
    x, y, c = lax.axis_index("x"), lax.axis_index("y"), lax.axis_index("c")
    me = 4 * x + 2 * y + c
    peers = []
    for k in range(1, N_DEV):
        px = 1 - x if (k >> 2) & 1 else x
        py = 1 - y if (k >> 1) & 1 else y
        pc = 1 - c if k & 1 else c
        peers.append((k, (px, py, pc), 4 * px + 2 * py + pc))
    return me, peers


def _exchange(src, gather, name):
    shape = src.shape[-2:]

    def body(src_ref, out_ref, send_sems, recv_sems, local_sem):
        me, peers = _peers()

        def part(dest):
            return src_ref if gather else src_ref.at[dest]

        local = pltpu.make_async_copy(part(me), out_ref.at[me], local_sem)
        local.start()
        sends = []
        for k, peer, peer_flat in peers:
            cp = pltpu.make_async_remote_copy(
                src_ref=part(peer_flat), dst_ref=out_ref.at[me],
                send_sem=send_sems.at[k], recv_sem=recv_sems.at[k],
                device_id=peer, device_id_type=MESH)
            cp.start()
            sends.append(cp)
        for k, peer, peer_flat in peers:
            pltpu.make_async_remote_copy(
                src_ref=part(peer_flat), dst_ref=out_ref.at[peer_flat],
                send_sem=send_sems.at[k], recv_sem=recv_sems.at[k],
                device_id=peer, device_id_type=MESH).wait_recv()
        for cp in sends:
            cp.wait_send()
        local.wait()

    return pl.pallas_call(
        body, name=name,
        out_shape=jax.ShapeDtypeStruct((N_DEV,) + shape, src.dtype),
        in_specs=[ANY], out_specs=ANY,
        scratch_shapes=[pltpu.SemaphoreType.DMA((N_DEV,)), pltpu.SemaphoreType.DMA((N_DEV,)),
                        pltpu.SemaphoreType.DMA],
    )(src)


def _allreduce_small(p):
    rows, d = p.shape

    def body(p_ref, o_ref, recv_ref, send_sems, recv_sems):
        me, peers = _peers()
        recv_ref[me] = p_ref[...]
        sends = []
        for k, peer, peer_flat in peers:
            cp = pltpu.make_async_remote_copy(
                src_ref=p_ref, dst_ref=recv_ref.at[me],
                send_sem=send_sems.at[k], recv_sem=recv_sems.at[k],
                device_id=peer, device_id_type=MESH)
            cp.start()
            sends.append(cp)
        for k, peer, peer_flat in peers:
            pltpu.make_async_remote_copy(
                src_ref=p_ref, dst_ref=recv_ref.at[peer_flat],
                send_sem=send_sems.at[k], recv_sem=recv_sems.at[k],
                device_id=peer, device_id_type=MESH).wait_recv()
        for cp in sends:
            cp.wait_send()
        acc = recv_ref[0]
        for s in range(1, N_DEV):
            acc = acc + recv_ref[s]
        is_loss = lax.broadcasted_iota(jnp.int32, (rows, d), 0) == rows - 1
        total = jnp.sum(jnp.where(is_loss, acc, 0.0))
        o_ref[...] = jnp.where(is_loss, total, acc)

    return pl.pallas_call(
        body, name="allreduce_small",
        out_shape=jax.ShapeDtypeStruct((rows, d), F32),
        in_specs=[pl.BlockSpec(memory_space=pltpu.VMEM)],
        out_specs=pl.BlockSpec(memory_space=pltpu.VMEM),
        scratch_shapes=[pltpu.VMEM((N_DEV, rows, d), F32),
                        pltpu.SemaphoreType.DMA((N_DEV,)), pltpu.SemaphoreType.DMA((N_DEV,))],
    )(p)


def _adam_math(w, g, m, v):
    m2 = ADAM_B1 * m + (1.0 - ADAM_B1) * g
    v2 = ADAM_B2 * v + (1.0 - ADAM_B2) * (g * g)
    m_hat = m2 / (1.0 - ADAM_B1 ** ADAM_STEP)
    v_hat = v2 / (1.0 - ADAM_B2 ** ADAM_STEP)
    delta = -ADAM_LR * (m_hat / (jnp.sqrt(v_hat) + ADAM_EPS) + ADAM_WD * w)
    return delta, m2, v2


def _adam_from_partials(parts, w, m, v, name):
    r, c = w.shape
    tr = _tile(r, 256, 16)

    def body(p_ref, w_ref, m_ref, v_ref, g_out, d_out, m_out, v_out):
        g = p_ref[0].astype(F32)
        for s in range(1, N_DEV):
            g = g + p_ref[s].astype(F32)
        delta, m2, v2 = _adam_math(w_ref[...], g, m_ref[...], v_ref[...])
        g_out[...] = g
        d_out[...] = delta
        m_out[...] = m2
        v_out[...] = v2

    blk = pl.BlockSpec((tr, c), lambda i: (i, 0))
    out = jax.ShapeDtypeStruct((r, c), F32)
    return pl.pallas_call(
        body, name=name, grid=(r // tr,),
        in_specs=[pl.BlockSpec((N_DEV, tr, c), lambda i: (0, i, 0)), blk, blk, blk],
        out_specs=[blk, blk, blk, blk], out_shape=[out, out, out, out],
        compiler_params=_params(),
    )(parts, w, m, v)


def _adam_small(g, w, m, v):
    def body(g_ref, w_ref, m_ref, v_ref, d_out, m_out, v_out):
        delta, m2, v2 = _adam_math(w_ref[...], g_ref[...], m_ref[...], v_ref[...])
        d_out[...] = delta
        m_out[...] = m2
        v_out[...] = v2

    out = jax.ShapeDtypeStruct(g.shape, F32)
    return pl.pallas_call(body, name="adam_small", out_shape=[out, out, out])(g, w, m, v)


def _rms_fwd(x, gain, name):
    t, d = x.shape
    tr = _tile(t, 256, 16)

    def body(x_ref, g_ref, o_ref):
        xv = x_ref[...]
        r = lax.rsqrt(jnp.mean(xv * xv, axis=-1, keepdims=True) + NORM_EPS)
        o_ref[...] = (xv * r * g_ref[...]).astype(BF)

    return pl.pallas_call(
        body, name=name, grid=(t // tr,),
        in_specs=[pl.BlockSpec((tr, d), lambda i: (i, 0)), pl.BlockSpec((1, d), lambda i: (0, 0))],
        out_specs=pl.BlockSpec((tr, d), lambda i: (i, 0)),
        out_shape=jax.ShapeDtypeStruct((t, d), BF), compiler_params=_params(),
    )(x, gain)


def _rms_vjp(xv, gain, dy):
    r = lax.rsqrt(jnp.mean(xv * xv, axis=-1, keepdims=True) + NORM_EPS)
    xhat = xv * r
    dxhat = dy * gain
    dx = r * (dxhat - xhat * jnp.mean(dxhat * xhat, axis=-1, keepdims=True))
    dgain = jnp.sum(dy * xhat, axis=0, keepdims=True)
    return dx, dgain


def _rms_bwd(dy, x, gain, dres, name):
    t, d = x.shape
    tr = _tile(t, 256, 16)

    def body(dy_ref, x_ref, g_ref, dres_ref, dx_ref, dxb_ref, dg_ref):
        dx, dgain = _rms_vjp(x_ref[...], g_ref[...], dy_ref[...])
        dx = dx + dres_ref[...]
        dx_ref[...] = dx
        dxb_ref[...] = dx.astype(BF)

        @pl.when(pl.program_id(0) == 0)
        def _():
            dg_ref[...] = jnp.zeros_like(dg_ref)

        dg_ref[...] += dgain

    row = pl.BlockSpec((tr, d), lambda i: (i, 0))
    vec = pl.BlockSpec((1, d), lambda i: (0, 0))
    return pl.pallas_call(
        body, name=name, grid=(t // tr,),
        in_specs=[row, row, vec, row], out_specs=[row, row, vec],
        out_shape=[jax.ShapeDtypeStruct((t, d), F32), jax.ShapeDtypeStruct((t, d), BF),
                   jax.ShapeDtypeStruct((1, d), F32)],
        compiler_params=_params(),
    )(dy, x, gain, dres)


def _loss_head(x, gain, target):
    t, d = x.shape
    tr = _tile(t, 256, 16)

    def body(x_ref, g_ref, t_ref, dx_ref, dxb_ref, dg_ref, loss_ref):
        xv = x_ref[...]
        gain = g_ref[...]
        r = lax.rsqrt(jnp.mean(xv * xv, axis=-1, keepdims=True) + NORM_EPS)
        err = xv * r * gain - t_ref[...]
        dx, dgain = _rms_vjp(xv, gain, err * (1.0 / d))
        dx_ref[...] = dx
        dxb_ref[...] = dx.astype(BF)

        @pl.when(pl.program_id(0) == 0)
        def _():
            dg_ref[...] = jnp.zeros_like(dg_ref)
            loss_ref[...] = jnp.zeros_like(loss_ref)

        dg_ref[...] += dgain
        loss_ref[...] += jnp.sum(err * err, axis=0, keepdims=True) * (0.5 / d)

    row = pl.BlockSpec((tr, d), lambda i: (i, 0))
    vec = pl.BlockSpec((1, d), lambda i: (0, 0))
    return pl.pallas_call(
        body, name="loss_head", grid=(t // tr,),
        in_specs=[row, vec, row], out_specs=[row, row, vec, vec],
        out_shape=[jax.ShapeDtypeStruct((t, d), F32), jax.ShapeDtypeStruct((t, d), BF),
                   jax.ShapeDtypeStruct((1, d), F32), jax.ShapeDtypeStruct((1, d), F32)],
        compiler_params=_params(),
    )(x, gain, target)


def _mm_nn(a, b, out_dtype, name, residual=None, tm_pref=512, tn_pref=1152):
    m, k = a.shape
    n = b.shape[1]
    tm, tn = _tile(m, tm_pref, 16), _tile(n, tn_pref, LANE)

    def body(*refs):
        if residual is None:
            a_ref, b_ref, o_ref = refs
            o_ref[...] = _dot(a_ref[...], b_ref[...]).astype(out_dtype)
        else:
            a_ref, b_ref, r_ref, o_ref = refs
            o_ref[...] = (r_ref[...] + _dot(a_ref[...], b_ref[...])).astype(out_dtype)

    in_specs = [pl.BlockSpec((tm, k), lambda j, i: (i, 0)), pl.BlockSpec((k, tn), lambda j, i: (0, j))]
    args = [a, b]
    if residual is not None:
        in_specs.append(pl.BlockSpec((tm, tn), lambda j, i: (i, j)))
        args.append(residual)
    return pl.pallas_call(
        body, name=name, grid=(n // tn, m // tm), in_specs=in_specs,
        out_specs=pl.BlockSpec((tm, tn), lambda j, i: (i, j)),
        out_shape=jax.ShapeDtypeStruct((m, n), out_dtype), compiler_params=_params(),
    )(*args)


def _mm_nt(a, b, out_dtype, name, tm_pref=512, tn_pref=1024, tk_pref=2048):
    m, k = a.shape
    n = b.shape[0]
    tm, tn, tk = _tile(m, tm_pref, 16), _tile(n, tn_pref, LANE), _tile(k, tk_pref, LANE)
    nk = k // tk

    def body(a_ref, b_ref, o_ref, acc_ref):
        kk = pl.program_id(2)

        @pl.when(kk == 0)
        def _():
            acc_ref[...] = jnp.zeros_like(acc_ref)

        acc_ref[...] += _dot(a_ref[...], b_ref[...], NT)

        @pl.when(kk == nk - 1)
        def _():
            o_ref[...] = acc_ref[...].astype(out_dtype)

    return pl.pallas_call(
        body, name=name, grid=(n // tn, m // tm, nk),
        in_specs=[pl.BlockSpec((tm, tk), lambda j, i, kk: (i, kk)),
                  pl.BlockSpec((tn, tk), lambda j, i, kk: (j, kk))],
        out_specs=pl.BlockSpec((tm, tn), lambda j, i, kk: (i, j)),
        out_shape=jax.ShapeDtypeStruct((m, n), out_dtype),
        scratch_shapes=[pltpu.VMEM((tm, tn), F32)], compiler_params=_params(),
    )(a, b)


def _mm_tn(a, b, out_dtype, name, tn_pref=1152, tk_pref=512):
    t, k = a.shape
    n = b.shape[1]
    tn, tk = _tile(n, tn_pref, LANE), _tile(t, tk_pref, 16)
    nt = t // tk

    def body(a_ref, b_ref, o_ref, acc_ref):
        tt = pl.program_id(1)

        @pl.when(tt == 0)
        def _():
            acc_ref[...] = jnp.zeros_like(acc_ref)

        acc_ref[...] += _dot(a_ref[...], b_ref[...], TN)

        @pl.when(tt == nt - 1)
        def _():
            o_ref[...] = acc_ref[...].astype(out_dtype)

    return pl.pallas_call(
        body, name=name, grid=(n // tn, nt),
        in_specs=[pl.BlockSpec((tk, k), lambda j, tt: (tt, 0)), pl.BlockSpec((tk, tn), lambda j, tt: (tt, j))],
        out_specs=pl.BlockSpec((k, tn), lambda j, tt: (0, j)),
        out_shape=jax.ShapeDtypeStruct((k, n), out_dtype),
        scratch_shapes=[pltpu.VMEM((k, tn), F32)], compiler_params=_params(),
    )(a, b)


def _ffn_gate_up(hn, wg, wu, name):
    t, d = hn.shape
    ns, _, f = wg.shape
    tm = _tile(t, 512, 16)

    def body(h_ref, wg_ref, wu_ref, g_ref, u_ref, a_ref):
        h = h_ref[...]
        g = _dot(h, wg_ref[...])
        u = _dot(h, wu_ref[...])
        g_ref[...] = g.astype(BF)
        u_ref[...] = u.astype(BF)
        a_ref[...] = (g * _sig(g) * u).astype(BF)

    wspec = pl.BlockSpec((None, d, f), lambda j, i: (j, 0, 0))
    hid = pl.BlockSpec((None, tm, f), lambda j, i: (j, i, 0))
    out = jax.ShapeDtypeStruct((ns, t, f), BF)
    return pl.pallas_call(
        body, name=name, grid=(ns, t // tm),
        in_specs=[pl.BlockSpec((tm, d), lambda j, i: (i, 0)), wspec, wspec],
        out_specs=[hid, hid, hid], out_shape=[out, out, out], compiler_params=_params(),
    )(hn, wg, wu)


def _ffn_down(act, wd, xres, name):
    ns, t, f = act.shape
    d = wd.shape[2]
    tm = _tile(t, 512, 16)

    def body(a_ref, w_ref, x_ref, o_ref):
        @pl.when(pl.program_id(1) == 0)
        def _():
            o_ref[...] = x_ref[...]

        o_ref[...] += 0.5 * _dot(a_ref[...], w_ref[...])

    row = pl.BlockSpec((tm, d), lambda i, j: (i, 0))
    return pl.pallas_call(
        body, name=name, grid=(t // tm, ns),
        in_specs=[pl.BlockSpec((None, tm, f), lambda i, j: (j, i, 0)),
                  pl.BlockSpec((None, f, d), lambda i, j: (j, 0, 0)), row],
        out_specs=row, out_shape=jax.ShapeDtypeStruct((t, d), F32), compiler_params=_params(),
    )(act, wd, xres)


def _ffn_bwd_hidden(dxb, wd, g, u, name):
    t, d = dxb.shape
    ns, f, _ = wd.shape
    tm = _tile(t, 512, 16)

    def body(dx_ref, w_ref, g_ref, u_ref, dg_ref, du_ref):
        dh = 0.5 * _dot(dx_ref[...], w_ref[...], NT)
        gv = g_ref[...].astype(F32)
        uv = u_ref[...].astype(F32)
        s = _sig(gv)
        dg_ref[...] = (dh * uv * (s * (1.0 + gv * (1.0 - s)))).astype(BF)
        du_ref[...] = (dh * (gv * s)).astype(BF)

    hid = pl.BlockSpec((None, tm, f), lambda j, i: (j, i, 0))
    out = jax.ShapeDtypeStruct((ns, t, f), BF)
    return pl.pallas_call(
        body, name=name, grid=(ns, t // tm),
        in_specs=[pl.BlockSpec((tm, d), lambda j, i: (i, 0)),
                  pl.BlockSpec((None, f, d), lambda j, i: (j, 0, 0)), hid, hid],
        out_specs=[hid, hid], out_shape=[out, out], compiler_params=_params(),
    )(dxb, wd, g, u)


def _ffn_dw_down(act, dxb, name):
    ns, t, f = act.shape
    d = dxb.shape[1]
    tk = _tile(t, 512, 16)
    nt = t // tk

    def body(a_ref, dx_ref, o_ref, acc_ref):
        tt = pl.program_id(1)

        @pl.when(tt == 0)
        def _():
            acc_ref[...] = jnp.zeros_like(acc_ref)

        acc_ref[...] += _dot(a_ref[...], dx_ref[...], TN)

        @pl.when(tt == nt - 1)
        def _():
            o_ref[...] = (0.5 * acc_ref[...]).astype(BF)

    return pl.pallas_call(
        body, name=name, grid=(ns, nt),
        in_specs=[pl.BlockSpec((None, tk, f), lambda j, tt: (j, tt, 0)),
                  pl.BlockSpec((tk, d), lambda j, tt: (tt, 0))],
        out_specs=pl.BlockSpec((None, f, d), lambda j, tt: (j, 0, 0)),
        out_shape=jax.ShapeDtypeStruct((ns, f, d), BF),
        scratch_shapes=[pltpu.VMEM((f, d), F32)], compiler_params=_params(),
    )(act, dxb)


def _ffn_dw_gate_up(hn, dg, du, name):
    t, d = hn.shape
    ns, _, f = dg.shape
    tk = _tile(t, 512, 16)
    nt = t // tk

    def body(h_ref, dg_ref, du_ref, og_ref, ou_ref, accg_ref, accu_ref):
        tt = pl.program_id(1)

        @pl.when(tt == 0)
        def _():
            accg_ref[...] = jnp.zeros_like(accg_ref)
            accu_ref[...] = jnp.zeros_like(accu_ref)

        h = h_ref[...]
        accg_ref[...] += _dot(h, dg_ref[...], TN)
        accu_ref[...] += _dot(h, du_ref[...], TN)

        @pl.when(tt == nt - 1)
        def _():
            og_ref[...] = accg_ref[...].astype(BF)
            ou_ref[...] = accu_ref[...].astype(BF)

    hid = pl.BlockSpec((None, tk, f), lambda j, tt: (j, tt, 0))
    wspec = pl.BlockSpec((None, d, f), lambda j, tt: (j, 0, 0))
    out = jax.ShapeDtypeStruct((ns, d, f), BF)
    return pl.pallas_call(
        body, name=name, grid=(ns, nt),
        in_specs=[pl.BlockSpec((tk, d), lambda j, tt: (tt, 0)), hid, hid],
        out_specs=[wspec, wspec], out_shape=[out, out],
        scratch_shapes=[pltpu.VMEM((d, f), F32), pltpu.VMEM((d, f), F32)], compiler_params=_params(),
    )(hn, dg, du)


def _ffn_bwd_input(dg, du, wg, wu, name):
    ns, t, f = dg.shape
    d = wg.shape[1]
    tm = _tile(t, 512, 16)

    def body(dg_ref, du_ref, wg_ref, wu_ref, o_ref):
        @pl.when(pl.program_id(1) == 0)
        def _():
            o_ref[...] = jnp.zeros_like(o_ref)

        o_ref[...] += _dot(dg_ref[...], wg_ref[...], NT) + _dot(du_ref[...], wu_ref[...], NT)

    hid = pl.BlockSpec((None, tm, f), lambda i, j: (j, i, 0))
    wspec = pl.BlockSpec((None, d, f), lambda i, j: (j, 0, 0))
    return pl.pallas_call(
        body, name=name, grid=(t // tm, ns),
        in_specs=[hid, hid, wspec, wspec],
        out_specs=pl.BlockSpec((tm, d), lambda i, j: (i, 0)),
        out_shape=jax.ShapeDtypeStruct((t, d), F32), compiler_params=_params(),
    )(dg, du, wg, wu)


def _rope_tables(t):
    pos = jnp.arange(t, dtype=F32)
    inv_freq = ROPE_THETA ** (-jnp.arange(0, ROPE_DIM, 2, dtype=F32) / ROPE_DIM)
    ang = pos[:, None] * inv_freq[None, :]
    cos, sin = jnp.cos(ang), jnp.sin(ang)
    rest = HEAD_DIM - ROPE_DIM
    one = jnp.ones((t, rest), F32)
    zero_h = jnp.zeros((t, ROPE_HALF), F32)
    zero_r = jnp.zeros((t, rest), F32)
    c = jnp.concatenate([cos, cos, one], axis=1)
    s1 = jnp.concatenate([-sin, zero_h, zero_r], axis=1)
    s2 = jnp.concatenate([zero_h, sin, zero_r], axis=1)
    return c, s1, s2


def _rope(xh, c, s1, s2):
    return xh * c + pltpu.roll(xh, HEAD_DIM - ROPE_HALF, 1) * s1 + pltpu.roll(xh, ROPE_HALF, 1) * s2


def _rope_t(dh, c, s1, s2):
    return dh * c + pltpu.roll(dh * s1, ROPE_HALF, 1) + pltpu.roll(dh * s2, HEAD_DIM - ROPE_HALF, 1)


def _mixer_prep(proj, tables, bf_pad, hd, scale):
    t, np_ = proj.shape
    tr = _tile(t, 256, 16)
    nh = hd // HEAD_DIM
    nblk = hd // LANE
    f_blk = np_ // LANE - 1

    def body(qd_ref, kd_ref, vd_ref, qf_ref, kf_ref, vf_ref, fl_ref, c_ref, s1_ref, s2_ref, b_ref,
             oqd, okd, ovd, oqf, okf, ovf, olog):
        c, s1, s2 = c_ref[...], s1_ref[...], s2_ref[...]
        for h in range(nh):
            sl = slice(h * HEAD_DIM, (h + 1) * HEAD_DIM)
            oqd[:, sl] = (_rope(qd_ref[:, sl], c, s1, s2) * scale).astype(BF)
            okd[:, sl] = _rope(kd_ref[:, sl], c, s1, s2).astype(BF)
        ovd[...] = vd_ref[...].astype(BF)
        oqf[...] = (qf_ref[...] * scale).astype(BF)
        okf[...] = kf_ref[...].astype(BF)
        ovf[...] = vf_ref[...].astype(BF)
        z = fl_ref[...] + b_ref[...]
        olog[...] = jnp.minimum(z, 0.0) - jnp.log(1.0 + jnp.exp(-jnp.abs(z)))

    def col(kblk):
        return pl.BlockSpec((tr, hd), lambda i, kblk=kblk: (i, kblk))

    lane_row = pl.BlockSpec((tr, LANE), lambda i: (i, 0))
    in_specs = [col(0), col(1), col(2), col(3), col(4), col(5),
                pl.BlockSpec((tr, LANE), lambda i: (i, f_blk)),
                lane_row, lane_row, lane_row, pl.BlockSpec((1, LANE), lambda i: (0, 0))]
    o = pl.BlockSpec((tr, hd), lambda i: (i, 0))
    ob = jax.ShapeDtypeStruct((t, hd), BF)
    del nblk
    return pl.pallas_call(
        body, name="mixer_prep", grid=(t // tr,), in_specs=in_specs,
        out_specs=[o, o, o, o, o, o, lane_row],
        out_shape=[ob, ob, ob, ob, ob, ob, jax.ShapeDtypeStruct((t, LANE), F32)],
        compiler_params=_params(),
    )(proj, proj, proj, proj, proj, proj, proj, *tables, bf_pad)


def _split3(x):
    x1 = x.astype(BF)
    r1 = x - x1.astype(F32)
    x2 = r1.astype(BF)
    x3 = (r1 - x2.astype(F32)).astype(BF)
    return x1, x2, x3


def _cumsum_rows(x, reverse, name):
    t, w = x.shape
    blk = LANE
    nb = t // blk

    def body(x_ref, o_ref):
        r = lax.broadcasted_iota(jnp.int32, (blk, blk), 0)
        c = lax.broadcasted_iota(jnp.int32, (blk, blk), 1)
        tri = jnp.where((c >= r) if reverse else (c <= r), 1.0, 0.0).astype(BF)

        def step(i, carry):
            b = (nb - 1 - i) if reverse else i
            off = pl.multiple_of(b * blk, blk)
            xb = x_ref[pl.ds(off, blk), :]
            x1, x2, x3 = _split3(xb)
            o_ref[pl.ds(off, blk), :] = _dot(tri, x1) + _dot(tri, x2) + _dot(tri, x3) + carry
            return carry + jnp.sum(xb, axis=0, keepdims=True)

        lax.fori_loop(0, nb, step, jnp.zeros((1, w), F32))

    return pl.pallas_call(body, name=name, out_shape=jax.ShapeDtypeStruct((t, w), F32),
                          compiler_params=_params())(x)


def _dil_mult(delta, m4, m16):
    (w0, _), (w1, _), (w2, _) = DIL_PATTERNS
    ok = delta >= 0
    mult = (jnp.where(ok & (delta <= w0), 1.0, 0.0) + jnp.where(ok & m4 & (delta <= w1), 1.0, 0.0)
            + jnp.where(ok & m16 & (delta <= w2), 1.0, 0.0))
    return mult


def _scores(mode, s, delta, m4, m16, bias):
    if mode == "fox":
        return jnp.where(delta >= 0, s + bias, NEG), None
    mult = _dil_mult(delta, m4, m16)
    return jnp.where(mult > 0.0, s, NEG), mult


def _attn_fwd(mode, q, k, v, c_col, c_row, tq, name):
    t, hd = q.shape
    nh = hd // HEAD_DIM
    nb = t // tq
    wb = MAX_WINDOW // tq
    fox = mode == "fox"

    def body(*refs):
        if fox:
            q_ref, k_ref, v_ref, cc_ref, cr_ref, o_ref, lse_ref = refs
        else:
            q_ref, k_ref, v_ref, o_ref, lse_ref = refs
        qi = pl.program_id(1)
        qb = q_ref[...]
        ij = lax.broadcasted_iota(jnp.int32, (tq, tq), 0) - lax.broadcasted_iota(jnp.int32, (tq, tq), 1)
        m4, m16 = (ij & 3) == 0, (ij & 15) == 0

        def step(kj, carry):
            m, l, acc = carry
            off = pl.multiple_of(kj * tq, tq)
            kb = k_ref[pl.ds(off, tq), :]
            vb = v_ref[pl.ds(off, tq), :]
            s = _dot(qb, kb, NT)
            bias = (cc_ref[...] - cr_ref[kj]) if fox else None
            s, mult = _scores(mode, s, ij + (qi - kj) * tq, m4, m16, bias)
            m_new = jnp.maximum(m, jnp.max(s, axis=1, keepdims=True))
            p = jnp.exp(s - m_new)
            if mult is not None:
                p = p * mult
            alpha = jnp.exp(m - m_new)
            l = alpha * l + jnp.sum(p, axis=1, keepdims=True)
            acc = alpha * acc + _dot(p.astype(BF), vb)
            return m_new, l, acc

        lo = 0 if fox else jnp.maximum(qi - wb, 0)
        init = (jnp.full((tq, 1), NEG, F32), jnp.zeros((tq, 1), F32), jnp.zeros((tq, HEAD_DIM), F32))
        m, l, acc = lax.fori_loop(lo, qi + 1, step, init)
        o_ref[...] = (acc / l).astype(BF)
        lse_ref[...] = m + jnp.log(l)

    qspec = pl.BlockSpec((tq, HEAD_DIM), lambda h, i: (i, h))
    kvspec = pl.BlockSpec((t, HEAD_DIM), lambda h, i: (0, h))
    colspec = pl.BlockSpec((None, tq, 1), lambda h, i: (h, i, 0))
    in_specs = [qspec, kvspec, kvspec]
    args = [q, k, v]
    if fox:
        in_specs += [colspec, pl.BlockSpec((None, nb, 1, tq), lambda h, i: (h, 0, 0, 0))]
        args += [c_col, c_row]
    return pl.pallas_call(
        body, name=name, grid=(nh, nb), in_specs=in_specs,
        out_specs=[qspec, colspec],
        out_shape=[jax.ShapeDtypeStruct((t, hd), BF), jax.ShapeDtypeStruct((nh, t, 1), F32)],
        compiler_params=_params(),
    )(*args)


def _attn_bwd_dq(mode, q, k, v, o, do, lse, c_col, c_row, tq, name):
    t, hd = q.shape
    nh = hd // HEAD_DIM
    nb = t // tq
    wb = MAX_WINDOW // tq
    fox = mode == "fox"

    def body(*refs):
        if fox:
            q_ref, k_ref, v_ref, o_ref, do_ref, lse_ref, cc_ref, cr_ref, dq_ref, dl_ref = refs
        else:
            q_ref, k_ref, v_ref, o_ref, do_ref, lse_ref, dq_ref, dl_ref = refs
        qi = pl.program_id(1)
        qb = q_ref[...]
        dob = do_ref[...]
        lse = lse_ref[...]
        ij = lax.broadcasted_iota(jnp.int32, (tq, tq), 0) - lax.broadcasted_iota(jnp.int32, (tq, tq), 1)
        m4, m16 = (ij & 3) == 0, (ij & 15) == 0
        zero = jnp.zeros((tq, HEAD_DIM), F32)

        def probs(kj):
            off = pl.multiple_of(kj * tq, tq)
            kb = k_ref[pl.ds(off, tq), :]
            vb = v_ref[pl.ds(off, tq), :]
            s = _dot(qb, kb, NT)
            bias = (cc_ref[...] - cr_ref[kj]) if fox else None
            s, mult = _scores(mode, s, ij + (qi - kj) * tq, m4, m16, bias)
            p = jnp.exp(s - lse)
            if mult is not None:
                p = p * mult
            return p, _dot(dob, vb, NT), kb

        if fox:
            def step(kj, carry):
                a, b, dl = carry
                p, dp, kb = probs(kj)
                pdp = p * dp
                return (a + _dot(pdp.astype(BF), kb), b + _dot(p.astype(BF), kb),
                        dl + jnp.sum(pdp, axis=1, keepdims=True))

            a, b, dl = lax.fori_loop(0, qi + 1, step, (zero, zero, jnp.zeros((tq, 1), F32)))
            dl_ref[...] = dl
            dq_ref[...] = a - dl * b
        else:
            dl = jnp.sum(o_ref[...].astype(F32) * dob.astype(F32), axis=1, keepdims=True)
            dl_ref[...] = dl

            def step(kj, dq):
                p, dp, kb = probs(kj)
                return dq + _dot((p * (dp - dl)).astype(BF), kb)

            dq_ref[...] = lax.fori_loop(jnp.maximum(qi - wb, 0), qi + 1, step, zero)

    qspec = pl.BlockSpec((tq, HEAD_DIM), lambda h, i: (i, h))
    kvspec = pl.BlockSpec((t, HEAD_DIM), lambda h, i: (0, h))
    colspec = pl.BlockSpec((None, tq, 1), lambda h, i: (h, i, 0))
    in_specs = [qspec, kvspec, kvspec, qspec, qspec, colspec]
    args = [q, k, v, o, do, lse]
    if fox:
        in_specs += [colspec, pl.BlockSpec((None, nb, 1, tq), lambda h, i: (h, 0, 0, 0))]
        args += [c_col, c_row]
    return pl.pallas_call(
        body, name=name, grid=(nh, nb), in_specs=in_specs,
        out_specs=[qspec, colspec],
        out_shape=[jax.ShapeDtypeStruct((t, hd), F32), jax.ShapeDtypeStruct((nh, t, 1), F32)],
        compiler_params=_params(),
    )(*args)


def _attn_bwd_dkv(mode, q, k, v, do, lse_row, dl_row, c_col, c_row, tq, name):
    t, hd = q.shape
    nh = hd // HEAD_DIM
    nb = t // tq
    wb = MAX_WINDOW // tq
    fox = mode == "fox"

    def body(*refs):
        if fox:
            q_ref, k_ref, v_ref, do_ref, lse_ref, dl_ref, cc_ref, cr_ref, dk_ref, dv_ref, dc_ref = refs
        else:
            q_ref, k_ref, v_ref, do_ref, lse_ref, dl_ref, dk_ref, dv_ref = refs
        kj = pl.program_id(1)
        kb = k_ref[...]
        vb = v_ref[...]
        ji = lax.broadcasted_iota(jnp.int32, (tq, tq), 1) - lax.broadcasted_iota(jnp.int32, (tq, tq), 0)
        m4, m16 = (ji & 3) == 0, (ji & 15) == 0

        def step(qi, carry):
            dk, dv, dc = carry
            off = pl.multiple_of(qi * tq, tq)
            qb = q_ref[pl.ds(off, tq), :]
            dob = do_ref[pl.ds(off, tq), :]
            st = _dot(kb, qb, NT)
            bias = (cr_ref[qi] - cc_ref[...]) if fox else None
            st, mult = _scores(mode, st, ji + (qi - kj) * tq, m4, m16, bias)
            pt = jnp.exp(st - lse_ref[qi])
            if mult is not None:
                pt = pt * mult
            dst = pt * (_dot(vb, dob, NT) - dl_ref[qi])
            dv = dv + _dot(pt.astype(BF), dob)
            dk = dk + _dot(dst.astype(BF), qb)
            if fox:
                dc = dc - jnp.sum(dst, axis=1, keepdims=True)
            return dk, dv, dc

        hi = nb if fox else jnp.minimum(kj + wb + 1, nb)
        zero = jnp.zeros((tq, HEAD_DIM), F32)
        dk, dv, dc = lax.fori_loop(kj, hi, step, (zero, zero, jnp.zeros((tq, 1), F32)))
        dk_ref[...] = dk
        dv_ref[...] = dv
        if fox:
            dc_ref[...] = dc

    blkspec = pl.BlockSpec((tq, HEAD_DIM), lambda h, j: (j, h))
    fullspec = pl.BlockSpec((t, HEAD_DIM), lambda h, j: (0, h))
    rowspec = pl.BlockSpec((None, nb, 1, tq), lambda h, j: (h, 0, 0, 0))
    colspec = pl.BlockSpec((None, tq, 1), lambda h, j: (h, j, 0))
    in_specs = [fullspec, blkspec, blkspec, fullspec, rowspec, rowspec]
    args = [q, k, v, do, lse_row, dl_row]
    out_specs = [blkspec, blkspec]
    out_shape = [jax.ShapeDtypeStruct((t, hd), F32), jax.ShapeDtypeStruct((t, hd), F32)]
    if fox:
        in_specs += [colspec, rowspec]
        args += [c_col, c_row]
        out_specs.append(colspec)
        out_shape.append(jax.ShapeDtypeStruct((nh, t, 1), F32))
    return pl.pallas_call(
        body, name=name, grid=(nh, nb), in_specs=in_specs, out_specs=out_specs, out_shape=out_shape,
        compiler_params=_params(),
    )(*args)


def _gate_specs(t, d, hd, tr):
    row = pl.BlockSpec((tr, d), lambda i: (i, 0))
    vec = pl.BlockSpec((1, d), lambda i: (0, 0))
    base = 6 * hd // d
    gd = pl.BlockSpec((tr, d), lambda i: (i, base))
    gf = pl.BlockSpec((tr, d), lambda i: (i, base + 1))
    return row, vec, gd, gf


def _merge_fwd(pd, pf, proj, b_d, b_f, hd):
    t, d = pd.shape
    tr = _tile(t, 256, 16)
    row, vec, gd, gf = _gate_specs(t, d, hd, tr)

    def body(pd_ref, pf_ref, gd_ref, gf_ref, bd_ref, bf_ref, o_ref):
        o_ref[...] = (_sig(gd_ref[...] + bd_ref[...]) * pd_ref[...]
                      + _sig(gf_ref[...] + bf_ref[...]) * pf_ref[...]).astype(BF)

    return pl.pallas_call(
        body, name="merge_fwd", grid=(t // tr,), in_specs=[row, row, gd, gf, vec, vec],
        out_specs=row, out_shape=jax.ShapeDtypeStruct((t, d), BF), compiler_params=_params(),
    )(pd, pf, proj, proj, b_d, b_f)


def _merge_bwd(dm, pd, pf, proj, b_d, b_f, hd):
    t, d = pd.shape
    tr = _tile(t, 256, 16)
    row, vec, gd, gf = _gate_specs(t, d, hd, tr)

    def body(dm_ref, pd_ref, pf_ref, gd_ref, gf_ref, bd_ref, bf_ref,
             dpd_ref, dpf_ref, dgd_ref, dgf_ref, dbd_ref, dbf_ref):
        dmv = dm_ref[...]
        sd = _sig(gd_ref[...] + bd_ref[...])
        sf = _sig(gf_ref[...] + bf_ref[...])
        dgd = dmv * pd_ref[...] * (sd * (1.0 - sd))
        dgf = dmv * pf_ref[...] * (sf * (1.0 - sf))
        dpd_ref[...] = (dmv * sd).astype(BF)
        dpf_ref[...] = (dmv * sf).astype(BF)
        dgd_ref[...] = dgd.astype(BF)
        dgf_ref[...] = dgf.astype(BF)

        @pl.when(pl.program_id(0) == 0)
        def _():
            dbd_ref[...] = jnp.zeros_like(dbd_ref)
            dbf_ref[...] = jnp.zeros_like(dbf_ref)

        dbd_ref[...] += jnp.sum(dgd, axis=0, keepdims=True)
        dbf_ref[...] += jnp.sum(dgf, axis=0, keepdims=True)

    ob = jax.ShapeDtypeStruct((t, d), BF)
    ov = jax.ShapeDtypeStruct((1, d), F32)
    return pl.pallas_call(
        body, name="merge_bwd", grid=(t // tr,), in_specs=[row, row, row, gd, gf, vec, vec],
        out_specs=[row, row, row, row, vec, vec], out_shape=[ob, ob, ob, ob, ov, ov],
        compiler_params=_params(),
    )(dm, pd, pf, proj, proj, b_d, b_f)


def _assemble_dproj(dqd, dkd, dvd, dqf, dkf, dvf, dgd, dgf, dlogf, proj, tables, bf_pad, scale):
    t, np_ = proj.shape
    hd = dqd.shape[1]
    d = dgd.shape[1]
    nh = hd // HEAD_DIM
    tr = _tile(t, 256, 16)
    f_blk = np_ // LANE - 1

    def body(dqd_ref, dkd_ref, dvd_ref, dqf_ref, dkf_ref, dvf_ref, dgd_ref, dgf_ref, dlog_ref, fl_ref,
             c_ref, s1_ref, s2_ref, b_ref, o_ref, db_ref):
        c, s1, s2 = c_ref[...], s1_ref[...], s2_ref[...]
        for h in range(nh):
            sl = slice(h * HEAD_DIM, (h + 1) * HEAD_DIM)
            o_ref[:, sl] = (_rope_t(dqd_ref[:, sl], c, s1, s2) * scale).astype(BF)
            o_ref[:, hd + h * HEAD_DIM:hd + (h + 1) * HEAD_DIM] = _rope_t(dkd_ref[:, sl], c, s1, s2).astype(BF)
        o_ref[:, 2 * hd:3 * hd] = dvd_ref[...].astype(BF)
        o_ref[:, 3 * hd:4 * hd] = (dqf_ref[...] * scale).astype(BF)
        o_ref[:, 4 * hd:5 * hd] = dkf_ref[...].astype(BF)
        o_ref[:, 5 * hd:6 * hd] = dvf_ref[...].astype(BF)
        o_ref[:, 6 * hd:6 * hd + d] = dgd_ref[...]
        o_ref[:, 6 * hd + d:6 * hd + 2 * d] = dgf_ref[...]
        z = fl_ref[...] + b_ref[...]
        dfl = dlog_ref[...] * _sig(-z)
        o_ref[:, 6 * hd + 2 * d:] = dfl.astype(BF)

        @pl.when(pl.program_id(0) == 0)
        def _():
            db_ref[...] = jnp.zeros_like(db_ref)

        db_ref[...] += jnp.sum(dfl, axis=0, keepdims=True)

    head = pl.BlockSpec((tr, hd), lambda i: (i, 0))
    row = pl.BlockSpec((tr, d), lambda i: (i, 0))
    lane_row = pl.BlockSpec((tr, LANE), lambda i: (i, 0))
    lane_vec = pl.BlockSpec((1, LANE), lambda i: (0, 0))
    return pl.pallas_call(
        body, name="assemble_dproj", grid=(t // tr,),
        in_specs=[head] * 6 + [row, row, lane_row, pl.BlockSpec((tr, LANE), lambda i: (i, f_blk)),
                               lane_row, lane_row, lane_row, lane_vec],
        out_specs=[pl.BlockSpec((tr, np_), lambda i: (i, 0)), lane_vec],
        out_shape=[jax.ShapeDtypeStruct((t, np_), BF), jax.ShapeDtypeStruct((1, LANE), F32)],
        compiler_params=_params(),
    )(dqd, dkd, dvd, dqf, dkf, dvf, dgd, dgf, dlogf, proj, *tables, bf_pad)


def _to_rows(a, tq):
    h, t, _ = a.shape
    return a.reshape(h, t // tq, 1, tq)


def kernel(x, ffn1_norm, ffn1_w_gate, ffn1_w_up, ffn1_w_down, mix_norm, w_in, b_forget, b_gate_dil, b_gate_fox, w_proj_dil, w_proj_fox, w_out, ffn2_norm, ffn2_w_gate, ffn2_w_up, ffn2_w_down, final_norm, loss_target, m_ffn1_norm, m_ffn1_w_gate, m_ffn1_w_up, m_ffn1_w_down, m_mix_norm, m_w_in, m_b_forget, m_b_gate_dil, m_b_gate_fox, m_w_proj_dil, m_w_proj_fox, m_w_out, m_ffn2_norm, m_ffn2_w_gate, m_ffn2_w_up, m_ffn2_w_down, m_final_norm, v_ffn1_norm, v_ffn1_w_gate, v_ffn1_w_up, v_ffn1_w_down, v_mix_norm, v_w_in, v_b_forget, v_b_gate_dil, v_b_gate_fox, v_w_proj_dil, v_w_proj_fox, v_w_out, v_ffn2_norm, v_ffn2_w_gate, v_ffn2_w_up, v_ffn2_w_down, v_final_norm):
    t, d = x.shape[1], x.shape[2]
    hd = w_proj_dil.shape[1]
    nh = hd // HEAD_DIM
    n_f = b_forget.shape[1]
    cols = w_in.shape[2]
    in_cols = N_DEV * cols
    assert in_cols == 6 * hd + n_f + 2 * d and n_f == nh and n_f <= LANE
    np_ = 6 * hd + 2 * d + LANE
    scale = HEAD_DIM ** -0.5
    tq = _tile(t, 512, LANE)
    assert MAX_WINDOW % tq == 0 and tq % 16 == 0

    x2d = x[0]
    tgt = loss_target[0]

    def gather(w, name):
        return _exchange(w[0].astype(BF), True, name)

    wg1, wu1, wd1 = gather(ffn1_w_gate, "ag_ffn1_gate"), gather(ffn1_w_up, "ag_ffn1_up"), gather(ffn1_w_down, "ag_ffn1_down")
    wg2, wu2, wd2 = gather(ffn2_w_gate, "ag_ffn2_gate"), gather(ffn2_w_up, "ag_ffn2_up"), gather(ffn2_w_down, "ag_ffn2_down")
    win_g = gather(w_in, "ag_w_in")
    wpd = gather(w_proj_dil, "ag_proj_dil").transpose(1, 0, 2).reshape(hd, d)
    wpf = gather(w_proj_fox, "ag_proj_fox").transpose(1, 0, 2).reshape(hd, d)
    wout = gather(w_out, "ag_w_out").reshape(d, d)
    win_full = win_g.transpose(1, 0, 2).reshape(d, in_cols)
    win_p = jnp.concatenate([win_full[:, :6 * hd], win_full[:, 6 * hd + n_f:], win_full[:, 6 * hd:6 * hd + n_f],
                             jnp.zeros((d, LANE - n_f), BF)], axis=1)

    tables = _rope_tables(t)
    bf_pad = jnp.pad(b_forget, ((0, 0), (0, LANE - n_f)))

    hn1 = _rms_fwd(x2d, ffn1_norm, "rms_ffn1")
    g1, u1, a1 = _ffn_gate_up(hn1, wg1, wu1, "ffn1_gate_up")
    x1 = _ffn_down(a1, wd1, x2d, "ffn1_down")

    hm = _rms_fwd(x1, mix_norm, "rms_mix")
    proj = _mm_nn(hm, win_p, F32, "w_in_fwd")
    qd, kd, vd, qf, kf, vf, logf = _mixer_prep(proj, tables, bf_pad, hd, scale)
    csum = _cumsum_rows(logf, False, "cumsum_logf")
    c_col = csum[:, :nh].T.reshape(nh, t, 1)
    c_row = _to_rows(c_col, tq)
    yd, lse_d = _attn_fwd("dil", qd, kd, vd, None, None, tq, "attn_dil_fwd")
    yf, lse_f = _attn_fwd("fox", qf, kf, vf, c_col, c_row, tq, "attn_fox_fwd")
    pd = _mm_nn(yd, wpd, F32, "proj_dil_fwd", tn_pref=1024)
    pf = _mm_nn(yf, wpf, F32, "proj_fox_fwd", tn_pref=1024)
    merged = _merge_fwd(pd, pf, proj, b_gate_dil, b_gate_fox, hd)
    x2 = _mm_nn(merged, wout, F32, "w_out_fwd", residual=x1, tn_pref=1024)

    hn2 = _rms_fwd(x2, ffn2_norm, "rms_ffn2")
    g2, u2, a2 = _ffn_gate_up(hn2, wg2, wu2, "ffn2_gate_up")
    x3 = _ffn_down(a2, wd2, x2, "ffn2_down")

    dx3, dx3b, d_final, loss_lanes = _loss_head(x3, final_norm.reshape(1, d), tgt)

    def ffn_bwd(dxb, hn, g, u, a, wg, wu, wd, tag):
        dg, du = _ffn_bwd_hidden(dxb, wd, g, u, tag + "_bwd_hidden")
        dwd = _ffn_dw_down(a, dxb, tag + "_dw_down")
        dwg, dwu = _ffn_dw_gate_up(hn, dg, du, tag + "_dw_gate_up")
        dhn = _ffn_bwd_input(dg, du, wg, wu, tag + "_bwd_input")
        return dhn, dwg, dwu, dwd

    dhn2, dwg2, dwu2, dwd2 = ffn_bwd(dx3b, hn2, g2, u2, a2, wg2, wu2, wd2, "ffn2")
    dx2, dx2b, d_ffn2_norm = _rms_bwd(dhn2, x2, ffn2_norm, dx3, "rms_ffn2_bwd")

    dmerged = _mm_nt(dx2b, wout, F32, "w_out_bwd")
    dwout = _mm_tn(merged, dx2b, BF, "w_out_dw", tn_pref=1024)
    dpd, dpf, dgd, dgf, d_bd, d_bf = _merge_bwd(dmerged, pd, pf, proj, b_gate_dil, b_gate_fox, hd)
    dyd = _mm_nt(dpd, wpd, BF, "proj_dil_bwd")
    dyf = _mm_nt(dpf, wpf, BF, "proj_fox_bwd")
    dwpd = _mm_tn(yd, dpd, BF, "proj_dil_dw", tn_pref=1024)
    dwpf = _mm_tn(yf, dpf, BF, "proj_fox_dw", tn_pref=1024)

    dqd, dl_d = _attn_bwd_dq("dil", qd, kd, vd, yd, dyd, lse_d, None, None, tq, "attn_dil_dq")
    dkd, dvd = _attn_bwd_dkv("dil", qd, kd, vd, dyd, _to_rows(lse_d, tq), _to_rows(dl_d, tq), None, None, tq,
                             "attn_dil_dkv")
    dqf, dl_f = _attn_bwd_dq("fox", qf, kf, vf, yf, dyf, lse_f, c_col, c_row, tq, "attn_fox_dq")
    dkf, dvf, dc = _attn_bwd_dkv("fox", qf, kf, vf, dyf, _to_rows(lse_f, tq), _to_rows(dl_f, tq), c_col, c_row, tq,
                                 "attn_fox_dkv")
    dc_pad = jnp.pad(dc.reshape(nh, t).T, ((0, 0), (0, LANE - nh)))
    dlogf = _cumsum_rows(dc_pad, True, "revcumsum_dc")
    dproj, d_bforget = _assemble_dproj(dqd, dkd, dvd, dqf, dkf, dvf, dgd, dgf, dlogf, proj, tables, bf_pad, scale)

    dhm = _mm_nt(dproj, win_p, F32, "w_in_bwd", tn_pref=2048, tk_pref=1152)
    dwin_p = _mm_tn(hm, dproj, BF, "w_in_dw")
    dx1, dx1b, d_mix_norm = _rms_bwd(dhm, x1, mix_norm, dx2, "rms_mix_bwd")

    dhn1, dwg1, dwu1, dwd1 = ffn_bwd(dx1b, hn1, g1, u1, a1, wg1, wu1, wd1, "ffn1")
    grad_x, _, d_ffn1_norm = _rms_bwd(dhn1, x2d, ffn1_norm, dx1, "rms_ffn1_bwd")

    dwin_full = jnp.concatenate([dwin_p[:, :6 * hd], dwin_p[:, 6 * hd + 2 * d:6 * hd + 2 * d + n_f],
                                 dwin_p[:, 6 * hd:6 * hd + 2 * d]], axis=1)
    dwin_c = dwin_full.reshape(d, N_DEV, cols).transpose(1, 0, 2)
    dwpd_c = dwpd.reshape(hd, N_DEV, d // N_DEV).transpose(1, 0, 2)
    dwpf_c = dwpf.reshape(hd, N_DEV, d // N_DEV).transpose(1, 0, 2)
    dwout_c = dwout.reshape(N_DEV, d // N_DEV, d)

    def update(parts, w, m, v, tag):
        recv = _exchange(parts, False, "rs_" + tag)
        g, delta, m2, v2 = _adam_from_partials(recv, w[0], m[0], v[0], "adam_" + tag)
        return g[None], delta[None], m2[None], v2[None]

    big = {
        "ffn1_w_gate": update(dwg1, ffn1_w_gate, m_ffn1_w_gate, v_ffn1_w_gate, "ffn1_gate"),
        "ffn1_w_up": update(dwu1, ffn1_w_up, m_ffn1_w_up, v_ffn1_w_up, "ffn1_up"),
        "ffn1_w_down": update(dwd1, ffn1_w_down, m_ffn1_w_down, v_ffn1_w_down, "ffn1_down"),
        "w_in": update(dwin_c, w_in, m_w_in, v_w_in, "w_in"),
        "w_proj_dil": update(dwpd_c, w_proj_dil, m_w_proj_dil, v_w_proj_dil, "proj_dil"),
        "w_proj_fox": update(dwpf_c, w_proj_fox, m_w_proj_fox, v_w_proj_fox, "proj_fox"),
        "w_out": update(dwout_c, w_out, m_w_out, v_w_out, "w_out"),
        "ffn2_w_gate": update(dwg2, ffn2_w_gate, m_ffn2_w_gate, v_ffn2_w_gate, "ffn2_gate"),
        "ffn2_w_up": update(dwu2, ffn2_w_up, m_ffn2_w_up, v_ffn2_w_up, "ffn2_up"),
        "ffn2_w_down": update(dwd2, ffn2_w_down, m_ffn2_w_down, v_ffn2_w_down, "ffn2_down"),
    }

    def lanes(a):
        a = a.reshape(1, -1)
        return jnp.pad(a, ((0, 0), (0, d - a.shape[1])))

    small_names = ["ffn1_norm", "mix_norm", "b_gate_dil", "b_gate_fox", "ffn2_norm", "final_norm", "b_forget"]
    small_g = [d_ffn1_norm, d_mix_norm, d_bd, d_bf, d_ffn2_norm, d_final, d_bforget[:, :n_f]]
    small_w = [ffn1_norm, mix_norm, b_gate_dil, b_gate_fox, ffn2_norm, final_norm, b_forget]
    small_m = [m_ffn1_norm, m_mix_norm, m_b_gate_dil, m_b_gate_fox, m_ffn2_norm, m_final_norm, m_b_forget]
    small_v = [v_ffn1_norm, v_mix_norm, v_b_gate_dil, v_b_gate_fox, v_ffn2_norm, v_final_norm, v_b_forget]
    pack = lambda arrs, last: jnp.concatenate([lanes(a) for a in arrs] + [last], axis=0)
    g_all = _allreduce_small(pack(small_g, loss_lanes))
    zero_row = jnp.zeros((1, d), F32)
    one_row = jnp.ones((1, d), F32)
    s_delta, s_m, s_v = _adam_small(g_all, pack(small_w, zero_row), pack(small_m, zero_row), pack(small_v, one_row))
    loss = g_all[len(small_names), 0]

    def unpack(packed, i, like):
        return packed[i, :like.size].reshape(like.shape)

    small = {}
    for i, (n, w) in enumerate(zip(small_names, small_w)):
        small[n] = (unpack(g_all, i, w), unpack(s_delta, i, w), unpack(s_m, i, w), unpack(s_v, i, w))

    order = ["ffn1_norm", "ffn1_w_gate", "ffn1_w_up", "ffn1_w_down", "mix_norm", "w_in", "b_forget", "b_gate_dil",
             "b_gate_fox", "w_proj_dil", "w_proj_fox", "w_out", "ffn2_norm", "ffn2_w_gate", "ffn2_w_up",
             "ffn2_w_down", "final_norm"]
    res = {**big, **small}
    outs = [loss, grad_x[None]]
    for slot in range(4):
        outs += [res[n][slot] for n in order]
    return tuple(outs)
```

```python
import functools

import numpy as np
import jax
import jax.numpy as jnp
from jax import lax
from jax.experimental import pallas as pl
from jax.experimental.pallas import tpu as pltpu

BF = jnp.bfloat16
F32 = jnp.float32
MESH = pl.DeviceIdType.MESH
N_DEV = 8

HEAD_DIM = 128
ROPE_DIM = HEAD_DIM // 4
ROPE_HALF = ROPE_DIM // 2
ROPE_THETA = 500000.0
NORM_EPS = 1e-6
DIL_PATTERNS = ((128, 1), (512, 4), (2048, 16))
MAX_WINDOW = 2048
LANE = 128
NEG = -1e30

ADAM_LR = 0.001
ADAM_B1 = 0.9
ADAM_B2 = 0.999
ADAM_EPS = 1e-08
ADAM_WD = 0.01
ADAM_STEP = 10

VMEM_LIMIT_BYTES = 56 * 1024 * 1024
ANY = pl.BlockSpec(memory_space=pl.ANY)

NN = (((1,), (0,)), ((), ()))
NT = (((1,), (1,)), ((), ()))
TN = (((0,), (0,)), ((), ()))


def _dot(a, b, dn=NN):
    return lax.dot_general(a, b, dn, preferred_element_type=F32)


def _sig(x):
    return 1.0 / (1.0 + jnp.exp(-x))


def _tile(n, pref, align):
    best = None
    t = align
    while t <= min(n, pref):
        if n % t == 0:
            best = t
        t += align
    return n if best is None else best


def _params():
    return pltpu.CompilerParams(vmem_limit_bytes=VMEM_LIMIT_BYTES)


def _call(body, args, dep=None, **kw):
    if dep is not None:
        n_in = len(args)
        inner = body

        def body(*refs):
            inner(*refs[:n_in], *refs[n_in + 1:])

        kw["in_specs"] = list(kw["in_specs"]) + [ANY]
        args = list(args) + [dep]
    return pl.pallas_call(body, **kw)(*args)


def _peers():
    x, y, c = lax.axis_index("x"), lax.axis_index("y"), lax.axis_index("c")
    me = 4 * x + 2 * y + c
    peers = []
    for k in range(1, N_DEV):
        px = 1 - x if (k >> 2) & 1 else x
        py = 1 - y if (k >> 1) & 1 else y
        pc = 1 - c if k & 1 else c
        peers.append((k, (px, py, pc), 4 * px + 2 * py + pc))
    return me, peers


HBM = pl.BlockSpec(memory_space=pltpu.HBM)
SEM = pl.BlockSpec(memory_space=pltpu.SEMAPHORE)
EFFECT = pltpu.SideEffectType.DATAFLOW_SIDE_EFFECTING


def _exchange_copy(gather, src_ref, land_ref, send_sems, recv_sems, me, k, peer, peer_flat, landing):
    return pltpu.make_async_remote_copy(
        src_ref=src_ref if gather else src_ref.at[peer_flat], dst_ref=land_ref.at[landing],
        send_sem=send_sems.at[k], recv_sem=recv_sems.at[k], device_id=peer, device_id_type=MESH)


def _exchange_start(srcs, gather, name):
    n = len(srcs)

    def body(*refs):
        src_refs, land_refs = refs[:n], refs[n:2 * n]
        send_refs, recv_refs = refs[2 * n:3 * n], refs[3 * n:4 * n]
        token, local_sems = refs[6 * n], refs[6 * n + 1]
        me, peers = _peers()
        local = [pltpu.make_async_copy(src_refs[i] if gather else src_refs[i].at[me], land_refs[i].at[me],
                                       local_sems.at[i]) for i in range(n)]
        for cp in local:
            cp.start()
        for i in range(n):
            for k, peer, peer_flat in peers:
                _exchange_copy(gather, src_refs[i], land_refs[i], send_refs[i], recv_refs[i],
                               me, k, peer, peer_flat, me).start()
        for cp in local:
            cp.wait()
        token[...] = jnp.zeros_like(token)

    lands = [lax.empty((N_DEV,) + s.shape[-2:], s.dtype) for s in srcs]
    sems = [pltpu.SemaphoreType.DMA((N_DEV,)) for _ in range(2 * n)]
    out = pl.pallas_call(
        body, name=name,
        out_shape=tuple(sems) + tuple(pltpu.HBM(a.shape, a.dtype) for a in list(srcs) + lands)
        + (jax.ShapeDtypeStruct((8, LANE), F32),),
        in_specs=[HBM] * (2 * n),
        out_specs=tuple([SEM] * (2 * n) + [HBM] * (2 * n) + [pl.BlockSpec(memory_space=pltpu.VMEM)]),
        input_output_aliases={i: 2 * n + i for i in range(2 * n)},
        scratch_shapes=[pltpu.SemaphoreType.DMA((n,))],
        compiler_params=pltpu.CompilerParams(has_side_effects=EFFECT),
    )(*[pltpu.with_memory_space_constraint(a, pltpu.HBM) for a in list(srcs) + lands])
    handles = [(out[2 * n + i], out[3 * n + i], out[i], out[n + i]) for i in range(n)]
    return handles, out[4 * n]


def _exchange_wait(handles, gather, after, name):
    n = len(handles)

    def body(*refs):
        src_refs, land_refs = refs[:n], refs[n:2 * n]
        send_refs, recv_refs = refs[2 * n:3 * n], refs[3 * n:4 * n]
        me, peers = _peers()
        for i in range(n):
            for k, peer, peer_flat in peers:
                cp = _exchange_copy(gather, src_refs[i], land_refs[i], send_refs[i], recv_refs[i],
                                    me, k, peer, peer_flat, peer_flat)
                cp.wait_send()
                cp.wait_recv()

    srcs = [h[0] for h in handles]
    lands = [h[1] for h in handles]
    out = pl.pallas_call(
        body, name=name,
        out_shape=tuple(pltpu.HBM(a.shape, a.dtype) for a in srcs + lands),
        in_specs=[HBM] * (2 * n) + [SEM] * (2 * n) + [ANY],
        out_specs=tuple([HBM] * (2 * n)),
        input_output_aliases={i: i for i in range(2 * n)},
        compiler_params=pltpu.CompilerParams(has_side_effects=EFFECT),
    )(*srcs, *lands, *[h[2] for h in handles], *[h[3] for h in handles], after)
    return list(out[n:])


def _allreduce_small(p):
    rows, d = p.shape

    def body(p_ref, o_ref, recv_ref, send_sems, recv_sems):
        me, peers = _peers()
        recv_ref[me] = p_ref[...]
        sends = []
        for k, peer, peer_flat in peers:
            cp = pltpu.make_async_remote_copy(
                src_ref=p_ref, dst_ref=recv_ref.at[me],
                send_sem=send_sems.at[k], recv_sem=recv_sems.at[k],
                device_id=peer, device_id_type=MESH)
            cp.start()
            sends.append(cp)
        for k, peer, peer_flat in peers:
            pltpu.make_async_remote_copy(
                src_ref=p_ref, dst_ref=recv_ref.at[peer_flat],
                send_sem=send_sems.at[k], recv_sem=recv_sems.at[k],
                device_id=peer, device_id_type=MESH).wait_recv()
        for cp in sends:
            cp.wait_send()
        acc = recv_ref[0]
        for s in range(1, N_DEV):
            acc = acc + recv_ref[s]
        is_loss = lax.broadcasted_iota(jnp.int32, (rows, d), 0) == rows - 1
        total = jnp.sum(jnp.where(is_loss, acc, 0.0))
        o_ref[...] = jnp.where(is_loss, total, acc)

    return pl.pallas_call(
        body, name="allreduce_small",
        out_shape=jax.ShapeDtypeStruct((rows, d), F32),
        in_specs=[pl.BlockSpec(memory_space=pltpu.VMEM)],
        out_specs=pl.BlockSpec(memory_space=pltpu.VMEM),
        scratch_shapes=[pltpu.VMEM((N_DEV, rows, d), F32),
                        pltpu.SemaphoreType.DMA((N_DEV,)), pltpu.SemaphoreType.DMA((N_DEV,))],
    )(p)


def _adam_math(w, g, m, v):
    m2 = ADAM_B1 * m + (1.0 - ADAM_B1) * g
    v2 = ADAM_B2 * v + (1.0 - ADAM_B2) * (g * g)
    m_hat = m2 / (1.0 - ADAM_B1 ** ADAM_STEP)
    v_hat = v2 / (1.0 - ADAM_B2 ** ADAM_STEP)
    delta = -ADAM_LR * (m_hat / (jnp.sqrt(v_hat) + ADAM_EPS) + ADAM_WD * w)
    return delta, m2, v2


def _adam_from_partials(parts, w, m, v, name):
    r, c = w.shape
    tr = _tile(r, 256, 16)

    def body(p_ref, w_ref, m_ref, v_ref, g_out, d_out, m_out, v_out):
        g = p_ref[0].astype(F32)
        for s in range(1, N_DEV):
            g = g + p_ref[s].astype(F32)
        delta, m2, v2 = _adam_math(w_ref[...], g, m_ref[...], v_ref[...])
        g_out[...] = g
        d_out[...] = delta
        m_out[...] = m2
        v_out[...] = v2

    blk = pl.BlockSpec((tr, c), lambda i: (i, 0))
    out = jax.ShapeDtypeStruct((r, c), F32)
    return pl.pallas_call(
        body, name=name, grid=(r // tr,),
        in_specs=[pl.BlockSpec((N_DEV, tr, c), lambda i: (0, i, 0)), blk, blk, blk],
        out_specs=[blk, blk, blk, blk], out_shape=[out, out, out, out],
        compiler_params=_params(),
    )(parts, w, m, v)


def _adam_small(g, w, m, v):
    def body(g_ref, w_ref, m_ref, v_ref, d_out, m_out, v_out):
        delta, m2, v2 = _adam_math(w_ref[...], g_ref[...], m_ref[...], v_ref[...])
        d_out[...] = delta
        m_out[...] = m2
        v_out[...] = v2

    out = jax.ShapeDtypeStruct(g.shape, F32)
    return pl.pallas_call(body, name="adam_small", out_shape=[out, out, out])(g, w, m, v)


def _rms_fwd(x, gain, name, dep=None):
    t, d = x.shape
    tr = _tile(t, 256, 16)

    def body(x_ref, g_ref, o_ref):
        xv = x_ref[...]
        r = lax.rsqrt(jnp.mean(xv * xv, axis=-1, keepdims=True) + NORM_EPS)
        o_ref[...] = (xv * r * g_ref[...]).astype(BF)

    return _call(
        body, [x, gain], dep=dep, name=name, grid=(t // tr,),
        in_specs=[pl.BlockSpec((tr, d), lambda i: (i, 0)), pl.BlockSpec((1, d), lambda i: (0, 0))],
        out_specs=pl.BlockSpec((tr, d), lambda i: (i, 0)),
        out_shape=jax.ShapeDtypeStruct((t, d), BF), compiler_params=_params(),
    )


def _rms_vjp(xv, gain, dy):
    r = lax.rsqrt(jnp.mean(xv * xv, axis=-1, keepdims=True) + NORM_EPS)
    xhat = xv * r
    dxhat = dy * gain
    dx = r * (dxhat - xhat * jnp.mean(dxhat * xhat, axis=-1, keepdims=True))
    dgain = jnp.sum(dy * xhat, axis=0, keepdims=True)
    return dx, dgain


def _rms_bwd(dy, x, gain, dres, name, dep=None):
    t, d = x.shape
    tr = _tile(t, 256, 16)

    def body(dy_ref, x_ref, g_ref, dres_ref, dx_ref, dxb_ref, dg_ref):
        dx, dgain = _rms_vjp(x_ref[...], g_ref[...], dy_ref[...])
        dx = dx + dres_ref[...]
        dx_ref[...] = dx
        dxb_ref[...] = dx.astype(BF)

        @pl.when(pl.program_id(0) == 0)
        def _():
            dg_ref[...] = jnp.zeros_like(dg_ref)

        dg_ref[...] += dgain

    row = pl.BlockSpec((tr, d), lambda i: (i, 0))
    vec = pl.BlockSpec((1, d), lambda i: (0, 0))
    return _call(
        body, [dy, x, gain, dres], dep=dep, name=name, grid=(t // tr,),
        in_specs=[row, row, vec, row], out_specs=[row, row, vec],
        out_shape=[jax.ShapeDtypeStruct((t, d), F32), jax.ShapeDtypeStruct((t, d), BF),
                   jax.ShapeDtypeStruct((1, d), F32)],
        compiler_params=_params(),
    )


def _loss_head(x, gain, target):
    t, d = x.shape
    tr = _tile(t, 256, 16)

    def body(x_ref, g_ref, t_ref, dx_ref, dxb_ref, dg_ref, loss_ref):
        xv = x_ref[...]
        gain = g_ref[...]
        r = lax.rsqrt(jnp.mean(xv * xv, axis=-1, keepdims=True) + NORM_EPS)
        err = xv * r * gain - t_ref[...]
        dx, dgain = _rms_vjp(xv, gain, err * (1.0 / d))
        dx_ref[...] = dx
        dxb_ref[...] = dx.astype(BF)

        @pl.when(pl.program_id(0) == 0)
        def _():
            dg_ref[...] = jnp.zeros_like(dg_ref)
            loss_ref[...] = jnp.zeros_like(loss_ref)

        dg_ref[...] += dgain
        loss_ref[...] += jnp.sum(err * err, axis=0, keepdims=True) * (0.5 / d)

    row = pl.BlockSpec((tr, d), lambda i: (i, 0))
    vec = pl.BlockSpec((1, d), lambda i: (0, 0))
    return pl.pallas_call(
        body, name="loss_head", grid=(t // tr,),
        in_specs=[row, vec, row], out_specs=[row, row, vec, vec],
        out_shape=[jax.ShapeDtypeStruct((t, d), F32), jax.ShapeDtypeStruct((t, d), BF),
                   jax.ShapeDtypeStruct((1, d), F32), jax.ShapeDtypeStruct((1, d), F32)],
        compiler_params=_params(),
    )(x, gain, target)


def _mm_nn(a, b, out_dtype, name, residual=None, tm_pref=512, tn_pref=1152):
    m, k = a.shape
    n = b.shape[1]
    tm, tn = _tile(m, tm_pref, 16), _tile(n, tn_pref, LANE)

    def body(*refs):
        if residual is None:
            a_ref, b_ref, o_ref = refs
            o_ref[...] = _dot(a_ref[...], b_ref[...]).astype(out_dtype)
        else:
            a_ref, b_ref, r_ref, o_ref = refs
            o_ref[...] = (r_ref[...] + _dot(a_ref[...], b_ref[...])).astype(out_dtype)

    in_specs = [pl.BlockSpec((tm, k), lambda j, i: (i, 0)), pl.BlockSpec((k, tn), lambda j, i: (0, j))]
    args = [a, b]
    if residual is not None:
        in_specs.append(pl.BlockSpec((tm, tn), lambda j, i: (i, j)))
        args.append(residual)
    return pl.pallas_call(
        body, name=name, grid=(n // tn, m // tm), in_specs=in_specs,
        out_specs=pl.BlockSpec((tm, tn), lambda j, i: (i, j)),
        out_shape=jax.ShapeDtypeStruct((m, n), out_dtype), compiler_params=_params(),
    )(*args)


def _mm_nt(a, b, out_dtype, name, tm_pref=512, tn_pref=1024, tk_pref=2048):
    m, k = a.shape
    n = b.shape[0]
    tm, tn, tk = _tile(m, tm_pref, 16), _tile(n, tn_pref, LANE), _tile(k, tk_pref, LANE)
    nk = k // tk

    def body(a_ref, b_ref, o_ref, acc_ref):
        kk = pl.program_id(2)

        @pl.when(kk == 0)
        def _():
            acc_ref[...] = jnp.zeros_like(acc_ref)

        acc_ref[...] += _dot(a_ref[...], b_ref[...], NT)

        @pl.when(kk == nk - 1)
        def _():
            o_ref[...] = acc_ref[...].astype(out_dtype)

    return pl.pallas_call(
        body, name=name, grid=(n // tn, m // tm, nk),
        in_specs=[pl.BlockSpec((tm, tk), lambda j, i, kk: (i, kk)),
                  pl.BlockSpec((tn, tk), lambda j, i, kk: (j, kk))],
        out_specs=pl.BlockSpec((tm, tn), lambda j, i, kk: (i, j)),
        out_shape=jax.ShapeDtypeStruct((m, n), out_dtype),
        scratch_shapes=[pltpu.VMEM((tm, tn), F32)], compiler_params=_params(),
    )(a, b)


def _mm_tn(a, b, out_dtype, name, tn_pref=1152, tk_pref=512):
    t, k = a.shape
    n = b.shape[1]
    tn, tk = _tile(n, tn_pref, LANE), _tile(t, tk_pref, 16)
    nt = t // tk

    def body(a_ref, b_ref, o_ref, acc_ref):
        tt = pl.program_id(1)

        @pl.when(tt == 0)
        def _():
            acc_ref[...] = jnp.zeros_like(acc_ref)

        acc_ref[...] += _dot(a_ref[...], b_ref[...], TN)

        @pl.when(tt == nt - 1)
        def _():
            o_ref[...] = acc_ref[...].astype(out_dtype)

    return pl.pallas_call(
        body, name=name, grid=(n // tn, nt),
        in_specs=[pl.BlockSpec((tk, k), lambda j, tt: (tt, 0)), pl.BlockSpec((tk, tn), lambda j, tt: (tt, j))],
        out_specs=pl.BlockSpec((k, tn), lambda j, tt: (0, j)),
        out_shape=jax.ShapeDtypeStruct((k, n), out_dtype),
        scratch_shapes=[pltpu.VMEM((k, tn), F32)], compiler_params=_params(),
    )(a, b)


def _ffn_gate_up(hn, wg, wu, name):
    t, d = hn.shape
    ns, _, f = wg.shape
    tm = _tile(t, 512, 16)

    def body(h_ref, wg_ref, wu_ref, g_ref, u_ref, a_ref):
        h = h_ref[...]
        g = _dot(h, wg_ref[...])
        u = _dot(h, wu_ref[...])
        g_ref[...] = g.astype(BF)
        u_ref[...] = u.astype(BF)
        a_ref[...] = (g * _sig(g) * u).astype(BF)

    wspec = pl.BlockSpec((None, d, f), lambda j, i: (j, 0, 0))
    hid = pl.BlockSpec((None, tm, f), lambda j, i: (j, i, 0))
    out = jax.ShapeDtypeStruct((ns, t, f), BF)
    return pl.pallas_call(
        body, name=name, grid=(ns, t // tm),
        in_specs=[pl.BlockSpec((tm, d), lambda j, i: (i, 0)), wspec, wspec],
        out_specs=[hid, hid, hid], out_shape=[out, out, out], compiler_params=_params(),
    )(hn, wg, wu)


def _ffn_down(act, wd, xres, name):
    ns, t, f = act.shape
    d = wd.shape[2]
    tm = _tile(t, 512, 16)

    def body(a_ref, w_ref, x_ref, o_ref):
        @pl.when(pl.program_id(1) == 0)
        def _():
            o_ref[...] = x_ref[...]

        o_ref[...] += 0.5 * _dot(a_ref[...], w_ref[...])

    row = pl.BlockSpec((tm, d), lambda i, j: (i, 0))
    return pl.pallas_call(
        body, name=name, grid=(t // tm, ns),
        in_specs=[pl.BlockSpec((None, tm, f), lambda i, j: (j, i, 0)),
                  pl.BlockSpec((None, f, d), lambda i, j: (j, 0, 0)), row],
        out_specs=row, out_shape=jax.ShapeDtypeStruct((t, d), F32), compiler_params=_params(),
    )(act, wd, xres)


def _ffn_bwd_hidden(dxb, wd, g, u, name):
    t, d = dxb.shape
    ns, f, _ = wd.shape
    tm = _tile(t, 512, 16)

    def body(dx_ref, w_ref, g_ref, u_ref, dg_ref, du_ref):
        dh = 0.5 * _dot(dx_ref[...], w_ref[...], NT)
        gv = g_ref[...].astype(F32)
        uv = u_ref[...].astype(F32)
        s = _sig(gv)
        dg_ref[...] = (dh * uv * (s * (1.0 + gv * (1.0 - s)))).astype(BF)
        du_ref[...] = (dh * (gv * s)).astype(BF)

    hid = pl.BlockSpec((None, tm, f), lambda j, i: (j, i, 0))
    out = jax.ShapeDtypeStruct((ns, t, f), BF)
    return pl.pallas_call(
        body, name=name, grid=(ns, t // tm),
        in_specs=[pl.BlockSpec((tm, d), lambda j, i: (i, 0)),
                  pl.BlockSpec((None, f, d), lambda j, i: (j, 0, 0)), hid, hid],
        out_specs=[hid, hid], out_shape=[out, out], compiler_params=_params(),
    )(dxb, wd, g, u)


def _ffn_dw_down(act, dxb, name):
    ns, t, f = act.shape
    d = dxb.shape[1]
    tk = _tile(t, 512, 16)
    nt = t // tk

    def body(a_ref, dx_ref, o_ref, acc_ref):
        tt = pl.program_id(1)

        @pl.when(tt == 0)
        def _():
            acc_ref[...] = jnp.zeros_like(acc_ref)

        acc_ref[...] += _dot(a_ref[...], dx_ref[...], TN)

        @pl.when(tt == nt - 1)
        def _():
            o_ref[...] = (0.5 * acc_ref[...]).astype(BF)

    return pl.pallas_call(
        body, name=name, grid=(ns, nt),
        in_specs=[pl.BlockSpec((None, tk, f), lambda j, tt: (j, tt, 0)),
                  pl.BlockSpec((tk, d), lambda j, tt: (tt, 0))],
        out_specs=pl.BlockSpec((None, f, d), lambda j, tt: (j, 0, 0)),
        out_shape=jax.ShapeDtypeStruct((ns, f, d), BF),
        scratch_shapes=[pltpu.VMEM((f, d), F32)], compiler_params=_params(),
    )(act, dxb)


def _ffn_dw_gate_up(hn, dg, du, name, dep=None):
    t, d = hn.shape
    ns, _, f = dg.shape
    tk = _tile(t, 512, 16)
    nt = t // tk

    def body(h_ref, dg_ref, du_ref, og_ref, ou_ref, accg_ref, accu_ref):
        tt = pl.program_id(1)

        @pl.when(tt == 0)
        def _():
            accg_ref[...] = jnp.zeros_like(accg_ref)
            accu_ref[...] = jnp.zeros_like(accu_ref)

        h = h_ref[...]
        accg_ref[...] += _dot(h, dg_ref[...], TN)
        accu_ref[...] += _dot(h, du_ref[...], TN)

        @pl.when(tt == nt - 1)
        def _():
            og_ref[...] = accg_ref[...].astype(BF)
            ou_ref[...] = accu_ref[...].astype(BF)

    hid = pl.BlockSpec((None, tk, f), lambda j, tt: (j, tt, 0))
    wspec = pl.BlockSpec((None, d, f), lambda j, tt: (j, 0, 0))
    out = jax.ShapeDtypeStruct((ns, d, f), BF)
    return _call(
        body, [hn, dg, du], dep=dep, name=name, grid=(ns, nt),
        in_specs=[pl.BlockSpec((tk, d), lambda j, tt: (tt, 0)), hid, hid],
        out_specs=[wspec, wspec], out_shape=[out, out],
        scratch_shapes=[pltpu.VMEM((d, f), F32), pltpu.VMEM((d, f), F32)], compiler_params=_params(),
    )


def _ffn_bwd_input(dg, du, wg, wu, name, dep=None):
    ns, t, f = dg.shape
    d = wg.shape[1]
    tm = _tile(t, 512, 16)

    def body(dg_ref, du_ref, wg_ref, wu_ref, o_ref):
        @pl.when(pl.program_id(1) == 0)
        def _():
            o_ref[...] = jnp.zeros_like(o_ref)

        o_ref[...] += _dot(dg_ref[...], wg_ref[...], NT) + _dot(du_ref[...], wu_ref[...], NT)

    hid = pl.BlockSpec((None, tm, f), lambda i, j: (j, i, 0))
    wspec = pl.BlockSpec((None, d, f), lambda i, j: (j, 0, 0))
    return _call(
        body, [dg, du, wg, wu], dep=dep, name=name, grid=(t // tm, ns),
        in_specs=[hid, hid, wspec, wspec],
        out_specs=pl.BlockSpec((tm, d), lambda i, j: (i, 0)),
        out_shape=jax.ShapeDtypeStruct((t, d), F32), compiler_params=_params(),
    )


def _rope_tables(t):
    pos = jnp.arange(t, dtype=F32)
    inv_freq = ROPE_THETA ** (-jnp.arange(0, ROPE_DIM, 2, dtype=F32) / ROPE_DIM)
    ang = pos[:, None] * inv_freq[None, :]
    cos, sin = jnp.cos(ang), jnp.sin(ang)
    rest = HEAD_DIM - ROPE_DIM
    one = jnp.ones((t, rest), F32)
    zero_h = jnp.zeros((t, ROPE_HALF), F32)
    zero_r = jnp.zeros((t, rest), F32)
    c = jnp.concatenate([cos, cos, one], axis=1)
    s1 = jnp.concatenate([-sin, zero_h, zero_r], axis=1)
    s2 = jnp.concatenate([zero_h, sin, zero_r], axis=1)
    return c, s1, s2


def _rope(xh, c, s1, s2):
    return xh * c + pltpu.roll(xh, HEAD_DIM - ROPE_HALF, 1) * s1 + pltpu.roll(xh, ROPE_HALF, 1) * s2


def _rope_t(dh, c, s1, s2):
    return dh * c + pltpu.roll(dh * s1, ROPE_HALF, 1) + pltpu.roll(dh * s2, HEAD_DIM - ROPE_HALF, 1)


def _mixer_prep(proj, tables, bf_pad, hd, scale):
    t, np_ = proj.shape
    tr = _tile(t, 256, 16)
    nh = hd // HEAD_DIM
    nblk = hd // LANE
    f_blk = np_ // LANE - 1

    def body(qd_ref, kd_ref, vd_ref, qf_ref, kf_ref, vf_ref, fl_ref, c_ref, s1_ref, s2_ref, b_ref,
             oqd, okd, ovd, oqf, okf, ovf, olog):
        c, s1, s2 = c_ref[...], s1_ref[...], s2_ref[...]
        for h in range(nh):
            sl = slice(h * HEAD_DIM, (h + 1) * HEAD_DIM)
            oqd[:, sl] = (_rope(qd_ref[:, sl], c, s1, s2) * scale).astype(BF)
            okd[:, sl] = _rope(kd_ref[:, sl], c, s1, s2).astype(BF)
        ovd[...] = vd_ref[...].astype(BF)
        oqf[...] = (qf_ref[...] * scale).astype(BF)
        okf[...] = kf_ref[...].astype(BF)
        ovf[...] = vf_ref[...].astype(BF)
        z = fl_ref[...] + b_ref[...]
        olog[...] = jnp.minimum(z, 0.0) - jnp.log(1.0 + jnp.exp(-jnp.abs(z)))

    def col(kblk):
        return pl.BlockSpec((tr, hd), lambda i, kblk=kblk: (i, kblk))

    lane_row = pl.BlockSpec((tr, LANE), lambda i: (i, 0))
    in_specs = [col(0), col(1), col(2), col(3), col(4), col(5),
                pl.BlockSpec((tr, LANE), lambda i: (i, f_blk)),
                lane_row, lane_row, lane_row, pl.BlockSpec((1, LANE), lambda i: (0, 0))]
    o = pl.BlockSpec((tr, hd), lambda i: (i, 0))
    ob = jax.ShapeDtypeStruct((t, hd), BF)
    del nblk
    return pl.pallas_call(
        body, name="mixer_prep", grid=(t // tr,), in_specs=in_specs,
        out_specs=[o, o, o, o, o, o, lane_row],
        out_shape=[ob, ob, ob, ob, ob, ob, jax.ShapeDtypeStruct((t, LANE), F32)],
        compiler_params=_params(),
    )(proj, proj, proj, proj, proj, proj, proj, *tables, bf_pad)


def _split3(x):
    x1 = x.astype(BF)
    r1 = x - x1.astype(F32)
    x2 = r1.astype(BF)
    x3 = (r1 - x2.astype(F32)).astype(BF)
    return x1, x2, x3


def _cumsum_rows(x, reverse, name):
    t, w = x.shape
    blk = LANE
    nb = t // blk

    def body(x_ref, o_ref):
        r = lax.broadcasted_iota(jnp.int32, (blk, blk), 0)
        c = lax.broadcasted_iota(jnp.int32, (blk, blk), 1)
        tri = jnp.where((c >= r) if reverse else (c <= r), 1.0, 0.0).astype(BF)

        def step(i, carry):
            b = (nb - 1 - i) if reverse else i
            off = pl.multiple_of(b * blk, blk)
            xb = x_ref[pl.ds(off, blk), :]
            x1, x2, x3 = _split3(xb)
            o_ref[pl.ds(off, blk), :] = _dot(tri, x1) + _dot(tri, x2) + _dot(tri, x3) + carry
            return carry + jnp.sum(xb, axis=0, keepdims=True)

        lax.fori_loop(0, nb, step, jnp.zeros((1, w), F32))

    return pl.pallas_call(body, name=name, out_shape=jax.ShapeDtypeStruct((t, w), F32),
                          compiler_params=_params())(x)


def _dil_mult(delta, m4, m16):
    (w0, _), (w1, _), (w2, _) = DIL_PATTERNS
    ok = delta >= 0
    mult = (jnp.where(ok & (delta <= w0), 1.0, 0.0) + jnp.where(ok & m4 & (delta <= w1), 1.0, 0.0)
            + jnp.where(ok & m16 & (delta <= w2), 1.0, 0.0))
    return mult


def _scores(mode, s, delta, m4, m16, bias):
    if mode == "fox":
        return jnp.where(delta >= 0, s + bias, NEG), None
    mult = _dil_mult(delta, m4, m16)
    return jnp.where(mult > 0.0, s, NEG), mult


def _attn_fwd(mode, q, k, v, c_col, c_row, tq, name):
    t, hd = q.shape
    nh = hd // HEAD_DIM
    nb = t // tq
    wb = MAX_WINDOW // tq
    fox = mode == "fox"

    def body(*refs):
        if fox:
            q_ref, k_ref, v_ref, cc_ref, cr_ref, o_ref, lse_ref = refs
        else:
            q_ref, k_ref, v_ref, o_ref, lse_ref = refs
        qi = pl.program_id(1)
        qb = q_ref[...]
        ij = lax.broadcasted_iota(jnp.int32, (tq, tq), 0) - lax.broadcasted_iota(jnp.int32, (tq, tq), 1)
        m4, m16 = (ij & 3) == 0, (ij & 15) == 0

        def step(kj, carry):
            m, l, acc = carry
            off = pl.multiple_of(kj * tq, tq)
            kb = k_ref[pl.ds(off, tq), :]
            vb = v_ref[pl.ds(off, tq), :]
            s = _dot(qb, kb, NT)
            bias = (cc_ref[...] - cr_ref[kj]) if fox else None
            s, mult = _scores(mode, s, ij + (qi - kj) * tq, m4, m16, bias)
            m_new = jnp.maximum(m, jnp.max(s, axis=1, keepdims=True))
            p = jnp.exp(s - m_new)
            if mult is not None:
                p = p * mult
            alpha = jnp.exp(m - m_new)
            l = alpha * l + jnp.sum(p, axis=1, keepdims=True)
            acc = alpha * acc + _dot(p.astype(BF), vb)
            return m_new, l, acc

        lo = 0 if fox else jnp.maximum(qi - wb, 0)
        init = (jnp.full((tq, 1), NEG, F32), jnp.zeros((tq, 1), F32), jnp.zeros((tq, HEAD_DIM), F32))
        m, l, acc = lax.fori_loop(lo, qi + 1, step, init)
        o_ref[...] = (acc / l).astype(BF)
        lse_ref[...] = m + jnp.log(l)

    qspec = pl.BlockSpec((tq, HEAD_DIM), lambda h, i: (i, h))
    kvspec = pl.BlockSpec((t, HEAD_DIM), lambda h, i: (0, h))
    colspec = pl.BlockSpec((None, tq, 1), lambda h, i: (h, i, 0))
    in_specs = [qspec, kvspec, kvspec]
    args = [q, k, v]
    if fox:
        in_specs += [colspec, pl.BlockSpec((None, nb, 1, tq), lambda h, i: (h, 0, 0, 0))]
        args += [c_col, c_row]
    return pl.pallas_call(
        body, name=name, grid=(nh, nb), in_specs=in_specs,
        out_specs=[qspec, colspec],
        out_shape=[jax.ShapeDtypeStruct((t, hd), BF), jax.ShapeDtypeStruct((nh, t, 1), F32)],
        compiler_params=_params(),
    )(*args)


def _attn_bwd_dq(mode, q, k, v, o, do, lse, c_col, c_row, tq, name, dep=None):
    t, hd = q.shape
    nh = hd // HEAD_DIM
    nb = t // tq
    wb = MAX_WINDOW // tq
    fox = mode == "fox"

    def body(*refs):
        if fox:
            q_ref, k_ref, v_ref, o_ref, do_ref, lse_ref, cc_ref, cr_ref, dq_ref, dl_ref = refs
        else:
            q_ref, k_ref, v_ref, o_ref, do_ref, lse_ref, dq_ref, dl_ref = refs
        qi = pl.program_id(1)
        qb = q_ref[...]
        dob = do_ref[...]
        lse = lse_ref[...]
        ij = lax.broadcasted_iota(jnp.int32, (tq, tq), 0) - lax.broadcasted_iota(jnp.int32, (tq, tq), 1)
        m4, m16 = (ij & 3) == 0, (ij & 15) == 0
        zero = jnp.zeros((tq, HEAD_DIM), F32)

        def probs(kj):
            off = pl.multiple_of(kj * tq, tq)
            kb = k_ref[pl.ds(off, tq), :]
            vb = v_ref[pl.ds(off, tq), :]
            s = _dot(qb, kb, NT)
            bias = (cc_ref[...] - cr_ref[kj]) if fox else None
            s, mult = _scores(mode, s, ij + (qi - kj) * tq, m4, m16, bias)
            p = jnp.exp(s - lse)
            if mult is not None:
                p = p * mult
            return p, _dot(dob, vb, NT), kb

        if fox:
            def step(kj, carry):
                a, b, dl = carry
                p, dp, kb = probs(kj)
                pdp = p * dp
                return (a + _dot(pdp.astype(BF), kb), b + _dot(p.astype(BF), kb),
                        dl + jnp.sum(pdp, axis=1, keepdims=True))

            a, b, dl = lax.fori_loop(0, qi + 1, step, (zero, zero, jnp.zeros((tq, 1), F32)))
            dl_ref[...] = dl
            dq_ref[...] = a - dl * b
        else:
            dl = jnp.sum(o_ref[...].astype(F32) * dob.astype(F32), axis=1, keepdims=True)
            dl_ref[...] = dl

            def step(kj, dq):
                p, dp, kb = probs(kj)
                return dq + _dot((p * (dp - dl)).astype(BF), kb)

            dq_ref[...] = lax.fori_loop(jnp.maximum(qi - wb, 0), qi + 1, step, zero)

    qspec = pl.BlockSpec((tq, HEAD_DIM), lambda h, i: (i, h))
    kvspec = pl.BlockSpec((t, HEAD_DIM), lambda h, i: (0, h))
    colspec = pl.BlockSpec((None, tq, 1), lambda h, i: (h, i, 0))
    in_specs = [qspec, kvspec, kvspec, qspec, qspec, colspec]
    args = [q, k, v, o, do, lse]
    if fox:
        in_specs += [colspec, pl.BlockSpec((None, nb, 1, tq), lambda h, i: (h, 0, 0, 0))]
        args += [c_col, c_row]
    return _call(
        body, args, dep=dep, name=name, grid=(nh, nb), in_specs=in_specs,
        out_specs=[qspec, colspec],
        out_shape=[jax.ShapeDtypeStruct((t, hd), F32), jax.ShapeDtypeStruct((nh, t, 1), F32)],
        compiler_params=_params(),
    )


def _attn_bwd_dkv(mode, q, k, v, do, lse_row, dl_row, c_col, c_row, tq, name):
    t, hd = q.shape
    nh = hd // HEAD_DIM
    nb = t // tq
    wb = MAX_WINDOW // tq
    fox = mode == "fox"

    def body(*refs):
        if fox:
            q_ref, k_ref, v_ref, do_ref, lse_ref, dl_ref, cc_ref, cr_ref, dk_ref, dv_ref, dc_ref = refs
        else:
            q_ref, k_ref, v_ref, do_ref, lse_ref, dl_ref, dk_ref, dv_ref = refs
        kj = pl.program_id(1)
        kb = k_ref[...]
        vb = v_ref[...]
        ji = lax.broadcasted_iota(jnp.int32, (tq, tq), 1) - lax.broadcasted_iota(jnp.int32, (tq, tq), 0)
        m4, m16 = (ji & 3) == 0, (ji & 15) == 0

        def step(qi, carry):
            dk, dv, dc = carry
            off = pl.multiple_of(qi * tq, tq)
            qb = q_ref[pl.ds(off, tq), :]
            dob = do_ref[pl.ds(off, tq), :]
            st = _dot(kb, qb, NT)
            bias = (cr_ref[qi] - cc_ref[...]) if fox else None
            st, mult = _scores(mode, st, ji + (qi - kj) * tq, m4, m16, bias)
            pt = jnp.exp(st - lse_ref[qi])
            if mult is not None:
                pt = pt * mult
            dst = pt * (_dot(vb, dob, NT) - dl_ref[qi])
            dv = dv + _dot(pt.astype(BF), dob)
            dk = dk + _dot(dst.astype(BF), qb)
            if fox:
                dc = dc - jnp.sum(dst, axis=1, keepdims=True)
            return dk, dv, dc

        hi = nb if fox else jnp.minimum(kj + wb + 1, nb)
        zero = jnp.zeros((tq, HEAD_DIM), F32)
        dk, dv, dc = lax.fori_loop(kj, hi, step, (zero, zero, jnp.zeros((tq, 1), F32)))
        dk_ref[...] = dk
        dv_ref[...] = dv
        if fox:
            dc_ref[...] = dc

    blkspec = pl.BlockSpec((tq, HEAD_DIM), lambda h, j: (j, h))
    fullspec = pl.BlockSpec((t, HEAD_DIM), lambda h, j: (0, h))
    rowspec = pl.BlockSpec((None, nb, 1, tq), lambda h, j: (h, 0, 0, 0))
    colspec = pl.BlockSpec((None, tq, 1), lambda h, j: (h, j, 0))
    in_specs = [fullspec, blkspec, blkspec, fullspec, rowspec, rowspec]
    args = [q, k, v, do, lse_row, dl_row]
    out_specs = [blkspec, blkspec]
    out_shape = [jax.ShapeDtypeStruct((t, hd), F32), jax.ShapeDtypeStruct((t, hd), F32)]
    if fox:
        in_specs += [colspec, rowspec]
        args += [c_col, c_row]
        out_specs.append(colspec)
        out_shape.append(jax.ShapeDtypeStruct((nh, t, 1), F32))
    return pl.pallas_call(
        body, name=name, grid=(nh, nb), in_specs=in_specs, out_specs=out_specs, out_shape=out_shape,
        compiler_params=_params(),
    )(*args)


def _gate_specs(t, d, hd, tr):
    row = pl.BlockSpec((tr, d), lambda i: (i, 0))
    vec = pl.BlockSpec((1, d), lambda i: (0, 0))
    base = 6 * hd // d
    gd = pl.BlockSpec((tr, d), lambda i: (i, base))
    gf = pl.BlockSpec((tr, d), lambda i: (i, base + 1))
    return row, vec, gd, gf


def _merge_fwd(pd, pf, proj, b_d, b_f, hd):
    t, d = pd.shape
    tr = _tile(t, 256, 16)
    row, vec, gd, gf = _gate_specs(t, d, hd, tr)

    def body(pd_ref, pf_ref, gd_ref, gf_ref, bd_ref, bf_ref, o_ref):
        o_ref[...] = (_sig(gd_ref[...] + bd_ref[...]) * pd_ref[...]
                      + _sig(gf_ref[...] + bf_ref[...]) * pf_ref[...]).astype(BF)

    return pl.pallas_call(
        body, name="merge_fwd", grid=(t // tr,), in_specs=[row, row, gd, gf, vec, vec],
        out_specs=row, out_shape=jax.ShapeDtypeStruct((t, d), BF), compiler_params=_params(),
    )(pd, pf, proj, proj, b_d, b_f)


def _merge_bwd(dm, pd, pf, proj, b_d, b_f, hd):
    t, d = pd.shape
    tr = _tile(t, 256, 16)
    row, vec, gd, gf = _gate_specs(t, d, hd, tr)

    def body(dm_ref, pd_ref, pf_ref, gd_ref, gf_ref, bd_ref, bf_ref,
             dpd_ref, dpf_ref, dgd_ref, dgf_ref, dbd_ref, dbf_ref):
        dmv = dm_ref[...]
        sd = _sig(gd_ref[...] + bd_ref[...])
        sf = _sig(gf_ref[...] + bf_ref[...])
        dgd = dmv * pd_ref[...] * (sd * (1.0 - sd))
        dgf = dmv * pf_ref[...] * (sf * (1.0 - sf))
        dpd_ref[...] = (dmv * sd).astype(BF)
        dpf_ref[...] = (dmv * sf).astype(BF)
        dgd_ref[...] = dgd.astype(BF)
        dgf_ref[...] = dgf.astype(BF)

        @pl.when(pl.program_id(0) == 0)
        def _():
            dbd_ref[...] = jnp.zeros_like(dbd_ref)
            dbf_ref[...] = jnp.zeros_like(dbf_ref)

        dbd_ref[...] += jnp.sum(dgd, axis=0, keepdims=True)
        dbf_ref[...] += jnp.sum(dgf, axis=0, keepdims=True)

    ob = jax.ShapeDtypeStruct((t, d), BF)
    ov = jax.ShapeDtypeStruct((1, d), F32)
    return pl.pallas_call(
        body, name="merge_bwd", grid=(t // tr,), in_specs=[row, row, row, gd, gf, vec, vec],
        out_specs=[row, row, row, row, vec, vec], out_shape=[ob, ob, ob, ob, ov, ov],
        compiler_params=_params(),
    )(dm, pd, pf, proj, proj, b_d, b_f)


def _assemble_dproj(dqd, dkd, dvd, dqf, dkf, dvf, dgd, dgf, dlogf, proj, tables, bf_pad, scale):
    t, np_ = proj.shape
    hd = dqd.shape[1]
    d = dgd.shape[1]
    nh = hd // HEAD_DIM
    tr = _tile(t, 256, 16)
    f_blk = np_ // LANE - 1

    def body(dqd_ref, dkd_ref, dvd_ref, dqf_ref, dkf_ref, dvf_ref, dgd_ref, dgf_ref, dlog_ref, fl_ref,
             c_ref, s1_ref, s2_ref, b_ref, o_ref, db_ref):
        c, s1, s2 = c_ref[...], s1_ref[...], s2_ref[...]
        for h in range(nh):
            sl = slice(h * HEAD_DIM, (h + 1) * HEAD_DIM)
            o_ref[:, sl] = (_rope_t(dqd_ref[:, sl], c, s1, s2) * scale).astype(BF)
            o_ref[:, hd + h * HEAD_DIM:hd + (h + 1) * HEAD_DIM] = _rope_t(dkd_ref[:, sl], c, s1, s2).astype(BF)
        o_ref[:, 2 * hd:3 * hd] = dvd_ref[...].astype(BF)
        o_ref[:, 3 * hd:4 * hd] = (dqf_ref[...] * scale).astype(BF)
        o_ref[:, 4 * hd:5 * hd] = dkf_ref[...].astype(BF)
        o_ref[:, 5 * hd:6 * hd] = dvf_ref[...].astype(BF)
        o_ref[:, 6 * hd:6 * hd + d] = dgd_ref[...]
        o_ref[:, 6 * hd + d:6 * hd + 2 * d] = dgf_ref[...]
        z = fl_ref[...] + b_ref[...]
        dfl = dlog_ref[...] * _sig(-z)
        o_ref[:, 6 * hd + 2 * d:] = dfl.astype(BF)

        @pl.when(pl.program_id(0) == 0)
        def _():
            db_ref[...] = jnp.zeros_like(db_ref)

        db_ref[...] += jnp.sum(dfl, axis=0, keepdims=True)

    head = pl.BlockSpec((tr, hd), lambda i: (i, 0))
    row = pl.BlockSpec((tr, d), lambda i: (i, 0))
    lane_row = pl.BlockSpec((tr, LANE), lambda i: (i, 0))
    lane_vec = pl.BlockSpec((1, LANE), lambda i: (0, 0))
    return pl.pallas_call(
        body, name="assemble_dproj", grid=(t // tr,),
        in_specs=[head] * 6 + [row, row, lane_row, pl.BlockSpec((tr, LANE), lambda i: (i, f_blk)),
                               lane_row, lane_row, lane_row, lane_vec],
        out_specs=[pl.BlockSpec((tr, np_), lambda i: (i, 0)), lane_vec],
        out_shape=[jax.ShapeDtypeStruct((t, np_), BF), jax.ShapeDtypeStruct((1, LANE), F32)],
        compiler_params=_params(),
    )(dqd, dkd, dvd, dqf, dkf, dvf, dgd, dgf, dlogf, proj, *tables, bf_pad)


def _to_rows(a, tq):
    h, t, _ = a.shape
    return a.reshape(h, t // tq, 1, tq)


def kernel(x, ffn1_norm, ffn1_w_gate, ffn1_w_up, ffn1_w_down, mix_norm, w_in, b_forget, b_gate_dil, b_gate_fox, w_proj_dil, w_proj_fox, w_out, ffn2_norm, ffn2_w_gate, ffn2_w_up, ffn2_w_down, final_norm, loss_target, m_ffn1_norm, m_ffn1_w_gate, m_ffn1_w_up, m_ffn1_w_down, m_mix_norm, m_w_in, m_b_forget, m_b_gate_dil, m_b_gate_fox, m_w_proj_dil, m_w_proj_fox, m_w_out, m_ffn2_norm, m_ffn2_w_gate, m_ffn2_w_up, m_ffn2_w_down, m_final_norm, v_ffn1_norm, v_ffn1_w_gate, v_ffn1_w_up, v_ffn1_w_down, v_mix_norm, v_w_in, v_b_forget, v_b_gate_dil, v_b_gate_fox, v_w_proj_dil, v_w_proj_fox, v_w_out, v_ffn2_norm, v_ffn2_w_gate, v_ffn2_w_up, v_ffn2_w_down, v_final_norm):
    t, d = x.shape[1], x.shape[2]
    hd = w_proj_dil.shape[1]
    nh = hd // HEAD_DIM
    n_f = b_forget.shape[1]
    cols = w_in.shape[2]
    in_cols = N_DEV * cols
    assert in_cols == 6 * hd + n_f + 2 * d and n_f == nh and n_f <= LANE
    np_ = 6 * hd + 2 * d + LANE
    scale = HEAD_DIM ** -0.5
    tq = _tile(t, 512, LANE)
    assert MAX_WINDOW % tq == 0 and tq % 16 == 0

    x2d = x[0]
    tgt = loss_target[0]

    ag_order = [ffn1_w_gate, ffn1_w_up, ffn1_w_down, w_in, w_proj_dil, w_proj_fox, w_out,
                ffn2_w_gate, ffn2_w_up, ffn2_w_down]
    ag, ag_token = _exchange_start([w[0].astype(BF) for w in ag_order], True, "ag_start")

    def gathered(idx, after, name):
        return _exchange_wait([ag[i] for i in idx], True, after, name)

    tables = _rope_tables(t)
    bf_pad = jnp.pad(b_forget, ((0, 0), (0, LANE - n_f)))

    hn1 = _rms_fwd(x2d, ffn1_norm, "rms_ffn1", dep=ag_token)
    wg1, wu1 = gathered([0, 1], hn1, "ag_wait_ffn1_gate_up")
    g1, u1, a1 = _ffn_gate_up(hn1, wg1, wu1, "ffn1_gate_up")
    wd1, = gathered([2], a1, "ag_wait_ffn1_down")
    x1 = _ffn_down(a1, wd1, x2d, "ffn1_down")

    hm = _rms_fwd(x1, mix_norm, "rms_mix")
    win_g, = gathered([3], hm, "ag_wait_w_in")
    win_full = win_g.transpose(1, 0, 2).reshape(d, in_cols)
    win_p = jnp.concatenate([win_full[:, :6 * hd], win_full[:, 6 * hd + n_f:], win_full[:, 6 * hd:6 * hd + n_f],
                             jnp.zeros((d, LANE - n_f), BF)], axis=1)
    proj = _mm_nn(hm, win_p, F32, "w_in_fwd")
    qd, kd, vd, qf, kf, vf, logf = _mixer_prep(proj, tables, bf_pad, hd, scale)
    csum = _cumsum_rows(logf, False, "cumsum_logf")
    c_col = csum[:, :nh].T.reshape(nh, t, 1)
    c_row = _to_rows(c_col, tq)
    yd, lse_d = _attn_fwd("dil", qd, kd, vd, None, None, tq, "attn_dil_fwd")
    yf, lse_f = _attn_fwd("fox", qf, kf, vf, c_col, c_row, tq, "attn_fox_fwd")
    wpd_g, wpf_g = gathered([4, 5], yf, "ag_wait_proj")
    wpd = wpd_g.transpose(1, 0, 2).reshape(hd, d)
    wpf = wpf_g.transpose(1, 0, 2).reshape(hd, d)
    pd = _mm_nn(yd, wpd, F32, "proj_dil_fwd", tn_pref=1024)
    pf = _mm_nn(yf, wpf, F32, "proj_fox_fwd", tn_pref=1024)
    merged = _merge_fwd(pd, pf, proj, b_gate_dil, b_gate_fox, hd)
    wout_g, = gathered([6], merged, "ag_wait_w_out")
    wout = wout_g.reshape(d, d)
    x2 = _mm_nn(merged, wout, F32, "w_out_fwd", residual=x1, tn_pref=1024)

    hn2 = _rms_fwd(x2, ffn2_norm, "rms_ffn2")
    wg2, wu2 = gathered([7, 8], hn2, "ag_wait_ffn2_gate_up")
    g2, u2, a2 = _ffn_gate_up(hn2, wg2, wu2, "ffn2_gate_up")
    wd2, = gathered([9], a2, "ag_wait_ffn2_down")
    x3 = _ffn_down(a2, wd2, x2, "ffn2_down")

    dx3, dx3b, d_final, loss_lanes = _loss_head(x3, final_norm.reshape(1, d), tgt)

    def ffn_bwd(dxb, hn, g, u, a, wg, wu, wd, tag):
        dg, du = _ffn_bwd_hidden(dxb, wd, g, u, tag + "_bwd_hidden")
        dwd = _ffn_dw_down(a, dxb, tag + "_dw_down")
        rs_down, tok = _exchange_start([dwd], False, "rs_start_" + tag + "_down")
        dwg, dwu = _ffn_dw_gate_up(hn, dg, du, tag + "_dw_gate_up", dep=tok)
        rs_gu, tok = _exchange_start([dwg, dwu], False, "rs_start_" + tag + "_gate_up")
        dhn = _ffn_bwd_input(dg, du, wg, wu, tag + "_bwd_input", dep=tok)
        return dhn, rs_gu + rs_down

    dhn2, rs_ffn2 = ffn_bwd(dx3b, hn2, g2, u2, a2, wg2, wu2, wd2, "ffn2")
    dx2, dx2b, d_ffn2_norm = _rms_bwd(dhn2, x2, ffn2_norm, dx3, "rms_ffn2_bwd")

    dmerged = _mm_nt(dx2b, wout, F32, "w_out_bwd")
    dwout = _mm_tn(merged, dx2b, BF, "w_out_dw", tn_pref=1024)
    dpd, dpf, dgd, dgf, d_bd, d_bf = _merge_bwd(dmerged, pd, pf, proj, b_gate_dil, b_gate_fox, hd)
    dyd = _mm_nt(dpd, wpd, BF, "proj_dil_bwd")
    dyf = _mm_nt(dpf, wpf, BF, "proj_fox_bwd")
    dwpd = _mm_tn(yd, dpd, BF, "proj_dil_dw", tn_pref=1024)
    dwpf = _mm_tn(yf, dpf, BF, "proj_fox_dw", tn_pref=1024)
    dwpd_c = dwpd.reshape(hd, N_DEV, d // N_DEV).transpose(1, 0, 2)
    dwpf_c = dwpf.reshape(hd, N_DEV, d // N_DEV).transpose(1, 0, 2)
    dwout_c = dwout.reshape(N_DEV, d // N_DEV, d)
    rs_mix, tok = _exchange_start([dwout_c, dwpd_c, dwpf_c], False, "rs_start_mixer")

    dqd, dl_d = _attn_bwd_dq("dil", qd, kd, vd, yd, dyd, lse_d, None, None, tq, "attn_dil_dq", dep=tok)
    dkd, dvd = _attn_bwd_dkv("dil", qd, kd, vd, dyd, _to_rows(lse_d, tq), _to_rows(dl_d, tq), None, None, tq,
                             "attn_dil_dkv")
    dqf, dl_f = _attn_bwd_dq("fox", qf, kf, vf, yf, dyf, lse_f, c_col, c_row, tq, "attn_fox_dq")
    dkf, dvf, dc = _attn_bwd_dkv("fox", qf, kf, vf, dyf, _to_rows(lse_f, tq), _to_rows(dl_f, tq), c_col, c_row, tq,
                                 "attn_fox_dkv")
    dc_pad = jnp.pad(dc.reshape(nh, t).T, ((0, 0), (0, LANE - nh)))
    dlogf = _cumsum_rows(dc_pad, True, "revcumsum_dc")
    dproj, d_bforget = _assemble_dproj(dqd, dkd, dvd, dqf, dkf, dvf, dgd, dgf, dlogf, proj, tables, bf_pad, scale)

    dwin_p = _mm_tn(hm, dproj, BF, "w_in_dw")
    dwin_full = jnp.concatenate([dwin_p[:, :6 * hd], dwin_p[:, 6 * hd + 2 * d:6 * hd + 2 * d + n_f],
                                 dwin_p[:, 6 * hd:6 * hd + 2 * d]], axis=1)
    dwin_c = dwin_full.reshape(d, N_DEV, cols).transpose(1, 0, 2)
    rs_win, tok = _exchange_start([dwin_c], False, "rs_start_w_in")
    dhm = _mm_nt(dproj, win_p, F32, "w_in_bwd", tn_pref=2048, tk_pref=1152)
    dx1, dx1b, d_mix_norm = _rms_bwd(dhm, x1, mix_norm, dx2, "rms_mix_bwd", dep=tok)

    dhn1, rs_ffn1 = ffn_bwd(dx1b, hn1, g1, u1, a1, wg1, wu1, wd1, "ffn1")
    grad_x, _, d_ffn1_norm = _rms_bwd(dhn1, x2d, ffn1_norm, dx1, "rms_ffn1_bwd")

    def update(handles, names, after, tag):
        recvs = _exchange_wait(handles, False, after, "rs_wait_" + tag)
        res = {}
        for recv, n in zip(recvs, names):
            w, m, v = wmv[n]
            g, delta, m2, v2 = _adam_from_partials(recv, w[0], m[0], v[0], "adam_" + n)
            res[n] = (g[None], delta[None], m2[None], v2[None])
        return res, g

    wmv = {
        "ffn1_w_gate": (ffn1_w_gate, m_ffn1_w_gate, v_ffn1_w_gate),
        "ffn1_w_up": (ffn1_w_up, m_ffn1_w_up, v_ffn1_w_up),
        "ffn1_w_down": (ffn1_w_down, m_ffn1_w_down, v_ffn1_w_down),
        "w_in": (w_in, m_w_in, v_w_in),
        "w_proj_dil": (w_proj_dil, m_w_proj_dil, v_w_proj_dil),
        "w_proj_fox": (w_proj_fox, m_w_proj_fox, v_w_proj_fox),
        "w_out": (w_out, m_w_out, v_w_out),
        "ffn2_w_gate": (ffn2_w_gate, m_ffn2_w_gate, v_ffn2_w_gate),
        "ffn2_w_up": (ffn2_w_up, m_ffn2_w_up, v_ffn2_w_up),
        "ffn2_w_down": (ffn2_w_down, m_ffn2_w_down, v_ffn2_w_down),
    }
    big = {}
    after = grad_x
    for handles, names, tag in [
            (rs_ffn2, ["ffn2_w_gate", "ffn2_w_up", "ffn2_w_down"], "ffn2"),
            (rs_mix, ["w_out", "w_proj_dil", "w_proj_fox"], "mixer"),
            (rs_win, ["w_in"], "w_in"),
            (rs_ffn1, ["ffn1_w_gate", "ffn1_w_up", "ffn1_w_down"], "ffn1")]:
        res, after = update(handles, names, after, tag)
        big.update(res)

    def lanes(a):
        a = a.reshape(1, -1)
        return jnp.pad(a, ((0, 0), (0, d - a.shape[1])))

    small_names = ["ffn1_norm", "mix_norm", "b_gate_dil", "b_gate_fox", "ffn2_norm", "final_norm", "b_forget"]
    small_g = [d_ffn1_norm, d_mix_norm, d_bd, d_bf, d_ffn2_norm, d_final, d_bforget[:, :n_f]]
    small_w = [ffn1_norm, mix_norm, b_gate_dil, b_gate_fox, ffn2_norm, final_norm, b_forget]
    small_m = [m_ffn1_norm, m_mix_norm, m_b_gate_dil, m_b_gate_fox, m_ffn2_norm, m_final_norm, m_b_forget]
    small_v = [v_ffn1_norm, v_mix_norm, v_b_gate_dil, v_b_gate_fox, v_ffn2_norm, v_final_norm, v_b_forget]
    pack = lambda arrs, last: jnp.concatenate([lanes(a) for a in arrs] + [last], axis=0)
    g_all = _allreduce_small(pack(small_g, loss_lanes))
    zero_row = jnp.zeros((1, d), F32)
    one_row = jnp.ones((1, d), F32)
    s_delta, s_m, s_v = _adam_small(g_all, pack(small_w, zero_row), pack(small_m, zero_row), pack(small_v, one_row))
    loss = g_all[len(small_names), 0]

    def unpack(packed, i, like):
        return packed[i, :like.size].reshape(like.shape)

    small = {}
    for i, (n, w) in enumerate(zip(small_names, small_w)):
        small[n] = (unpack(g_all, i, w), unpack(s_delta, i, w), unpack(s_m, i, w), unpack(s_v, i, w))

    order = ["ffn1_norm", "ffn1_w_gate", "ffn1_w_up", "ffn1_w_down", "mix_norm", "w_in", "b_forget", "b_gate_dil",
             "b_gate_fox", "w_proj_dil", "w_proj_fox", "w_out", "ffn2_norm", "ffn2_w_gate", "ffn2_w_up",
             "ffn2_w_down", "final_norm"]
    res = {**big, **small}
    outs = [loss, grad_x[None]]
    for slot in range(4):
        outs += [res[n][slot] for n in order]
    return tuple(outs)
```

```python
import functools

import numpy as np
import jax
import jax.numpy as jnp
from jax import lax
from jax.experimental import pallas as pl
from jax.experimental.pallas import tpu as pltpu

BF = jnp.bfloat16
F32 = jnp.float32
MESH = pl.DeviceIdType.MESH
N_DEV = 8

HEAD_DIM = 128
ROPE_DIM = HEAD_DIM // 4
ROPE_HALF = ROPE_DIM // 2
ROPE_THETA = 500000.0
NORM_EPS = 1e-6
DIL_PATTERNS = ((128, 1), (512, 4), (2048, 16))
MAX_WINDOW = 2048
LANE = 128
NEG = -1e30

ADAM_LR = 0.001
ADAM_B1 = 0.9
ADAM_B2 = 0.999
ADAM_EPS = 1e-08
ADAM_WD = 0.01
ADAM_STEP = 10

VMEM_LIMIT_BYTES = 56 * 1024 * 1024
ANY = pl.BlockSpec(memory_space=pl.ANY)

NN = (((1,), (0,)), ((), ()))
NT = (((1,), (1,)), ((), ()))
TN = (((0,), (0,)), ((), ()))


def _dot(a, b, dn=NN):
    return lax.dot_general(a, b, dn, preferred_element_type=F32)


def _sig(x):
    return 1.0 / (1.0 + jnp.exp(-x))


def _tile(n, pref, align):
    best = None
    t = align
    while t <= min(n, pref):
        if n % t == 0:
            best = t
        t += align
    return n if best is None else best


def _params():
    return pltpu.CompilerParams(vmem_limit_bytes=VMEM_LIMIT_BYTES)


def _call(body, args, dep=None, **kw):
    if dep is not None:
        n_in = len(args)
        inner = body

        def body(*refs):
            inner(*refs[:n_in], *refs[n_in + 1:])

        kw["in_specs"] = list(kw["in_specs"]) + [ANY]
        args = list(args) + [dep]
    return pl.pallas_call(body, **kw)(*args)


def _peers():
    x, y, c = lax.axis_index("x"), lax.axis_index("y"), lax.axis_index("c")
    me = 4 * x + 2 * y + c
    peers = []
    for k in range(1, N_DEV):
        px = 1 - x if (k >> 2) & 1 else x
        py = 1 - y if (k >> 1) & 1 else y
        pc = 1 - c if k & 1 else c
        peers.append((k, (px, py, pc), 4 * px + 2 * py + pc))
    return me, peers


HBM = pl.BlockSpec(memory_space=pltpu.HBM)
SEM = pl.BlockSpec(memory_space=pltpu.SEMAPHORE)
EFFECT = pltpu.SideEffectType.DATAFLOW_SIDE_EFFECTING


def _exchange_copy(gather, src_ref, land_ref, send_sems, recv_sems, me, k, peer, peer_flat, landing):
    return pltpu.make_async_remote_copy(
        src_ref=src_ref if gather else src_ref.at[peer_flat], dst_ref=land_ref.at[landing],
        send_sem=send_sems.at[k], recv_sem=recv_sems.at[k], device_id=peer, device_id_type=MESH)


def _exchange_start(srcs, gather, name):
    n = len(srcs)

    def body(*refs):
        src_refs, land_refs = refs[:n], refs[n:2 * n]
        send_refs, recv_refs = refs[2 * n:3 * n], refs[3 * n:4 * n]
        token, local_sems = refs[6 * n], refs[6 * n + 1]
        me, peers = _peers()
        local = [pltpu.make_async_copy(src_refs[i] if gather else src_refs[i].at[me], land_refs[i].at[me],
                                       local_sems.at[i]) for i in range(n)]
        for cp in local:
            cp.start()
        for i in range(n):
            for k, peer, peer_flat in peers:
                _exchange_copy(gather, src_refs[i], land_refs[i], send_refs[i], recv_refs[i],
                               me, k, peer, peer_flat, me).start()
        for cp in local:
            cp.wait()
        token[...] = jnp.zeros_like(token)

    lands = [lax.empty((N_DEV,) + s.shape[-2:], s.dtype) for s in srcs]
    sems = [pltpu.SemaphoreType.DMA((N_DEV,)) for _ in range(2 * n)]
    out = pl.pallas_call(
        body, name=name,
        out_shape=tuple(sems) + tuple(pltpu.HBM(a.shape, a.dtype) for a in list(srcs) + lands)
        + (jax.ShapeDtypeStruct((8, LANE), F32),),
        in_specs=[HBM] * (2 * n),
        out_specs=tuple([SEM] * (2 * n) + [HBM] * (2 * n) + [pl.BlockSpec(memory_space=pltpu.VMEM)]),
        input_output_aliases={i: 2 * n + i for i in range(2 * n)},
        scratch_shapes=[pltpu.SemaphoreType.DMA((n,))],
        compiler_params=pltpu.CompilerParams(has_side_effects=EFFECT),
    )(*[pltpu.with_memory_space_constraint(a, pltpu.HBM) for a in list(srcs) + lands])
    handles = [(out[2 * n + i], out[3 * n + i], out[i], out[n + i]) for i in range(n)]
    return handles, out[4 * n]


def _exchange_wait(handles, gather, after, name):
    n = len(handles)

    def body(*refs):
        src_refs, land_refs = refs[:n], refs[n:2 * n]
        send_refs, recv_refs = refs[2 * n:3 * n], refs[3 * n:4 * n]
        me, peers = _peers()
        for i in range(n):
            for k, peer, peer_flat in peers:
                cp = _exchange_copy(gather, src_refs[i], land_refs[i], send_refs[i], recv_refs[i],
                                    me, k, peer, peer_flat, peer_flat)
                cp.wait_send()
                cp.wait_recv()

    srcs = [h[0] for h in handles]
    lands = [h[1] for h in handles]
    out = pl.pallas_call(
        body, name=name,
        out_shape=tuple(pltpu.HBM(a.shape, a.dtype) for a in srcs + lands),
        in_specs=[HBM] * (2 * n) + [SEM] * (2 * n) + [ANY],
        out_specs=tuple([HBM] * (2 * n)),
        input_output_aliases={i: i for i in range(2 * n)},
        compiler_params=pltpu.CompilerParams(has_side_effects=EFFECT),
    )(*srcs, *lands, *[h[2] for h in handles], *[h[3] for h in handles], after)
    return list(out[n:])


def _allreduce_small(p):
    rows, d = p.shape

    def body(p_ref, o_ref, recv_ref, send_sems, recv_sems):
        me, peers = _peers()
        recv_ref[me] = p_ref[...]
        sends = []
        for k, peer, peer_flat in peers:
            cp = pltpu.make_async_remote_copy(
                src_ref=p_ref, dst_ref=recv_ref.at[me],
                send_sem=send_sems.at[k], recv_sem=recv_sems.at[k],
                device_id=peer, device_id_type=MESH)
            cp.start()
            sends.append(cp)
        for k, peer, peer_flat in peers:
            pltpu.make_async_remote_copy(
                src_ref=p_ref, dst_ref=recv_ref.at[peer_flat],
                send_sem=send_sems.at[k], recv_sem=recv_sems.at[k],
                device_id=peer, device_id_type=MESH).wait_recv()
        for cp in sends:
            cp.wait_send()
        acc = recv_ref[0]
        for s in range(1, N_DEV):
            acc = acc + recv_ref[s]
        is_loss = lax.broadcasted_iota(jnp.int32, (rows, d), 0) == rows - 1
        total = jnp.sum(jnp.where(is_loss, acc, 0.0))
        o_ref[...] = jnp.where(is_loss, total, acc)

    return pl.pallas_call(
        body, name="allreduce_small",
        out_shape=jax.ShapeDtypeStruct((rows, d), F32),
        in_specs=[pl.BlockSpec(memory_space=pltpu.VMEM)],
        out_specs=pl.BlockSpec(memory_space=pltpu.VMEM),
        scratch_shapes=[pltpu.VMEM((N_DEV, rows, d), F32),
                        pltpu.SemaphoreType.DMA((N_DEV,)), pltpu.SemaphoreType.DMA((N_DEV,))],
    )(p)


def _adam_math(w, g, m, v):
    m2 = ADAM_B1 * m + (1.0 - ADAM_B1) * g
    v2 = ADAM_B2 * v + (1.0 - ADAM_B2) * (g * g)
    m_hat = m2 / (1.0 - ADAM_B1 ** ADAM_STEP)
    v_hat = v2 / (1.0 - ADAM_B2 ** ADAM_STEP)
    delta = -ADAM_LR * (m_hat / (jnp.sqrt(v_hat) + ADAM_EPS) + ADAM_WD * w)
    return delta, m2, v2


def _adam_from_partials(parts, w, m, v, name):
    r, c = w.shape
    tr = _tile(r, 256, 16)

    def body(p_ref, w_ref, m_ref, v_ref, g_out, d_out, m_out, v_out):
        g = p_ref[0].astype(F32)
        for s in range(1, N_DEV):
            g = g + p_ref[s].astype(F32)
        delta, m2, v2 = _adam_math(w_ref[...], g, m_ref[...], v_ref[...])
        g_out[...] = g
        d_out[...] = delta
        m_out[...] = m2
        v_out[...] = v2

    blk = pl.BlockSpec((tr, c), lambda i: (i, 0))
    out = jax.ShapeDtypeStruct((r, c), F32)
    return pl.pallas_call(
        body, name=name, grid=(r // tr,),
        in_specs=[pl.BlockSpec((N_DEV, tr, c), lambda i: (0, i, 0)), blk, blk, blk],
        out_specs=[blk, blk, blk, blk], out_shape=[out, out, out, out],
        compiler_params=_params(),
    )(parts, w, m, v)


def _adam_small(g, w, m, v):
    def body(g_ref, w_ref, m_ref, v_ref, d_out, m_out, v_out):
        delta, m2, v2 = _adam_math(w_ref[...], g_ref[...], m_ref[...], v_ref[...])
        d_out[...] = delta
        m_out[...] = m2
        v_out[...] = v2

    out = jax.ShapeDtypeStruct(g.shape, F32)
    return pl.pallas_call(body, name="adam_small", out_shape=[out, out, out])(g, w, m, v)


def _rms_fwd(x, gain, name, dep=None):
    t, d = x.shape
    tr = _tile(t, 256, 16)

    def body(x_ref, g_ref, o_ref):
        xv = x_ref[...]
        r = lax.rsqrt(jnp.mean(xv * xv, axis=-1, keepdims=True) + NORM_EPS)
        o_ref[...] = (xv * r * g_ref[...]).astype(BF)

    return _call(
        body, [x, gain], dep=dep, name=name, grid=(t // tr,),
        in_specs=[pl.BlockSpec((tr, d), lambda i: (i, 0)), pl.BlockSpec((1, d), lambda i: (0, 0))],
        out_specs=pl.BlockSpec((tr, d), lambda i: (i, 0)),
        out_shape=jax.ShapeDtypeStruct((t, d), BF), compiler_params=_params(),
    )


def _rms_vjp(xv, gain, dy):
    r = lax.rsqrt(jnp.mean(xv * xv, axis=-1, keepdims=True) + NORM_EPS)
    xhat = xv * r
    dxhat = dy * gain
    dx = r * (dxhat - xhat * jnp.mean(dxhat * xhat, axis=-1, keepdims=True))
    dgain = jnp.sum(dy * xhat, axis=0, keepdims=True)
    return dx, dgain


def _rms_bwd(dy, x, gain, dres, name, dep=None):
    t, d = x.shape
    tr = _tile(t, 256, 16)

    def body(dy_ref, x_ref, g_ref, dres_ref, dx_ref, dxb_ref, dg_ref):
        dx, dgain = _rms_vjp(x_ref[...], g_ref[...], dy_ref[...])
        dx = dx + dres_ref[...]
        dx_ref[...] = dx
        dxb_ref[...] = dx.astype(BF)

        @pl.when(pl.program_id(0) == 0)
        def _():
            dg_ref[...] = jnp.zeros_like(dg_ref)

        dg_ref[...] += dgain

    row = pl.BlockSpec((tr, d), lambda i: (i, 0))
    vec = pl.BlockSpec((1, d), lambda i: (0, 0))
    return _call(
        body, [dy, x, gain, dres], dep=dep, name=name, grid=(t // tr,),
        in_specs=[row, row, vec, row], out_specs=[row, row, vec],
        out_shape=[jax.ShapeDtypeStruct((t, d), F32), jax.ShapeDtypeStruct((t, d), BF),
                   jax.ShapeDtypeStruct((1, d), F32)],
        compiler_params=_params(),
    )


def _loss_head(x, gain, target):
    t, d = x.shape
    tr = _tile(t, 256, 16)

    def body(x_ref, g_ref, t_ref, dx_ref, dxb_ref, dg_ref, loss_ref):
        xv = x_ref[...]
        gain = g_ref[...]
        r = lax.rsqrt(jnp.mean(xv * xv, axis=-1, keepdims=True) + NORM_EPS)
        err = xv * r * gain - t_ref[...]
        dx, dgain = _rms_vjp(xv, gain, err * (1.0 / d))
        dx_ref[...] = dx
        dxb_ref[...] = dx.astype(BF)

        @pl.when(pl.program_id(0) == 0)
        def _():
            dg_ref[...] = jnp.zeros_like(dg_ref)
            loss_ref[...] = jnp.zeros_like(loss_ref)

        dg_ref[...] += dgain
        loss_ref[...] += jnp.sum(err * err, axis=0, keepdims=True) * (0.5 / d)

    row = pl.BlockSpec((tr, d), lambda i: (i, 0))
    vec = pl.BlockSpec((1, d), lambda i: (0, 0))
    return pl.pallas_call(
        body, name="loss_head", grid=(t // tr,),
        in_specs=[row, vec, row], out_specs=[row, row, vec, vec],
        out_shape=[jax.ShapeDtypeStruct((t, d), F32), jax.ShapeDtypeStruct((t, d), BF),
                   jax.ShapeDtypeStruct((1, d), F32), jax.ShapeDtypeStruct((1, d), F32)],
        compiler_params=_params(),
    )(x, gain, target)


def _mm_nn(a, b, out_dtype, name, residual=None, tm_pref=512, tn_pref=1152):
    m, k = a.shape
    n = b.shape[1]
    tm, tn = _tile(m, tm_pref, 16), _tile(n, tn_pref, LANE)

    def body(*refs):
        if residual is None:
            a_ref, b_ref, o_ref = refs
            o_ref[...] = _dot(a_ref[...], b_ref[...]).astype(out_dtype)
        else:
            a_ref, b_ref, r_ref, o_ref = refs
            o_ref[...] = (r_ref[...] + _dot(a_ref[...], b_ref[...])).astype(out_dtype)

    in_specs = [pl.BlockSpec((tm, k), lambda j, i: (i, 0)), pl.BlockSpec((k, tn), lambda j, i: (0, j))]
    args = [a, b]
    if residual is not None:
        in_specs.append(pl.BlockSpec((tm, tn), lambda j, i: (i, j)))
        args.append(residual)
    return pl.pallas_call(
        body, name=name, grid=(n // tn, m // tm), in_specs=in_specs,
        out_specs=pl.BlockSpec((tm, tn), lambda j, i: (i, j)),
        out_shape=jax.ShapeDtypeStruct((m, n), out_dtype), compiler_params=_params(),
    )(*args)


def _mm_nt(a, b, out_dtype, name, tm_pref=512, tn_pref=1024, tk_pref=2048):
    m, k = a.shape
    n = b.shape[0]
    tm, tn, tk = _tile(m, tm_pref, 16), _tile(n, tn_pref, LANE), _tile(k, tk_pref, LANE)
    nk = k // tk

    def body(a_ref, b_ref, o_ref, acc_ref):
        kk = pl.program_id(2)

        @pl.when(kk == 0)
        def _():
            acc_ref[...] = jnp.zeros_like(acc_ref)

        acc_ref[...] += _dot(a_ref[...], b_ref[...], NT)

        @pl.when(kk == nk - 1)
        def _():
            o_ref[...] = acc_ref[...].astype(out_dtype)

    return pl.pallas_call(
        body, name=name, grid=(n // tn, m // tm, nk),
        in_specs=[pl.BlockSpec((tm, tk), lambda j, i, kk: (i, kk)),
                  pl.BlockSpec((tn, tk), lambda j, i, kk: (j, kk))],
        out_specs=pl.BlockSpec((tm, tn), lambda j, i, kk: (i, j)),
        out_shape=jax.ShapeDtypeStruct((m, n), out_dtype),
        scratch_shapes=[pltpu.VMEM((tm, tn), F32)], compiler_params=_params(),
    )(a, b)


def _mm_tn(a, b, out_dtype, name, tn_pref=1152, tk_pref=512):
    t, k = a.shape
    n = b.shape[1]
    tn, tk = _tile(n, tn_pref, LANE), _tile(t, tk_pref, 16)
    nt = t // tk

    def body(a_ref, b_ref, o_ref, acc_ref):
        tt = pl.program_id(1)

        @pl.when(tt == 0)
        def _():
            acc_ref[...] = jnp.zeros_like(acc_ref)

        acc_ref[...] += _dot(a_ref[...], b_ref[...], TN)

        @pl.when(tt == nt - 1)
        def _():
            o_ref[...] = acc_ref[...].astype(out_dtype)

    return pl.pallas_call(
        body, name=name, grid=(n // tn, nt),
        in_specs=[pl.BlockSpec((tk, k), lambda j, tt: (tt, 0)), pl.BlockSpec((tk, tn), lambda j, tt: (tt, j))],
        out_specs=pl.BlockSpec((k, tn), lambda j, tt: (0, j)),
        out_shape=jax.ShapeDtypeStruct((k, n), out_dtype),
        scratch_shapes=[pltpu.VMEM((k, tn), F32)], compiler_params=_params(),
    )(a, b)


def _ffn_gate_up(hn, wg, wu, name):
    t, d = hn.shape
    ns, _, f = wg.shape
    tm = _tile(t, 512, 16)

    def body(h_ref, wg_ref, wu_ref, g_ref, u_ref, a_ref):
        h = h_ref[...]
        g = _dot(h, wg_ref[...])
        u = _dot(h, wu_ref[...])
        g_ref[...] = g.astype(BF)
        u_ref[...] = u.astype(BF)
        a_ref[...] = (g * _sig(g) * u).astype(BF)

    wspec = pl.BlockSpec((None, d, f), lambda j, i: (j, 0, 0))
    hid = pl.BlockSpec((None, tm, f), lambda j, i: (j, i, 0))
    out = jax.ShapeDtypeStruct((ns, t, f), BF)
    return pl.pallas_call(
        body, name=name, grid=(ns, t // tm),
        in_specs=[pl.BlockSpec((tm, d), lambda j, i: (i, 0)), wspec, wspec],
        out_specs=[hid, hid, hid], out_shape=[out, out, out], compiler_params=_params(),
    )(hn, wg, wu)


def _ffn_down(act, wd, xres, name):
    ns, t, f = act.shape
    d = wd.shape[2]
    tm = _tile(t, 512, 16)

    def body(a_ref, w_ref, x_ref, o_ref):
        @pl.when(pl.program_id(1) == 0)
        def _():
            o_ref[...] = x_ref[...]

        o_ref[...] += 0.5 * _dot(a_ref[...], w_ref[...])

    row = pl.BlockSpec((tm, d), lambda i, j: (i, 0))
    return pl.pallas_call(
        body, name=name, grid=(t // tm, ns),
        in_specs=[pl.BlockSpec((None, tm, f), lambda i, j: (j, i, 0)),
                  pl.BlockSpec((None, f, d), lambda i, j: (j, 0, 0)), row],
        out_specs=row, out_shape=jax.ShapeDtypeStruct((t, d), F32), compiler_params=_params(),
    )(act, wd, xres)


def _ffn_bwd_hidden(dxb, wd, g, u, name):
    t, d = dxb.shape
    ns, f, _ = wd.shape
    tm = _tile(t, 512, 16)

    def body(dx_ref, w_ref, g_ref, u_ref, dg_ref, du_ref):
        dh = 0.5 * _dot(dx_ref[...], w_ref[...], NT)
        gv = g_ref[...].astype(F32)
        uv = u_ref[...].astype(F32)
        s = _sig(gv)
        dg_ref[...] = (dh * uv * (s * (1.0 + gv * (1.0 - s)))).astype(BF)
        du_ref[...] = (dh * (gv * s)).astype(BF)

    hid = pl.BlockSpec((None, tm, f), lambda j, i: (j, i, 0))
    out = jax.ShapeDtypeStruct((ns, t, f), BF)
    return pl.pallas_call(
        body, name=name, grid=(ns, t // tm),
        in_specs=[pl.BlockSpec((tm, d), lambda j, i: (i, 0)),
                  pl.BlockSpec((None, f, d), lambda j, i: (j, 0, 0)), hid, hid],
        out_specs=[hid, hid], out_shape=[out, out], compiler_params=_params(),
    )(dxb, wd, g, u)


def _ffn_dw_down(act, dxb, name):
    ns, t, f = act.shape
    d = dxb.shape[1]
    tk = _tile(t, 512, 16)
    nt = t // tk

    def body(a_ref, dx_ref, o_ref, acc_ref):
        tt = pl.program_id(1)

        @pl.when(tt == 0)
        def _():
            acc_ref[...] = jnp.zeros_like(acc_ref)

        acc_ref[...] += _dot(a_ref[...], dx_ref[...], TN)

        @pl.when(tt == nt - 1)
        def _():
            o_ref[...] = (0.5 * acc_ref[...]).astype(BF)

    return pl.pallas_call(
        body, name=name, grid=(ns, nt),
        in_specs=[pl.BlockSpec((None, tk, f), lambda j, tt: (j, tt, 0)),
                  pl.BlockSpec((tk, d), lambda j, tt: (tt, 0))],
        out_specs=pl.BlockSpec((None, f, d), lambda j, tt: (j, 0, 0)),
        out_shape=jax.ShapeDtypeStruct((ns, f, d), BF),
        scratch_shapes=[pltpu.VMEM((f, d), F32)], compiler_params=_params(),
    )(act, dxb)


def _ffn_dw_gate_up(hn, dg, du, name, dep=None):
    t, d = hn.shape
    ns, _, f = dg.shape
    tk = _tile(t, 512, 16)
    nt = t // tk

    def body(h_ref, dg_ref, du_ref, og_ref, ou_ref, accg_ref, accu_ref):
        tt = pl.program_id(1)

        @pl.when(tt == 0)
        def _():
            accg_ref[...] = jnp.zeros_like(accg_ref)
            accu_ref[...] = jnp.zeros_like(accu_ref)

        h = h_ref[...]
        accg_ref[...] += _dot(h, dg_ref[...], TN)
        accu_ref[...] += _dot(h, du_ref[...], TN)

        @pl.when(tt == nt - 1)
        def _():
            og_ref[...] = accg_ref[...].astype(BF)
            ou_ref[...] = accu_ref[...].astype(BF)

    hid = pl.BlockSpec((None, tk, f), lambda j, tt: (j, tt, 0))
    wspec = pl.BlockSpec((None, d, f), lambda j, tt: (j, 0, 0))
    out = jax.ShapeDtypeStruct((ns, d, f), BF)
    return _call(
        body, [hn, dg, du], dep=dep, name=name, grid=(ns, nt),
        in_specs=[pl.BlockSpec((tk, d), lambda j, tt: (tt, 0)), hid, hid],
        out_specs=[wspec, wspec], out_shape=[out, out],
        scratch_shapes=[pltpu.VMEM((d, f), F32), pltpu.VMEM((d, f), F32)], compiler_params=_params(),
    )


def _ffn_bwd_input(dg, du, wg, wu, name, dep=None):
    ns, t, f = dg.shape
    d = wg.shape[1]
    tm = _tile(t, 512, 16)

    def body(dg_ref, du_ref, wg_ref, wu_ref, o_ref):
        @pl.when(pl.program_id(1) == 0)
        def _():
            o_ref[...] = jnp.zeros_like(o_ref)

        o_ref[...] += _dot(dg_ref[...], wg_ref[...], NT) + _dot(du_ref[...], wu_ref[...], NT)

    hid = pl.BlockSpec((None, tm, f), lambda i, j: (j, i, 0))
    wspec = pl.BlockSpec((None, d, f), lambda i, j: (j, 0, 0))
    return _call(
        body, [dg, du, wg, wu], dep=dep, name=name, grid=(t // tm, ns),
        in_specs=[hid, hid, wspec, wspec],
        out_specs=pl.BlockSpec((tm, d), lambda i, j: (i, 0)),
        out_shape=jax.ShapeDtypeStruct((t, d), F32), compiler_params=_params(),
    )


def _rope_tables(t):
    pos = jnp.arange(t, dtype=F32)
    inv_freq = ROPE_THETA ** (-jnp.arange(0, ROPE_DIM, 2, dtype=F32) / ROPE_DIM)
    ang = pos[:, None] * inv_freq[None, :]
    cos, sin = jnp.cos(ang), jnp.sin(ang)
    rest = HEAD_DIM - ROPE_DIM
    one = jnp.ones((t, rest), F32)
    zero_h = jnp.zeros((t, ROPE_HALF), F32)
    zero_r = jnp.zeros((t, rest), F32)
    c = jnp.concatenate([cos, cos, one], axis=1)
    s1 = jnp.concatenate([-sin, zero_h, zero_r], axis=1)
    s2 = jnp.concatenate([zero_h, sin, zero_r], axis=1)
    return c, s1, s2


def _rope(xh, c, s1, s2):
    return xh * c + pltpu.roll(xh, HEAD_DIM - ROPE_HALF, 1) * s1 + pltpu.roll(xh, ROPE_HALF, 1) * s2


def _rope_t(dh, c, s1, s2):
    return dh * c + pltpu.roll(dh * s1, ROPE_HALF, 1) + pltpu.roll(dh * s2, HEAD_DIM - ROPE_HALF, 1)


def _mixer_prep(proj, tables, bf_pad, hd, scale):
    t, np_ = proj.shape
    tr = _tile(t, 256, 16)
    nh = hd // HEAD_DIM
    nblk = hd // LANE
    f_blk = np_ // LANE - 1

    def body(qd_ref, kd_ref, vd_ref, qf_ref, kf_ref, vf_ref, fl_ref, c_ref, s1_ref, s2_ref, b_ref,
             oqd, okd, ovd, oqf, okf, ovf, olog):
        c, s1, s2 = c_ref[...], s1_ref[...], s2_ref[...]
        for h in range(nh):
            sl = slice(h * HEAD_DIM, (h + 1) * HEAD_DIM)
            oqd[:, sl] = (_rope(qd_ref[:, sl], c, s1, s2) * scale).astype(BF)
            okd[:, sl] = _rope(kd_ref[:, sl], c, s1, s2).astype(BF)
        ovd[...] = vd_ref[...].astype(BF)
        oqf[...] = (qf_ref[...] * scale).astype(BF)
        okf[...] = kf_ref[...].astype(BF)
        ovf[...] = vf_ref[...].astype(BF)
        z = fl_ref[...] + b_ref[...]
        olog[...] = jnp.minimum(z, 0.0) - jnp.log(1.0 + jnp.exp(-jnp.abs(z)))

    def col(kblk):
        return pl.BlockSpec((tr, hd), lambda i, kblk=kblk: (i, kblk))

    lane_row = pl.BlockSpec((tr, LANE), lambda i: (i, 0))
    in_specs = [col(0), col(1), col(2), col(3), col(4), col(5),
                pl.BlockSpec((tr, LANE), lambda i: (i, f_blk)),
                lane_row, lane_row, lane_row, pl.BlockSpec((1, LANE), lambda i: (0, 0))]
    o = pl.BlockSpec((tr, hd), lambda i: (i, 0))
    ob = jax.ShapeDtypeStruct((t, hd), BF)
    del nblk
    return pl.pallas_call(
        body, name="mixer_prep", grid=(t // tr,), in_specs=in_specs,
        out_specs=[o, o, o, o, o, o, lane_row],
        out_shape=[ob, ob, ob, ob, ob, ob, jax.ShapeDtypeStruct((t, LANE), F32)],
        compiler_params=_params(),
    )(proj, proj, proj, proj, proj, proj, proj, *tables, bf_pad)


def _split3(x):
    x1 = x.astype(BF)
    r1 = x - x1.astype(F32)
    x2 = r1.astype(BF)
    x3 = (r1 - x2.astype(F32)).astype(BF)
    return x1, x2, x3


def _cumsum_rows(x, reverse, name):
    t, w = x.shape
    blk = LANE
    nb = t // blk

    def body(x_ref, o_ref):
        r = lax.broadcasted_iota(jnp.int32, (blk, blk), 0)
        c = lax.broadcasted_iota(jnp.int32, (blk, blk), 1)
        tri = jnp.where((c >= r) if reverse else (c <= r), 1.0, 0.0).astype(BF)

        def step(i, carry):
            b = (nb - 1 - i) if reverse else i
            off = pl.multiple_of(b * blk, blk)
            xb = x_ref[pl.ds(off, blk), :]
            x1, x2, x3 = _split3(xb)
            o_ref[pl.ds(off, blk), :] = _dot(tri, x1) + _dot(tri, x2) + _dot(tri, x3) + carry
            return carry + jnp.sum(xb, axis=0, keepdims=True)

        lax.fori_loop(0, nb, step, jnp.zeros((1, w), F32))

    return pl.pallas_call(body, name=name, out_shape=jax.ShapeDtypeStruct((t, w), F32),
                          compiler_params=_params())(x)


ATTN_ROWS = 16


def _dil_bias_tiles(tq):
    nbias = MAX_WINDOW // tq + 1
    b = lax.broadcasted_iota(jnp.int32, (nbias, tq, tq), 0)
    i = lax.broadcasted_iota(jnp.int32, (nbias, tq, tq), 1)
    j = lax.broadcasted_iota(jnp.int32, (nbias, tq, tq), 2)
    delta = b * tq + i - j
    mult = jnp.zeros((nbias, tq, tq), F32)
    for w, dil in DIL_PATTERNS:
        mult = mult + jnp.where((delta >= 0) & (delta <= w) & (delta % dil == 0), 1.0, 0.0)
    return jnp.where(mult > 0.0, jnp.log(jnp.maximum(mult, 1.0)), NEG)


def _rep(x, width):
    return jnp.tile(x, (1, width // LANE))


def _chunks(n_rows, fn):
    for c in range(n_rows // ATTN_ROWS):
        fn(c * ATTN_ROWS)


def _causal(r0, tq, transposed):
    a = lax.broadcasted_iota(jnp.int32, (ATTN_ROWS, tq), 0) + r0
    b = lax.broadcasted_iota(jnp.int32, (ATTN_ROWS, tq), 1)
    return (a <= b) if transposed else (b <= a)


def _attn_fwd(mode, q, k, v, bias, tq, name):
    t, hd = q.shape
    nh = hd // HEAD_DIM
    nb = t // tq
    wb = MAX_WINDOW // tq
    fox = mode == "fox"

    def body(q_ref, k_ref, v_ref, b_ref, o_ref, lse_ref, s_ref, p_ref, m_ref, l_ref, acc_ref):
        qi = pl.program_id(1)
        qb = q_ref[...]
        m_ref[...] = jnp.full_like(m_ref, NEG)
        l_ref[...] = jnp.zeros_like(l_ref)
        acc_ref[...] = jnp.zeros_like(acc_ref)

        def tile(kj, diag):
            off = pl.multiple_of(kj * tq, tq)
            s_ref[...] = _dot(qb, k_ref[pl.ds(off, tq), :], NT)
            if fox:
                brow = -b_ref[kj]

            def chunk(r0):
                rows = pl.ds(r0, ATTN_ROWS)
                if fox:
                    s = s_ref[rows, :] + brow
                    if diag:
                        s = jnp.where(_causal(r0, tq, False), s, NEG)
                else:
                    s = s_ref[rows, :] + b_ref[qi - kj, rows, :]
                m_old = m_ref[rows, :]
                m_new = jnp.maximum(m_old, jnp.max(s, axis=1, keepdims=True))
                p = jnp.exp(s - _rep(m_new, tq))
                alpha = jnp.exp(m_old - m_new)
                l_ref[rows, :] = alpha * l_ref[rows, :] + jnp.sum(p, axis=1, keepdims=True)
                m_ref[rows, :] = m_new
                acc_ref[rows, :] = alpha * acc_ref[rows, :]
                p_ref[rows, :] = p.astype(BF)

            _chunks(tq, chunk)
            acc_ref[...] += _dot(p_ref[...], v_ref[pl.ds(off, tq), :])

        tile(qi, True)
        if fox:
            lax.fori_loop(0, qi, lambda kj, c: (tile(kj, False), c)[1], 0)
        else:
            lax.fori_loop(1, jnp.minimum(qi, wb) + 1, lambda i, c: (tile(qi - i, False), c)[1], 0)
        o_ref[...] = (acc_ref[...] / l_ref[...]).astype(BF)
        lse_ref[...] = m_ref[...] + jnp.log(l_ref[...])

    qspec = pl.BlockSpec((tq, HEAD_DIM), lambda h, i: (i, h))
    kvspec = pl.BlockSpec((t, HEAD_DIM), lambda h, i: (0, h))
    repspec = pl.BlockSpec((None, tq, LANE), lambda h, i: (h, i, 0))
    if fox:
        bspec = pl.BlockSpec((None, nb, 1, tq), lambda h, i: (h, 0, 0, 0))
    else:
        bspec = pl.BlockSpec((wb + 1, tq, tq), lambda h, i: (0, 0, 0))
    return pl.pallas_call(
        body, name=name, grid=(nh, nb), in_specs=[qspec, kvspec, kvspec, bspec],
        out_specs=[qspec, repspec],
        out_shape=[jax.ShapeDtypeStruct((t, hd), BF), jax.ShapeDtypeStruct((nh, t, LANE), F32)],
        scratch_shapes=[pltpu.VMEM((tq, tq), F32), pltpu.VMEM((tq, tq), BF), pltpu.VMEM((tq, LANE), F32),
                        pltpu.VMEM((tq, LANE), F32), pltpu.VMEM((tq, HEAD_DIM), F32)],
        compiler_params=_params(),
    )(q, k, v, bias)


def _attn_bwd_dq(mode, q, k, v, o, do, lse, bias, tq, name, dep=None):
    t, hd = q.shape
    nh = hd // HEAD_DIM
    nb = t // tq
    wb = MAX_WINDOW // tq
    fox = mode == "fox"

    def body(q_ref, k_ref, v_ref, o_ref, do_ref, lse_ref, b_ref, dq_ref, dl_ref,
             s_ref, dp_ref, x_ref, y_ref, acc_ref, acc2_ref):
        qi = pl.program_id(1)
        qb = q_ref[...]
        dob = do_ref[...]
        acc_ref[...] = jnp.zeros_like(acc_ref)
        if fox:
            acc2_ref[...] = jnp.zeros_like(acc2_ref)
            dl_ref[...] = jnp.zeros_like(dl_ref)
        else:
            prod = o_ref[...].astype(F32) * dob.astype(F32)
            dl_ref[...] = jnp.broadcast_to(jnp.sum(prod, axis=1, keepdims=True), (tq, LANE))

        def tile(kj, diag):
            off = pl.multiple_of(kj * tq, tq)
            kb = k_ref[pl.ds(off, tq), :]
            s_ref[...] = _dot(qb, kb, NT)
            dp_ref[...] = _dot(dob, v_ref[pl.ds(off, tq), :], NT)
            if fox:
                brow = -b_ref[kj]

            def chunk(r0):
                rows = pl.ds(r0, ATTN_ROWS)
                lse_c = _rep(lse_ref[rows, :], tq)
                if fox:
                    s = s_ref[rows, :] + brow
                    if diag:
                        s = jnp.where(_causal(r0, tq, False), s, NEG)
                    p = jnp.exp(s - lse_c)
                    pdp = p * dp_ref[rows, :]
                    dl_ref[rows, :] += jnp.sum(pdp, axis=1, keepdims=True)
                    x_ref[rows, :] = pdp.astype(BF)
                    y_ref[rows, :] = p.astype(BF)
                else:
                    p = jnp.exp(s_ref[rows, :] + b_ref[qi - kj, rows, :] - lse_c)
                    x_ref[rows, :] = (p * (dp_ref[rows, :] - _rep(dl_ref[rows, :], tq))).astype(BF)

            _chunks(tq, chunk)
            acc_ref[...] += _dot(x_ref[...], kb)
            if fox:
                acc2_ref[...] += _dot(y_ref[...], kb)

        tile(qi, True)
        if fox:
            lax.fori_loop(0, qi, lambda kj, c: (tile(kj, False), c)[1], 0)
            dq_ref[...] = acc_ref[...] - dl_ref[...] * acc2_ref[...]
        else:
            lax.fori_loop(1, jnp.minimum(qi, wb) + 1, lambda i, c: (tile(qi - i, False), c)[1], 0)
            dq_ref[...] = acc_ref[...]

    qspec = pl.BlockSpec((tq, HEAD_DIM), lambda h, i: (i, h))
    kvspec = pl.BlockSpec((t, HEAD_DIM), lambda h, i: (0, h))
    repspec = pl.BlockSpec((None, tq, LANE), lambda h, i: (h, i, 0))
    if fox:
        bspec = pl.BlockSpec((None, nb, 1, tq), lambda h, i: (h, 0, 0, 0))
    else:
        bspec = pl.BlockSpec((wb + 1, tq, tq), lambda h, i: (0, 0, 0))
    return _call(
        body, [q, k, v, o, do, lse, bias], dep=dep, name=name, grid=(nh, nb),
        in_specs=[qspec, kvspec, kvspec, qspec, qspec, repspec, bspec],
        out_specs=[qspec, repspec],
        out_shape=[jax.ShapeDtypeStruct((t, hd), F32), jax.ShapeDtypeStruct((nh, t, LANE), F32)],
        scratch_shapes=[pltpu.VMEM((tq, tq), F32), pltpu.VMEM((tq, tq), F32), pltpu.VMEM((tq, tq), BF),
                        pltpu.VMEM((tq, tq), BF), pltpu.VMEM((tq, HEAD_DIM), F32),
                        pltpu.VMEM((tq, HEAD_DIM), F32)],
        compiler_params=_params(),
    )


def _attn_bwd_dkv(mode, q, k, v, do, lse_row, dl_row, bias_t, tq, name):
    t, hd = q.shape
    nh = hd // HEAD_DIM
    nb = t // tq
    wb = MAX_WINDOW // tq
    fox = mode == "fox"

    def body(*refs):
        if fox:
            (q_ref, k_ref, v_ref, do_ref, lse_ref, dl_ref, b_ref, dk_ref, dv_ref, dc_ref,
             s_ref, dp_ref, x_ref, y_ref) = refs
        else:
            q_ref, k_ref, v_ref, do_ref, lse_ref, dl_ref, b_ref, dk_ref, dv_ref, s_ref, dp_ref, x_ref, y_ref = refs
        kj = pl.program_id(1)
        kb = k_ref[...]
        vb = v_ref[...]
        dk_ref[...] = jnp.zeros_like(dk_ref)
        dv_ref[...] = jnp.zeros_like(dv_ref)
        if fox:
            dc_ref[...] = jnp.zeros_like(dc_ref)

        def tile(qi, diag):
            off = pl.multiple_of(qi * tq, tq)
            qb = q_ref[pl.ds(off, tq), :]
            dob = do_ref[pl.ds(off, tq), :]
            s_ref[...] = _dot(kb, qb, NT)
            dp_ref[...] = _dot(vb, dob, NT)
            lse_r = lse_ref[qi]
            dl_r = dl_ref[qi]

            def chunk(r0):
                rows = pl.ds(r0, ATTN_ROWS)
                if fox:
                    s = s_ref[rows, :] - _rep(b_ref[rows, :], tq)
                    if diag:
                        s = jnp.where(_causal(r0, tq, True), s, NEG)
                else:
                    s = s_ref[rows, :] + b_ref[qi - kj, rows, :]
                pt = jnp.exp(s - lse_r)
                dst = pt * (dp_ref[rows, :] - dl_r)
                x_ref[rows, :] = pt.astype(BF)
                y_ref[rows, :] = dst.astype(BF)
                if fox:
                    dc_ref[rows, :] -= jnp.sum(dst, axis=1, keepdims=True)

            _chunks(tq, chunk)
            dv_ref[...] += _dot(x_ref[...], dob)
            dk_ref[...] += _dot(y_ref[...], qb)

        tile(kj, True)
        hi = nb if fox else jnp.minimum(kj + wb + 1, nb)
        lax.fori_loop(kj + 1, hi, lambda qi, c: (tile(qi, False), c)[1], 0)

    blkspec = pl.BlockSpec((tq, HEAD_DIM), lambda h, j: (j, h))
    fullspec = pl.BlockSpec((t, HEAD_DIM), lambda h, j: (0, h))
    rowspec = pl.BlockSpec((None, nb, 1, tq), lambda h, j: (h, 0, 0, 0))
    repspec = pl.BlockSpec((None, tq, LANE), lambda h, j: (h, j, 0))
    bspec = repspec if fox else pl.BlockSpec((wb + 1, tq, tq), lambda h, j: (0, 0, 0))
    out_specs = [blkspec, blkspec]
    out_shape = [jax.ShapeDtypeStruct((t, hd), F32), jax.ShapeDtypeStruct((t, hd), F32)]
    if fox:
        out_specs.append(repspec)
        out_shape.append(jax.ShapeDtypeStruct((nh, t, LANE), F32))
    return pl.pallas_call(
        body, name=name, grid=(nh, nb),
        in_specs=[fullspec, blkspec, blkspec, fullspec, rowspec, rowspec, bspec],
        out_specs=out_specs, out_shape=out_shape,
        scratch_shapes=[pltpu.VMEM((tq, tq), F32), pltpu.VMEM((tq, tq), F32), pltpu.VMEM((tq, tq), BF),
                        pltpu.VMEM((tq, tq), BF)],
        compiler_params=_params(),
    )(q, k, v, do, lse_row, dl_row, bias_t)


def _gate_specs(t, d, hd, tr):
    row = pl.BlockSpec((tr, d), lambda i: (i, 0))
    vec = pl.BlockSpec((1, d), lambda i: (0, 0))
    base = 6 * hd // d
    gd = pl.BlockSpec((tr, d), lambda i: (i, base))
    gf = pl.BlockSpec((tr, d), lambda i: (i, base + 1))
    return row, vec, gd, gf


def _merge_fwd(pd, pf, proj, b_d, b_f, hd):
    t, d = pd.shape
    tr = _tile(t, 256, 16)
    row, vec, gd, gf = _gate_specs(t, d, hd, tr)

    def body(pd_ref, pf_ref, gd_ref, gf_ref, bd_ref, bf_ref, o_ref):
        o_ref[...] = (_sig(gd_ref[...] + bd_ref[...]) * pd_ref[...]
                      + _sig(gf_ref[...] + bf_ref[...]) * pf_ref[...]).astype(BF)

    return pl.pallas_call(
        body, name="merge_fwd", grid=(t // tr,), in_specs=[row, row, gd, gf, vec, vec],
        out_specs=row, out_shape=jax.ShapeDtypeStruct((t, d), BF), compiler_params=_params(),
    )(pd, pf, proj, proj, b_d, b_f)


def _merge_bwd(dm, pd, pf, proj, b_d, b_f, hd):
    t, d = pd.shape
    tr = _tile(t, 256, 16)
    row, vec, gd, gf = _gate_specs(t, d, hd, tr)

    def body(dm_ref, pd_ref, pf_ref, gd_ref, gf_ref, bd_ref, bf_ref,
             dpd_ref, dpf_ref, dgd_ref, dgf_ref, dbd_ref, dbf_ref):
        dmv = dm_ref[...]
        sd = _sig(gd_ref[...] + bd_ref[...])
        sf = _sig(gf_ref[...] + bf_ref[...])
        dgd = dmv * pd_ref[...] * (sd * (1.0 - sd))
        dgf = dmv * pf_ref[...] * (sf * (1.0 - sf))
        dpd_ref[...] = (dmv * sd).astype(BF)
        dpf_ref[...] = (dmv * sf).astype(BF)
        dgd_ref[...] = dgd.astype(BF)
        dgf_ref[...] = dgf.astype(BF)

        @pl.when(pl.program_id(0) == 0)
        def _():
            dbd_ref[...] = jnp.zeros_like(dbd_ref)
            dbf_ref[...] = jnp.zeros_like(dbf_ref)

        dbd_ref[...] += jnp.sum(dgd, axis=0, keepdims=True)
        dbf_ref[...] += jnp.sum(dgf, axis=0, keepdims=True)

    ob = jax.ShapeDtypeStruct((t, d), BF)
    ov = jax.ShapeDtypeStruct((1, d), F32)
    return pl.pallas_call(
        body, name="merge_bwd", grid=(t // tr,), in_specs=[row, row, row, gd, gf, vec, vec],
        out_specs=[row, row, row, row, vec, vec], out_shape=[ob, ob, ob, ob, ov, ov],
        compiler_params=_params(),
    )(dm, pd, pf, proj, proj, b_d, b_f)


def _assemble_dproj(dqd, dkd, dvd, dqf, dkf, dvf, dgd, dgf, dlogf, proj, tables, bf_pad, scale):
    t, np_ = proj.shape
    hd = dqd.shape[1]
    d = dgd.shape[1]
    nh = hd // HEAD_DIM
    tr = _tile(t, 256, 16)
    f_blk = np_ // LANE - 1

    def body(dqd_ref, dkd_ref, dvd_ref, dqf_ref, dkf_ref, dvf_ref, dgd_ref, dgf_ref, dlog_ref, fl_ref,
             c_ref, s1_ref, s2_ref, b_ref, o_ref, db_ref):
        c, s1, s2 = c_ref[...], s1_ref[...], s2_ref[...]
        for h in range(nh):
            sl = slice(h * HEAD_DIM, (h + 1) * HEAD_DIM)
            o_ref[:, sl] = (_rope_t(dqd_ref[:, sl], c, s1, s2) * scale).astype(BF)
            o_ref[:, hd + h * HEAD_DIM:hd + (h + 1) * HEAD_DIM] = _rope_t(dkd_ref[:, sl], c, s1, s2).astype(BF)
        o_ref[:, 2 * hd:3 * hd] = dvd_ref[...].astype(BF)
        o_ref[:, 3 * hd:4 * hd] = (dqf_ref[...] * scale).astype(BF)
        o_ref[:, 4 * hd:5 * hd] = dkf_ref[...].astype(BF)
        o_ref[:, 5 * hd:6 * hd] = dvf_ref[...].astype(BF)
        o_ref[:, 6 * hd:6 * hd + d] = dgd_ref[...]
        o_ref[:, 6 * hd + d:6 * hd + 2 * d] = dgf_ref[...]
        z = fl_ref[...] + b_ref[...]
        dfl = dlog_ref[...] * _sig(-z)
        o_ref[:, 6 * hd + 2 * d:] = dfl.astype(BF)

        @pl.when(pl.program_id(0) == 0)
        def _():
            db_ref[...] = jnp.zeros_like(db_ref)

        db_ref[...] += jnp.sum(dfl, axis=0, keepdims=True)

    head = pl.BlockSpec((tr, hd), lambda i: (i, 0))
    row = pl.BlockSpec((tr, d), lambda i: (i, 0))
    lane_row = pl.BlockSpec((tr, LANE), lambda i: (i, 0))
    lane_vec = pl.BlockSpec((1, LANE), lambda i: (0, 0))
    return pl.pallas_call(
        body, name="assemble_dproj", grid=(t // tr,),
        in_specs=[head] * 6 + [row, row, lane_row, pl.BlockSpec((tr, LANE), lambda i: (i, f_blk)),
                               lane_row, lane_row, lane_row, lane_vec],
        out_specs=[pl.BlockSpec((tr, np_), lambda i: (i, 0)), lane_vec],
        out_shape=[jax.ShapeDtypeStruct((t, np_), BF), jax.ShapeDtypeStruct((1, LANE), F32)],
        compiler_params=_params(),
    )(dqd, dkd, dvd, dqf, dkf, dvf, dgd, dgf, dlogf, proj, *tables, bf_pad)


def _to_rows(a, tq):
    h, t = a.shape
    return a.reshape(h, t // tq, 1, tq)


def kernel(x, ffn1_norm, ffn1_w_gate, ffn1_w_up, ffn1_w_down, mix_norm, w_in, b_forget, b_gate_dil, b_gate_fox, w_proj_dil, w_proj_fox, w_out, ffn2_norm, ffn2_w_gate, ffn2_w_up, ffn2_w_down, final_norm, loss_target, m_ffn1_norm, m_ffn1_w_gate, m_ffn1_w_up, m_ffn1_w_down, m_mix_norm, m_w_in, m_b_forget, m_b_gate_dil, m_b_gate_fox, m_w_proj_dil, m_w_proj_fox, m_w_out, m_ffn2_norm, m_ffn2_w_gate, m_ffn2_w_up, m_ffn2_w_down, m_final_norm, v_ffn1_norm, v_ffn1_w_gate, v_ffn1_w_up, v_ffn1_w_down, v_mix_norm, v_w_in, v_b_forget, v_b_gate_dil, v_b_gate_fox, v_w_proj_dil, v_w_proj_fox, v_w_out, v_ffn2_norm, v_ffn2_w_gate, v_ffn2_w_up, v_ffn2_w_down, v_final_norm):
    t, d = x.shape[1], x.shape[2]
    hd = w_proj_dil.shape[1]
    nh = hd // HEAD_DIM
    n_f = b_forget.shape[1]
    cols = w_in.shape[2]
    in_cols = N_DEV * cols
    assert in_cols == 6 * hd + n_f + 2 * d and n_f == nh and n_f <= LANE
    np_ = 6 * hd + 2 * d + LANE
    scale = HEAD_DIM ** -0.5
    tq = _tile(t, 512, LANE)
    assert MAX_WINDOW % tq == 0 and tq % 16 == 0

    x2d = x[0]
    tgt = loss_target[0]

    ag_order = [ffn1_w_gate, ffn1_w_up, ffn1_w_down, w_in, w_proj_dil, w_proj_fox, w_out,
                ffn2_w_gate, ffn2_w_up, ffn2_w_down]
    ag, ag_token = _exchange_start([w[0].astype(BF) for w in ag_order], True, "ag_start")

    def gathered(idx, after, name):
        return _exchange_wait([ag[i] for i in idx], True, after, name)

    tables = _rope_tables(t)
    bf_pad = jnp.pad(b_forget, ((0, 0), (0, LANE - n_f)))

    hn1 = _rms_fwd(x2d, ffn1_norm, "rms_ffn1", dep=ag_token)
    wg1, wu1 = gathered([0, 1], hn1, "ag_wait_ffn1_gate_up")
    g1, u1, a1 = _ffn_gate_up(hn1, wg1, wu1, "ffn1_gate_up")
    wd1, = gathered([2], a1, "ag_wait_ffn1_down")
    x1 = _ffn_down(a1, wd1, x2d, "ffn1_down")

    hm = _rms_fwd(x1, mix_norm, "rms_mix")
    win_g, = gathered([3], hm, "ag_wait_w_in")
    win_full = win_g.transpose(1, 0, 2).reshape(d, in_cols)
    win_p = jnp.concatenate([win_full[:, :6 * hd], win_full[:, 6 * hd + n_f:], win_full[:, 6 * hd:6 * hd + n_f],
                             jnp.zeros((d, LANE - n_f), BF)], axis=1)
    proj = _mm_nn(hm, win_p, F32, "w_in_fwd")
    qd, kd, vd, qf, kf, vf, logf = _mixer_prep(proj, tables, bf_pad, hd, scale)
    csum = _cumsum_rows(logf, False, "cumsum_logf")
    c_heads = csum[:, :nh].T
    c_row = _to_rows(c_heads, tq)
    c_rep = jnp.broadcast_to(c_heads[:, :, None], (nh, t, LANE))
    dil_bias = _dil_bias_tiles(tq)
    dil_bias_t = dil_bias.transpose(0, 2, 1)
    yd, lse_d = _attn_fwd("dil", qd, kd, vd, dil_bias, tq, "attn_dil_fwd")
    yf, lse_f = _attn_fwd("fox", qf, kf, vf, c_row, tq, "attn_fox_fwd")
    wpd_g, wpf_g = gathered([4, 5], yf, "ag_wait_proj")
    wpd = wpd_g.transpose(1, 0, 2).reshape(hd, d)
    wpf = wpf_g.transpose(1, 0, 2).reshape(hd, d)
    pd = _mm_nn(yd, wpd, F32, "proj_dil_fwd", tn_pref=1024)
    pf = _mm_nn(yf, wpf, F32, "proj_fox_fwd", tn_pref=1024)
    merged = _merge_fwd(pd, pf, proj, b_gate_dil, b_gate_fox, hd)
    wout_g, = gathered([6], merged, "ag_wait_w_out")
    wout = wout_g.reshape(d, d)
    x2 = _mm_nn(merged, wout, F32, "w_out_fwd", residual=x1, tn_pref=1024)

    hn2 = _rms_fwd(x2, ffn2_norm, "rms_ffn2")
    wg2, wu2 = gathered([7, 8], hn2, "ag_wait_ffn2_gate_up")
    g2, u2, a2 = _ffn_gate_up(hn2, wg2, wu2, "ffn2_gate_up")
    wd2, = gathered([9], a2, "ag_wait_ffn2_down")
    x3 = _ffn_down(a2, wd2, x2, "ffn2_down")

    dx3, dx3b, d_final, loss_lanes = _loss_head(x3, final_norm.reshape(1, d), tgt)

    def ffn_bwd(dxb, hn, g, u, a, wg, wu, wd, tag):
        dg, du = _ffn_bwd_hidden(dxb, wd, g, u, tag + "_bwd_hidden")
        dwd = _ffn_dw_down(a, dxb, tag + "_dw_down")
        rs_down, tok = _exchange_start([dwd], False, "rs_start_" + tag + "_down")
        dwg, dwu = _ffn_dw_gate_up(hn, dg, du, tag + "_dw_gate_up", dep=tok)
        rs_gu, tok = _exchange_start([dwg, dwu], False, "rs_start_" + tag + "_gate_up")
        dhn = _ffn_bwd_input(dg, du, wg, wu, tag + "_bwd_input", dep=tok)
        return dhn, rs_gu + rs_down

    dhn2, rs_ffn2 = ffn_bwd(dx3b, hn2, g2, u2, a2, wg2, wu2, wd2, "ffn2")
    dx2, dx2b, d_ffn2_norm = _rms_bwd(dhn2, x2, ffn2_norm, dx3, "rms_ffn2_bwd")

    dmerged = _mm_nt(dx2b, wout, F32, "w_out_bwd")
    dwout = _mm_tn(merged, dx2b, BF, "w_out_dw", tn_pref=1024)
    dpd, dpf, dgd, dgf, d_bd, d_bf = _merge_bwd(dmerged, pd, pf, proj, b_gate_dil, b_gate_fox, hd)
    dyd = _mm_nt(dpd, wpd, BF, "proj_dil_bwd")
    dyf = _mm_nt(dpf, wpf, BF, "proj_fox_bwd")
    dwpd = _mm_tn(yd, dpd, BF, "proj_dil_dw", tn_pref=1024)
    dwpf = _mm_tn(yf, dpf, BF, "proj_fox_dw", tn_pref=1024)
    dwpd_c = dwpd.reshape(hd, N_DEV, d // N_DEV).transpose(1, 0, 2)
    dwpf_c = dwpf.reshape(hd, N_DEV, d // N_DEV).transpose(1, 0, 2)
    dwout_c = dwout.reshape(N_DEV, d // N_DEV, d)
    rs_mix, tok = _exchange_start([dwout_c, dwpd_c, dwpf_c], False, "rs_start_mixer")

    dqd, dl_d = _attn_bwd_dq("dil", qd, kd, vd, yd, dyd, lse_d, dil_bias, tq, "attn_dil_dq", dep=tok)
    dkd, dvd = _attn_bwd_dkv("dil", qd, kd, vd, dyd, _to_rows(lse_d[:, :, 0], tq), _to_rows(dl_d[:, :, 0], tq),
                             dil_bias_t, tq, "attn_dil_dkv")
    dqf, dl_f = _attn_bwd_dq("fox", qf, kf, vf, yf, dyf, lse_f, c_row, tq, "attn_fox_dq")
    dkf, dvf, dc = _attn_bwd_dkv("fox", qf, kf, vf, dyf, _to_rows(lse_f[:, :, 0], tq), _to_rows(dl_f[:, :, 0], tq),
                                 c_rep, tq, "attn_fox_dkv")
    dc_pad = jnp.pad(dc[:, :, 0].T, ((0, 0), (0, LANE - nh)))
    dlogf = _cumsum_rows(dc_pad, True, "revcumsum_dc")
    dproj, d_bforget = _assemble_dproj(dqd, dkd, dvd, dqf, dkf, dvf, dgd, dgf, dlogf, proj, tables, bf_pad, scale)

    dwin_p = _mm_tn(hm, dproj, BF, "w_in_dw")
    dwin_full = jnp.concatenate([dwin_p[:, :6 * hd], dwin_p[:, 6 * hd + 2 * d:6 * hd + 2 * d + n_f],
                                 dwin_p[:, 6 * hd:6 * hd + 2 * d]], axis=1)
    dwin_c = dwin_full.reshape(d, N_DEV, cols).transpose(1, 0, 2)
    rs_win, tok = _exchange_start([dwin_c], False, "rs_start_w_in")
    dhm = _mm_nt(dproj, win_p, F32, "w_in_bwd", tn_pref=2048, tk_pref=1152)
    dx1, dx1b, d_mix_norm = _rms_bwd(dhm, x1, mix_norm, dx2, "rms_mix_bwd", dep=tok)

    dhn1, rs_ffn1 = ffn_bwd(dx1b, hn1, g1, u1, a1, wg1, wu1, wd1, "ffn1")
    grad_x, _, d_ffn1_norm = _rms_bwd(dhn1, x2d, ffn1_norm, dx1, "rms_ffn1_bwd")

    def update(handles, names, after, tag):
        recvs = _exchange_wait(handles, False, after, "rs_wait_" + tag)
        res = {}
        for recv, n in zip(recvs, names):
            w, m, v = wmv[n]
            g, delta, m2, v2 = _adam_from_partials(recv, w[0], m[0], v[0], "adam_" + n)
            res[n] = (g[None], delta[None], m2[None], v2[None])
        return res, g

    wmv = {
        "ffn1_w_gate": (ffn1_w_gate, m_ffn1_w_gate, v_ffn1_w_gate),
        "ffn1_w_up": (ffn1_w_up, m_ffn1_w_up, v_ffn1_w_up),
        "ffn1_w_down": (ffn1_w_down, m_ffn1_w_down, v_ffn1_w_down),
        "w_in": (w_in, m_w_in, v_w_in),
        "w_proj_dil": (w_proj_dil, m_w_proj_dil, v_w_proj_dil),
        "w_proj_fox": (w_proj_fox, m_w_proj_fox, v_w_proj_fox),
        "w_out": (w_out, m_w_out, v_w_out),
        "ffn2_w_gate": (ffn2_w_gate, m_ffn2_w_gate, v_ffn2_w_gate),
        "ffn2_w_up": (ffn2_w_up, m_ffn2_w_up, v_ffn2_w_up),
        "ffn2_w_down": (ffn2_w_down, m_ffn2_w_down, v_ffn2_w_down),
    }
    big = {}
    after = grad_x
    for handles, names, tag in [
            (rs_ffn2, ["ffn2_w_gate", "ffn2_w_up", "ffn2_w_down"], "ffn2"),
            (rs_mix, ["w_out", "w_proj_dil", "w_proj_fox"], "mixer"),
            (rs_win, ["w_in"], "w_in"),
            (rs_ffn1, ["ffn1_w_gate", "ffn1_w_up", "ffn1_w_down"], "ffn1")]:
        res, after = update(handles, names, after, tag)
        big.update(res)

    def lanes(a):
        a = a.reshape(1, -1)
        return jnp.pad(a, ((0, 0), (0, d - a.shape[1])))

    small_names = ["ffn1_norm", "mix_norm", "b_gate_dil", "b_gate_fox", "ffn2_norm", "final_norm", "b_forget"]
    small_g = [d_ffn1_norm, d_mix_norm, d_bd, d_bf, d_ffn2_norm, d_final, d_bforget[:, :n_f]]
    small_w = [ffn1_norm, mix_norm, b_gate_dil, b_gate_fox, ffn2_norm, final_norm, b_forget]
    small_m = [m_ffn1_norm, m_mix_norm, m_b_gate_dil, m_b_gate_fox, m_ffn2_norm, m_final_norm, m_b_forget]
    small_v = [v_ffn1_norm, v_mix_norm, v_b_gate_dil, v_b_gate_fox, v_ffn2_norm, v_final_norm, v_b_forget]
    pack = lambda arrs, last: jnp.concatenate([lanes(a) for a in arrs] + [last], axis=0)
    g_all = _allreduce_small(pack(small_g, loss_lanes))
    zero_row = jnp.zeros((1, d), F32)
    one_row = jnp.ones((1, d), F32)
    s_delta, s_m, s_v = _adam_small(g_all, pack(small_w, zero_row), pack(small_m, zero_row), pack(small_v, one_row))
    loss = g_all[len(small_names), 0]

    def unpack(packed, i, like):
        return packed[i, :like.size].reshape(like.shape)

    small = {}
    for i, (n, w) in enumerate(zip(small_names, small_w)):
        small[n] = (unpack(g_all, i, w), unpack(s_delta, i, w), unpack(s_m, i, w), unpack(s_v, i, w))

    order = ["ffn1_norm", "ffn1_w_gate", "ffn1_w_up", "ffn1_w_down", "mix_norm", "w_in", "b_forget", "b_gate_dil",
             "b_gate_fox", "w_proj_dil", "w_proj_fox", "w_out", "ffn2_norm", "ffn2_w_gate", "ffn2_w_up",
             "ffn2_w_down", "final_norm"]
    res = {**big, **small}
    outs = [loss, grad_x[None]]
    for slot in range(4):
        outs += [res[n][slot] for n in order]
    return tuple(outs)
```

```python
import functools

import numpy as np
import jax
import jax.numpy as jnp
from jax import lax
from jax.experimental import pallas as pl
from jax.experimental.pallas import tpu as pltpu

BF = jnp.bfloat16
F32 = jnp.float32
MESH = pl.DeviceIdType.MESH
N_DEV = 8

HEAD_DIM = 128
ROPE_DIM = HEAD_DIM // 4
ROPE_HALF = ROPE_DIM // 2
ROPE_THETA = 500000.0
NORM_EPS = 1e-6
DIL_PATTERNS = ((128, 1), (512, 4), (2048, 16))
MAX_WINDOW = 2048
LANE = 128
NEG = -1e30

ADAM_LR = 0.001
ADAM_B1 = 0.9
ADAM_B2 = 0.999
ADAM_EPS = 1e-08
ADAM_WD = 0.01
ADAM_STEP = 10

VMEM_LIMIT_BYTES = 56 * 1024 * 1024
ANY = pl.BlockSpec(memory_space=pl.ANY)

NN = (((1,), (0,)), ((), ()))
NT = (((1,), (1,)), ((), ()))
TN = (((0,), (0,)), ((), ()))


def _dot(a, b, dn=NN):
    return lax.dot_general(a, b, dn, preferred_element_type=F32)


def _sig(x):
    return 1.0 / (1.0 + jnp.exp(-x))


def _tile(n, pref, align):
    best = None
    t = align
    while t <= min(n, pref):
        if n % t == 0:
            best = t
        t += align
    return n if best is None else best


def _params():
    return pltpu.CompilerParams(vmem_limit_bytes=VMEM_LIMIT_BYTES)


def _call(body, args, dep=None, **kw):
    if dep is not None:
        n_in = len(args)
        inner = body

        def body(*refs):
            inner(*refs[:n_in], *refs[n_in + 1:])

        kw["in_specs"] = list(kw["in_specs"]) + [ANY]
        args = list(args) + [dep]
    return pl.pallas_call(body, **kw)(*args)


def _peers():
    x, y, c = lax.axis_index("x"), lax.axis_index("y"), lax.axis_index("c")
    me = 4 * x + 2 * y + c
    peers = []
    for k in range(1, N_DEV):
        px = 1 - x if (k >> 2) & 1 else x
        py = 1 - y if (k >> 1) & 1 else y
        pc = 1 - c if k & 1 else c
        peers.append((k, (px, py, pc), 4 * px + 2 * py + pc))
    return me, peers


HBM = pl.BlockSpec(memory_space=pltpu.HBM)
SEM = pl.BlockSpec(memory_space=pltpu.SEMAPHORE)
EFFECT = pltpu.SideEffectType.DATAFLOW_SIDE_EFFECTING


def _exchange_copy(gather, src_ref, land_ref, send_sems, recv_sems, me, k, peer, peer_flat, landing):
    return pltpu.make_async_remote_copy(
        src_ref=src_ref if gather else src_ref.at[peer_flat], dst_ref=land_ref.at[landing],
        send_sem=send_sems.at[k], recv_sem=recv_sems.at[k], device_id=peer, device_id_type=MESH)


def _exchange_start(srcs, gather, name):
    n = len(srcs)

    def body(*refs):
        src_refs, land_refs = refs[:n], refs[n:2 * n]
        send_refs, recv_refs = refs[2 * n:3 * n], refs[3 * n:4 * n]
        token, local_sems = refs[6 * n], refs[6 * n + 1]
        me, peers = _peers()
        local = [pltpu.make_async_copy(src_refs[i] if gather else src_refs[i].at[me], land_refs[i].at[me],
                                       local_sems.at[i]) for i in range(n)]
        for cp in local:
            cp.start()
        for i in range(n):
            for k, peer, peer_flat in peers:
                _exchange_copy(gather, src_refs[i], land_refs[i], send_refs[i], recv_refs[i],
                               me, k, peer, peer_flat, me).start()
        for cp in local:
            cp.wait()
        token[...] = jnp.zeros_like(token)

    lands = [lax.empty((N_DEV,) + s.shape[-2:], s.dtype) for s in srcs]
    sems = [pltpu.SemaphoreType.DMA((N_DEV,)) for _ in range(2 * n)]
    out = pl.pallas_call(
        body, name=name,
        out_shape=tuple(sems) + tuple(pltpu.HBM(a.shape, a.dtype) for a in list(srcs) + lands)
        + (jax.ShapeDtypeStruct((8, LANE), F32),),
        in_specs=[HBM] * (2 * n),
        out_specs=tuple([SEM] * (2 * n) + [HBM] * (2 * n) + [pl.BlockSpec(memory_space=pltpu.VMEM)]),
        input_output_aliases={i: 2 * n + i for i in range(2 * n)},
        scratch_shapes=[pltpu.SemaphoreType.DMA((n,))],
        compiler_params=pltpu.CompilerParams(has_side_effects=EFFECT),
    )(*[pltpu.with_memory_space_constraint(a, pltpu.HBM) for a in list(srcs) + lands])
    handles = [(out[2 * n + i], out[3 * n + i], out[i], out[n + i]) for i in range(n)]
    return handles, out[4 * n]


def _exchange_wait(handles, gather, after, name):
    n = len(handles)

    def body(*refs):
        src_refs, land_refs = refs[:n], refs[n:2 * n]
        send_refs, recv_refs = refs[2 * n:3 * n], refs[3 * n:4 * n]
        me, peers = _peers()
        for i in range(n):
            for k, peer, peer_flat in peers:
                cp = _exchange_copy(gather, src_refs[i], land_refs[i], send_refs[i], recv_refs[i],
                                    me, k, peer, peer_flat, peer_flat)
                cp.wait_send()
                cp.wait_recv()

    srcs = [h[0] for h in handles]
    lands = [h[1] for h in handles]
    out = pl.pallas_call(
        body, name=name,
        out_shape=tuple(pltpu.HBM(a.shape, a.dtype) for a in srcs + lands),
        in_specs=[HBM] * (2 * n) + [SEM] * (2 * n) + [ANY],
        out_specs=tuple([HBM] * (2 * n)),
        input_output_aliases={i: i for i in range(2 * n)},
        compiler_params=pltpu.CompilerParams(has_side_effects=EFFECT),
    )(*srcs, *lands, *[h[2] for h in handles], *[h[3] for h in handles], after)
    return list(out[n:])


def _allreduce_small(p):
    rows, d = p.shape

    def body(p_ref, o_ref, recv_ref, send_sems, recv_sems):
        me, peers = _peers()
        recv_ref[me] = p_ref[...]
        sends = []
        for k, peer, peer_flat in peers:
            cp = pltpu.make_async_remote_copy(
                src_ref=p_ref, dst_ref=recv_ref.at[me],
                send_sem=send_sems.at[k], recv_sem=recv_sems.at[k],
                device_id=peer, device_id_type=MESH)
            cp.start()
            sends.append(cp)
        for k, peer, peer_flat in peers:
            pltpu.make_async_remote_copy(
                src_ref=p_ref, dst_ref=recv_ref.at[peer_flat],
                send_sem=send_sems.at[k], recv_sem=recv_sems.at[k],
                device_id=peer, device_id_type=MESH).wait_recv()
        for cp in sends:
            cp.wait_send()
        acc = recv_ref[0]
        for s in range(1, N_DEV):
            acc = acc + recv_ref[s]
        is_loss = lax.broadcasted_iota(jnp.int32, (rows, d), 0) == rows - 1
        total = jnp.sum(jnp.where(is_loss, acc, 0.0))
        o_ref[...] = jnp.where(is_loss, total, acc)

    return pl.pallas_call(
        body, name="allreduce_small",
        out_shape=jax.ShapeDtypeStruct((rows, d), F32),
        in_specs=[pl.BlockSpec(memory_space=pltpu.VMEM)],
        out_specs=pl.BlockSpec(memory_space=pltpu.VMEM),
        scratch_shapes=[pltpu.VMEM((N_DEV, rows, d), F32),
                        pltpu.SemaphoreType.DMA((N_DEV,)), pltpu.SemaphoreType.DMA((N_DEV,))],
    )(p)


def _adam_math(w, g, m, v):
    m2 = ADAM_B1 * m + (1.0 - ADAM_B1) * g
    v2 = ADAM_B2 * v + (1.0 - ADAM_B2) * (g * g)
    m_hat = m2 / (1.0 - ADAM_B1 ** ADAM_STEP)
    v_hat = v2 / (1.0 - ADAM_B2 ** ADAM_STEP)
    delta = -ADAM_LR * (m_hat / (jnp.sqrt(v_hat) + ADAM_EPS) + ADAM_WD * w)
    return delta, m2, v2


def _adam_from_partials(parts, w, m, v, name):
    r, c = w.shape
    tr = _tile(r, 256, 16)

    def body(p_ref, w_ref, m_ref, v_ref, g_out, d_out, m_out, v_out):
        g = p_ref[0].astype(F32)
        for s in range(1, N_DEV):
            g = g + p_ref[s].astype(F32)
        delta, m2, v2 = _adam_math(w_ref[...], g, m_ref[...], v_ref[...])
        g_out[...] = g
        d_out[...] = delta
        m_out[...] = m2
        v_out[...] = v2

    blk = pl.BlockSpec((tr, c), lambda i: (i, 0))
    out = jax.ShapeDtypeStruct((r, c), F32)
    return pl.pallas_call(
        body, name=name, grid=(r // tr,),
        in_specs=[pl.BlockSpec((N_DEV, tr, c), lambda i: (0, i, 0)), blk, blk, blk],
        out_specs=[blk, blk, blk, blk], out_shape=[out, out, out, out],
        compiler_params=_params(),
    )(parts, w, m, v)


def _adam_small(g, w, m, v):
    def body(g_ref, w_ref, m_ref, v_ref, d_out, m_out, v_out):
        delta, m2, v2 = _adam_math(w_ref[...], g_ref[...], m_ref[...], v_ref[...])
        d_out[...] = delta
        m_out[...] = m2
        v_out[...] = v2

    out = jax.ShapeDtypeStruct(g.shape, F32)
    return pl.pallas_call(body, name="adam_small", out_shape=[out, out, out])(g, w, m, v)


def _rms_fwd(x, gain, name, dep=None):
    t, d = x.shape
    tr = _tile(t, 256, 16)

    def body(x_ref, g_ref, o_ref):
        xv = x_ref[...]
        r = lax.rsqrt(jnp.mean(xv * xv, axis=-1, keepdims=True) + NORM_EPS)
        o_ref[...] = (xv * r * g_ref[...]).astype(BF)

    return _call(
        body, [x, gain], dep=dep, name=name, grid=(t // tr,),
        in_specs=[pl.BlockSpec((tr, d), lambda i: (i, 0)), pl.BlockSpec((1, d), lambda i: (0, 0))],
        out_specs=pl.BlockSpec((tr, d), lambda i: (i, 0)),
        out_shape=jax.ShapeDtypeStruct((t, d), BF), compiler_params=_params(),
    )


def _rms_vjp(xv, gain, dy):
    r = lax.rsqrt(jnp.mean(xv * xv, axis=-1, keepdims=True) + NORM_EPS)
    xhat = xv * r
    dxhat = dy * gain
    dx = r * (dxhat - xhat * jnp.mean(dxhat * xhat, axis=-1, keepdims=True))
    dgain = jnp.sum(dy * xhat, axis=0, keepdims=True)
    return dx, dgain


def _rms_bwd(dy, x, gain, dres, name, dep=None):
    t, d = x.shape
    tr = _tile(t, 256, 16)

    def body(dy_ref, x_ref, g_ref, dres_ref, dx_ref, dxb_ref, dg_ref):
        dx, dgain = _rms_vjp(x_ref[...], g_ref[...], dy_ref[...])
        dx = dx + dres_ref[...]
        dx_ref[...] = dx
        dxb_ref[...] = dx.astype(BF)

        @pl.when(pl.program_id(0) == 0)
        def _():
            dg_ref[...] = jnp.zeros_like(dg_ref)

        dg_ref[...] += dgain

    row = pl.BlockSpec((tr, d), lambda i: (i, 0))
    vec = pl.BlockSpec((1, d), lambda i: (0, 0))
    return _call(
        body, [dy, x, gain, dres], dep=dep, name=name, grid=(t // tr,),
        in_specs=[row, row, vec, row], out_specs=[row, row, vec],
        out_shape=[jax.ShapeDtypeStruct((t, d), F32), jax.ShapeDtypeStruct((t, d), BF),
                   jax.ShapeDtypeStruct((1, d), F32)],
        compiler_params=_params(),
    )


def _loss_head(x, gain, target):
    t, d = x.shape
    tr = _tile(t, 256, 16)

    def body(x_ref, g_ref, t_ref, dx_ref, dxb_ref, dg_ref, loss_ref):
        xv = x_ref[...]
        gain = g_ref[...]
        r = lax.rsqrt(jnp.mean(xv * xv, axis=-1, keepdims=True) + NORM_EPS)
        err = xv * r * gain - t_ref[...]
        dx, dgain = _rms_vjp(xv, gain, err * (1.0 / d))
        dx_ref[...] = dx
        dxb_ref[...] = dx.astype(BF)

        @pl.when(pl.program_id(0) == 0)
        def _():
            dg_ref[...] = jnp.zeros_like(dg_ref)
            loss_ref[...] = jnp.zeros_like(loss_ref)

        dg_ref[...] += dgain
        loss_ref[...] += jnp.sum(err * err, axis=0, keepdims=True) * (0.5 / d)

    row = pl.BlockSpec((tr, d), lambda i: (i, 0))
    vec = pl.BlockSpec((1, d), lambda i: (0, 0))
    return pl.pallas_call(
        body, name="loss_head", grid=(t // tr,),
        in_specs=[row, vec, row], out_specs=[row, row, vec, vec],
        out_shape=[jax.ShapeDtypeStruct((t, d), F32), jax.ShapeDtypeStruct((t, d), BF),
                   jax.ShapeDtypeStruct((1, d), F32), jax.ShapeDtypeStruct((1, d), F32)],
        compiler_params=_params(),
    )(x, gain, target)


def _mm_nn(a, b, out_dtype, name, residual=None, tm_pref=512, tn_pref=1152):
    m, k = a.shape
    n = b.shape[1]
    tm, tn = _tile(m, tm_pref, 16), _tile(n, tn_pref, LANE)

    def body(*refs):
        if residual is None:
            a_ref, b_ref, o_ref = refs
            o_ref[...] = _dot(a_ref[...], b_ref[...]).astype(out_dtype)
        else:
            a_ref, b_ref, r_ref, o_ref = refs
            o_ref[...] = (r_ref[...] + _dot(a_ref[...], b_ref[...])).astype(out_dtype)

    in_specs = [pl.BlockSpec((tm, k), lambda j, i: (i, 0)), pl.BlockSpec((k, tn), lambda j, i: (0, j))]
    args = [a, b]
    if residual is not None:
        in_specs.append(pl.BlockSpec((tm, tn), lambda j, i: (i, j)))
        args.append(residual)
    return pl.pallas_call(
        body, name=name, grid=(n // tn, m // tm), in_specs=in_specs,
        out_specs=pl.BlockSpec((tm, tn), lambda j, i: (i, j)),
        out_shape=jax.ShapeDtypeStruct((m, n), out_dtype), compiler_params=_params(),
    )(*args)


def _mm_nt(a, b, out_dtype, name, tm_pref=512, tn_pref=1024, tk_pref=2048):
    m, k = a.shape
    n = b.shape[0]
    tm, tn, tk = _tile(m, tm_pref, 16), _tile(n, tn_pref, LANE), _tile(k, tk_pref, LANE)
    nk = k // tk

    def body(a_ref, b_ref, o_ref, acc_ref):
        kk = pl.program_id(2)

        @pl.when(kk == 0)
        def _():
            acc_ref[...] = jnp.zeros_like(acc_ref)

        acc_ref[...] += _dot(a_ref[...], b_ref[...], NT)

        @pl.when(kk == nk - 1)
        def _():
            o_ref[...] = acc_ref[...].astype(out_dtype)

    return pl.pallas_call(
        body, name=name, grid=(n // tn, m // tm, nk),
        in_specs=[pl.BlockSpec((tm, tk), lambda j, i, kk: (i, kk)),
                  pl.BlockSpec((tn, tk), lambda j, i, kk: (j, kk))],
        out_specs=pl.BlockSpec((tm, tn), lambda j, i, kk: (i, j)),
        out_shape=jax.ShapeDtypeStruct((m, n), out_dtype),
        scratch_shapes=[pltpu.VMEM((tm, tn), F32)], compiler_params=_params(),
    )(a, b)


def _mm_tn(a, b, out_dtype, name, tn_pref=1152, tk_pref=512):
    t, k = a.shape
    n = b.shape[1]
    tn, tk = _tile(n, tn_pref, LANE), _tile(t, tk_pref, 16)
    nt = t // tk

    def body(a_ref, b_ref, o_ref, acc_ref):
        tt = pl.program_id(1)

        @pl.when(tt == 0)
        def _():
            acc_ref[...] = jnp.zeros_like(acc_ref)

        acc_ref[...] += _dot(a_ref[...], b_ref[...], TN)

        @pl.when(tt == nt - 1)
        def _():
            o_ref[...] = acc_ref[...].astype(out_dtype)

    return pl.pallas_call(
        body, name=name, grid=(n // tn, nt),
        in_specs=[pl.BlockSpec((tk, k), lambda j, tt: (tt, 0)), pl.BlockSpec((tk, tn), lambda j, tt: (tt, j))],
        out_specs=pl.BlockSpec((k, tn), lambda j, tt: (0, j)),
        out_shape=jax.ShapeDtypeStruct((k, n), out_dtype),
        scratch_shapes=[pltpu.VMEM((k, tn), F32)], compiler_params=_params(),
    )(a, b)


def _ffn_gate_up(hn, wg, wu, name):
    t, d = hn.shape
    ns, _, f = wg.shape
    tm = _tile(t, 512, 16)

    def body(h_ref, wg_ref, wu_ref, g_ref, u_ref, a_ref):
        h = h_ref[...]
        g = _dot(h, wg_ref[...])
        u = _dot(h, wu_ref[...])
        g_ref[...] = g.astype(BF)
        u_ref[...] = u.astype(BF)
        a_ref[...] = (g * _sig(g) * u).astype(BF)

    wspec = pl.BlockSpec((None, d, f), lambda j, i: (j, 0, 0))
    hid = pl.BlockSpec((None, tm, f), lambda j, i: (j, i, 0))
    out = jax.ShapeDtypeStruct((ns, t, f), BF)
    return pl.pallas_call(
        body, name=name, grid=(ns, t // tm),
        in_specs=[pl.BlockSpec((tm, d), lambda j, i: (i, 0)), wspec, wspec],
        out_specs=[hid, hid, hid], out_shape=[out, out, out], compiler_params=_params(),
    )(hn, wg, wu)


def _ffn_down(act, wd, xres, name):
    ns, t, f = act.shape
    d = wd.shape[2]
    tm = _tile(t, 512, 16)

    def body(a_ref, w_ref, x_ref, o_ref):
        @pl.when(pl.program_id(1) == 0)
        def _():
            o_ref[...] = x_ref[...]

        o_ref[...] += 0.5 * _dot(a_ref[...], w_ref[...])

    row = pl.BlockSpec((tm, d), lambda i, j: (i, 0))
    return pl.pallas_call(
        body, name=name, grid=(t // tm, ns),
        in_specs=[pl.BlockSpec((None, tm, f), lambda i, j: (j, i, 0)),
                  pl.BlockSpec((None, f, d), lambda i, j: (j, 0, 0)), row],
        out_specs=row, out_shape=jax.ShapeDtypeStruct((t, d), F32), compiler_params=_params(),
    )(act, wd, xres)


def _ffn_bwd_hidden(dxb, wd, g, u, name):
    t, d = dxb.shape
    ns, f, _ = wd.shape
    tm = _tile(t, 512, 16)

    def body(dx_ref, w_ref, g_ref, u_ref, dg_ref, du_ref):
        dh = 0.5 * _dot(dx_ref[...], w_ref[...], NT)
        gv = g_ref[...].astype(F32)
        uv = u_ref[...].astype(F32)
        s = _sig(gv)
        dg_ref[...] = (dh * uv * (s * (1.0 + gv * (1.0 - s)))).astype(BF)
        du_ref[...] = (dh * (gv * s)).astype(BF)

    hid = pl.BlockSpec((None, tm, f), lambda j, i: (j, i, 0))
    out = jax.ShapeDtypeStruct((ns, t, f), BF)
    return pl.pallas_call(
        body, name=name, grid=(ns, t // tm),
        in_specs=[pl.BlockSpec((tm, d), lambda j, i: (i, 0)),
                  pl.BlockSpec((None, f, d), lambda j, i: (j, 0, 0)), hid, hid],
        out_specs=[hid, hid], out_shape=[out, out], compiler_params=_params(),
    )(dxb, wd, g, u)


def _ffn_dw_down(act, dxb, name):
    ns, t, f = act.shape
    d = dxb.shape[1]
    tk = _tile(t, 512, 16)
    nt = t // tk

    def body(a_ref, dx_ref, o_ref, acc_ref):
        tt = pl.program_id(1)

        @pl.when(tt == 0)
        def _():
            acc_ref[...] = jnp.zeros_like(acc_ref)

        acc_ref[...] += _dot(a_ref[...], dx_ref[...], TN)

        @pl.when(tt == nt - 1)
        def _():
            o_ref[...] = (0.5 * acc_ref[...]).astype(BF)

    return pl.pallas_call(
        body, name=name, grid=(ns, nt),
        in_specs=[pl.BlockSpec((None, tk, f), lambda j, tt: (j, tt, 0)),
                  pl.BlockSpec((tk, d), lambda j, tt: (tt, 0))],
        out_specs=pl.BlockSpec((None, f, d), lambda j, tt: (j, 0, 0)),
        out_shape=jax.ShapeDtypeStruct((ns, f, d), BF),
        scratch_shapes=[pltpu.VMEM((f, d), F32)], compiler_params=_params(),
    )(act, dxb)


def _ffn_dw_gate_up(hn, dg, du, name, dep=None):
    t, d = hn.shape
    ns, _, f = dg.shape
    tk = _tile(t, 512, 16)
    nt = t // tk

    def body(h_ref, dg_ref, du_ref, og_ref, ou_ref, accg_ref, accu_ref):
        tt = pl.program_id(1)

        @pl.when(tt == 0)
        def _():
            accg_ref[...] = jnp.zeros_like(accg_ref)
            accu_ref[...] = jnp.zeros_like(accu_ref)

        h = h_ref[...]
        accg_ref[...] += _dot(h, dg_ref[...], TN)
        accu_ref[...] += _dot(h, du_ref[...], TN)

        @pl.when(tt == nt - 1)
        def _():
            og_ref[...] = accg_ref[...].astype(BF)
            ou_ref[...] = accu_ref[...].astype(BF)

    hid = pl.BlockSpec((None, tk, f), lambda j, tt: (j, tt, 0))
    wspec = pl.BlockSpec((None, d, f), lambda j, tt: (j, 0, 0))
    out = jax.ShapeDtypeStruct((ns, d, f), BF)
    return _call(
        body, [hn, dg, du], dep=dep, name=name, grid=(ns, nt),
        in_specs=[pl.BlockSpec((tk, d), lambda j, tt: (tt, 0)), hid, hid],
        out_specs=[wspec, wspec], out_shape=[out, out],
        scratch_shapes=[pltpu.VMEM((d, f), F32), pltpu.VMEM((d, f), F32)], compiler_params=_params(),
    )


def _ffn_bwd_input(dg, du, wg, wu, name, dep=None):
    ns, t, f = dg.shape
    d = wg.shape[1]
    tm = _tile(t, 512, 16)

    def body(dg_ref, du_ref, wg_ref, wu_ref, o_ref):
        @pl.when(pl.program_id(1) == 0)
        def _():
            o_ref[...] = jnp.zeros_like(o_ref)

        o_ref[...] += _dot(dg_ref[...], wg_ref[...], NT) + _dot(du_ref[...], wu_ref[...], NT)

    hid = pl.BlockSpec((None, tm, f), lambda i, j: (j, i, 0))
    wspec = pl.BlockSpec((None, d, f), lambda i, j: (j, 0, 0))
    return _call(
        body, [dg, du, wg, wu], dep=dep, name=name, grid=(t // tm, ns),
        in_specs=[hid, hid, wspec, wspec],
        out_specs=pl.BlockSpec((tm, d), lambda i, j: (i, 0)),
        out_shape=jax.ShapeDtypeStruct((t, d), F32), compiler_params=_params(),
    )


def _rope_tables(t):
    pos = jnp.arange(t, dtype=F32)
    inv_freq = ROPE_THETA ** (-jnp.arange(0, ROPE_DIM, 2, dtype=F32) / ROPE_DIM)
    ang = pos[:, None] * inv_freq[None, :]
    cos, sin = jnp.cos(ang), jnp.sin(ang)
    rest = HEAD_DIM - ROPE_DIM
    one = jnp.ones((t, rest), F32)
    zero_h = jnp.zeros((t, ROPE_HALF), F32)
    zero_r = jnp.zeros((t, rest), F32)
    c = jnp.concatenate([cos, cos, one], axis=1)
    s1 = jnp.concatenate([-sin, zero_h, zero_r], axis=1)
    s2 = jnp.concatenate([zero_h, sin, zero_r], axis=1)
    return c, s1, s2


def _rope(xh, c, s1, s2):
    return xh * c + pltpu.roll(xh, HEAD_DIM - ROPE_HALF, 1) * s1 + pltpu.roll(xh, ROPE_HALF, 1) * s2


def _rope_t(dh, c, s1, s2):
    return dh * c + pltpu.roll(dh * s1, ROPE_HALF, 1) + pltpu.roll(dh * s2, HEAD_DIM - ROPE_HALF, 1)


def _mixer_prep(proj, tables, bf_pad, hd, scale):
    t, np_ = proj.shape
    tr = _tile(t, 256, 16)
    nh = hd // HEAD_DIM
    nblk = hd // LANE
    f_blk = np_ // LANE - 1

    def body(qd_ref, kd_ref, vd_ref, qf_ref, kf_ref, vf_ref, fl_ref, c_ref, s1_ref, s2_ref, b_ref,
             oqd, okd, ovd, oqf, okf, ovf, olog):
        c, s1, s2 = c_ref[...], s1_ref[...], s2_ref[...]
        for h in range(nh):
            sl = slice(h * HEAD_DIM, (h + 1) * HEAD_DIM)
            oqd[:, sl] = (_rope(qd_ref[:, sl], c, s1, s2) * scale).astype(BF)
            okd[:, sl] = _rope(kd_ref[:, sl], c, s1, s2).astype(BF)
        ovd[...] = vd_ref[...].astype(BF)
        oqf[...] = (qf_ref[...] * scale).astype(BF)
        okf[...] = kf_ref[...].astype(BF)
        ovf[...] = vf_ref[...].astype(BF)
        z = fl_ref[...] + b_ref[...]
        olog[...] = jnp.minimum(z, 0.0) - jnp.log(1.0 + jnp.exp(-jnp.abs(z)))

    def col(kblk):
        return pl.BlockSpec((tr, hd), lambda i, kblk=kblk: (i, kblk))

    lane_row = pl.BlockSpec((tr, LANE), lambda i: (i, 0))
    in_specs = [col(0), col(1), col(2), col(3), col(4), col(5),
                pl.BlockSpec((tr, LANE), lambda i: (i, f_blk)),
                lane_row, lane_row, lane_row, pl.BlockSpec((1, LANE), lambda i: (0, 0))]
    o = pl.BlockSpec((tr, hd), lambda i: (i, 0))
    ob = jax.ShapeDtypeStruct((t, hd), BF)
    del nblk
    return pl.pallas_call(
        body, name="mixer_prep", grid=(t // tr,), in_specs=in_specs,
        out_specs=[o, o, o, o, o, o, lane_row],
        out_shape=[ob, ob, ob, ob, ob, ob, jax.ShapeDtypeStruct((t, LANE), F32)],
        compiler_params=_params(),
    )(proj, proj, proj, proj, proj, proj, proj, *tables, bf_pad)


def _split3(x):
    x1 = x.astype(BF)
    r1 = x - x1.astype(F32)
    x2 = r1.astype(BF)
    x3 = (r1 - x2.astype(F32)).astype(BF)
    return x1, x2, x3


def _cumsum_rows(x, reverse, name):
    t, w = x.shape
    blk = LANE
    nb = t // blk

    def body(x_ref, o_ref):
        r = lax.broadcasted_iota(jnp.int32, (blk, blk), 0)
        c = lax.broadcasted_iota(jnp.int32, (blk, blk), 1)
        tri = jnp.where((c >= r) if reverse else (c <= r), 1.0, 0.0).astype(BF)

        def step(i, carry):
            b = (nb - 1 - i) if reverse else i
            off = pl.multiple_of(b * blk, blk)
            xb = x_ref[pl.ds(off, blk), :]
            x1, x2, x3 = _split3(xb)
            o_ref[pl.ds(off, blk), :] = _dot(tri, x1) + _dot(tri, x2) + _dot(tri, x3) + carry
            return carry + jnp.sum(xb, axis=0, keepdims=True)

        lax.fori_loop(0, nb, step, jnp.zeros((1, w), F32))

    return pl.pallas_call(body, name=name, out_shape=jax.ShapeDtypeStruct((t, w), F32),
                          compiler_params=_params())(x)


ATTN_ROWS = 16


def _dil_bias_tiles(tq):
    nbias = MAX_WINDOW // tq + 1
    b = lax.broadcasted_iota(jnp.int32, (nbias, tq, tq), 0)
    i = lax.broadcasted_iota(jnp.int32, (nbias, tq, tq), 1)
    j = lax.broadcasted_iota(jnp.int32, (nbias, tq, tq), 2)
    delta = b * tq + i - j
    mult = jnp.zeros((nbias, tq, tq), F32)
    for w, dil in DIL_PATTERNS:
        mult = mult + jnp.where((delta >= 0) & (delta <= w) & (delta % dil == 0), 1.0, 0.0)
    return jnp.where(mult > 0.0, jnp.log(jnp.maximum(mult, 1.0)), NEG)


def _rep(x, width):
    return jnp.tile(x, (1, width // LANE))


def _chunks(n_rows, fn):
    for c in range(n_rows // ATTN_ROWS):
        fn(c * ATTN_ROWS)


def _causal(r0, tq, transposed):
    a = lax.broadcasted_iota(jnp.int32, (ATTN_ROWS, tq), 0) + r0
    b = lax.broadcasted_iota(jnp.int32, (ATTN_ROWS, tq), 1)
    return (a <= b) if transposed else (b <= a)


def _rows8(x):
    return jnp.transpose(x)[:8, :]


def _attn_fwd(mode, q, k, v, bias, tq, name):
    t, hd = q.shape
    nh = hd // HEAD_DIM
    nb = t // tq
    wb = MAX_WINDOW // tq
    fox = mode == "fox"

    def body(q_ref, k_ref, v_ref, b_ref, o_ref, lse_ref, lse_row_ref, s_ref, p_ref, m_ref, l_ref, acc_ref):
        qi = pl.program_id(1)
        qb = q_ref[...]
        m_ref[...] = jnp.full_like(m_ref, NEG)
        l_ref[...] = jnp.zeros_like(l_ref)
        acc_ref[...] = jnp.zeros_like(acc_ref)

        def tile(kj, diag):
            off = pl.multiple_of(kj * tq, tq)
            s_ref[...] = _dot(qb, k_ref[pl.ds(off, tq), :], NT)
            if fox:
                brow = b_ref[qi][:, :1] - b_ref[kj]

            def chunk(r0):
                rows = pl.ds(r0, ATTN_ROWS)
                if fox:
                    s = s_ref[rows, :] + brow
                    if diag:
                        s = jnp.where(_causal(r0, tq, False), s, NEG)
                else:
                    s = s_ref[rows, :] + b_ref[qi - kj, rows, :]
                m_old = m_ref[rows, :]
                m_new = jnp.maximum(m_old, jnp.max(s, axis=1, keepdims=True))
                p = jnp.exp(s - _rep(m_new, tq))
                alpha = jnp.exp(m_old - m_new)
                l_ref[rows, :] = alpha * l_ref[rows, :] + jnp.sum(p, axis=1, keepdims=True)
                m_ref[rows, :] = m_new
                acc_ref[rows, :] = alpha * acc_ref[rows, :]
                p_ref[rows, :] = p.astype(BF)

            _chunks(tq, chunk)
            acc_ref[...] += _dot(p_ref[...], v_ref[pl.ds(off, tq), :])

        tile(qi, True)
        if fox:
            lax.fori_loop(0, qi, lambda kj, c: (tile(kj, False), c)[1], 0)
        else:
            lax.fori_loop(1, jnp.minimum(qi, wb) + 1, lambda i, c: (tile(qi - i, False), c)[1], 0)
        o_ref[...] = (acc_ref[...] / l_ref[...]).astype(BF)
        lse = m_ref[...] + jnp.log(l_ref[...])
        lse_ref[...] = lse
        lse_row_ref[...] = _rows8(lse)

    qspec = pl.BlockSpec((tq, HEAD_DIM), lambda h, i: (i, h))
    kvspec = pl.BlockSpec((t, HEAD_DIM), lambda h, i: (0, h))
    repspec = pl.BlockSpec((None, tq, LANE), lambda h, i: (h, i, 0))
    row8spec = pl.BlockSpec((None, None, 8, tq), lambda h, i: (h, i, 0, 0))
    if fox:
        bspec = pl.BlockSpec((None, nb, 1, tq), lambda h, i: (h, 0, 0, 0))
    else:
        bspec = pl.BlockSpec((wb + 1, tq, tq), lambda h, i: (0, 0, 0))
    return pl.pallas_call(
        body, name=name, grid=(nh, nb), in_specs=[qspec, kvspec, kvspec, bspec],
        out_specs=[qspec, repspec, row8spec],
        out_shape=[jax.ShapeDtypeStruct((t, hd), BF), jax.ShapeDtypeStruct((nh, t, LANE), F32),
                   jax.ShapeDtypeStruct((nh, nb, 8, tq), F32)],
        scratch_shapes=[pltpu.VMEM((tq, tq), F32), pltpu.VMEM((tq, tq), BF), pltpu.VMEM((tq, LANE), F32),
                        pltpu.VMEM((tq, LANE), F32), pltpu.VMEM((tq, HEAD_DIM), F32)],
        compiler_params=_params(),
    )(q, k, v, bias)


def _attn_bwd_dq(mode, q, k, v, o, do, lse, bias, tq, name, dep=None):
    t, hd = q.shape
    nh = hd // HEAD_DIM
    nb = t // tq
    wb = MAX_WINDOW // tq
    fox = mode == "fox"

    def body(q_ref, k_ref, v_ref, o_ref, do_ref, lse_ref, b_ref, dq_ref, dl_row_ref,
             s_ref, dp_ref, x_ref, y_ref, acc_ref, acc2_ref, dl_ref):
        qi = pl.program_id(1)
        qb = q_ref[...]
        dob = do_ref[...]
        acc_ref[...] = jnp.zeros_like(acc_ref)
        if fox:
            acc2_ref[...] = jnp.zeros_like(acc2_ref)
            dl_ref[...] = jnp.zeros_like(dl_ref)
        else:
            prod = o_ref[...].astype(F32) * dob.astype(F32)
            dl_ref[...] = jnp.broadcast_to(jnp.sum(prod, axis=1, keepdims=True), (tq, LANE))

        def tile(kj, diag):
            off = pl.multiple_of(kj * tq, tq)
            kb = k_ref[pl.ds(off, tq), :]
            s_ref[...] = _dot(qb, kb, NT)
            dp_ref[...] = _dot(dob, v_ref[pl.ds(off, tq), :], NT)
            if fox:
                brow = b_ref[qi][:, :1] - b_ref[kj]

            def chunk(r0):
                rows = pl.ds(r0, ATTN_ROWS)
                lse_c = _rep(lse_ref[rows, :], tq)
                if fox:
                    s = s_ref[rows, :] + brow
                    if diag:
                        s = jnp.where(_causal(r0, tq, False), s, NEG)
                    p = jnp.exp(s - lse_c)
                    pdp = p * dp_ref[rows, :]
                    dl_ref[rows, :] += jnp.sum(pdp, axis=1, keepdims=True)
                    x_ref[rows, :] = pdp.astype(BF)
                    y_ref[rows, :] = p.astype(BF)
                else:
                    p = jnp.exp(s_ref[rows, :] + b_ref[qi - kj, rows, :] - lse_c)
                    x_ref[rows, :] = (p * (dp_ref[rows, :] - _rep(dl_ref[rows, :], tq))).astype(BF)

            _chunks(tq, chunk)
            acc_ref[...] += _dot(x_ref[...], kb)
            if fox:
                acc2_ref[...] += _dot(y_ref[...], kb)

        tile(qi, True)
        if fox:
            lax.fori_loop(0, qi, lambda kj, c: (tile(kj, False), c)[1], 0)
            dq_ref[...] = acc_ref[...] - dl_ref[...] * acc2_ref[...]
        else:
            lax.fori_loop(1, jnp.minimum(qi, wb) + 1, lambda i, c: (tile(qi - i, False), c)[1], 0)
            dq_ref[...] = acc_ref[...]
        dl_row_ref[...] = _rows8(dl_ref[...])

    qspec = pl.BlockSpec((tq, HEAD_DIM), lambda h, i: (i, h))
    kvspec = pl.BlockSpec((t, HEAD_DIM), lambda h, i: (0, h))
    repspec = pl.BlockSpec((None, tq, LANE), lambda h, i: (h, i, 0))
    row8spec = pl.BlockSpec((None, None, 8, tq), lambda h, i: (h, i, 0, 0))
    if fox:
        bspec = pl.BlockSpec((None, nb, 1, tq), lambda h, i: (h, 0, 0, 0))
    else:
        bspec = pl.BlockSpec((wb + 1, tq, tq), lambda h, i: (0, 0, 0))
    return _call(
        body, [q, k, v, o, do, lse, bias], dep=dep, name=name, grid=(nh, nb),
        in_specs=[qspec, kvspec, kvspec, qspec, qspec, repspec, bspec],
        out_specs=[qspec, row8spec],
        out_shape=[jax.ShapeDtypeStruct((t, hd), F32), jax.ShapeDtypeStruct((nh, nb, 8, tq), F32)],
        scratch_shapes=[pltpu.VMEM((tq, tq), F32), pltpu.VMEM((tq, tq), F32), pltpu.VMEM((tq, tq), BF),
                        pltpu.VMEM((tq, tq), BF), pltpu.VMEM((tq, HEAD_DIM), F32),
                        pltpu.VMEM((tq, HEAD_DIM), F32), pltpu.VMEM((tq, LANE), F32)],
        compiler_params=_params(),
    )


def _attn_bwd_dkv(mode, q, k, v, do, lse_row, dl_row, bias_t, c_row, tq, name):
    t, hd = q.shape
    nh = hd // HEAD_DIM
    nb = t // tq
    wb = MAX_WINDOW // tq
    fox = mode == "fox"

    def body(*refs):
        if fox:
            (q_ref, k_ref, v_ref, do_ref, lse_ref, dl_ref, b_ref, cq_ref, dk_ref, dv_ref, dc_row_ref,
             s_ref, dp_ref, x_ref, y_ref, dc_ref) = refs
        else:
            q_ref, k_ref, v_ref, do_ref, lse_ref, dl_ref, b_ref, dk_ref, dv_ref, s_ref, dp_ref, x_ref, y_ref = refs
        kj = pl.program_id(1)
        kb = k_ref[...]
        vb = v_ref[...]
        dk_ref[...] = jnp.zeros_like(dk_ref)
        dv_ref[...] = jnp.zeros_like(dv_ref)
        if fox:
            dc_ref[...] = jnp.zeros_like(dc_ref)

        def tile(qi, diag):
            off = pl.multiple_of(qi * tq, tq)
            qb = q_ref[pl.ds(off, tq), :]
            dob = do_ref[pl.ds(off, tq), :]
            s_ref[...] = _dot(kb, qb, NT)
            dp_ref[...] = _dot(vb, dob, NT)
            lse_r = lse_ref[qi, 0:1, :]
            dl_r = dl_ref[qi, 0:1, :]
            if fox:
                kbias = cq_ref[qi][:, :1] - b_ref[...]

            def chunk(r0):
                rows = pl.ds(r0, ATTN_ROWS)
                if fox:
                    s = s_ref[rows, :] + _rep(kbias[r0:r0 + ATTN_ROWS, :], tq)
                    if diag:
                        s = jnp.where(_causal(r0, tq, True), s, NEG)
                else:
                    s = s_ref[rows, :] + b_ref[qi - kj, rows, :]
                pt = jnp.exp(s - lse_r)
                dst = pt * (dp_ref[rows, :] - dl_r)
                x_ref[rows, :] = pt.astype(BF)
                y_ref[rows, :] = dst.astype(BF)
                if fox:
                    dc_ref[rows, :] -= jnp.sum(dst, axis=1, keepdims=True)

            _chunks(tq, chunk)
            dv_ref[...] += _dot(x_ref[...], dob)
            dk_ref[...] += _dot(y_ref[...], qb)

        tile(kj, True)
        hi = nb if fox else jnp.minimum(kj + wb + 1, nb)
        lax.fori_loop(kj + 1, hi, lambda qi, c: (tile(qi, False), c)[1], 0)
        if fox:
            dc_row_ref[...] = _rows8(dc_ref[...])

    blkspec = pl.BlockSpec((tq, HEAD_DIM), lambda h, j: (j, h))
    fullspec = pl.BlockSpec((t, HEAD_DIM), lambda h, j: (0, h))
    rows8spec = pl.BlockSpec((None, nb, 8, tq), lambda h, j: (h, 0, 0, 0))
    repspec = pl.BlockSpec((None, tq, LANE), lambda h, j: (h, j, 0))
    in_specs = [fullspec, blkspec, blkspec, fullspec, rows8spec, rows8spec]
    args = [q, k, v, do, lse_row, dl_row, bias_t]
    out_specs = [blkspec, blkspec]
    out_shape = [jax.ShapeDtypeStruct((t, hd), F32), jax.ShapeDtypeStruct((t, hd), F32)]
    scratch = [pltpu.VMEM((tq, tq), F32), pltpu.VMEM((tq, tq), F32), pltpu.VMEM((tq, tq), BF),
               pltpu.VMEM((tq, tq), BF)]
    if fox:
        in_specs += [repspec, pl.BlockSpec((None, nb, 1, tq), lambda h, j: (h, 0, 0, 0))]
        args.append(c_row)
        out_specs.append(pl.BlockSpec((None, None, 8, tq), lambda h, j: (h, j, 0, 0)))
        out_shape.append(jax.ShapeDtypeStruct((nh, nb, 8, tq), F32))
        scratch.append(pltpu.VMEM((tq, LANE), F32))
    else:
        in_specs.append(pl.BlockSpec((wb + 1, tq, tq), lambda h, j: (0, 0, 0)))
    return pl.pallas_call(
        body, name=name, grid=(nh, nb), in_specs=in_specs, out_specs=out_specs, out_shape=out_shape,
        scratch_shapes=scratch, compiler_params=_params(),
    )(*args)


def _gate_specs(t, d, hd, tr):
    row = pl.BlockSpec((tr, d), lambda i: (i, 0))
    vec = pl.BlockSpec((1, d), lambda i: (0, 0))
    base = 6 * hd // d
    gd = pl.BlockSpec((tr, d), lambda i: (i, base))
    gf = pl.BlockSpec((tr, d), lambda i: (i, base + 1))
    return row, vec, gd, gf


def _merge_fwd(pd, pf, proj, b_d, b_f, hd):
    t, d = pd.shape
    tr = _tile(t, 256, 16)
    row, vec, gd, gf = _gate_specs(t, d, hd, tr)

    def body(pd_ref, pf_ref, gd_ref, gf_ref, bd_ref, bf_ref, o_ref):
        o_ref[...] = (_sig(gd_ref[...] + bd_ref[...]) * pd_ref[...]
                      + _sig(gf_ref[...] + bf_ref[...]) * pf_ref[...]).astype(BF)

    return pl.pallas_call(
        body, name="merge_fwd", grid=(t // tr,), in_specs=[row, row, gd, gf, vec, vec],
        out_specs=row, out_shape=jax.ShapeDtypeStruct((t, d), BF), compiler_params=_params(),
    )(pd, pf, proj, proj, b_d, b_f)


def _merge_bwd(dm, pd, pf, proj, b_d, b_f, hd):
    t, d = pd.shape
    tr = _tile(t, 256, 16)
    row, vec, gd, gf = _gate_specs(t, d, hd, tr)

    def body(dm_ref, pd_ref, pf_ref, gd_ref, gf_ref, bd_ref, bf_ref,
             dpd_ref, dpf_ref, dgd_ref, dgf_ref, dbd_ref, dbf_ref):
        dmv = dm_ref[...]
        sd = _sig(gd_ref[...] + bd_ref[...])
        sf = _sig(gf_ref[...] + bf_ref[...])
        dgd = dmv * pd_ref[...] * (sd * (1.0 - sd))
        dgf = dmv * pf_ref[...] * (sf * (1.0 - sf))
        dpd_ref[...] = (dmv * sd).astype(BF)
        dpf_ref[...] = (dmv * sf).astype(BF)
        dgd_ref[...] = dgd.astype(BF)
        dgf_ref[...] = dgf.astype(BF)

        @pl.when(pl.program_id(0) == 0)
        def _():
            dbd_ref[...] = jnp.zeros_like(dbd_ref)
            dbf_ref[...] = jnp.zeros_like(dbf_ref)

        dbd_ref[...] += jnp.sum(dgd, axis=0, keepdims=True)
        dbf_ref[...] += jnp.sum(dgf, axis=0, keepdims=True)

    ob = jax.ShapeDtypeStruct((t, d), BF)
    ov = jax.ShapeDtypeStruct((1, d), F32)
    return pl.pallas_call(
        body, name="merge_bwd", grid=(t // tr,), in_specs=[row, row, row, gd, gf, vec, vec],
        out_specs=[row, row, row, row, vec, vec], out_shape=[ob, ob, ob, ob, ov, ov],
        compiler_params=_params(),
    )(dm, pd, pf, proj, proj, b_d, b_f)


def _assemble_dproj(dqd, dkd, dvd, dqf, dkf, dvf, dgd, dgf, dlogf, proj, tables, bf_pad, scale):
    t, np_ = proj.shape
    hd = dqd.shape[1]
    d = dgd.shape[1]
    nh = hd // HEAD_DIM
    tr = _tile(t, 256, 16)
    f_blk = np_ // LANE - 1

    def body(dqd_ref, dkd_ref, dvd_ref, dqf_ref, dkf_ref, dvf_ref, dgd_ref, dgf_ref, dlog_ref, fl_ref,
             c_ref, s1_ref, s2_ref, b_ref, o_ref, db_ref):
        c, s1, s2 = c_ref[...], s1_ref[...], s2_ref[...]
        for h in range(nh):
            sl = slice(h * HEAD_DIM, (h + 1) * HEAD_DIM)
            o_ref[:, sl] = (_rope_t(dqd_ref[:, sl], c, s1, s2) * scale).astype(BF)
            o_ref[:, hd + h * HEAD_DIM:hd + (h + 1) * HEAD_DIM] = _rope_t(dkd_ref[:, sl], c, s1, s2).astype(BF)
        o_ref[:, 2 * hd:3 * hd] = dvd_ref[...].astype(BF)
        o_ref[:, 3 * hd:4 * hd] = (dqf_ref[...] * scale).astype(BF)
        o_ref[:, 4 * hd:5 * hd] = dkf_ref[...].astype(BF)
        o_ref[:, 5 * hd:6 * hd] = dvf_ref[...].astype(BF)
        o_ref[:, 6 * hd:6 * hd + d] = dgd_ref[...]
        o_ref[:, 6 * hd + d:6 * hd + 2 * d] = dgf_ref[...]
        z = fl_ref[...] + b_ref[...]
        dfl = dlog_ref[...] * _sig(-z)
        o_ref[:, 6 * hd + 2 * d:] = dfl.astype(BF)

        @pl.when(pl.program_id(0) == 0)
        def _():
            db_ref[...] = jnp.zeros_like(db_ref)

        db_ref[...] += jnp.sum(dfl, axis=0, keepdims=True)

    head = pl.BlockSpec((tr, hd), lambda i: (i, 0))
    row = pl.BlockSpec((tr, d), lambda i: (i, 0))
    lane_row = pl.BlockSpec((tr, LANE), lambda i: (i, 0))
    lane_vec = pl.BlockSpec((1, LANE), lambda i: (0, 0))
    return pl.pallas_call(
        body, name="assemble_dproj", grid=(t // tr,),
        in_specs=[head] * 6 + [row, row, lane_row, pl.BlockSpec((tr, LANE), lambda i: (i, f_blk)),
                               lane_row, lane_row, lane_row, lane_vec],
        out_specs=[pl.BlockSpec((tr, np_), lambda i: (i, 0)), lane_vec],
        out_shape=[jax.ShapeDtypeStruct((t, np_), BF), jax.ShapeDtypeStruct((1, LANE), F32)],
        compiler_params=_params(),
    )(dqd, dkd, dvd, dqf, dkf, dvf, dgd, dgf, dlogf, proj, *tables, bf_pad)


def _to_rows(a, tq):
    h, t = a.shape
    return a.reshape(h, t // tq, 1, tq)


def kernel(x, ffn1_norm, ffn1_w_gate, ffn1_w_up, ffn1_w_down, mix_norm, w_in, b_forget, b_gate_dil, b_gate_fox, w_proj_dil, w_proj_fox, w_out, ffn2_norm, ffn2_w_gate, ffn2_w_up, ffn2_w_down, final_norm, loss_target, m_ffn1_norm, m_ffn1_w_gate, m_ffn1_w_up, m_ffn1_w_down, m_mix_norm, m_w_in, m_b_forget, m_b_gate_dil, m_b_gate_fox, m_w_proj_dil, m_w_proj_fox, m_w_out, m_ffn2_norm, m_ffn2_w_gate, m_ffn2_w_up, m_ffn2_w_down, m_final_norm, v_ffn1_norm, v_ffn1_w_gate, v_ffn1_w_up, v_ffn1_w_down, v_mix_norm, v_w_in, v_b_forget, v_b_gate_dil, v_b_gate_fox, v_w_proj_dil, v_w_proj_fox, v_w_out, v_ffn2_norm, v_ffn2_w_gate, v_ffn2_w_up, v_ffn2_w_down, v_final_norm):
    t, d = x.shape[1], x.shape[2]
    hd = w_proj_dil.shape[1]
    nh = hd // HEAD_DIM
    n_f = b_forget.shape[1]
    cols = w_in.shape[2]
    in_cols = N_DEV * cols
    assert in_cols == 6 * hd + n_f + 2 * d and n_f == nh and n_f <= LANE
    np_ = 6 * hd + 2 * d + LANE
    scale = HEAD_DIM ** -0.5
    tq = _tile(t, 512, LANE)
    assert MAX_WINDOW % tq == 0 and tq % 16 == 0

    x2d = x[0]
    tgt = loss_target[0]

    ag_order = [ffn1_w_gate, ffn1_w_up, ffn1_w_down, w_in, w_proj_dil, w_proj_fox, w_out,
                ffn2_w_gate, ffn2_w_up, ffn2_w_down]
    ag, ag_token = _exchange_start([w[0].astype(BF) for w in ag_order], True, "ag_start")

    def gathered(idx, after, name):
        return _exchange_wait([ag[i] for i in idx], True, after, name)

    tables = _rope_tables(t)
    bf_pad = jnp.pad(b_forget, ((0, 0), (0, LANE - n_f)))

    hn1 = _rms_fwd(x2d, ffn1_norm, "rms_ffn1", dep=ag_token)
    wg1, wu1 = gathered([0, 1], hn1, "ag_wait_ffn1_gate_up")
    g1, u1, a1 = _ffn_gate_up(hn1, wg1, wu1, "ffn1_gate_up")
    wd1, = gathered([2], a1, "ag_wait_ffn1_down")
    x1 = _ffn_down(a1, wd1, x2d, "ffn1_down")

    hm = _rms_fwd(x1, mix_norm, "rms_mix")
    win_g, = gathered([3], hm, "ag_wait_w_in")
    win_full = win_g.transpose(1, 0, 2).reshape(d, in_cols)
    win_p = jnp.concatenate([win_full[:, :6 * hd], win_full[:, 6 * hd + n_f:], win_full[:, 6 * hd:6 * hd + n_f],
                             jnp.zeros((d, LANE - n_f), BF)], axis=1)
    proj = _mm_nn(hm, win_p, F32, "w_in_fwd")
    qd, kd, vd, qf, kf, vf, logf = _mixer_prep(proj, tables, bf_pad, hd, scale)
    csum = _cumsum_rows(logf, False, "cumsum_logf")
    c_heads = csum[:, :nh].T
    c_row = _to_rows(c_heads, tq)
    c_rep = jnp.broadcast_to(c_heads[:, :, None], (nh, t, LANE))
    dil_bias = _dil_bias_tiles(tq)
    dil_bias_t = dil_bias.transpose(0, 2, 1)
    yd, lse_d, lse_d_row = _attn_fwd("dil", qd, kd, vd, dil_bias, tq, "attn_dil_fwd")
    yf, lse_f, lse_f_row = _attn_fwd("fox", qf, kf, vf, c_row, tq, "attn_fox_fwd")
    wpd_g, wpf_g = gathered([4, 5], yf, "ag_wait_proj")
    wpd = wpd_g.transpose(1, 0, 2).reshape(hd, d)
    wpf = wpf_g.transpose(1, 0, 2).reshape(hd, d)
    pd = _mm_nn(yd, wpd, F32, "proj_dil_fwd", tn_pref=1024)
    pf = _mm_nn(yf, wpf, F32, "proj_fox_fwd", tn_pref=1024)
    merged = _merge_fwd(pd, pf, proj, b_gate_dil, b_gate_fox, hd)
    wout_g, = gathered([6], merged, "ag_wait_w_out")
    wout = wout_g.reshape(d, d)
    x2 = _mm_nn(merged, wout, F32, "w_out_fwd", residual=x1, tn_pref=1024)

    hn2 = _rms_fwd(x2, ffn2_norm, "rms_ffn2")
    wg2, wu2 = gathered([7, 8], hn2, "ag_wait_ffn2_gate_up")
    g2, u2, a2 = _ffn_gate_up(hn2, wg2, wu2, "ffn2_gate_up")
    wd2, = gathered([9], a2, "ag_wait_ffn2_down")
    x3 = _ffn_down(a2, wd2, x2, "ffn2_down")

    dx3, dx3b, d_final, loss_lanes = _loss_head(x3, final_norm.reshape(1, d), tgt)

    def ffn_bwd(dxb, hn, g, u, a, wg, wu, wd, tag):
        dg, du = _ffn_bwd_hidden(dxb, wd, g, u, tag + "_bwd_hidden")
        dwd = _ffn_dw_down(a, dxb, tag + "_dw_down")
        rs_down, tok = _exchange_start([dwd], False, "rs_start_" + tag + "_down")
        dwg, dwu = _ffn_dw_gate_up(hn, dg, du, tag + "_dw_gate_up", dep=tok)
        rs_gu, tok = _exchange_start([dwg, dwu], False, "rs_start_" + tag + "_gate_up")
        dhn = _ffn_bwd_input(dg, du, wg, wu, tag + "_bwd_input", dep=tok)
        return dhn, rs_gu + rs_down

    dhn2, rs_ffn2 = ffn_bwd(dx3b, hn2, g2, u2, a2, wg2, wu2, wd2, "ffn2")
    dx2, dx2b, d_ffn2_norm = _rms_bwd(dhn2, x2, ffn2_norm, dx3, "rms_ffn2_bwd")

    dmerged = _mm_nt(dx2b, wout, F32, "w_out_bwd")
    dwout = _mm_tn(merged, dx2b, BF, "w_out_dw", tn_pref=1024)
    dpd, dpf, dgd, dgf, d_bd, d_bf = _merge_bwd(dmerged, pd, pf, proj, b_gate_dil, b_gate_fox, hd)
    dyd = _mm_nt(dpd, wpd, BF, "proj_dil_bwd")
    dyf = _mm_nt(dpf, wpf, BF, "proj_fox_bwd")
    dwpd = _mm_tn(yd, dpd, BF, "proj_dil_dw", tn_pref=1024)
    dwpf = _mm_tn(yf, dpf, BF, "proj_fox_dw", tn_pref=1024)
    dwpd_c = dwpd.reshape(hd, N_DEV, d // N_DEV).transpose(1, 0, 2)
    dwpf_c = dwpf.reshape(hd, N_DEV, d // N_DEV).transpose(1, 0, 2)
    dwout_c = dwout.reshape(N_DEV, d // N_DEV, d)
    rs_mix, tok = _exchange_start([dwout_c, dwpd_c, dwpf_c], False, "rs_start_mixer")

    dqd, dl_d = _attn_bwd_dq("dil", qd, kd, vd, yd, dyd, lse_d, dil_bias, tq, "attn_dil_dq", dep=tok)
    dkd, dvd = _attn_bwd_dkv("dil", qd, kd, vd, dyd, lse_d_row, dl_d, dil_bias_t, None, tq, "attn_dil_dkv")
    dqf, dl_f = _attn_bwd_dq("fox", qf, kf, vf, yf, dyf, lse_f, c_row, tq, "attn_fox_dq")
    dkf, dvf, dc = _attn_bwd_dkv("fox", qf, kf, vf, dyf, lse_f_row, dl_f, c_rep, c_row, tq, "attn_fox_dkv")
    dc_pad = jnp.pad(dc[:, :, 0, :].reshape(nh, t).T, ((0, 0), (0, LANE - nh)))
    dlogf = _cumsum_rows(dc_pad, True, "revcumsum_dc")
    dproj, d_bforget = _assemble_dproj(dqd, dkd, dvd, dqf, dkf, dvf, dgd, dgf, dlogf, proj, tables, bf_pad, scale)

    dwin_p = _mm_tn(hm, dproj, BF, "w_in_dw")
    dwin_full = jnp.concatenate([dwin_p[:, :6 * hd], dwin_p[:, 6 * hd + 2 * d:6 * hd + 2 * d + n_f],
                                 dwin_p[:, 6 * hd:6 * hd + 2 * d]], axis=1)
    dwin_c = dwin_full.reshape(d, N_DEV, cols).transpose(1, 0, 2)
    rs_win, tok = _exchange_start([dwin_c], False, "rs_start_w_in")
    dhm = _mm_nt(dproj, win_p, F32, "w_in_bwd", tn_pref=2048, tk_pref=1152)
    dx1, dx1b, d_mix_norm = _rms_bwd(dhm, x1, mix_norm, dx2, "rms_mix_bwd", dep=tok)

    dhn1, rs_ffn1 = ffn_bwd(dx1b, hn1, g1, u1, a1, wg1, wu1, wd1, "ffn1")
    grad_x, _, d_ffn1_norm = _rms_bwd(dhn1, x2d, ffn1_norm, dx1, "rms_ffn1_bwd")

    def update(handles, names, after, tag):
        recvs = _exchange_wait(handles, False, after, "rs_wait_" + tag)
        res = {}
        for recv, n in zip(recvs, names):
            w, m, v = wmv[n]
            g, delta, m2, v2 = _adam_from_partials(recv, w[0], m[0], v[0], "adam_" + n)
            res[n] = (g[None], delta[None], m2[None], v2[None])
        return res, g

    wmv = {
        "ffn1_w_gate": (ffn1_w_gate, m_ffn1_w_gate, v_ffn1_w_gate),
        "ffn1_w_up": (ffn1_w_up, m_ffn1_w_up, v_ffn1_w_up),
        "ffn1_w_down": (ffn1_w_down, m_ffn1_w_down, v_ffn1_w_down),
        "w_in": (w_in, m_w_in, v_w_in),
        "w_proj_dil": (w_proj_dil, m_w_proj_dil, v_w_proj_dil),
        "w_proj_fox": (w_proj_fox, m_w_proj_fox, v_w_proj_fox),
        "w_out": (w_out, m_w_out, v_w_out),
        "ffn2_w_gate": (ffn2_w_gate, m_ffn2_w_gate, v_ffn2_w_gate),
        "ffn2_w_up": (ffn2_w_up, m_ffn2_w_up, v_ffn2_w_up),
        "ffn2_w_down": (ffn2_w_down, m_ffn2_w_down, v_ffn2_w_down),
    }
    big = {}
    after = grad_x
    for handles, names, tag in [
            (rs_ffn2, ["ffn2_w_gate", "ffn2_w_up", "ffn2_w_down"], "ffn2"),
            (rs_mix, ["w_out", "w_proj_dil", "w_proj_fox"], "mixer"),
            (rs_win, ["w_in"], "w_in"),
            (rs_ffn1, ["ffn1_w_gate", "ffn1_w_up", "ffn1_w_down"], "ffn1")]:
        res, after = update(handles, names, after, tag)
        big.update(res)

    def lanes(a):
        a = a.reshape(1, -1)
        return jnp.pad(a, ((0, 0), (0, d - a.shape[1])))

    small_names = ["ffn1_norm", "mix_norm", "b_gate_dil", "b_gate_fox", "ffn2_norm", "final_norm", "b_forget"]
    small_g = [d_ffn1_norm, d_mix_norm, d_bd, d_bf, d_ffn2_norm, d_final, d_bforget[:, :n_f]]
    small_w = [ffn1_norm, mix_norm, b_gate_dil, b_gate_fox, ffn2_norm, final_norm, b_forget]
    small_m = [m_ffn1_norm, m_mix_norm, m_b_gate_dil, m_b_gate_fox, m_ffn2_norm, m_final_norm, m_b_forget]
    small_v = [v_ffn1_norm, v_mix_norm, v_b_gate_dil, v_b_gate_fox, v_ffn2_norm, v_final_norm, v_b_forget]
    pack = lambda arrs, last: jnp.concatenate([lanes(a) for a in arrs] + [last], axis=0)
    g_all = _allreduce_small(pack(small_g, loss_lanes))
    zero_row = jnp.zeros((1, d), F32)
    one_row = jnp.ones((1, d), F32)
    s_delta, s_m, s_v = _adam_small(g_all, pack(small_w, zero_row), pack(small_m, zero_row), pack(small_v, one_row))
    loss = g_all[len(small_names), 0]

    def unpack(packed, i, like):
        return packed[i, :like.size].reshape(like.shape)

    small = {}
    for i, (n, w) in enumerate(zip(small_names, small_w)):
        small[n] = (unpack(g_all, i, w), unpack(s_delta, i, w), unpack(s_m, i, w), unpack(s_v, i, w))

    order = ["ffn1_norm", "ffn1_w_gate", "ffn1_w_up", "ffn1_w_down", "mix_norm", "w_in", "b_forget", "b_gate_dil",
             "b_gate_fox", "w_proj_dil", "w_proj_fox", "w_out", "ffn2_norm", "ffn2_w_gate", "ffn2_w_up",
             "ffn2_w_down", "final_norm"]
    res = {**big, **small}
    outs = [loss, grad_x[None]]
    for slot in range(4):
        outs += [res[n][slot] for n in order]
    return tuple(outs)
```

```python
import functools

import numpy as np
import jax
import jax.numpy as jnp
from jax import lax
from jax.experimental import pallas as pl
from jax.experimental.pallas import tpu as pltpu

BF = jnp.bfloat16
F32 = jnp.float32
MESH = pl.DeviceIdType.MESH
N_DEV = 8

HEAD_DIM = 128
ROPE_DIM = HEAD_DIM // 4
ROPE_HALF = ROPE_DIM // 2
ROPE_THETA = 500000.0
NORM_EPS = 1e-6
DIL_PATTERNS = ((128, 1), (512, 4), (2048, 16))
MAX_WINDOW = 2048
LANE = 128
NEG = -1e30

ADAM_LR = 0.001
ADAM_B1 = 0.9
ADAM_B2 = 0.999
ADAM_EPS = 1e-08
ADAM_WD = 0.01
ADAM_STEP = 10

VMEM_LIMIT_BYTES = 56 * 1024 * 1024
ANY = pl.BlockSpec(memory_space=pl.ANY)

NN = (((1,), (0,)), ((), ()))
NT = (((1,), (1,)), ((), ()))
TN = (((0,), (0,)), ((), ()))


def _dot(a, b, dn=NN):
    return lax.dot_general(a, b, dn, preferred_element_type=F32)


def _sig(x):
    return 1.0 / (1.0 + jnp.exp(-x))


def _tile(n, pref, align):
    best = None
    t = align
    while t <= min(n, pref):
        if n % t == 0:
            best = t
        t += align
    return n if best is None else best


def _params():
    return pltpu.CompilerParams(vmem_limit_bytes=VMEM_LIMIT_BYTES)


def _call(body, args, dep=None, **kw):
    if dep is not None:
        n_in = len(args)
        inner = body

        def body(*refs):
            inner(*refs[:n_in], *refs[n_in + 1:])

        kw["in_specs"] = list(kw["in_specs"]) + [ANY]
        args = list(args) + [dep]
    return pl.pallas_call(body, **kw)(*args)


def _peers():
    x, y, c = lax.axis_index("x"), lax.axis_index("y"), lax.axis_index("c")
    me = 4 * x + 2 * y + c
    peers = []
    for k in range(1, N_DEV):
        px = 1 - x if (k >> 2) & 1 else x
        py = 1 - y if (k >> 1) & 1 else y
        pc = 1 - c if k & 1 else c
        peers.append((k, (px, py, pc), 4 * px + 2 * py + pc))
    return me, peers


HBM = pl.BlockSpec(memory_space=pltpu.HBM)
SEM = pl.BlockSpec(memory_space=pltpu.SEMAPHORE)
EFFECT = pltpu.SideEffectType.DATAFLOW_SIDE_EFFECTING


def _exchange_copy(gather, src_ref, land_ref, send_sems, recv_sems, me, k, peer, peer_flat, landing):
    return pltpu.make_async_remote_copy(
        src_ref=src_ref if gather else src_ref.at[peer_flat], dst_ref=land_ref.at[landing],
        send_sem=send_sems.at[k], recv_sem=recv_sems.at[k], device_id=peer, device_id_type=MESH)


def _exchange_start(srcs, gather, name):
    n = len(srcs)

    def body(*refs):
        src_refs, land_refs = refs[:n], refs[n:2 * n]
        send_refs, recv_refs = refs[2 * n:3 * n], refs[3 * n:4 * n]
        token = refs[6 * n]
        me, peers = _peers()
        for i in range(n):
            for k, peer, peer_flat in peers:
                _exchange_copy(gather, src_refs[i], land_refs[i], send_refs[i], recv_refs[i],
                               me, k, peer, peer_flat, me).start()
        token[...] = jnp.zeros_like(token)

    lands = [lax.empty((N_DEV,) + s.shape[-2:], s.dtype) for s in srcs]
    sems = [pltpu.SemaphoreType.DMA((N_DEV,)) for _ in range(2 * n)]
    out = pl.pallas_call(
        body, name=name,
        out_shape=tuple(sems) + tuple(pltpu.HBM(a.shape, a.dtype) for a in list(srcs) + lands)
        + (jax.ShapeDtypeStruct((8, LANE), F32),),
        in_specs=[HBM] * (2 * n),
        out_specs=tuple([SEM] * (2 * n) + [HBM] * (2 * n) + [pl.BlockSpec(memory_space=pltpu.VMEM)]),
        input_output_aliases={i: 2 * n + i for i in range(2 * n)},
        compiler_params=pltpu.CompilerParams(has_side_effects=EFFECT),
    )(*[pltpu.with_memory_space_constraint(a, pltpu.HBM) for a in list(srcs) + lands])
    handles = [(out[2 * n + i], out[3 * n + i], out[i], out[n + i]) for i in range(n)]
    return handles, out[4 * n]


def _exchange_wait(handles, gather, after, name):
    n = len(handles)

    def body(*refs):
        src_refs, land_refs = refs[:n], refs[n:2 * n]
        send_refs, recv_refs = refs[2 * n:3 * n], refs[3 * n:4 * n]
        local_sems = refs[6 * n + 1]
        me, peers = _peers()
        local = [pltpu.make_async_copy(src_refs[i] if gather else src_refs[i].at[me], land_refs[i].at[me],
                                       local_sems.at[i]) for i in range(n)]
        for cp in local:
            cp.start()
        for i in range(n):
            for k, peer, peer_flat in peers:
                cp = _exchange_copy(gather, src_refs[i], land_refs[i], send_refs[i], recv_refs[i],
                                    me, k, peer, peer_flat, peer_flat)
                cp.wait_send()
                cp.wait_recv()
        for cp in local:
            cp.wait()

    srcs = [h[0] for h in handles]
    lands = [h[1] for h in handles]
    out = pl.pallas_call(
        body, name=name,
        out_shape=tuple(pltpu.HBM(a.shape, a.dtype) for a in srcs + lands),
        in_specs=[HBM] * (2 * n) + [SEM] * (2 * n) + [ANY],
        out_specs=tuple([HBM] * (2 * n)),
        input_output_aliases={i: i for i in range(2 * n)},
        scratch_shapes=[pltpu.SemaphoreType.DMA((n,))],
        compiler_params=pltpu.CompilerParams(has_side_effects=EFFECT),
    )(*srcs, *lands, *[h[2] for h in handles], *[h[3] for h in handles], after)
    return list(out[n:])


def _allreduce_small(p):
    rows, d = p.shape

    def body(p_ref, o_ref, recv_ref, send_sems, recv_sems):
        me, peers = _peers()
        recv_ref[me] = p_ref[...]
        sends = []
        for k, peer, peer_flat in peers:
            cp = pltpu.make_async_remote_copy(
                src_ref=p_ref, dst_ref=recv_ref.at[me],
                send_sem=send_sems.at[k], recv_sem=recv_sems.at[k],
                device_id=peer, device_id_type=MESH)
            cp.start()
            sends.append(cp)
        for k, peer, peer_flat in peers:
            pltpu.make_async_remote_copy(
                src_ref=p_ref, dst_ref=recv_ref.at[peer_flat],
                send_sem=send_sems.at[k], recv_sem=recv_sems.at[k],
                device_id=peer, device_id_type=MESH).wait_recv()
        for cp in sends:
            cp.wait_send()
        acc = recv_ref[0]
        for s in range(1, N_DEV):
            acc = acc + recv_ref[s]
        is_loss = lax.broadcasted_iota(jnp.int32, (rows, d), 0) == rows - 1
        total = jnp.sum(jnp.where(is_loss, acc, 0.0))
        o_ref[...] = jnp.where(is_loss, total, acc)

    return pl.pallas_call(
        body, name="allreduce_small",
        out_shape=jax.ShapeDtypeStruct((rows, d), F32),
        in_specs=[pl.BlockSpec(memory_space=pltpu.VMEM)],
        out_specs=pl.BlockSpec(memory_space=pltpu.VMEM),
        scratch_shapes=[pltpu.VMEM((N_DEV, rows, d), F32),
                        pltpu.SemaphoreType.DMA((N_DEV,)), pltpu.SemaphoreType.DMA((N_DEV,))],
    )(p)


def _adam_math(w, g, m, v):
    m2 = ADAM_B1 * m + (1.0 - ADAM_B1) * g
    v2 = ADAM_B2 * v + (1.0 - ADAM_B2) * (g * g)
    m_hat = m2 / (1.0 - ADAM_B1 ** ADAM_STEP)
    v_hat = v2 / (1.0 - ADAM_B2 ** ADAM_STEP)
    delta = -ADAM_LR * (m_hat / (jnp.sqrt(v_hat) + ADAM_EPS) + ADAM_WD * w)
    return delta, m2, v2


def _adam_from_partials(parts, w, m, v, name):
    r, c = w.shape
    tr = _tile(r, 256, 16)

    def body(p_ref, w_ref, m_ref, v_ref, g_out, d_out, m_out, v_out):
        g = p_ref[0].astype(F32)
        for s in range(1, N_DEV):
            g = g + p_ref[s].astype(F32)
        delta, m2, v2 = _adam_math(w_ref[...], g, m_ref[...], v_ref[...])
        g_out[...] = g
        d_out[...] = delta
        m_out[...] = m2
        v_out[...] = v2

    blk = pl.BlockSpec((tr, c), lambda i: (i, 0))
    out = jax.ShapeDtypeStruct((r, c), F32)
    return pl.pallas_call(
        body, name=name, grid=(r // tr,),
        in_specs=[pl.BlockSpec((N_DEV, tr, c), lambda i: (0, i, 0)), blk, blk, blk],
        out_specs=[blk, blk, blk, blk], out_shape=[out, out, out, out],
        compiler_params=_params(),
    )(parts, w, m, v)


def _adam_small(g, w, m, v):
    def body(g_ref, w_ref, m_ref, v_ref, d_out, m_out, v_out):
        delta, m2, v2 = _adam_math(w_ref[...], g_ref[...], m_ref[...], v_ref[...])
        d_out[...] = delta
        m_out[...] = m2
        v_out[...] = v2

    out = jax.ShapeDtypeStruct(g.shape, F32)
    return pl.pallas_call(body, name="adam_small", out_shape=[out, out, out])(g, w, m, v)


def _rms_fwd(x, gain, name, dep=None):
    t, d = x.shape
    tr = _tile(t, 256, 16)

    def body(x_ref, g_ref, o_ref):
        xv = x_ref[...]
        r = lax.rsqrt(jnp.mean(xv * xv, axis=-1, keepdims=True) + NORM_EPS)
        o_ref[...] = (xv * r * g_ref[...]).astype(BF)

    return _call(
        body, [x, gain], dep=dep, name=name, grid=(t // tr,),
        in_specs=[pl.BlockSpec((tr, d), lambda i: (i, 0)), pl.BlockSpec((1, d), lambda i: (0, 0))],
        out_specs=pl.BlockSpec((tr, d), lambda i: (i, 0)),
        out_shape=jax.ShapeDtypeStruct((t, d), BF), compiler_params=_params(),
    )


def _rms_vjp(xv, gain, dy):
    r = lax.rsqrt(jnp.mean(xv * xv, axis=-1, keepdims=True) + NORM_EPS)
    xhat = xv * r
    dxhat = dy * gain
    dx = r * (dxhat - xhat * jnp.mean(dxhat * xhat, axis=-1, keepdims=True))
    dgain = jnp.sum(dy * xhat, axis=0, keepdims=True)
    return dx, dgain


def _rms_bwd(dy, x, gain, dres, name, dep=None):
    t, d = x.shape
    tr = _tile(t, 256, 16)

    def body(dy_ref, x_ref, g_ref, dres_ref, dx_ref, dxb_ref, dg_ref):
        dx, dgain = _rms_vjp(x_ref[...], g_ref[...], dy_ref[...])
        dx = dx + dres_ref[...]
        dx_ref[...] = dx
        dxb_ref[...] = dx.astype(BF)

        @pl.when(pl.program_id(0) == 0)
        def _():
            dg_ref[...] = jnp.zeros_like(dg_ref)

        dg_ref[...] += dgain

    row = pl.BlockSpec((tr, d), lambda i: (i, 0))
    vec = pl.BlockSpec((1, d), lambda i: (0, 0))
    return _call(
        body, [dy, x, gain, dres], dep=dep, name=name, grid=(t // tr,),
        in_specs=[row, row, vec, row], out_specs=[row, row, vec],
        out_shape=[jax.ShapeDtypeStruct((t, d), F32), jax.ShapeDtypeStruct((t, d), BF),
                   jax.ShapeDtypeStruct((1, d), F32)],
        compiler_params=_params(),
    )


def _loss_head(x, gain, target):
    t, d = x.shape
    tr = _tile(t, 256, 16)

    def body(x_ref, g_ref, t_ref, dx_ref, dxb_ref, dg_ref, loss_ref):
        xv = x_ref[...]
        gain = g_ref[...]
        r = lax.rsqrt(jnp.mean(xv * xv, axis=-1, keepdims=True) + NORM_EPS)
        err = xv * r * gain - t_ref[...]
        dx, dgain = _rms_vjp(xv, gain, err * (1.0 / d))
        dx_ref[...] = dx
        dxb_ref[...] = dx.astype(BF)

        @pl.when(pl.program_id(0) == 0)
        def _():
            dg_ref[...] = jnp.zeros_like(dg_ref)
            loss_ref[...] = jnp.zeros_like(loss_ref)

        dg_ref[...] += dgain
        loss_ref[...] += jnp.sum(err * err, axis=0, keepdims=True) * (0.5 / d)

    row = pl.BlockSpec((tr, d), lambda i: (i, 0))
    vec = pl.BlockSpec((1, d), lambda i: (0, 0))
    return pl.pallas_call(
        body, name="loss_head", grid=(t // tr,),
        in_specs=[row, vec, row], out_specs=[row, row, vec, vec],
        out_shape=[jax.ShapeDtypeStruct((t, d), F32), jax.ShapeDtypeStruct((t, d), BF),
                   jax.ShapeDtypeStruct((1, d), F32), jax.ShapeDtypeStruct((1, d), F32)],
        compiler_params=_params(),
    )(x, gain, target)


def _mm_nn(a, b, out_dtype, name, residual=None, tm_pref=512, tn_pref=1152):
    m, k = a.shape
    n = b.shape[1]
    tm, tn = _tile(m, tm_pref, 16), _tile(n, tn_pref, LANE)

    def body(*refs):
        if residual is None:
            a_ref, b_ref, o_ref = refs
            o_ref[...] = _dot(a_ref[...], b_ref[...]).astype(out_dtype)
        else:
            a_ref, b_ref, r_ref, o_ref = refs
            o_ref[...] = (r_ref[...] + _dot(a_ref[...], b_ref[...])).astype(out_dtype)

    in_specs = [pl.BlockSpec((tm, k), lambda j, i: (i, 0)), pl.BlockSpec((k, tn), lambda j, i: (0, j))]
    args = [a, b]
    if residual is not None:
        in_specs.append(pl.BlockSpec((tm, tn), lambda j, i: (i, j)))
        args.append(residual)
    return pl.pallas_call(
        body, name=name, grid=(n // tn, m // tm), in_specs=in_specs,
        out_specs=pl.BlockSpec((tm, tn), lambda j, i: (i, j)),
        out_shape=jax.ShapeDtypeStruct((m, n), out_dtype), compiler_params=_params(),
    )(*args)


def _mm_nt(a, b, out_dtype, name, tm_pref=512, tn_pref=1024, tk_pref=2048):
    m, k = a.shape
    n = b.shape[0]
    tm, tn, tk = _tile(m, tm_pref, 16), _tile(n, tn_pref, LANE), _tile(k, tk_pref, LANE)
    nk = k // tk

    def body(a_ref, b_ref, o_ref, acc_ref):
        kk = pl.program_id(2)

        @pl.when(kk == 0)
        def _():
            acc_ref[...] = jnp.zeros_like(acc_ref)

        acc_ref[...] += _dot(a_ref[...], b_ref[...], NT)

        @pl.when(kk == nk - 1)
        def _():
            o_ref[...] = acc_ref[...].astype(out_dtype)

    return pl.pallas_call(
        body, name=name, grid=(n // tn, m // tm, nk),
        in_specs=[pl.BlockSpec((tm, tk), lambda j, i, kk: (i, kk)),
                  pl.BlockSpec((tn, tk), lambda j, i, kk: (j, kk))],
        out_specs=pl.BlockSpec((tm, tn), lambda j, i, kk: (i, j)),
        out_shape=jax.ShapeDtypeStruct((m, n), out_dtype),
        scratch_shapes=[pltpu.VMEM((tm, tn), F32)], compiler_params=_params(),
    )(a, b)


def _mm_tn(a, b, out_dtype, name, tn_pref=1152, tk_pref=512):
    t, k = a.shape
    n = b.shape[1]
    tn, tk = _tile(n, tn_pref, LANE), _tile(t, tk_pref, 16)
    nt = t // tk

    def body(a_ref, b_ref, o_ref, acc_ref):
        tt = pl.program_id(1)

        @pl.when(tt == 0)
        def _():
            acc_ref[...] = jnp.zeros_like(acc_ref)

        acc_ref[...] += _dot(a_ref[...], b_ref[...], TN)

        @pl.when(tt == nt - 1)
        def _():
            o_ref[...] = acc_ref[...].astype(out_dtype)

    return pl.pallas_call(
        body, name=name, grid=(n // tn, nt),
        in_specs=[pl.BlockSpec((tk, k), lambda j, tt: (tt, 0)), pl.BlockSpec((tk, tn), lambda j, tt: (tt, j))],
        out_specs=pl.BlockSpec((k, tn), lambda j, tt: (0, j)),
        out_shape=jax.ShapeDtypeStruct((k, n), out_dtype),
        scratch_shapes=[pltpu.VMEM((k, tn), F32)], compiler_params=_params(),
    )(a, b)


def _ffn_gate_up(hn, wg, wu, name):
    t, d = hn.shape
    ns, _, f = wg.shape
    tm = _tile(t, 512, 16)

    def body(h_ref, wg_ref, wu_ref, g_ref, u_ref, a_ref):
        h = h_ref[...]
        g = _dot(h, wg_ref[...])
        u = _dot(h, wu_ref[...])
        g_ref[...] = g.astype(BF)
        u_ref[...] = u.astype(BF)
        a_ref[...] = (g * _sig(g) * u).astype(BF)

    wspec = pl.BlockSpec((None, d, f), lambda j, i: (j, 0, 0))
    hid = pl.BlockSpec((None, tm, f), lambda j, i: (j, i, 0))
    out = jax.ShapeDtypeStruct((ns, t, f), BF)
    return pl.pallas_call(
        body, name=name, grid=(ns, t // tm),
        in_specs=[pl.BlockSpec((tm, d), lambda j, i: (i, 0)), wspec, wspec],
        out_specs=[hid, hid, hid], out_shape=[out, out, out], compiler_params=_params(),
    )(hn, wg, wu)


def _ffn_down(act, wd, xres, name):
    ns, t, f = act.shape
    d = wd.shape[2]
    tm = _tile(t, 512, 16)

    def body(a_ref, w_ref, x_ref, o_ref):
        @pl.when(pl.program_id(1) == 0)
        def _():
            o_ref[...] = x_ref[...]

        o_ref[...] += 0.5 * _dot(a_ref[...], w_ref[...])

    row = pl.BlockSpec((tm, d), lambda i, j: (i, 0))
    return pl.pallas_call(
        body, name=name, grid=(t // tm, ns),
        in_specs=[pl.BlockSpec((None, tm, f), lambda i, j: (j, i, 0)),
                  pl.BlockSpec((None, f, d), lambda i, j: (j, 0, 0)), row],
        out_specs=row, out_shape=jax.ShapeDtypeStruct((t, d), F32), compiler_params=_params(),
    )(act, wd, xres)


def _ffn_bwd_hidden(dxb, wd, g, u, name):
    t, d = dxb.shape
    ns, f, _ = wd.shape
    tm = _tile(t, 512, 16)

    def body(dx_ref, w_ref, g_ref, u_ref, dg_ref, du_ref):
        dh = 0.5 * _dot(dx_ref[...], w_ref[...], NT)
        gv = g_ref[...].astype(F32)
        uv = u_ref[...].astype(F32)
        s = _sig(gv)
        dg_ref[...] = (dh * uv * (s * (1.0 + gv * (1.0 - s)))).astype(BF)
        du_ref[...] = (dh * (gv * s)).astype(BF)

    hid = pl.BlockSpec((None, tm, f), lambda j, i: (j, i, 0))
    out = jax.ShapeDtypeStruct((ns, t, f), BF)
    return pl.pallas_call(
        body, name=name, grid=(ns, t // tm),
        in_specs=[pl.BlockSpec((tm, d), lambda j, i: (i, 0)),
                  pl.BlockSpec((None, f, d), lambda j, i: (j, 0, 0)), hid, hid],
        out_specs=[hid, hid], out_shape=[out, out], compiler_params=_params(),
    )(dxb, wd, g, u)


def _ffn_dw_down(act, dxb, name):
    ns, t, f = act.shape
    d = dxb.shape[1]
    tk = _tile(t, 512, 16)
    nt = t // tk

    def body(a_ref, dx_ref, o_ref, acc_ref):
        tt = pl.program_id(1)

        @pl.when(tt == 0)
        def _():
            acc_ref[...] = jnp.zeros_like(acc_ref)

        acc_ref[...] += _dot(a_ref[...], dx_ref[...], TN)

        @pl.when(tt == nt - 1)
        def _():
            o_ref[...] = (0.5 * acc_ref[...]).astype(BF)

    return pl.pallas_call(
        body, name=name, grid=(ns, nt),
        in_specs=[pl.BlockSpec((None, tk, f), lambda j, tt: (j, tt, 0)),
                  pl.BlockSpec((tk, d), lambda j, tt: (tt, 0))],
        out_specs=pl.BlockSpec((None, f, d), lambda j, tt: (j, 0, 0)),
        out_shape=jax.ShapeDtypeStruct((ns, f, d), BF),
        scratch_shapes=[pltpu.VMEM((f, d), F32)], compiler_params=_params(),
    )(act, dxb)


def _ffn_dw_gate_up(hn, dg, du, name, dep=None):
    t, d = hn.shape
    ns, _, f = dg.shape
    tk = _tile(t, 512, 16)
    nt = t // tk

    def body(h_ref, dg_ref, du_ref, og_ref, ou_ref, accg_ref, accu_ref):
        tt = pl.program_id(1)

        @pl.when(tt == 0)
        def _():
            accg_ref[...] = jnp.zeros_like(accg_ref)
            accu_ref[...] = jnp.zeros_like(accu_ref)

        h = h_ref[...]
        accg_ref[...] += _dot(h, dg_ref[...], TN)
        accu_ref[...] += _dot(h, du_ref[...], TN)

        @pl.when(tt == nt - 1)
        def _():
            og_ref[...] = accg_ref[...].astype(BF)
            ou_ref[...] = accu_ref[...].astype(BF)

    hid = pl.BlockSpec((None, tk, f), lambda j, tt: (j, tt, 0))
    wspec = pl.BlockSpec((None, d, f), lambda j, tt: (j, 0, 0))
    out = jax.ShapeDtypeStruct((ns, d, f), BF)
    return _call(
        body, [hn, dg, du], dep=dep, name=name, grid=(ns, nt),
        in_specs=[pl.BlockSpec((tk, d), lambda j, tt: (tt, 0)), hid, hid],
        out_specs=[wspec, wspec], out_shape=[out, out],
        scratch_shapes=[pltpu.VMEM((d, f), F32), pltpu.VMEM((d, f), F32)], compiler_params=_params(),
    )


def _ffn_bwd_input(dg, du, wg, wu, name, dep=None):
    ns, t, f = dg.shape
    d = wg.shape[1]
    tm = _tile(t, 512, 16)

    def body(dg_ref, du_ref, wg_ref, wu_ref, o_ref):
        @pl.when(pl.program_id(1) == 0)
        def _():
            o_ref[...] = jnp.zeros_like(o_ref)

        o_ref[...] += _dot(dg_ref[...], wg_ref[...], NT) + _dot(du_ref[...], wu_ref[...], NT)

    hid = pl.BlockSpec((None, tm, f), lambda i, j: (j, i, 0))
    wspec = pl.BlockSpec((None, d, f), lambda i, j: (j, 0, 0))
    return _call(
        body, [dg, du, wg, wu], dep=dep, name=name, grid=(t // tm, ns),
        in_specs=[hid, hid, wspec, wspec],
        out_specs=pl.BlockSpec((tm, d), lambda i, j: (i, 0)),
        out_shape=jax.ShapeDtypeStruct((t, d), F32), compiler_params=_params(),
    )


def _rope_tables(t):
    pos = jnp.arange(t, dtype=F32)
    inv_freq = ROPE_THETA ** (-jnp.arange(0, ROPE_DIM, 2, dtype=F32) / ROPE_DIM)
    ang = pos[:, None] * inv_freq[None, :]
    cos, sin = jnp.cos(ang), jnp.sin(ang)
    rest = HEAD_DIM - ROPE_DIM
    one = jnp.ones((t, rest), F32)
    zero_h = jnp.zeros((t, ROPE_HALF), F32)
    zero_r = jnp.zeros((t, rest), F32)
    c = jnp.concatenate([cos, cos, one], axis=1)
    s1 = jnp.concatenate([-sin, zero_h, zero_r], axis=1)
    s2 = jnp.concatenate([zero_h, sin, zero_r], axis=1)
    return c, s1, s2


def _rope(xh, c, s1, s2):
    return xh * c + pltpu.roll(xh, HEAD_DIM - ROPE_HALF, 1) * s1 + pltpu.roll(xh, ROPE_HALF, 1) * s2


def _rope_t(dh, c, s1, s2):
    return dh * c + pltpu.roll(dh * s1, ROPE_HALF, 1) + pltpu.roll(dh * s2, HEAD_DIM - ROPE_HALF, 1)


def _mixer_prep(proj, tables, bf_pad, hd, scale):
    t, np_ = proj.shape
    tr = _tile(t, 256, 16)
    nh = hd // HEAD_DIM
    nblk = hd // LANE
    f_blk = np_ // LANE - 1

    def body(qd_ref, kd_ref, vd_ref, qf_ref, kf_ref, vf_ref, fl_ref, c_ref, s1_ref, s2_ref, b_ref,
             oqd, okd, ovd, oqf, okf, ovf, olog):
        c, s1, s2 = c_ref[...], s1_ref[...], s2_ref[...]
        for h in range(nh):
            sl = slice(h * HEAD_DIM, (h + 1) * HEAD_DIM)
            oqd[:, sl] = (_rope(qd_ref[:, sl], c, s1, s2) * scale).astype(BF)
            okd[:, sl] = _rope(kd_ref[:, sl], c, s1, s2).astype(BF)
        ovd[...] = vd_ref[...].astype(BF)
        oqf[...] = (qf_ref[...] * scale).astype(BF)
        okf[...] = kf_ref[...].astype(BF)
        ovf[...] = vf_ref[...].astype(BF)
        z = fl_ref[...] + b_ref[...]
        olog[...] = jnp.minimum(z, 0.0) - jnp.log(1.0 + jnp.exp(-jnp.abs(z)))

    def col(kblk):
        return pl.BlockSpec((tr, hd), lambda i, kblk=kblk: (i, kblk))

    lane_row = pl.BlockSpec((tr, LANE), lambda i: (i, 0))
    in_specs = [col(0), col(1), col(2), col(3), col(4), col(5),
                pl.BlockSpec((tr, LANE), lambda i: (i, f_blk)),
                lane_row, lane_row, lane_row, pl.BlockSpec((1, LANE), lambda i: (0, 0))]
    o = pl.BlockSpec((tr, hd), lambda i: (i, 0))
    ob = jax.ShapeDtypeStruct((t, hd), BF)
    del nblk
    return pl.pallas_call(
        body, name="mixer_prep", grid=(t // tr,), in_specs=in_specs,
        out_specs=[o, o, o, o, o, o, lane_row],
        out_shape=[ob, ob, ob, ob, ob, ob, jax.ShapeDtypeStruct((t, LANE), F32)],
        compiler_params=_params(),
    )(proj, proj, proj, proj, proj, proj, proj, *tables, bf_pad)


def _split3(x):
    x1 = x.astype(BF)
    r1 = x - x1.astype(F32)
    x2 = r1.astype(BF)
    x3 = (r1 - x2.astype(F32)).astype(BF)
    return x1, x2, x3


def _cumsum_rows(x, reverse, name):
    t, w = x.shape
    blk = LANE
    nb = t // blk

    def body(x_ref, o_ref):
        r = lax.broadcasted_iota(jnp.int32, (blk, blk), 0)
        c = lax.broadcasted_iota(jnp.int32, (blk, blk), 1)
        tri = jnp.where((c >= r) if reverse else (c <= r), 1.0, 0.0).astype(BF)

        def step(i, carry):
            b = (nb - 1 - i) if reverse else i
            off = pl.multiple_of(b * blk, blk)
            xb = x_ref[pl.ds(off, blk), :]
            x1, x2, x3 = _split3(xb)
            o_ref[pl.ds(off, blk), :] = _dot(tri, x1) + _dot(tri, x2) + _dot(tri, x3) + carry
            return carry + jnp.sum(xb, axis=0, keepdims=True)

        lax.fori_loop(0, nb, step, jnp.zeros((1, w), F32))

    return pl.pallas_call(body, name=name, out_shape=jax.ShapeDtypeStruct((t, w), F32),
                          compiler_params=_params())(x)


ATTN_ROWS = 16


def _dil_bias_tiles(tq):
    nbias = MAX_WINDOW // tq + 1
    b = lax.broadcasted_iota(jnp.int32, (nbias, tq, tq), 0)
    i = lax.broadcasted_iota(jnp.int32, (nbias, tq, tq), 1)
    j = lax.broadcasted_iota(jnp.int32, (nbias, tq, tq), 2)
    delta = b * tq + i - j
    mult = jnp.zeros((nbias, tq, tq), F32)
    for w, dil in DIL_PATTERNS:
        mult = mult + jnp.where((delta >= 0) & (delta <= w) & (delta % dil == 0), 1.0, 0.0)
    return jnp.where(mult > 0.0, jnp.log(jnp.maximum(mult, 1.0)), NEG)


def _rep(x, width):
    return jnp.tile(x, (1, width // LANE))


def _chunks(n_rows, fn):
    for c in range(n_rows // ATTN_ROWS):
        fn(c * ATTN_ROWS)


def _causal(r0, tq, transposed):
    a = lax.broadcasted_iota(jnp.int32, (ATTN_ROWS, tq), 0) + r0
    b = lax.broadcasted_iota(jnp.int32, (ATTN_ROWS, tq), 1)
    return (a <= b) if transposed else (b <= a)


def _rows8(x):
    return jnp.transpose(x)[:8, :]


def _attn_fwd(mode, q, k, v, bias, tq, name):
    t, hd = q.shape
    nh = hd // HEAD_DIM
    nb = t // tq
    wb = MAX_WINDOW // tq
    fox = mode == "fox"

    def body(q_ref, k_ref, v_ref, b_ref, o_ref, lse_ref, lse_row_ref, s_ref, p_ref, m_ref, l_ref, acc_ref):
        qi = pl.program_id(1)
        qb = q_ref[...]
        m_ref[...] = jnp.full_like(m_ref, NEG)
        l_ref[...] = jnp.zeros_like(l_ref)
        acc_ref[...] = jnp.zeros_like(acc_ref)

        def tile(kj, diag):
            off = pl.multiple_of(kj * tq, tq)
            s_ref[...] = _dot(qb, k_ref[pl.ds(off, tq), :], NT)
            if fox:
                brow = b_ref[qi][:, :1] - b_ref[kj]

            def chunk(r0):
                rows = pl.ds(r0, ATTN_ROWS)
                if fox:
                    s = s_ref[rows, :] + brow
                    if diag:
                        s = jnp.where(_causal(r0, tq, False), s, NEG)
                else:
                    s = s_ref[rows, :] + b_ref[qi - kj, rows, :]
                m_old = m_ref[rows, :]
                m_new = jnp.maximum(m_old, jnp.max(s, axis=1, keepdims=True))
                p = jnp.exp(s - _rep(m_new, tq))
                alpha = jnp.exp(m_old - m_new)
                l_ref[rows, :] = alpha * l_ref[rows, :] + jnp.sum(p, axis=1, keepdims=True)
                m_ref[rows, :] = m_new
                acc_ref[rows, :] = alpha * acc_ref[rows, :]
                p_ref[rows, :] = p.astype(BF)

            _chunks(tq, chunk)
            acc_ref[...] += _dot(p_ref[...], v_ref[pl.ds(off, tq), :])

        tile(qi, True)
        if fox:
            lax.fori_loop(0, qi, lambda kj, c: (tile(kj, False), c)[1], 0)
        else:
            lax.fori_loop(1, jnp.minimum(qi, wb) + 1, lambda i, c: (tile(qi - i, False), c)[1], 0)
        o_ref[...] = (acc_ref[...] / l_ref[...]).astype(BF)
        lse = m_ref[...] + jnp.log(l_ref[...])
        lse_ref[...] = lse
        lse_row_ref[...] = _rows8(lse)

    qspec = pl.BlockSpec((tq, HEAD_DIM), lambda h, i: (i, h))
    kvspec = pl.BlockSpec((t, HEAD_DIM), lambda h, i: (0, h))
    repspec = pl.BlockSpec((None, tq, LANE), lambda h, i: (h, i, 0))
    row8spec = pl.BlockSpec((None, None, 8, tq), lambda h, i: (h, i, 0, 0))
    if fox:
        bspec = pl.BlockSpec((None, nb, 1, tq), lambda h, i: (h, 0, 0, 0))
    else:
        bspec = pl.BlockSpec((wb + 1, tq, tq), lambda h, i: (0, 0, 0))
    return pl.pallas_call(
        body, name=name, grid=(nh, nb), in_specs=[qspec, kvspec, kvspec, bspec],
        out_specs=[qspec, repspec, row8spec],
        out_shape=[jax.ShapeDtypeStruct((t, hd), BF), jax.ShapeDtypeStruct((nh, t, LANE), F32),
                   jax.ShapeDtypeStruct((nh, nb, 8, tq), F32)],
        scratch_shapes=[pltpu.VMEM((tq, tq), F32), pltpu.VMEM((tq, tq), BF), pltpu.VMEM((tq, LANE), F32),
                        pltpu.VMEM((tq, LANE), F32), pltpu.VMEM((tq, HEAD_DIM), F32)],
        compiler_params=_params(),
    )(q, k, v, bias)


def _attn_bwd_dq(mode, q, k, v, o, do, lse, bias, tq, name, dep=None):
    t, hd = q.shape
    nh = hd // HEAD_DIM
    nb = t // tq
    wb = MAX_WINDOW // tq
    fox = mode == "fox"

    def body(q_ref, k_ref, v_ref, o_ref, do_ref, lse_ref, b_ref, dq_ref, dl_row_ref,
             s_ref, dp_ref, x_ref, y_ref, acc_ref, acc2_ref, dl_ref):
        qi = pl.program_id(1)
        qb = q_ref[...]
        dob = do_ref[...]
        acc_ref[...] = jnp.zeros_like(acc_ref)
        if fox:
            acc2_ref[...] = jnp.zeros_like(acc2_ref)
            dl_ref[...] = jnp.zeros_like(dl_ref)
        else:
            prod = o_ref[...].astype(F32) * dob.astype(F32)
            dl_ref[...] = jnp.broadcast_to(jnp.sum(prod, axis=1, keepdims=True), (tq, LANE))

        def tile(kj, diag):
            off = pl.multiple_of(kj * tq, tq)
            kb = k_ref[pl.ds(off, tq), :]
            s_ref[...] = _dot(qb, kb, NT)
            dp_ref[...] = _dot(dob, v_ref[pl.ds(off, tq), :], NT)
            if fox:
                brow = b_ref[qi][:, :1] - b_ref[kj]

            def chunk(r0):
                rows = pl.ds(r0, ATTN_ROWS)
                lse_c = _rep(lse_ref[rows, :], tq)
                if fox:
                    s = s_ref[rows, :] + brow
                    if diag:
                        s = jnp.where(_causal(r0, tq, False), s, NEG)
                    p = jnp.exp(s - lse_c)
                    pdp = p * dp_ref[rows, :]
                    dl_ref[rows, :] += jnp.sum(pdp, axis=1, keepdims=True)
                    x_ref[rows, :] = pdp.astype(BF)
                    y_ref[rows, :] = p.astype(BF)
                else:
                    p = jnp.exp(s_ref[rows, :] + b_ref[qi - kj, rows, :] - lse_c)
                    x_ref[rows, :] = (p * (dp_ref[rows, :] - _rep(dl_ref[rows, :], tq))).astype(BF)

            _chunks(tq, chunk)
            acc_ref[...] += _dot(x_ref[...], kb)
            if fox:
                acc2_ref[...] += _dot(y_ref[...], kb)

        tile(qi, True)
        if fox:
            lax.fori_loop(0, qi, lambda kj, c: (tile(kj, False), c)[1], 0)
            dq_ref[...] = acc_ref[...] - dl_ref[...] * acc2_ref[...]
        else:
            lax.fori_loop(1, jnp.minimum(qi, wb) + 1, lambda i, c: (tile(qi - i, False), c)[1], 0)
            dq_ref[...] = acc_ref[...]
        dl_row_ref[...] = _rows8(dl_ref[...])

    qspec = pl.BlockSpec((tq, HEAD_DIM), lambda h, i: (i, h))
    kvspec = pl.BlockSpec((t, HEAD_DIM), lambda h, i: (0, h))
    repspec = pl.BlockSpec((None, tq, LANE), lambda h, i: (h, i, 0))
    row8spec = pl.BlockSpec((None, None, 8, tq), lambda h, i: (h, i, 0, 0))
    if fox:
        bspec = pl.BlockSpec((None, nb, 1, tq), lambda h, i: (h, 0, 0, 0))
    else:
        bspec = pl.BlockSpec((wb + 1, tq, tq), lambda h, i: (0, 0, 0))
    return _call(
        body, [q, k, v, o, do, lse, bias], dep=dep, name=name, grid=(nh, nb),
        in_specs=[qspec, kvspec, kvspec, qspec, qspec, repspec, bspec],
        out_specs=[qspec, row8spec],
        out_shape=[jax.ShapeDtypeStruct((t, hd), F32), jax.ShapeDtypeStruct((nh, nb, 8, tq), F32)],
        scratch_shapes=[pltpu.VMEM((tq, tq), F32), pltpu.VMEM((tq, tq), F32), pltpu.VMEM((tq, tq), BF),
                        pltpu.VMEM((tq, tq), BF), pltpu.VMEM((tq, HEAD_DIM), F32),
                        pltpu.VMEM((tq, HEAD_DIM), F32), pltpu.VMEM((tq, LANE), F32)],
        compiler_params=_params(),
    )


def _attn_bwd_dkv(mode, q, k, v, do, lse_row, dl_row, bias_t, c_row, tq, name):
    t, hd = q.shape
    nh = hd // HEAD_DIM
    nb = t // tq
    wb = MAX_WINDOW // tq
    fox = mode == "fox"

    def body(*refs):
        if fox:
            (q_ref, k_ref, v_ref, do_ref, lse_ref, dl_ref, b_ref, cq_ref, dk_ref, dv_ref, dc_row_ref,
             s_ref, dp_ref, x_ref, y_ref, dc_ref) = refs
        else:
            q_ref, k_ref, v_ref, do_ref, lse_ref, dl_ref, b_ref, dk_ref, dv_ref, s_ref, dp_ref, x_ref, y_ref = refs
        kj = pl.program_id(1)
        kb = k_ref[...]
        vb = v_ref[...]
        dk_ref[...] = jnp.zeros_like(dk_ref)
        dv_ref[...] = jnp.zeros_like(dv_ref)
        if fox:
            dc_ref[...] = jnp.zeros_like(dc_ref)

        def tile(qi, diag):
            off = pl.multiple_of(qi * tq, tq)
            qb = q_ref[pl.ds(off, tq), :]
            dob = do_ref[pl.ds(off, tq), :]
            s_ref[...] = _dot(kb, qb, NT)
            dp_ref[...] = _dot(vb, dob, NT)
            lse_r = lse_ref[qi, 0:1, :]
            dl_r = dl_ref[qi, 0:1, :]
            if fox:
                kbias = cq_ref[qi][:, :1] - b_ref[...]

            def chunk(r0):
                rows = pl.ds(r0, ATTN_ROWS)
                if fox:
                    s = s_ref[rows, :] + _rep(kbias[r0:r0 + ATTN_ROWS, :], tq)
                    if diag:
                        s = jnp.where(_causal(r0, tq, True), s, NEG)
                else:
                    s = s_ref[rows, :] + b_ref[qi - kj, rows, :]
                pt = jnp.exp(s - lse_r)
                dst = pt * (dp_ref[rows, :] - dl_r)
                x_ref[rows, :] = pt.astype(BF)
                y_ref[rows, :] = dst.astype(BF)
                if fox:
                    dc_ref[rows, :] -= jnp.sum(dst, axis=1, keepdims=True)

            _chunks(tq, chunk)
            dv_ref[...] += _dot(x_ref[...], dob)
            dk_ref[...] += _dot(y_ref[...], qb)

        tile(kj, True)
        hi = nb if fox else jnp.minimum(kj + wb + 1, nb)
        lax.fori_loop(kj + 1, hi, lambda qi, c: (tile(qi, False), c)[1], 0)
        if fox:
            dc_row_ref[...] = _rows8(dc_ref[...])

    blkspec = pl.BlockSpec((tq, HEAD_DIM), lambda h, j: (j, h))
    fullspec = pl.BlockSpec((t, HEAD_DIM), lambda h, j: (0, h))
    rows8spec = pl.BlockSpec((None, nb, 8, tq), lambda h, j: (h, 0, 0, 0))
    repspec = pl.BlockSpec((None, tq, LANE), lambda h, j: (h, j, 0))
    in_specs = [fullspec, blkspec, blkspec, fullspec, rows8spec, rows8spec]
    args = [q, k, v, do, lse_row, dl_row, bias_t]
    out_specs = [blkspec, blkspec]
    out_shape = [jax.ShapeDtypeStruct((t, hd), F32), jax.ShapeDtypeStruct((t, hd), F32)]
    scratch = [pltpu.VMEM((tq, tq), F32), pltpu.VMEM((tq, tq), F32), pltpu.VMEM((tq, tq), BF),
               pltpu.VMEM((tq, tq), BF)]
    if fox:
        in_specs += [repspec, pl.BlockSpec((None, nb, 1, tq), lambda h, j: (h, 0, 0, 0))]
        args.append(c_row)
        out_specs.append(pl.BlockSpec((None, None, 8, tq), lambda h, j: (h, j, 0, 0)))
        out_shape.append(jax.ShapeDtypeStruct((nh, nb, 8, tq), F32))
        scratch.append(pltpu.VMEM((tq, LANE), F32))
    else:
        in_specs.append(pl.BlockSpec((wb + 1, tq, tq), lambda h, j: (0, 0, 0)))
    return pl.pallas_call(
        body, name=name, grid=(nh, nb), in_specs=in_specs, out_specs=out_specs, out_shape=out_shape,
        scratch_shapes=scratch, compiler_params=_params(),
    )(*args)


def _gate_specs(t, d, hd, tr):
    row = pl.BlockSpec((tr, d), lambda i: (i, 0))
    vec = pl.BlockSpec((1, d), lambda i: (0, 0))
    base = 6 * hd // d
    gd = pl.BlockSpec((tr, d), lambda i: (i, base))
    gf = pl.BlockSpec((tr, d), lambda i: (i, base + 1))
    return row, vec, gd, gf


def _merge_fwd(pd, pf, proj, b_d, b_f, hd):
    t, d = pd.shape
    tr = _tile(t, 256, 16)
    row, vec, gd, gf = _gate_specs(t, d, hd, tr)

    def body(pd_ref, pf_ref, gd_ref, gf_ref, bd_ref, bf_ref, o_ref):
        o_ref[...] = (_sig(gd_ref[...] + bd_ref[...]) * pd_ref[...]
                      + _sig(gf_ref[...] + bf_ref[...]) * pf_ref[...]).astype(BF)

    return pl.pallas_call(
        body, name="merge_fwd", grid=(t // tr,), in_specs=[row, row, gd, gf, vec, vec],
        out_specs=row, out_shape=jax.ShapeDtypeStruct((t, d), BF), compiler_params=_params(),
    )(pd, pf, proj, proj, b_d, b_f)


def _merge_bwd(dm, pd, pf, proj, b_d, b_f, hd):
    t, d = pd.shape
    tr = _tile(t, 256, 16)
    row, vec, gd, gf = _gate_specs(t, d, hd, tr)

    def body(dm_ref, pd_ref, pf_ref, gd_ref, gf_ref, bd_ref, bf_ref,
             dpd_ref, dpf_ref, dgd_ref, dgf_ref, dbd_ref, dbf_ref):
        dmv = dm_ref[...]
        sd = _sig(gd_ref[...] + bd_ref[...])
        sf = _sig(gf_ref[...] + bf_ref[...])
        dgd = dmv * pd_ref[...] * (sd * (1.0 - sd))
        dgf = dmv * pf_ref[...] * (sf * (1.0 - sf))
        dpd_ref[...] = (dmv * sd).astype(BF)
        dpf_ref[...] = (dmv * sf).astype(BF)
        dgd_ref[...] = dgd.astype(BF)
        dgf_ref[...] = dgf.astype(BF)

        @pl.when(pl.program_id(0) == 0)
        def _():
            dbd_ref[...] = jnp.zeros_like(dbd_ref)
            dbf_ref[...] = jnp.zeros_like(dbf_ref)

        dbd_ref[...] += jnp.sum(dgd, axis=0, keepdims=True)
        dbf_ref[...] += jnp.sum(dgf, axis=0, keepdims=True)

    ob = jax.ShapeDtypeStruct((t, d), BF)
    ov = jax.ShapeDtypeStruct((1, d), F32)
    return pl.pallas_call(
        body, name="merge_bwd", grid=(t // tr,), in_specs=[row, row, row, gd, gf, vec, vec],
        out_specs=[row, row, row, row, vec, vec], out_shape=[ob, ob, ob, ob, ov, ov],
        compiler_params=_params(),
    )(dm, pd, pf, proj, proj, b_d, b_f)


def _assemble_dproj(dqd, dkd, dvd, dqf, dkf, dvf, dgd, dgf, dlogf, proj, tables, bf_pad, scale):
    t, np_ = proj.shape
    hd = dqd.shape[1]
    d = dgd.shape[1]
    nh = hd // HEAD_DIM
    tr = _tile(t, 256, 16)
    f_blk = np_ // LANE - 1

    def body(dqd_ref, dkd_ref, dvd_ref, dqf_ref, dkf_ref, dvf_ref, dgd_ref, dgf_ref, dlog_ref, fl_ref,
             c_ref, s1_ref, s2_ref, b_ref, o_ref, db_ref):
        c, s1, s2 = c_ref[...], s1_ref[...], s2_ref[...]
        for h in range(nh):
            sl = slice(h * HEAD_DIM, (h + 1) * HEAD_DIM)
            o_ref[:, sl] = (_rope_t(dqd_ref[:, sl], c, s1, s2) * scale).astype(BF)
            o_ref[:, hd + h * HEAD_DIM:hd + (h + 1) * HEAD_DIM] = _rope_t(dkd_ref[:, sl], c, s1, s2).astype(BF)
        o_ref[:, 2 * hd:3 * hd] = dvd_ref[...].astype(BF)
        o_ref[:, 3 * hd:4 * hd] = (dqf_ref[...] * scale).astype(BF)
        o_ref[:, 4 * hd:5 * hd] = dkf_ref[...].astype(BF)
        o_ref[:, 5 * hd:6 * hd] = dvf_ref[...].astype(BF)
        o_ref[:, 6 * hd:6 * hd + d] = dgd_ref[...]
        o_ref[:, 6 * hd + d:6 * hd + 2 * d] = dgf_ref[...]
        z = fl_ref[...] + b_ref[...]
        dfl = dlog_ref[...] * _sig(-z)
        o_ref[:, 6 * hd + 2 * d:] = dfl.astype(BF)

        @pl.when(pl.program_id(0) == 0)
        def _():
            db_ref[...] = jnp.zeros_like(db_ref)

        db_ref[...] += jnp.sum(dfl, axis=0, keepdims=True)

    head = pl.BlockSpec((tr, hd), lambda i: (i, 0))
    row = pl.BlockSpec((tr, d), lambda i: (i, 0))
    lane_row = pl.BlockSpec((tr, LANE), lambda i: (i, 0))
    lane_vec = pl.BlockSpec((1, LANE), lambda i: (0, 0))
    return pl.pallas_call(
        body, name="assemble_dproj", grid=(t // tr,),
        in_specs=[head] * 6 + [row, row, lane_row, pl.BlockSpec((tr, LANE), lambda i: (i, f_blk)),
                               lane_row, lane_row, lane_row, lane_vec],
        out_specs=[pl.BlockSpec((tr, np_), lambda i: (i, 0)), lane_vec],
        out_shape=[jax.ShapeDtypeStruct((t, np_), BF), jax.ShapeDtypeStruct((1, LANE), F32)],
        compiler_params=_params(),
    )(dqd, dkd, dvd, dqf, dkf, dvf, dgd, dgf, dlogf, proj, *tables, bf_pad)


def _to_rows(a, tq):
    h, t = a.shape
    return a.reshape(h, t // tq, 1, tq)


def kernel(x, ffn1_norm, ffn1_w_gate, ffn1_w_up, ffn1_w_down, mix_norm, w_in, b_forget, b_gate_dil, b_gate_fox, w_proj_dil, w_proj_fox, w_out, ffn2_norm, ffn2_w_gate, ffn2_w_up, ffn2_w_down, final_norm, loss_target, m_ffn1_norm, m_ffn1_w_gate, m_ffn1_w_up, m_ffn1_w_down, m_mix_norm, m_w_in, m_b_forget, m_b_gate_dil, m_b_gate_fox, m_w_proj_dil, m_w_proj_fox, m_w_out, m_ffn2_norm, m_ffn2_w_gate, m_ffn2_w_up, m_ffn2_w_down, m_final_norm, v_ffn1_norm, v_ffn1_w_gate, v_ffn1_w_up, v_ffn1_w_down, v_mix_norm, v_w_in, v_b_forget, v_b_gate_dil, v_b_gate_fox, v_w_proj_dil, v_w_proj_fox, v_w_out, v_ffn2_norm, v_ffn2_w_gate, v_ffn2_w_up, v_ffn2_w_down, v_final_norm):
    t, d = x.shape[1], x.shape[2]
    hd = w_proj_dil.shape[1]
    nh = hd // HEAD_DIM
    n_f = b_forget.shape[1]
    cols = w_in.shape[2]
    in_cols = N_DEV * cols
    assert in_cols == 6 * hd + n_f + 2 * d and n_f == nh and n_f <= LANE
    np_ = 6 * hd + 2 * d + LANE
    scale = HEAD_DIM ** -0.5
    tq = _tile(t, 512, LANE)
    assert MAX_WINDOW % tq == 0 and tq % 16 == 0

    x2d = x[0]
    tgt = loss_target[0]

    ag_order = [ffn1_w_gate, ffn1_w_up, ffn1_w_down, w_in, w_proj_dil, w_proj_fox, w_out,
                ffn2_w_gate, ffn2_w_up, ffn2_w_down]
    ag, ag_token = _exchange_start([w[0].astype(BF) for w in ag_order], True, "ag_start")

    def gathered(idx, after, name):
        return _exchange_wait([ag[i] for i in idx], True, after, name)

    tables = _rope_tables(t)
    bf_pad = jnp.pad(b_forget, ((0, 0), (0, LANE - n_f)))

    hn1 = _rms_fwd(x2d, ffn1_norm, "rms_ffn1", dep=ag_token)
    wg1, wu1 = gathered([0, 1], hn1, "ag_wait_ffn1_gate_up")
    g1, u1, a1 = _ffn_gate_up(hn1, wg1, wu1, "ffn1_gate_up")
    wd1, = gathered([2], a1, "ag_wait_ffn1_down")
    x1 = _ffn_down(a1, wd1, x2d, "ffn1_down")

    hm = _rms_fwd(x1, mix_norm, "rms_mix")
    win_g, = gathered([3], hm, "ag_wait_w_in")
    win_full = win_g.transpose(1, 0, 2).reshape(d, in_cols)
    win_p = jnp.concatenate([win_full[:, :6 * hd], win_full[:, 6 * hd + n_f:], win_full[:, 6 * hd:6 * hd + n_f],
                             jnp.zeros((d, LANE - n_f), BF)], axis=1)
    proj = _mm_nn(hm, win_p, F32, "w_in_fwd")
    qd, kd, vd, qf, kf, vf, logf = _mixer_prep(proj, tables, bf_pad, hd, scale)
    csum = _cumsum_rows(logf, False, "cumsum_logf")
    c_heads = csum[:, :nh].T
    c_row = _to_rows(c_heads, tq)
    c_rep = jnp.broadcast_to(c_heads[:, :, None], (nh, t, LANE))
    dil_bias = _dil_bias_tiles(tq)
    dil_bias_t = dil_bias.transpose(0, 2, 1)
    yd, lse_d, lse_d_row = _attn_fwd("dil", qd, kd, vd, dil_bias, tq, "attn_dil_fwd")
    yf, lse_f, lse_f_row = _attn_fwd("fox", qf, kf, vf, c_row, tq, "attn_fox_fwd")
    wpd_g, wpf_g = gathered([4, 5], yf, "ag_wait_proj")
    wpd = wpd_g.transpose(1, 0, 2).reshape(hd, d)
    wpf = wpf_g.transpose(1, 0, 2).reshape(hd, d)
    pd = _mm_nn(yd, wpd, F32, "proj_dil_fwd", tn_pref=1024)
    pf = _mm_nn(yf, wpf, F32, "proj_fox_fwd", tn_pref=1024)
    merged = _merge_fwd(pd, pf, proj, b_gate_dil, b_gate_fox, hd)
    wout_g, = gathered([6], merged, "ag_wait_w_out")
    wout = wout_g.reshape(d, d)
    x2 = _mm_nn(merged, wout, F32, "w_out_fwd", residual=x1, tn_pref=1024)

    hn2 = _rms_fwd(x2, ffn2_norm, "rms_ffn2")
    wg2, wu2 = gathered([7, 8], hn2, "ag_wait_ffn2_gate_up")
    g2, u2, a2 = _ffn_gate_up(hn2, wg2, wu2, "ffn2_gate_up")
    wd2, = gathered([9], a2, "ag_wait_ffn2_down")
    x3 = _ffn_down(a2, wd2, x2, "ffn2_down")

    dx3, dx3b, d_final, loss_lanes = _loss_head(x3, final_norm.reshape(1, d), tgt)

    def ffn_bwd(dxb, hn, g, u, a, wg, wu, wd, tag):
        dg, du = _ffn_bwd_hidden(dxb, wd, g, u, tag + "_bwd_hidden")
        dwd = _ffn_dw_down(a, dxb, tag + "_dw_down")
        rs_down, tok = _exchange_start([dwd], False, "rs_start_" + tag + "_down")
        dwg, dwu = _ffn_dw_gate_up(hn, dg, du, tag + "_dw_gate_up", dep=tok)
        rs_gu, tok = _exchange_start([dwg, dwu], False, "rs_start_" + tag + "_gate_up")
        dhn = _ffn_bwd_input(dg, du, wg, wu, tag + "_bwd_input", dep=tok)
        return dhn, rs_gu + rs_down

    dhn2, rs_ffn2 = ffn_bwd(dx3b, hn2, g2, u2, a2, wg2, wu2, wd2, "ffn2")
    dx2, dx2b, d_ffn2_norm = _rms_bwd(dhn2, x2, ffn2_norm, dx3, "rms_ffn2_bwd")

    dmerged = _mm_nt(dx2b, wout, F32, "w_out_bwd")
    dwout = _mm_tn(merged, dx2b, BF, "w_out_dw", tn_pref=1024)
    dpd, dpf, dgd, dgf, d_bd, d_bf = _merge_bwd(dmerged, pd, pf, proj, b_gate_dil, b_gate_fox, hd)
    dyd = _mm_nt(dpd, wpd, BF, "proj_dil_bwd")
    dyf = _mm_nt(dpf, wpf, BF, "proj_fox_bwd")
    dwpd = _mm_tn(yd, dpd, BF, "proj_dil_dw", tn_pref=1024)
    dwpf = _mm_tn(yf, dpf, BF, "proj_fox_dw", tn_pref=1024)
    dwpd_c = dwpd.reshape(hd, N_DEV, d // N_DEV).transpose(1, 0, 2)
    dwpf_c = dwpf.reshape(hd, N_DEV, d // N_DEV).transpose(1, 0, 2)
    dwout_c = dwout.reshape(N_DEV, d // N_DEV, d)
    rs_mix, tok = _exchange_start([dwout_c, dwpd_c, dwpf_c], False, "rs_start_mixer")

    dqd, dl_d = _attn_bwd_dq("dil", qd, kd, vd, yd, dyd, lse_d, dil_bias, tq, "attn_dil_dq", dep=tok)
    dkd, dvd = _attn_bwd_dkv("dil", qd, kd, vd, dyd, lse_d_row, dl_d, dil_bias_t, None, tq, "attn_dil_dkv")
    dqf, dl_f = _attn_bwd_dq("fox", qf, kf, vf, yf, dyf, lse_f, c_row, tq, "attn_fox_dq")
    dkf, dvf, dc = _attn_bwd_dkv("fox", qf, kf, vf, dyf, lse_f_row, dl_f, c_rep, c_row, tq, "attn_fox_dkv")
    dc_pad = jnp.pad(dc[:, :, 0, :].reshape(nh, t).T, ((0, 0), (0, LANE - nh)))
    dlogf = _cumsum_rows(dc_pad, True, "revcumsum_dc")
    dproj, d_bforget = _assemble_dproj(dqd, dkd, dvd, dqf, dkf, dvf, dgd, dgf, dlogf, proj, tables, bf_pad, scale)

    dwin_p = _mm_tn(hm, dproj, BF, "w_in_dw")
    dwin_full = jnp.concatenate([dwin_p[:, :6 * hd], dwin_p[:, 6 * hd + 2 * d:6 * hd + 2 * d + n_f],
                                 dwin_p[:, 6 * hd:6 * hd + 2 * d]], axis=1)
    dwin_c = dwin_full.reshape(d, N_DEV, cols).transpose(1, 0, 2)
    rs_win, tok = _exchange_start([dwin_c], False, "rs_start_w_in")
    dhm = _mm_nt(dproj, win_p, F32, "w_in_bwd", tn_pref=2048, tk_pref=1152)
    dx1, dx1b, d_mix_norm = _rms_bwd(dhm, x1, mix_norm, dx2, "rms_mix_bwd", dep=tok)

    dhn1, rs_ffn1 = ffn_bwd(dx1b, hn1, g1, u1, a1, wg1, wu1, wd1, "ffn1")
    grad_x, _, d_ffn1_norm = _rms_bwd(dhn1, x2d, ffn1_norm, dx1, "rms_ffn1_bwd")

    def update(handles, names, after, tag):
        recvs = _exchange_wait(handles, False, after, "rs_wait_" + tag)
        res = {}
        for recv, n in zip(recvs, names):
            w, m, v = wmv[n]
            g, delta, m2, v2 = _adam_from_partials(recv, w[0], m[0], v[0], "adam_" + n)
            res[n] = (g[None], delta[None], m2[None], v2[None])
        return res, g

    wmv = {
        "ffn1_w_gate": (ffn1_w_gate, m_ffn1_w_gate, v_ffn1_w_gate),
        "ffn1_w_up": (ffn1_w_up, m_ffn1_w_up, v_ffn1_w_up),
        "ffn1_w_down": (ffn1_w_down, m_ffn1_w_down, v_ffn1_w_down),
        "w_in": (w_in, m_w_in, v_w_in),
        "w_proj_dil": (w_proj_dil, m_w_proj_dil, v_w_proj_dil),
        "w_proj_fox": (w_proj_fox, m_w_proj_fox, v_w_proj_fox),
        "w_out": (w_out, m_w_out, v_w_out),
        "ffn2_w_gate": (ffn2_w_gate, m_ffn2_w_gate, v_ffn2_w_gate),
        "ffn2_w_up": (ffn2_w_up, m_ffn2_w_up, v_ffn2_w_up),
        "ffn2_w_down": (ffn2_w_down, m_ffn2_w_down, v_ffn2_w_down),
    }
    big = {}
    after = grad_x
    for handles, names, tag in [
            (rs_ffn2, ["ffn2_w_gate", "ffn2_w_up", "ffn2_w_down"], "ffn2"),
            (rs_mix, ["w_out", "w_proj_dil", "w_proj_fox"], "mixer"),
            (rs_win, ["w_in"], "w_in"),
            (rs_ffn1, ["ffn1_w_gate", "ffn1_w_up", "ffn1_w_down"], "ffn1")]:
        res, after = update(handles, names, after, tag)
        big.update(res)

    def lanes(a):
        a = a.reshape(1, -1)
        return jnp.pad(a, ((0, 0), (0, d - a.shape[1])))

    small_names = ["ffn1_norm", "mix_norm", "b_gate_dil", "b_gate_fox", "ffn2_norm", "final_norm", "b_forget"]
    small_g = [d_ffn1_norm, d_mix_norm, d_bd, d_bf, d_ffn2_norm, d_final, d_bforget[:, :n_f]]
    small_w = [ffn1_norm, mix_norm, b_gate_dil, b_gate_fox, ffn2_norm, final_norm, b_forget]
    small_m = [m_ffn1_norm, m_mix_norm, m_b_gate_dil, m_b_gate_fox, m_ffn2_norm, m_final_norm, m_b_forget]
    small_v = [v_ffn1_norm, v_mix_norm, v_b_gate_dil, v_b_gate_fox, v_ffn2_norm, v_final_norm, v_b_forget]
    pack = lambda arrs, last: jnp.concatenate([lanes(a) for a in arrs] + [last], axis=0)
    g_all = _allreduce_small(pack(small_g, loss_lanes))
    zero_row = jnp.zeros((1, d), F32)
    one_row = jnp.ones((1, d), F32)
    s_delta, s_m, s_v = _adam_small(g_all, pack(small_w, zero_row), pack(small_m, zero_row), pack(small_v, one_row))
    loss = g_all[len(small_names), 0]

    def unpack(packed, i, like):
        return packed[i, :like.size].reshape(like.shape)

    small = {}
    for i, (n, w) in enumerate(zip(small_names, small_w)):
        small[n] = (unpack(g_all, i, w), unpack(s_delta, i, w), unpack(s_m, i, w), unpack(s_v, i, w))

    order = ["ffn1_norm", "ffn1_w_gate", "ffn1_w_up", "ffn1_w_down", "mix_norm", "w_in", "b_forget", "b_gate_dil",
             "b_gate_fox", "w_proj_dil", "w_proj_fox", "w_out", "ffn2_norm", "ffn2_w_gate", "ffn2_w_up",
             "ffn2_w_down", "final_norm"]
    res = {**big, **small}
    outs = [loss, grad_x[None]]
    for slot in range(4):
        outs += [res[n][slot] for n in order]
    return tuple(outs)
```

```python
import functools

import numpy as np
import jax
import jax.numpy as jnp
from jax import lax
from jax.experimental import pallas as pl
from jax.experimental.pallas import tpu as pltpu

BF = jnp.bfloat16
F32 = jnp.float32
MESH = pl.DeviceIdType.MESH
N_DEV = 8

HEAD_DIM = 128
ROPE_DIM = HEAD_DIM // 4
ROPE_HALF = ROPE_DIM // 2
ROPE_THETA = 500000.0
NORM_EPS = 1e-6
DIL_PATTERNS = ((128, 1), (512, 4), (2048, 16))
MAX_WINDOW = 2048
LANE = 128
NEG = -1e30

ADAM_LR = 0.001
ADAM_B1 = 0.9
ADAM_B2 = 0.999
ADAM_EPS = 1e-08
ADAM_WD = 0.01
ADAM_STEP = 10

VMEM_LIMIT_BYTES = 56 * 1024 * 1024
ANY = pl.BlockSpec(memory_space=pl.ANY)

NN = (((1,), (0,)), ((), ()))
NT = (((1,), (1,)), ((), ()))
TN = (((0,), (0,)), ((), ()))


def _dot(a, b, dn=NN):
    return lax.dot_general(a, b, dn, preferred_element_type=F32)


def _sig(x):
    return 1.0 / (1.0 + jnp.exp(-x))


def _tile(n, pref, align):
    best = None
    t = align
    while t <= min(n, pref):
        if n % t == 0:
            best = t
        t += align
    return n if best is None else best


def _params():
    return pltpu.CompilerParams(vmem_limit_bytes=VMEM_LIMIT_BYTES)


def _call(body, args, dep=None, **kw):
    if dep is not None:
        n_in = len(args)
        inner = body

        def body(*refs):
            inner(*refs[:n_in], *refs[n_in + 1:])

        kw["in_specs"] = list(kw["in_specs"]) + [ANY]
        args = list(args) + [dep]
    return pl.pallas_call(body, **kw)(*args)


def _peers():
    x, y, c = lax.axis_index("x"), lax.axis_index("y"), lax.axis_index("c")
    me = 4 * x + 2 * y + c
    peers = []
    for k in range(1, N_DEV):
        px = 1 - x if (k >> 2) & 1 else x
        py = 1 - y if (k >> 1) & 1 else y
        pc = 1 - c if k & 1 else c
        peers.append((k, (px, py, pc), 4 * px + 2 * py + pc))
    return me, peers


HBM = pl.BlockSpec(memory_space=pltpu.HBM)
SEM = pl.BlockSpec(memory_space=pltpu.SEMAPHORE)
EFFECT = pltpu.SideEffectType.DATAFLOW_SIDE_EFFECTING


def _exchange_copy(gather, src_ref, land_ref, send_sems, recv_sems, me, k, peer, peer_flat, landing):
    return pltpu.make_async_remote_copy(
        src_ref=src_ref if gather else src_ref.at[peer_flat], dst_ref=land_ref.at[landing],
        send_sem=send_sems.at[k], recv_sem=recv_sems.at[k], device_id=peer, device_id_type=MESH)


def _exchange_start(srcs, gather, name):
    n = len(srcs)

    def body(*refs):
        src_refs, land_refs = refs[:n], refs[n:2 * n]
        send_refs, recv_refs = refs[2 * n:3 * n], refs[3 * n:4 * n]
        token = refs[6 * n]
        me, peers = _peers()
        for i in range(n):
            for k, peer, peer_flat in peers:
                _exchange_copy(gather, src_refs[i], land_refs[i], send_refs[i], recv_refs[i],
                               me, k, peer, peer_flat, me).start()
        token[...] = jnp.zeros_like(token)

    lands = [lax.empty((N_DEV,) + s.shape[-2:], s.dtype) for s in srcs]
    sems = [pltpu.SemaphoreType.DMA((N_DEV,)) for _ in range(2 * n)]
    out = pl.pallas_call(
        body, name=name,
        out_shape=tuple(sems) + tuple(pltpu.HBM(a.shape, a.dtype) for a in list(srcs) + lands)
        + (jax.ShapeDtypeStruct((8, LANE), F32),),
        in_specs=[HBM] * (2 * n),
        out_specs=tuple([SEM] * (2 * n) + [HBM] * (2 * n) + [pl.BlockSpec(memory_space=pltpu.VMEM)]),
        input_output_aliases={i: 2 * n + i for i in range(2 * n)},
        compiler_params=pltpu.CompilerParams(has_side_effects=EFFECT),
    )(*[pltpu.with_memory_space_constraint(a, pltpu.HBM) for a in list(srcs) + lands])
    handles = [(out[2 * n + i], out[3 * n + i], out[i], out[n + i]) for i in range(n)]
    return handles, out[4 * n]


def _exchange_wait(handles, gather, after, name):
    n = len(handles)

    def body(*refs):
        src_refs, land_refs = refs[:n], refs[n:2 * n]
        send_refs, recv_refs = refs[2 * n:3 * n], refs[3 * n:4 * n]
        me, peers = _peers()
        for i in range(n):
            for k, peer, peer_flat in peers:
                cp = _exchange_copy(gather, src_refs[i], land_refs[i], send_refs[i], recv_refs[i],
                                    me, k, peer, peer_flat, peer_flat)
                cp.wait_send()
                cp.wait_recv()

    srcs = [h[0] for h in handles]
    lands = [h[1] for h in handles]
    out = pl.pallas_call(
        body, name=name,
        out_shape=tuple(pltpu.HBM(a.shape, a.dtype) for a in srcs + lands),
        in_specs=[HBM] * (2 * n) + [SEM] * (2 * n) + [ANY],
        out_specs=tuple([HBM] * (2 * n)),
        input_output_aliases={i: i for i in range(2 * n)},
        compiler_params=pltpu.CompilerParams(has_side_effects=EFFECT),
    )(*srcs, *lands, *[h[2] for h in handles], *[h[3] for h in handles], after)
    me = 4 * lax.axis_index("x") + 2 * lax.axis_index("y") + lax.axis_index("c")
    filled = []
    for src, land in zip(out[:n], out[n:]):
        own = src[None] if gather else lax.dynamic_slice_in_dim(src, me, 1, axis=0)
        filled.append(lax.dynamic_update_slice_in_dim(land, own, me, axis=0))
    return filled


def _allreduce_small(p):
    rows, d = p.shape

    def body(p_ref, o_ref, recv_ref, send_sems, recv_sems):
        me, peers = _peers()
        recv_ref[me] = p_ref[...]
        sends = []
        for k, peer, peer_flat in peers:
            cp = pltpu.make_async_remote_copy(
                src_ref=p_ref, dst_ref=recv_ref.at[me],
                send_sem=send_sems.at[k], recv_sem=recv_sems.at[k],
                device_id=peer, device_id_type=MESH)
            cp.start()
            sends.append(cp)
        for k, peer, peer_flat in peers:
            pltpu.make_async_remote_copy(
                src_ref=p_ref, dst_ref=recv_ref.at[peer_flat],
                send_sem=send_sems.at[k], recv_sem=recv_sems.at[k],
                device_id=peer, device_id_type=MESH).wait_recv()
        for cp in sends:
            cp.wait_send()
        acc = recv_ref[0]
        for s in range(1, N_DEV):
            acc = acc + recv_ref[s]
        is_loss = lax.broadcasted_iota(jnp.int32, (rows, d), 0) == rows - 1
        total = jnp.sum(jnp.where(is_loss, acc, 0.0))
        o_ref[...] = jnp.where(is_loss, total, acc)

    return pl.pallas_call(
        body, name="allreduce_small",
        out_shape=jax.ShapeDtypeStruct((rows, d), F32),
        in_specs=[pl.BlockSpec(memory_space=pltpu.VMEM)],
        out_specs=pl.BlockSpec(memory_space=pltpu.VMEM),
        scratch_shapes=[pltpu.VMEM((N_DEV, rows, d), F32),
                        pltpu.SemaphoreType.DMA((N_DEV,)), pltpu.SemaphoreType.DMA((N_DEV,))],
    )(p)


def _adam_math(w, g, m, v):
    m2 = ADAM_B1 * m + (1.0 - ADAM_B1) * g
    v2 = ADAM_B2 * v + (1.0 - ADAM_B2) * (g * g)
    m_hat = m2 / (1.0 - ADAM_B1 ** ADAM_STEP)
    v_hat = v2 / (1.0 - ADAM_B2 ** ADAM_STEP)
    delta = -ADAM_LR * (m_hat / (jnp.sqrt(v_hat) + ADAM_EPS) + ADAM_WD * w)
    return delta, m2, v2


def _adam_from_partials(parts, w, m, v, name):
    r, c = w.shape
    tr = _tile(r, 256, 16)

    def body(p_ref, w_ref, m_ref, v_ref, g_out, d_out, m_out, v_out):
        g = p_ref[0].astype(F32)
        for s in range(1, N_DEV):
            g = g + p_ref[s].astype(F32)
        delta, m2, v2 = _adam_math(w_ref[...], g, m_ref[...], v_ref[...])
        g_out[...] = g
        d_out[...] = delta
        m_out[...] = m2
        v_out[...] = v2

    blk = pl.BlockSpec((tr, c), lambda i: (i, 0))
    out = jax.ShapeDtypeStruct((r, c), F32)
    return pl.pallas_call(
        body, name=name, grid=(r // tr,),
        in_specs=[pl.BlockSpec((N_DEV, tr, c), lambda i: (0, i, 0)), blk, blk, blk],
        out_specs=[blk, blk, blk, blk], out_shape=[out, out, out, out],
        compiler_params=_params(),
    )(parts, w, m, v)


def _adam_small(g, w, m, v):
    def body(g_ref, w_ref, m_ref, v_ref, d_out, m_out, v_out):
        delta, m2, v2 = _adam_math(w_ref[...], g_ref[...], m_ref[...], v_ref[...])
        d_out[...] = delta
        m_out[...] = m2
        v_out[...] = v2

    out = jax.ShapeDtypeStruct(g.shape, F32)
    return pl.pallas_call(body, name="adam_small", out_shape=[out, out, out])(g, w, m, v)


def _rms_fwd(x, gain, name, dep=None):
    t, d = x.shape
    tr = _tile(t, 256, 16)

    def body(x_ref, g_ref, o_ref):
        xv = x_ref[...]
        r = lax.rsqrt(jnp.mean(xv * xv, axis=-1, keepdims=True) + NORM_EPS)
        o_ref[...] = (xv * r * g_ref[...]).astype(BF)

    return _call(
        body, [x, gain], dep=dep, name=name, grid=(t // tr,),
        in_specs=[pl.BlockSpec((tr, d), lambda i: (i, 0)), pl.BlockSpec((1, d), lambda i: (0, 0))],
        out_specs=pl.BlockSpec((tr, d), lambda i: (i, 0)),
        out_shape=jax.ShapeDtypeStruct((t, d), BF), compiler_params=_params(),
    )


def _rms_vjp(xv, gain, dy):
    r = lax.rsqrt(jnp.mean(xv * xv, axis=-1, keepdims=True) + NORM_EPS)
    xhat = xv * r
    dxhat = dy * gain
    dx = r * (dxhat - xhat * jnp.mean(dxhat * xhat, axis=-1, keepdims=True))
    dgain = jnp.sum(dy * xhat, axis=0, keepdims=True)
    return dx, dgain


def _rms_bwd(dy, x, gain, dres, name, dep=None):
    t, d = x.shape
    tr = _tile(t, 256, 16)

    def body(dy_ref, x_ref, g_ref, dres_ref, dx_ref, dxb_ref, dg_ref):
        dx, dgain = _rms_vjp(x_ref[...], g_ref[...], dy_ref[...])
        dx = dx + dres_ref[...]
        dx_ref[...] = dx
        dxb_ref[...] = dx.astype(BF)

        @pl.when(pl.program_id(0) == 0)
        def _():
            dg_ref[...] = jnp.zeros_like(dg_ref)

        dg_ref[...] += dgain

    row = pl.BlockSpec((tr, d), lambda i: (i, 0))
    vec = pl.BlockSpec((1, d), lambda i: (0, 0))
    return _call(
        body, [dy, x, gain, dres], dep=dep, name=name, grid=(t // tr,),
        in_specs=[row, row, vec, row], out_specs=[row, row, vec],
        out_shape=[jax.ShapeDtypeStruct((t, d), F32), jax.ShapeDtypeStruct((t, d), BF),
                   jax.ShapeDtypeStruct((1, d), F32)],
        compiler_params=_params(),
    )


def _loss_head(x, gain, target):
    t, d = x.shape
    tr = _tile(t, 256, 16)

    def body(x_ref, g_ref, t_ref, dx_ref, dxb_ref, dg_ref, loss_ref):
        xv = x_ref[...]
        gain = g_ref[...]
        r = lax.rsqrt(jnp.mean(xv * xv, axis=-1, keepdims=True) + NORM_EPS)
        err = xv * r * gain - t_ref[...]
        dx, dgain = _rms_vjp(xv, gain, err * (1.0 / d))
        dx_ref[...] = dx
        dxb_ref[...] = dx.astype(BF)

        @pl.when(pl.program_id(0) == 0)
        def _():
            dg_ref[...] = jnp.zeros_like(dg_ref)
            loss_ref[...] = jnp.zeros_like(loss_ref)

        dg_ref[...] += dgain
        loss_ref[...] += jnp.sum(err * err, axis=0, keepdims=True) * (0.5 / d)

    row = pl.BlockSpec((tr, d), lambda i: (i, 0))
    vec = pl.BlockSpec((1, d), lambda i: (0, 0))
    return pl.pallas_call(
        body, name="loss_head", grid=(t // tr,),
        in_specs=[row, vec, row], out_specs=[row, row, vec, vec],
        out_shape=[jax.ShapeDtypeStruct((t, d), F32), jax.ShapeDtypeStruct((t, d), BF),
                   jax.ShapeDtypeStruct((1, d), F32), jax.ShapeDtypeStruct((1, d), F32)],
        compiler_params=_params(),
    )(x, gain, target)


def _mm_nn(a, b, out_dtype, name, residual=None, tm_pref=512, tn_pref=1152):
    m, k = a.shape
    n = b.shape[1]
    tm, tn = _tile(m, tm_pref, 16), _tile(n, tn_pref, LANE)

    def body(*refs):
        if residual is None:
            a_ref, b_ref, o_ref = refs
            o_ref[...] = _dot(a_ref[...], b_ref[...]).astype(out_dtype)
        else:
            a_ref, b_ref, r_ref, o_ref = refs
            o_ref[...] = (r_ref[...] + _dot(a_ref[...], b_ref[...])).astype(out_dtype)

    in_specs = [pl.BlockSpec((tm, k), lambda j, i: (i, 0)), pl.BlockSpec((k, tn), lambda j, i: (0, j))]
    args = [a, b]
    if residual is not None:
        in_specs.append(pl.BlockSpec((tm, tn), lambda j, i: (i, j)))
        args.append(residual)
    return pl.pallas_call(
        body, name=name, grid=(n // tn, m // tm), in_specs=in_specs,
        out_specs=pl.BlockSpec((tm, tn), lambda j, i: (i, j)),
        out_shape=jax.ShapeDtypeStruct((m, n), out_dtype), compiler_params=_params(),
    )(*args)


def _mm_nt(a, b, out_dtype, name, tm_pref=512, tn_pref=1024, tk_pref=2048):
    m, k = a.shape
    n = b.shape[0]
    tm, tn, tk = _tile(m, tm_pref, 16), _tile(n, tn_pref, LANE), _tile(k, tk_pref, LANE)
    nk = k // tk

    def body(a_ref, b_ref, o_ref, acc_ref):
        kk = pl.program_id(2)

        @pl.when(kk == 0)
        def _():
            acc_ref[...] = jnp.zeros_like(acc_ref)

        acc_ref[...] += _dot(a_ref[...], b_ref[...], NT)

        @pl.when(kk == nk - 1)
        def _():
            o_ref[...] = acc_ref[...].astype(out_dtype)

    return pl.pallas_call(
        body, name=name, grid=(n // tn, m // tm, nk),
        in_specs=[pl.BlockSpec((tm, tk), lambda j, i, kk: (i, kk)),
                  pl.BlockSpec((tn, tk), lambda j, i, kk: (j, kk))],
        out_specs=pl.BlockSpec((tm, tn), lambda j, i, kk: (i, j)),
        out_shape=jax.ShapeDtypeStruct((m, n), out_dtype),
        scratch_shapes=[pltpu.VMEM((tm, tn), F32)], compiler_params=_params(),
    )(a, b)


def _mm_tn(a, b, out_dtype, name, tn_pref=1152, tk_pref=512):
    t, k = a.shape
    n = b.shape[1]
    tn, tk = _tile(n, tn_pref, LANE), _tile(t, tk_pref, 16)
    nt = t // tk

    def body(a_ref, b_ref, o_ref, acc_ref):
        tt = pl.program_id(1)

        @pl.when(tt == 0)
        def _():
            acc_ref[...] = jnp.zeros_like(acc_ref)

        acc_ref[...] += _dot(a_ref[...], b_ref[...], TN)

        @pl.when(tt == nt - 1)
        def _():
            o_ref[...] = acc_ref[...].astype(out_dtype)

    return pl.pallas_call(
        body, name=name, grid=(n // tn, nt),
        in_specs=[pl.BlockSpec((tk, k), lambda j, tt: (tt, 0)), pl.BlockSpec((tk, tn), lambda j, tt: (tt, j))],
        out_specs=pl.BlockSpec((k, tn), lambda j, tt: (0, j)),
        out_shape=jax.ShapeDtypeStruct((k, n), out_dtype),
        scratch_shapes=[pltpu.VMEM((k, tn), F32)], compiler_params=_params(),
    )(a, b)


def _ffn_gate_up(hn, wg, wu, name):
    t, d = hn.shape
    ns, _, f = wg.shape
    tm = _tile(t, 512, 16)

    def body(h_ref, wg_ref, wu_ref, g_ref, u_ref, a_ref):
        h = h_ref[...]
        g = _dot(h, wg_ref[...])
        u = _dot(h, wu_ref[...])
        g_ref[...] = g.astype(BF)
        u_ref[...] = u.astype(BF)
        a_ref[...] = (g * _sig(g) * u).astype(BF)

    wspec = pl.BlockSpec((None, d, f), lambda j, i: (j, 0, 0))
    hid = pl.BlockSpec((None, tm, f), lambda j, i: (j, i, 0))
    out = jax.ShapeDtypeStruct((ns, t, f), BF)
    return pl.pallas_call(
        body, name=name, grid=(ns, t // tm),
        in_specs=[pl.BlockSpec((tm, d), lambda j, i: (i, 0)), wspec, wspec],
        out_specs=[hid, hid, hid], out_shape=[out, out, out], compiler_params=_params(),
    )(hn, wg, wu)


def _ffn_down(act, wd, xres, name):
    ns, t, f = act.shape
    d = wd.shape[2]
    tm = _tile(t, 512, 16)

    def body(a_ref, w_ref, x_ref, o_ref):
        @pl.when(pl.program_id(1) == 0)
        def _():
            o_ref[...] = x_ref[...]

        o_ref[...] += 0.5 * _dot(a_ref[...], w_ref[...])

    row = pl.BlockSpec((tm, d), lambda i, j: (i, 0))
    return pl.pallas_call(
        body, name=name, grid=(t // tm, ns),
        in_specs=[pl.BlockSpec((None, tm, f), lambda i, j: (j, i, 0)),
                  pl.BlockSpec((None, f, d), lambda i, j: (j, 0, 0)), row],
        out_specs=row, out_shape=jax.ShapeDtypeStruct((t, d), F32), compiler_params=_params(),
    )(act, wd, xres)


def _ffn_bwd_hidden(dxb, wd, g, u, name):
    t, d = dxb.shape
    ns, f, _ = wd.shape
    tm = _tile(t, 512, 16)

    def body(dx_ref, w_ref, g_ref, u_ref, dg_ref, du_ref):
        dh = 0.5 * _dot(dx_ref[...], w_ref[...], NT)
        gv = g_ref[...].astype(F32)
        uv = u_ref[...].astype(F32)
        s = _sig(gv)
        dg_ref[...] = (dh * uv * (s * (1.0 + gv * (1.0 - s)))).astype(BF)
        du_ref[...] = (dh * (gv * s)).astype(BF)

    hid = pl.BlockSpec((None, tm, f), lambda j, i: (j, i, 0))
    out = jax.ShapeDtypeStruct((ns, t, f), BF)
    return pl.pallas_call(
        body, name=name, grid=(ns, t // tm),
        in_specs=[pl.BlockSpec((tm, d), lambda j, i: (i, 0)),
                  pl.BlockSpec((None, f, d), lambda j, i: (j, 0, 0)), hid, hid],
        out_specs=[hid, hid], out_shape=[out, out], compiler_params=_params(),
    )(dxb, wd, g, u)


def _ffn_dw_down(act, dxb, name):
    ns, t, f = act.shape
    d = dxb.shape[1]
    tk = _tile(t, 512, 16)
    nt = t // tk

    def body(a_ref, dx_ref, o_ref, acc_ref):
        tt = pl.program_id(1)

        @pl.when(tt == 0)
        def _():
            acc_ref[...] = jnp.zeros_like(acc_ref)

        acc_ref[...] += _dot(a_ref[...], dx_ref[...], TN)

        @pl.when(tt == nt - 1)
        def _():
            o_ref[...] = (0.5 * acc_ref[...]).astype(BF)

    return pl.pallas_call(
        body, name=name, grid=(ns, nt),
        in_specs=[pl.BlockSpec((None, tk, f), lambda j, tt: (j, tt, 0)),
                  pl.BlockSpec((tk, d), lambda j, tt: (tt, 0))],
        out_specs=pl.BlockSpec((None, f, d), lambda j, tt: (j, 0, 0)),
        out_shape=jax.ShapeDtypeStruct((ns, f, d), BF),
        scratch_shapes=[pltpu.VMEM((f, d), F32)], compiler_params=_params(),
    )(act, dxb)


def _ffn_dw_gate_up(hn, dg, du, name, dep=None):
    t, d = hn.shape
    ns, _, f = dg.shape
    tk = _tile(t, 512, 16)
    nt = t // tk

    def body(h_ref, dg_ref, du_ref, og_ref, ou_ref, accg_ref, accu_ref):
        tt = pl.program_id(1)

        @pl.when(tt == 0)
        def _():
            accg_ref[...] = jnp.zeros_like(accg_ref)
            accu_ref[...] = jnp.zeros_like(accu_ref)

        h = h_ref[...]
        accg_ref[...] += _dot(h, dg_ref[...], TN)
        accu_ref[...] += _dot(h, du_ref[...], TN)

        @pl.when(tt == nt - 1)
        def _():
            og_ref[...] = accg_ref[...].astype(BF)
            ou_ref[...] = accu_ref[...].astype(BF)

    hid = pl.BlockSpec((None, tk, f), lambda j, tt: (j, tt, 0))
    wspec = pl.BlockSpec((None, d, f), lambda j, tt: (j, 0, 0))
    out = jax.ShapeDtypeStruct((ns, d, f), BF)
    return _call(
        body, [hn, dg, du], dep=dep, name=name, grid=(ns, nt),
        in_specs=[pl.BlockSpec((tk, d), lambda j, tt: (tt, 0)), hid, hid],
        out_specs=[wspec, wspec], out_shape=[out, out],
        scratch_shapes=[pltpu.VMEM((d, f), F32), pltpu.VMEM((d, f), F32)], compiler_params=_params(),
    )


def _ffn_bwd_input(dg, du, wg, wu, name, dep=None):
    ns, t, f = dg.shape
    d = wg.shape[1]
    tm = _tile(t, 512, 16)

    def body(dg_ref, du_ref, wg_ref, wu_ref, o_ref):
        @pl.when(pl.program_id(1) == 0)
        def _():
            o_ref[...] = jnp.zeros_like(o_ref)

        o_ref[...] += _dot(dg_ref[...], wg_ref[...], NT) + _dot(du_ref[...], wu_ref[...], NT)

    hid = pl.BlockSpec((None, tm, f), lambda i, j: (j, i, 0))
    wspec = pl.BlockSpec((None, d, f), lambda i, j: (j, 0, 0))
    return _call(
        body, [dg, du, wg, wu], dep=dep, name=name, grid=(t // tm, ns),
        in_specs=[hid, hid, wspec, wspec],
        out_specs=pl.BlockSpec((tm, d), lambda i, j: (i, 0)),
        out_shape=jax.ShapeDtypeStruct((t, d), F32), compiler_params=_params(),
    )


def _rope_tables(t):
    pos = jnp.arange(t, dtype=F32)
    inv_freq = ROPE_THETA ** (-jnp.arange(0, ROPE_DIM, 2, dtype=F32) / ROPE_DIM)
    ang = pos[:, None] * inv_freq[None, :]
    cos, sin = jnp.cos(ang), jnp.sin(ang)
    rest = HEAD_DIM - ROPE_DIM
    one = jnp.ones((t, rest), F32)
    zero_h = jnp.zeros((t, ROPE_HALF), F32)
    zero_r = jnp.zeros((t, rest), F32)
    c = jnp.concatenate([cos, cos, one], axis=1)
    s1 = jnp.concatenate([-sin, zero_h, zero_r], axis=1)
    s2 = jnp.concatenate([zero_h, sin, zero_r], axis=1)
    return c, s1, s2


def _rope(xh, c, s1, s2):
    return xh * c + pltpu.roll(xh, HEAD_DIM - ROPE_HALF, 1) * s1 + pltpu.roll(xh, ROPE_HALF, 1) * s2


def _rope_t(dh, c, s1, s2):
    return dh * c + pltpu.roll(dh * s1, ROPE_HALF, 1) + pltpu.roll(dh * s2, HEAD_DIM - ROPE_HALF, 1)


def _mixer_prep(proj, tables, bf_pad, hd, scale):
    t, np_ = proj.shape
    tr = _tile(t, 256, 16)
    nh = hd // HEAD_DIM
    nblk = hd // LANE
    f_blk = np_ // LANE - 1

    def body(qd_ref, kd_ref, vd_ref, qf_ref, kf_ref, vf_ref, fl_ref, c_ref, s1_ref, s2_ref, b_ref,
             oqd, okd, ovd, oqf, okf, ovf, olog):
        c, s1, s2 = c_ref[...], s1_ref[...], s2_ref[...]
        for h in range(nh):
            sl = slice(h * HEAD_DIM, (h + 1) * HEAD_DIM)
            oqd[:, sl] = (_rope(qd_ref[:, sl], c, s1, s2) * scale).astype(BF)
            okd[:, sl] = _rope(kd_ref[:, sl], c, s1, s2).astype(BF)
        ovd[...] = vd_ref[...].astype(BF)
        oqf[...] = (qf_ref[...] * scale).astype(BF)
        okf[...] = kf_ref[...].astype(BF)
        ovf[...] = vf_ref[...].astype(BF)
        z = fl_ref[...] + b_ref[...]
        olog[...] = jnp.minimum(z, 0.0) - jnp.log(1.0 + jnp.exp(-jnp.abs(z)))

    def col(kblk):
        return pl.BlockSpec((tr, hd), lambda i, kblk=kblk: (i, kblk))

    lane_row = pl.BlockSpec((tr, LANE), lambda i: (i, 0))
    in_specs = [col(0), col(1), col(2), col(3), col(4), col(5),
                pl.BlockSpec((tr, LANE), lambda i: (i, f_blk)),
                lane_row, lane_row, lane_row, pl.BlockSpec((1, LANE), lambda i: (0, 0))]
    o = pl.BlockSpec((tr, hd), lambda i: (i, 0))
    ob = jax.ShapeDtypeStruct((t, hd), BF)
    del nblk
    return pl.pallas_call(
        body, name="mixer_prep", grid=(t // tr,), in_specs=in_specs,
        out_specs=[o, o, o, o, o, o, lane_row],
        out_shape=[ob, ob, ob, ob, ob, ob, jax.ShapeDtypeStruct((t, LANE), F32)],
        compiler_params=_params(),
    )(proj, proj, proj, proj, proj, proj, proj, *tables, bf_pad)


def _split3(x):
    x1 = x.astype(BF)
    r1 = x - x1.astype(F32)
    x2 = r1.astype(BF)
    x3 = (r1 - x2.astype(F32)).astype(BF)
    return x1, x2, x3


def _cumsum_rows(x, reverse, name):
    t, w = x.shape
    blk = LANE
    nb = t // blk

    def body(x_ref, o_ref):
        r = lax.broadcasted_iota(jnp.int32, (blk, blk), 0)
        c = lax.broadcasted_iota(jnp.int32, (blk, blk), 1)
        tri = jnp.where((c >= r) if reverse else (c <= r), 1.0, 0.0).astype(BF)

        def step(i, carry):
            b = (nb - 1 - i) if reverse else i
            off = pl.multiple_of(b * blk, blk)
            xb = x_ref[pl.ds(off, blk), :]
            x1, x2, x3 = _split3(xb)
            o_ref[pl.ds(off, blk), :] = _dot(tri, x1) + _dot(tri, x2) + _dot(tri, x3) + carry
            return carry + jnp.sum(xb, axis=0, keepdims=True)

        lax.fori_loop(0, nb, step, jnp.zeros((1, w), F32))

    return pl.pallas_call(body, name=name, out_shape=jax.ShapeDtypeStruct((t, w), F32),
                          compiler_params=_params())(x)


ATTN_ROWS = 16


def _dil_bias_tiles(tq):
    nbias = MAX_WINDOW // tq + 1
    b = lax.broadcasted_iota(jnp.int32, (nbias, tq, tq), 0)
    i = lax.broadcasted_iota(jnp.int32, (nbias, tq, tq), 1)
    j = lax.broadcasted_iota(jnp.int32, (nbias, tq, tq), 2)
    delta = b * tq + i - j
    mult = jnp.zeros((nbias, tq, tq), F32)
    for w, dil in DIL_PATTERNS:
        mult = mult + jnp.where((delta >= 0) & (delta <= w) & (delta % dil == 0), 1.0, 0.0)
    return jnp.where(mult > 0.0, jnp.log(jnp.maximum(mult, 1.0)), NEG)


def _rep(x, width):
    return jnp.tile(x, (1, width // LANE))


def _chunks(n_rows, fn):
    for c in range(n_rows // ATTN_ROWS):
        fn(c * ATTN_ROWS)


def _causal(r0, tq, transposed):
    a = lax.broadcasted_iota(jnp.int32, (ATTN_ROWS, tq), 0) + r0
    b = lax.broadcasted_iota(jnp.int32, (ATTN_ROWS, tq), 1)
    return (a <= b) if transposed else (b <= a)


def _rows8(x):
    return jnp.transpose(x)[:8, :]


def _attn_fwd(mode, q, k, v, bias, tq, name):
    t, hd = q.shape
    nh = hd // HEAD_DIM
    nb = t // tq
    wb = MAX_WINDOW // tq
    fox = mode == "fox"

    def body(q_ref, k_ref, v_ref, b_ref, o_ref, lse_ref, lse_row_ref, s_ref, p_ref, m_ref, l_ref, acc_ref):
        qi = pl.program_id(1)
        qb = q_ref[...]
        m_ref[...] = jnp.full_like(m_ref, NEG)
        l_ref[...] = jnp.zeros_like(l_ref)
        acc_ref[...] = jnp.zeros_like(acc_ref)

        def tile(kj, diag):
            off = pl.multiple_of(kj * tq, tq)
            s_ref[...] = _dot(qb, k_ref[pl.ds(off, tq), :], NT)
            if fox:
                brow = b_ref[qi][:, :1] - b_ref[kj]

            def chunk(r0):
                rows = pl.ds(r0, ATTN_ROWS)
                if fox:
                    s = s_ref[rows, :] + brow
                    if diag:
                        s = jnp.where(_causal(r0, tq, False), s, NEG)
                else:
                    s = s_ref[rows, :] + b_ref[qi - kj, rows, :]
                m_old = m_ref[rows, :]
                m_new = jnp.maximum(m_old, jnp.max(s, axis=1, keepdims=True))
                p = jnp.exp(s - _rep(m_new, tq))
                alpha = jnp.exp(m_old - m_new)
                l_ref[rows, :] = alpha * l_ref[rows, :] + jnp.sum(p, axis=1, keepdims=True)
                m_ref[rows, :] = m_new
                acc_ref[rows, :] = alpha * acc_ref[rows, :]
                p_ref[rows, :] = p.astype(BF)

            _chunks(tq, chunk)
            acc_ref[...] += _dot(p_ref[...], v_ref[pl.ds(off, tq), :])

        tile(qi, True)
        if fox:
            lax.fori_loop(0, qi, lambda kj, c: (tile(kj, False), c)[1], 0)
        else:
            lax.fori_loop(1, jnp.minimum(qi, wb) + 1, lambda i, c: (tile(qi - i, False), c)[1], 0)
        o_ref[...] = (acc_ref[...] / l_ref[...]).astype(BF)
        lse = m_ref[...] + jnp.log(l_ref[...])
        lse_ref[...] = lse
        lse_row_ref[...] = _rows8(lse)

    qspec = pl.BlockSpec((tq, HEAD_DIM), lambda h, i: (i, h))
    kvspec = pl.BlockSpec((t, HEAD_DIM), lambda h, i: (0, h))
    repspec = pl.BlockSpec((None, tq, LANE), lambda h, i: (h, i, 0))
    row8spec = pl.BlockSpec((None, None, 8, tq), lambda h, i: (h, i, 0, 0))
    if fox:
        bspec = pl.BlockSpec((None, nb, 1, tq), lambda h, i: (h, 0, 0, 0))
    else:
        bspec = pl.BlockSpec((wb + 1, tq, tq), lambda h, i: (0, 0, 0))
    return pl.pallas_call(
        body, name=name, grid=(nh, nb), in_specs=[qspec, kvspec, kvspec, bspec],
        out_specs=[qspec, repspec, row8spec],
        out_shape=[jax.ShapeDtypeStruct((t, hd), BF), jax.ShapeDtypeStruct((nh, t, LANE), F32),
                   jax.ShapeDtypeStruct((nh, nb, 8, tq), F32)],
        scratch_shapes=[pltpu.VMEM((tq, tq), F32), pltpu.VMEM((tq, tq), BF), pltpu.VMEM((tq, LANE), F32),
                        pltpu.VMEM((tq, LANE), F32), pltpu.VMEM((tq, HEAD_DIM), F32)],
        compiler_params=_params(),
    )(q, k, v, bias)


def _attn_bwd_dq(mode, q, k, v, o, do, lse, bias, tq, name, dep=None):
    t, hd = q.shape
    nh = hd // HEAD_DIM
    nb = t // tq
    wb = MAX_WINDOW // tq
    fox = mode == "fox"

    def body(q_ref, k_ref, v_ref, o_ref, do_ref, lse_ref, b_ref, dq_ref, dl_row_ref,
             s_ref, dp_ref, x_ref, y_ref, acc_ref, acc2_ref, dl_ref):
        qi = pl.program_id(1)
        qb = q_ref[...]
        dob = do_ref[...]
        acc_ref[...] = jnp.zeros_like(acc_ref)
        if fox:
            acc2_ref[...] = jnp.zeros_like(acc2_ref)
            dl_ref[...] = jnp.zeros_like(dl_ref)
        else:
            prod = o_ref[...].astype(F32) * dob.astype(F32)
            dl_ref[...] = jnp.broadcast_to(jnp.sum(prod, axis=1, keepdims=True), (tq, LANE))

        def tile(kj, diag):
            off = pl.multiple_of(kj * tq, tq)
            kb = k_ref[pl.ds(off, tq), :]
            s_ref[...] = _dot(qb, kb, NT)
            dp_ref[...] = _dot(dob, v_ref[pl.ds(off, tq), :], NT)
            if fox:
                brow = b_ref[qi][:, :1] - b_ref[kj]

            def chunk(r0):
                rows = pl.ds(r0, ATTN_ROWS)
                lse_c = _rep(lse_ref[rows, :], tq)
                if fox:
                    s = s_ref[rows, :] + brow
                    if diag:
                        s = jnp.where(_causal(r0, tq, False), s, NEG)
                    p = jnp.exp(s - lse_c)
                    pdp = p * dp_ref[rows, :]
                    dl_ref[rows, :] += jnp.sum(pdp, axis=1, keepdims=True)
                    x_ref[rows, :] = pdp.astype(BF)
                    y_ref[rows, :] = p.astype(BF)
                else:
                    p = jnp.exp(s_ref[rows, :] + b_ref[qi - kj, rows, :] - lse_c)
                    x_ref[rows, :] = (p * (dp_ref[rows, :] - _rep(dl_ref[rows, :], tq))).astype(BF)

            _chunks(tq, chunk)
            acc_ref[...] += _dot(x_ref[...], kb)
            if fox:
                acc2_ref[...] += _dot(y_ref[...], kb)

        tile(qi, True)
        if fox:
            lax.fori_loop(0, qi, lambda kj, c: (tile(kj, False), c)[1], 0)
            dq_ref[...] = acc_ref[...] - dl_ref[...] * acc2_ref[...]
        else:
            lax.fori_loop(1, jnp.minimum(qi, wb) + 1, lambda i, c: (tile(qi - i, False), c)[1], 0)
            dq_ref[...] = acc_ref[...]
        dl_row_ref[...] = _rows8(dl_ref[...])

    qspec = pl.BlockSpec((tq, HEAD_DIM), lambda h, i: (i, h))
    kvspec = pl.BlockSpec((t, HEAD_DIM), lambda h, i: (0, h))
    repspec = pl.BlockSpec((None, tq, LANE), lambda h, i: (h, i, 0))
    row8spec = pl.BlockSpec((None, None, 8, tq), lambda h, i: (h, i, 0, 0))
    if fox:
        bspec = pl.BlockSpec((None, nb, 1, tq), lambda h, i: (h, 0, 0, 0))
    else:
        bspec = pl.BlockSpec((wb + 1, tq, tq), lambda h, i: (0, 0, 0))
    return _call(
        body, [q, k, v, o, do, lse, bias], dep=dep, name=name, grid=(nh, nb),
        in_specs=[qspec, kvspec, kvspec, qspec, qspec, repspec, bspec],
        out_specs=[qspec, row8spec],
        out_shape=[jax.ShapeDtypeStruct((t, hd), F32), jax.ShapeDtypeStruct((nh, nb, 8, tq), F32)],
        scratch_shapes=[pltpu.VMEM((tq, tq), F32), pltpu.VMEM((tq, tq), F32), pltpu.VMEM((tq, tq), BF),
                        pltpu.VMEM((tq, tq), BF), pltpu.VMEM((tq, HEAD_DIM), F32),
                        pltpu.VMEM((tq, HEAD_DIM), F32), pltpu.VMEM((tq, LANE), F32)],
        compiler_params=_params(),
    )


def _attn_bwd_dkv(mode, q, k, v, do, lse_row, dl_row, bias_t, c_row, tq, name):
    t, hd = q.shape
    nh = hd // HEAD_DIM
    nb = t // tq
    wb = MAX_WINDOW // tq
    fox = mode == "fox"

    def body(*refs):
        if fox:
            (q_ref, k_ref, v_ref, do_ref, lse_ref, dl_ref, b_ref, cq_ref, dk_ref, dv_ref, dc_row_ref,
             s_ref, dp_ref, x_ref, y_ref, dc_ref) = refs
        else:
            q_ref, k_ref, v_ref, do_ref, lse_ref, dl_ref, b_ref, dk_ref, dv_ref, s_ref, dp_ref, x_ref, y_ref = refs
        kj = pl.program_id(1)
        kb = k_ref[...]
        vb = v_ref[...]
        dk_ref[...] = jnp.zeros_like(dk_ref)
        dv_ref[...] = jnp.zeros_like(dv_ref)
        if fox:
            dc_ref[...] = jnp.zeros_like(dc_ref)

        def tile(qi, diag):
            off = pl.multiple_of(qi * tq, tq)
            qb = q_ref[pl.ds(off, tq), :]
            dob = do_ref[pl.ds(off, tq), :]
            s_ref[...] = _dot(kb, qb, NT)
            dp_ref[...] = _dot(vb, dob, NT)
            lse_r = lse_ref[qi, 0:1, :]
            dl_r = dl_ref[qi, 0:1, :]
            if fox:
                kbias = cq_ref[qi][:, :1] - b_ref[...]

            def chunk(r0):
                rows = pl.ds(r0, ATTN_ROWS)
                if fox:
                    s = s_ref[rows, :] + _rep(kbias[r0:r0 + ATTN_ROWS, :], tq)
                    if diag:
                        s = jnp.where(_causal(r0, tq, True), s, NEG)
                else:
                    s = s_ref[rows, :] + b_ref[qi - kj, rows, :]
                pt = jnp.exp(s - lse_r)
                dst = pt * (dp_ref[rows, :] - dl_r)
                x_ref[rows, :] = pt.astype(BF)
                y_ref[rows, :] = dst.astype(BF)
                if fox:
                    dc_ref[rows, :] -= jnp.sum(dst, axis=1, keepdims=True)

            _chunks(tq, chunk)
            dv_ref[...] += _dot(x_ref[...], dob)
            dk_ref[...] += _dot(y_ref[...], qb)

        tile(kj, True)
        hi = nb if fox else jnp.minimum(kj + wb + 1, nb)
        lax.fori_loop(kj + 1, hi, lambda qi, c: (tile(qi, False), c)[1], 0)
        if fox:
            dc_row_ref[...] = _rows8(dc_ref[...])

    blkspec = pl.BlockSpec((tq, HEAD_DIM), lambda h, j: (j, h))
    fullspec = pl.BlockSpec((t, HEAD_DIM), lambda h, j: (0, h))
    rows8spec = pl.BlockSpec((None, nb, 8, tq), lambda h, j: (h, 0, 0, 0))
    repspec = pl.BlockSpec((None, tq, LANE), lambda h, j: (h, j, 0))
    in_specs = [fullspec, blkspec, blkspec, fullspec, rows8spec, rows8spec]
    args = [q, k, v, do, lse_row, dl_row, bias_t]
    out_specs = [blkspec, blkspec]
    out_shape = [jax.ShapeDtypeStruct((t, hd), F32), jax.ShapeDtypeStruct((t, hd), F32)]
    scratch = [pltpu.VMEM((tq, tq), F32), pltpu.VMEM((tq, tq), F32), pltpu.VMEM((tq, tq), BF),
               pltpu.VMEM((tq, tq), BF)]
    if fox:
        in_specs += [repspec, pl.BlockSpec((None, nb, 1, tq), lambda h, j: (h, 0, 0, 0))]
        args.append(c_row)
        out_specs.append(pl.BlockSpec((None, None, 8, tq), lambda h, j: (h, j, 0, 0)))
        out_shape.append(jax.ShapeDtypeStruct((nh, nb, 8, tq), F32))
        scratch.append(pltpu.VMEM((tq, LANE), F32))
    else:
        in_specs.append(pl.BlockSpec((wb + 1, tq, tq), lambda h, j: (0, 0, 0)))
    return pl.pallas_call(
        body, name=name, grid=(nh, nb), in_specs=in_specs, out_specs=out_specs, out_shape=out_shape,
        scratch_shapes=scratch, compiler_params=_params(),
    )(*args)


def _gate_specs(t, d, hd, tr):
    row = pl.BlockSpec((tr, d), lambda i: (i, 0))
    vec = pl.BlockSpec((1, d), lambda i: (0, 0))
    base = 6 * hd // d
    gd = pl.BlockSpec((tr, d), lambda i: (i, base))
    gf = pl.BlockSpec((tr, d), lambda i: (i, base + 1))
    return row, vec, gd, gf


def _merge_fwd(pd, pf, proj, b_d, b_f, hd):
    t, d = pd.shape
    tr = _tile(t, 256, 16)
    row, vec, gd, gf = _gate_specs(t, d, hd, tr)

    def body(pd_ref, pf_ref, gd_ref, gf_ref, bd_ref, bf_ref, o_ref):
        o_ref[...] = (_sig(gd_ref[...] + bd_ref[...]) * pd_ref[...]
                      + _sig(gf_ref[...] + bf_ref[...]) * pf_ref[...]).astype(BF)

    return pl.pallas_call(
        body, name="merge_fwd", grid=(t // tr,), in_specs=[row, row, gd, gf, vec, vec],
        out_specs=row, out_shape=jax.ShapeDtypeStruct((t, d), BF), compiler_params=_params(),
    )(pd, pf, proj, proj, b_d, b_f)


def _merge_bwd(dm, pd, pf, proj, b_d, b_f, hd):
    t, d = pd.shape
    tr = _tile(t, 256, 16)
    row, vec, gd, gf = _gate_specs(t, d, hd, tr)

    def body(dm_ref, pd_ref, pf_ref, gd_ref, gf_ref, bd_ref, bf_ref,
             dpd_ref, dpf_ref, dgd_ref, dgf_ref, dbd_ref, dbf_ref):
        dmv = dm_ref[...]
        sd = _sig(gd_ref[...] + bd_ref[...])
        sf = _sig(gf_ref[...] + bf_ref[...])
        dgd = dmv * pd_ref[...] * (sd * (1.0 - sd))
        dgf = dmv * pf_ref[...] * (sf * (1.0 - sf))
        dpd_ref[...] = (dmv * sd).astype(BF)
        dpf_ref[...] = (dmv * sf).astype(BF)
        dgd_ref[...] = dgd.astype(BF)
        dgf_ref[...] = dgf.astype(BF)

        @pl.when(pl.program_id(0) == 0)
        def _():
            dbd_ref[...] = jnp.zeros_like(dbd_ref)
            dbf_ref[...] = jnp.zeros_like(dbf_ref)

        dbd_ref[...] += jnp.sum(dgd, axis=0, keepdims=True)
        dbf_ref[...] += jnp.sum(dgf, axis=0, keepdims=True)

    ob = jax.ShapeDtypeStruct((t, d), BF)
    ov = jax.ShapeDtypeStruct((1, d), F32)
    return pl.pallas_call(
        body, name="merge_bwd", grid=(t // tr,), in_specs=[row, row, row, gd, gf, vec, vec],
        out_specs=[row, row, row, row, vec, vec], out_shape=[ob, ob, ob, ob, ov, ov],
        compiler_params=_params(),
    )(dm, pd, pf, proj, proj, b_d, b_f)


def _assemble_dproj(dqd, dkd, dvd, dqf, dkf, dvf, dgd, dgf, dlogf, proj, tables, bf_pad, scale):
    t, np_ = proj.shape
    hd = dqd.shape[1]
    d = dgd.shape[1]
    nh = hd // HEAD_DIM
    tr = _tile(t, 256, 16)
    f_blk = np_ // LANE - 1

    def body(dqd_ref, dkd_ref, dvd_ref, dqf_ref, dkf_ref, dvf_ref, dgd_ref, dgf_ref, dlog_ref, fl_ref,
             c_ref, s1_ref, s2_ref, b_ref, o_ref, db_ref):
        c, s1, s2 = c_ref[...], s1_ref[...], s2_ref[...]
        for h in range(nh):
            sl = slice(h * HEAD_DIM, (h + 1) * HEAD_DIM)
            o_ref[:, sl] = (_rope_t(dqd_ref[:, sl], c, s1, s2) * scale).astype(BF)
            o_ref[:, hd + h * HEAD_DIM:hd + (h + 1) * HEAD_DIM] = _rope_t(dkd_ref[:, sl], c, s1, s2).astype(BF)
        o_ref[:, 2 * hd:3 * hd] = dvd_ref[...].astype(BF)
        o_ref[:, 3 * hd:4 * hd] = (dqf_ref[...] * scale).astype(BF)
        o_ref[:, 4 * hd:5 * hd] = dkf_ref[...].astype(BF)
        o_ref[:, 5 * hd:6 * hd] = dvf_ref[...].astype(BF)
        o_ref[:, 6 * hd:6 * hd + d] = dgd_ref[...]
        o_ref[:, 6 * hd + d:6 * hd + 2 * d] = dgf_ref[...]
        z = fl_ref[...] + b_ref[...]
        dfl = dlog_ref[...] * _sig(-z)
        o_ref[:, 6 * hd + 2 * d:] = dfl.astype(BF)

        @pl.when(pl.program_id(0) == 0)
        def _():
            db_ref[...] = jnp.zeros_like(db_ref)

        db_ref[...] += jnp.sum(dfl, axis=0, keepdims=True)

    head = pl.BlockSpec((tr, hd), lambda i: (i, 0))
    row = pl.BlockSpec((tr, d), lambda i: (i, 0))
    lane_row = pl.BlockSpec((tr, LANE), lambda i: (i, 0))
    lane_vec = pl.BlockSpec((1, LANE), lambda i: (0, 0))
    return pl.pallas_call(
        body, name="assemble_dproj", grid=(t // tr,),
        in_specs=[head] * 6 + [row, row, lane_row, pl.BlockSpec((tr, LANE), lambda i: (i, f_blk)),
                               lane_row, lane_row, lane_row, lane_vec],
        out_specs=[pl.BlockSpec((tr, np_), lambda i: (i, 0)), lane_vec],
        out_shape=[jax.ShapeDtypeStruct((t, np_), BF), jax.ShapeDtypeStruct((1, LANE), F32)],
        compiler_params=_params(),
    )(dqd, dkd, dvd, dqf, dkf, dvf, dgd, dgf, dlogf, proj, *tables, bf_pad)


def _to_rows(a, tq):
    h, t = a.shape
    return a.reshape(h, t // tq, 1, tq)


def kernel(x, ffn1_norm, ffn1_w_gate, ffn1_w_up, ffn1_w_down, mix_norm, w_in, b_forget, b_gate_dil, b_gate_fox, w_proj_dil, w_proj_fox, w_out, ffn2_norm, ffn2_w_gate, ffn2_w_up, ffn2_w_down, final_norm, loss_target, m_ffn1_norm, m_ffn1_w_gate, m_ffn1_w_up, m_ffn1_w_down, m_mix_norm, m_w_in, m_b_forget, m_b_gate_dil, m_b_gate_fox, m_w_proj_dil, m_w_proj_fox, m_w_out, m_ffn2_norm, m_ffn2_w_gate, m_ffn2_w_up, m_ffn2_w_down, m_final_norm, v_ffn1_norm, v_ffn1_w_gate, v_ffn1_w_up, v_ffn1_w_down, v_mix_norm, v_w_in, v_b_forget, v_b_gate_dil, v_b_gate_fox, v_w_proj_dil, v_w_proj_fox, v_w_out, v_ffn2_norm, v_ffn2_w_gate, v_ffn2_w_up, v_ffn2_w_down, v_final_norm):
    t, d = x.shape[1], x.shape[2]
    hd = w_proj_dil.shape[1]
    nh = hd // HEAD_DIM
    n_f = b_forget.shape[1]
    cols = w_in.shape[2]
    in_cols = N_DEV * cols
    assert in_cols == 6 * hd + n_f + 2 * d and n_f == nh and n_f <= LANE
    np_ = 6 * hd + 2 * d + LANE
    scale = HEAD_DIM ** -0.5
    tq = _tile(t, 512, LANE)
    assert MAX_WINDOW % tq == 0 and tq % 16 == 0

    x2d = x[0]
    tgt = loss_target[0]

    ag_order = [ffn1_w_gate, ffn1_w_up, ffn1_w_down, w_in, w_proj_dil, w_proj_fox, w_out,
                ffn2_w_gate, ffn2_w_up, ffn2_w_down]
    ag, ag_token = _exchange_start([w[0].astype(BF) for w in ag_order], True, "ag_start")

    def gathered(idx, after, name):
        return _exchange_wait([ag[i] for i in idx], True, after, name)

    tables = _rope_tables(t)
    bf_pad = jnp.pad(b_forget, ((0, 0), (0, LANE - n_f)))

    hn1 = _rms_fwd(x2d, ffn1_norm, "rms_ffn1", dep=ag_token)
    wg1, wu1 = gathered([0, 1], hn1, "ag_wait_ffn1_gate_up")
    g1, u1, a1 = _ffn_gate_up(hn1, wg1, wu1, "ffn1_gate_up")
    wd1, = gathered([2], a1, "ag_wait_ffn1_down")
    x1 = _ffn_down(a1, wd1, x2d, "ffn1_down")

    hm = _rms_fwd(x1, mix_norm, "rms_mix")
    win_g, = gathered([3], hm, "ag_wait_w_in")
    win_full = win_g.transpose(1, 0, 2).reshape(d, in_cols)
    win_p = jnp.concatenate([win_full[:, :6 * hd], win_full[:, 6 * hd + n_f:], win_full[:, 6 * hd:6 * hd + n_f],
                             jnp.zeros((d, LANE - n_f), BF)], axis=1)
    proj = _mm_nn(hm, win_p, F32, "w_in_fwd")
    qd, kd, vd, qf, kf, vf, logf = _mixer_prep(proj, tables, bf_pad, hd, scale)
    csum = _cumsum_rows(logf, False, "cumsum_logf")
    c_heads = csum[:, :nh].T
    c_row = _to_rows(c_heads, tq)
    c_rep = jnp.broadcast_to(c_heads[:, :, None], (nh, t, LANE))
    dil_bias = _dil_bias_tiles(tq)
    dil_bias_t = dil_bias.transpose(0, 2, 1)
    yd, lse_d, lse_d_row = _attn_fwd("dil", qd, kd, vd, dil_bias, tq, "attn_dil_fwd")
    yf, lse_f, lse_f_row = _attn_fwd("fox", qf, kf, vf, c_row, tq, "attn_fox_fwd")
    wpd_g, wpf_g = gathered([4, 5], yf, "ag_wait_proj")
    wpd = wpd_g.transpose(1, 0, 2).reshape(hd, d)
    wpf = wpf_g.transpose(1, 0, 2).reshape(hd, d)
    pd = _mm_nn(yd, wpd, F32, "proj_dil_fwd", tn_pref=1024)
    pf = _mm_nn(yf, wpf, F32, "proj_fox_fwd", tn_pref=1024)
    merged = _merge_fwd(pd, pf, proj, b_gate_dil, b_gate_fox, hd)
    wout_g, = gathered([6], merged, "ag_wait_w_out")
    wout = wout_g.reshape(d, d)
    x2 = _mm_nn(merged, wout, F32, "w_out_fwd", residual=x1, tn_pref=1024)

    hn2 = _rms_fwd(x2, ffn2_norm, "rms_ffn2")
    wg2, wu2 = gathered([7, 8], hn2, "ag_wait_ffn2_gate_up")
    g2, u2, a2 = _ffn_gate_up(hn2, wg2, wu2, "ffn2_gate_up")
    wd2, = gathered([9], a2, "ag_wait_ffn2_down")
    x3 = _ffn_down(a2, wd2, x2, "ffn2_down")

    dx3, dx3b, d_final, loss_lanes = _loss_head(x3, final_norm.reshape(1, d), tgt)

    def ffn_bwd(dxb, hn, g, u, a, wg, wu, wd, tag):
        dg, du = _ffn_bwd_hidden(dxb, wd, g, u, tag + "_bwd_hidden")
        dwd = _ffn_dw_down(a, dxb, tag + "_dw_down")
        rs_down, tok = _exchange_start([dwd], False, "rs_start_" + tag + "_down")
        dwg, dwu = _ffn_dw_gate_up(hn, dg, du, tag + "_dw_gate_up", dep=tok)
        rs_gu, tok = _exchange_start([dwg, dwu], False, "rs_start_" + tag + "_gate_up")
        dhn = _ffn_bwd_input(dg, du, wg, wu, tag + "_bwd_input", dep=tok)
        return dhn, rs_gu + rs_down

    dhn2, rs_ffn2 = ffn_bwd(dx3b, hn2, g2, u2, a2, wg2, wu2, wd2, "ffn2")
    dx2, dx2b, d_ffn2_norm = _rms_bwd(dhn2, x2, ffn2_norm, dx3, "rms_ffn2_bwd")

    dmerged = _mm_nt(dx2b, wout, F32, "w_out_bwd")
    dwout = _mm_tn(merged, dx2b, BF, "w_out_dw", tn_pref=1024)
    dpd, dpf, dgd, dgf, d_bd, d_bf = _merge_bwd(dmerged, pd, pf, proj, b_gate_dil, b_gate_fox, hd)
    dyd = _mm_nt(dpd, wpd, BF, "proj_dil_bwd")
    dyf = _mm_nt(dpf, wpf, BF, "proj_fox_bwd")
    dwpd = _mm_tn(yd, dpd, BF, "proj_dil_dw", tn_pref=1024)
    dwpf = _mm_tn(yf, dpf, BF, "proj_fox_dw", tn_pref=1024)
    dwpd_c = dwpd.reshape(hd, N_DEV, d // N_DEV).transpose(1, 0, 2)
    dwpf_c = dwpf.reshape(hd, N_DEV, d // N_DEV).transpose(1, 0, 2)
    dwout_c = dwout.reshape(N_DEV, d // N_DEV, d)
    rs_mix, tok = _exchange_start([dwout_c, dwpd_c, dwpf_c], False, "rs_start_mixer")

    dqd, dl_d = _attn_bwd_dq("dil", qd, kd, vd, yd, dyd, lse_d, dil_bias, tq, "attn_dil_dq", dep=tok)
    dkd, dvd = _attn_bwd_dkv("dil", qd, kd, vd, dyd, lse_d_row, dl_d, dil_bias_t, None, tq, "attn_dil_dkv")
    dqf, dl_f = _attn_bwd_dq("fox", qf, kf, vf, yf, dyf, lse_f, c_row, tq, "attn_fox_dq")
    dkf, dvf, dc = _attn_bwd_dkv("fox", qf, kf, vf, dyf, lse_f_row, dl_f, c_rep, c_row, tq, "attn_fox_dkv")
    dc_pad = jnp.pad(dc[:, :, 0, :].reshape(nh, t).T, ((0, 0), (0, LANE - nh)))
    dlogf = _cumsum_rows(dc_pad, True, "revcumsum_dc")
    dproj, d_bforget = _assemble_dproj(dqd, dkd, dvd, dqf, dkf, dvf, dgd, dgf, dlogf, proj, tables, bf_pad, scale)

    dwin_p = _mm_tn(hm, dproj, BF, "w_in_dw")
    dwin_full = jnp.concatenate([dwin_p[:, :6 * hd], dwin_p[:, 6 * hd + 2 * d:6 * hd + 2 * d + n_f],
                                 dwin_p[:, 6 * hd:6 * hd + 2 * d]], axis=1)
    dwin_c = dwin_full.reshape(d, N_DEV, cols).transpose(1, 0, 2)
    rs_win, tok = _exchange_start([dwin_c], False, "rs_start_w_in")
    dhm = _mm_nt(dproj, win_p, F32, "w_in_bwd", tn_pref=2048, tk_pref=1152)
    dx1, dx1b, d_mix_norm = _rms_bwd(dhm, x1, mix_norm, dx2, "rms_mix_bwd", dep=tok)

    dhn1, rs_ffn1 = ffn_bwd(dx1b, hn1, g1, u1, a1, wg1, wu1, wd1, "ffn1")
    grad_x, _, d_ffn1_norm = _rms_bwd(dhn1, x2d, ffn1_norm, dx1, "rms_ffn1_bwd")

    def update(handles, names, after, tag):
        recvs = _exchange_wait(handles, False, after, "rs_wait_" + tag)
        res = {}
        for recv, n in zip(recvs, names):
            w, m, v = wmv[n]
            g, delta, m2, v2 = _adam_from_partials(recv, w[0], m[0], v[0], "adam_" + n)
            res[n] = (g[None], delta[None], m2[None], v2[None])
        return res, g

    wmv = {
        "ffn1_w_gate": (ffn1_w_gate, m_ffn1_w_gate, v_ffn1_w_gate),
        "ffn1_w_up": (ffn1_w_up, m_ffn1_w_up, v_ffn1_w_up),
        "ffn1_w_down": (ffn1_w_down, m_ffn1_w_down, v_ffn1_w_down),
        "w_in": (w_in, m_w_in, v_w_in),
        "w_proj_dil": (w_proj_dil, m_w_proj_dil, v_w_proj_dil),
        "w_proj_fox": (w_proj_fox, m_w_proj_fox, v_w_proj_fox),
        "w_out": (w_out, m_w_out, v_w_out),
        "ffn2_w_gate": (ffn2_w_gate, m_ffn2_w_gate, v_ffn2_w_gate),
        "ffn2_w_up": (ffn2_w_up, m_ffn2_w_up, v_ffn2_w_up),
        "ffn2_w_down": (ffn2_w_down, m_ffn2_w_down, v_ffn2_w_down),
    }
    big = {}
    after = grad_x
    for handles, names, tag in [
            (rs_ffn2, ["ffn2_w_gate", "ffn2_w_up", "ffn2_w_down"], "ffn2"),
            (rs_mix, ["w_out", "w_proj_dil", "w_proj_fox"], "mixer"),
            (rs_win, ["w_in"], "w_in"),
            (rs_ffn1, ["ffn1_w_gate", "ffn1_w_up", "ffn1_w_down"], "ffn1")]:
        res, after = update(handles, names, after, tag)
        big.update(res)

    def lanes(a):
        a = a.reshape(1, -1)
        return jnp.pad(a, ((0, 0), (0, d - a.shape[1])))

    small_names = ["ffn1_norm", "mix_norm", "b_gate_dil", "b_gate_fox", "ffn2_norm", "final_norm", "b_forget"]
    small_g = [d_ffn1_norm, d_mix_norm, d_bd, d_bf, d_ffn2_norm, d_final, d_bforget[:, :n_f]]
    small_w = [ffn1_norm, mix_norm, b_gate_dil, b_gate_fox, ffn2_norm, final_norm, b_forget]
    small_m = [m_ffn1_norm, m_mix_norm, m_b_gate_dil, m_b_gate_fox, m_ffn2_norm, m_final_norm, m_b_forget]
    small_v = [v_ffn1_norm, v_mix_norm, v_b_gate_dil, v_b_gate_fox, v_ffn2_norm, v_final_norm, v_b_forget]
    pack = lambda arrs, last: jnp.concatenate([lanes(a) for a in arrs] + [last], axis=0)
    g_all = _allreduce_small(pack(small_g, loss_lanes))
    zero_row = jnp.zeros((1, d), F32)
    one_row = jnp.ones((1, d), F32)
    s_delta, s_m, s_v = _adam_small(g_all, pack(small_w, zero_row), pack(small_m, zero_row), pack(small_v, one_row))
    loss = g_all[len(small_names), 0]

    def unpack(packed, i, like):
        return packed[i, :like.size].reshape(like.shape)

    small = {}
    for i, (n, w) in enumerate(zip(small_names, small_w)):
        small[n] = (unpack(g_all, i, w), unpack(s_delta, i, w), unpack(s_m, i, w), unpack(s_v, i, w))

    order = ["ffn1_norm", "ffn1_w_gate", "ffn1_w_up", "ffn1_w_down", "mix_norm", "w_in", "b_forget", "b_gate_dil",
             "b_gate_fox", "w_proj_dil", "w_proj_fox", "w_out", "ffn2_norm", "ffn2_w_gate", "ffn2_w_up",
             "ffn2_w_down", "final_norm"]
    res = {**big, **small}
    outs = [loss, grad_x[None]]
    for slot in range(4):
        outs += [res[n][slot] for n in order]
    return tuple(outs)
```

```python
import functools

import numpy as np
import jax
import jax.numpy as jnp
from jax import lax
from jax.experimental import pallas as pl
from jax.experimental.pallas import tpu as pltpu

BF = jnp.bfloat16
F32 = jnp.float32
MESH = pl.DeviceIdType.MESH
N_DEV = 8

HEAD_DIM = 128
ROPE_DIM = HEAD_DIM // 4
ROPE_HALF = ROPE_DIM // 2
ROPE_THETA = 500000.0
NORM_EPS = 1e-6
DIL_PATTERNS = ((128, 1), (512, 4), (2048, 16))
MAX_WINDOW = 2048
LANE = 128
NEG = -1e30

ADAM_LR = 0.001
ADAM_B1 = 0.9
ADAM_B2 = 0.999
ADAM_EPS = 1e-08
ADAM_WD = 0.01
ADAM_STEP = 10

VMEM_LIMIT_BYTES = 56 * 1024 * 1024
FFN_ROWS = 1024
ANY = pl.BlockSpec(memory_space=pl.ANY)

NN = (((1,), (0,)), ((), ()))
NT = (((1,), (1,)), ((), ()))
TN = (((0,), (0,)), ((), ()))


def _dot(a, b, dn=NN):
    return lax.dot_general(a, b, dn, preferred_element_type=F32)


def _sig(x):
    return 1.0 / (1.0 + jnp.exp(-x))


def _tile(n, pref, align):
    best = None
    t = align
    while t <= min(n, pref):
        if n % t == 0:
            best = t
        t += align
    return n if best is None else best


def _params():
    return pltpu.CompilerParams(vmem_limit_bytes=VMEM_LIMIT_BYTES)


def _call(body, args, dep=None, **kw):
    if dep is not None:
        n_in = len(args)
        inner = body

        def body(*refs):
            inner(*refs[:n_in], *refs[n_in + 1:])

        kw["in_specs"] = list(kw["in_specs"]) + [ANY]
        args = list(args) + [dep]
    return pl.pallas_call(body, **kw)(*args)


def _peers():
    x, y, c = lax.axis_index("x"), lax.axis_index("y"), lax.axis_index("c")
    me = 4 * x + 2 * y + c
    peers = []
    for k in range(1, N_DEV):
        px = 1 - x if (k >> 2) & 1 else x
        py = 1 - y if (k >> 1) & 1 else y
        pc = 1 - c if k & 1 else c
        peers.append((k, (px, py, pc), 4 * px + 2 * py + pc))
    return me, peers


HBM = pl.BlockSpec(memory_space=pltpu.HBM)
SEM = pl.BlockSpec(memory_space=pltpu.SEMAPHORE)
EFFECT = pltpu.SideEffectType.DATAFLOW_SIDE_EFFECTING


def _exchange_copy(gather, src_ref, land_ref, send_sems, recv_sems, me, k, peer, peer_flat, landing):
    return pltpu.make_async_remote_copy(
        src_ref=src_ref if gather else src_ref.at[peer_flat], dst_ref=land_ref.at[landing],
        send_sem=send_sems.at[k], recv_sem=recv_sems.at[k], device_id=peer, device_id_type=MESH)


def _exchange_start(srcs, gather, name):
    n = len(srcs)

    def body(*refs):
        src_refs, land_refs = refs[:n], refs[n:2 * n]
        send_refs, recv_refs = refs[2 * n:3 * n], refs[3 * n:4 * n]
        token = refs[6 * n]
        me, peers = _peers()
        for i in range(n):
            for k, peer, peer_flat in peers:
                _exchange_copy(gather, src_refs[i], land_refs[i], send_refs[i], recv_refs[i],
                               me, k, peer, peer_flat, me).start()
        token[...] = jnp.zeros_like(token)

    lands = [lax.empty((N_DEV,) + s.shape[-2:], s.dtype) for s in srcs]
    sems = [pltpu.SemaphoreType.DMA((N_DEV,)) for _ in range(2 * n)]
    out = pl.pallas_call(
        body, name=name,
        out_shape=tuple(sems) + tuple(pltpu.HBM(a.shape, a.dtype) for a in list(srcs) + lands)
        + (jax.ShapeDtypeStruct((8, LANE), F32),),
        in_specs=[HBM] * (2 * n),
        out_specs=tuple([SEM] * (2 * n) + [HBM] * (2 * n) + [pl.BlockSpec(memory_space=pltpu.VMEM)]),
        input_output_aliases={i: 2 * n + i for i in range(2 * n)},
        compiler_params=pltpu.CompilerParams(has_side_effects=EFFECT),
    )(*[pltpu.with_memory_space_constraint(a, pltpu.HBM) for a in list(srcs) + lands])
    handles = [(out[2 * n + i], out[3 * n + i], out[i], out[n + i]) for i in range(n)]
    return handles, out[4 * n]


def _exchange_wait(handles, gather, after, name):
    n = len(handles)

    def body(*refs):
        src_refs, land_refs = refs[:n], refs[n:2 * n]
        send_refs, recv_refs = refs[2 * n:3 * n], refs[3 * n:4 * n]
        me, peers = _peers()
        for i in range(n):
            for k, peer, peer_flat in peers:
                cp = _exchange_copy(gather, src_refs[i], land_refs[i], send_refs[i], recv_refs[i],
                                    me, k, peer, peer_flat, peer_flat)
                cp.wait_send()
                cp.wait_recv()

    srcs = [h[0] for h in handles]
    lands = [h[1] for h in handles]
    out = pl.pallas_call(
        body, name=name,
        out_shape=tuple(pltpu.HBM(a.shape, a.dtype) for a in srcs + lands),
        in_specs=[HBM] * (2 * n) + [SEM] * (2 * n) + [ANY],
        out_specs=tuple([HBM] * (2 * n)),
        input_output_aliases={i: i for i in range(2 * n)},
        compiler_params=pltpu.CompilerParams(has_side_effects=EFFECT),
    )(*srcs, *lands, *[h[2] for h in handles], *[h[3] for h in handles], after)
    me = 4 * lax.axis_index("x") + 2 * lax.axis_index("y") + lax.axis_index("c")
    filled = []
    for src, land in zip(out[:n], out[n:]):
        own = src[None] if gather else lax.dynamic_slice_in_dim(src, me, 1, axis=0)
        filled.append(lax.dynamic_update_slice_in_dim(land, own, me, axis=0))
    return filled


def _allreduce_small(p):
    rows, d = p.shape

    def body(p_ref, o_ref, recv_ref, send_sems, recv_sems):
        me, peers = _peers()
        recv_ref[me] = p_ref[...]
        sends = []
        for k, peer, peer_flat in peers:
            cp = pltpu.make_async_remote_copy(
                src_ref=p_ref, dst_ref=recv_ref.at[me],
                send_sem=send_sems.at[k], recv_sem=recv_sems.at[k],
                device_id=peer, device_id_type=MESH)
            cp.start()
            sends.append(cp)
        for k, peer, peer_flat in peers:
            pltpu.make_async_remote_copy(
                src_ref=p_ref, dst_ref=recv_ref.at[peer_flat],
                send_sem=send_sems.at[k], recv_sem=recv_sems.at[k],
                device_id=peer, device_id_type=MESH).wait_recv()
        for cp in sends:
            cp.wait_send()
        acc = recv_ref[0]
        for s in range(1, N_DEV):
            acc = acc + recv_ref[s]
        is_loss = lax.broadcasted_iota(jnp.int32, (rows, d), 0) == rows - 1
        total = jnp.sum(jnp.where(is_loss, acc, 0.0))
        o_ref[...] = jnp.where(is_loss, total, acc)

    return pl.pallas_call(
        body, name="allreduce_small",
        out_shape=jax.ShapeDtypeStruct((rows, d), F32),
        in_specs=[pl.BlockSpec(memory_space=pltpu.VMEM)],
        out_specs=pl.BlockSpec(memory_space=pltpu.VMEM),
        scratch_shapes=[pltpu.VMEM((N_DEV, rows, d), F32),
                        pltpu.SemaphoreType.DMA((N_DEV,)), pltpu.SemaphoreType.DMA((N_DEV,))],
    )(p)


def _adam_math(w, g, m, v):
    m2 = ADAM_B1 * m + (1.0 - ADAM_B1) * g
    v2 = ADAM_B2 * v + (1.0 - ADAM_B2) * (g * g)
    m_hat = m2 / (1.0 - ADAM_B1 ** ADAM_STEP)
    v_hat = v2 / (1.0 - ADAM_B2 ** ADAM_STEP)
    delta = -ADAM_LR * (m_hat / (jnp.sqrt(v_hat) + ADAM_EPS) + ADAM_WD * w)
    return delta, m2, v2


def _adam_from_partials(parts, w, m, v, name):
    r, c = w.shape
    tr = _tile(r, 256, 16)

    def body(p_ref, w_ref, m_ref, v_ref, g_out, d_out, m_out, v_out):
        g = p_ref[0].astype(F32)
        for s in range(1, N_DEV):
            g = g + p_ref[s].astype(F32)
        delta, m2, v2 = _adam_math(w_ref[...], g, m_ref[...], v_ref[...])
        g_out[...] = g
        d_out[...] = delta
        m_out[...] = m2
        v_out[...] = v2

    blk = pl.BlockSpec((tr, c), lambda i: (i, 0))
    out = jax.ShapeDtypeStruct((r, c), F32)
    return pl.pallas_call(
        body, name=name, grid=(r // tr,),
        in_specs=[pl.BlockSpec((N_DEV, tr, c), lambda i: (0, i, 0)), blk, blk, blk],
        out_specs=[blk, blk, blk, blk], out_shape=[out, out, out, out],
        compiler_params=_params(),
    )(parts, w, m, v)


def _adam_small(g, w, m, v):
    def body(g_ref, w_ref, m_ref, v_ref, d_out, m_out, v_out):
        delta, m2, v2 = _adam_math(w_ref[...], g_ref[...], m_ref[...], v_ref[...])
        d_out[...] = delta
        m_out[...] = m2
        v_out[...] = v2

    out = jax.ShapeDtypeStruct(g.shape, F32)
    return pl.pallas_call(body, name="adam_small", out_shape=[out, out, out])(g, w, m, v)


def _rms_fwd(x, gain, name, dep=None):
    t, d = x.shape
    tr = _tile(t, 256, LANE)

    def body(x_ref, g_ref, o_ref, ot_ref):
        xv = x_ref[...]
        r = lax.rsqrt(jnp.mean(xv * xv, axis=-1, keepdims=True) + NORM_EPS)
        y = xv * r * g_ref[...]
        o_ref[...] = y.astype(BF)
        ot_ref[...] = jnp.transpose(y).astype(BF)

    return _call(
        body, [x, gain], dep=dep, name=name, grid=(t // tr,),
        in_specs=[pl.BlockSpec((tr, d), lambda i: (i, 0)), pl.BlockSpec((1, d), lambda i: (0, 0))],
        out_specs=[pl.BlockSpec((tr, d), lambda i: (i, 0)), pl.BlockSpec((d, tr), lambda i: (0, i))],
        out_shape=[jax.ShapeDtypeStruct((t, d), BF), jax.ShapeDtypeStruct((d, t), BF)],
        compiler_params=_params(),
    )


def _rms_vjp(xv, gain, dy):
    r = lax.rsqrt(jnp.mean(xv * xv, axis=-1, keepdims=True) + NORM_EPS)
    xhat = xv * r
    dxhat = dy * gain
    dx = r * (dxhat - xhat * jnp.mean(dxhat * xhat, axis=-1, keepdims=True))
    dgain = jnp.sum(dy * xhat, axis=0, keepdims=True)
    return dx, dgain


def _rms_bwd(dy, x, gain, dres, name, dep=None):
    t, d = x.shape
    tr = _tile(t, 256, 16)

    def body(dy_ref, x_ref, g_ref, dres_ref, dx_ref, dxb_ref, dg_ref):
        dx, dgain = _rms_vjp(x_ref[...], g_ref[...], dy_ref[...])
        dx = dx + dres_ref[...]
        dx_ref[...] = dx
        dxb_ref[...] = dx.astype(BF)

        @pl.when(pl.program_id(0) == 0)
        def _():
            dg_ref[...] = jnp.zeros_like(dg_ref)

        dg_ref[...] += dgain

    row = pl.BlockSpec((tr, d), lambda i: (i, 0))
    vec = pl.BlockSpec((1, d), lambda i: (0, 0))
    return _call(
        body, [dy, x, gain, dres], dep=dep, name=name, grid=(t // tr,),
        in_specs=[row, row, vec, row], out_specs=[row, row, vec],
        out_shape=[jax.ShapeDtypeStruct((t, d), F32), jax.ShapeDtypeStruct((t, d), BF),
                   jax.ShapeDtypeStruct((1, d), F32)],
        compiler_params=_params(),
    )


def _loss_head(x, gain, target):
    t, d = x.shape
    tr = _tile(t, 256, 16)

    def body(x_ref, g_ref, t_ref, dx_ref, dxb_ref, dg_ref, loss_ref):
        xv = x_ref[...]
        gain = g_ref[...]
        r = lax.rsqrt(jnp.mean(xv * xv, axis=-1, keepdims=True) + NORM_EPS)
        err = xv * r * gain - t_ref[...]
        dx, dgain = _rms_vjp(xv, gain, err * (1.0 / d))
        dx_ref[...] = dx
        dxb_ref[...] = dx.astype(BF)

        @pl.when(pl.program_id(0) == 0)
        def _():
            dg_ref[...] = jnp.zeros_like(dg_ref)
            loss_ref[...] = jnp.zeros_like(loss_ref)

        dg_ref[...] += dgain
        loss_ref[...] += jnp.sum(err * err, axis=0, keepdims=True) * (0.5 / d)

    row = pl.BlockSpec((tr, d), lambda i: (i, 0))
    vec = pl.BlockSpec((1, d), lambda i: (0, 0))
    return pl.pallas_call(
        body, name="loss_head", grid=(t // tr,),
        in_specs=[row, vec, row], out_specs=[row, row, vec, vec],
        out_shape=[jax.ShapeDtypeStruct((t, d), F32), jax.ShapeDtypeStruct((t, d), BF),
                   jax.ShapeDtypeStruct((1, d), F32), jax.ShapeDtypeStruct((1, d), F32)],
        compiler_params=_params(),
    )(x, gain, target)


def _mm_nn(a, b, out_dtype, name, residual=None, tm_pref=512, tn_pref=1152):
    m, k = a.shape
    n = b.shape[1]
    tm, tn = _tile(m, tm_pref, 16), _tile(n, tn_pref, LANE)

    def body(*refs):
        if residual is None:
            a_ref, b_ref, o_ref = refs
            o_ref[...] = _dot(a_ref[...], b_ref[...]).astype(out_dtype)
        else:
            a_ref, b_ref, r_ref, o_ref = refs
            o_ref[...] = (r_ref[...] + _dot(a_ref[...], b_ref[...])).astype(out_dtype)

    in_specs = [pl.BlockSpec((tm, k), lambda j, i: (i, 0)), pl.BlockSpec((k, tn), lambda j, i: (0, j))]
    args = [a, b]
    if residual is not None:
        in_specs.append(pl.BlockSpec((tm, tn), lambda j, i: (i, j)))
        args.append(residual)
    return pl.pallas_call(
        body, name=name, grid=(n // tn, m // tm), in_specs=in_specs,
        out_specs=pl.BlockSpec((tm, tn), lambda j, i: (i, j)),
        out_shape=jax.ShapeDtypeStruct((m, n), out_dtype), compiler_params=_params(),
    )(*args)


def _mm_nt(a, b, out_dtype, name, tm_pref=512, tn_pref=1024, tk_pref=2048):
    m, k = a.shape
    n = b.shape[0]
    tm, tn, tk = _tile(m, tm_pref, 16), _tile(n, tn_pref, LANE), _tile(k, tk_pref, LANE)
    nk = k // tk

    def body(a_ref, b_ref, o_ref, acc_ref):
        kk = pl.program_id(2)

        @pl.when(kk == 0)
        def _():
            acc_ref[...] = jnp.zeros_like(acc_ref)

        acc_ref[...] += _dot(a_ref[...], b_ref[...], NT)

        @pl.when(kk == nk - 1)
        def _():
            o_ref[...] = acc_ref[...].astype(out_dtype)

    return pl.pallas_call(
        body, name=name, grid=(n // tn, m // tm, nk),
        in_specs=[pl.BlockSpec((tm, tk), lambda j, i, kk: (i, kk)),
                  pl.BlockSpec((tn, tk), lambda j, i, kk: (j, kk))],
        out_specs=pl.BlockSpec((tm, tn), lambda j, i, kk: (i, j)),
        out_shape=jax.ShapeDtypeStruct((m, n), out_dtype),
        scratch_shapes=[pltpu.VMEM((tm, tn), F32)], compiler_params=_params(),
    )(a, b)


def _mm_tn(a, b, out_dtype, name, tn_pref=1152, tk_pref=512, a_transposed=False):
    (k, t) = a.shape if a_transposed else a.shape[::-1]
    n = b.shape[1]
    tn, tk = _tile(n, tn_pref, LANE), _tile(t, tk_pref, LANE if a_transposed else 16)
    nt = t // tk

    def body(a_ref, b_ref, o_ref, acc_ref):
        tt = pl.program_id(1)

        @pl.when(tt == 0)
        def _():
            acc_ref[...] = jnp.zeros_like(acc_ref)

        acc_ref[...] += _dot(a_ref[...], b_ref[...], NN if a_transposed else TN)

        @pl.when(tt == nt - 1)
        def _():
            o_ref[...] = acc_ref[...].astype(out_dtype)

    if a_transposed:
        a_spec = pl.BlockSpec((k, tk), lambda j, tt: (0, tt))
    else:
        a_spec = pl.BlockSpec((tk, k), lambda j, tt: (tt, 0))
    return pl.pallas_call(
        body, name=name, grid=(n // tn, nt),
        in_specs=[a_spec, pl.BlockSpec((tk, tn), lambda j, tt: (tt, j))],
        out_specs=pl.BlockSpec((k, tn), lambda j, tt: (0, j)),
        out_shape=jax.ShapeDtypeStruct((k, n), out_dtype),
        scratch_shapes=[pltpu.VMEM((k, tn), F32)], compiler_params=_params(),
    )(a, b)


def _ffn_gate_up(hn, wg, wu, name):
    t, d = hn.shape
    ns, _, f = wg.shape
    tm = _tile(t, FFN_ROWS, 16)

    def body(h_ref, wg_ref, wu_ref, g_ref, u_ref, a_ref):
        h = h_ref[...]
        g = _dot(h, wg_ref[...])
        u = _dot(h, wu_ref[...])
        g_ref[...] = g.astype(BF)
        u_ref[...] = u.astype(BF)
        a_ref[...] = (g * _sig(g) * u).astype(BF)

    wspec = pl.BlockSpec((None, d, f), lambda j, i: (j, 0, 0))
    hid = pl.BlockSpec((None, tm, f), lambda j, i: (j, i, 0))
    out = jax.ShapeDtypeStruct((ns, t, f), BF)
    return pl.pallas_call(
        body, name=name, grid=(ns, t // tm),
        in_specs=[pl.BlockSpec((tm, d), lambda j, i: (i, 0)), wspec, wspec],
        out_specs=[hid, hid, hid], out_shape=[out, out, out], compiler_params=_params(),
    )(hn, wg, wu)


def _ffn_down(act, wd, xres, name):
    ns, t, f = act.shape
    d = wd.shape[2]
    tm = _tile(t, FFN_ROWS, 16)

    def body(a_ref, w_ref, x_ref, o_ref):
        @pl.when(pl.program_id(1) == 0)
        def _():
            o_ref[...] = x_ref[...]

        o_ref[...] += 0.5 * _dot(a_ref[...], w_ref[...])

    row = pl.BlockSpec((tm, d), lambda i, j: (i, 0))
    return pl.pallas_call(
        body, name=name, grid=(t // tm, ns),
        in_specs=[pl.BlockSpec((None, tm, f), lambda i, j: (j, i, 0)),
                  pl.BlockSpec((None, f, d), lambda i, j: (j, 0, 0)), row],
        out_specs=row, out_shape=jax.ShapeDtypeStruct((t, d), F32), compiler_params=_params(),
    )(act, wd, xres)


def _ffn_bwd_hidden(dxb, wd, g, u, name):
    t, d = dxb.shape
    ns, f, _ = wd.shape
    tm = _tile(t, FFN_ROWS, 16)

    def body(dx_ref, w_ref, g_ref, u_ref, dg_ref, du_ref):
        dh = 0.5 * _dot(dx_ref[...], w_ref[...], NT)
        gv = g_ref[...].astype(F32)
        uv = u_ref[...].astype(F32)
        s = _sig(gv)
        dg_ref[...] = (dh * uv * (s * (1.0 + gv * (1.0 - s)))).astype(BF)
        du_ref[...] = (dh * (gv * s)).astype(BF)

    hid = pl.BlockSpec((None, tm, f), lambda j, i: (j, i, 0))
    out = jax.ShapeDtypeStruct((ns, t, f), BF)
    return pl.pallas_call(
        body, name=name, grid=(ns, t // tm),
        in_specs=[pl.BlockSpec((tm, d), lambda j, i: (i, 0)),
                  pl.BlockSpec((None, f, d), lambda j, i: (j, 0, 0)), hid, hid],
        out_specs=[hid, hid], out_shape=[out, out], compiler_params=_params(),
    )(dxb, wd, g, u)


def _ffn_dw_down(act, dxb, name):
    ns, t, f = act.shape
    d = dxb.shape[1]
    tk = _tile(t, 512, 16)
    nt = t // tk

    def body(a_ref, dx_ref, o_ref, acc_ref):
        tt = pl.program_id(1)

        @pl.when(tt == 0)
        def _():
            acc_ref[...] = jnp.zeros_like(acc_ref)

        acc_ref[...] += _dot(a_ref[...], dx_ref[...], TN)

        @pl.when(tt == nt - 1)
        def _():
            o_ref[...] = (0.5 * acc_ref[...]).astype(BF)

    return pl.pallas_call(
        body, name=name, grid=(ns, nt),
        in_specs=[pl.BlockSpec((None, tk, f), lambda j, tt: (j, tt, 0)),
                  pl.BlockSpec((tk, d), lambda j, tt: (tt, 0))],
        out_specs=pl.BlockSpec((None, f, d), lambda j, tt: (j, 0, 0)),
        out_shape=jax.ShapeDtypeStruct((ns, f, d), BF),
        scratch_shapes=[pltpu.VMEM((f, d), F32)], compiler_params=_params(),
    )(act, dxb)


def _ffn_dw_gate_up(hn_t, dg, du, name, dep=None):
    d, t = hn_t.shape
    ns, _, f = dg.shape
    tk = _tile(t, 512, LANE)
    nt = t // tk

    def body(h_ref, dg_ref, du_ref, og_ref, ou_ref, accg_ref, accu_ref):
        tt = pl.program_id(1)

        @pl.when(tt == 0)
        def _():
            accg_ref[...] = jnp.zeros_like(accg_ref)
            accu_ref[...] = jnp.zeros_like(accu_ref)

        h = h_ref[...]
        accg_ref[...] += _dot(h, dg_ref[...])
        accu_ref[...] += _dot(h, du_ref[...])

        @pl.when(tt == nt - 1)
        def _():
            og_ref[...] = accg_ref[...].astype(BF)
            ou_ref[...] = accu_ref[...].astype(BF)

    hid = pl.BlockSpec((None, tk, f), lambda j, tt: (j, tt, 0))
    wspec = pl.BlockSpec((None, d, f), lambda j, tt: (j, 0, 0))
    out = jax.ShapeDtypeStruct((ns, d, f), BF)
    return _call(
        body, [hn_t, dg, du], dep=dep, name=name, grid=(ns, nt),
        in_specs=[pl.BlockSpec((d, tk), lambda j, tt: (0, tt)), hid, hid],
        out_specs=[wspec, wspec], out_shape=[out, out],
        scratch_shapes=[pltpu.VMEM((d, f), F32), pltpu.VMEM((d, f), F32)], compiler_params=_params(),
    )


def _ffn_bwd_input(dg, du, wg, wu, name, dep=None):
    ns, t, f = dg.shape
    d = wg.shape[1]
    tm = _tile(t, FFN_ROWS, 16)

    def body(dg_ref, du_ref, wg_ref, wu_ref, o_ref):
        @pl.when(pl.program_id(1) == 0)
        def _():
            o_ref[...] = jnp.zeros_like(o_ref)

        o_ref[...] += _dot(dg_ref[...], wg_ref[...], NT) + _dot(du_ref[...], wu_ref[...], NT)

    hid = pl.BlockSpec((None, tm, f), lambda i, j: (j, i, 0))
    wspec = pl.BlockSpec((None, d, f), lambda i, j: (j, 0, 0))
    return _call(
        body, [dg, du, wg, wu], dep=dep, name=name, grid=(t // tm, ns),
        in_specs=[hid, hid, wspec, wspec],
        out_specs=pl.BlockSpec((tm, d), lambda i, j: (i, 0)),
        out_shape=jax.ShapeDtypeStruct((t, d), F32), compiler_params=_params(),
    )


def _rope_tables(t):
    pos = jnp.arange(t, dtype=F32)
    inv_freq = ROPE_THETA ** (-jnp.arange(0, ROPE_DIM, 2, dtype=F32) / ROPE_DIM)
    ang = pos[:, None] * inv_freq[None, :]
    cos, sin = jnp.cos(ang), jnp.sin(ang)
    rest = HEAD_DIM - ROPE_DIM
    one = jnp.ones((t, rest), F32)
    zero_h = jnp.zeros((t, ROPE_HALF), F32)
    zero_r = jnp.zeros((t, rest), F32)
    c = jnp.concatenate([cos, cos, one], axis=1)
    s1 = jnp.concatenate([-sin, zero_h, zero_r], axis=1)
    s2 = jnp.concatenate([zero_h, sin, zero_r], axis=1)
    return c, s1, s2


def _rope(xh, c, s1, s2):
    return xh * c + pltpu.roll(xh, HEAD_DIM - ROPE_HALF, 1) * s1 + pltpu.roll(xh, ROPE_HALF, 1) * s2


def _rope_t(dh, c, s1, s2):
    return dh * c + pltpu.roll(dh * s1, ROPE_HALF, 1) + pltpu.roll(dh * s2, HEAD_DIM - ROPE_HALF, 1)


def _mixer_prep(proj, tables, bf_pad, hd, scale):
    t, np_ = proj.shape
    tr = _tile(t, 256, 16)
    nh = hd // HEAD_DIM
    nblk = hd // LANE
    f_blk = np_ // LANE - 1

    def body(qd_ref, kd_ref, vd_ref, qf_ref, kf_ref, vf_ref, fl_ref, c_ref, s1_ref, s2_ref, b_ref,
             oqd, okd, ovd, oqf, okf, ovf, olog):
        c, s1, s2 = c_ref[...], s1_ref[...], s2_ref[...]
        for h in range(nh):
            sl = slice(h * HEAD_DIM, (h + 1) * HEAD_DIM)
            oqd[:, sl] = (_rope(qd_ref[:, sl], c, s1, s2) * scale).astype(BF)
            okd[:, sl] = _rope(kd_ref[:, sl], c, s1, s2).astype(BF)
        ovd[...] = vd_ref[...].astype(BF)
        oqf[...] = (qf_ref[...] * scale).astype(BF)
        okf[...] = kf_ref[...].astype(BF)
        ovf[...] = vf_ref[...].astype(BF)
        z = fl_ref[...] + b_ref[...]
        olog[...] = jnp.minimum(z, 0.0) - jnp.log(1.0 + jnp.exp(-jnp.abs(z)))

    def col(kblk):
        return pl.BlockSpec((tr, hd), lambda i, kblk=kblk: (i, kblk))

    lane_row = pl.BlockSpec((tr, LANE), lambda i: (i, 0))
    in_specs = [col(0), col(1), col(2), col(3), col(4), col(5),
                pl.BlockSpec((tr, LANE), lambda i: (i, f_blk)),
                lane_row, lane_row, lane_row, pl.BlockSpec((1, LANE), lambda i: (0, 0))]
    o = pl.BlockSpec((tr, hd), lambda i: (i, 0))
    ob = jax.ShapeDtypeStruct((t, hd), BF)
    del nblk
    return pl.pallas_call(
        body, name="mixer_prep", grid=(t // tr,), in_specs=in_specs,
        out_specs=[o, o, o, o, o, o, lane_row],
        out_shape=[ob, ob, ob, ob, ob, ob, jax.ShapeDtypeStruct((t, LANE), F32)],
        compiler_params=_params(),
    )(proj, proj, proj, proj, proj, proj, proj, *tables, bf_pad)


def _split3(x):
    x1 = x.astype(BF)
    r1 = x - x1.astype(F32)
    x2 = r1.astype(BF)
    x3 = (r1 - x2.astype(F32)).astype(BF)
    return x1, x2, x3


def _cumsum_rows(x, reverse, name):
    t, w = x.shape
    blk = LANE
    nb = t // blk

    def body(x_ref, o_ref):
        r = lax.broadcasted_iota(jnp.int32, (blk, blk), 0)
        c = lax.broadcasted_iota(jnp.int32, (blk, blk), 1)
        tri = jnp.where((c >= r) if reverse else (c <= r), 1.0, 0.0).astype(BF)

        def step(i, carry):
            b = (nb - 1 - i) if reverse else i
            off = pl.multiple_of(b * blk, blk)
            xb = x_ref[pl.ds(off, blk), :]
            x1, x2, x3 = _split3(xb)
            o_ref[pl.ds(off, blk), :] = _dot(tri, x1) + _dot(tri, x2) + _dot(tri, x3) + carry
            return carry + jnp.sum(xb, axis=0, keepdims=True)

        lax.fori_loop(0, nb, step, jnp.zeros((1, w), F32))

    return pl.pallas_call(body, name=name, out_shape=jax.ShapeDtypeStruct((t, w), F32),
                          compiler_params=_params())(x)


ATTN_ROWS = 16


def _dil_bias_tiles(tq):
    nbias = MAX_WINDOW // tq + 1
    b = lax.broadcasted_iota(jnp.int32, (nbias, tq, tq), 0)
    i = lax.broadcasted_iota(jnp.int32, (nbias, tq, tq), 1)
    j = lax.broadcasted_iota(jnp.int32, (nbias, tq, tq), 2)
    delta = b * tq + i - j
    mult = jnp.zeros((nbias, tq, tq), F32)
    for w, dil in DIL_PATTERNS:
        mult = mult + jnp.where((delta >= 0) & (delta <= w) & (delta % dil == 0), 1.0, 0.0)
    return jnp.where(mult > 0.0, jnp.log(jnp.maximum(mult, 1.0)), NEG)


def _rep(x, width):
    return jnp.tile(x, (1, width // LANE))


def _chunks(n_rows, fn):
    for c in range(n_rows // ATTN_ROWS):
        fn(c * ATTN_ROWS)


def _causal(r0, tq, transposed):
    a = lax.broadcasted_iota(jnp.int32, (ATTN_ROWS, tq), 0) + r0
    b = lax.broadcasted_iota(jnp.int32, (ATTN_ROWS, tq), 1)
    return (a <= b) if transposed else (b <= a)


def _rows8(x):
    return jnp.transpose(x)[:8, :]


def _attn_fwd(mode, q, k, v, bias, tq, name):
    t, hd = q.shape
    nh = hd // HEAD_DIM
    nb = t // tq
    wb = MAX_WINDOW // tq
    fox = mode == "fox"

    def body(q_ref, k_ref, v_ref, b_ref, o_ref, lse_ref, lse_row_ref, s_ref, p_ref, m_ref, l_ref, acc_ref):
        qi = pl.program_id(1)
        qb = q_ref[...]
        m_ref[...] = jnp.full_like(m_ref, NEG)
        l_ref[...] = jnp.zeros_like(l_ref)
        acc_ref[...] = jnp.zeros_like(acc_ref)

        def tile(kj, diag):
            off = pl.multiple_of(kj * tq, tq)
            s_ref[...] = _dot(qb, k_ref[pl.ds(off, tq), :], NT)
            if fox:
                brow = b_ref[qi][:, :1] - b_ref[kj]

            def chunk(r0):
                rows = pl.ds(r0, ATTN_ROWS)
                if fox:
                    s = s_ref[rows, :] + brow
                    if diag:
                        s = jnp.where(_causal(r0, tq, False), s, NEG)
                else:
                    s = s_ref[rows, :] + b_ref[qi - kj, rows, :]
                m_old = m_ref[rows, :]
                m_new = jnp.maximum(m_old, jnp.max(s, axis=1, keepdims=True))
                p = jnp.exp(s - _rep(m_new, tq))
                alpha = jnp.exp(m_old - m_new)
                l_ref[rows, :] = alpha * l_ref[rows, :] + jnp.sum(p, axis=1, keepdims=True)
                m_ref[rows, :] = m_new
                acc_ref[rows, :] = alpha * acc_ref[rows, :]
                p_ref[rows, :] = p.astype(BF)

            _chunks(tq, chunk)
            acc_ref[...] += _dot(p_ref[...], v_ref[pl.ds(off, tq), :])

        tile(qi, True)
        if fox:
            lax.fori_loop(0, qi, lambda kj, c: (tile(kj, False), c)[1], 0)
        else:
            lax.fori_loop(1, jnp.minimum(qi, wb) + 1, lambda i, c: (tile(qi - i, False), c)[1], 0)
        o_ref[...] = (acc_ref[...] / l_ref[...]).astype(BF)
        lse = m_ref[...] + jnp.log(l_ref[...])
        lse_ref[...] = lse
        lse_row_ref[...] = _rows8(lse)

    qspec = pl.BlockSpec((tq, HEAD_DIM), lambda h, i: (i, h))
    kvspec = pl.BlockSpec((t, HEAD_DIM), lambda h, i: (0, h))
    repspec = pl.BlockSpec((None, tq, LANE), lambda h, i: (h, i, 0))
    row8spec = pl.BlockSpec((None, None, 8, tq), lambda h, i: (h, i, 0, 0))
    if fox:
        bspec = pl.BlockSpec((None, nb, 1, tq), lambda h, i: (h, 0, 0, 0))
    else:
        bspec = pl.BlockSpec((wb + 1, tq, tq), lambda h, i: (0, 0, 0))
    return pl.pallas_call(
        body, name=name, grid=(nh, nb), in_specs=[qspec, kvspec, kvspec, bspec],
        out_specs=[qspec, repspec, row8spec],
        out_shape=[jax.ShapeDtypeStruct((t, hd), BF), jax.ShapeDtypeStruct((nh, t, LANE), F32),
                   jax.ShapeDtypeStruct((nh, nb, 8, tq), F32)],
        scratch_shapes=[pltpu.VMEM((tq, tq), F32), pltpu.VMEM((tq, tq), BF), pltpu.VMEM((tq, LANE), F32),
                        pltpu.VMEM((tq, LANE), F32), pltpu.VMEM((tq, HEAD_DIM), F32)],
        compiler_params=_params(),
    )(q, k, v, bias)


def _attn_bwd_dq(mode, q, k, v, o, do, lse, bias, tq, name, dep=None):
    t, hd = q.shape
    nh = hd // HEAD_DIM
    nb = t // tq
    wb = MAX_WINDOW // tq
    fox = mode == "fox"

    def body(q_ref, k_ref, v_ref, o_ref, do_ref, lse_ref, b_ref, dq_ref, dl_row_ref,
             s_ref, dp_ref, x_ref, y_ref, acc_ref, acc2_ref, dl_ref):
        qi = pl.program_id(1)
        qb = q_ref[...]
        dob = do_ref[...]
        acc_ref[...] = jnp.zeros_like(acc_ref)
        if fox:
            acc2_ref[...] = jnp.zeros_like(acc2_ref)
            dl_ref[...] = jnp.zeros_like(dl_ref)
        else:
            prod = o_ref[...].astype(F32) * dob.astype(F32)
            dl_ref[...] = jnp.broadcast_to(jnp.sum(prod, axis=1, keepdims=True), (tq, LANE))

        def tile(kj, diag):
            off = pl.multiple_of(kj * tq, tq)
            kb = k_ref[pl.ds(off, tq), :]
            s_ref[...] = _dot(qb, kb, NT)
            dp_ref[...] = _dot(dob, v_ref[pl.ds(off, tq), :], NT)
            if fox:
                brow = b_ref[qi][:, :1] - b_ref[kj]

            def chunk(r0):
                rows = pl.ds(r0, ATTN_ROWS)
                lse_c = _rep(lse_ref[rows, :], tq)
                if fox:
                    s = s_ref[rows, :] + brow
                    if diag:
                        s = jnp.where(_causal(r0, tq, False), s, NEG)
                    p = jnp.exp(s - lse_c)
                    pdp = p * dp_ref[rows, :]
                    dl_ref[rows, :] += jnp.sum(pdp, axis=1, keepdims=True)
                    x_ref[rows, :] = pdp.astype(BF)
                    y_ref[rows, :] = p.astype(BF)
                else:
                    p = jnp.exp(s_ref[rows, :] + b_ref[qi - kj, rows, :] - lse_c)
                    x_ref[rows, :] = (p * (dp_ref[rows, :] - _rep(dl_ref[rows, :], tq))).astype(BF)

            _chunks(tq, chunk)
            acc_ref[...] += _dot(x_ref[...], kb)
            if fox:
                acc2_ref[...] += _dot(y_ref[...], kb)

        tile(qi, True)
        if fox:
            lax.fori_loop(0, qi, lambda kj, c: (tile(kj, False), c)[1], 0)
            dq_ref[...] = acc_ref[...] - dl_ref[...] * acc2_ref[...]
        else:
            lax.fori_loop(1, jnp.minimum(qi, wb) + 1, lambda i, c: (tile(qi - i, False), c)[1], 0)
            dq_ref[...] = acc_ref[...]
        dl_row_ref[...] = _rows8(dl_ref[...])

    qspec = pl.BlockSpec((tq, HEAD_DIM), lambda h, i: (i, h))
    kvspec = pl.BlockSpec((t, HEAD_DIM), lambda h, i: (0, h))
    repspec = pl.BlockSpec((None, tq, LANE), lambda h, i: (h, i, 0))
    row8spec = pl.BlockSpec((None, None, 8, tq), lambda h, i: (h, i, 0, 0))
    if fox:
        bspec = pl.BlockSpec((None, nb, 1, tq), lambda h, i: (h, 0, 0, 0))
    else:
        bspec = pl.BlockSpec((wb + 1, tq, tq), lambda h, i: (0, 0, 0))
    return _call(
        body, [q, k, v, o, do, lse, bias], dep=dep, name=name, grid=(nh, nb),
        in_specs=[qspec, kvspec, kvspec, qspec, qspec, repspec, bspec],
        out_specs=[qspec, row8spec],
        out_shape=[jax.ShapeDtypeStruct((t, hd), F32), jax.ShapeDtypeStruct((nh, nb, 8, tq), F32)],
        scratch_shapes=[pltpu.VMEM((tq, tq), F32), pltpu.VMEM((tq, tq), F32), pltpu.VMEM((tq, tq), BF),
                        pltpu.VMEM((tq, tq), BF), pltpu.VMEM((tq, HEAD_DIM), F32),
                        pltpu.VMEM((tq, HEAD_DIM), F32), pltpu.VMEM((tq, LANE), F32)],
        compiler_params=_params(),
    )


def _attn_bwd_dkv(mode, q, k, v, do, lse_row, dl_row, bias_t, c_row, tq, name):
    t, hd = q.shape
    nh = hd // HEAD_DIM
    nb = t // tq
    wb = MAX_WINDOW // tq
    fox = mode == "fox"

    def body(*refs):
        if fox:
            (q_ref, k_ref, v_ref, do_ref, lse_ref, dl_ref, b_ref, cq_ref, dk_ref, dv_ref, dc_row_ref,
             s_ref, dp_ref, x_ref, y_ref, dc_ref) = refs
        else:
            q_ref, k_ref, v_ref, do_ref, lse_ref, dl_ref, b_ref, dk_ref, dv_ref, s_ref, dp_ref, x_ref, y_ref = refs
        kj = pl.program_id(1)
        kb = k_ref[...]
        vb = v_ref[...]
        dk_ref[...] = jnp.zeros_like(dk_ref)
        dv_ref[...] = jnp.zeros_like(dv_ref)
        if fox:
            dc_ref[...] = jnp.zeros_like(dc_ref)

        def tile(qi, diag):
            off = pl.multiple_of(qi * tq, tq)
            qb = q_ref[pl.ds(off, tq), :]
            dob = do_ref[pl.ds(off, tq), :]
            s_ref[...] = _dot(kb, qb, NT)
            dp_ref[...] = _dot(vb, dob, NT)
            lse_r = lse_ref[qi, 0:1, :]
            dl_r = dl_ref[qi, 0:1, :]
            if fox:
                kbias = cq_ref[qi][:, :1] - b_ref[...]

            def chunk(r0):
                rows = pl.ds(r0, ATTN_ROWS)
                if fox:
                    s = s_ref[rows, :] + _rep(kbias[r0:r0 + ATTN_ROWS, :], tq)
                    if diag:
                        s = jnp.where(_causal(r0, tq, True), s, NEG)
                else:
                    s = s_ref[rows, :] + b_ref[qi - kj, rows, :]
                pt = jnp.exp(s - lse_r)
                dst = pt * (dp_ref[rows, :] - dl_r)
                x_ref[rows, :] = pt.astype(BF)
                y_ref[rows, :] = dst.astype(BF)
                if fox:
                    dc_ref[rows, :] -= jnp.sum(dst, axis=1, keepdims=True)

            _chunks(tq, chunk)
            dv_ref[...] += _dot(x_ref[...], dob)
            dk_ref[...] += _dot(y_ref[...], qb)

        tile(kj, True)
        hi = nb if fox else jnp.minimum(kj + wb + 1, nb)
        lax.fori_loop(kj + 1, hi, lambda qi, c: (tile(qi, False), c)[1], 0)
        if fox:
            dc_row_ref[...] = _rows8(dc_ref[...])

    blkspec = pl.BlockSpec((tq, HEAD_DIM), lambda h, j: (j, h))
    fullspec = pl.BlockSpec((t, HEAD_DIM), lambda h, j: (0, h))
    rows8spec = pl.BlockSpec((None, nb, 8, tq), lambda h, j: (h, 0, 0, 0))
    repspec = pl.BlockSpec((None, tq, LANE), lambda h, j: (h, j, 0))
    in_specs = [fullspec, blkspec, blkspec, fullspec, rows8spec, rows8spec]
    args = [q, k, v, do, lse_row, dl_row, bias_t]
    out_specs = [blkspec, blkspec]
    out_shape = [jax.ShapeDtypeStruct((t, hd), F32), jax.ShapeDtypeStruct((t, hd), F32)]
    scratch = [pltpu.VMEM((tq, tq), F32), pltpu.VMEM((tq, tq), F32), pltpu.VMEM((tq, tq), BF),
               pltpu.VMEM((tq, tq), BF)]
    if fox:
        in_specs += [repspec, pl.BlockSpec((None, nb, 1, tq), lambda h, j: (h, 0, 0, 0))]
        args.append(c_row)
        out_specs.append(pl.BlockSpec((None, None, 8, tq), lambda h, j: (h, j, 0, 0)))
        out_shape.append(jax.ShapeDtypeStruct((nh, nb, 8, tq), F32))
        scratch.append(pltpu.VMEM((tq, LANE), F32))
    else:
        in_specs.append(pl.BlockSpec((wb + 1, tq, tq), lambda h, j: (0, 0, 0)))
    return pl.pallas_call(
        body, name=name, grid=(nh, nb), in_specs=in_specs, out_specs=out_specs, out_shape=out_shape,
        scratch_shapes=scratch, compiler_params=_params(),
    )(*args)


def _gate_specs(t, d, hd, tr):
    row = pl.BlockSpec((tr, d), lambda i: (i, 0))
    vec = pl.BlockSpec((1, d), lambda i: (0, 0))
    base = 6 * hd // d
    gd = pl.BlockSpec((tr, d), lambda i: (i, base))
    gf = pl.BlockSpec((tr, d), lambda i: (i, base + 1))
    return row, vec, gd, gf


def _merge_fwd(pd, pf, proj, b_d, b_f, hd):
    t, d = pd.shape
    tr = _tile(t, 256, 16)
    row, vec, gd, gf = _gate_specs(t, d, hd, tr)

    def body(pd_ref, pf_ref, gd_ref, gf_ref, bd_ref, bf_ref, o_ref):
        o_ref[...] = (_sig(gd_ref[...] + bd_ref[...]) * pd_ref[...]
                      + _sig(gf_ref[...] + bf_ref[...]) * pf_ref[...]).astype(BF)

    return pl.pallas_call(
        body, name="merge_fwd", grid=(t // tr,), in_specs=[row, row, gd, gf, vec, vec],
        out_specs=row, out_shape=jax.ShapeDtypeStruct((t, d), BF), compiler_params=_params(),
    )(pd, pf, proj, proj, b_d, b_f)


def _merge_bwd(dm, pd, pf, proj, b_d, b_f, hd):
    t, d = pd.shape
    tr = _tile(t, 256, 16)
    row, vec, gd, gf = _gate_specs(t, d, hd, tr)

    def body(dm_ref, pd_ref, pf_ref, gd_ref, gf_ref, bd_ref, bf_ref,
             dpd_ref, dpf_ref, dgd_ref, dgf_ref, dbd_ref, dbf_ref):
        dmv = dm_ref[...]
        sd = _sig(gd_ref[...] + bd_ref[...])
        sf = _sig(gf_ref[...] + bf_ref[...])
        dgd = dmv * pd_ref[...] * (sd * (1.0 - sd))
        dgf = dmv * pf_ref[...] * (sf * (1.0 - sf))
        dpd_ref[...] = (dmv * sd).astype(BF)
        dpf_ref[...] = (dmv * sf).astype(BF)
        dgd_ref[...] = dgd.astype(BF)
        dgf_ref[...] = dgf.astype(BF)

        @pl.when(pl.program_id(0) == 0)
        def _():
            dbd_ref[...] = jnp.zeros_like(dbd_ref)
            dbf_ref[...] = jnp.zeros_like(dbf_ref)

        dbd_ref[...] += jnp.sum(dgd, axis=0, keepdims=True)
        dbf_ref[...] += jnp.sum(dgf, axis=0, keepdims=True)

    ob = jax.ShapeDtypeStruct((t, d), BF)
    ov = jax.ShapeDtypeStruct((1, d), F32)
    return pl.pallas_call(
        body, name="merge_bwd", grid=(t // tr,), in_specs=[row, row, row, gd, gf, vec, vec],
        out_specs=[row, row, row, row, vec, vec], out_shape=[ob, ob, ob, ob, ov, ov],
        compiler_params=_params(),
    )(dm, pd, pf, proj, proj, b_d, b_f)


def _assemble_dproj(dqd, dkd, dvd, dqf, dkf, dvf, dgd, dgf, dlogf, proj, tables, bf_pad, scale):
    t, np_ = proj.shape
    hd = dqd.shape[1]
    d = dgd.shape[1]
    nh = hd // HEAD_DIM
    tr = _tile(t, 256, 16)
    f_blk = np_ // LANE - 1

    def body(dqd_ref, dkd_ref, dvd_ref, dqf_ref, dkf_ref, dvf_ref, dgd_ref, dgf_ref, dlog_ref, fl_ref,
             c_ref, s1_ref, s2_ref, b_ref, o_ref, db_ref):
        c, s1, s2 = c_ref[...], s1_ref[...], s2_ref[...]
        for h in range(nh):
            sl = slice(h * HEAD_DIM, (h + 1) * HEAD_DIM)
            o_ref[:, sl] = (_rope_t(dqd_ref[:, sl], c, s1, s2) * scale).astype(BF)
            o_ref[:, hd + h * HEAD_DIM:hd + (h + 1) * HEAD_DIM] = _rope_t(dkd_ref[:, sl], c, s1, s2).astype(BF)
        o_ref[:, 2 * hd:3 * hd] = dvd_ref[...].astype(BF)
        o_ref[:, 3 * hd:4 * hd] = (dqf_ref[...] * scale).astype(BF)
        o_ref[:, 4 * hd:5 * hd] = dkf_ref[...].astype(BF)
        o_ref[:, 5 * hd:6 * hd] = dvf_ref[...].astype(BF)
        o_ref[:, 6 * hd:6 * hd + d] = dgd_ref[...]
        o_ref[:, 6 * hd + d:6 * hd + 2 * d] = dgf_ref[...]
        z = fl_ref[...] + b_ref[...]
        dfl = dlog_ref[...] * _sig(-z)
        o_ref[:, 6 * hd + 2 * d:] = dfl.astype(BF)

        @pl.when(pl.program_id(0) == 0)
        def _():
            db_ref[...] = jnp.zeros_like(db_ref)

        db_ref[...] += jnp.sum(dfl, axis=0, keepdims=True)

    head = pl.BlockSpec((tr, hd), lambda i: (i, 0))
    row = pl.BlockSpec((tr, d), lambda i: (i, 0))
    lane_row = pl.BlockSpec((tr, LANE), lambda i: (i, 0))
    lane_vec = pl.BlockSpec((1, LANE), lambda i: (0, 0))
    return pl.pallas_call(
        body, name="assemble_dproj", grid=(t // tr,),
        in_specs=[head] * 6 + [row, row, lane_row, pl.BlockSpec((tr, LANE), lambda i: (i, f_blk)),
                               lane_row, lane_row, lane_row, lane_vec],
        out_specs=[pl.BlockSpec((tr, np_), lambda i: (i, 0)), lane_vec],
        out_shape=[jax.ShapeDtypeStruct((t, np_), BF), jax.ShapeDtypeStruct((1, LANE), F32)],
        compiler_params=_params(),
    )(dqd, dkd, dvd, dqf, dkf, dvf, dgd, dgf, dlogf, proj, *tables, bf_pad)


def _to_rows(a, tq):
    h, t = a.shape
    return a.reshape(h, t // tq, 1, tq)


def kernel(x, ffn1_norm, ffn1_w_gate, ffn1_w_up, ffn1_w_down, mix_norm, w_in, b_forget, b_gate_dil, b_gate_fox, w_proj_dil, w_proj_fox, w_out, ffn2_norm, ffn2_w_gate, ffn2_w_up, ffn2_w_down, final_norm, loss_target, m_ffn1_norm, m_ffn1_w_gate, m_ffn1_w_up, m_ffn1_w_down, m_mix_norm, m_w_in, m_b_forget, m_b_gate_dil, m_b_gate_fox, m_w_proj_dil, m_w_proj_fox, m_w_out, m_ffn2_norm, m_ffn2_w_gate, m_ffn2_w_up, m_ffn2_w_down, m_final_norm, v_ffn1_norm, v_ffn1_w_gate, v_ffn1_w_up, v_ffn1_w_down, v_mix_norm, v_w_in, v_b_forget, v_b_gate_dil, v_b_gate_fox, v_w_proj_dil, v_w_proj_fox, v_w_out, v_ffn2_norm, v_ffn2_w_gate, v_ffn2_w_up, v_ffn2_w_down, v_final_norm):
    t, d = x.shape[1], x.shape[2]
    hd = w_proj_dil.shape[1]
    nh = hd // HEAD_DIM
    n_f = b_forget.shape[1]
    cols = w_in.shape[2]
    in_cols = N_DEV * cols
    assert in_cols == 6 * hd + n_f + 2 * d and n_f == nh and n_f <= LANE
    np_ = 6 * hd + 2 * d + LANE
    scale = HEAD_DIM ** -0.5
    tq = _tile(t, 512, LANE)
    assert MAX_WINDOW % tq == 0 and tq % 16 == 0

    x2d = x[0]
    tgt = loss_target[0]

    ag_order = [ffn1_w_gate, ffn1_w_up, ffn1_w_down, w_in, w_proj_dil, w_proj_fox, w_out,
                ffn2_w_gate, ffn2_w_up, ffn2_w_down]
    ag, ag_token = _exchange_start([w[0].astype(BF) for w in ag_order], True, "ag_start")

    def gathered(idx, after, name):
        return _exchange_wait([ag[i] for i in idx], True, after, name)

    tables = _rope_tables(t)
    bf_pad = jnp.pad(b_forget, ((0, 0), (0, LANE - n_f)))

    hn1, hn1_t = _rms_fwd(x2d, ffn1_norm, "rms_ffn1", dep=ag_token)
    wg1, wu1 = gathered([0, 1], hn1, "ag_wait_ffn1_gate_up")
    g1, u1, a1 = _ffn_gate_up(hn1, wg1, wu1, "ffn1_gate_up")
    wd1, = gathered([2], a1, "ag_wait_ffn1_down")
    x1 = _ffn_down(a1, wd1, x2d, "ffn1_down")

    hm, hm_t = _rms_fwd(x1, mix_norm, "rms_mix")
    win_g, = gathered([3], hm, "ag_wait_w_in")
    segments = [(0, 6 * hd), (6 * hd + n_f, in_cols), (6 * hd, 6 * hd + n_f)]
    pieces = []
    for lo, hi in segments:
        for j in range(lo // cols, (hi - 1) // cols + 1):
            s, e = max(lo, j * cols), min(hi, (j + 1) * cols)
            pieces.append(win_g[j, :, s - j * cols:e - j * cols])
    win_p = jnp.concatenate(pieces + [jnp.zeros((d, LANE - n_f), BF)], axis=1)
    proj = _mm_nn(hm, win_p, F32, "w_in_fwd")
    qd, kd, vd, qf, kf, vf, logf = _mixer_prep(proj, tables, bf_pad, hd, scale)
    csum = _cumsum_rows(logf, False, "cumsum_logf")
    c_heads = csum[:, :nh].T
    c_row = _to_rows(c_heads, tq)
    c_rep = jnp.broadcast_to(c_heads[:, :, None], (nh, t, LANE))
    dil_bias = _dil_bias_tiles(tq)
    dil_bias_t = dil_bias.transpose(0, 2, 1)
    yd, lse_d, lse_d_row = _attn_fwd("dil", qd, kd, vd, dil_bias, tq, "attn_dil_fwd")
    yf, lse_f, lse_f_row = _attn_fwd("fox", qf, kf, vf, c_row, tq, "attn_fox_fwd")
    wpd_g, wpf_g = gathered([4, 5], yf, "ag_wait_proj")
    wpd = wpd_g.transpose(1, 0, 2).reshape(hd, d)
    wpf = wpf_g.transpose(1, 0, 2).reshape(hd, d)
    pd = _mm_nn(yd, wpd, F32, "proj_dil_fwd", tn_pref=1024)
    pf = _mm_nn(yf, wpf, F32, "proj_fox_fwd", tn_pref=1024)
    merged = _merge_fwd(pd, pf, proj, b_gate_dil, b_gate_fox, hd)
    wout_g, = gathered([6], merged, "ag_wait_w_out")
    wout = wout_g.reshape(d, d)
    x2 = _mm_nn(merged, wout, F32, "w_out_fwd", residual=x1, tn_pref=1024)

    hn2, hn2_t = _rms_fwd(x2, ffn2_norm, "rms_ffn2")
    wg2, wu2 = gathered([7, 8], hn2, "ag_wait_ffn2_gate_up")
    g2, u2, a2 = _ffn_gate_up(hn2, wg2, wu2, "ffn2_gate_up")
    wd2, = gathered([9], a2, "ag_wait_ffn2_down")
    x3 = _ffn_down(a2, wd2, x2, "ffn2_down")

    dx3, dx3b, d_final, loss_lanes = _loss_head(x3, final_norm.reshape(1, d), tgt)

    def ffn_bwd(dxb, hn_t, g, u, a, wg, wu, wd, tag):
        dg, du = _ffn_bwd_hidden(dxb, wd, g, u, tag + "_bwd_hidden")
        dwd = _ffn_dw_down(a, dxb, tag + "_dw_down")
        rs_down, tok = _exchange_start([dwd], False, "rs_start_" + tag + "_down")
        dwg, dwu = _ffn_dw_gate_up(hn_t, dg, du, tag + "_dw_gate_up", dep=tok)
        rs_gu, tok = _exchange_start([dwg, dwu], False, "rs_start_" + tag + "_gate_up")
        dhn = _ffn_bwd_input(dg, du, wg, wu, tag + "_bwd_input", dep=tok)
        return dhn, rs_gu + rs_down

    dhn2, rs_ffn2 = ffn_bwd(dx3b, hn2_t, g2, u2, a2, wg2, wu2, wd2, "ffn2")
    dx2, dx2b, d_ffn2_norm = _rms_bwd(dhn2, x2, ffn2_norm, dx3, "rms_ffn2_bwd")

    dmerged = _mm_nt(dx2b, wout, F32, "w_out_bwd")
    dwout = _mm_tn(merged, dx2b, BF, "w_out_dw", tn_pref=1024)
    dpd, dpf, dgd, dgf, d_bd, d_bf = _merge_bwd(dmerged, pd, pf, proj, b_gate_dil, b_gate_fox, hd)
    dyd = _mm_nt(dpd, wpd, BF, "proj_dil_bwd")
    dyf = _mm_nt(dpf, wpf, BF, "proj_fox_bwd")
    dwpd = _mm_tn(yd, dpd, BF, "proj_dil_dw", tn_pref=1024)
    dwpf = _mm_tn(yf, dpf, BF, "proj_fox_dw", tn_pref=1024)
    dwpd_c = dwpd.reshape(hd, N_DEV, d // N_DEV).transpose(1, 0, 2)
    dwpf_c = dwpf.reshape(hd, N_DEV, d // N_DEV).transpose(1, 0, 2)
    dwout_c = dwout.reshape(N_DEV, d // N_DEV, d)
    rs_mix, tok = _exchange_start([dwout_c, dwpd_c, dwpf_c], False, "rs_start_mixer")

    dqd, dl_d = _attn_bwd_dq("dil", qd, kd, vd, yd, dyd, lse_d, dil_bias, tq, "attn_dil_dq", dep=tok)
    dkd, dvd = _attn_bwd_dkv("dil", qd, kd, vd, dyd, lse_d_row, dl_d, dil_bias_t, None, tq, "attn_dil_dkv")
    dqf, dl_f = _attn_bwd_dq("fox", qf, kf, vf, yf, dyf, lse_f, c_row, tq, "attn_fox_dq")
    dkf, dvf, dc = _attn_bwd_dkv("fox", qf, kf, vf, dyf, lse_f_row, dl_f, c_rep, c_row, tq, "attn_fox_dkv")
    dc_pad = jnp.pad(dc[:, :, 0, :].reshape(nh, t).T, ((0, 0), (0, LANE - nh)))
    dlogf = _cumsum_rows(dc_pad, True, "revcumsum_dc")
    dproj, d_bforget = _assemble_dproj(dqd, dkd, dvd, dqf, dkf, dvf, dgd, dgf, dlogf, proj, tables, bf_pad, scale)

    dwin_p = _mm_tn(hm_t, dproj, BF, "w_in_dw", a_transposed=True)
    def perm_col(c):
        if c < 6 * hd:
            return c
        return c + 2 * d if c < 6 * hd + n_f else c - n_f

    shards = []
    for j in range(N_DEV):
        cuts = sorted({j * cols, (j + 1) * cols} | {c for c in (6 * hd, 6 * hd + n_f) if j * cols < c < (j + 1) * cols})
        shards.append(jnp.concatenate([dwin_p[:, perm_col(lo):perm_col(lo) + hi - lo]
                                       for lo, hi in zip(cuts[:-1], cuts[1:])], axis=1))
    dwin_c = jnp.stack(shards)
    rs_win, tok = _exchange_start([dwin_c], False, "rs_start_w_in")
    dhm = _mm_nt(dproj, win_p, F32, "w_in_bwd", tn_pref=2048, tk_pref=1152)
    dx1, dx1b, d_mix_norm = _rms_bwd(dhm, x1, mix_norm, dx2, "rms_mix_bwd", dep=tok)

    dhn1, rs_ffn1 = ffn_bwd(dx1b, hn1_t, g1, u1, a1, wg1, wu1, wd1, "ffn1")
    grad_x, _, d_ffn1_norm = _rms_bwd(dhn1, x2d, ffn1_norm, dx1, "rms_ffn1_bwd")

    def update(handles, names, after, tag):
        recvs = _exchange_wait(handles, False, after, "rs_wait_" + tag)
        res = {}
        for recv, n in zip(recvs, names):
            w, m, v = wmv[n]
            g, delta, m2, v2 = _adam_from_partials(recv, w[0], m[0], v[0], "adam_" + n)
            res[n] = (g[None], delta[None], m2[None], v2[None])
        return res, g

    wmv = {
        "ffn1_w_gate": (ffn1_w_gate, m_ffn1_w_gate, v_ffn1_w_gate),
        "ffn1_w_up": (ffn1_w_up, m_ffn1_w_up, v_ffn1_w_up),
        "ffn1_w_down": (ffn1_w_down, m_ffn1_w_down, v_ffn1_w_down),
        "w_in": (w_in, m_w_in, v_w_in),
        "w_proj_dil": (w_proj_dil, m_w_proj_dil, v_w_proj_dil),
        "w_proj_fox": (w_proj_fox, m_w_proj_fox, v_w_proj_fox),
        "w_out": (w_out, m_w_out, v_w_out),
        "ffn2_w_gate": (ffn2_w_gate, m_ffn2_w_gate, v_ffn2_w_gate),
        "ffn2_w_up": (ffn2_w_up, m_ffn2_w_up, v_ffn2_w_up),
        "ffn2_w_down": (ffn2_w_down, m_ffn2_w_down, v_ffn2_w_down),
    }
    big = {}
    after = grad_x
    for handles, names, tag in [
            (rs_ffn2, ["ffn2_w_gate", "ffn2_w_up", "ffn2_w_down"], "ffn2"),
            (rs_mix, ["w_out", "w_proj_dil", "w_proj_fox"], "mixer"),
            (rs_win, ["w_in"], "w_in"),
            (rs_ffn1, ["ffn1_w_gate", "ffn1_w_up", "ffn1_w_down"], "ffn1")]:
        res, after = update(handles, names, after, tag)
        big.update(res)

    def lanes(a):
        a = a.reshape(1, -1)
        return jnp.pad(a, ((0, 0), (0, d - a.shape[1])))

    small_names = ["ffn1_norm", "mix_norm", "b_gate_dil", "b_gate_fox", "ffn2_norm", "final_norm", "b_forget"]
    small_g = [d_ffn1_norm, d_mix_norm, d_bd, d_bf, d_ffn2_norm, d_final, d_bforget[:, :n_f]]
    small_w = [ffn1_norm, mix_norm, b_gate_dil, b_gate_fox, ffn2_norm, final_norm, b_forget]
    small_m = [m_ffn1_norm, m_mix_norm, m_b_gate_dil, m_b_gate_fox, m_ffn2_norm, m_final_norm, m_b_forget]
    small_v = [v_ffn1_norm, v_mix_norm, v_b_gate_dil, v_b_gate_fox, v_ffn2_norm, v_final_norm, v_b_forget]
    pack = lambda arrs, last: jnp.concatenate([lanes(a) for a in arrs] + [last], axis=0)
    g_all = _allreduce_small(pack(small_g, loss_lanes))
    zero_row = jnp.zeros((1, d), F32)
    one_row = jnp.ones((1, d), F32)
    s_delta, s_m, s_v = _adam_small(g_all, pack(small_w, zero_row), pack(small_m, zero_row), pack(small_v, one_row))
    loss = g_all[len(small_names), 0]

    def unpack(packed, i, like):
        return packed[i, :like.size].reshape(like.shape)

    small = {}
    for i, (n, w) in enumerate(zip(small_names, small_w)):
        small[n] = (unpack(g_all, i, w), unpack(s_delta, i, w), unpack(s_m, i, w), unpack(s_v, i, w))

    order = ["ffn1_norm", "ffn1_w_gate", "ffn1_w_up", "ffn1_w_down", "mix_norm", "w_in", "b_forget", "b_gate_dil",
             "b_gate_fox", "w_proj_dil", "w_proj_fox", "w_out", "ffn2_norm", "ffn2_w_gate", "ffn2_w_up",
             "ffn2_w_down", "final_norm"]
    res = {**big, **small}
    outs = [loss, grad_x[None]]
    for slot in range(4):
        outs += [res[n][slot] for n in order]
    return tuple(outs)
```

```python
import functools

import numpy as np
import jax
import jax.numpy as jnp
from jax import lax
from jax.experimental import pallas as pl
from jax.experimental.pallas import tpu as pltpu

BF = jnp.bfloat16
F32 = jnp.float32
MESH = pl.DeviceIdType.MESH
N_DEV = 8

HEAD_DIM = 128
ROPE_DIM = HEAD_DIM // 4
ROPE_HALF = ROPE_DIM // 2
ROPE_THETA = 500000.0
NORM_EPS = 1e-6
DIL_PATTERNS = ((128, 1), (512, 4), (2048, 16))
MAX_WINDOW = 2048
LANE = 128
NEG = -1e30

ADAM_LR = 0.001
ADAM_B1 = 0.9
ADAM_B2 = 0.999
ADAM_EPS = 1e-08
ADAM_WD = 0.01
ADAM_STEP = 10

VMEM_LIMIT_BYTES = 56 * 1024 * 1024
FFN_ROWS = 1024
DW_ROWS = 1024
ANY = pl.BlockSpec(memory_space=pl.ANY)

NN = (((1,), (0,)), ((), ()))
NT = (((1,), (1,)), ((), ()))
TN = (((0,), (0,)), ((), ()))


def _dot(a, b, dn=NN):
    return lax.dot_general(a, b, dn, preferred_element_type=F32)


def _sig(x):
    return 1.0 / (1.0 + jnp.exp(-x))


def _tile(n, pref, align):
    best = None
    t = align
    while t <= min(n, pref):
        if n % t == 0:
            best = t
        t += align
    return n if best is None else best


def _params():
    return pltpu.CompilerParams(vmem_limit_bytes=VMEM_LIMIT_BYTES)


def _call(body, args, dep=None, **kw):
    if dep is not None:
        n_in = len(args)
        inner = body

        def body(*refs):
            inner(*refs[:n_in], *refs[n_in + 1:])

        kw["in_specs"] = list(kw["in_specs"]) + [ANY]
        args = list(args) + [dep]
    return pl.pallas_call(body, **kw)(*args)


def _peers():
    x, y, c = lax.axis_index("x"), lax.axis_index("y"), lax.axis_index("c")
    me = 4 * x + 2 * y + c
    peers = []
    for k in range(1, N_DEV):
        px = 1 - x if (k >> 2) & 1 else x
        py = 1 - y if (k >> 1) & 1 else y
        pc = 1 - c if k & 1 else c
        peers.append((k, (px, py, pc), 4 * px + 2 * py + pc))
    return me, peers


HBM = pl.BlockSpec(memory_space=pltpu.HBM)
SEM = pl.BlockSpec(memory_space=pltpu.SEMAPHORE)
EFFECT = pltpu.SideEffectType.DATAFLOW_SIDE_EFFECTING


def _exchange_copy(gather, src_ref, land_ref, send_sems, recv_sems, me, k, peer, peer_flat, landing):
    return pltpu.make_async_remote_copy(
        src_ref=src_ref if gather else src_ref.at[peer_flat], dst_ref=land_ref.at[landing],
        send_sem=send_sems.at[k], recv_sem=recv_sems.at[k], device_id=peer, device_id_type=MESH)


def _exchange_start(srcs, gather, name, dep=None):
    n = len(srcs)
    extra = [] if dep is None else [dep]

    def body(*refs):
        src_refs, land_refs = refs[:n], refs[n:2 * n]
        refs = refs[2 * n + len(extra):]
        send_refs, recv_refs = refs[:n], refs[n:2 * n]
        token = refs[4 * n]
        me, peers = _peers()
        for i in range(n):
            for k, peer, peer_flat in peers:
                _exchange_copy(gather, src_refs[i], land_refs[i], send_refs[i], recv_refs[i],
                               me, k, peer, peer_flat, me).start()
        token[...] = jnp.zeros_like(token)

    lands = [lax.empty((N_DEV,) + s.shape[-2:], s.dtype) for s in srcs]
    sems = [pltpu.SemaphoreType.DMA((N_DEV,)) for _ in range(2 * n)]
    out = pl.pallas_call(
        body, name=name,
        out_shape=tuple(sems) + tuple(pltpu.HBM(a.shape, a.dtype) for a in list(srcs) + lands)
        + (jax.ShapeDtypeStruct((8, LANE), F32),),
        in_specs=[HBM] * (2 * n) + [ANY] * len(extra),
        out_specs=tuple([SEM] * (2 * n) + [HBM] * (2 * n) + [pl.BlockSpec(memory_space=pltpu.VMEM)]),
        input_output_aliases={i: 2 * n + i for i in range(2 * n)},
        compiler_params=pltpu.CompilerParams(has_side_effects=EFFECT),
    )(*[pltpu.with_memory_space_constraint(a, pltpu.HBM) for a in list(srcs) + lands], *extra)
    handles = [(out[2 * n + i], out[3 * n + i], out[i], out[n + i]) for i in range(n)]
    return handles, out[4 * n]


def _exchange_wait(handles, gather, after, name):
    n = len(handles)

    def body(*refs):
        src_refs, land_refs = refs[:n], refs[n:2 * n]
        send_refs, recv_refs = refs[2 * n:3 * n], refs[3 * n:4 * n]
        me, peers = _peers()
        for i in range(n):
            for k, peer, peer_flat in peers:
                cp = _exchange_copy(gather, src_refs[i], land_refs[i], send_refs[i], recv_refs[i],
                                    me, k, peer, peer_flat, peer_flat)
                cp.wait_send()
                cp.wait_recv()

    srcs = [h[0] for h in handles]
    lands = [h[1] for h in handles]
    out = pl.pallas_call(
        body, name=name,
        out_shape=tuple(pltpu.HBM(a.shape, a.dtype) for a in srcs + lands),
        in_specs=[HBM] * (2 * n) + [SEM] * (2 * n) + [ANY],
        out_specs=tuple([HBM] * (2 * n)),
        input_output_aliases={i: i for i in range(2 * n)},
        compiler_params=pltpu.CompilerParams(has_side_effects=EFFECT),
    )(*srcs, *lands, *[h[2] for h in handles], *[h[3] for h in handles], after)
    me = 4 * lax.axis_index("x") + 2 * lax.axis_index("y") + lax.axis_index("c")
    filled = []
    for src, land in zip(out[:n], out[n:]):
        own = src[None] if gather else lax.dynamic_slice_in_dim(src, me, 1, axis=0)
        filled.append(lax.dynamic_update_slice_in_dim(land, own, me, axis=0))
    return filled


def _allreduce_small(p):
    rows, d = p.shape

    def body(p_ref, o_ref, recv_ref, send_sems, recv_sems):
        me, peers = _peers()
        recv_ref[me] = p_ref[...]
        sends = []
        for k, peer, peer_flat in peers:
            cp = pltpu.make_async_remote_copy(
                src_ref=p_ref, dst_ref=recv_ref.at[me],
                send_sem=send_sems.at[k], recv_sem=recv_sems.at[k],
                device_id=peer, device_id_type=MESH)
            cp.start()
            sends.append(cp)
        for k, peer, peer_flat in peers:
            pltpu.make_async_remote_copy(
                src_ref=p_ref, dst_ref=recv_ref.at[peer_flat],
                send_sem=send_sems.at[k], recv_sem=recv_sems.at[k],
                device_id=peer, device_id_type=MESH).wait_recv()
        for cp in sends:
            cp.wait_send()
        acc = recv_ref[0]
        for s in range(1, N_DEV):
            acc = acc + recv_ref[s]
        is_loss = lax.broadcasted_iota(jnp.int32, (rows, d), 0) == rows - 1
        total = jnp.sum(jnp.where(is_loss, acc, 0.0))
        o_ref[...] = jnp.where(is_loss, total, acc)

    return pl.pallas_call(
        body, name="allreduce_small",
        out_shape=jax.ShapeDtypeStruct((rows, d), F32),
        in_specs=[pl.BlockSpec(memory_space=pltpu.VMEM)],
        out_specs=pl.BlockSpec(memory_space=pltpu.VMEM),
        scratch_shapes=[pltpu.VMEM((N_DEV, rows, d), F32),
                        pltpu.SemaphoreType.DMA((N_DEV,)), pltpu.SemaphoreType.DMA((N_DEV,))],
    )(p)


def _adam_math(w, g, m, v):
    m2 = ADAM_B1 * m + (1.0 - ADAM_B1) * g
    v2 = ADAM_B2 * v + (1.0 - ADAM_B2) * (g * g)
    m_hat = m2 / (1.0 - ADAM_B1 ** ADAM_STEP)
    v_hat = v2 / (1.0 - ADAM_B2 ** ADAM_STEP)
    delta = -ADAM_LR * (m_hat / (jnp.sqrt(v_hat) + ADAM_EPS) + ADAM_WD * w)
    return delta, m2, v2


def _adam_from_partials(parts, w, m, v, name):
    r, c = w.shape
    tr = _tile(r, 256, 16)

    def body(p_ref, w_ref, m_ref, v_ref, g_out, d_out, m_out, v_out):
        g = p_ref[0].astype(F32)
        for s in range(1, N_DEV):
            g = g + p_ref[s].astype(F32)
        delta, m2, v2 = _adam_math(w_ref[...], g, m_ref[...], v_ref[...])
        g_out[...] = g
        d_out[...] = delta
        m_out[...] = m2
        v_out[...] = v2

    blk = pl.BlockSpec((tr, c), lambda i: (i, 0))
    out = jax.ShapeDtypeStruct((r, c), F32)
    return pl.pallas_call(
        body, name=name, grid=(r // tr,),
        in_specs=[pl.BlockSpec((N_DEV, tr, c), lambda i: (0, i, 0)), blk, blk, blk],
        out_specs=[blk, blk, blk, blk], out_shape=[out, out, out, out],
        compiler_params=_params(),
    )(parts, w, m, v)


def _adam_small(g, w, m, v):
    def body(g_ref, w_ref, m_ref, v_ref, d_out, m_out, v_out):
        delta, m2, v2 = _adam_math(w_ref[...], g_ref[...], m_ref[...], v_ref[...])
        d_out[...] = delta
        m_out[...] = m2
        v_out[...] = v2

    out = jax.ShapeDtypeStruct(g.shape, F32)
    return pl.pallas_call(body, name="adam_small", out_shape=[out, out, out])(g, w, m, v)


def _rms_fwd(x, gain, name, dep=None):
    t, d = x.shape
    tr = _tile(t, 256, LANE)

    def body(x_ref, g_ref, o_ref, ot_ref):
        xv = x_ref[...]
        r = lax.rsqrt(jnp.mean(xv * xv, axis=-1, keepdims=True) + NORM_EPS)
        y = xv * r * g_ref[...]
        o_ref[...] = y.astype(BF)
        ot_ref[...] = jnp.transpose(y).astype(BF)

    return _call(
        body, [x, gain], dep=dep, name=name, grid=(t // tr,),
        in_specs=[pl.BlockSpec((tr, d), lambda i: (i, 0)), pl.BlockSpec((1, d), lambda i: (0, 0))],
        out_specs=[pl.BlockSpec((tr, d), lambda i: (i, 0)), pl.BlockSpec((d, tr), lambda i: (0, i))],
        out_shape=[jax.ShapeDtypeStruct((t, d), BF), jax.ShapeDtypeStruct((d, t), BF)],
        compiler_params=_params(),
    )


def _rms_vjp(xv, gain, dy):
    r = lax.rsqrt(jnp.mean(xv * xv, axis=-1, keepdims=True) + NORM_EPS)
    xhat = xv * r
    dxhat = dy * gain
    dx = r * (dxhat - xhat * jnp.mean(dxhat * xhat, axis=-1, keepdims=True))
    dgain = jnp.sum(dy * xhat, axis=0, keepdims=True)
    return dx, dgain


def _loss_head(x, gain, target):
    t, d = x.shape
    tr = _tile(t, 256, 16)

    def body(x_ref, g_ref, t_ref, dx_ref, dxb_ref, dg_ref, loss_ref):
        xv = x_ref[...]
        gain = g_ref[...]
        r = lax.rsqrt(jnp.mean(xv * xv, axis=-1, keepdims=True) + NORM_EPS)
        err = xv * r * gain - t_ref[...]
        dx, dgain = _rms_vjp(xv, gain, err * (1.0 / d))
        dx_ref[...] = dx
        dxb_ref[...] = dx.astype(BF)

        @pl.when(pl.program_id(0) == 0)
        def _():
            dg_ref[...] = jnp.zeros_like(dg_ref)
            loss_ref[...] = jnp.zeros_like(loss_ref)

        dg_ref[...] += dgain
        loss_ref[...] += jnp.sum(err * err, axis=0, keepdims=True) * (0.5 / d)

    row = pl.BlockSpec((tr, d), lambda i: (i, 0))
    vec = pl.BlockSpec((1, d), lambda i: (0, 0))
    return pl.pallas_call(
        body, name="loss_head", grid=(t // tr,),
        in_specs=[row, vec, row], out_specs=[row, row, vec, vec],
        out_shape=[jax.ShapeDtypeStruct((t, d), F32), jax.ShapeDtypeStruct((t, d), BF),
                   jax.ShapeDtypeStruct((1, d), F32), jax.ShapeDtypeStruct((1, d), F32)],
        compiler_params=_params(),
    )(x, gain, target)


def _mm_nn(a, b, out_dtype, name, residual=None, tm_pref=512, tn_pref=1152):
    m, k = a.shape
    n = b.shape[1]
    tm, tn = _tile(m, tm_pref, 16), _tile(n, tn_pref, LANE)

    def body(*refs):
        if residual is None:
            a_ref, b_ref, o_ref = refs
            o_ref[...] = _dot(a_ref[...], b_ref[...]).astype(out_dtype)
        else:
            a_ref, b_ref, r_ref, o_ref = refs
            o_ref[...] = (r_ref[...] + _dot(a_ref[...], b_ref[...])).astype(out_dtype)

    in_specs = [pl.BlockSpec((tm, k), lambda j, i: (i, 0)), pl.BlockSpec((k, tn), lambda j, i: (0, j))]
    args = [a, b]
    if residual is not None:
        in_specs.append(pl.BlockSpec((tm, tn), lambda j, i: (i, j)))
        args.append(residual)
    return pl.pallas_call(
        body, name=name, grid=(n // tn, m // tm), in_specs=in_specs,
        out_specs=pl.BlockSpec((tm, tn), lambda j, i: (i, j)),
        out_shape=jax.ShapeDtypeStruct((m, n), out_dtype), compiler_params=_params(),
    )(*args)


def _rms_bwd_tail(dy_ref, first, x_ref, g_ref, dres_ref, dx_ref, dxb_ref, dg_ref):
    @pl.when(first)
    def _():
        dg_ref[...] = jnp.zeros_like(dg_ref)

    gain = g_ref[...]
    for r in range(0, dy_ref.shape[0], LANE):
        rows = pl.ds(r, min(LANE, dy_ref.shape[0] - r))
        dx, dgain = _rms_vjp(x_ref[rows, :], gain, dy_ref[rows, :])
        dx = dx + dres_ref[rows, :]
        dx_ref[rows, :] = dx
        dxb_ref[rows, :] = dx.astype(BF)
        dg_ref[...] += dgain


def _mm_nt(a, b, out_dtype, name, tm_pref=512, tn_pref=1024, tk_pref=2048, rms=None, dep=None):
    m, k = a.shape
    n = b.shape[0]
    tm, tn, tk = _tile(m, tm_pref, 16), _tile(n, tn_pref, LANE), _tile(k, tk_pref, LANE)
    nk = k // tk
    assert rms is None or tn == n

    def body(*refs):
        if rms is None:
            a_ref, b_ref, o_ref, acc_ref = refs
        else:
            a_ref, b_ref, x_ref, g_ref, dres_ref, dx_ref, dxb_ref, dg_ref, acc_ref = refs
        kk = pl.program_id(2)

        @pl.when(kk == 0)
        def _():
            acc_ref[...] = jnp.zeros_like(acc_ref)

        acc_ref[...] += _dot(a_ref[...], b_ref[...], NT)

        @pl.when(kk == nk - 1)
        def _():
            if rms is None:
                o_ref[...] = acc_ref[...].astype(out_dtype)
            else:
                _rms_bwd_tail(acc_ref, pl.program_id(1) == 0, x_ref, g_ref, dres_ref, dx_ref, dxb_ref, dg_ref)

    in_specs = [pl.BlockSpec((tm, tk), lambda j, i, kk: (i, kk)), pl.BlockSpec((tn, tk), lambda j, i, kk: (j, kk))]
    row = pl.BlockSpec((tm, tn), lambda j, i, kk: (i, j))
    if rms is None:
        args, out_specs, out_shape = [a, b], row, jax.ShapeDtypeStruct((m, n), out_dtype)
    else:
        vec = pl.BlockSpec((1, n), lambda j, i, kk: (0, 0))
        args, in_specs = [a, b, *rms], in_specs + [row, vec, row]
        out_specs = [row, row, vec]
        out_shape = [jax.ShapeDtypeStruct((m, n), F32), jax.ShapeDtypeStruct((m, n), BF),
                     jax.ShapeDtypeStruct((1, n), F32)]
    return _call(
        body, args, dep=dep, name=name, grid=(n // tn, m // tm, nk), in_specs=in_specs, out_specs=out_specs,
        out_shape=out_shape, scratch_shapes=[pltpu.VMEM((tm, tn), F32)], compiler_params=_params(),
    )


def _mm_tn(a, b, out_dtype, name, tn_pref=1152, tk_pref=512, a_transposed=False):
    (k, t) = a.shape if a_transposed else a.shape[::-1]
    n = b.shape[1]
    tn, tk = _tile(n, tn_pref, LANE), _tile(t, tk_pref, LANE if a_transposed else 16)
    nt = t // tk

    def body(a_ref, b_ref, o_ref, acc_ref):
        tt = pl.program_id(1)

        @pl.when(tt == 0)
        def _():
            acc_ref[...] = jnp.zeros_like(acc_ref)

        acc_ref[...] += _dot(a_ref[...], b_ref[...], NN if a_transposed else TN)

        @pl.when(tt == nt - 1)
        def _():
            o_ref[...] = acc_ref[...].astype(out_dtype)

    if a_transposed:
        a_spec = pl.BlockSpec((k, tk), lambda j, tt: (0, tt))
    else:
        a_spec = pl.BlockSpec((tk, k), lambda j, tt: (tt, 0))
    return pl.pallas_call(
        body, name=name, grid=(n // tn, nt),
        in_specs=[a_spec, pl.BlockSpec((tk, tn), lambda j, tt: (tt, j))],
        out_specs=pl.BlockSpec((k, tn), lambda j, tt: (0, j)),
        out_shape=jax.ShapeDtypeStruct((k, n), out_dtype),
        scratch_shapes=[pltpu.VMEM((k, tn), F32)], compiler_params=_params(),
    )(a, b)


def _ffn_gate_up(hn, wg, wu, name):
    t, d = hn.shape
    ns, _, f = wg.shape
    tm = _tile(t, FFN_ROWS, 16)

    def body(h_ref, wg_ref, wu_ref, g_ref, u_ref, a_ref):
        h = h_ref[...]
        g = _dot(h, wg_ref[...])
        u = _dot(h, wu_ref[...])
        g_ref[...] = g.astype(BF)
        u_ref[...] = u.astype(BF)
        a_ref[...] = (g * _sig(g) * u).astype(BF)

    wspec = pl.BlockSpec((None, d, f), lambda j, i: (j, 0, 0))
    hid = pl.BlockSpec((None, tm, f), lambda j, i: (j, i, 0))
    out = jax.ShapeDtypeStruct((ns, t, f), BF)
    return pl.pallas_call(
        body, name=name, grid=(ns, t // tm),
        in_specs=[pl.BlockSpec((tm, d), lambda j, i: (i, 0)), wspec, wspec],
        out_specs=[hid, hid, hid], out_shape=[out, out, out], compiler_params=_params(),
    )(hn, wg, wu)


def _ffn_down(act, wd, xres, name):
    ns, t, f = act.shape
    d = wd.shape[2]
    tm = _tile(t, FFN_ROWS, 16)

    def body(a_ref, w_ref, x_ref, o_ref):
        @pl.when(pl.program_id(1) == 0)
        def _():
            o_ref[...] = x_ref[...]

        o_ref[...] += 0.5 * _dot(a_ref[...], w_ref[...])

    row = pl.BlockSpec((tm, d), lambda i, j: (i, 0))
    return pl.pallas_call(
        body, name=name, grid=(t // tm, ns),
        in_specs=[pl.BlockSpec((None, tm, f), lambda i, j: (j, i, 0)),
                  pl.BlockSpec((None, f, d), lambda i, j: (j, 0, 0)), row],
        out_specs=row, out_shape=jax.ShapeDtypeStruct((t, d), F32), compiler_params=_params(),
    )(act, wd, xres)


def _ffn_bwd_hidden(dxb, wd, g, u, name):
    t, d = dxb.shape
    ns, f, _ = wd.shape
    tm = _tile(t, FFN_ROWS, 16)

    def body(dx_ref, w_ref, g_ref, u_ref, dg_ref, du_ref):
        dh = 0.5 * _dot(dx_ref[...], w_ref[...], NT)
        gv = g_ref[...].astype(F32)
        uv = u_ref[...].astype(F32)
        s = _sig(gv)
        dg_ref[...] = (dh * uv * (s * (1.0 + gv * (1.0 - s)))).astype(BF)
        du_ref[...] = (dh * (gv * s)).astype(BF)

    hid = pl.BlockSpec((None, tm, f), lambda j, i: (j, i, 0))
    out = jax.ShapeDtypeStruct((ns, t, f), BF)
    return pl.pallas_call(
        body, name=name, grid=(ns, t // tm),
        in_specs=[pl.BlockSpec((tm, d), lambda j, i: (i, 0)),
                  pl.BlockSpec((None, f, d), lambda j, i: (j, 0, 0)), hid, hid],
        out_specs=[hid, hid], out_shape=[out, out], compiler_params=_params(),
    )(dxb, wd, g, u)


def _ffn_dw_down(act, dxb, name):
    ns, t, f = act.shape
    d = dxb.shape[1]
    tk = _tile(t, DW_ROWS, 16)
    nt = t // tk

    def body(a_ref, dx_ref, o_ref, acc_ref):
        tt = pl.program_id(1)

        @pl.when(tt == 0)
        def _():
            acc_ref[...] = jnp.zeros_like(acc_ref)

        acc_ref[...] += _dot(a_ref[...], dx_ref[...], TN)

        @pl.when(tt == nt - 1)
        def _():
            o_ref[...] = (0.5 * acc_ref[...]).astype(BF)

    return pl.pallas_call(
        body, name=name, grid=(ns, nt),
        in_specs=[pl.BlockSpec((None, tk, f), lambda j, tt: (j, tt, 0)),
                  pl.BlockSpec((tk, d), lambda j, tt: (tt, 0))],
        out_specs=pl.BlockSpec((None, f, d), lambda j, tt: (j, 0, 0)),
        out_shape=jax.ShapeDtypeStruct((ns, f, d), BF),
        scratch_shapes=[pltpu.VMEM((f, d), F32)], compiler_params=_params(),
    )(act, dxb)


def _ffn_dw_gate_up(hn_t, dg, du, name, dep=None):
    d, t = hn_t.shape
    ns, _, f = dg.shape
    tk = _tile(t, DW_ROWS, LANE)
    nt = t // tk

    def body(h_ref, dg_ref, du_ref, og_ref, ou_ref, accg_ref, accu_ref):
        tt = pl.program_id(1)

        @pl.when(tt == 0)
        def _():
            accg_ref[...] = jnp.zeros_like(accg_ref)
            accu_ref[...] = jnp.zeros_like(accu_ref)

        h = h_ref[...]
        accg_ref[...] += _dot(h, dg_ref[...])
        accu_ref[...] += _dot(h, du_ref[...])

        @pl.when(tt == nt - 1)
        def _():
            og_ref[...] = accg_ref[...].astype(BF)
            ou_ref[...] = accu_ref[...].astype(BF)

    hid = pl.BlockSpec((None, tk, f), lambda j, tt: (j, tt, 0))
    wspec = pl.BlockSpec((None, d, f), lambda j, tt: (j, 0, 0))
    out = jax.ShapeDtypeStruct((ns, d, f), BF)
    return _call(
        body, [hn_t, dg, du], dep=dep, name=name, grid=(ns, nt),
        in_specs=[pl.BlockSpec((d, tk), lambda j, tt: (0, tt)), hid, hid],
        out_specs=[wspec, wspec], out_shape=[out, out],
        scratch_shapes=[pltpu.VMEM((d, f), F32), pltpu.VMEM((d, f), F32)], compiler_params=_params(),
    )


def _ffn_bwd_input(dg, du, wg, wu, x, gain, dres, name, dep=None):
    ns, t, f = dg.shape
    d = wg.shape[1]
    tm = _tile(t, 512, 16)

    def body(dg_ref, du_ref, wg_ref, wu_ref, x_ref, g_ref, dres_ref, dx_ref, dxb_ref, dgain_ref, acc_ref):
        j = pl.program_id(1)

        @pl.when(j == 0)
        def _():
            acc_ref[...] = jnp.zeros_like(acc_ref)

        acc_ref[...] += _dot(dg_ref[...], wg_ref[...], NT) + _dot(du_ref[...], wu_ref[...], NT)

        @pl.when(j == ns - 1)
        def _():
            _rms_bwd_tail(acc_ref, pl.program_id(0) == 0, x_ref, g_ref, dres_ref, dx_ref, dxb_ref, dgain_ref)

    hid = pl.BlockSpec((None, tm, f), lambda i, j: (j, i, 0))
    wspec = pl.BlockSpec((None, d, f), lambda i, j: (j, 0, 0))
    row = pl.BlockSpec((tm, d), lambda i, j: (i, 0))
    row_once = pl.BlockSpec((tm, d), lambda i, j: (i, 0), pipeline_mode=pl.Buffered(1))
    vec = pl.BlockSpec((1, d), lambda i, j: (0, 0))
    return _call(
        body, [dg, du, wg, wu, x, gain, dres], dep=dep, name=name, grid=(t // tm, ns),
        in_specs=[hid, hid, wspec, wspec, row_once, vec, row_once],
        out_specs=[row, row, vec],
        out_shape=[jax.ShapeDtypeStruct((t, d), F32), jax.ShapeDtypeStruct((t, d), BF),
                   jax.ShapeDtypeStruct((1, d), F32)],
        scratch_shapes=[pltpu.VMEM((tm, d), F32)], compiler_params=_params(),
    )


def _rope_tables(t):
    pos = jnp.arange(t, dtype=F32)
    inv_freq = ROPE_THETA ** (-jnp.arange(0, ROPE_DIM, 2, dtype=F32) / ROPE_DIM)
    ang = pos[:, None] * inv_freq[None, :]
    cos, sin = jnp.cos(ang), jnp.sin(ang)
    rest = HEAD_DIM - ROPE_DIM
    one = jnp.ones((t, rest), F32)
    zero_h = jnp.zeros((t, ROPE_HALF), F32)
    zero_r = jnp.zeros((t, rest), F32)
    c = jnp.concatenate([cos, cos, one], axis=1)
    s1 = jnp.concatenate([-sin, zero_h, zero_r], axis=1)
    s2 = jnp.concatenate([zero_h, sin, zero_r], axis=1)
    return c, s1, s2


def _rope(xh, c, s1, s2):
    return xh * c + pltpu.roll(xh, HEAD_DIM - ROPE_HALF, 1) * s1 + pltpu.roll(xh, ROPE_HALF, 1) * s2


def _rope_t(dh, c, s1, s2):
    return dh * c + pltpu.roll(dh * s1, ROPE_HALF, 1) + pltpu.roll(dh * s2, HEAD_DIM - ROPE_HALF, 1)


def _mixer_prep(proj, tables, bf_pad, hd, scale):
    t, np_ = proj.shape
    tr = _tile(t, 256, 16)
    nh = hd // HEAD_DIM
    nblk = hd // LANE
    f_blk = np_ // LANE - 1

    def body(qd_ref, kd_ref, vd_ref, qf_ref, kf_ref, vf_ref, fl_ref, c_ref, s1_ref, s2_ref, b_ref,
             oqd, okd, ovd, oqf, okf, ovf, olog):
        c, s1, s2 = c_ref[...], s1_ref[...], s2_ref[...]
        for h in range(nh):
            sl = slice(h * HEAD_DIM, (h + 1) * HEAD_DIM)
            oqd[:, sl] = (_rope(qd_ref[:, sl], c, s1, s2) * scale).astype(BF)
            okd[:, sl] = _rope(kd_ref[:, sl], c, s1, s2).astype(BF)
        ovd[...] = vd_ref[...].astype(BF)
        oqf[...] = (qf_ref[...] * scale).astype(BF)
        okf[...] = kf_ref[...].astype(BF)
        ovf[...] = vf_ref[...].astype(BF)
        z = fl_ref[...] + b_ref[...]
        olog[...] = jnp.minimum(z, 0.0) - jnp.log(1.0 + jnp.exp(-jnp.abs(z)))

    def col(kblk):
        return pl.BlockSpec((tr, hd), lambda i, kblk=kblk: (i, kblk))

    lane_row = pl.BlockSpec((tr, LANE), lambda i: (i, 0))
    in_specs = [col(0), col(1), col(2), col(3), col(4), col(5),
                pl.BlockSpec((tr, LANE), lambda i: (i, f_blk)),
                lane_row, lane_row, lane_row, pl.BlockSpec((1, LANE), lambda i: (0, 0))]
    o = pl.BlockSpec((tr, hd), lambda i: (i, 0))
    ob = jax.ShapeDtypeStruct((t, hd), BF)
    del nblk
    return pl.pallas_call(
        body, name="mixer_prep", grid=(t // tr,), in_specs=in_specs,
        out_specs=[o, o, o, o, o, o, lane_row],
        out_shape=[ob, ob, ob, ob, ob, ob, jax.ShapeDtypeStruct((t, LANE), F32)],
        compiler_params=_params(),
    )(proj, proj, proj, proj, proj, proj, proj, *tables, bf_pad)


def _split3(x):
    x1 = x.astype(BF)
    r1 = x - x1.astype(F32)
    x2 = r1.astype(BF)
    x3 = (r1 - x2.astype(F32)).astype(BF)
    return x1, x2, x3


def _cumsum_rows(x, reverse, name):
    t, w = x.shape
    blk = LANE
    nb = t // blk

    def body(x_ref, o_ref):
        r = lax.broadcasted_iota(jnp.int32, (blk, blk), 0)
        c = lax.broadcasted_iota(jnp.int32, (blk, blk), 1)
        tri = jnp.where((c >= r) if reverse else (c <= r), 1.0, 0.0).astype(BF)

        def step(i, carry):
            b = (nb - 1 - i) if reverse else i
            off = pl.multiple_of(b * blk, blk)
            xb = x_ref[pl.ds(off, blk), :]
            x1, x2, x3 = _split3(xb)
            o_ref[pl.ds(off, blk), :] = _dot(tri, x1) + _dot(tri, x2) + _dot(tri, x3) + carry
            return carry + jnp.sum(xb, axis=0, keepdims=True)

        lax.fori_loop(0, nb, step, jnp.zeros((1, w), F32))

    return pl.pallas_call(body, name=name, out_shape=jax.ShapeDtypeStruct((t, w), F32),
                          compiler_params=_params())(x)


ATTN_ROWS = 16


def _dil_bias_tiles(tq):
    nbias = MAX_WINDOW // tq + 1
    b = lax.broadcasted_iota(jnp.int32, (nbias, tq, tq), 0)
    i = lax.broadcasted_iota(jnp.int32, (nbias, tq, tq), 1)
    j = lax.broadcasted_iota(jnp.int32, (nbias, tq, tq), 2)
    delta = b * tq + i - j
    mult = jnp.zeros((nbias, tq, tq), F32)
    for w, dil in DIL_PATTERNS:
        mult = mult + jnp.where((delta >= 0) & (delta <= w) & (delta % dil == 0), 1.0, 0.0)
    return jnp.where(mult > 0.0, jnp.log(jnp.maximum(mult, 1.0)), NEG)


def _rep(x, width):
    return jnp.tile(x, (1, width // LANE))


def _chunks(n_rows, fn):
    for c in range(n_rows // ATTN_ROWS):
        fn(c * ATTN_ROWS)


def _causal(r0, tq, transposed):
    a = lax.broadcasted_iota(jnp.int32, (ATTN_ROWS, tq), 0) + r0
    b = lax.broadcasted_iota(jnp.int32, (ATTN_ROWS, tq), 1)
    return (a <= b) if transposed else (b <= a)


def _rows8(x):
    return jnp.transpose(x)[:8, :]


def _attn_fwd(mode, q, k, v, bias, tq, name):
    t, hd = q.shape
    nh = hd // HEAD_DIM
    nb = t // tq
    wb = MAX_WINDOW // tq
    fox = mode == "fox"

    def body(q_ref, k_ref, v_ref, b_ref, o_ref, lse_ref, lse_row_ref, s_ref, p_ref, m_ref, l_ref, acc_ref):
        qi = pl.program_id(1)
        qb = q_ref[...]
        m_ref[...] = jnp.full_like(m_ref, NEG)
        l_ref[...] = jnp.zeros_like(l_ref)
        acc_ref[...] = jnp.zeros_like(acc_ref)

        def tile(kj, diag):
            off = pl.multiple_of(kj * tq, tq)
            s_ref[...] = _dot(qb, k_ref[pl.ds(off, tq), :], NT)
            if fox:
                brow = b_ref[qi][:, :1] - b_ref[kj]

            def chunk(r0):
                rows = pl.ds(r0, ATTN_ROWS)
                if fox:
                    s = s_ref[rows, :] + brow
                    if diag:
                        s = jnp.where(_causal(r0, tq, False), s, NEG)
                else:
                    s = s_ref[rows, :] + b_ref[qi - kj, rows, :]
                m_old = m_ref[rows, :]
                m_new = jnp.maximum(m_old, jnp.max(s, axis=1, keepdims=True))
                p = jnp.exp(s - _rep(m_new, tq))
                alpha = jnp.exp(m_old - m_new)
                l_ref[rows, :] = alpha * l_ref[rows, :] + jnp.sum(p, axis=1, keepdims=True)
                m_ref[rows, :] = m_new
                acc_ref[rows, :] = alpha * acc_ref[rows, :]
                p_ref[rows, :] = p.astype(BF)

            _chunks(tq, chunk)
            acc_ref[...] += _dot(p_ref[...], v_ref[pl.ds(off, tq), :])

        tile(qi, True)
        if fox:
            lax.fori_loop(0, qi, lambda kj, c: (tile(kj, False), c)[1], 0)
        else:
            lax.fori_loop(1, jnp.minimum(qi, wb) + 1, lambda i, c: (tile(qi - i, False), c)[1], 0)
        o_ref[...] = (acc_ref[...] / l_ref[...]).astype(BF)
        lse = m_ref[...] + jnp.log(l_ref[...])
        lse_ref[...] = lse
        lse_row_ref[...] = _rows8(lse)

    qspec = pl.BlockSpec((tq, HEAD_DIM), lambda h, i: (i, h))
    kvspec = pl.BlockSpec((t, HEAD_DIM), lambda h, i: (0, h))
    repspec = pl.BlockSpec((None, tq, LANE), lambda h, i: (h, i, 0))
    row8spec = pl.BlockSpec((None, None, 8, tq), lambda h, i: (h, i, 0, 0))
    if fox:
        bspec = pl.BlockSpec((None, nb, 1, tq), lambda h, i: (h, 0, 0, 0))
    else:
        bspec = pl.BlockSpec((wb + 1, tq, tq), lambda h, i: (0, 0, 0))
    return pl.pallas_call(
        body, name=name, grid=(nh, nb), in_specs=[qspec, kvspec, kvspec, bspec],
        out_specs=[qspec, repspec, row8spec],
        out_shape=[jax.ShapeDtypeStruct((t, hd), BF), jax.ShapeDtypeStruct((nh, t, LANE), F32),
                   jax.ShapeDtypeStruct((nh, nb, 8, tq), F32)],
        scratch_shapes=[pltpu.VMEM((tq, tq), F32), pltpu.VMEM((tq, tq), BF), pltpu.VMEM((tq, LANE), F32),
                        pltpu.VMEM((tq, LANE), F32), pltpu.VMEM((tq, HEAD_DIM), F32)],
        compiler_params=_params(),
    )(q, k, v, bias)


def _attn_bwd_dq(mode, q, k, v, o, do, lse, bias, tq, name, dep=None):
    t, hd = q.shape
    nh = hd // HEAD_DIM
    nb = t // tq
    wb = MAX_WINDOW // tq
    fox = mode == "fox"

    def body(q_ref, k_ref, v_ref, o_ref, do_ref, lse_ref, b_ref, dq_ref, dl_row_ref,
             s_ref, dp_ref, x_ref, y_ref, acc_ref, acc2_ref, dl_ref):
        qi = pl.program_id(1)
        qb = q_ref[...]
        dob = do_ref[...]
        acc_ref[...] = jnp.zeros_like(acc_ref)
        if fox:
            acc2_ref[...] = jnp.zeros_like(acc2_ref)
            dl_ref[...] = jnp.zeros_like(dl_ref)
        else:
            prod = o_ref[...].astype(F32) * dob.astype(F32)
            dl_ref[...] = jnp.broadcast_to(jnp.sum(prod, axis=1, keepdims=True), (tq, LANE))

        def tile(kj, diag):
            off = pl.multiple_of(kj * tq, tq)
            kb = k_ref[pl.ds(off, tq), :]
            s_ref[...] = _dot(qb, kb, NT)
            dp_ref[...] = _dot(dob, v_ref[pl.ds(off, tq), :], NT)
            if fox:
                brow = b_ref[qi][:, :1] - b_ref[kj]

            def chunk(r0):
                rows = pl.ds(r0, ATTN_ROWS)
                lse_c = _rep(lse_ref[rows, :], tq)
                if fox:
                    s = s_ref[rows, :] + brow
                    if diag:
                        s = jnp.where(_causal(r0, tq, False), s, NEG)
                    p = jnp.exp(s - lse_c)
                    pdp = p * dp_ref[rows, :]
                    dl_ref[rows, :] += jnp.sum(pdp, axis=1, keepdims=True)
                    x_ref[rows, :] = pdp.astype(BF)
                    y_ref[rows, :] = p.astype(BF)
                else:
                    p = jnp.exp(s_ref[rows, :] + b_ref[qi - kj, rows, :] - lse_c)
                    x_ref[rows, :] = (p * (dp_ref[rows, :] - _rep(dl_ref[rows, :], tq))).astype(BF)

            _chunks(tq, chunk)
            acc_ref[...] += _dot(x_ref[...], kb)
            if fox:
                acc2_ref[...] += _dot(y_ref[...], kb)

        tile(qi, True)
        if fox:
            lax.fori_loop(0, qi, lambda kj, c: (tile(kj, False), c)[1], 0)
            dq_ref[...] = acc_ref[...] - dl_ref[...] * acc2_ref[...]
        else:
            lax.fori_loop(1, jnp.minimum(qi, wb) + 1, lambda i, c: (tile(qi - i, False), c)[1], 0)
            dq_ref[...] = acc_ref[...]
        dl_row_ref[...] = _rows8(dl_ref[...])

    qspec = pl.BlockSpec((tq, HEAD_DIM), lambda h, i: (i, h))
    kvspec = pl.BlockSpec((t, HEAD_DIM), lambda h, i: (0, h))
    repspec = pl.BlockSpec((None, tq, LANE), lambda h, i: (h, i, 0))
    row8spec = pl.BlockSpec((None, None, 8, tq), lambda h, i: (h, i, 0, 0))
    if fox:
        bspec = pl.BlockSpec((None, nb, 1, tq), lambda h, i: (h, 0, 0, 0))
    else:
        bspec = pl.BlockSpec((wb + 1, tq, tq), lambda h, i: (0, 0, 0))
    return _call(
        body, [q, k, v, o, do, lse, bias], dep=dep, name=name, grid=(nh, nb),
        in_specs=[qspec, kvspec, kvspec, qspec, qspec, repspec, bspec],
        out_specs=[qspec, row8spec],
        out_shape=[jax.ShapeDtypeStruct((t, hd), F32), jax.ShapeDtypeStruct((nh, nb, 8, tq), F32)],
        scratch_shapes=[pltpu.VMEM((tq, tq), F32), pltpu.VMEM((tq, tq), F32), pltpu.VMEM((tq, tq), BF),
                        pltpu.VMEM((tq, tq), BF), pltpu.VMEM((tq, HEAD_DIM), F32),
                        pltpu.VMEM((tq, HEAD_DIM), F32), pltpu.VMEM((tq, LANE), F32)],
        compiler_params=_params(),
    )


def _attn_bwd_dkv(mode, q, k, v, do, lse_row, dl_row, bias_t, c_row, tq, name):
    t, hd = q.shape
    nh = hd // HEAD_DIM
    nb = t // tq
    wb = MAX_WINDOW // tq
    fox = mode == "fox"

    def body(*refs):
        if fox:
            (q_ref, k_ref, v_ref, do_ref, lse_ref, dl_ref, b_ref, cq_ref, dk_ref, dv_ref, dc_row_ref,
             s_ref, dp_ref, x_ref, y_ref, dc_ref) = refs
        else:
            q_ref, k_ref, v_ref, do_ref, lse_ref, dl_ref, b_ref, dk_ref, dv_ref, s_ref, dp_ref, x_ref, y_ref = refs
        kj = pl.program_id(1)
        kb = k_ref[...]
        vb = v_ref[...]
        dk_ref[...] = jnp.zeros_like(dk_ref)
        dv_ref[...] = jnp.zeros_like(dv_ref)
        if fox:
            dc_ref[...] = jnp.zeros_like(dc_ref)

        def tile(qi, diag):
            off = pl.multiple_of(qi * tq, tq)
            qb = q_ref[pl.ds(off, tq), :]
            dob = do_ref[pl.ds(off, tq), :]
            s_ref[...] = _dot(kb, qb, NT)
            dp_ref[...] = _dot(vb, dob, NT)
            lse_r = lse_ref[qi, 0:1, :]
            dl_r = dl_ref[qi, 0:1, :]
            if fox:
                kbias = cq_ref[qi][:, :1] - b_ref[...]

            def chunk(r0):
                rows = pl.ds(r0, ATTN_ROWS)
                if fox:
                    s = s_ref[rows, :] + _rep(kbias[r0:r0 + ATTN_ROWS, :], tq)
                    if diag:
                        s = jnp.where(_causal(r0, tq, True), s, NEG)
                else:
                    s = s_ref[rows, :] + b_ref[qi - kj, rows, :]
                pt = jnp.exp(s - lse_r)
                dst = pt * (dp_ref[rows, :] - dl_r)
                x_ref[rows, :] = pt.astype(BF)
                y_ref[rows, :] = dst.astype(BF)
                if fox:
                    dc_ref[rows, :] -= jnp.sum(dst, axis=1, keepdims=True)

            _chunks(tq, chunk)
            dv_ref[...] += _dot(x_ref[...], dob)
            dk_ref[...] += _dot(y_ref[...], qb)

        tile(kj, True)
        hi = nb if fox else jnp.minimum(kj + wb + 1, nb)
        lax.fori_loop(kj + 1, hi, lambda qi, c: (tile(qi, False), c)[1], 0)
        if fox:
            dc_row_ref[...] = _rows8(dc_ref[...])

    blkspec = pl.BlockSpec((tq, HEAD_DIM), lambda h, j: (j, h))
    fullspec = pl.BlockSpec((t, HEAD_DIM), lambda h, j: (0, h))
    rows8spec = pl.BlockSpec((None, nb, 8, tq), lambda h, j: (h, 0, 0, 0))
    repspec = pl.BlockSpec((None, tq, LANE), lambda h, j: (h, j, 0))
    in_specs = [fullspec, blkspec, blkspec, fullspec, rows8spec, rows8spec]
    args = [q, k, v, do, lse_row, dl_row, bias_t]
    out_specs = [blkspec, blkspec]
    out_shape = [jax.ShapeDtypeStruct((t, hd), F32), jax.ShapeDtypeStruct((t, hd), F32)]
    scratch = [pltpu.VMEM((tq, tq), F32), pltpu.VMEM((tq, tq), F32), pltpu.VMEM((tq, tq), BF),
               pltpu.VMEM((tq, tq), BF)]
    if fox:
        in_specs += [repspec, pl.BlockSpec((None, nb, 1, tq), lambda h, j: (h, 0, 0, 0))]
        args.append(c_row)
        out_specs.append(pl.BlockSpec((None, None, 8, tq), lambda h, j: (h, j, 0, 0)))
        out_shape.append(jax.ShapeDtypeStruct((nh, nb, 8, tq), F32))
        scratch.append(pltpu.VMEM((tq, LANE), F32))
    else:
        in_specs.append(pl.BlockSpec((wb + 1, tq, tq), lambda h, j: (0, 0, 0)))
    return pl.pallas_call(
        body, name=name, grid=(nh, nb), in_specs=in_specs, out_specs=out_specs, out_shape=out_shape,
        scratch_shapes=scratch, compiler_params=_params(),
    )(*args)


def _gate_specs(t, d, hd, tr):
    row = pl.BlockSpec((tr, d), lambda i: (i, 0))
    vec = pl.BlockSpec((1, d), lambda i: (0, 0))
    base = 6 * hd // d
    gd = pl.BlockSpec((tr, d), lambda i: (i, base))
    gf = pl.BlockSpec((tr, d), lambda i: (i, base + 1))
    return row, vec, gd, gf


def _proj_merge(yd, yf, wpd, wpf, proj, b_d, b_f, hd):
    t = yd.shape[0]
    d = wpd.shape[1]
    tr = _tile(t, 256, 16)
    row, vec, gd, gf = _gate_specs(t, d, hd, tr)

    def body(yd_ref, yf_ref, wd_ref, wf_ref, gd_ref, gf_ref, bd_ref, bf_ref, pd_ref, pf_ref, o_ref):
        pd = _dot(yd_ref[...], wd_ref[...])
        pf = _dot(yf_ref[...], wf_ref[...])
        pd_ref[...] = pd
        pf_ref[...] = pf
        o_ref[...] = (_sig(gd_ref[...] + bd_ref[...]) * pd + _sig(gf_ref[...] + bf_ref[...]) * pf).astype(BF)

    yspec = pl.BlockSpec((tr, hd), lambda i: (i, 0))
    wspec = pl.BlockSpec((hd, d), lambda i: (0, 0))
    f32 = jax.ShapeDtypeStruct((t, d), F32)
    return pl.pallas_call(
        body, name="proj_merge", grid=(t // tr,), in_specs=[yspec, yspec, wspec, wspec, gd, gf, vec, vec],
        out_specs=[row, row, row], out_shape=[f32, f32, jax.ShapeDtypeStruct((t, d), BF)],
        compiler_params=_params(),
    )(yd, yf, wpd, wpf, proj, proj, b_d, b_f)


def _merge_bwd(dm, pd, pf, proj, b_d, b_f, hd):
    t, d = pd.shape
    tr = _tile(t, 256, 16)
    row, vec, gd, gf = _gate_specs(t, d, hd, tr)

    def body(dm_ref, pd_ref, pf_ref, gd_ref, gf_ref, bd_ref, bf_ref,
             dpd_ref, dpf_ref, dgd_ref, dgf_ref, dbd_ref, dbf_ref):
        dmv = dm_ref[...]
        sd = _sig(gd_ref[...] + bd_ref[...])
        sf = _sig(gf_ref[...] + bf_ref[...])
        dgd = dmv * pd_ref[...] * (sd * (1.0 - sd))
        dgf = dmv * pf_ref[...] * (sf * (1.0 - sf))
        dpd_ref[...] = (dmv * sd).astype(BF)
        dpf_ref[...] = (dmv * sf).astype(BF)
        dgd_ref[...] = dgd.astype(BF)
        dgf_ref[...] = dgf.astype(BF)

        @pl.when(pl.program_id(0) == 0)
        def _():
            dbd_ref[...] = jnp.zeros_like(dbd_ref)
            dbf_ref[...] = jnp.zeros_like(dbf_ref)

        dbd_ref[...] += jnp.sum(dgd, axis=0, keepdims=True)
        dbf_ref[...] += jnp.sum(dgf, axis=0, keepdims=True)

    ob = jax.ShapeDtypeStruct((t, d), BF)
    ov = jax.ShapeDtypeStruct((1, d), F32)
    return pl.pallas_call(
        body, name="merge_bwd", grid=(t // tr,), in_specs=[row, row, row, gd, gf, vec, vec],
        out_specs=[row, row, row, row, vec, vec], out_shape=[ob, ob, ob, ob, ov, ov],
        compiler_params=_params(),
    )(dm, pd, pf, proj, proj, b_d, b_f)


def _assemble_dproj(dqd, dkd, dvd, dqf, dkf, dvf, dgd, dgf, dlogf, proj, tables, bf_pad, scale):
    t, np_ = proj.shape
    hd = dqd.shape[1]
    d = dgd.shape[1]
    nh = hd // HEAD_DIM
    tr = _tile(t, 256, 16)
    f_blk = np_ // LANE - 1

    def body(dqd_ref, dkd_ref, dvd_ref, dqf_ref, dkf_ref, dvf_ref, dgd_ref, dgf_ref, dlog_ref, fl_ref,
             c_ref, s1_ref, s2_ref, b_ref, o_ref, db_ref):
        c, s1, s2 = c_ref[...], s1_ref[...], s2_ref[...]
        for h in range(nh):
            sl = slice(h * HEAD_DIM, (h + 1) * HEAD_DIM)
            o_ref[:, sl] = (_rope_t(dqd_ref[:, sl], c, s1, s2) * scale).astype(BF)
            o_ref[:, hd + h * HEAD_DIM:hd + (h + 1) * HEAD_DIM] = _rope_t(dkd_ref[:, sl], c, s1, s2).astype(BF)
        o_ref[:, 2 * hd:3 * hd] = dvd_ref[...].astype(BF)
        o_ref[:, 3 * hd:4 * hd] = (dqf_ref[...] * scale).astype(BF)
        o_ref[:, 4 * hd:5 * hd] = dkf_ref[...].astype(BF)
        o_ref[:, 5 * hd:6 * hd] = dvf_ref[...].astype(BF)
        o_ref[:, 6 * hd:6 * hd + d] = dgd_ref[...]
        o_ref[:, 6 * hd + d:6 * hd + 2 * d] = dgf_ref[...]
        z = fl_ref[...] + b_ref[...]
        dfl = dlog_ref[...] * _sig(-z)
        o_ref[:, 6 * hd + 2 * d:] = dfl.astype(BF)

        @pl.when(pl.program_id(0) == 0)
        def _():
            db_ref[...] = jnp.zeros_like(db_ref)

        db_ref[...] += jnp.sum(dfl, axis=0, keepdims=True)

    head = pl.BlockSpec((tr, hd), lambda i: (i, 0))
    row = pl.BlockSpec((tr, d), lambda i: (i, 0))
    lane_row = pl.BlockSpec((tr, LANE), lambda i: (i, 0))
    lane_vec = pl.BlockSpec((1, LANE), lambda i: (0, 0))
    return pl.pallas_call(
        body, name="assemble_dproj", grid=(t // tr,),
        in_specs=[head] * 6 + [row, row, lane_row, pl.BlockSpec((tr, LANE), lambda i: (i, f_blk)),
                               lane_row, lane_row, lane_row, lane_vec],
        out_specs=[pl.BlockSpec((tr, np_), lambda i: (i, 0)), lane_vec],
        out_shape=[jax.ShapeDtypeStruct((t, np_), BF), jax.ShapeDtypeStruct((1, LANE), F32)],
        compiler_params=_params(),
    )(dqd, dkd, dvd, dqf, dkf, dvf, dgd, dgf, dlogf, proj, *tables, bf_pad)


def _to_rows(a, tq):
    h, t = a.shape
    return a.reshape(h, t // tq, 1, tq)


def kernel(x, ffn1_norm, ffn1_w_gate, ffn1_w_up, ffn1_w_down, mix_norm, w_in, b_forget, b_gate_dil, b_gate_fox, w_proj_dil, w_proj_fox, w_out, ffn2_norm, ffn2_w_gate, ffn2_w_up, ffn2_w_down, final_norm, loss_target, m_ffn1_norm, m_ffn1_w_gate, m_ffn1_w_up, m_ffn1_w_down, m_mix_norm, m_w_in, m_b_forget, m_b_gate_dil, m_b_gate_fox, m_w_proj_dil, m_w_proj_fox, m_w_out, m_ffn2_norm, m_ffn2_w_gate, m_ffn2_w_up, m_ffn2_w_down, m_final_norm, v_ffn1_norm, v_ffn1_w_gate, v_ffn1_w_up, v_ffn1_w_down, v_mix_norm, v_w_in, v_b_forget, v_b_gate_dil, v_b_gate_fox, v_w_proj_dil, v_w_proj_fox, v_w_out, v_ffn2_norm, v_ffn2_w_gate, v_ffn2_w_up, v_ffn2_w_down, v_final_norm):
    t, d = x.shape[1], x.shape[2]
    hd = w_proj_dil.shape[1]
    nh = hd // HEAD_DIM
    n_f = b_forget.shape[1]
    cols = w_in.shape[2]
    in_cols = N_DEV * cols
    assert in_cols == 6 * hd + n_f + 2 * d and n_f == nh and n_f <= LANE
    np_ = 6 * hd + 2 * d + LANE
    scale = HEAD_DIM ** -0.5
    tq = _tile(t, 512, LANE)
    assert MAX_WINDOW % tq == 0 and tq % 16 == 0

    x2d = x[0]
    tgt = loss_target[0]

    ag_order = [ffn1_w_gate, ffn1_w_up, ffn1_w_down, w_in, w_proj_dil, w_proj_fox, w_out,
                ffn2_w_gate, ffn2_w_up, ffn2_w_down]
    ag_first, tok = _exchange_start([w[0].astype(BF) for w in ag_order[:2]], True, "ag_start_first")
    ag_rest, ag_token = _exchange_start([w[0].astype(BF) for w in ag_order[2:]], True, "ag_start", dep=tok)
    ag = ag_first + ag_rest

    def gathered(idx, after, name):
        return _exchange_wait([ag[i] for i in idx], True, after, name)

    tables = _rope_tables(t)
    bf_pad = jnp.pad(b_forget, ((0, 0), (0, LANE - n_f)))

    hn1, hn1_t = _rms_fwd(x2d, ffn1_norm, "rms_ffn1", dep=ag_token)
    wg1, wu1 = gathered([0, 1], hn1, "ag_wait_ffn1_gate_up")
    g1, u1, a1 = _ffn_gate_up(hn1, wg1, wu1, "ffn1_gate_up")
    wd1, = gathered([2], a1, "ag_wait_ffn1_down")
    x1 = _ffn_down(a1, wd1, x2d, "ffn1_down")

    hm, hm_t = _rms_fwd(x1, mix_norm, "rms_mix")
    win_g, = gathered([3], hm, "ag_wait_w_in")
    segments = [(0, 6 * hd), (6 * hd + n_f, in_cols), (6 * hd, 6 * hd + n_f)]
    pieces = []
    for lo, hi in segments:
        for j in range(lo // cols, (hi - 1) // cols + 1):
            s, e = max(lo, j * cols), min(hi, (j + 1) * cols)
            pieces.append(win_g[j, :, s - j * cols:e - j * cols])
    win_p = jnp.concatenate(pieces + [jnp.zeros((d, LANE - n_f), BF)], axis=1)
    proj = _mm_nn(hm, win_p, F32, "w_in_fwd")
    qd, kd, vd, qf, kf, vf, logf = _mixer_prep(proj, tables, bf_pad, hd, scale)
    csum = _cumsum_rows(logf, False, "cumsum_logf")
    c_heads = csum[:, :nh].T
    c_row = _to_rows(c_heads, tq)
    c_rep = jnp.broadcast_to(c_heads[:, :, None], (nh, t, LANE))
    dil_bias = _dil_bias_tiles(tq)
    dil_bias_t = dil_bias.transpose(0, 2, 1)
    yd, lse_d, lse_d_row = _attn_fwd("dil", qd, kd, vd, dil_bias, tq, "attn_dil_fwd")
    yf, lse_f, lse_f_row = _attn_fwd("fox", qf, kf, vf, c_row, tq, "attn_fox_fwd")
    wpd_g, wpf_g = gathered([4, 5], yf, "ag_wait_proj")
    wpd = wpd_g.transpose(1, 0, 2).reshape(hd, d)
    wpf = wpf_g.transpose(1, 0, 2).reshape(hd, d)
    pd, pf, merged = _proj_merge(yd, yf, wpd, wpf, proj, b_gate_dil, b_gate_fox, hd)
    wout_g, = gathered([6], merged, "ag_wait_w_out")
    wout = wout_g.reshape(d, d)
    x2 = _mm_nn(merged, wout, F32, "w_out_fwd", residual=x1, tn_pref=1024)

    hn2, hn2_t = _rms_fwd(x2, ffn2_norm, "rms_ffn2")
    wg2, wu2 = gathered([7, 8], hn2, "ag_wait_ffn2_gate_up")
    g2, u2, a2 = _ffn_gate_up(hn2, wg2, wu2, "ffn2_gate_up")
    wd2, = gathered([9], a2, "ag_wait_ffn2_down")
    x3 = _ffn_down(a2, wd2, x2, "ffn2_down")

    dx3, dx3b, d_final, loss_lanes = _loss_head(x3, final_norm.reshape(1, d), tgt)

    def ffn_bwd(dxb, hn_t, g, u, a, wg, wu, wd, x_in, gain, dres, tag):
        dg, du = _ffn_bwd_hidden(dxb, wd, g, u, tag + "_bwd_hidden")
        dwd = _ffn_dw_down(a, dxb, tag + "_dw_down")
        rs_down, tok = _exchange_start([dwd], False, "rs_start_" + tag + "_down")
        dwg, dwu = _ffn_dw_gate_up(hn_t, dg, du, tag + "_dw_gate_up", dep=tok)
        rs_gu, tok = _exchange_start([dwg, dwu], False, "rs_start_" + tag + "_gate_up")
        dx, dx_bf, dgain = _ffn_bwd_input(dg, du, wg, wu, x_in, gain, dres, tag + "_bwd_input", dep=tok)
        return dx, dx_bf, dgain, rs_gu + rs_down

    dx2, dx2b, d_ffn2_norm, rs_ffn2 = ffn_bwd(dx3b, hn2_t, g2, u2, a2, wg2, wu2, wd2, x2, ffn2_norm, dx3, "ffn2")

    dmerged = _mm_nt(dx2b, wout, F32, "w_out_bwd")
    dwout = _mm_tn(merged, dx2b, BF, "w_out_dw", tn_pref=1024)
    dpd, dpf, dgd, dgf, d_bd, d_bf = _merge_bwd(dmerged, pd, pf, proj, b_gate_dil, b_gate_fox, hd)
    dyd = _mm_nt(dpd, wpd, BF, "proj_dil_bwd")
    dyf = _mm_nt(dpf, wpf, BF, "proj_fox_bwd")
    dwpd = _mm_tn(yd, dpd, BF, "proj_dil_dw", tn_pref=1024)
    dwpf = _mm_tn(yf, dpf, BF, "proj_fox_dw", tn_pref=1024)
    dwpd_c = dwpd.reshape(hd, N_DEV, d // N_DEV).transpose(1, 0, 2)
    dwpf_c = dwpf.reshape(hd, N_DEV, d // N_DEV).transpose(1, 0, 2)
    dwout_c = dwout.reshape(N_DEV, d // N_DEV, d)
    rs_mix, tok = _exchange_start([dwout_c, dwpd_c, dwpf_c], False, "rs_start_mixer")

    dqd, dl_d = _attn_bwd_dq("dil", qd, kd, vd, yd, dyd, lse_d, dil_bias, tq, "attn_dil_dq", dep=tok)
    dkd, dvd = _attn_bwd_dkv("dil", qd, kd, vd, dyd, lse_d_row, dl_d, dil_bias_t, None, tq, "attn_dil_dkv")
    dqf, dl_f = _attn_bwd_dq("fox", qf, kf, vf, yf, dyf, lse_f, c_row, tq, "attn_fox_dq")
    dkf, dvf, dc = _attn_bwd_dkv("fox", qf, kf, vf, dyf, lse_f_row, dl_f, c_rep, c_row, tq, "attn_fox_dkv")
    dc_pad = jnp.pad(dc[:, :, 0, :].reshape(nh, t).T, ((0, 0), (0, LANE - nh)))
    dlogf = _cumsum_rows(dc_pad, True, "revcumsum_dc")
    dproj, d_bforget = _assemble_dproj(dqd, dkd, dvd, dqf, dkf, dvf, dgd, dgf, dlogf, proj, tables, bf_pad, scale)

    dwin_p = _mm_tn(hm_t, dproj, BF, "w_in_dw", tk_pref=DW_ROWS, a_transposed=True)
    def perm_col(c):
        if c < 6 * hd:
            return c
        return c + 2 * d if c < 6 * hd + n_f else c - n_f

    shards = []
    for j in range(N_DEV):
        cuts = sorted({j * cols, (j + 1) * cols} | {c for c in (6 * hd, 6 * hd + n_f) if j * cols < c < (j + 1) * cols})
        shards.append(jnp.concatenate([dwin_p[:, perm_col(lo):perm_col(lo) + hi - lo]
                                       for lo, hi in zip(cuts[:-1], cuts[1:])], axis=1))
    dwin_c = jnp.stack(shards)
    rs_win, tok = _exchange_start([dwin_c], False, "rs_start_w_in")
    dx1, dx1b, d_mix_norm = _mm_nt(dproj, win_p, F32, "w_in_bwd", tn_pref=d, tk_pref=1152,
                                   rms=(x1, mix_norm, dx2), dep=tok)

    grad_x, _, d_ffn1_norm, rs_ffn1 = ffn_bwd(dx1b, hn1_t, g1, u1, a1, wg1, wu1, wd1, x2d, ffn1_norm, dx1, "ffn1")

    def update(handles, names, after, tag):
        recvs = _exchange_wait(handles, False, after, "rs_wait_" + tag)
        res = {}
        for recv, n in zip(recvs, names):
            w, m, v = wmv[n]
            g, delta, m2, v2 = _adam_from_partials(recv, w[0], m[0], v[0], "adam_" + n)
            res[n] = (g[None], delta[None], m2[None], v2[None])
        return res, g

    wmv = {
        "ffn1_w_gate": (ffn1_w_gate, m_ffn1_w_gate, v_ffn1_w_gate),
        "ffn1_w_up": (ffn1_w_up, m_ffn1_w_up, v_ffn1_w_up),
        "ffn1_w_down": (ffn1_w_down, m_ffn1_w_down, v_ffn1_w_down),
        "w_in": (w_in, m_w_in, v_w_in),
        "w_proj_dil": (w_proj_dil, m_w_proj_dil, v_w_proj_dil),
        "w_proj_fox": (w_proj_fox, m_w_proj_fox, v_w_proj_fox),
        "w_out": (w_out, m_w_out, v_w_out),
        "ffn2_w_gate": (ffn2_w_gate, m_ffn2_w_gate, v_ffn2_w_gate),
        "ffn2_w_up": (ffn2_w_up, m_ffn2_w_up, v_ffn2_w_up),
        "ffn2_w_down": (ffn2_w_down, m_ffn2_w_down, v_ffn2_w_down),
    }
    big = {}
    after = grad_x
    for handles, names, tag in [
            (rs_ffn2, ["ffn2_w_gate", "ffn2_w_up", "ffn2_w_down"], "ffn2"),
            (rs_mix, ["w_out", "w_proj_dil", "w_proj_fox"], "mixer"),
            (rs_win, ["w_in"], "w_in"),
            (rs_ffn1, ["ffn1_w_gate", "ffn1_w_up", "ffn1_w_down"], "ffn1")]:
        res, after = update(handles, names, after, tag)
        big.update(res)

    def lanes(a):
        a = a.reshape(1, -1)
        return jnp.pad(a, ((0, 0), (0, d - a.shape[1])))

    small_names = ["ffn1_norm", "mix_norm", "b_gate_dil", "b_gate_fox", "ffn2_norm", "final_norm", "b_forget"]
    small_g = [d_ffn1_norm, d_mix_norm, d_bd, d_bf, d_ffn2_norm, d_final, d_bforget[:, :n_f]]
    small_w = [ffn1_norm, mix_norm, b_gate_dil, b_gate_fox, ffn2_norm, final_norm, b_forget]
    small_m = [m_ffn1_norm, m_mix_norm, m_b_gate_dil, m_b_gate_fox, m_ffn2_norm, m_final_norm, m_b_forget]
    small_v = [v_ffn1_norm, v_mix_norm, v_b_gate_dil, v_b_gate_fox, v_ffn2_norm, v_final_norm, v_b_forget]
    pack = lambda arrs, last: jnp.concatenate([lanes(a) for a in arrs] + [last], axis=0)
    g_all = _allreduce_small(pack(small_g, loss_lanes))
    zero_row = jnp.zeros((1, d), F32)
    one_row = jnp.ones((1, d), F32)
    s_delta, s_m, s_v = _adam_small(g_all, pack(small_w, zero_row), pack(small_m, zero_row), pack(small_v, one_row))
    loss = g_all[len(small_names), 0]

    def unpack(packed, i, like):
        return packed[i, :like.size].reshape(like.shape)

    small = {}
    for i, (n, w) in enumerate(zip(small_names, small_w)):
        small[n] = (unpack(g_all, i, w), unpack(s_delta, i, w), unpack(s_m, i, w), unpack(s_v, i, w))

    order = ["ffn1_norm", "ffn1_w_gate", "ffn1_w_up", "ffn1_w_down", "mix_norm", "w_in", "b_forget", "b_gate_dil",
             "b_gate_fox", "w_proj_dil", "w_proj_fox", "w_out", "ffn2_norm", "ffn2_w_gate", "ffn2_w_up",
             "ffn2_w_down", "final_norm"]
    res = {**big, **small}
    outs = [loss, grad_x[None]]
    for slot in range(4):
        outs += [res[n][slot] for n in order]
    return tuple(outs)
```

```python
import functools

import numpy as np
import jax
import jax.numpy as jnp
from jax import lax
from jax.experimental import pallas as pl
from jax.experimental.pallas import tpu as pltpu

BF = jnp.bfloat16
F32 = jnp.float32
MESH = pl.DeviceIdType.MESH
N_DEV = 8

HEAD_DIM = 128
ROPE_DIM = HEAD_DIM // 4
ROPE_HALF = ROPE_DIM // 2
ROPE_THETA = 500000.0
NORM_EPS = 1e-6
DIL_PATTERNS = ((128, 1), (512, 4), (2048, 16))
MAX_WINDOW = 2048
LANE = 128
NEG = -1e30

ADAM_LR = 0.001
ADAM_B1 = 0.9
ADAM_B2 = 0.999
ADAM_EPS = 1e-08
ADAM_WD = 0.01
ADAM_STEP = 10

VMEM_LIMIT_BYTES = 56 * 1024 * 1024
FFN_ROWS = 1024
DW_ROWS = 1024
ANY = pl.BlockSpec(memory_space=pl.ANY)

NN = (((1,), (0,)), ((), ()))
NT = (((1,), (1,)), ((), ()))
TN = (((0,), (0,)), ((), ()))


def _dot(a, b, dn=NN):
    return lax.dot_general(a, b, dn, preferred_element_type=F32)


def _sig(x):
    return 1.0 / (1.0 + jnp.exp(-x))


def _tile(n, pref, align):
    best = None
    t = align
    while t <= min(n, pref):
        if n % t == 0:
            best = t
        t += align
    return n if best is None else best


def _params():
    return pltpu.CompilerParams(vmem_limit_bytes=VMEM_LIMIT_BYTES)


def _call(body, args, dep=None, **kw):
    if dep is not None:
        n_in = len(args)
        inner = body

        def body(*refs):
            inner(*refs[:n_in], *refs[n_in + 1:])

        kw["in_specs"] = list(kw["in_specs"]) + [ANY]
        args = list(args) + [dep]
    return pl.pallas_call(body, **kw)(*args)


def _peers():
    x, y, c = lax.axis_index("x"), lax.axis_index("y"), lax.axis_index("c")
    me = 4 * x + 2 * y + c
    peers = []
    for k in range(1, N_DEV):
        px = 1 - x if (k >> 2) & 1 else x
        py = 1 - y if (k >> 1) & 1 else y
        pc = 1 - c if k & 1 else c
        peers.append((k, (px, py, pc), 4 * px + 2 * py + pc))
    return me, peers


HBM = pl.BlockSpec(memory_space=pltpu.HBM)
SEM = pl.BlockSpec(memory_space=pltpu.SEMAPHORE)
EFFECT = pltpu.SideEffectType.DATAFLOW_SIDE_EFFECTING


def _exchange_copy(gather, src_ref, land_ref, send_sems, recv_sems, me, k, peer, peer_flat, landing):
    return pltpu.make_async_remote_copy(
        src_ref=src_ref if gather else src_ref.at[peer_flat], dst_ref=land_ref.at[landing],
        send_sem=send_sems.at[k], recv_sem=recv_sems.at[k], device_id=peer, device_id_type=MESH)


def _exchange_start(srcs, gather, name, dep=None):
    n = len(srcs)
    extra = [] if dep is None else [dep]

    def body(*refs):
        src_refs, land_refs = refs[:n], refs[n:2 * n]
        refs = refs[2 * n + len(extra):]
        send_refs, recv_refs = refs[:n], refs[n:2 * n]
        token = refs[4 * n]
        me, peers = _peers()
        for i in range(n):
            for k, peer, peer_flat in peers:
                _exchange_copy(gather, src_refs[i], land_refs[i], send_refs[i], recv_refs[i],
                               me, k, peer, peer_flat, me).start()
        token[...] = jnp.zeros_like(token)

    lands = [lax.empty((N_DEV,) + s.shape[-2:], s.dtype) for s in srcs]
    sems = [pltpu.SemaphoreType.DMA((N_DEV,)) for _ in range(2 * n)]
    out = pl.pallas_call(
        body, name=name,
        out_shape=tuple(sems) + tuple(pltpu.HBM(a.shape, a.dtype) for a in list(srcs) + lands)
        + (jax.ShapeDtypeStruct((8, LANE), F32),),
        in_specs=[HBM] * (2 * n) + [ANY] * len(extra),
        out_specs=tuple([SEM] * (2 * n) + [HBM] * (2 * n) + [pl.BlockSpec(memory_space=pltpu.VMEM)]),
        input_output_aliases={i: 2 * n + i for i in range(2 * n)},
        compiler_params=pltpu.CompilerParams(has_side_effects=EFFECT),
    )(*[pltpu.with_memory_space_constraint(a, pltpu.HBM) for a in list(srcs) + lands], *extra)
    handles = [(out[2 * n + i], out[3 * n + i], out[i], out[n + i]) for i in range(n)]
    return handles, out[4 * n]


def _exchange_wait(handles, gather, after, name):
    n = len(handles)

    def body(*refs):
        src_refs, land_refs = refs[:n], refs[n:2 * n]
        send_refs, recv_refs = refs[2 * n:3 * n], refs[3 * n:4 * n]
        me, peers = _peers()
        for i in range(n):
            for k, peer, peer_flat in peers:
                cp = _exchange_copy(gather, src_refs[i], land_refs[i], send_refs[i], recv_refs[i],
                                    me, k, peer, peer_flat, peer_flat)
                cp.wait_send()
                cp.wait_recv()

    srcs = [h[0] for h in handles]
    lands = [h[1] for h in handles]
    out = pl.pallas_call(
        body, name=name,
        out_shape=tuple(pltpu.HBM(a.shape, a.dtype) for a in srcs + lands),
        in_specs=[HBM] * (2 * n) + [SEM] * (2 * n) + [ANY],
        out_specs=tuple([HBM] * (2 * n)),
        input_output_aliases={i: i for i in range(2 * n)},
        compiler_params=pltpu.CompilerParams(has_side_effects=EFFECT),
    )(*srcs, *lands, *[h[2] for h in handles], *[h[3] for h in handles], after)
    me = 4 * lax.axis_index("x") + 2 * lax.axis_index("y") + lax.axis_index("c")
    filled = []
    for src, land in zip(out[:n], out[n:]):
        own = src[None] if gather else lax.dynamic_slice_in_dim(src, me, 1, axis=0)
        filled.append(lax.dynamic_update_slice_in_dim(land, own, me, axis=0))
    return filled


def _allreduce_small(p):
    rows, d = p.shape

    def body(p_ref, o_ref, recv_ref, send_sems, recv_sems):
        me, peers = _peers()
        recv_ref[me] = p_ref[...]
        sends = []
        for k, peer, peer_flat in peers:
            cp = pltpu.make_async_remote_copy(
                src_ref=p_ref, dst_ref=recv_ref.at[me],
                send_sem=send_sems.at[k], recv_sem=recv_sems.at[k],
                device_id=peer, device_id_type=MESH)
            cp.start()
            sends.append(cp)
        for k, peer, peer_flat in peers:
            pltpu.make_async_remote_copy(
                src_ref=p_ref, dst_ref=recv_ref.at[peer_flat],
                send_sem=send_sems.at[k], recv_sem=recv_sems.at[k],
                device_id=peer, device_id_type=MESH).wait_recv()
        for cp in sends:
            cp.wait_send()
        acc = recv_ref[0]
        for s in range(1, N_DEV):
            acc = acc + recv_ref[s]
        is_loss = lax.broadcasted_iota(jnp.int32, (rows, d), 0) == rows - 1
        total = jnp.sum(jnp.where(is_loss, acc, 0.0))
        o_ref[...] = jnp.where(is_loss, total, acc)

    return pl.pallas_call(
        body, name="allreduce_small",
        out_shape=jax.ShapeDtypeStruct((rows, d), F32),
        in_specs=[pl.BlockSpec(memory_space=pltpu.VMEM)],
        out_specs=pl.BlockSpec(memory_space=pltpu.VMEM),
        scratch_shapes=[pltpu.VMEM((N_DEV, rows, d), F32),
                        pltpu.SemaphoreType.DMA((N_DEV,)), pltpu.SemaphoreType.DMA((N_DEV,))],
    )(p)


def _adam_math(w, g, m, v):
    m2 = ADAM_B1 * m + (1.0 - ADAM_B1) * g
    v2 = ADAM_B2 * v + (1.0 - ADAM_B2) * (g * g)
    m_hat = m2 / (1.0 - ADAM_B1 ** ADAM_STEP)
    v_hat = v2 / (1.0 - ADAM_B2 ** ADAM_STEP)
    delta = -ADAM_LR * (m_hat / (jnp.sqrt(v_hat) + ADAM_EPS) + ADAM_WD * w)
    return delta, m2, v2


def _adam_from_partials(parts, w, m, v, name):
    r, c = w.shape
    tr = _tile(r, 256, 16)

    def body(p_ref, w_ref, m_ref, v_ref, g_out, d_out, m_out, v_out):
        g = p_ref[0].astype(F32)
        for s in range(1, N_DEV):
            g = g + p_ref[s].astype(F32)
        delta, m2, v2 = _adam_math(w_ref[...], g, m_ref[...], v_ref[...])
        g_out[...] = g
        d_out[...] = delta
        m_out[...] = m2
        v_out[...] = v2

    blk = pl.BlockSpec((tr, c), lambda i: (i, 0))
    out = jax.ShapeDtypeStruct((r, c), F32)
    return pl.pallas_call(
        body, name=name, grid=(r // tr,),
        in_specs=[pl.BlockSpec((N_DEV, tr, c), lambda i: (0, i, 0)), blk, blk, blk],
        out_specs=[blk, blk, blk, blk], out_shape=[out, out, out, out],
        compiler_params=_params(),
    )(parts, w, m, v)


def _adam_small(g, w, m, v):
    def body(g_ref, w_ref, m_ref, v_ref, d_out, m_out, v_out):
        delta, m2, v2 = _adam_math(w_ref[...], g_ref[...], m_ref[...], v_ref[...])
        d_out[...] = delta
        m_out[...] = m2
        v_out[...] = v2

    out = jax.ShapeDtypeStruct(g.shape, F32)
    return pl.pallas_call(body, name="adam_small", out_shape=[out, out, out])(g, w, m, v)


def _rms_fwd(x, gain, name, dep=None):
    t, d = x.shape
    tr = _tile(t, 256, LANE)

    def body(x_ref, g_ref, o_ref, ot_ref):
        xv = x_ref[...]
        r = lax.rsqrt(jnp.mean(xv * xv, axis=-1, keepdims=True) + NORM_EPS)
        y = xv * r * g_ref[...]
        o_ref[...] = y.astype(BF)
        ot_ref[...] = jnp.transpose(y).astype(BF)

    return _call(
        body, [x, gain], dep=dep, name=name, grid=(t // tr,),
        in_specs=[pl.BlockSpec((tr, d), lambda i: (i, 0)), pl.BlockSpec((1, d), lambda i: (0, 0))],
        out_specs=[pl.BlockSpec((tr, d), lambda i: (i, 0)), pl.BlockSpec((d, tr), lambda i: (0, i))],
        out_shape=[jax.ShapeDtypeStruct((t, d), BF), jax.ShapeDtypeStruct((d, t), BF)],
        compiler_params=_params(),
    )


def _rms_vjp(xv, gain, dy):
    r = lax.rsqrt(jnp.mean(xv * xv, axis=-1, keepdims=True) + NORM_EPS)
    xhat = xv * r
    dxhat = dy * gain
    dx = r * (dxhat - xhat * jnp.mean(dxhat * xhat, axis=-1, keepdims=True))
    dgain = jnp.sum(dy * xhat, axis=0, keepdims=True)
    return dx, dgain


def _loss_head(x, gain, target):
    t, d = x.shape
    tr = _tile(t, 256, 16)

    def body(x_ref, g_ref, t_ref, dx_ref, dxb_ref, dg_ref, loss_ref):
        xv = x_ref[...]
        gain = g_ref[...]
        r = lax.rsqrt(jnp.mean(xv * xv, axis=-1, keepdims=True) + NORM_EPS)
        err = xv * r * gain - t_ref[...]
        dx, dgain = _rms_vjp(xv, gain, err * (1.0 / d))
        dx_ref[...] = dx
        dxb_ref[...] = dx.astype(BF)

        @pl.when(pl.program_id(0) == 0)
        def _():
            dg_ref[...] = jnp.zeros_like(dg_ref)
            loss_ref[...] = jnp.zeros_like(loss_ref)

        dg_ref[...] += dgain
        loss_ref[...] += jnp.sum(err * err, axis=0, keepdims=True) * (0.5 / d)

    row = pl.BlockSpec((tr, d), lambda i: (i, 0))
    vec = pl.BlockSpec((1, d), lambda i: (0, 0))
    return pl.pallas_call(
        body, name="loss_head", grid=(t // tr,),
        in_specs=[row, vec, row], out_specs=[row, row, vec, vec],
        out_shape=[jax.ShapeDtypeStruct((t, d), F32), jax.ShapeDtypeStruct((t, d), BF),
                   jax.ShapeDtypeStruct((1, d), F32), jax.ShapeDtypeStruct((1, d), F32)],
        compiler_params=_params(),
    )(x, gain, target)


def _mm_nn(a, b, out_dtype, name, residual=None, tm_pref=512, tn_pref=1152):
    m, k = a.shape
    n = b.shape[1]
    tm, tn = _tile(m, tm_pref, 16), _tile(n, tn_pref, LANE)

    def body(*refs):
        if residual is None:
            a_ref, b_ref, o_ref = refs
            o_ref[...] = _dot(a_ref[...], b_ref[...]).astype(out_dtype)
        else:
            a_ref, b_ref, r_ref, o_ref = refs
            o_ref[...] = (r_ref[...] + _dot(a_ref[...], b_ref[...])).astype(out_dtype)

    in_specs = [pl.BlockSpec((tm, k), lambda j, i: (i, 0)), pl.BlockSpec((k, tn), lambda j, i: (0, j))]
    args = [a, b]
    if residual is not None:
        in_specs.append(pl.BlockSpec((tm, tn), lambda j, i: (i, j)))
        args.append(residual)
    return pl.pallas_call(
        body, name=name, grid=(n // tn, m // tm), in_specs=in_specs,
        out_specs=pl.BlockSpec((tm, tn), lambda j, i: (i, j)),
        out_shape=jax.ShapeDtypeStruct((m, n), out_dtype), compiler_params=_params(),
    )(*args)


def _rms_bwd_tail(dy_ref, first, x_ref, g_ref, dres_ref, dx_ref, dxb_ref, dg_ref):
    @pl.when(first)
    def _():
        dg_ref[...] = jnp.zeros_like(dg_ref)

    gain = g_ref[...]
    for r in range(0, dy_ref.shape[0], LANE):
        rows = pl.ds(r, min(LANE, dy_ref.shape[0] - r))
        dx, dgain = _rms_vjp(x_ref[rows, :], gain, dy_ref[rows, :])
        dx = dx + dres_ref[rows, :]
        dx_ref[rows, :] = dx
        dxb_ref[rows, :] = dx.astype(BF)
        dg_ref[...] += dgain


def _mm_nt(a, b, out_dtype, name, tm_pref=512, tn_pref=1024, tk_pref=2048, rms=None, dep=None):
    m, k = a.shape
    n = b.shape[0]
    tm, tn, tk = _tile(m, tm_pref, 16), _tile(n, tn_pref, LANE), _tile(k, tk_pref, LANE)
    nk = k // tk
    assert rms is None or tn == n

    def body(*refs):
        if rms is None:
            a_ref, b_ref, o_ref, acc_ref = refs
        else:
            a_ref, b_ref, x_ref, g_ref, dres_ref, dx_ref, dxb_ref, dg_ref, acc_ref = refs
        kk = pl.program_id(2)

        @pl.when(kk == 0)
        def _():
            acc_ref[...] = jnp.zeros_like(acc_ref)

        acc_ref[...] += _dot(a_ref[...], b_ref[...], NT)

        @pl.when(kk == nk - 1)
        def _():
            if rms is None:
                o_ref[...] = acc_ref[...].astype(out_dtype)
            else:
                _rms_bwd_tail(acc_ref, pl.program_id(1) == 0, x_ref, g_ref, dres_ref, dx_ref, dxb_ref, dg_ref)

    in_specs = [pl.BlockSpec((tm, tk), lambda j, i, kk: (i, kk)), pl.BlockSpec((tn, tk), lambda j, i, kk: (j, kk))]
    row = pl.BlockSpec((tm, tn), lambda j, i, kk: (i, j))
    if rms is None:
        args, out_specs, out_shape = [a, b], row, jax.ShapeDtypeStruct((m, n), out_dtype)
    else:
        vec = pl.BlockSpec((1, n), lambda j, i, kk: (0, 0))
        args, in_specs = [a, b, *rms], in_specs + [row, vec, row]
        out_specs = [row, row, vec]
        out_shape = [jax.ShapeDtypeStruct((m, n), F32), jax.ShapeDtypeStruct((m, n), BF),
                     jax.ShapeDtypeStruct((1, n), F32)]
    return _call(
        body, args, dep=dep, name=name, grid=(n // tn, m // tm, nk), in_specs=in_specs, out_specs=out_specs,
        out_shape=out_shape, scratch_shapes=[pltpu.VMEM((tm, tn), F32)], compiler_params=_params(),
    )


def _mm_tn(a, b, out_dtype, name, tn_pref=1152, tk_pref=512, a_transposed=False):
    (k, t) = a.shape if a_transposed else a.shape[::-1]
    n = b.shape[1]
    tn, tk = _tile(n, tn_pref, LANE), _tile(t, tk_pref, LANE if a_transposed else 16)
    nt = t // tk

    def body(a_ref, b_ref, o_ref, acc_ref):
        tt = pl.program_id(1)

        @pl.when(tt == 0)
        def _():
            acc_ref[...] = jnp.zeros_like(acc_ref)

        acc_ref[...] += _dot(a_ref[...], b_ref[...], NN if a_transposed else TN)

        @pl.when(tt == nt - 1)
        def _():
            o_ref[...] = acc_ref[...].astype(out_dtype)

    if a_transposed:
        a_spec = pl.BlockSpec((k, tk), lambda j, tt: (0, tt))
    else:
        a_spec = pl.BlockSpec((tk, k), lambda j, tt: (tt, 0))
    return pl.pallas_call(
        body, name=name, grid=(n // tn, nt),
        in_specs=[a_spec, pl.BlockSpec((tk, tn), lambda j, tt: (tt, j))],
        out_specs=pl.BlockSpec((k, tn), lambda j, tt: (0, j)),
        out_shape=jax.ShapeDtypeStruct((k, n), out_dtype),
        scratch_shapes=[pltpu.VMEM((k, tn), F32)], compiler_params=_params(),
    )(a, b)


def _ffn_gate_up(hn, wg, wu, name):
    t, d = hn.shape
    ns, _, f = wg.shape
    tm = _tile(t, FFN_ROWS, 16)

    def body(h_ref, wg_ref, wu_ref, g_ref, u_ref, a_ref):
        h = h_ref[...]
        g = _dot(h, wg_ref[...])
        u = _dot(h, wu_ref[...])
        g_ref[...] = g.astype(BF)
        u_ref[...] = u.astype(BF)
        a_ref[...] = (g * _sig(g) * u).astype(BF)

    wspec = pl.BlockSpec((None, d, f), lambda j, i: (j, 0, 0))
    hid = pl.BlockSpec((None, tm, f), lambda j, i: (j, i, 0))
    out = jax.ShapeDtypeStruct((ns, t, f), BF)
    return pl.pallas_call(
        body, name=name, grid=(ns, t // tm),
        in_specs=[pl.BlockSpec((tm, d), lambda j, i: (i, 0)), wspec, wspec],
        out_specs=[hid, hid, hid], out_shape=[out, out, out], compiler_params=_params(),
    )(hn, wg, wu)


def _ffn_gate(hn, wg, name):
    t, d = hn.shape
    ns, _, f = wg.shape
    tm = _tile(t, FFN_ROWS, 16)

    def body(h_ref, wg_ref, g_ref):
        g_ref[...] = _dot(h_ref[...], wg_ref[...]).astype(BF)

    return pl.pallas_call(
        body, name=name, grid=(ns, t // tm),
        in_specs=[pl.BlockSpec((tm, d), lambda j, i: (i, 0)), pl.BlockSpec((None, d, f), lambda j, i: (j, 0, 0))],
        out_specs=pl.BlockSpec((None, tm, f), lambda j, i: (j, i, 0)),
        out_shape=jax.ShapeDtypeStruct((ns, t, f), BF), compiler_params=_params(),
    )(hn, wg)


def _ffn_up_act(hn, wu, g, name):
    t, d = hn.shape
    ns, _, f = wu.shape
    tm = _tile(t, FFN_ROWS, 16)

    def body(h_ref, wu_ref, g_ref, u_ref, a_ref):
        u = _dot(h_ref[...], wu_ref[...])
        gv = g_ref[...].astype(F32)
        u_ref[...] = u.astype(BF)
        a_ref[...] = (gv * _sig(gv) * u).astype(BF)

    hid = pl.BlockSpec((None, tm, f), lambda j, i: (j, i, 0))
    out = jax.ShapeDtypeStruct((ns, t, f), BF)
    return pl.pallas_call(
        body, name=name, grid=(ns, t // tm),
        in_specs=[pl.BlockSpec((tm, d), lambda j, i: (i, 0)), pl.BlockSpec((None, d, f), lambda j, i: (j, 0, 0)), hid],
        out_specs=[hid, hid], out_shape=[out, out], compiler_params=_params(),
    )(hn, wu, g)


def _ffn_down(act, wd, xres, name):
    ns, t, f = act.shape
    d = wd.shape[2]
    tm = _tile(t, FFN_ROWS, 16)

    def body(a_ref, w_ref, x_ref, o_ref):
        @pl.when(pl.program_id(1) == 0)
        def _():
            o_ref[...] = x_ref[...]

        o_ref[...] += 0.5 * _dot(a_ref[...], w_ref[...])

    row = pl.BlockSpec((tm, d), lambda i, j: (i, 0))
    return pl.pallas_call(
        body, name=name, grid=(t // tm, ns),
        in_specs=[pl.BlockSpec((None, tm, f), lambda i, j: (j, i, 0)),
                  pl.BlockSpec((None, f, d), lambda i, j: (j, 0, 0)), row],
        out_specs=row, out_shape=jax.ShapeDtypeStruct((t, d), F32), compiler_params=_params(),
    )(act, wd, xres)


def _ffn_bwd_hidden(dxb, wd, g, u, name):
    t, d = dxb.shape
    ns, f, _ = wd.shape
    tm = _tile(t, FFN_ROWS, 16)

    def body(dx_ref, w_ref, g_ref, u_ref, dg_ref, du_ref):
        dh = 0.5 * _dot(dx_ref[...], w_ref[...], NT)
        gv = g_ref[...].astype(F32)
        uv = u_ref[...].astype(F32)
        s = _sig(gv)
        dg_ref[...] = (dh * uv * (s * (1.0 + gv * (1.0 - s)))).astype(BF)
        du_ref[...] = (dh * (gv * s)).astype(BF)

    hid = pl.BlockSpec((None, tm, f), lambda j, i: (j, i, 0))
    out = jax.ShapeDtypeStruct((ns, t, f), BF)
    return pl.pallas_call(
        body, name=name, grid=(ns, t // tm),
        in_specs=[pl.BlockSpec((tm, d), lambda j, i: (i, 0)),
                  pl.BlockSpec((None, f, d), lambda j, i: (j, 0, 0)), hid, hid],
        out_specs=[hid, hid], out_shape=[out, out], compiler_params=_params(),
    )(dxb, wd, g, u)


def _ffn_dw_down(act, dxb, name):
    ns, t, f = act.shape
    d = dxb.shape[1]
    tk = _tile(t, DW_ROWS, 16)
    nt = t // tk

    def body(a_ref, dx_ref, o_ref, acc_ref):
        tt = pl.program_id(1)

        @pl.when(tt == 0)
        def _():
            acc_ref[...] = jnp.zeros_like(acc_ref)

        acc_ref[...] += _dot(a_ref[...], dx_ref[...], TN)

        @pl.when(tt == nt - 1)
        def _():
            o_ref[...] = (0.5 * acc_ref[...]).astype(BF)

    return pl.pallas_call(
        body, name=name, grid=(ns, nt),
        in_specs=[pl.BlockSpec((None, tk, f), lambda j, tt: (j, tt, 0)),
                  pl.BlockSpec((tk, d), lambda j, tt: (tt, 0))],
        out_specs=pl.BlockSpec((None, f, d), lambda j, tt: (j, 0, 0)),
        out_shape=jax.ShapeDtypeStruct((ns, f, d), BF),
        scratch_shapes=[pltpu.VMEM((f, d), F32)], compiler_params=_params(),
    )(act, dxb)


def _ffn_dw_gate_up(hn_t, dg, du, name, dep=None):
    d, t = hn_t.shape
    ns, _, f = dg.shape
    tk = _tile(t, DW_ROWS, LANE)
    nt = t // tk

    def body(h_ref, dg_ref, du_ref, og_ref, ou_ref, accg_ref, accu_ref):
        tt = pl.program_id(1)

        @pl.when(tt == 0)
        def _():
            accg_ref[...] = jnp.zeros_like(accg_ref)
            accu_ref[...] = jnp.zeros_like(accu_ref)

        h = h_ref[...]
        accg_ref[...] += _dot(h, dg_ref[...])
        accu_ref[...] += _dot(h, du_ref[...])

        @pl.when(tt == nt - 1)
        def _():
            og_ref[...] = accg_ref[...].astype(BF)
            ou_ref[...] = accu_ref[...].astype(BF)

    hid = pl.BlockSpec((None, tk, f), lambda j, tt: (j, tt, 0))
    wspec = pl.BlockSpec((None, d, f), lambda j, tt: (j, 0, 0))
    out = jax.ShapeDtypeStruct((ns, d, f), BF)
    return _call(
        body, [hn_t, dg, du], dep=dep, name=name, grid=(ns, nt),
        in_specs=[pl.BlockSpec((d, tk), lambda j, tt: (0, tt)), hid, hid],
        out_specs=[wspec, wspec], out_shape=[out, out],
        scratch_shapes=[pltpu.VMEM((d, f), F32), pltpu.VMEM((d, f), F32)], compiler_params=_params(),
    )


def _rms_bwd(dy, x, gain, dres, name):
    t, d = x.shape
    tr = _tile(t, 256, 16)

    def body(dy_ref, x_ref, g_ref, dres_ref, dx_ref, dxb_ref, dg_ref):
        _rms_bwd_tail(dy_ref, pl.program_id(0) == 0, x_ref, g_ref, dres_ref, dx_ref, dxb_ref, dg_ref)

    row = pl.BlockSpec((tr, d), lambda i: (i, 0))
    vec = pl.BlockSpec((1, d), lambda i: (0, 0))
    return pl.pallas_call(
        body, name=name, grid=(t // tr,),
        in_specs=[row, row, vec, row], out_specs=[row, row, vec],
        out_shape=[jax.ShapeDtypeStruct((t, d), F32), jax.ShapeDtypeStruct((t, d), BF),
                   jax.ShapeDtypeStruct((1, d), F32)],
        compiler_params=_params(),
    )(dy, x, gain, dres)


def _ffn_bwd_input(dg, du, wg, wu, name, dep=None):
    ns, t, f = dg.shape
    d = wg.shape[1]
    tm = _tile(t, FFN_ROWS, 16)

    def body(dg_ref, du_ref, wg_ref, wu_ref, o_ref):
        @pl.when(pl.program_id(1) == 0)
        def _():
            o_ref[...] = jnp.zeros_like(o_ref)

        o_ref[...] += _dot(dg_ref[...], wg_ref[...], NT) + _dot(du_ref[...], wu_ref[...], NT)

    hid = pl.BlockSpec((None, tm, f), lambda i, j: (j, i, 0))
    wspec = pl.BlockSpec((None, d, f), lambda i, j: (j, 0, 0))
    return _call(
        body, [dg, du, wg, wu], dep=dep, name=name, grid=(t // tm, ns),
        in_specs=[hid, hid, wspec, wspec],
        out_specs=pl.BlockSpec((tm, d), lambda i, j: (i, 0)),
        out_shape=jax.ShapeDtypeStruct((t, d), F32), compiler_params=_params(),
    )


def _rope_tables(t):
    pos = jnp.arange(t, dtype=F32)
    inv_freq = ROPE_THETA ** (-jnp.arange(0, ROPE_DIM, 2, dtype=F32) / ROPE_DIM)
    ang = pos[:, None] * inv_freq[None, :]
    cos, sin = jnp.cos(ang), jnp.sin(ang)
    rest = HEAD_DIM - ROPE_DIM
    one = jnp.ones((t, rest), F32)
    zero_h = jnp.zeros((t, ROPE_HALF), F32)
    zero_r = jnp.zeros((t, rest), F32)
    c = jnp.concatenate([cos, cos, one], axis=1)
    s1 = jnp.concatenate([-sin, zero_h, zero_r], axis=1)
    s2 = jnp.concatenate([zero_h, sin, zero_r], axis=1)
    return c, s1, s2


def _rope(xh, c, s1, s2):
    return xh * c + pltpu.roll(xh, HEAD_DIM - ROPE_HALF, 1) * s1 + pltpu.roll(xh, ROPE_HALF, 1) * s2


def _rope_t(dh, c, s1, s2):
    return dh * c + pltpu.roll(dh * s1, ROPE_HALF, 1) + pltpu.roll(dh * s2, HEAD_DIM - ROPE_HALF, 1)


def _mixer_prep(proj, tables, bf_pad, hd, scale):
    t, np_ = proj.shape
    tr = _tile(t, 256, 16)
    nh = hd // HEAD_DIM
    nblk = hd // LANE
    f_blk = np_ // LANE - 1

    def body(qd_ref, kd_ref, vd_ref, qf_ref, kf_ref, vf_ref, fl_ref, c_ref, s1_ref, s2_ref, b_ref,
             oqd, okd, ovd, oqf, okf, ovf, olog):
        c, s1, s2 = c_ref[...], s1_ref[...], s2_ref[...]
        for h in range(nh):
            sl = slice(h * HEAD_DIM, (h + 1) * HEAD_DIM)
            oqd[:, sl] = (_rope(qd_ref[:, sl], c, s1, s2) * scale).astype(BF)
            okd[:, sl] = _rope(kd_ref[:, sl], c, s1, s2).astype(BF)
        ovd[...] = vd_ref[...].astype(BF)
        oqf[...] = (qf_ref[...] * scale).astype(BF)
        okf[...] = kf_ref[...].astype(BF)
        ovf[...] = vf_ref[...].astype(BF)
        z = fl_ref[...] + b_ref[...]
        olog[...] = jnp.minimum(z, 0.0) - jnp.log(1.0 + jnp.exp(-jnp.abs(z)))

    def col(kblk):
        return pl.BlockSpec((tr, hd), lambda i, kblk=kblk: (i, kblk))

    lane_row = pl.BlockSpec((tr, LANE), lambda i: (i, 0))
    in_specs = [col(0), col(1), col(2), col(3), col(4), col(5),
                pl.BlockSpec((tr, LANE), lambda i: (i, f_blk)),
                lane_row, lane_row, lane_row, pl.BlockSpec((1, LANE), lambda i: (0, 0))]
    o = pl.BlockSpec((tr, hd), lambda i: (i, 0))
    ob = jax.ShapeDtypeStruct((t, hd), BF)
    del nblk
    return pl.pallas_call(
        body, name="mixer_prep", grid=(t // tr,), in_specs=in_specs,
        out_specs=[o, o, o, o, o, o, lane_row],
        out_shape=[ob, ob, ob, ob, ob, ob, jax.ShapeDtypeStruct((t, LANE), F32)],
        compiler_params=_params(),
    )(proj, proj, proj, proj, proj, proj, proj, *tables, bf_pad)


def _split3(x):
    x1 = x.astype(BF)
    r1 = x - x1.astype(F32)
    x2 = r1.astype(BF)
    x3 = (r1 - x2.astype(F32)).astype(BF)
    return x1, x2, x3


def _cumsum_rows(x, reverse, name):
    t, w = x.shape
    blk = LANE
    nb = t // blk

    def body(x_ref, o_ref):
        r = lax.broadcasted_iota(jnp.int32, (blk, blk), 0)
        c = lax.broadcasted_iota(jnp.int32, (blk, blk), 1)
        tri = jnp.where((c >= r) if reverse else (c <= r), 1.0, 0.0).astype(BF)

        def step(i, carry):
            b = (nb - 1 - i) if reverse else i
            off = pl.multiple_of(b * blk, blk)
            xb = x_ref[pl.ds(off, blk), :]
            x1, x2, x3 = _split3(xb)
            o_ref[pl.ds(off, blk), :] = _dot(tri, x1) + _dot(tri, x2) + _dot(tri, x3) + carry
            return carry + jnp.sum(xb, axis=0, keepdims=True)

        lax.fori_loop(0, nb, step, jnp.zeros((1, w), F32))

    return pl.pallas_call(body, name=name, out_shape=jax.ShapeDtypeStruct((t, w), F32),
                          compiler_params=_params())(x)


ATTN_ROWS = 16


def _dil_bias_tiles(tq):
    nbias = MAX_WINDOW // tq + 1
    b = lax.broadcasted_iota(jnp.int32, (nbias, tq, tq), 0)
    i = lax.broadcasted_iota(jnp.int32, (nbias, tq, tq), 1)
    j = lax.broadcasted_iota(jnp.int32, (nbias, tq, tq), 2)
    delta = b * tq + i - j
    mult = jnp.zeros((nbias, tq, tq), F32)
    for w, dil in DIL_PATTERNS:
        mult = mult + jnp.where((delta >= 0) & (delta <= w) & (delta % dil == 0), 1.0, 0.0)
    return jnp.where(mult > 0.0, jnp.log(jnp.maximum(mult, 1.0)), NEG)


def _rep(x, width):
    return jnp.tile(x, (1, width // LANE))


def _chunks(n_rows, fn):
    for c in range(n_rows // ATTN_ROWS):
        fn(c * ATTN_ROWS)


def _causal(r0, tq, transposed):
    a = lax.broadcasted_iota(jnp.int32, (ATTN_ROWS, tq), 0) + r0
    b = lax.broadcasted_iota(jnp.int32, (ATTN_ROWS, tq), 1)
    return (a <= b) if transposed else (b <= a)


def _rows8(x):
    return jnp.transpose(x)[:8, :]


def _attn_fwd(mode, q, k, v, bias, tq, name):
    t, hd = q.shape
    nh = hd // HEAD_DIM
    nb = t // tq
    wb = MAX_WINDOW // tq
    fox = mode == "fox"

    def body(q_ref, k_ref, v_ref, b_ref, o_ref, lse_ref, lse_row_ref, s_ref, p_ref, m_ref, l_ref, acc_ref):
        qi = pl.program_id(1)
        qb = q_ref[...]
        m_ref[...] = jnp.full_like(m_ref, NEG)
        l_ref[...] = jnp.zeros_like(l_ref)
        acc_ref[...] = jnp.zeros_like(acc_ref)

        def tile(kj, diag):
            off = pl.multiple_of(kj * tq, tq)
            s_ref[...] = _dot(qb, k_ref[pl.ds(off, tq), :], NT)
            if fox:
                brow = b_ref[qi][:, :1] - b_ref[kj]

            def chunk(r0):
                rows = pl.ds(r0, ATTN_ROWS)
                if fox:
                    s = s_ref[rows, :] + brow
                    if diag:
                        s = jnp.where(_causal(r0, tq, False), s, NEG)
                else:
                    s = s_ref[rows, :] + b_ref[qi - kj, rows, :]
                m_old = m_ref[rows, :]
                m_new = jnp.maximum(m_old, jnp.max(s, axis=1, keepdims=True))
                p = jnp.exp(s - _rep(m_new, tq))
                alpha = jnp.exp(m_old - m_new)
                l_ref[rows, :] = alpha * l_ref[rows, :] + jnp.sum(p, axis=1, keepdims=True)
                m_ref[rows, :] = m_new
                acc_ref[rows, :] = alpha * acc_ref[rows, :]
                p_ref[rows, :] = p.astype(BF)

            _chunks(tq, chunk)
            acc_ref[...] += _dot(p_ref[...], v_ref[pl.ds(off, tq), :])

        tile(qi, True)
        if fox:
            lax.fori_loop(0, qi, lambda kj, c: (tile(kj, False), c)[1], 0)
        else:
            lax.fori_loop(1, jnp.minimum(qi, wb) + 1, lambda i, c: (tile(qi - i, False), c)[1], 0)
        o_ref[...] = (acc_ref[...] / l_ref[...]).astype(BF)
        lse = m_ref[...] + jnp.log(l_ref[...])
        lse_ref[...] = lse
        lse_row_ref[...] = _rows8(lse)

    qspec = pl.BlockSpec((tq, HEAD_DIM), lambda h, i: (i, h))
    kvspec = pl.BlockSpec((t, HEAD_DIM), lambda h, i: (0, h))
    repspec = pl.BlockSpec((None, tq, LANE), lambda h, i: (h, i, 0))
    row8spec = pl.BlockSpec((None, None, 8, tq), lambda h, i: (h, i, 0, 0))
    if fox:
        bspec = pl.BlockSpec((None, nb, 1, tq), lambda h, i: (h, 0, 0, 0))
    else:
        bspec = pl.BlockSpec((wb + 1, tq, tq), lambda h, i: (0, 0, 0))
    return pl.pallas_call(
        body, name=name, grid=(nh, nb), in_specs=[qspec, kvspec, kvspec, bspec],
        out_specs=[qspec, repspec, row8spec],
        out_shape=[jax.ShapeDtypeStruct((t, hd), BF), jax.ShapeDtypeStruct((nh, t, LANE), F32),
                   jax.ShapeDtypeStruct((nh, nb, 8, tq), F32)],
        scratch_shapes=[pltpu.VMEM((tq, tq), F32), pltpu.VMEM((tq, tq), BF), pltpu.VMEM((tq, LANE), F32),
                        pltpu.VMEM((tq, LANE), F32), pltpu.VMEM((tq, HEAD_DIM), F32)],
        compiler_params=_params(),
    )(q, k, v, bias)


def _attn_bwd_dq(mode, q, k, v, o, do, lse, bias, tq, name, dep=None):
    t, hd = q.shape
    nh = hd // HEAD_DIM
    nb = t // tq
    wb = MAX_WINDOW // tq
    fox = mode == "fox"

    def body(q_ref, k_ref, v_ref, o_ref, do_ref, lse_ref, b_ref, dq_ref, dl_row_ref,
             s_ref, dp_ref, x_ref, y_ref, acc_ref, acc2_ref, dl_ref):
        qi = pl.program_id(1)
        qb = q_ref[...]
        dob = do_ref[...]
        acc_ref[...] = jnp.zeros_like(acc_ref)
        if fox:
            acc2_ref[...] = jnp.zeros_like(acc2_ref)
            dl_ref[...] = jnp.zeros_like(dl_ref)
        else:
            prod = o_ref[...].astype(F32) * dob.astype(F32)
            dl_ref[...] = jnp.broadcast_to(jnp.sum(prod, axis=1, keepdims=True), (tq, LANE))

        def tile(kj, diag):
            off = pl.multiple_of(kj * tq, tq)
            kb = k_ref[pl.ds(off, tq), :]
            s_ref[...] = _dot(qb, kb, NT)
            dp_ref[...] = _dot(dob, v_ref[pl.ds(off, tq), :], NT)
            if fox:
                brow = b_ref[qi][:, :1] - b_ref[kj]

            def chunk(r0):
                rows = pl.ds(r0, ATTN_ROWS)
                lse_c = _rep(lse_ref[rows, :], tq)
                if fox:
                    s = s_ref[rows, :] + brow
                    if diag:
                        s = jnp.where(_causal(r0, tq, False), s, NEG)
                    p = jnp.exp(s - lse_c)
                    pdp = p * dp_ref[rows, :]
                    dl_ref[rows, :] += jnp.sum(pdp, axis=1, keepdims=True)
                    x_ref[rows, :] = pdp.astype(BF)
                    y_ref[rows, :] = p.astype(BF)
                else:
                    p = jnp.exp(s_ref[rows, :] + b_ref[qi - kj, rows, :] - lse_c)
                    x_ref[rows, :] = (p * (dp_ref[rows, :] - _rep(dl_ref[rows, :], tq))).astype(BF)

            _chunks(tq, chunk)
            acc_ref[...] += _dot(x_ref[...], kb)
            if fox:
                acc2_ref[...] += _dot(y_ref[...], kb)

        tile(qi, True)
        if fox:
            lax.fori_loop(0, qi, lambda kj, c: (tile(kj, False), c)[1], 0)
            dq_ref[...] = acc_ref[...] - dl_ref[...] * acc2_ref[...]
        else:
            lax.fori_loop(1, jnp.minimum(qi, wb) + 1, lambda i, c: (tile(qi - i, False), c)[1], 0)
            dq_ref[...] = acc_ref[...]
        dl_row_ref[...] = _rows8(dl_ref[...])

    qspec = pl.BlockSpec((tq, HEAD_DIM), lambda h, i: (i, h))
    kvspec = pl.BlockSpec((t, HEAD_DIM), lambda h, i: (0, h))
    repspec = pl.BlockSpec((None, tq, LANE), lambda h, i: (h, i, 0))
    row8spec = pl.BlockSpec((None, None, 8, tq), lambda h, i: (h, i, 0, 0))
    if fox:
        bspec = pl.BlockSpec((None, nb, 1, tq), lambda h, i: (h, 0, 0, 0))
    else:
        bspec = pl.BlockSpec((wb + 1, tq, tq), lambda h, i: (0, 0, 0))
    return _call(
        body, [q, k, v, o, do, lse, bias], dep=dep, name=name, grid=(nh, nb),
        in_specs=[qspec, kvspec, kvspec, qspec, qspec, repspec, bspec],
        out_specs=[qspec, row8spec],
        out_shape=[jax.ShapeDtypeStruct((t, hd), F32), jax.ShapeDtypeStruct((nh, nb, 8, tq), F32)],
        scratch_shapes=[pltpu.VMEM((tq, tq), F32), pltpu.VMEM((tq, tq), F32), pltpu.VMEM((tq, tq), BF),
                        pltpu.VMEM((tq, tq), BF), pltpu.VMEM((tq, HEAD_DIM), F32),
                        pltpu.VMEM((tq, HEAD_DIM), F32), pltpu.VMEM((tq, LANE), F32)],
        compiler_params=_params(),
    )


def _attn_bwd_dkv(mode, q, k, v, do, lse_row, dl_row, bias_t, c_row, tq, name):
    t, hd = q.shape
    nh = hd // HEAD_DIM
    nb = t // tq
    wb = MAX_WINDOW // tq
    fox = mode == "fox"

    def body(*refs):
        if fox:
            (q_ref, k_ref, v_ref, do_ref, lse_ref, dl_ref, b_ref, cq_ref, dk_ref, dv_ref, dc_row_ref,
             s_ref, dp_ref, x_ref, y_ref, dc_ref) = refs
        else:
            q_ref, k_ref, v_ref, do_ref, lse_ref, dl_ref, b_ref, dk_ref, dv_ref, s_ref, dp_ref, x_ref, y_ref = refs
        kj = pl.program_id(1)
        kb = k_ref[...]
        vb = v_ref[...]
        dk_ref[...] = jnp.zeros_like(dk_ref)
        dv_ref[...] = jnp.zeros_like(dv_ref)
        if fox:
            dc_ref[...] = jnp.zeros_like(dc_ref)

        def tile(qi, diag):
            off = pl.multiple_of(qi * tq, tq)
            qb = q_ref[pl.ds(off, tq), :]
            dob = do_ref[pl.ds(off, tq), :]
            s_ref[...] = _dot(kb, qb, NT)
            dp_ref[...] = _dot(vb, dob, NT)
            lse_r = lse_ref[qi, 0:1, :]
            dl_r = dl_ref[qi, 0:1, :]
            if fox:
                kbias = cq_ref[qi][:, :1] - b_ref[...]

            def chunk(r0):
                rows = pl.ds(r0, ATTN_ROWS)
                if fox:
                    s = s_ref[rows, :] + _rep(kbias[r0:r0 + ATTN_ROWS, :], tq)
                    if diag:
                        s = jnp.where(_causal(r0, tq, True), s, NEG)
                else:
                    s = s_ref[rows, :] + b_ref[qi - kj, rows, :]
                pt = jnp.exp(s - lse_r)
                dst = pt * (dp_ref[rows, :] - dl_r)
                x_ref[rows, :] = pt.astype(BF)
                y_ref[rows, :] = dst.astype(BF)
                if fox:
                    dc_ref[rows, :] -= jnp.sum(dst, axis=1, keepdims=True)

            _chunks(tq, chunk)
            dv_ref[...] += _dot(x_ref[...], dob)
            dk_ref[...] += _dot(y_ref[...], qb)

        tile(kj, True)
        hi = nb if fox else jnp.minimum(kj + wb + 1, nb)
        lax.fori_loop(kj + 1, hi, lambda qi, c: (tile(qi, False), c)[1], 0)
        if fox:
            dc_row_ref[...] = _rows8(dc_ref[...])

    blkspec = pl.BlockSpec((tq, HEAD_DIM), lambda h, j: (j, h))
    fullspec = pl.BlockSpec((t, HEAD_DIM), lambda h, j: (0, h))
    rows8spec = pl.BlockSpec((None, nb, 8, tq), lambda h, j: (h, 0, 0, 0))
    repspec = pl.BlockSpec((None, tq, LANE), lambda h, j: (h, j, 0))
    in_specs = [fullspec, blkspec, blkspec, fullspec, rows8spec, rows8spec]
    args = [q, k, v, do, lse_row, dl_row, bias_t]
    out_specs = [blkspec, blkspec]
    out_shape = [jax.ShapeDtypeStruct((t, hd), F32), jax.ShapeDtypeStruct((t, hd), F32)]
    scratch = [pltpu.VMEM((tq, tq), F32), pltpu.VMEM((tq, tq), F32), pltpu.VMEM((tq, tq), BF),
               pltpu.VMEM((tq, tq), BF)]
    if fox:
        in_specs += [repspec, pl.BlockSpec((None, nb, 1, tq), lambda h, j: (h, 0, 0, 0))]
        args.append(c_row)
        out_specs.append(pl.BlockSpec((None, None, 8, tq), lambda h, j: (h, j, 0, 0)))
        out_shape.append(jax.ShapeDtypeStruct((nh, nb, 8, tq), F32))
        scratch.append(pltpu.VMEM((tq, LANE), F32))
    else:
        in_specs.append(pl.BlockSpec((wb + 1, tq, tq), lambda h, j: (0, 0, 0)))
    return pl.pallas_call(
        body, name=name, grid=(nh, nb), in_specs=in_specs, out_specs=out_specs, out_shape=out_shape,
        scratch_shapes=scratch, compiler_params=_params(),
    )(*args)


def _gate_specs(t, d, hd, tr):
    row = pl.BlockSpec((tr, d), lambda i: (i, 0))
    vec = pl.BlockSpec((1, d), lambda i: (0, 0))
    base = 6 * hd // d
    gd = pl.BlockSpec((tr, d), lambda i: (i, base))
    gf = pl.BlockSpec((tr, d), lambda i: (i, base + 1))
    return row, vec, gd, gf


def _proj_merge(yd, yf, wpd, wpf, proj, b_d, b_f, hd):
    t = yd.shape[0]
    d = wpd.shape[1]
    tr = _tile(t, 256, 16)
    row, vec, gd, gf = _gate_specs(t, d, hd, tr)

    def body(yd_ref, yf_ref, wd_ref, wf_ref, gd_ref, gf_ref, bd_ref, bf_ref, pd_ref, pf_ref, o_ref):
        pd = _dot(yd_ref[...], wd_ref[...])
        pf = _dot(yf_ref[...], wf_ref[...])
        pd_ref[...] = pd
        pf_ref[...] = pf
        o_ref[...] = (_sig(gd_ref[...] + bd_ref[...]) * pd + _sig(gf_ref[...] + bf_ref[...]) * pf).astype(BF)

    yspec = pl.BlockSpec((tr, hd), lambda i: (i, 0))
    wspec = pl.BlockSpec((hd, d), lambda i: (0, 0))
    f32 = jax.ShapeDtypeStruct((t, d), F32)
    return pl.pallas_call(
        body, name="proj_merge", grid=(t // tr,), in_specs=[yspec, yspec, wspec, wspec, gd, gf, vec, vec],
        out_specs=[row, row, row], out_shape=[f32, f32, jax.ShapeDtypeStruct((t, d), BF)],
        compiler_params=_params(),
    )(yd, yf, wpd, wpf, proj, proj, b_d, b_f)


def _merge_bwd(dm, pd, pf, proj, b_d, b_f, hd):
    t, d = pd.shape
    tr = _tile(t, 256, 16)
    row, vec, gd, gf = _gate_specs(t, d, hd, tr)

    def body(dm_ref, pd_ref, pf_ref, gd_ref, gf_ref, bd_ref, bf_ref,
             dpd_ref, dpf_ref, dgd_ref, dgf_ref, dbd_ref, dbf_ref):
        dmv = dm_ref[...]
        sd = _sig(gd_ref[...] + bd_ref[...])
        sf = _sig(gf_ref[...] + bf_ref[...])
        dgd = dmv * pd_ref[...] * (sd * (1.0 - sd))
        dgf = dmv * pf_ref[...] * (sf * (1.0 - sf))
        dpd_ref[...] = (dmv * sd).astype(BF)
        dpf_ref[...] = (dmv * sf).astype(BF)
        dgd_ref[...] = dgd.astype(BF)
        dgf_ref[...] = dgf.astype(BF)

        @pl.when(pl.program_id(0) == 0)
        def _():
            dbd_ref[...] = jnp.zeros_like(dbd_ref)
            dbf_ref[...] = jnp.zeros_like(dbf_ref)

        dbd_ref[...] += jnp.sum(dgd, axis=0, keepdims=True)
        dbf_ref[...] += jnp.sum(dgf, axis=0, keepdims=True)

    ob = jax.ShapeDtypeStruct((t, d), BF)
    ov = jax.ShapeDtypeStruct((1, d), F32)
    return pl.pallas_call(
        body, name="merge_bwd", grid=(t // tr,), in_specs=[row, row, row, gd, gf, vec, vec],
        out_specs=[row, row, row, row, vec, vec], out_shape=[ob, ob, ob, ob, ov, ov],
        compiler_params=_params(),
    )(dm, pd, pf, proj, proj, b_d, b_f)


def _assemble_dproj(dqd, dkd, dvd, dqf, dkf, dvf, dgd, dgf, dlogf, proj, tables, bf_pad, scale):
    t, np_ = proj.shape
    hd = dqd.shape[1]
    d = dgd.shape[1]
    nh = hd // HEAD_DIM
    tr = _tile(t, 256, 16)
    f_blk = np_ // LANE - 1

    def body(dqd_ref, dkd_ref, dvd_ref, dqf_ref, dkf_ref, dvf_ref, dgd_ref, dgf_ref, dlog_ref, fl_ref,
             c_ref, s1_ref, s2_ref, b_ref, o_ref, db_ref):
        c, s1, s2 = c_ref[...], s1_ref[...], s2_ref[...]
        for h in range(nh):
            sl = slice(h * HEAD_DIM, (h + 1) * HEAD_DIM)
            o_ref[:, sl] = (_rope_t(dqd_ref[:, sl], c, s1, s2) * scale).astype(BF)
            o_ref[:, hd + h * HEAD_DIM:hd + (h + 1) * HEAD_DIM] = _rope_t(dkd_ref[:, sl], c, s1, s2).astype(BF)
        o_ref[:, 2 * hd:3 * hd] = dvd_ref[...].astype(BF)
        o_ref[:, 3 * hd:4 * hd] = (dqf_ref[...] * scale).astype(BF)
        o_ref[:, 4 * hd:5 * hd] = dkf_ref[...].astype(BF)
        o_ref[:, 5 * hd:6 * hd] = dvf_ref[...].astype(BF)
        o_ref[:, 6 * hd:6 * hd + d] = dgd_ref[...]
        o_ref[:, 6 * hd + d:6 * hd + 2 * d] = dgf_ref[...]
        z = fl_ref[...] + b_ref[...]
        dfl = dlog_ref[...] * _sig(-z)
        o_ref[:, 6 * hd + 2 * d:] = dfl.astype(BF)

        @pl.when(pl.program_id(0) == 0)
        def _():
            db_ref[...] = jnp.zeros_like(db_ref)

        db_ref[...] += jnp.sum(dfl, axis=0, keepdims=True)

    head = pl.BlockSpec((tr, hd), lambda i: (i, 0))
    row = pl.BlockSpec((tr, d), lambda i: (i, 0))
    lane_row = pl.BlockSpec((tr, LANE), lambda i: (i, 0))
    lane_vec = pl.BlockSpec((1, LANE), lambda i: (0, 0))
    return pl.pallas_call(
        body, name="assemble_dproj", grid=(t // tr,),
        in_specs=[head] * 6 + [row, row, lane_row, pl.BlockSpec((tr, LANE), lambda i: (i, f_blk)),
                               lane_row, lane_row, lane_row, lane_vec],
        out_specs=[pl.BlockSpec((tr, np_), lambda i: (i, 0)), lane_vec],
        out_shape=[jax.ShapeDtypeStruct((t, np_), BF), jax.ShapeDtypeStruct((1, LANE), F32)],
        compiler_params=_params(),
    )(dqd, dkd, dvd, dqf, dkf, dvf, dgd, dgf, dlogf, proj, *tables, bf_pad)


def _to_rows(a, tq):
    h, t = a.shape
    return a.reshape(h, t // tq, 1, tq)


def kernel(x, ffn1_norm, ffn1_w_gate, ffn1_w_up, ffn1_w_down, mix_norm, w_in, b_forget, b_gate_dil, b_gate_fox, w_proj_dil, w_proj_fox, w_out, ffn2_norm, ffn2_w_gate, ffn2_w_up, ffn2_w_down, final_norm, loss_target, m_ffn1_norm, m_ffn1_w_gate, m_ffn1_w_up, m_ffn1_w_down, m_mix_norm, m_w_in, m_b_forget, m_b_gate_dil, m_b_gate_fox, m_w_proj_dil, m_w_proj_fox, m_w_out, m_ffn2_norm, m_ffn2_w_gate, m_ffn2_w_up, m_ffn2_w_down, m_final_norm, v_ffn1_norm, v_ffn1_w_gate, v_ffn1_w_up, v_ffn1_w_down, v_mix_norm, v_w_in, v_b_forget, v_b_gate_dil, v_b_gate_fox, v_w_proj_dil, v_w_proj_fox, v_w_out, v_ffn2_norm, v_ffn2_w_gate, v_ffn2_w_up, v_ffn2_w_down, v_final_norm):
    t, d = x.shape[1], x.shape[2]
    hd = w_proj_dil.shape[1]
    nh = hd // HEAD_DIM
    n_f = b_forget.shape[1]
    cols = w_in.shape[2]
    in_cols = N_DEV * cols
    assert in_cols == 6 * hd + n_f + 2 * d and n_f == nh and n_f <= LANE
    np_ = 6 * hd + 2 * d + LANE
    scale = HEAD_DIM ** -0.5
    tq = _tile(t, 512, LANE)
    assert MAX_WINDOW % tq == 0 and tq % 16 == 0

    x2d = x[0]
    tgt = loss_target[0]

    ag_order = [ffn1_w_gate, ffn1_w_up, ffn1_w_down, w_in, w_proj_dil, w_proj_fox, w_out,
                ffn2_w_gate, ffn2_w_up, ffn2_w_down]
    ag_first, tok = _exchange_start([w[0].astype(BF) for w in ag_order[:2]], True, "ag_start_first")
    ag_rest, ag_token = _exchange_start([w[0].astype(BF) for w in ag_order[2:]], True, "ag_start", dep=tok)
    ag = ag_first + ag_rest

    def gathered(idx, after, name):
        return _exchange_wait([ag[i] for i in idx], True, after, name)

    tables = _rope_tables(t)
    bf_pad = jnp.pad(b_forget, ((0, 0), (0, LANE - n_f)))

    hn1, hn1_t = _rms_fwd(x2d, ffn1_norm, "rms_ffn1", dep=ag_token)
    wg1, = gathered([0], hn1, "ag_wait_ffn1_gate")
    g1 = _ffn_gate(hn1, wg1, "ffn1_gate")
    wu1, = gathered([1], g1, "ag_wait_ffn1_up")
    u1, a1 = _ffn_up_act(hn1, wu1, g1, "ffn1_up_act")
    wd1, = gathered([2], a1, "ag_wait_ffn1_down")
    x1 = _ffn_down(a1, wd1, x2d, "ffn1_down")

    hm, hm_t = _rms_fwd(x1, mix_norm, "rms_mix")
    win_g, = gathered([3], hm, "ag_wait_w_in")
    segments = [(0, 6 * hd), (6 * hd + n_f, in_cols), (6 * hd, 6 * hd + n_f)]
    pieces = []
    for lo, hi in segments:
        for j in range(lo // cols, (hi - 1) // cols + 1):
            s, e = max(lo, j * cols), min(hi, (j + 1) * cols)
            pieces.append(win_g[j, :, s - j * cols:e - j * cols])
    win_p = jnp.concatenate(pieces + [jnp.zeros((d, LANE - n_f), BF)], axis=1)
    proj = _mm_nn(hm, win_p, F32, "w_in_fwd")
    qd, kd, vd, qf, kf, vf, logf = _mixer_prep(proj, tables, bf_pad, hd, scale)
    csum = _cumsum_rows(logf, False, "cumsum_logf")
    c_heads = csum[:, :nh].T
    c_row = _to_rows(c_heads, tq)
    c_rep = jnp.broadcast_to(c_heads[:, :, None], (nh, t, LANE))
    dil_bias = _dil_bias_tiles(tq)
    dil_bias_t = dil_bias.transpose(0, 2, 1)
    yd, lse_d, lse_d_row = _attn_fwd("dil", qd, kd, vd, dil_bias, tq, "attn_dil_fwd")
    yf, lse_f, lse_f_row = _attn_fwd("fox", qf, kf, vf, c_row, tq, "attn_fox_fwd")
    wpd_g, wpf_g = gathered([4, 5], yf, "ag_wait_proj")
    wpd = wpd_g.transpose(1, 0, 2).reshape(hd, d)
    wpf = wpf_g.transpose(1, 0, 2).reshape(hd, d)
    pd, pf, merged = _proj_merge(yd, yf, wpd, wpf, proj, b_gate_dil, b_gate_fox, hd)
    wout_g, = gathered([6], merged, "ag_wait_w_out")
    wout = wout_g.reshape(d, d)
    x2 = _mm_nn(merged, wout, F32, "w_out_fwd", residual=x1, tn_pref=1024)

    hn2, hn2_t = _rms_fwd(x2, ffn2_norm, "rms_ffn2")
    wg2, wu2 = gathered([7, 8], hn2, "ag_wait_ffn2_gate_up")
    g2, u2, a2 = _ffn_gate_up(hn2, wg2, wu2, "ffn2_gate_up")
    wd2, = gathered([9], a2, "ag_wait_ffn2_down")
    x3 = _ffn_down(a2, wd2, x2, "ffn2_down")

    dx3, dx3b, d_final, loss_lanes = _loss_head(x3, final_norm.reshape(1, d), tgt)

    def ffn_bwd(dxb, hn_t, g, u, a, wg, wu, wd, x_in, gain, dres, tag):
        dg, du = _ffn_bwd_hidden(dxb, wd, g, u, tag + "_bwd_hidden")
        dwd = _ffn_dw_down(a, dxb, tag + "_dw_down")
        rs_down, tok = _exchange_start([dwd], False, "rs_start_" + tag + "_down")
        dwg, dwu = _ffn_dw_gate_up(hn_t, dg, du, tag + "_dw_gate_up", dep=tok)
        rs_gu, tok = _exchange_start([dwg, dwu], False, "rs_start_" + tag + "_gate_up")
        dhn = _ffn_bwd_input(dg, du, wg, wu, tag + "_bwd_input", dep=tok)
        dx, dx_bf, dgain = _rms_bwd(dhn, x_in, gain, dres, "rms_" + tag + "_bwd")
        return dx, dx_bf, dgain, rs_gu + rs_down

    dx2, dx2b, d_ffn2_norm, rs_ffn2 = ffn_bwd(dx3b, hn2_t, g2, u2, a2, wg2, wu2, wd2, x2, ffn2_norm, dx3, "ffn2")

    dmerged = _mm_nt(dx2b, wout, F32, "w_out_bwd")
    dwout = _mm_tn(merged, dx2b, BF, "w_out_dw", tn_pref=1024)
    dpd, dpf, dgd, dgf, d_bd, d_bf = _merge_bwd(dmerged, pd, pf, proj, b_gate_dil, b_gate_fox, hd)
    dyd = _mm_nt(dpd, wpd, BF, "proj_dil_bwd")
    dyf = _mm_nt(dpf, wpf, BF, "proj_fox_bwd")
    dwpd = _mm_tn(yd, dpd, BF, "proj_dil_dw", tn_pref=1024)
    dwpf = _mm_tn(yf, dpf, BF, "proj_fox_dw", tn_pref=1024)
    dwpd_c = dwpd.reshape(hd, N_DEV, d // N_DEV).transpose(1, 0, 2)
    dwpf_c = dwpf.reshape(hd, N_DEV, d // N_DEV).transpose(1, 0, 2)
    dwout_c = dwout.reshape(N_DEV, d // N_DEV, d)
    rs_mix, tok = _exchange_start([dwout_c, dwpd_c, dwpf_c], False, "rs_start_mixer")

    dqd, dl_d = _attn_bwd_dq("dil", qd, kd, vd, yd, dyd, lse_d, dil_bias, tq, "attn_dil_dq", dep=tok)
    dkd, dvd = _attn_bwd_dkv("dil", qd, kd, vd, dyd, lse_d_row, dl_d, dil_bias_t, None, tq, "attn_dil_dkv")
    dqf, dl_f = _attn_bwd_dq("fox", qf, kf, vf, yf, dyf, lse_f, c_row, tq, "attn_fox_dq")
    dkf, dvf, dc = _attn_bwd_dkv("fox", qf, kf, vf, dyf, lse_f_row, dl_f, c_rep, c_row, tq, "attn_fox_dkv")
    dc_pad = jnp.pad(dc[:, :, 0, :].reshape(nh, t).T, ((0, 0), (0, LANE - nh)))
    dlogf = _cumsum_rows(dc_pad, True, "revcumsum_dc")
    dproj, d_bforget = _assemble_dproj(dqd, dkd, dvd, dqf, dkf, dvf, dgd, dgf, dlogf, proj, tables, bf_pad, scale)

    dwin_p = _mm_tn(hm_t, dproj, BF, "w_in_dw", tk_pref=DW_ROWS, a_transposed=True)
    def perm_col(c):
        if c < 6 * hd:
            return c
        return c + 2 * d if c < 6 * hd + n_f else c - n_f

    shards = []
    for j in range(N_DEV):
        cuts = sorted({j * cols, (j + 1) * cols} | {c for c in (6 * hd, 6 * hd + n_f) if j * cols < c < (j + 1) * cols})
        shards.append(jnp.concatenate([dwin_p[:, perm_col(lo):perm_col(lo) + hi - lo]
                                       for lo, hi in zip(cuts[:-1], cuts[1:])], axis=1))
    dwin_c = jnp.stack(shards)
    rs_win, tok = _exchange_start([dwin_c], False, "rs_start_w_in")
    dx1, dx1b, d_mix_norm = _mm_nt(dproj, win_p, F32, "w_in_bwd", tn_pref=d, tk_pref=1152,
                                   rms=(x1, mix_norm, dx2), dep=tok)

    grad_x, _, d_ffn1_norm, rs_ffn1 = ffn_bwd(dx1b, hn1_t, g1, u1, a1, wg1, wu1, wd1, x2d, ffn1_norm, dx1, "ffn1")

    def update(handles, names, after, tag):
        recvs = _exchange_wait(handles, False, after, "rs_wait_" + tag)
        res = {}
        for recv, n in zip(recvs, names):
            w, m, v = wmv[n]
            g, delta, m2, v2 = _adam_from_partials(recv, w[0], m[0], v[0], "adam_" + n)
            res[n] = (g[None], delta[None], m2[None], v2[None])
        return res, g

    wmv = {
        "ffn1_w_gate": (ffn1_w_gate, m_ffn1_w_gate, v_ffn1_w_gate),
        "ffn1_w_up": (ffn1_w_up, m_ffn1_w_up, v_ffn1_w_up),
        "ffn1_w_down": (ffn1_w_down, m_ffn1_w_down, v_ffn1_w_down),
        "w_in": (w_in, m_w_in, v_w_in),
        "w_proj_dil": (w_proj_dil, m_w_proj_dil, v_w_proj_dil),
        "w_proj_fox": (w_proj_fox, m_w_proj_fox, v_w_proj_fox),
        "w_out": (w_out, m_w_out, v_w_out),
        "ffn2_w_gate": (ffn2_w_gate, m_ffn2_w_gate, v_ffn2_w_gate),
        "ffn2_w_up": (ffn2_w_up, m_ffn2_w_up, v_ffn2_w_up),
        "ffn2_w_down": (ffn2_w_down, m_ffn2_w_down, v_ffn2_w_down),
    }
    big = {}
    after = grad_x
    for handles, names, tag in [
            (rs_ffn2, ["ffn2_w_gate", "ffn2_w_up", "ffn2_w_down"], "ffn2"),
            (rs_mix, ["w_out", "w_proj_dil", "w_proj_fox"], "mixer"),
            (rs_win, ["w_in"], "w_in"),
            (rs_ffn1, ["ffn1_w_gate", "ffn1_w_up", "ffn1_w_down"], "ffn1")]:
        res, after = update(handles, names, after, tag)
        big.update(res)

    def lanes(a):
        a = a.reshape(1, -1)
        return jnp.pad(a, ((0, 0), (0, d - a.shape[1])))

    small_names = ["ffn1_norm", "mix_norm", "b_gate_dil", "b_gate_fox", "ffn2_norm", "final_norm", "b_forget"]
    small_g = [d_ffn1_norm, d_mix_norm, d_bd, d_bf, d_ffn2_norm, d_final, d_bforget[:, :n_f]]
    small_w = [ffn1_norm, mix_norm, b_gate_dil, b_gate_fox, ffn2_norm, final_norm, b_forget]
    small_m = [m_ffn1_norm, m_mix_norm, m_b_gate_dil, m_b_gate_fox, m_ffn2_norm, m_final_norm, m_b_forget]
    small_v = [v_ffn1_norm, v_mix_norm, v_b_gate_dil, v_b_gate_fox, v_ffn2_norm, v_final_norm, v_b_forget]
    pack = lambda arrs, last: jnp.concatenate([lanes(a) for a in arrs] + [last], axis=0)
    g_all = _allreduce_small(pack(small_g, loss_lanes))
    zero_row = jnp.zeros((1, d), F32)
    one_row = jnp.ones((1, d), F32)
    s_delta, s_m, s_v = _adam_small(g_all, pack(small_w, zero_row), pack(small_m, zero_row), pack(small_v, one_row))
    loss = g_all[len(small_names), 0]

    def unpack(packed, i, like):
        return packed[i, :like.size].reshape(like.shape)

    small = {}
    for i, (n, w) in enumerate(zip(small_names, small_w)):
        small[n] = (unpack(g_all, i, w), unpack(s_delta, i, w), unpack(s_m, i, w), unpack(s_v, i, w))

    order = ["ffn1_norm", "ffn1_w_gate", "ffn1_w_up", "ffn1_w_down", "mix_norm", "w_in", "b_forget", "b_gate_dil",
             "b_gate_fox", "w_proj_dil", "w_proj_fox", "w_out", "ffn2_norm", "ffn2_w_gate", "ffn2_w_up",
             "ffn2_w_down", "final_norm"]
    res = {**big, **small}
    outs = [loss, grad_x[None]]
    for slot in range(4):
        outs += [res[n][slot] for n in order]
    return tuple(outs)
```

```python
import functools

import numpy as np
import jax
import jax.numpy as jnp
from jax import lax
from jax.experimental import pallas as pl
from jax.experimental.pallas import tpu as pltpu

BF = jnp.bfloat16
F32 = jnp.float32
MESH = pl.DeviceIdType.MESH
N_DEV = 8

HEAD_DIM = 128
ROPE_DIM = HEAD_DIM // 4
ROPE_HALF = ROPE_DIM // 2
ROPE_THETA = 500000.0
NORM_EPS = 1e-6
DIL_PATTERNS = ((128, 1), (512, 4), (2048, 16))
MAX_WINDOW = 2048
LANE = 128
NEG = -1e30

ADAM_LR = 0.001
ADAM_B1 = 0.9
ADAM_B2 = 0.999
ADAM_EPS = 1e-08
ADAM_WD = 0.01
ADAM_STEP = 10

VMEM_LIMIT_BYTES = 56 * 1024 * 1024
FFN_ROWS = 1024
DW_ROWS = 1024
ANY = pl.BlockSpec(memory_space=pl.ANY)

NN = (((1,), (0,)), ((), ()))
NT = (((1,), (1,)), ((), ()))
TN = (((0,), (0,)), ((), ()))


def _dot(a, b, dn=NN):
    return lax.dot_general(a, b, dn, preferred_element_type=F32)


def _sig(x):
    return 1.0 / (1.0 + jnp.exp(-x))


def _tile(n, pref, align):
    best = None
    t = align
    while t <= min(n, pref):
        if n % t == 0:
            best = t
        t += align
    return n if best is None else best


def _params():
    return pltpu.CompilerParams(vmem_limit_bytes=VMEM_LIMIT_BYTES)


def _call(body, args, dep=None, **kw):
    if dep is not None:
        n_in = len(args)
        inner = body

        def body(*refs):
            inner(*refs[:n_in], *refs[n_in + 1:])

        kw["in_specs"] = list(kw["in_specs"]) + [ANY]
        args = list(args) + [dep]
    return pl.pallas_call(body, **kw)(*args)


def _peers():
    x, y, c = lax.axis_index("x"), lax.axis_index("y"), lax.axis_index("c")
    me = 4 * x + 2 * y + c
    peers = []
    for k in range(1, N_DEV):
        px = 1 - x if (k >> 2) & 1 else x
        py = 1 - y if (k >> 1) & 1 else y
        pc = 1 - c if k & 1 else c
        peers.append((k, (px, py, pc), 4 * px + 2 * py + pc))
    return me, peers


HBM = pl.BlockSpec(memory_space=pltpu.HBM)
SEM = pl.BlockSpec(memory_space=pltpu.SEMAPHORE)
EFFECT = pltpu.SideEffectType.DATAFLOW_SIDE_EFFECTING


def _exchange_copy(gather, src_ref, land_ref, send_sems, recv_sems, me, k, peer, peer_flat, landing):
    return pltpu.make_async_remote_copy(
        src_ref=src_ref if gather else src_ref.at[peer_flat], dst_ref=land_ref.at[landing],
        send_sem=send_sems.at[k], recv_sem=recv_sems.at[k], device_id=peer, device_id_type=MESH)


ALL_PEERS = (1, 2, 3, 4, 5, 6, 7)
SIBLING = 1
SAME_CORE = (2, 4, 6)
FIRST_LEVEL = (SIBLING,) + SAME_CORE


def _exchange_start(srcs, gather, name, dep=None, ks=ALL_PEERS):
    n = len(srcs)
    extra = [] if dep is None else [dep]

    def body(*refs):
        src_refs, land_refs = refs[:n], refs[n:2 * n]
        refs = refs[2 * n + len(extra):]
        send_refs, recv_refs = refs[:n], refs[n:2 * n]
        token = refs[4 * n]
        me, peers = _peers()
        for i in range(n):
            for k, peer, peer_flat in peers:
                if k in ks:
                    _exchange_copy(gather, src_refs[i], land_refs[i], send_refs[i], recv_refs[i],
                                   me, k, peer, peer_flat, me).start()
        token[...] = jnp.zeros_like(token)

    lands = [lax.empty((N_DEV,) + s.shape[-2:], s.dtype) for s in srcs]
    sems = [pltpu.SemaphoreType.DMA((N_DEV,)) for _ in range(2 * n)]
    out = pl.pallas_call(
        body, name=name,
        out_shape=tuple(sems) + tuple(pltpu.HBM(a.shape, a.dtype) for a in list(srcs) + lands)
        + (jax.ShapeDtypeStruct((8, LANE), F32),),
        in_specs=[HBM] * (2 * n) + [ANY] * len(extra),
        out_specs=tuple([SEM] * (2 * n) + [HBM] * (2 * n) + [pl.BlockSpec(memory_space=pltpu.VMEM)]),
        input_output_aliases={i: 2 * n + i for i in range(2 * n)},
        compiler_params=pltpu.CompilerParams(has_side_effects=EFFECT),
    )(*[pltpu.with_memory_space_constraint(a, pltpu.HBM) for a in list(srcs) + lands], *extra)
    handles = [(out[2 * n + i], out[3 * n + i], out[i], out[n + i]) for i in range(n)]
    return handles, out[4 * n]


def _exchange_wait(handles, gather, after, name):
    n = len(handles)

    def body(*refs):
        src_refs, land_refs = refs[:n], refs[n:2 * n]
        send_refs, recv_refs = refs[2 * n:3 * n], refs[3 * n:4 * n]
        me, peers = _peers()
        for i in range(n):
            for k, peer, peer_flat in peers:
                cp = _exchange_copy(gather, src_refs[i], land_refs[i], send_refs[i], recv_refs[i],
                                    me, k, peer, peer_flat, peer_flat)
                cp.wait_send()
                cp.wait_recv()

    srcs = [h[0] for h in handles]
    lands = [h[1] for h in handles]
    out = pl.pallas_call(
        body, name=name,
        out_shape=tuple(pltpu.HBM(a.shape, a.dtype) for a in srcs + lands),
        in_specs=[HBM] * (2 * n) + [SEM] * (2 * n) + [ANY],
        out_specs=tuple([HBM] * (2 * n)),
        input_output_aliases={i: i for i in range(2 * n)},
        compiler_params=pltpu.CompilerParams(has_side_effects=EFFECT),
    )(*srcs, *lands, *[h[2] for h in handles], *[h[3] for h in handles], after)
    me = 4 * lax.axis_index("x") + 2 * lax.axis_index("y") + lax.axis_index("c")
    filled = []
    for src, land in zip(out[:n], out[n:]):
        own = src[None] if gather else lax.dynamic_slice_in_dim(src, me, 1, axis=0)
        filled.append(lax.dynamic_update_slice_in_dim(land, own, me, axis=0))
    return filled


def _gather_relay(handles, after, name):
    n = len(handles)

    def body(*refs):
        land_refs, recv_refs = refs[:n], refs[n:2 * n]
        refs = refs[2 * n + 1:]
        send2_refs, recv2_refs = refs[n:2 * n], refs[2 * n:3 * n]
        me, peers = _peers()
        sibling = peers[SIBLING - 1][1]
        for i in range(n):
            for k, peer, peer_flat in peers:
                if k in SAME_CORE:
                    block = land_refs[i].at[peer_flat]
                    pltpu.make_async_remote_copy(
                        src_ref=block, dst_ref=block, send_sem=send2_refs[i].at[k], recv_sem=recv_refs[i].at[k],
                        device_id=peer, device_id_type=MESH).wait_recv()
                    pltpu.make_async_remote_copy(
                        src_ref=block, dst_ref=block, send_sem=send2_refs[i].at[k], recv_sem=recv2_refs[i].at[k],
                        device_id=sibling, device_id_type=MESH).start()

    lands = [h[1] for h in handles]
    sems = [pltpu.SemaphoreType.DMA((N_DEV,)) for _ in range(2 * n)]
    out = pl.pallas_call(
        body, name=name,
        out_shape=tuple(pltpu.HBM(a.shape, a.dtype) for a in lands) + tuple(sems),
        in_specs=[HBM] * n + [SEM] * n + [ANY],
        out_specs=tuple([HBM] * n + [SEM] * (2 * n)),
        input_output_aliases={i: i for i in range(n)},
        compiler_params=pltpu.CompilerParams(has_side_effects=EFFECT),
    )(*lands, *[h[3] for h in handles], after)
    return [(h[0], out[i], h[2], h[3], out[n + i], out[2 * n + i]) for i, h in enumerate(handles)]


def _gather_wait(handles, after, name):
    n = len(handles)

    def body(*refs):
        src_refs, land_refs = refs[:n], refs[n:2 * n]
        send_refs, recv_refs = refs[2 * n:3 * n], refs[3 * n:4 * n]
        send2_refs, recv2_refs = refs[4 * n:5 * n], refs[5 * n:6 * n]
        me, peers = _peers()
        _, sibling, sibling_flat = peers[SIBLING - 1]
        for i in range(n):
            for k, peer, peer_flat in peers:
                if k in FIRST_LEVEL:
                    cp = _exchange_copy(True, src_refs[i], land_refs[i], send_refs[i], recv_refs[i],
                                        me, k, peer, peer_flat, peer_flat)
                    cp.wait_send()
                    if k == SIBLING:
                        cp.wait_recv()
                if k in SAME_CORE:
                    mine = land_refs[i].at[peer_flat]
                    theirs = land_refs[i].at[peer_flat ^ SIBLING]
                    cp = pltpu.make_async_remote_copy(
                        src_ref=mine, dst_ref=theirs, send_sem=send2_refs[i].at[k], recv_sem=recv2_refs[i].at[k],
                        device_id=sibling, device_id_type=MESH)
                    cp.wait_send()
                    cp.wait_recv()

    srcs = [h[0] for h in handles]
    lands = [h[1] for h in handles]
    out = pl.pallas_call(
        body, name=name,
        out_shape=tuple(pltpu.HBM(a.shape, a.dtype) for a in srcs + lands),
        in_specs=[HBM] * (2 * n) + [SEM] * (4 * n) + [ANY],
        out_specs=tuple([HBM] * (2 * n)),
        input_output_aliases={i: i for i in range(2 * n)},
        compiler_params=pltpu.CompilerParams(has_side_effects=EFFECT),
    )(*srcs, *lands, *[h[2] for h in handles], *[h[3] for h in handles],
      *[h[4] for h in handles], *[h[5] for h in handles], after)
    me = 4 * lax.axis_index("x") + 2 * lax.axis_index("y") + lax.axis_index("c")
    return [lax.dynamic_update_slice_in_dim(land, src[None], me, axis=0) for src, land in zip(out[:n], out[n:])]


def _allreduce_small(p):
    rows, d = p.shape

    def body(p_ref, o_ref, recv_ref, send_sems, recv_sems):
        me, peers = _peers()
        recv_ref[me] = p_ref[...]
        sends = []
        for k, peer, peer_flat in peers:
            cp = pltpu.make_async_remote_copy(
                src_ref=p_ref, dst_ref=recv_ref.at[me],
                send_sem=send_sems.at[k], recv_sem=recv_sems.at[k],
                device_id=peer, device_id_type=MESH)
            cp.start()
            sends.append(cp)
        for k, peer, peer_flat in peers:
            pltpu.make_async_remote_copy(
                src_ref=p_ref, dst_ref=recv_ref.at[peer_flat],
                send_sem=send_sems.at[k], recv_sem=recv_sems.at[k],
                device_id=peer, device_id_type=MESH).wait_recv()
        for cp in sends:
            cp.wait_send()
        acc = recv_ref[0]
        for s in range(1, N_DEV):
            acc = acc + recv_ref[s]
        is_loss = lax.broadcasted_iota(jnp.int32, (rows, d), 0) == rows - 1
        total = jnp.sum(jnp.where(is_loss, acc, 0.0))
        o_ref[...] = jnp.where(is_loss, total, acc)

    return pl.pallas_call(
        body, name="allreduce_small",
        out_shape=jax.ShapeDtypeStruct((rows, d), F32),
        in_specs=[pl.BlockSpec(memory_space=pltpu.VMEM)],
        out_specs=pl.BlockSpec(memory_space=pltpu.VMEM),
        scratch_shapes=[pltpu.VMEM((N_DEV, rows, d), F32),
                        pltpu.SemaphoreType.DMA((N_DEV,)), pltpu.SemaphoreType.DMA((N_DEV,))],
    )(p)


def _adam_math(w, g, m, v):
    m2 = ADAM_B1 * m + (1.0 - ADAM_B1) * g
    v2 = ADAM_B2 * v + (1.0 - ADAM_B2) * (g * g)
    m_hat = m2 / (1.0 - ADAM_B1 ** ADAM_STEP)
    v_hat = v2 / (1.0 - ADAM_B2 ** ADAM_STEP)
    delta = -ADAM_LR * (m_hat / (jnp.sqrt(v_hat) + ADAM_EPS) + ADAM_WD * w)
    return delta, m2, v2


def _adam_from_partials(parts, w, m, v, name):
    r, c = w.shape
    tr = _tile(r, 256, 16)

    def body(p_ref, w_ref, m_ref, v_ref, g_out, d_out, m_out, v_out):
        g = p_ref[0].astype(F32)
        for s in range(1, N_DEV):
            g = g + p_ref[s].astype(F32)
        delta, m2, v2 = _adam_math(w_ref[...], g, m_ref[...], v_ref[...])
        g_out[...] = g
        d_out[...] = delta
        m_out[...] = m2
        v_out[...] = v2

    blk = pl.BlockSpec((tr, c), lambda i: (i, 0))
    out = jax.ShapeDtypeStruct((r, c), F32)
    return pl.pallas_call(
        body, name=name, grid=(r // tr,),
        in_specs=[pl.BlockSpec((N_DEV, tr, c), lambda i: (0, i, 0)), blk, blk, blk],
        out_specs=[blk, blk, blk, blk], out_shape=[out, out, out, out],
        compiler_params=_params(),
    )(parts, w, m, v)


def _adam_small(g, w, m, v):
    def body(g_ref, w_ref, m_ref, v_ref, d_out, m_out, v_out):
        delta, m2, v2 = _adam_math(w_ref[...], g_ref[...], m_ref[...], v_ref[...])
        d_out[...] = delta
        m_out[...] = m2
        v_out[...] = v2

    out = jax.ShapeDtypeStruct(g.shape, F32)
    return pl.pallas_call(body, name="adam_small", out_shape=[out, out, out])(g, w, m, v)


def _rms_fwd(x, gain, name, dep=None):
    t, d = x.shape
    tr = _tile(t, 256, LANE)

    def body(x_ref, g_ref, o_ref, ot_ref):
        xv = x_ref[...]
        r = lax.rsqrt(jnp.mean(xv * xv, axis=-1, keepdims=True) + NORM_EPS)
        y = xv * r * g_ref[...]
        o_ref[...] = y.astype(BF)
        ot_ref[...] = jnp.transpose(y).astype(BF)

    return _call(
        body, [x, gain], dep=dep, name=name, grid=(t // tr,),
        in_specs=[pl.BlockSpec((tr, d), lambda i: (i, 0)), pl.BlockSpec((1, d), lambda i: (0, 0))],
        out_specs=[pl.BlockSpec((tr, d), lambda i: (i, 0)), pl.BlockSpec((d, tr), lambda i: (0, i))],
        out_shape=[jax.ShapeDtypeStruct((t, d), BF), jax.ShapeDtypeStruct((d, t), BF)],
        compiler_params=_params(),
    )


def _rms_vjp(xv, gain, dy):
    r = lax.rsqrt(jnp.mean(xv * xv, axis=-1, keepdims=True) + NORM_EPS)
    xhat = xv * r
    dxhat = dy * gain
    dx = r * (dxhat - xhat * jnp.mean(dxhat * xhat, axis=-1, keepdims=True))
    dgain = jnp.sum(dy * xhat, axis=0, keepdims=True)
    return dx, dgain


def _loss_head(x, gain, target):
    t, d = x.shape
    tr = _tile(t, 256, 16)

    def body(x_ref, g_ref, t_ref, dx_ref, dxb_ref, dg_ref, loss_ref):
        xv = x_ref[...]
        gain = g_ref[...]
        r = lax.rsqrt(jnp.mean(xv * xv, axis=-1, keepdims=True) + NORM_EPS)
        err = xv * r * gain - t_ref[...]
        dx, dgain = _rms_vjp(xv, gain, err * (1.0 / d))
        dx_ref[...] = dx
        dxb_ref[...] = dx.astype(BF)

        @pl.when(pl.program_id(0) == 0)
        def _():
            dg_ref[...] = jnp.zeros_like(dg_ref)
            loss_ref[...] = jnp.zeros_like(loss_ref)

        dg_ref[...] += dgain
        loss_ref[...] += jnp.sum(err * err, axis=0, keepdims=True) * (0.5 / d)

    row = pl.BlockSpec((tr, d), lambda i: (i, 0))
    vec = pl.BlockSpec((1, d), lambda i: (0, 0))
    return pl.pallas_call(
        body, name="loss_head", grid=(t // tr,),
        in_specs=[row, vec, row], out_specs=[row, row, vec, vec],
        out_shape=[jax.ShapeDtypeStruct((t, d), F32), jax.ShapeDtypeStruct((t, d), BF),
                   jax.ShapeDtypeStruct((1, d), F32), jax.ShapeDtypeStruct((1, d), F32)],
        compiler_params=_params(),
    )(x, gain, target)


def _mm_nn(a, b, out_dtype, name, residual=None, tm_pref=512, tn_pref=1152):
    m, k = a.shape
    n = b.shape[1]
    tm, tn = _tile(m, tm_pref, 16), _tile(n, tn_pref, LANE)

    def body(*refs):
        if residual is None:
            a_ref, b_ref, o_ref = refs
            o_ref[...] = _dot(a_ref[...], b_ref[...]).astype(out_dtype)
        else:
            a_ref, b_ref, r_ref, o_ref = refs
            o_ref[...] = (r_ref[...] + _dot(a_ref[...], b_ref[...])).astype(out_dtype)

    in_specs = [pl.BlockSpec((tm, k), lambda j, i: (i, 0)), pl.BlockSpec((k, tn), lambda j, i: (0, j))]
    args = [a, b]
    if residual is not None:
        in_specs.append(pl.BlockSpec((tm, tn), lambda j, i: (i, j)))
        args.append(residual)
    return pl.pallas_call(
        body, name=name, grid=(n // tn, m // tm), in_specs=in_specs,
        out_specs=pl.BlockSpec((tm, tn), lambda j, i: (i, j)),
        out_shape=jax.ShapeDtypeStruct((m, n), out_dtype), compiler_params=_params(),
    )(*args)


def _rms_bwd_tail(dy_ref, first, x_ref, g_ref, dres_ref, dx_ref, dxb_ref, dg_ref):
    @pl.when(first)
    def _():
        dg_ref[...] = jnp.zeros_like(dg_ref)

    gain = g_ref[...]
    for r in range(0, dy_ref.shape[0], LANE):
        rows = pl.ds(r, min(LANE, dy_ref.shape[0] - r))
        dx, dgain = _rms_vjp(x_ref[rows, :], gain, dy_ref[rows, :])
        dx = dx + dres_ref[rows, :]
        dx_ref[rows, :] = dx
        dxb_ref[rows, :] = dx.astype(BF)
        dg_ref[...] += dgain


def _mm_nt(a, b, out_dtype, name, tm_pref=512, tn_pref=1024, tk_pref=2048, rms=None, dep=None):
    m, k = a.shape
    n = b.shape[0]
    tm, tn, tk = _tile(m, tm_pref, 16), _tile(n, tn_pref, LANE), _tile(k, tk_pref, LANE)
    nk = k // tk
    assert rms is None or tn == n

    def body(*refs):
        if rms is None:
            a_ref, b_ref, o_ref, acc_ref = refs
        else:
            a_ref, b_ref, x_ref, g_ref, dres_ref, dx_ref, dxb_ref, dg_ref, acc_ref = refs
        kk = pl.program_id(2)

        @pl.when(kk == 0)
        def _():
            acc_ref[...] = jnp.zeros_like(acc_ref)

        acc_ref[...] += _dot(a_ref[...], b_ref[...], NT)

        @pl.when(kk == nk - 1)
        def _():
            if rms is None:
                o_ref[...] = acc_ref[...].astype(out_dtype)
            else:
                _rms_bwd_tail(acc_ref, pl.program_id(1) == 0, x_ref, g_ref, dres_ref, dx_ref, dxb_ref, dg_ref)

    in_specs = [pl.BlockSpec((tm, tk), lambda j, i, kk: (i, kk)), pl.BlockSpec((tn, tk), lambda j, i, kk: (j, kk))]
    row = pl.BlockSpec((tm, tn), lambda j, i, kk: (i, j))
    if rms is None:
        args, out_specs, out_shape = [a, b], row, jax.ShapeDtypeStruct((m, n), out_dtype)
    else:
        vec = pl.BlockSpec((1, n), lambda j, i, kk: (0, 0))
        args, in_specs = [a, b, *rms], in_specs + [row, vec, row]
        out_specs = [row, row, vec]
        out_shape = [jax.ShapeDtypeStruct((m, n), F32), jax.ShapeDtypeStruct((m, n), BF),
                     jax.ShapeDtypeStruct((1, n), F32)]
    return _call(
        body, args, dep=dep, name=name, grid=(n // tn, m // tm, nk), in_specs=in_specs, out_specs=out_specs,
        out_shape=out_shape, scratch_shapes=[pltpu.VMEM((tm, tn), F32)], compiler_params=_params(),
    )


def _mm_tn(a, b, out_dtype, name, tn_pref=1152, tk_pref=512, a_transposed=False):
    (k, t) = a.shape if a_transposed else a.shape[::-1]
    n = b.shape[1]
    tn, tk = _tile(n, tn_pref, LANE), _tile(t, tk_pref, LANE if a_transposed else 16)
    nt = t // tk

    def body(a_ref, b_ref, o_ref, acc_ref):
        tt = pl.program_id(1)

        @pl.when(tt == 0)
        def _():
            acc_ref[...] = jnp.zeros_like(acc_ref)

        acc_ref[...] += _dot(a_ref[...], b_ref[...], NN if a_transposed else TN)

        @pl.when(tt == nt - 1)
        def _():
            o_ref[...] = acc_ref[...].astype(out_dtype)

    if a_transposed:
        a_spec = pl.BlockSpec((k, tk), lambda j, tt: (0, tt))
    else:
        a_spec = pl.BlockSpec((tk, k), lambda j, tt: (tt, 0))
    return pl.pallas_call(
        body, name=name, grid=(n // tn, nt),
        in_specs=[a_spec, pl.BlockSpec((tk, tn), lambda j, tt: (tt, j))],
        out_specs=pl.BlockSpec((k, tn), lambda j, tt: (0, j)),
        out_shape=jax.ShapeDtypeStruct((k, n), out_dtype),
        scratch_shapes=[pltpu.VMEM((k, tn), F32)], compiler_params=_params(),
    )(a, b)


def _ffn_gate_up(hn, wg, wu, name):
    t, d = hn.shape
    ns, _, f = wg.shape
    tm = _tile(t, FFN_ROWS, 16)

    def body(h_ref, wg_ref, wu_ref, g_ref, u_ref, a_ref):
        h = h_ref[...]
        g = _dot(h, wg_ref[...])
        u = _dot(h, wu_ref[...])
        g_ref[...] = g.astype(BF)
        u_ref[...] = u.astype(BF)
        a_ref[...] = (g * _sig(g) * u).astype(BF)

    wspec = pl.BlockSpec((None, d, f), lambda j, i: (j, 0, 0))
    hid = pl.BlockSpec((None, tm, f), lambda j, i: (j, i, 0))
    out = jax.ShapeDtypeStruct((ns, t, f), BF)
    return pl.pallas_call(
        body, name=name, grid=(ns, t // tm),
        in_specs=[pl.BlockSpec((tm, d), lambda j, i: (i, 0)), wspec, wspec],
        out_specs=[hid, hid, hid], out_shape=[out, out, out], compiler_params=_params(),
    )(hn, wg, wu)


def _ffn_gate(hn, wg, name):
    t, d = hn.shape
    ns, _, f = wg.shape
    tm = _tile(t, FFN_ROWS, 16)

    def body(h_ref, wg_ref, g_ref):
        g_ref[...] = _dot(h_ref[...], wg_ref[...]).astype(BF)

    return pl.pallas_call(
        body, name=name, grid=(ns, t // tm),
        in_specs=[pl.BlockSpec((tm, d), lambda j, i: (i, 0)), pl.BlockSpec((None, d, f), lambda j, i: (j, 0, 0))],
        out_specs=pl.BlockSpec((None, tm, f), lambda j, i: (j, i, 0)),
        out_shape=jax.ShapeDtypeStruct((ns, t, f), BF), compiler_params=_params(),
    )(hn, wg)


def _ffn_up_act(hn, wu, g, name):
    t, d = hn.shape
    ns, _, f = wu.shape
    tm = _tile(t, FFN_ROWS, 16)

    def body(h_ref, wu_ref, g_ref, u_ref, a_ref):
        u = _dot(h_ref[...], wu_ref[...])
        gv = g_ref[...].astype(F32)
        u_ref[...] = u.astype(BF)
        a_ref[...] = (gv * _sig(gv) * u).astype(BF)

    hid = pl.BlockSpec((None, tm, f), lambda j, i: (j, i, 0))
    out = jax.ShapeDtypeStruct((ns, t, f), BF)
    return pl.pallas_call(
        body, name=name, grid=(ns, t // tm),
        in_specs=[pl.BlockSpec((tm, d), lambda j, i: (i, 0)), pl.BlockSpec((None, d, f), lambda j, i: (j, 0, 0)), hid],
        out_specs=[hid, hid], out_shape=[out, out], compiler_params=_params(),
    )(hn, wu, g)


def _ffn_down(act, wd, xres, name):
    ns, t, f = act.shape
    d = wd.shape[2]
    tm = _tile(t, FFN_ROWS, 16)

    def body(a_ref, w_ref, x_ref, o_ref):
        @pl.when(pl.program_id(1) == 0)
        def _():
            o_ref[...] = x_ref[...]

        o_ref[...] += 0.5 * _dot(a_ref[...], w_ref[...])

    row = pl.BlockSpec((tm, d), lambda i, j: (i, 0))
    return pl.pallas_call(
        body, name=name, grid=(t // tm, ns),
        in_specs=[pl.BlockSpec((None, tm, f), lambda i, j: (j, i, 0)),
                  pl.BlockSpec((None, f, d), lambda i, j: (j, 0, 0)), row],
        out_specs=row, out_shape=jax.ShapeDtypeStruct((t, d), F32), compiler_params=_params(),
    )(act, wd, xres)


def _ffn_bwd_hidden(dxb, wd, g, u, name):
    t, d = dxb.shape
    ns, f, _ = wd.shape
    tm = _tile(t, FFN_ROWS, 16)

    def body(dx_ref, w_ref, g_ref, u_ref, dg_ref, du_ref):
        dh = 0.5 * _dot(dx_ref[...], w_ref[...], NT)
        gv = g_ref[...].astype(F32)
        uv = u_ref[...].astype(F32)
        s = _sig(gv)
        dg_ref[...] = (dh * uv * (s * (1.0 + gv * (1.0 - s)))).astype(BF)
        du_ref[...] = (dh * (gv * s)).astype(BF)

    hid = pl.BlockSpec((None, tm, f), lambda j, i: (j, i, 0))
    out = jax.ShapeDtypeStruct((ns, t, f), BF)
    return pl.pallas_call(
        body, name=name, grid=(ns, t // tm),
        in_specs=[pl.BlockSpec((tm, d), lambda j, i: (i, 0)),
                  pl.BlockSpec((None, f, d), lambda j, i: (j, 0, 0)), hid, hid],
        out_specs=[hid, hid], out_shape=[out, out], compiler_params=_params(),
    )(dxb, wd, g, u)


def _ffn_dw_down(act, dxb, name):
    ns, t, f = act.shape
    d = dxb.shape[1]
    tk = _tile(t, DW_ROWS, 16)
    nt = t // tk

    def body(a_ref, dx_ref, o_ref, acc_ref):
        tt = pl.program_id(1)

        @pl.when(tt == 0)
        def _():
            acc_ref[...] = jnp.zeros_like(acc_ref)

        acc_ref[...] += _dot(a_ref[...], dx_ref[...], TN)

        @pl.when(tt == nt - 1)
        def _():
            o_ref[...] = (0.5 * acc_ref[...]).astype(BF)

    return pl.pallas_call(
        body, name=name, grid=(ns, nt),
        in_specs=[pl.BlockSpec((None, tk, f), lambda j, tt: (j, tt, 0)),
                  pl.BlockSpec((tk, d), lambda j, tt: (tt, 0))],
        out_specs=pl.BlockSpec((None, f, d), lambda j, tt: (j, 0, 0)),
        out_shape=jax.ShapeDtypeStruct((ns, f, d), BF),
        scratch_shapes=[pltpu.VMEM((f, d), F32)], compiler_params=_params(),
    )(act, dxb)


def _ffn_dw_gate_up(hn_t, dg, du, name, dep=None):
    d, t = hn_t.shape
    ns, _, f = dg.shape
    tk = _tile(t, DW_ROWS, LANE)
    nt = t // tk

    def body(h_ref, dg_ref, du_ref, og_ref, ou_ref, accg_ref, accu_ref):
        tt = pl.program_id(1)

        @pl.when(tt == 0)
        def _():
            accg_ref[...] = jnp.zeros_like(accg_ref)
            accu_ref[...] = jnp.zeros_like(accu_ref)

        h = h_ref[...]
        accg_ref[...] += _dot(h, dg_ref[...])
        accu_ref[...] += _dot(h, du_ref[...])

        @pl.when(tt == nt - 1)
        def _():
            og_ref[...] = accg_ref[...].astype(BF)
            ou_ref[...] = accu_ref[...].astype(BF)

    hid = pl.BlockSpec((None, tk, f), lambda j, tt: (j, tt, 0))
    wspec = pl.BlockSpec((None, d, f), lambda j, tt: (j, 0, 0))
    out = jax.ShapeDtypeStruct((ns, d, f), BF)
    return _call(
        body, [hn_t, dg, du], dep=dep, name=name, grid=(ns, nt),
        in_specs=[pl.BlockSpec((d, tk), lambda j, tt: (0, tt)), hid, hid],
        out_specs=[wspec, wspec], out_shape=[out, out],
        scratch_shapes=[pltpu.VMEM((d, f), F32), pltpu.VMEM((d, f), F32)], compiler_params=_params(),
    )


def _rms_bwd(dy, x, gain, dres, name):
    t, d = x.shape
    tr = _tile(t, 256, 16)

    def body(dy_ref, x_ref, g_ref, dres_ref, dx_ref, dxb_ref, dg_ref):
        _rms_bwd_tail(dy_ref, pl.program_id(0) == 0, x_ref, g_ref, dres_ref, dx_ref, dxb_ref, dg_ref)

    row = pl.BlockSpec((tr, d), lambda i: (i, 0))
    vec = pl.BlockSpec((1, d), lambda i: (0, 0))
    return pl.pallas_call(
        body, name=name, grid=(t // tr,),
        in_specs=[row, row, vec, row], out_specs=[row, row, vec],
        out_shape=[jax.ShapeDtypeStruct((t, d), F32), jax.ShapeDtypeStruct((t, d), BF),
                   jax.ShapeDtypeStruct((1, d), F32)],
        compiler_params=_params(),
    )(dy, x, gain, dres)


def _ffn_bwd_input(dg, du, wg, wu, name, dep=None):
    ns, t, f = dg.shape
    d = wg.shape[1]
    tm = _tile(t, FFN_ROWS, 16)

    def body(dg_ref, du_ref, wg_ref, wu_ref, o_ref):
        @pl.when(pl.program_id(1) == 0)
        def _():
            o_ref[...] = jnp.zeros_like(o_ref)

        o_ref[...] += _dot(dg_ref[...], wg_ref[...], NT) + _dot(du_ref[...], wu_ref[...], NT)

    hid = pl.BlockSpec((None, tm, f), lambda i, j: (j, i, 0))
    wspec = pl.BlockSpec((None, d, f), lambda i, j: (j, 0, 0))
    return _call(
        body, [dg, du, wg, wu], dep=dep, name=name, grid=(t // tm, ns),
        in_specs=[hid, hid, wspec, wspec],
        out_specs=pl.BlockSpec((tm, d), lambda i, j: (i, 0)),
        out_shape=jax.ShapeDtypeStruct((t, d), F32), compiler_params=_params(),
    )


def _rope_tables(t):
    pos = jnp.arange(t, dtype=F32)
    inv_freq = ROPE_THETA ** (-jnp.arange(0, ROPE_DIM, 2, dtype=F32) / ROPE_DIM)
    ang = pos[:, None] * inv_freq[None, :]
    cos, sin = jnp.cos(ang), jnp.sin(ang)
    rest = HEAD_DIM - ROPE_DIM
    one = jnp.ones((t, rest), F32)
    zero_h = jnp.zeros((t, ROPE_HALF), F32)
    zero_r = jnp.zeros((t, rest), F32)
    c = jnp.concatenate([cos, cos, one], axis=1)
    s1 = jnp.concatenate([-sin, zero_h, zero_r], axis=1)
    s2 = jnp.concatenate([zero_h, sin, zero_r], axis=1)
    return c, s1, s2


def _rope(xh, c, s1, s2):
    return xh * c + pltpu.roll(xh, HEAD_DIM - ROPE_HALF, 1) * s1 + pltpu.roll(xh, ROPE_HALF, 1) * s2


def _rope_t(dh, c, s1, s2):
    return dh * c + pltpu.roll(dh * s1, ROPE_HALF, 1) + pltpu.roll(dh * s2, HEAD_DIM - ROPE_HALF, 1)


def _mixer_prep(proj, tables, bf_pad, hd, scale):
    t, np_ = proj.shape
    tr = _tile(t, 256, 16)
    nh = hd // HEAD_DIM
    nblk = hd // LANE
    f_blk = np_ // LANE - 1

    def body(qd_ref, kd_ref, vd_ref, qf_ref, kf_ref, vf_ref, fl_ref, c_ref, s1_ref, s2_ref, b_ref,
             oqd, okd, ovd, oqf, okf, ovf, olog):
        c, s1, s2 = c_ref[...], s1_ref[...], s2_ref[...]
        for h in range(nh):
            sl = slice(h * HEAD_DIM, (h + 1) * HEAD_DIM)
            oqd[:, sl] = (_rope(qd_ref[:, sl], c, s1, s2) * scale).astype(BF)
            okd[:, sl] = _rope(kd_ref[:, sl], c, s1, s2).astype(BF)
        ovd[...] = vd_ref[...].astype(BF)
        oqf[...] = (qf_ref[...] * scale).astype(BF)
        okf[...] = kf_ref[...].astype(BF)
        ovf[...] = vf_ref[...].astype(BF)
        z = fl_ref[...] + b_ref[...]
        olog[...] = jnp.minimum(z, 0.0) - jnp.log(1.0 + jnp.exp(-jnp.abs(z)))

    def col(kblk):
        return pl.BlockSpec((tr, hd), lambda i, kblk=kblk: (i, kblk))

    lane_row = pl.BlockSpec((tr, LANE), lambda i: (i, 0))
    in_specs = [col(0), col(1), col(2), col(3), col(4), col(5),
                pl.BlockSpec((tr, LANE), lambda i: (i, f_blk)),
                lane_row, lane_row, lane_row, pl.BlockSpec((1, LANE), lambda i: (0, 0))]
    o = pl.BlockSpec((tr, hd), lambda i: (i, 0))
    ob = jax.ShapeDtypeStruct((t, hd), BF)
    del nblk
    return pl.pallas_call(
        body, name="mixer_prep", grid=(t // tr,), in_specs=in_specs,
        out_specs=[o, o, o, o, o, o, lane_row],
        out_shape=[ob, ob, ob, ob, ob, ob, jax.ShapeDtypeStruct((t, LANE), F32)],
        compiler_params=_params(),
    )(proj, proj, proj, proj, proj, proj, proj, *tables, bf_pad)


def _split3(x):
    x1 = x.astype(BF)
    r1 = x - x1.astype(F32)
    x2 = r1.astype(BF)
    x3 = (r1 - x2.astype(F32)).astype(BF)
    return x1, x2, x3


def _cumsum_rows(x, reverse, name):
    t, w = x.shape
    blk = LANE
    nb = t // blk

    def body(x_ref, o_ref):
        r = lax.broadcasted_iota(jnp.int32, (blk, blk), 0)
        c = lax.broadcasted_iota(jnp.int32, (blk, blk), 1)
        tri = jnp.where((c >= r) if reverse else (c <= r), 1.0, 0.0).astype(BF)

        def step(i, carry):
            b = (nb - 1 - i) if reverse else i
            off = pl.multiple_of(b * blk, blk)
            xb = x_ref[pl.ds(off, blk), :]
            x1, x2, x3 = _split3(xb)
            o_ref[pl.ds(off, blk), :] = _dot(tri, x1) + _dot(tri, x2) + _dot(tri, x3) + carry
            return carry + jnp.sum(xb, axis=0, keepdims=True)

        lax.fori_loop(0, nb, step, jnp.zeros((1, w), F32))

    return pl.pallas_call(body, name=name, out_shape=jax.ShapeDtypeStruct((t, w), F32),
                          compiler_params=_params())(x)


ATTN_ROWS = 16


def _dil_bias_tiles(tq):
    nbias = MAX_WINDOW // tq + 1
    b = lax.broadcasted_iota(jnp.int32, (nbias, tq, tq), 0)
    i = lax.broadcasted_iota(jnp.int32, (nbias, tq, tq), 1)
    j = lax.broadcasted_iota(jnp.int32, (nbias, tq, tq), 2)
    delta = b * tq + i - j
    mult = jnp.zeros((nbias, tq, tq), F32)
    for w, dil in DIL_PATTERNS:
        mult = mult + jnp.where((delta >= 0) & (delta <= w) & (delta % dil == 0), 1.0, 0.0)
    return jnp.where(mult > 0.0, jnp.log(jnp.maximum(mult, 1.0)), NEG)


def _rep(x, width):
    return jnp.tile(x, (1, width // LANE))


def _chunks(n_rows, fn):
    for c in range(n_rows // ATTN_ROWS):
        fn(c * ATTN_ROWS)


def _causal(r0, tq, transposed):
    a = lax.broadcasted_iota(jnp.int32, (ATTN_ROWS, tq), 0) + r0
    b = lax.broadcasted_iota(jnp.int32, (ATTN_ROWS, tq), 1)
    return (a <= b) if transposed else (b <= a)


def _rows8(x):
    return jnp.transpose(x)[:8, :]


def _attn_fwd(mode, q, k, v, bias, tq, name):
    t, hd = q.shape
    nh = hd // HEAD_DIM
    nb = t // tq
    wb = MAX_WINDOW // tq
    fox = mode == "fox"

    def body(q_ref, k_ref, v_ref, b_ref, o_ref, lse_ref, lse_row_ref, s_ref, p_ref, m_ref, l_ref, acc_ref):
        qi = pl.program_id(1)
        qb = q_ref[...]
        m_ref[...] = jnp.full_like(m_ref, NEG)
        l_ref[...] = jnp.zeros_like(l_ref)
        acc_ref[...] = jnp.zeros_like(acc_ref)

        def tile(kj, diag):
            off = pl.multiple_of(kj * tq, tq)
            s_ref[...] = _dot(qb, k_ref[pl.ds(off, tq), :], NT)
            if fox:
                brow = b_ref[qi][:, :1] - b_ref[kj]

            def chunk(r0):
                rows = pl.ds(r0, ATTN_ROWS)
                if fox:
                    s = s_ref[rows, :] + brow
                    if diag:
                        s = jnp.where(_causal(r0, tq, False), s, NEG)
                else:
                    s = s_ref[rows, :] + b_ref[qi - kj, rows, :]
                m_old = m_ref[rows, :]
                m_new = jnp.maximum(m_old, jnp.max(s, axis=1, keepdims=True))
                p = jnp.exp(s - _rep(m_new, tq))
                alpha = jnp.exp(m_old - m_new)
                l_ref[rows, :] = alpha * l_ref[rows, :] + jnp.sum(p, axis=1, keepdims=True)
                m_ref[rows, :] = m_new
                acc_ref[rows, :] = alpha * acc_ref[rows, :]
                p_ref[rows, :] = p.astype(BF)

            _chunks(tq, chunk)
            acc_ref[...] += _dot(p_ref[...], v_ref[pl.ds(off, tq), :])

        tile(qi, True)
        if fox:
            lax.fori_loop(0, qi, lambda kj, c: (tile(kj, False), c)[1], 0)
        else:
            lax.fori_loop(1, jnp.minimum(qi, wb) + 1, lambda i, c: (tile(qi - i, False), c)[1], 0)
        o_ref[...] = (acc_ref[...] / l_ref[...]).astype(BF)
        lse = m_ref[...] + jnp.log(l_ref[...])
        lse_ref[...] = lse
        lse_row_ref[...] = _rows8(lse)

    qspec = pl.BlockSpec((tq, HEAD_DIM), lambda h, i: (i, h))
    kvspec = pl.BlockSpec((t, HEAD_DIM), lambda h, i: (0, h))
    repspec = pl.BlockSpec((None, tq, LANE), lambda h, i: (h, i, 0))
    row8spec = pl.BlockSpec((None, None, 8, tq), lambda h, i: (h, i, 0, 0))
    if fox:
        bspec = pl.BlockSpec((None, nb, 1, tq), lambda h, i: (h, 0, 0, 0))
    else:
        bspec = pl.BlockSpec((wb + 1, tq, tq), lambda h, i: (0, 0, 0))
    return pl.pallas_call(
        body, name=name, grid=(nh, nb), in_specs=[qspec, kvspec, kvspec, bspec],
        out_specs=[qspec, repspec, row8spec],
        out_shape=[jax.ShapeDtypeStruct((t, hd), BF), jax.ShapeDtypeStruct((nh, t, LANE), F32),
                   jax.ShapeDtypeStruct((nh, nb, 8, tq), F32)],
        scratch_shapes=[pltpu.VMEM((tq, tq), F32), pltpu.VMEM((tq, tq), BF), pltpu.VMEM((tq, LANE), F32),
                        pltpu.VMEM((tq, LANE), F32), pltpu.VMEM((tq, HEAD_DIM), F32)],
        compiler_params=_params(),
    )(q, k, v, bias)


def _attn_bwd_dq(mode, q, k, v, o, do, lse, bias, tq, name, dep=None):
    t, hd = q.shape
    nh = hd // HEAD_DIM
    nb = t // tq
    wb = MAX_WINDOW // tq
    fox = mode == "fox"

    def body(q_ref, k_ref, v_ref, o_ref, do_ref, lse_ref, b_ref, dq_ref, dl_row_ref,
             s_ref, dp_ref, x_ref, y_ref, acc_ref, acc2_ref, dl_ref):
        qi = pl.program_id(1)
        qb = q_ref[...]
        dob = do_ref[...]
        acc_ref[...] = jnp.zeros_like(acc_ref)
        if fox:
            acc2_ref[...] = jnp.zeros_like(acc2_ref)
            dl_ref[...] = jnp.zeros_like(dl_ref)
        else:
            prod = o_ref[...].astype(F32) * dob.astype(F32)
            dl_ref[...] = jnp.broadcast_to(jnp.sum(prod, axis=1, keepdims=True), (tq, LANE))

        def tile(kj, diag):
            off = pl.multiple_of(kj * tq, tq)
            kb = k_ref[pl.ds(off, tq), :]
            s_ref[...] = _dot(qb, kb, NT)
            dp_ref[...] = _dot(dob, v_ref[pl.ds(off, tq), :], NT)
            if fox:
                brow = b_ref[qi][:, :1] - b_ref[kj]

            def chunk(r0):
                rows = pl.ds(r0, ATTN_ROWS)
                lse_c = _rep(lse_ref[rows, :], tq)
                if fox:
                    s = s_ref[rows, :] + brow
                    if diag:
                        s = jnp.where(_causal(r0, tq, False), s, NEG)
                    p = jnp.exp(s - lse_c)
                    pdp = p * dp_ref[rows, :]
                    dl_ref[rows, :] += jnp.sum(pdp, axis=1, keepdims=True)
                    x_ref[rows, :] = pdp.astype(BF)
                    y_ref[rows, :] = p.astype(BF)
                else:
                    p = jnp.exp(s_ref[rows, :] + b_ref[qi - kj, rows, :] - lse_c)
                    x_ref[rows, :] = (p * (dp_ref[rows, :] - _rep(dl_ref[rows, :], tq))).astype(BF)

            _chunks(tq, chunk)
            acc_ref[...] += _dot(x_ref[...], kb)
            if fox:
                acc2_ref[...] += _dot(y_ref[...], kb)

        tile(qi, True)
        if fox:
            lax.fori_loop(0, qi, lambda kj, c: (tile(kj, False), c)[1], 0)
            dq_ref[...] = acc_ref[...] - dl_ref[...] * acc2_ref[...]
        else:
            lax.fori_loop(1, jnp.minimum(qi, wb) + 1, lambda i, c: (tile(qi - i, False), c)[1], 0)
            dq_ref[...] = acc_ref[...]
        dl_row_ref[...] = _rows8(dl_ref[...])

    qspec = pl.BlockSpec((tq, HEAD_DIM), lambda h, i: (i, h))
    kvspec = pl.BlockSpec((t, HEAD_DIM), lambda h, i: (0, h))
    repspec = pl.BlockSpec((None, tq, LANE), lambda h, i: (h, i, 0))
    row8spec = pl.BlockSpec((None, None, 8, tq), lambda h, i: (h, i, 0, 0))
    if fox:
        bspec = pl.BlockSpec((None, nb, 1, tq), lambda h, i: (h, 0, 0, 0))
    else:
        bspec = pl.BlockSpec((wb + 1, tq, tq), lambda h, i: (0, 0, 0))
    return _call(
        body, [q, k, v, o, do, lse, bias], dep=dep, name=name, grid=(nh, nb),
        in_specs=[qspec, kvspec, kvspec, qspec, qspec, repspec, bspec],
        out_specs=[qspec, row8spec],
        out_shape=[jax.ShapeDtypeStruct((t, hd), F32), jax.ShapeDtypeStruct((nh, nb, 8, tq), F32)],
        scratch_shapes=[pltpu.VMEM((tq, tq), F32), pltpu.VMEM((tq, tq), F32), pltpu.VMEM((tq, tq), BF),
                        pltpu.VMEM((tq, tq), BF), pltpu.VMEM((tq, HEAD_DIM), F32),
                        pltpu.VMEM((tq, HEAD_DIM), F32), pltpu.VMEM((tq, LANE), F32)],
        compiler_params=_params(),
    )


def _attn_bwd_dkv(mode, q, k, v, do, lse_row, dl_row, bias_t, c_row, tq, name):
    t, hd = q.shape
    nh = hd // HEAD_DIM
    nb = t // tq
    wb = MAX_WINDOW // tq
    fox = mode == "fox"

    def body(*refs):
        if fox:
            (q_ref, k_ref, v_ref, do_ref, lse_ref, dl_ref, b_ref, cq_ref, dk_ref, dv_ref, dc_row_ref,
             s_ref, dp_ref, x_ref, y_ref, dc_ref) = refs
        else:
            q_ref, k_ref, v_ref, do_ref, lse_ref, dl_ref, b_ref, dk_ref, dv_ref, s_ref, dp_ref, x_ref, y_ref = refs
        kj = pl.program_id(1)
        kb = k_ref[...]
        vb = v_ref[...]
        dk_ref[...] = jnp.zeros_like(dk_ref)
        dv_ref[...] = jnp.zeros_like(dv_ref)
        if fox:
            dc_ref[...] = jnp.zeros_like(dc_ref)

        def tile(qi, diag):
            off = pl.multiple_of(qi * tq, tq)
            qb = q_ref[pl.ds(off, tq), :]
            dob = do_ref[pl.ds(off, tq), :]
            s_ref[...] = _dot(kb, qb, NT)
            dp_ref[...] = _dot(vb, dob, NT)
            lse_r = lse_ref[qi, 0:1, :]
            dl_r = dl_ref[qi, 0:1, :]
            if fox:
                kbias = cq_ref[qi][:, :1] - b_ref[...]

            def chunk(r0):
                rows = pl.ds(r0, ATTN_ROWS)
                if fox:
                    s = s_ref[rows, :] + _rep(kbias[r0:r0 + ATTN_ROWS, :], tq)
                    if diag:
                        s = jnp.where(_causal(r0, tq, True), s, NEG)
                else:
                    s = s_ref[rows, :] + b_ref[qi - kj, rows, :]
                pt = jnp.exp(s - lse_r)
                dst = pt * (dp_ref[rows, :] - dl_r)
                x_ref[rows, :] = pt.astype(BF)
                y_ref[rows, :] = dst.astype(BF)
                if fox:
                    dc_ref[rows, :] -= jnp.sum(dst, axis=1, keepdims=True)

            _chunks(tq, chunk)
            dv_ref[...] += _dot(x_ref[...], dob)
            dk_ref[...] += _dot(y_ref[...], qb)

        tile(kj, True)
        hi = nb if fox else jnp.minimum(kj + wb + 1, nb)
        lax.fori_loop(kj + 1, hi, lambda qi, c: (tile(qi, False), c)[1], 0)
        if fox:
            dc_row_ref[...] = _rows8(dc_ref[...])

    blkspec = pl.BlockSpec((tq, HEAD_DIM), lambda h, j: (j, h))
    fullspec = pl.BlockSpec((t, HEAD_DIM), lambda h, j: (0, h))
    rows8spec = pl.BlockSpec((None, nb, 8, tq), lambda h, j: (h, 0, 0, 0))
    repspec = pl.BlockSpec((None, tq, LANE), lambda h, j: (h, j, 0))
    in_specs = [fullspec, blkspec, blkspec, fullspec, rows8spec, rows8spec]
    args = [q, k, v, do, lse_row, dl_row, bias_t]
    out_specs = [blkspec, blkspec]
    out_shape = [jax.ShapeDtypeStruct((t, hd), F32), jax.ShapeDtypeStruct((t, hd), F32)]
    scratch = [pltpu.VMEM((tq, tq), F32), pltpu.VMEM((tq, tq), F32), pltpu.VMEM((tq, tq), BF),
               pltpu.VMEM((tq, tq), BF)]
    if fox:
        in_specs += [repspec, pl.BlockSpec((None, nb, 1, tq), lambda h, j: (h, 0, 0, 0))]
        args.append(c_row)
        out_specs.append(pl.BlockSpec((None, None, 8, tq), lambda h, j: (h, j, 0, 0)))
        out_shape.append(jax.ShapeDtypeStruct((nh, nb, 8, tq), F32))
        scratch.append(pltpu.VMEM((tq, LANE), F32))
    else:
        in_specs.append(pl.BlockSpec((wb + 1, tq, tq), lambda h, j: (0, 0, 0)))
    return pl.pallas_call(
        body, name=name, grid=(nh, nb), in_specs=in_specs, out_specs=out_specs, out_shape=out_shape,
        scratch_shapes=scratch, compiler_params=_params(),
    )(*args)


def _gate_specs(t, d, hd, tr):
    row = pl.BlockSpec((tr, d), lambda i: (i, 0))
    vec = pl.BlockSpec((1, d), lambda i: (0, 0))
    base = 6 * hd // d
    gd = pl.BlockSpec((tr, d), lambda i: (i, base))
    gf = pl.BlockSpec((tr, d), lambda i: (i, base + 1))
    return row, vec, gd, gf


def _proj_merge(yd, yf, wpd, wpf, proj, b_d, b_f, hd):
    t = yd.shape[0]
    d = wpd.shape[1]
    tr = _tile(t, 256, 16)
    row, vec, gd, gf = _gate_specs(t, d, hd, tr)

    def body(yd_ref, yf_ref, wd_ref, wf_ref, gd_ref, gf_ref, bd_ref, bf_ref, pd_ref, pf_ref, o_ref):
        pd = _dot(yd_ref[...], wd_ref[...])
        pf = _dot(yf_ref[...], wf_ref[...])
        pd_ref[...] = pd
        pf_ref[...] = pf
        o_ref[...] = (_sig(gd_ref[...] + bd_ref[...]) * pd + _sig(gf_ref[...] + bf_ref[...]) * pf).astype(BF)

    yspec = pl.BlockSpec((tr, hd), lambda i: (i, 0))
    wspec = pl.BlockSpec((hd, d), lambda i: (0, 0))
    f32 = jax.ShapeDtypeStruct((t, d), F32)
    return pl.pallas_call(
        body, name="proj_merge", grid=(t // tr,), in_specs=[yspec, yspec, wspec, wspec, gd, gf, vec, vec],
        out_specs=[row, row, row], out_shape=[f32, f32, jax.ShapeDtypeStruct((t, d), BF)],
        compiler_params=_params(),
    )(yd, yf, wpd, wpf, proj, proj, b_d, b_f)


def _merge_bwd(dm, pd, pf, proj, b_d, b_f, hd):
    t, d = pd.shape
    tr = _tile(t, 256, 16)
    row, vec, gd, gf = _gate_specs(t, d, hd, tr)

    def body(dm_ref, pd_ref, pf_ref, gd_ref, gf_ref, bd_ref, bf_ref,
             dpd_ref, dpf_ref, dgd_ref, dgf_ref, dbd_ref, dbf_ref):
        dmv = dm_ref[...]
        sd = _sig(gd_ref[...] + bd_ref[...])
        sf = _sig(gf_ref[...] + bf_ref[...])
        dgd = dmv * pd_ref[...] * (sd * (1.0 - sd))
        dgf = dmv * pf_ref[...] * (sf * (1.0 - sf))
        dpd_ref[...] = (dmv * sd).astype(BF)
        dpf_ref[...] = (dmv * sf).astype(BF)
        dgd_ref[...] = dgd.astype(BF)
        dgf_ref[...] = dgf.astype(BF)

        @pl.when(pl.program_id(0) == 0)
        def _():
            dbd_ref[...] = jnp.zeros_like(dbd_ref)
            dbf_ref[...] = jnp.zeros_like(dbf_ref)

        dbd_ref[...] += jnp.sum(dgd, axis=0, keepdims=True)
        dbf_ref[...] += jnp.sum(dgf, axis=0, keepdims=True)

    ob = jax.ShapeDtypeStruct((t, d), BF)
    ov = jax.ShapeDtypeStruct((1, d), F32)
    return pl.pallas_call(
        body, name="merge_bwd", grid=(t // tr,), in_specs=[row, row, row, gd, gf, vec, vec],
        out_specs=[row, row, row, row, vec, vec], out_shape=[ob, ob, ob, ob, ov, ov],
        compiler_params=_params(),
    )(dm, pd, pf, proj, proj, b_d, b_f)


def _assemble_dproj(dqd, dkd, dvd, dqf, dkf, dvf, dgd, dgf, dlogf, proj, tables, bf_pad, scale):
    t, np_ = proj.shape
    hd = dqd.shape[1]
    d = dgd.shape[1]
    nh = hd // HEAD_DIM
    tr = _tile(t, 256, 16)
    f_blk = np_ // LANE - 1

    def body(dqd_ref, dkd_ref, dvd_ref, dqf_ref, dkf_ref, dvf_ref, dgd_ref, dgf_ref, dlog_ref, fl_ref,
             c_ref, s1_ref, s2_ref, b_ref, o_ref, db_ref):
        c, s1, s2 = c_ref[...], s1_ref[...], s2_ref[...]
        for h in range(nh):
            sl = slice(h * HEAD_DIM, (h + 1) * HEAD_DIM)
            o_ref[:, sl] = (_rope_t(dqd_ref[:, sl], c, s1, s2) * scale).astype(BF)
            o_ref[:, hd + h * HEAD_DIM:hd + (h + 1) * HEAD_DIM] = _rope_t(dkd_ref[:, sl], c, s1, s2).astype(BF)
        o_ref[:, 2 * hd:3 * hd] = dvd_ref[...].astype(BF)
        o_ref[:, 3 * hd:4 * hd] = (dqf_ref[...] * scale).astype(BF)
        o_ref[:, 4 * hd:5 * hd] = dkf_ref[...].astype(BF)
        o_ref[:, 5 * hd:6 * hd] = dvf_ref[...].astype(BF)
        o_ref[:, 6 * hd:6 * hd + d] = dgd_ref[...]
        o_ref[:, 6 * hd + d:6 * hd + 2 * d] = dgf_ref[...]
        z = fl_ref[...] + b_ref[...]
        dfl = dlog_ref[...] * _sig(-z)
        o_ref[:, 6 * hd + 2 * d:] = dfl.astype(BF)

        @pl.when(pl.program_id(0) == 0)
        def _():
            db_ref[...] = jnp.zeros_like(db_ref)

        db_ref[...] += jnp.sum(dfl, axis=0, keepdims=True)

    head = pl.BlockSpec((tr, hd), lambda i: (i, 0))
    row = pl.BlockSpec((tr, d), lambda i: (i, 0))
    lane_row = pl.BlockSpec((tr, LANE), lambda i: (i, 0))
    lane_vec = pl.BlockSpec((1, LANE), lambda i: (0, 0))
    return pl.pallas_call(
        body, name="assemble_dproj", grid=(t // tr,),
        in_specs=[head] * 6 + [row, row, lane_row, pl.BlockSpec((tr, LANE), lambda i: (i, f_blk)),
                               lane_row, lane_row, lane_row, lane_vec],
        out_specs=[pl.BlockSpec((tr, np_), lambda i: (i, 0)), lane_vec],
        out_shape=[jax.ShapeDtypeStruct((t, np_), BF), jax.ShapeDtypeStruct((1, LANE), F32)],
        compiler_params=_params(),
    )(dqd, dkd, dvd, dqf, dkf, dvf, dgd, dgf, dlogf, proj, *tables, bf_pad)


def _to_rows(a, tq):
    h, t = a.shape
    return a.reshape(h, t // tq, 1, tq)


def kernel(x, ffn1_norm, ffn1_w_gate, ffn1_w_up, ffn1_w_down, mix_norm, w_in, b_forget, b_gate_dil, b_gate_fox, w_proj_dil, w_proj_fox, w_out, ffn2_norm, ffn2_w_gate, ffn2_w_up, ffn2_w_down, final_norm, loss_target, m_ffn1_norm, m_ffn1_w_gate, m_ffn1_w_up, m_ffn1_w_down, m_mix_norm, m_w_in, m_b_forget, m_b_gate_dil, m_b_gate_fox, m_w_proj_dil, m_w_proj_fox, m_w_out, m_ffn2_norm, m_ffn2_w_gate, m_ffn2_w_up, m_ffn2_w_down, m_final_norm, v_ffn1_norm, v_ffn1_w_gate, v_ffn1_w_up, v_ffn1_w_down, v_mix_norm, v_w_in, v_b_forget, v_b_gate_dil, v_b_gate_fox, v_w_proj_dil, v_w_proj_fox, v_w_out, v_ffn2_norm, v_ffn2_w_gate, v_ffn2_w_up, v_ffn2_w_down, v_final_norm):
    t, d = x.shape[1], x.shape[2]
    hd = w_proj_dil.shape[1]
    nh = hd // HEAD_DIM
    n_f = b_forget.shape[1]
    cols = w_in.shape[2]
    in_cols = N_DEV * cols
    assert in_cols == 6 * hd + n_f + 2 * d and n_f == nh and n_f <= LANE
    np_ = 6 * hd + 2 * d + LANE
    scale = HEAD_DIM ** -0.5
    tq = _tile(t, 512, LANE)
    assert MAX_WINDOW % tq == 0 and tq % 16 == 0

    x2d = x[0]
    tgt = loss_target[0]

    ag_order = [ffn1_w_gate, ffn1_w_up, ffn1_w_down, w_in, w_proj_dil, w_proj_fox, w_out,
                ffn2_w_gate, ffn2_w_up, ffn2_w_down]
    ag_first, tok = _exchange_start([w[0].astype(BF) for w in ag_order[:2]], True, "ag_start_first", ks=FIRST_LEVEL)
    ag_rest, ag_token = _exchange_start([w[0].astype(BF) for w in ag_order[2:]], True, "ag_start", dep=tok,
                                        ks=FIRST_LEVEL)
    ag = ag_first + ag_rest

    def relay(idx, after, name):
        for i, h in zip(idx, _gather_relay([ag[i] for i in idx], after, name)):
            ag[i] = h

    def gathered(idx, after, name):
        return _gather_wait([ag[i] for i in idx], after, name)

    tables = _rope_tables(t)
    bf_pad = jnp.pad(b_forget, ((0, 0), (0, LANE - n_f)))

    hn1, hn1_t = _rms_fwd(x2d, ffn1_norm, "rms_ffn1", dep=ag_token)
    relay([0], hn1, "ag_relay_ffn1_gate")
    wg1, = gathered([0], hn1, "ag_wait_ffn1_gate")
    g1 = _ffn_gate(hn1, wg1, "ffn1_gate")
    relay([1], g1, "ag_relay_ffn1_up")
    wu1, = gathered([1], g1, "ag_wait_ffn1_up")
    relay([2], wu1, "ag_relay_ffn1_down")
    u1, a1 = _ffn_up_act(hn1, wu1, g1, "ffn1_up_act")
    wd1, = gathered([2], a1, "ag_wait_ffn1_down")
    relay([3], wd1, "ag_relay_w_in")
    x1 = _ffn_down(a1, wd1, x2d, "ffn1_down")

    hm, hm_t = _rms_fwd(x1, mix_norm, "rms_mix")
    win_g, = gathered([3], hm, "ag_wait_w_in")
    relay([4, 5, 6], win_g, "ag_relay_mixer")
    segments = [(0, 6 * hd), (6 * hd + n_f, in_cols), (6 * hd, 6 * hd + n_f)]
    pieces = []
    for lo, hi in segments:
        for j in range(lo // cols, (hi - 1) // cols + 1):
            s, e = max(lo, j * cols), min(hi, (j + 1) * cols)
            pieces.append(win_g[j, :, s - j * cols:e - j * cols])
    win_p = jnp.concatenate(pieces + [jnp.zeros((d, LANE - n_f), BF)], axis=1)
    proj = _mm_nn(hm, win_p, F32, "w_in_fwd")
    qd, kd, vd, qf, kf, vf, logf = _mixer_prep(proj, tables, bf_pad, hd, scale)
    csum = _cumsum_rows(logf, False, "cumsum_logf")
    c_heads = csum[:, :nh].T
    c_row = _to_rows(c_heads, tq)
    c_rep = jnp.broadcast_to(c_heads[:, :, None], (nh, t, LANE))
    dil_bias = _dil_bias_tiles(tq)
    dil_bias_t = dil_bias.transpose(0, 2, 1)
    relay([7, 8, 9], qd, "ag_relay_ffn2")
    yd, lse_d, lse_d_row = _attn_fwd("dil", qd, kd, vd, dil_bias, tq, "attn_dil_fwd")
    yf, lse_f, lse_f_row = _attn_fwd("fox", qf, kf, vf, c_row, tq, "attn_fox_fwd")
    wpd_g, wpf_g = gathered([4, 5], yf, "ag_wait_proj")
    wpd = wpd_g.transpose(1, 0, 2).reshape(hd, d)
    wpf = wpf_g.transpose(1, 0, 2).reshape(hd, d)
    pd, pf, merged = _proj_merge(yd, yf, wpd, wpf, proj, b_gate_dil, b_gate_fox, hd)
    wout_g, = gathered([6], merged, "ag_wait_w_out")
    wout = wout_g.reshape(d, d)
    x2 = _mm_nn(merged, wout, F32, "w_out_fwd", residual=x1, tn_pref=1024)

    hn2, hn2_t = _rms_fwd(x2, ffn2_norm, "rms_ffn2")
    wg2, wu2 = gathered([7, 8], hn2, "ag_wait_ffn2_gate_up")
    g2, u2, a2 = _ffn_gate_up(hn2, wg2, wu2, "ffn2_gate_up")
    wd2, = gathered([9], a2, "ag_wait_ffn2_down")
    x3 = _ffn_down(a2, wd2, x2, "ffn2_down")

    dx3, dx3b, d_final, loss_lanes = _loss_head(x3, final_norm.reshape(1, d), tgt)

    def ffn_bwd(dxb, hn_t, g, u, a, wg, wu, wd, x_in, gain, dres, tag):
        dg, du = _ffn_bwd_hidden(dxb, wd, g, u, tag + "_bwd_hidden")
        dwd = _ffn_dw_down(a, dxb, tag + "_dw_down")
        rs_down, tok = _exchange_start([dwd], False, "rs_start_" + tag + "_down")
        dwg, dwu = _ffn_dw_gate_up(hn_t, dg, du, tag + "_dw_gate_up", dep=tok)
        rs_gu, tok = _exchange_start([dwg, dwu], False, "rs_start_" + tag + "_gate_up")
        dhn = _ffn_bwd_input(dg, du, wg, wu, tag + "_bwd_input", dep=tok)
        dx, dx_bf, dgain = _rms_bwd(dhn, x_in, gain, dres, "rms_" + tag + "_bwd")
        return dx, dx_bf, dgain, rs_gu + rs_down

    dx2, dx2b, d_ffn2_norm, rs_ffn2 = ffn_bwd(dx3b, hn2_t, g2, u2, a2, wg2, wu2, wd2, x2, ffn2_norm, dx3, "ffn2")

    dmerged = _mm_nt(dx2b, wout, F32, "w_out_bwd")
    dwout = _mm_tn(merged, dx2b, BF, "w_out_dw", tn_pref=1024)
    dpd, dpf, dgd, dgf, d_bd, d_bf = _merge_bwd(dmerged, pd, pf, proj, b_gate_dil, b_gate_fox, hd)
    dyd = _mm_nt(dpd, wpd, BF, "proj_dil_bwd")
    dyf = _mm_nt(dpf, wpf, BF, "proj_fox_bwd")
    dwpd = _mm_tn(yd, dpd, BF, "proj_dil_dw", tn_pref=1024)
    dwpf = _mm_tn(yf, dpf, BF, "proj_fox_dw", tn_pref=1024)
    dwpd_c = dwpd.reshape(hd, N_DEV, d // N_DEV).transpose(1, 0, 2)
    dwpf_c = dwpf.reshape(hd, N_DEV, d // N_DEV).transpose(1, 0, 2)
    dwout_c = dwout.reshape(N_DEV, d // N_DEV, d)
    rs_mix, tok = _exchange_start([dwout_c, dwpd_c, dwpf_c], False, "rs_start_mixer")

    dqd, dl_d = _attn_bwd_dq("dil", qd, kd, vd, yd, dyd, lse_d, dil_bias, tq, "attn_dil_dq", dep=tok)
    dkd, dvd = _attn_bwd_dkv("dil", qd, kd, vd, dyd, lse_d_row, dl_d, dil_bias_t, None, tq, "attn_dil_dkv")
    dqf, dl_f = _attn_bwd_dq("fox", qf, kf, vf, yf, dyf, lse_f, c_row, tq, "attn_fox_dq")
    dkf, dvf, dc = _attn_bwd_dkv("fox", qf, kf, vf, dyf, lse_f_row, dl_f, c_rep, c_row, tq, "attn_fox_dkv")
    dc_pad = jnp.pad(dc[:, :, 0, :].reshape(nh, t).T, ((0, 0), (0, LANE - nh)))
    dlogf = _cumsum_rows(dc_pad, True, "revcumsum_dc")
    dproj, d_bforget = _assemble_dproj(dqd, dkd, dvd, dqf, dkf, dvf, dgd, dgf, dlogf, proj, tables, bf_pad, scale)

    dwin_p = _mm_tn(hm_t, dproj, BF, "w_in_dw", tk_pref=DW_ROWS, a_transposed=True)
    def perm_col(c):
        if c < 6 * hd:
            return c
        return c + 2 * d if c < 6 * hd + n_f else c - n_f

    shards = []
    for j in range(N_DEV):
        cuts = sorted({j * cols, (j + 1) * cols} | {c for c in (6 * hd, 6 * hd + n_f) if j * cols < c < (j + 1) * cols})
        shards.append(jnp.concatenate([dwin_p[:, perm_col(lo):perm_col(lo) + hi - lo]
                                       for lo, hi in zip(cuts[:-1], cuts[1:])], axis=1))
    dwin_c = jnp.stack(shards)
    rs_win, tok = _exchange_start([dwin_c], False, "rs_start_w_in")
    dx1, dx1b, d_mix_norm = _mm_nt(dproj, win_p, F32, "w_in_bwd", tn_pref=d, tk_pref=1152,
                                   rms=(x1, mix_norm, dx2), dep=tok)

    grad_x, _, d_ffn1_norm, rs_ffn1 = ffn_bwd(dx1b, hn1_t, g1, u1, a1, wg1, wu1, wd1, x2d, ffn1_norm, dx1, "ffn1")

    def update(handles, names, after, tag):
        recvs = _exchange_wait(handles, False, after, "rs_wait_" + tag)
        res = {}
        for recv, n in zip(recvs, names):
            w, m, v = wmv[n]
            g, delta, m2, v2 = _adam_from_partials(recv, w[0], m[0], v[0], "adam_" + n)
            res[n] = (g[None], delta[None], m2[None], v2[None])
        return res, g

    wmv = {
        "ffn1_w_gate": (ffn1_w_gate, m_ffn1_w_gate, v_ffn1_w_gate),
        "ffn1_w_up": (ffn1_w_up, m_ffn1_w_up, v_ffn1_w_up),
        "ffn1_w_down": (ffn1_w_down, m_ffn1_w_down, v_ffn1_w_down),
        "w_in": (w_in, m_w_in, v_w_in),
        "w_proj_dil": (w_proj_dil, m_w_proj_dil, v_w_proj_dil),
        "w_proj_fox": (w_proj_fox, m_w_proj_fox, v_w_proj_fox),
        "w_out": (w_out, m_w_out, v_w_out),
        "ffn2_w_gate": (ffn2_w_gate, m_ffn2_w_gate, v_ffn2_w_gate),
        "ffn2_w_up": (ffn2_w_up, m_ffn2_w_up, v_ffn2_w_up),
        "ffn2_w_down": (ffn2_w_down, m_ffn2_w_down, v_ffn2_w_down),
    }
    big = {}
    after = grad_x
    for handles, names, tag in [
            (rs_ffn2, ["ffn2_w_gate", "ffn2_w_up", "ffn2_w_down"], "ffn2"),
            (rs_mix, ["w_out", "w_proj_dil", "w_proj_fox"], "mixer"),
            (rs_win, ["w_in"], "w_in"),
            (rs_ffn1, ["ffn1_w_gate", "ffn1_w_up", "ffn1_w_down"], "ffn1")]:
        res, after = update(handles, names, after, tag)
        big.update(res)

    def lanes(a):
        a = a.reshape(1, -1)
        return jnp.pad(a, ((0, 0), (0, d - a.shape[1])))

    small_names = ["ffn1_norm", "mix_norm", "b_gate_dil", "b_gate_fox", "ffn2_norm", "final_norm", "b_forget"]
    small_g = [d_ffn1_norm, d_mix_norm, d_bd, d_bf, d_ffn2_norm, d_final, d_bforget[:, :n_f]]
    small_w = [ffn1_norm, mix_norm, b_gate_dil, b_gate_fox, ffn2_norm, final_norm, b_forget]
    small_m = [m_ffn1_norm, m_mix_norm, m_b_gate_dil, m_b_gate_fox, m_ffn2_norm, m_final_norm, m_b_forget]
    small_v = [v_ffn1_norm, v_mix_norm, v_b_gate_dil, v_b_gate_fox, v_ffn2_norm, v_final_norm, v_b_forget]
    pack = lambda arrs, last: jnp.concatenate([lanes(a) for a in arrs] + [last], axis=0)
    g_all = _allreduce_small(pack(small_g, loss_lanes))
    zero_row = jnp.zeros((1, d), F32)
    one_row = jnp.ones((1, d), F32)
    s_delta, s_m, s_v = _adam_small(g_all, pack(small_w, zero_row), pack(small_m, zero_row), pack(small_v, one_row))
    loss = g_all[len(small_names), 0]

    def unpack(packed, i, like):
        return packed[i, :like.size].reshape(like.shape)

    small = {}
    for i, (n, w) in enumerate(zip(small_names, small_w)):
        small[n] = (unpack(g_all, i, w), unpack(s_delta, i, w), unpack(s_m, i, w), unpack(s_v, i, w))

    order = ["ffn1_norm", "ffn1_w_gate", "ffn1_w_up", "ffn1_w_down", "mix_norm", "w_in", "b_forget", "b_gate_dil",
             "b_gate_fox", "w_proj_dil", "w_proj_fox", "w_out", "ffn2_norm", "ffn2_w_gate", "ffn2_w_up",
             "ffn2_w_down", "final_norm"]
    res = {**big, **small}
    outs = [loss, grad_x[None]]
    for slot in range(4):
        outs += [res[n][slot] for n in order]
    return tuple(outs)
```

```python
import functools

import numpy as np
import jax
import jax.numpy as jnp
from jax import lax
from jax.experimental import pallas as pl
from jax.experimental.pallas import tpu as pltpu

BF = jnp.bfloat16
F32 = jnp.float32
MESH = pl.DeviceIdType.MESH
N_DEV = 8

HEAD_DIM = 128
ROPE_DIM = HEAD_DIM // 4
ROPE_HALF = ROPE_DIM // 2
ROPE_THETA = 500000.0
NORM_EPS = 1e-6
DIL_PATTERNS = ((128, 1), (512, 4), (2048, 16))
MAX_WINDOW = 2048
LANE = 128
NEG = -1e30

ADAM_LR = 0.001
ADAM_B1 = 0.9
ADAM_B2 = 0.999
ADAM_EPS = 1e-08
ADAM_WD = 0.01
ADAM_STEP = 10

VMEM_LIMIT_BYTES = 56 * 1024 * 1024
FFN_ROWS = 1024
DW_ROWS = 1024
ANY = pl.BlockSpec(memory_space=pl.ANY)

NN = (((1,), (0,)), ((), ()))
NT = (((1,), (1,)), ((), ()))
TN = (((0,), (0,)), ((), ()))


def _dot(a, b, dn=NN):
    return lax.dot_general(a, b, dn, preferred_element_type=F32)


def _sig(x):
    return 1.0 / (1.0 + jnp.exp(-x))


def _tile(n, pref, align):
    best = None
    t = align
    while t <= min(n, pref):
        if n % t == 0:
            best = t
        t += align
    return n if best is None else best


def _params():
    return pltpu.CompilerParams(vmem_limit_bytes=VMEM_LIMIT_BYTES)


def _call(body, args, dep=None, **kw):
    if dep is not None:
        n_in = len(args)
        inner = body

        def body(*refs):
            inner(*refs[:n_in], *refs[n_in + 1:])

        kw["in_specs"] = list(kw["in_specs"]) + [ANY]
        args = list(args) + [dep]
    return pl.pallas_call(body, **kw)(*args)


def _peers():
    x, y, c = lax.axis_index("x"), lax.axis_index("y"), lax.axis_index("c")
    me = 4 * x + 2 * y + c
    peers = []
    for k in range(1, N_DEV):
        px = 1 - x if (k >> 2) & 1 else x
        py = 1 - y if (k >> 1) & 1 else y
        pc = 1 - c if k & 1 else c
        peers.append((k, (px, py, pc), 4 * px + 2 * py + pc))
    return me, peers


HBM = pl.BlockSpec(memory_space=pltpu.HBM)
SEM = pl.BlockSpec(memory_space=pltpu.SEMAPHORE)
EFFECT = pltpu.SideEffectType.DATAFLOW_SIDE_EFFECTING


def _exchange_copy(gather, src_ref, land_ref, send_sems, recv_sems, me, k, peer, peer_flat, landing):
    return pltpu.make_async_remote_copy(
        src_ref=src_ref if gather else src_ref.at[peer_flat], dst_ref=land_ref.at[landing],
        send_sem=send_sems.at[k], recv_sem=recv_sems.at[k], device_id=peer, device_id_type=MESH)


ALL_PEERS = (1, 2, 3, 4, 5, 6, 7)
SIBLING = 1
SAME_CORE = (2, 4, 6)
FIRST_LEVEL = (SIBLING,) + SAME_CORE


def _exchange_start(srcs, gather, name, dep=None, ks=ALL_PEERS):
    n = len(srcs)
    extra = [] if dep is None else [dep]

    def body(*refs):
        src_refs, land_refs = refs[:n], refs[n:2 * n]
        refs = refs[2 * n + len(extra):]
        send_refs, recv_refs = refs[:n], refs[n:2 * n]
        token = refs[4 * n]
        me, peers = _peers()
        for i in range(n):
            for k, peer, peer_flat in peers:
                if k in ks:
                    _exchange_copy(gather, src_refs[i], land_refs[i], send_refs[i], recv_refs[i],
                                   me, k, peer, peer_flat, me).start()
        token[...] = jnp.zeros_like(token)

    lands = [lax.empty((N_DEV,) + s.shape[-2:], s.dtype) for s in srcs]
    sems = [pltpu.SemaphoreType.DMA((N_DEV,)) for _ in range(2 * n)]
    out = pl.pallas_call(
        body, name=name,
        out_shape=tuple(sems) + tuple(pltpu.HBM(a.shape, a.dtype) for a in list(srcs) + lands)
        + (jax.ShapeDtypeStruct((8, LANE), F32),),
        in_specs=[HBM] * (2 * n) + [ANY] * len(extra),
        out_specs=tuple([SEM] * (2 * n) + [HBM] * (2 * n) + [pl.BlockSpec(memory_space=pltpu.VMEM)]),
        input_output_aliases={i: 2 * n + i for i in range(2 * n)},
        compiler_params=pltpu.CompilerParams(has_side_effects=EFFECT),
    )(*[pltpu.with_memory_space_constraint(a, pltpu.HBM) for a in list(srcs) + lands], *extra)
    handles = [(out[2 * n + i], out[3 * n + i], out[i], out[n + i]) for i in range(n)]
    return handles, out[4 * n]


def _exchange_wait(handles, gather, after, name):
    n = len(handles)

    def body(*refs):
        src_refs, land_refs = refs[:n], refs[n:2 * n]
        send_refs, recv_refs = refs[2 * n:3 * n], refs[3 * n:4 * n]
        me, peers = _peers()
        for i in range(n):
            for k, peer, peer_flat in peers:
                cp = _exchange_copy(gather, src_refs[i], land_refs[i], send_refs[i], recv_refs[i],
                                    me, k, peer, peer_flat, peer_flat)
                cp.wait_send()
                cp.wait_recv()

    srcs = [h[0] for h in handles]
    lands = [h[1] for h in handles]
    out = pl.pallas_call(
        body, name=name,
        out_shape=tuple(pltpu.HBM(a.shape, a.dtype) for a in srcs + lands),
        in_specs=[HBM] * (2 * n) + [SEM] * (2 * n) + [ANY],
        out_specs=tuple([HBM] * (2 * n)),
        input_output_aliases={i: i for i in range(2 * n)},
        compiler_params=pltpu.CompilerParams(has_side_effects=EFFECT),
    )(*srcs, *lands, *[h[2] for h in handles], *[h[3] for h in handles], after)
    me = 4 * lax.axis_index("x") + 2 * lax.axis_index("y") + lax.axis_index("c")
    filled = []
    for src, land in zip(out[:n], out[n:]):
        own = src[None] if gather else lax.dynamic_slice_in_dim(src, me, 1, axis=0)
        filled.append(lax.dynamic_update_slice_in_dim(land, own, me, axis=0))
    return filled


def _gather_relay(handles, after, name):
    n = len(handles)

    def body(*refs):
        land_refs, recv_refs = refs[:n], refs[n:2 * n]
        refs = refs[2 * n + 1:]
        send2_refs, recv2_refs = refs[n:2 * n], refs[2 * n:3 * n]
        me, peers = _peers()
        sibling = peers[SIBLING - 1][1]
        for i in range(n):
            for k, peer, peer_flat in peers:
                if k in SAME_CORE:
                    block = land_refs[i].at[peer_flat]
                    pltpu.make_async_remote_copy(
                        src_ref=block, dst_ref=block, send_sem=send2_refs[i].at[k], recv_sem=recv_refs[i].at[k],
                        device_id=peer, device_id_type=MESH).wait_recv()
                    pltpu.make_async_remote_copy(
                        src_ref=block, dst_ref=block, send_sem=send2_refs[i].at[k], recv_sem=recv2_refs[i].at[k],
                        device_id=sibling, device_id_type=MESH).start()

    lands = [h[1] for h in handles]
    sems = [pltpu.SemaphoreType.DMA((N_DEV,)) for _ in range(2 * n)]
    out = pl.pallas_call(
        body, name=name,
        out_shape=tuple(pltpu.HBM(a.shape, a.dtype) for a in lands) + tuple(sems),
        in_specs=[HBM] * n + [SEM] * n + [ANY],
        out_specs=tuple([HBM] * n + [SEM] * (2 * n)),
        input_output_aliases={i: i for i in range(n)},
        compiler_params=pltpu.CompilerParams(has_side_effects=EFFECT),
    )(*lands, *[h[3] for h in handles], after)
    return [(h[0], out[i], h[2], h[3], out[n + i], out[2 * n + i]) for i, h in enumerate(handles)]


def _gather_wait(handles, after, name):
    n = len(handles)

    def body(*refs):
        src_refs, land_refs = refs[:n], refs[n:2 * n]
        send_refs, recv_refs = refs[2 * n:3 * n], refs[3 * n:4 * n]
        send2_refs, recv2_refs = refs[4 * n:5 * n], refs[5 * n:6 * n]
        me, peers = _peers()
        _, sibling, sibling_flat = peers[SIBLING - 1]
        for i in range(n):
            for k, peer, peer_flat in peers:
                if k in FIRST_LEVEL:
                    cp = _exchange_copy(True, src_refs[i], land_refs[i], send_refs[i], recv_refs[i],
                                        me, k, peer, peer_flat, peer_flat)
                    cp.wait_send()
                    if k == SIBLING:
                        cp.wait_recv()
                if k in SAME_CORE:
                    mine = land_refs[i].at[peer_flat]
                    theirs = land_refs[i].at[peer_flat ^ SIBLING]
                    cp = pltpu.make_async_remote_copy(
                        src_ref=mine, dst_ref=theirs, send_sem=send2_refs[i].at[k], recv_sem=recv2_refs[i].at[k],
                        device_id=sibling, device_id_type=MESH)
                    cp.wait_send()
                    cp.wait_recv()

    srcs = [h[0] for h in handles]
    lands = [h[1] for h in handles]
    out = pl.pallas_call(
        body, name=name,
        out_shape=tuple(pltpu.HBM(a.shape, a.dtype) for a in srcs + lands),
        in_specs=[HBM] * (2 * n) + [SEM] * (4 * n) + [ANY],
        out_specs=tuple([HBM] * (2 * n)),
        input_output_aliases={i: i for i in range(2 * n)},
        compiler_params=pltpu.CompilerParams(has_side_effects=EFFECT),
    )(*srcs, *lands, *[h[2] for h in handles], *[h[3] for h in handles],
      *[h[4] for h in handles], *[h[5] for h in handles], after)
    me = 4 * lax.axis_index("x") + 2 * lax.axis_index("y") + lax.axis_index("c")
    return [lax.dynamic_update_slice_in_dim(land, src[None], me, axis=0) for src, land in zip(out[:n], out[n:])]


def _allreduce_small(p):
    rows, d = p.shape

    def body(p_ref, o_ref, recv_ref, send_sems, recv_sems):
        me, peers = _peers()
        recv_ref[me] = p_ref[...]
        sends = []
        for k, peer, peer_flat in peers:
            cp = pltpu.make_async_remote_copy(
                src_ref=p_ref, dst_ref=recv_ref.at[me],
                send_sem=send_sems.at[k], recv_sem=recv_sems.at[k],
                device_id=peer, device_id_type=MESH)
            cp.start()
            sends.append(cp)
        for k, peer, peer_flat in peers:
            pltpu.make_async_remote_copy(
                src_ref=p_ref, dst_ref=recv_ref.at[peer_flat],
                send_sem=send_sems.at[k], recv_sem=recv_sems.at[k],
                device_id=peer, device_id_type=MESH).wait_recv()
        for cp in sends:
            cp.wait_send()
        acc = recv_ref[0]
        for s in range(1, N_DEV):
            acc = acc + recv_ref[s]
        is_loss = lax.broadcasted_iota(jnp.int32, (rows, d), 0) == rows - 1
        total = jnp.sum(jnp.where(is_loss, acc, 0.0))
        o_ref[...] = jnp.where(is_loss, total, acc)

    return pl.pallas_call(
        body, name="allreduce_small",
        out_shape=jax.ShapeDtypeStruct((rows, d), F32),
        in_specs=[pl.BlockSpec(memory_space=pltpu.VMEM)],
        out_specs=pl.BlockSpec(memory_space=pltpu.VMEM),
        scratch_shapes=[pltpu.VMEM((N_DEV, rows, d), F32),
                        pltpu.SemaphoreType.DMA((N_DEV,)), pltpu.SemaphoreType.DMA((N_DEV,))],
    )(p)


def _adam_math(w, g, m, v):
    m2 = ADAM_B1 * m + (1.0 - ADAM_B1) * g
    v2 = ADAM_B2 * v + (1.0 - ADAM_B2) * (g * g)
    m_hat = m2 / (1.0 - ADAM_B1 ** ADAM_STEP)
    v_hat = v2 / (1.0 - ADAM_B2 ** ADAM_STEP)
    delta = -ADAM_LR * (m_hat / (jnp.sqrt(v_hat) + ADAM_EPS) + ADAM_WD * w)
    return delta, m2, v2


def _adam_from_partials(parts, w, m, v, name):
    r, c = w.shape
    tr = _tile(r, 256, 16)

    def body(p_ref, w_ref, m_ref, v_ref, g_out, d_out, m_out, v_out):
        g = p_ref[0].astype(F32)
        for s in range(1, N_DEV):
            g = g + p_ref[s].astype(F32)
        delta, m2, v2 = _adam_math(w_ref[...], g, m_ref[...], v_ref[...])
        g_out[...] = g
        d_out[...] = delta
        m_out[...] = m2
        v_out[...] = v2

    blk = pl.BlockSpec((tr, c), lambda i: (i, 0))
    out = jax.ShapeDtypeStruct((r, c), F32)
    return pl.pallas_call(
        body, name=name, grid=(r // tr,),
        in_specs=[pl.BlockSpec((N_DEV, tr, c), lambda i: (0, i, 0)), blk, blk, blk],
        out_specs=[blk, blk, blk, blk], out_shape=[out, out, out, out],
        compiler_params=_params(),
    )(parts, w, m, v)


def _adam_small(g, w, m, v):
    def body(g_ref, w_ref, m_ref, v_ref, d_out, m_out, v_out):
        delta, m2, v2 = _adam_math(w_ref[...], g_ref[...], m_ref[...], v_ref[...])
        d_out[...] = delta
        m_out[...] = m2
        v_out[...] = v2

    out = jax.ShapeDtypeStruct(g.shape, F32)
    return pl.pallas_call(body, name="adam_small", out_shape=[out, out, out])(g, w, m, v)


def _rms_fwd(x, gain, name, dep=None):
    t, d = x.shape
    tr = _tile(t, 256, LANE)

    def body(x_ref, g_ref, o_ref, ot_ref):
        xv = x_ref[...]
        r = lax.rsqrt(jnp.mean(xv * xv, axis=-1, keepdims=True) + NORM_EPS)
        y = xv * r * g_ref[...]
        o_ref[...] = y.astype(BF)
        ot_ref[...] = jnp.transpose(y).astype(BF)

    return _call(
        body, [x, gain], dep=dep, name=name, grid=(t // tr,),
        in_specs=[pl.BlockSpec((tr, d), lambda i: (i, 0)), pl.BlockSpec((1, d), lambda i: (0, 0))],
        out_specs=[pl.BlockSpec((tr, d), lambda i: (i, 0)), pl.BlockSpec((d, tr), lambda i: (0, i))],
        out_shape=[jax.ShapeDtypeStruct((t, d), BF), jax.ShapeDtypeStruct((d, t), BF)],
        compiler_params=_params(),
    )


def _rms_vjp(xv, gain, dy):
    r = lax.rsqrt(jnp.mean(xv * xv, axis=-1, keepdims=True) + NORM_EPS)
    xhat = xv * r
    dxhat = dy * gain
    dx = r * (dxhat - xhat * jnp.mean(dxhat * xhat, axis=-1, keepdims=True))
    dgain = jnp.sum(dy * xhat, axis=0, keepdims=True)
    return dx, dgain


def _loss_head(x, gain, target):
    t, d = x.shape
    tr = _tile(t, 256, 16)

    def body(x_ref, g_ref, t_ref, dx_ref, dxb_ref, dg_ref, loss_ref):
        xv = x_ref[...]
        gain = g_ref[...]
        r = lax.rsqrt(jnp.mean(xv * xv, axis=-1, keepdims=True) + NORM_EPS)
        err = xv * r * gain - t_ref[...]
        dx, dgain = _rms_vjp(xv, gain, err * (1.0 / d))
        dx_ref[...] = dx
        dxb_ref[...] = dx.astype(BF)

        @pl.when(pl.program_id(0) == 0)
        def _():
            dg_ref[...] = jnp.zeros_like(dg_ref)
            loss_ref[...] = jnp.zeros_like(loss_ref)

        dg_ref[...] += dgain
        loss_ref[...] += jnp.sum(err * err, axis=0, keepdims=True) * (0.5 / d)

    row = pl.BlockSpec((tr, d), lambda i: (i, 0))
    vec = pl.BlockSpec((1, d), lambda i: (0, 0))
    return pl.pallas_call(
        body, name="loss_head", grid=(t // tr,),
        in_specs=[row, vec, row], out_specs=[row, row, vec, vec],
        out_shape=[jax.ShapeDtypeStruct((t, d), F32), jax.ShapeDtypeStruct((t, d), BF),
                   jax.ShapeDtypeStruct((1, d), F32), jax.ShapeDtypeStruct((1, d), F32)],
        compiler_params=_params(),
    )(x, gain, target)


def _mm_nn(a, b, out_dtype, name, residual=None, tm_pref=512, tn_pref=1152):
    m, k = a.shape
    n = b.shape[1]
    tm, tn = _tile(m, tm_pref, 16), _tile(n, tn_pref, LANE)

    def body(*refs):
        if residual is None:
            a_ref, b_ref, o_ref = refs
            o_ref[...] = _dot(a_ref[...], b_ref[...]).astype(out_dtype)
        else:
            a_ref, b_ref, r_ref, o_ref = refs
            o_ref[...] = (r_ref[...] + _dot(a_ref[...], b_ref[...])).astype(out_dtype)

    in_specs = [pl.BlockSpec((tm, k), lambda j, i: (i, 0)), pl.BlockSpec((k, tn), lambda j, i: (0, j))]
    args = [a, b]
    if residual is not None:
        in_specs.append(pl.BlockSpec((tm, tn), lambda j, i: (i, j)))
        args.append(residual)
    return pl.pallas_call(
        body, name=name, grid=(n // tn, m // tm), in_specs=in_specs,
        out_specs=pl.BlockSpec((tm, tn), lambda j, i: (i, j)),
        out_shape=jax.ShapeDtypeStruct((m, n), out_dtype), compiler_params=_params(),
    )(*args)


def _rms_bwd_tail(dy_ref, first, x_ref, g_ref, dres_ref, dx_ref, dxb_ref, dg_ref):
    @pl.when(first)
    def _():
        dg_ref[...] = jnp.zeros_like(dg_ref)

    gain = g_ref[...]
    for r in range(0, dy_ref.shape[0], LANE):
        rows = pl.ds(r, min(LANE, dy_ref.shape[0] - r))
        dx, dgain = _rms_vjp(x_ref[rows, :], gain, dy_ref[rows, :])
        dx = dx + dres_ref[rows, :]
        dx_ref[rows, :] = dx
        dxb_ref[rows, :] = dx.astype(BF)
        dg_ref[...] += dgain


def _mm_nt(a, b, out_dtype, name, tm_pref=512, tn_pref=1024, tk_pref=2048, rms=None, dep=None):
    m, k = a.shape
    n = b.shape[0]
    tm, tn, tk = _tile(m, tm_pref, 16), _tile(n, tn_pref, LANE), _tile(k, tk_pref, LANE)
    nk = k // tk
    assert rms is None or tn == n

    def body(*refs):
        if rms is None:
            a_ref, b_ref, o_ref, acc_ref = refs
        else:
            a_ref, b_ref, x_ref, g_ref, dres_ref, dx_ref, dxb_ref, dg_ref, acc_ref = refs
        kk = pl.program_id(2)

        @pl.when(kk == 0)
        def _():
            acc_ref[...] = jnp.zeros_like(acc_ref)

        acc_ref[...] += _dot(a_ref[...], b_ref[...], NT)

        @pl.when(kk == nk - 1)
        def _():
            if rms is None:
                o_ref[...] = acc_ref[...].astype(out_dtype)
            else:
                _rms_bwd_tail(acc_ref, pl.program_id(1) == 0, x_ref, g_ref, dres_ref, dx_ref, dxb_ref, dg_ref)

    in_specs = [pl.BlockSpec((tm, tk), lambda j, i, kk: (i, kk)), pl.BlockSpec((tn, tk), lambda j, i, kk: (j, kk))]
    row = pl.BlockSpec((tm, tn), lambda j, i, kk: (i, j))
    if rms is None:
        args, out_specs, out_shape = [a, b], row, jax.ShapeDtypeStruct((m, n), out_dtype)
    else:
        vec = pl.BlockSpec((1, n), lambda j, i, kk: (0, 0))
        args, in_specs = [a, b, *rms], in_specs + [row, vec, row]
        out_specs = [row, row, vec]
        out_shape = [jax.ShapeDtypeStruct((m, n), F32), jax.ShapeDtypeStruct((m, n), BF),
                     jax.ShapeDtypeStruct((1, n), F32)]
    return _call(
        body, args, dep=dep, name=name, grid=(n // tn, m // tm, nk), in_specs=in_specs, out_specs=out_specs,
        out_shape=out_shape, scratch_shapes=[pltpu.VMEM((tm, tn), F32)], compiler_params=_params(),
    )


def _mm_tn(a, b, out_dtype, name, tn_pref=1152, tk_pref=512, a_transposed=False):
    (k, t) = a.shape if a_transposed else a.shape[::-1]
    n = b.shape[1]
    tn, tk = _tile(n, tn_pref, LANE), _tile(t, tk_pref, LANE if a_transposed else 16)
    nt = t // tk

    def body(a_ref, b_ref, o_ref, acc_ref):
        tt = pl.program_id(1)

        @pl.when(tt == 0)
        def _():
            acc_ref[...] = jnp.zeros_like(acc_ref)

        acc_ref[...] += _dot(a_ref[...], b_ref[...], NN if a_transposed else TN)

        @pl.when(tt == nt - 1)
        def _():
            o_ref[...] = acc_ref[...].astype(out_dtype)

    if a_transposed:
        a_spec = pl.BlockSpec((k, tk), lambda j, tt: (0, tt))
    else:
        a_spec = pl.BlockSpec((tk, k), lambda j, tt: (tt, 0))
    return pl.pallas_call(
        body, name=name, grid=(n // tn, nt),
        in_specs=[a_spec, pl.BlockSpec((tk, tn), lambda j, tt: (tt, j))],
        out_specs=pl.BlockSpec((k, tn), lambda j, tt: (0, j)),
        out_shape=jax.ShapeDtypeStruct((k, n), out_dtype),
        scratch_shapes=[pltpu.VMEM((k, tn), F32)], compiler_params=_params(),
    )(a, b)


def _ffn_gate_up(hn, wg, wu, name):
    t, d = hn.shape
    ns, _, f = wg.shape
    tm = _tile(t, FFN_ROWS, 16)

    def body(h_ref, wg_ref, wu_ref, g_ref, u_ref, a_ref):
        h = h_ref[...]
        g = _dot(h, wg_ref[...])
        u = _dot(h, wu_ref[...])
        g_ref[...] = g.astype(BF)
        u_ref[...] = u.astype(BF)
        a_ref[...] = (g * _sig(g) * u).astype(BF)

    wspec = pl.BlockSpec((None, d, f), lambda j, i: (j, 0, 0))
    hid = pl.BlockSpec((None, tm, f), lambda j, i: (j, i, 0))
    out = jax.ShapeDtypeStruct((ns, t, f), BF)
    return pl.pallas_call(
        body, name=name, grid=(ns, t // tm),
        in_specs=[pl.BlockSpec((tm, d), lambda j, i: (i, 0)), wspec, wspec],
        out_specs=[hid, hid, hid], out_shape=[out, out, out], compiler_params=_params(),
    )(hn, wg, wu)


def _ffn_gate(hn, wg, name):
    t, d = hn.shape
    ns, _, f = wg.shape
    tm = _tile(t, FFN_ROWS, 16)

    def body(h_ref, wg_ref, g_ref):
        g_ref[...] = _dot(h_ref[...], wg_ref[...])

    return pl.pallas_call(
        body, name=name, grid=(ns, t // tm),
        in_specs=[pl.BlockSpec((tm, d), lambda j, i: (i, 0)), pl.BlockSpec((None, d, f), lambda j, i: (j, 0, 0))],
        out_specs=pl.BlockSpec((None, tm, f), lambda j, i: (j, i, 0)),
        out_shape=jax.ShapeDtypeStruct((ns, t, f), F32), compiler_params=_params(),
    )(hn, wg)


def _ffn_up_act(hn, wu, g, name):
    t, d = hn.shape
    ns, _, f = wu.shape
    tm = _tile(t, FFN_ROWS, 16)

    def body(h_ref, wu_ref, g_ref, gb_ref, u_ref, a_ref):
        u = _dot(h_ref[...], wu_ref[...])
        gv = g_ref[...]
        gb_ref[...] = gv.astype(BF)
        u_ref[...] = u.astype(BF)
        a_ref[...] = (gv * _sig(gv) * u).astype(BF)

    hid = pl.BlockSpec((None, tm, f), lambda j, i: (j, i, 0))
    out = jax.ShapeDtypeStruct((ns, t, f), BF)
    return pl.pallas_call(
        body, name=name, grid=(ns, t // tm),
        in_specs=[pl.BlockSpec((tm, d), lambda j, i: (i, 0)), pl.BlockSpec((None, d, f), lambda j, i: (j, 0, 0)), hid],
        out_specs=[hid, hid, hid], out_shape=[out, out, out], compiler_params=_params(),
    )(hn, wu, g)


def _ffn_down(act, wd, xres, name):
    ns, t, f = act.shape
    d = wd.shape[2]
    tm = _tile(t, FFN_ROWS, 16)

    def body(a_ref, w_ref, x_ref, o_ref):
        @pl.when(pl.program_id(1) == 0)
        def _():
            o_ref[...] = x_ref[...]

        o_ref[...] += 0.5 * _dot(a_ref[...], w_ref[...])

    row = pl.BlockSpec((tm, d), lambda i, j: (i, 0))
    return pl.pallas_call(
        body, name=name, grid=(t // tm, ns),
        in_specs=[pl.BlockSpec((None, tm, f), lambda i, j: (j, i, 0)),
                  pl.BlockSpec((None, f, d), lambda i, j: (j, 0, 0)), row],
        out_specs=row, out_shape=jax.ShapeDtypeStruct((t, d), F32), compiler_params=_params(),
    )(act, wd, xres)


def _ffn_bwd_hidden(dxb, wd, g, u, name):
    t, d = dxb.shape
    ns, f, _ = wd.shape
    tm = _tile(t, FFN_ROWS, 16)

    def body(dx_ref, w_ref, g_ref, u_ref, dg_ref, du_ref):
        dh = 0.5 * _dot(dx_ref[...], w_ref[...], NT)
        gv = g_ref[...].astype(F32)
        uv = u_ref[...].astype(F32)
        s = _sig(gv)
        dg_ref[...] = (dh * uv * (s * (1.0 + gv * (1.0 - s)))).astype(BF)
        du_ref[...] = (dh * (gv * s)).astype(BF)

    hid = pl.BlockSpec((None, tm, f), lambda j, i: (j, i, 0))
    out = jax.ShapeDtypeStruct((ns, t, f), BF)
    return pl.pallas_call(
        body, name=name, grid=(ns, t // tm),
        in_specs=[pl.BlockSpec((tm, d), lambda j, i: (i, 0)),
                  pl.BlockSpec((None, f, d), lambda j, i: (j, 0, 0)), hid, hid],
        out_specs=[hid, hid], out_shape=[out, out], compiler_params=_params(),
    )(dxb, wd, g, u)


def _ffn_dw_down(act, dxb, name):
    ns, t, f = act.shape
    d = dxb.shape[1]
    tk = _tile(t, DW_ROWS, 16)
    nt = t // tk

    def body(a_ref, dx_ref, o_ref, acc_ref):
        tt = pl.program_id(1)

        @pl.when(tt == 0)
        def _():
            acc_ref[...] = jnp.zeros_like(acc_ref)

        acc_ref[...] += _dot(a_ref[...], dx_ref[...], TN)

        @pl.when(tt == nt - 1)
        def _():
            o_ref[...] = (0.5 * acc_ref[...]).astype(BF)

    return pl.pallas_call(
        body, name=name, grid=(ns, nt),
        in_specs=[pl.BlockSpec((None, tk, f), lambda j, tt: (j, tt, 0)),
                  pl.BlockSpec((tk, d), lambda j, tt: (tt, 0))],
        out_specs=pl.BlockSpec((None, f, d), lambda j, tt: (j, 0, 0)),
        out_shape=jax.ShapeDtypeStruct((ns, f, d), BF),
        scratch_shapes=[pltpu.VMEM((f, d), F32)], compiler_params=_params(),
    )(act, dxb)


def _ffn_dw_gate_up(hn_t, dg, du, name, dep=None):
    d, t = hn_t.shape
    ns, _, f = dg.shape
    tk = _tile(t, DW_ROWS, LANE)
    nt = t // tk

    def body(h_ref, dg_ref, du_ref, og_ref, ou_ref, accg_ref, accu_ref):
        tt = pl.program_id(1)

        @pl.when(tt == 0)
        def _():
            accg_ref[...] = jnp.zeros_like(accg_ref)
            accu_ref[...] = jnp.zeros_like(accu_ref)

        h = h_ref[...]
        accg_ref[...] += _dot(h, dg_ref[...])
        accu_ref[...] += _dot(h, du_ref[...])

        @pl.when(tt == nt - 1)
        def _():
            og_ref[...] = accg_ref[...].astype(BF)
            ou_ref[...] = accu_ref[...].astype(BF)

    hid = pl.BlockSpec((None, tk, f), lambda j, tt: (j, tt, 0))
    wspec = pl.BlockSpec((None, d, f), lambda j, tt: (j, 0, 0))
    out = jax.ShapeDtypeStruct((ns, d, f), BF)
    return _call(
        body, [hn_t, dg, du], dep=dep, name=name, grid=(ns, nt),
        in_specs=[pl.BlockSpec((d, tk), lambda j, tt: (0, tt)), hid, hid],
        out_specs=[wspec, wspec], out_shape=[out, out],
        scratch_shapes=[pltpu.VMEM((d, f), F32), pltpu.VMEM((d, f), F32)], compiler_params=_params(),
    )


def _rms_bwd(dy, x, gain, dres, name):
    t, d = x.shape
    tr = _tile(t, 256, 16)

    def body(dy_ref, x_ref, g_ref, dres_ref, dx_ref, dxb_ref, dg_ref):
        _rms_bwd_tail(dy_ref, pl.program_id(0) == 0, x_ref, g_ref, dres_ref, dx_ref, dxb_ref, dg_ref)

    row = pl.BlockSpec((tr, d), lambda i: (i, 0))
    vec = pl.BlockSpec((1, d), lambda i: (0, 0))
    return pl.pallas_call(
        body, name=name, grid=(t // tr,),
        in_specs=[row, row, vec, row], out_specs=[row, row, vec],
        out_shape=[jax.ShapeDtypeStruct((t, d), F32), jax.ShapeDtypeStruct((t, d), BF),
                   jax.ShapeDtypeStruct((1, d), F32)],
        compiler_params=_params(),
    )(dy, x, gain, dres)


def _ffn_bwd_input(dg, du, wg, wu, name, dep=None):
    ns, t, f = dg.shape
    d = wg.shape[1]
    tm = _tile(t, FFN_ROWS, 16)

    def body(dg_ref, du_ref, wg_ref, wu_ref, o_ref):
        @pl.when(pl.program_id(1) == 0)
        def _():
            o_ref[...] = jnp.zeros_like(o_ref)

        o_ref[...] += _dot(dg_ref[...], wg_ref[...], NT) + _dot(du_ref[...], wu_ref[...], NT)

    hid = pl.BlockSpec((None, tm, f), lambda i, j: (j, i, 0))
    wspec = pl.BlockSpec((None, d, f), lambda i, j: (j, 0, 0))
    return _call(
        body, [dg, du, wg, wu], dep=dep, name=name, grid=(t // tm, ns),
        in_specs=[hid, hid, wspec, wspec],
        out_specs=pl.BlockSpec((tm, d), lambda i, j: (i, 0)),
        out_shape=jax.ShapeDtypeStruct((t, d), F32), compiler_params=_params(),
    )


def _rope_tables(t):
    pos = jnp.arange(t, dtype=F32)
    inv_freq = ROPE_THETA ** (-jnp.arange(0, ROPE_DIM, 2, dtype=F32) / ROPE_DIM)
    ang = pos[:, None] * inv_freq[None, :]
    cos, sin = jnp.cos(ang), jnp.sin(ang)
    rest = HEAD_DIM - ROPE_DIM
    one = jnp.ones((t, rest), F32)
    zero_h = jnp.zeros((t, ROPE_HALF), F32)
    zero_r = jnp.zeros((t, rest), F32)
    c = jnp.concatenate([cos, cos, one], axis=1)
    s1 = jnp.concatenate([-sin, zero_h, zero_r], axis=1)
    s2 = jnp.concatenate([zero_h, sin, zero_r], axis=1)
    return c, s1, s2


def _rope(xh, c, s1, s2):
    return xh * c + pltpu.roll(xh, HEAD_DIM - ROPE_HALF, 1) * s1 + pltpu.roll(xh, ROPE_HALF, 1) * s2


def _rope_t(dh, c, s1, s2):
    return dh * c + pltpu.roll(dh * s1, ROPE_HALF, 1) + pltpu.roll(dh * s2, HEAD_DIM - ROPE_HALF, 1)


def _mixer_prep(proj, tables, bf_pad, hd, scale):
    t, np_ = proj.shape
    tr = _tile(t, 256, 16)
    nh = hd // HEAD_DIM
    nblk = hd // LANE
    f_blk = np_ // LANE - 1

    def body(qd_ref, kd_ref, vd_ref, qf_ref, kf_ref, vf_ref, fl_ref, c_ref, s1_ref, s2_ref, b_ref,
             oqd, okd, ovd, oqf, okf, ovf, olog):
        c, s1, s2 = c_ref[...], s1_ref[...], s2_ref[...]
        for h in range(nh):
            sl = slice(h * HEAD_DIM, (h + 1) * HEAD_DIM)
            oqd[:, sl] = (_rope(qd_ref[:, sl], c, s1, s2) * scale).astype(BF)
            okd[:, sl] = _rope(kd_ref[:, sl], c, s1, s2).astype(BF)
        ovd[...] = vd_ref[...].astype(BF)
        oqf[...] = (qf_ref[...] * scale).astype(BF)
        okf[...] = kf_ref[...].astype(BF)
        ovf[...] = vf_ref[...].astype(BF)
        z = fl_ref[...] + b_ref[...]
        olog[...] = jnp.minimum(z, 0.0) - jnp.log(1.0 + jnp.exp(-jnp.abs(z)))

    def col(kblk):
        return pl.BlockSpec((tr, hd), lambda i, kblk=kblk: (i, kblk))

    lane_row = pl.BlockSpec((tr, LANE), lambda i: (i, 0))
    in_specs = [col(0), col(1), col(2), col(3), col(4), col(5),
                pl.BlockSpec((tr, LANE), lambda i: (i, f_blk)),
                lane_row, lane_row, lane_row, pl.BlockSpec((1, LANE), lambda i: (0, 0))]
    o = pl.BlockSpec((tr, hd), lambda i: (i, 0))
    ob = jax.ShapeDtypeStruct((t, hd), BF)
    del nblk
    return pl.pallas_call(
        body, name="mixer_prep", grid=(t // tr,), in_specs=in_specs,
        out_specs=[o, o, o, o, o, o, lane_row],
        out_shape=[ob, ob, ob, ob, ob, ob, jax.ShapeDtypeStruct((t, LANE), F32)],
        compiler_params=_params(),
    )(proj, proj, proj, proj, proj, proj, proj, *tables, bf_pad)


def _split3(x):
    x1 = x.astype(BF)
    r1 = x - x1.astype(F32)
    x2 = r1.astype(BF)
    x3 = (r1 - x2.astype(F32)).astype(BF)
    return x1, x2, x3


def _cumsum_rows(x, reverse, name):
    t, w = x.shape
    blk = LANE
    nb = t // blk

    def body(x_ref, o_ref):
        r = lax.broadcasted_iota(jnp.int32, (blk, blk), 0)
        c = lax.broadcasted_iota(jnp.int32, (blk, blk), 1)
        tri = jnp.where((c >= r) if reverse else (c <= r), 1.0, 0.0).astype(BF)

        def step(i, carry):
            b = (nb - 1 - i) if reverse else i
            off = pl.multiple_of(b * blk, blk)
            xb = x_ref[pl.ds(off, blk), :]
            x1, x2, x3 = _split3(xb)
            o_ref[pl.ds(off, blk), :] = _dot(tri, x1) + _dot(tri, x2) + _dot(tri, x3) + carry
            return carry + jnp.sum(xb, axis=0, keepdims=True)

        lax.fori_loop(0, nb, step, jnp.zeros((1, w), F32))

    return pl.pallas_call(body, name=name, out_shape=jax.ShapeDtypeStruct((t, w), F32),
                          compiler_params=_params())(x)


ATTN_ROWS = 16


def _dil_bias_tiles(tq):
    nbias = MAX_WINDOW // tq + 1
    b = lax.broadcasted_iota(jnp.int32, (nbias, tq, tq), 0)
    i = lax.broadcasted_iota(jnp.int32, (nbias, tq, tq), 1)
    j = lax.broadcasted_iota(jnp.int32, (nbias, tq, tq), 2)
    delta = b * tq + i - j
    mult = jnp.zeros((nbias, tq, tq), F32)
    for w, dil in DIL_PATTERNS:
        mult = mult + jnp.where((delta >= 0) & (delta <= w) & (delta % dil == 0), 1.0, 0.0)
    return jnp.where(mult > 0.0, jnp.log(jnp.maximum(mult, 1.0)), NEG)


def _rep(x, width):
    return jnp.tile(x, (1, width // LANE))


def _chunks(n_rows, fn):
    for c in range(n_rows // ATTN_ROWS):
        fn(c * ATTN_ROWS)


def _causal(r0, tq, transposed):
    a = lax.broadcasted_iota(jnp.int32, (ATTN_ROWS, tq), 0) + r0
    b = lax.broadcasted_iota(jnp.int32, (ATTN_ROWS, tq), 1)
    return (a <= b) if transposed else (b <= a)


def _rows8(x):
    return jnp.transpose(x)[:8, :]


def _attn_fwd(mode, q, k, v, bias, tq, name):
    t, hd = q.shape
    nh = hd // HEAD_DIM
    nb = t // tq
    wb = MAX_WINDOW // tq
    fox = mode == "fox"

    def body(q_ref, k_ref, v_ref, b_ref, o_ref, lse_ref, lse_row_ref, s_ref, p_ref, m_ref, l_ref, acc_ref):
        qi = pl.program_id(1)
        qb = q_ref[...]
        m_ref[...] = jnp.full_like(m_ref, NEG)
        l_ref[...] = jnp.zeros_like(l_ref)
        acc_ref[...] = jnp.zeros_like(acc_ref)

        def tile(kj, diag):
            off = pl.multiple_of(kj * tq, tq)
            s_ref[...] = _dot(qb, k_ref[pl.ds(off, tq), :], NT)
            if fox:
                brow = b_ref[qi][:, :1] - b_ref[kj]

            def chunk(r0):
                rows = pl.ds(r0, ATTN_ROWS)
                if fox:
                    s = s_ref[rows, :] + brow
                    if diag:
                        s = jnp.where(_causal(r0, tq, False), s, NEG)
                else:
                    s = s_ref[rows, :] + b_ref[qi - kj, rows, :]
                m_old = m_ref[rows, :]
                m_new = jnp.maximum(m_old, jnp.max(s, axis=1, keepdims=True))
                p = jnp.exp(s - _rep(m_new, tq))
                alpha = jnp.exp(m_old - m_new)
                l_ref[rows, :] = alpha * l_ref[rows, :] + jnp.sum(p, axis=1, keepdims=True)
                m_ref[rows, :] = m_new
                acc_ref[rows, :] = alpha * acc_ref[rows, :]
                p_ref[rows, :] = p.astype(BF)

            _chunks(tq, chunk)
            acc_ref[...] += _dot(p_ref[...], v_ref[pl.ds(off, tq), :])

        tile(qi, True)
        if fox:
            lax.fori_loop(0, qi, lambda kj, c: (tile(kj, False), c)[1], 0)
        else:
            lax.fori_loop(1, jnp.minimum(qi, wb) + 1, lambda i, c: (tile(qi - i, False), c)[1], 0)
        o_ref[...] = (acc_ref[...] / l_ref[...]).astype(BF)
        lse = m_ref[...] + jnp.log(l_ref[...])
        lse_ref[...] = lse
        lse_row_ref[...] = _rows8(lse)

    qspec = pl.BlockSpec((tq, HEAD_DIM), lambda h, i: (i, h))
    kvspec = pl.BlockSpec((t, HEAD_DIM), lambda h, i: (0, h))
    repspec = pl.BlockSpec((None, tq, LANE), lambda h, i: (h, i, 0))
    row8spec = pl.BlockSpec((None, None, 8, tq), lambda h, i: (h, i, 0, 0))
    if fox:
        bspec = pl.BlockSpec((None, nb, 1, tq), lambda h, i: (h, 0, 0, 0))
    else:
        bspec = pl.BlockSpec((wb + 1, tq, tq), lambda h, i: (0, 0, 0))
    return pl.pallas_call(
        body, name=name, grid=(nh, nb), in_specs=[qspec, kvspec, kvspec, bspec],
        out_specs=[qspec, repspec, row8spec],
        out_shape=[jax.ShapeDtypeStruct((t, hd), BF), jax.ShapeDtypeStruct((nh, t, LANE), F32),
                   jax.ShapeDtypeStruct((nh, nb, 8, tq), F32)],
        scratch_shapes=[pltpu.VMEM((tq, tq), F32), pltpu.VMEM((tq, tq), BF), pltpu.VMEM((tq, LANE), F32),
                        pltpu.VMEM((tq, LANE), F32), pltpu.VMEM((tq, HEAD_DIM), F32)],
        compiler_params=_params(),
    )(q, k, v, bias)


def _attn_bwd_dq(mode, q, k, v, o, do, lse, bias, tq, name, dep=None):
    t, hd = q.shape
    nh = hd // HEAD_DIM
    nb = t // tq
    wb = MAX_WINDOW // tq
    fox = mode == "fox"

    def body(q_ref, k_ref, v_ref, o_ref, do_ref, lse_ref, b_ref, dq_ref, dl_row_ref,
             s_ref, dp_ref, x_ref, y_ref, acc_ref, acc2_ref, dl_ref):
        qi = pl.program_id(1)
        qb = q_ref[...]
        dob = do_ref[...]
        acc_ref[...] = jnp.zeros_like(acc_ref)
        if fox:
            acc2_ref[...] = jnp.zeros_like(acc2_ref)
            dl_ref[...] = jnp.zeros_like(dl_ref)
        else:
            prod = o_ref[...].astype(F32) * dob.astype(F32)
            dl_ref[...] = jnp.broadcast_to(jnp.sum(prod, axis=1, keepdims=True), (tq, LANE))

        def tile(kj, diag):
            off = pl.multiple_of(kj * tq, tq)
            kb = k_ref[pl.ds(off, tq), :]
            s_ref[...] = _dot(qb, kb, NT)
            dp_ref[...] = _dot(dob, v_ref[pl.ds(off, tq), :], NT)
            if fox:
                brow = b_ref[qi][:, :1] - b_ref[kj]

            def chunk(r0):
                rows = pl.ds(r0, ATTN_ROWS)
                lse_c = _rep(lse_ref[rows, :], tq)
                if fox:
                    s = s_ref[rows, :] + brow
                    if diag:
                        s = jnp.where(_causal(r0, tq, False), s, NEG)
                    p = jnp.exp(s - lse_c)
                    pdp = p * dp_ref[rows, :]
                    dl_ref[rows, :] += jnp.sum(pdp, axis=1, keepdims=True)
                    x_ref[rows, :] = pdp.astype(BF)
                    y_ref[rows, :] = p.astype(BF)
                else:
                    p = jnp.exp(s_ref[rows, :] + b_ref[qi - kj, rows, :] - lse_c)
                    x_ref[rows, :] = (p * (dp_ref[rows, :] - _rep(dl_ref[rows, :], tq))).astype(BF)

            _chunks(tq, chunk)
            acc_ref[...] += _dot(x_ref[...], kb)
            if fox:
                acc2_ref[...] += _dot(y_ref[...], kb)

        tile(qi, True)
        if fox:
            lax.fori_loop(0, qi, lambda kj, c: (tile(kj, False), c)[1], 0)
            dq_ref[...] = acc_ref[...] - dl_ref[...] * acc2_ref[...]
        else:
            lax.fori_loop(1, jnp.minimum(qi, wb) + 1, lambda i, c: (tile(qi - i, False), c)[1], 0)
            dq_ref[...] = acc_ref[...]
        dl_row_ref[...] = _rows8(dl_ref[...])

    qspec = pl.BlockSpec((tq, HEAD_DIM), lambda h, i: (i, h))
    kvspec = pl.BlockSpec((t, HEAD_DIM), lambda h, i: (0, h))
    repspec = pl.BlockSpec((None, tq, LANE), lambda h, i: (h, i, 0))
    row8spec = pl.BlockSpec((None, None, 8, tq), lambda h, i: (h, i, 0, 0))
    if fox:
        bspec = pl.BlockSpec((None, nb, 1, tq), lambda h, i: (h, 0, 0, 0))
    else:
        bspec = pl.BlockSpec((wb + 1, tq, tq), lambda h, i: (0, 0, 0))
    return _call(
        body, [q, k, v, o, do, lse, bias], dep=dep, name=name, grid=(nh, nb),
        in_specs=[qspec, kvspec, kvspec, qspec, qspec, repspec, bspec],
        out_specs=[qspec, row8spec],
        out_shape=[jax.ShapeDtypeStruct((t, hd), F32), jax.ShapeDtypeStruct((nh, nb, 8, tq), F32)],
        scratch_shapes=[pltpu.VMEM((tq, tq), F32), pltpu.VMEM((tq, tq), F32), pltpu.VMEM((tq, tq), BF),
                        pltpu.VMEM((tq, tq), BF), pltpu.VMEM((tq, HEAD_DIM), F32),
                        pltpu.VMEM((tq, HEAD_DIM), F32), pltpu.VMEM((tq, LANE), F32)],
        compiler_params=_params(),
    )


def _attn_bwd_dkv(mode, q, k, v, do, lse_row, dl_row, bias_t, c_row, tq, name):
    t, hd = q.shape
    nh = hd // HEAD_DIM
    nb = t // tq
    wb = MAX_WINDOW // tq
    fox = mode == "fox"

    def body(*refs):
        if fox:
            (q_ref, k_ref, v_ref, do_ref, lse_ref, dl_ref, b_ref, cq_ref, dk_ref, dv_ref, dc_row_ref,
             s_ref, dp_ref, x_ref, y_ref, dc_ref) = refs
        else:
            q_ref, k_ref, v_ref, do_ref, lse_ref, dl_ref, b_ref, dk_ref, dv_ref, s_ref, dp_ref, x_ref, y_ref = refs
        kj = pl.program_id(1)
        kb = k_ref[...]
        vb = v_ref[...]
        dk_ref[...] = jnp.zeros_like(dk_ref)
        dv_ref[...] = jnp.zeros_like(dv_ref)
        if fox:
            dc_ref[...] = jnp.zeros_like(dc_ref)

        def tile(qi, diag):
            off = pl.multiple_of(qi * tq, tq)
            qb = q_ref[pl.ds(off, tq), :]
            dob = do_ref[pl.ds(off, tq), :]
            s_ref[...] = _dot(kb, qb, NT)
            dp_ref[...] = _dot(vb, dob, NT)
            lse_r = lse_ref[qi, 0:1, :]
            dl_r = dl_ref[qi, 0:1, :]
            if fox:
                kbias = cq_ref[qi][:, :1] - b_ref[...]

            def chunk(r0):
                rows = pl.ds(r0, ATTN_ROWS)
                if fox:
                    s = s_ref[rows, :] + _rep(kbias[r0:r0 + ATTN_ROWS, :], tq)
                    if diag:
                        s = jnp.where(_causal(r0, tq, True), s, NEG)
                else:
                    s = s_ref[rows, :] + b_ref[qi - kj, rows, :]
                pt = jnp.exp(s - lse_r)
                dst = pt * (dp_ref[rows, :] - dl_r)
                x_ref[rows, :] = pt.astype(BF)
                y_ref[rows, :] = dst.astype(BF)
                if fox:
                    dc_ref[rows, :] -= jnp.sum(dst, axis=1, keepdims=True)

            _chunks(tq, chunk)
            dv_ref[...] += _dot(x_ref[...], dob)
            dk_ref[...] += _dot(y_ref[...], qb)

        tile(kj, True)
        hi = nb if fox else jnp.minimum(kj + wb + 1, nb)
        lax.fori_loop(kj + 1, hi, lambda qi, c: (tile(qi, False), c)[1], 0)
        if fox:
            dc_row_ref[...] = _rows8(dc_ref[...])

    blkspec = pl.BlockSpec((tq, HEAD_DIM), lambda h, j: (j, h))
    fullspec = pl.BlockSpec((t, HEAD_DIM), lambda h, j: (0, h))
    rows8spec = pl.BlockSpec((None, nb, 8, tq), lambda h, j: (h, 0, 0, 0))
    repspec = pl.BlockSpec((None, tq, LANE), lambda h, j: (h, j, 0))
    in_specs = [fullspec, blkspec, blkspec, fullspec, rows8spec, rows8spec]
    args = [q, k, v, do, lse_row, dl_row, bias_t]
    out_specs = [blkspec, blkspec]
    out_shape = [jax.ShapeDtypeStruct((t, hd), F32), jax.ShapeDtypeStruct((t, hd), F32)]
    scratch = [pltpu.VMEM((tq, tq), F32), pltpu.VMEM((tq, tq), F32), pltpu.VMEM((tq, tq), BF),
               pltpu.VMEM((tq, tq), BF)]
    if fox:
        in_specs += [repspec, pl.BlockSpec((None, nb, 1, tq), lambda h, j: (h, 0, 0, 0))]
        args.append(c_row)
        out_specs.append(pl.BlockSpec((None, None, 8, tq), lambda h, j: (h, j, 0, 0)))
        out_shape.append(jax.ShapeDtypeStruct((nh, nb, 8, tq), F32))
        scratch.append(pltpu.VMEM((tq, LANE), F32))
    else:
        in_specs.append(pl.BlockSpec((wb + 1, tq, tq), lambda h, j: (0, 0, 0)))
    return pl.pallas_call(
        body, name=name, grid=(nh, nb), in_specs=in_specs, out_specs=out_specs, out_shape=out_shape,
        scratch_shapes=scratch, compiler_params=_params(),
    )(*args)


def _gate_specs(t, d, hd, tr):
    row = pl.BlockSpec((tr, d), lambda i: (i, 0))
    vec = pl.BlockSpec((1, d), lambda i: (0, 0))
    base = 6 * hd // d
    gd = pl.BlockSpec((tr, d), lambda i: (i, base))
    gf = pl.BlockSpec((tr, d), lambda i: (i, base + 1))
    return row, vec, gd, gf


def _proj_merge(yd, yf, wpd, wpf, proj, b_d, b_f, hd):
    t = yd.shape[0]
    d = wpd.shape[1]
    tr = _tile(t, 256, 16)
    row, vec, gd, gf = _gate_specs(t, d, hd, tr)

    def body(yd_ref, yf_ref, wd_ref, wf_ref, gd_ref, gf_ref, bd_ref, bf_ref, pd_ref, pf_ref, o_ref):
        pd = _dot(yd_ref[...], wd_ref[...])
        pf = _dot(yf_ref[...], wf_ref[...])
        pd_ref[...] = pd
        pf_ref[...] = pf
        o_ref[...] = (_sig(gd_ref[...] + bd_ref[...]) * pd + _sig(gf_ref[...] + bf_ref[...]) * pf).astype(BF)

    yspec = pl.BlockSpec((tr, hd), lambda i: (i, 0))
    wspec = pl.BlockSpec((hd, d), lambda i: (0, 0))
    f32 = jax.ShapeDtypeStruct((t, d), F32)
    return pl.pallas_call(
        body, name="proj_merge", grid=(t // tr,), in_specs=[yspec, yspec, wspec, wspec, gd, gf, vec, vec],
        out_specs=[row, row, row], out_shape=[f32, f32, jax.ShapeDtypeStruct((t, d), BF)],
        compiler_params=_params(),
    )(yd, yf, wpd, wpf, proj, proj, b_d, b_f)


def _merge_bwd(dm, pd, pf, proj, b_d, b_f, hd):
    t, d = pd.shape
    tr = _tile(t, 256, 16)
    row, vec, gd, gf = _gate_specs(t, d, hd, tr)

    def body(dm_ref, pd_ref, pf_ref, gd_ref, gf_ref, bd_ref, bf_ref,
             dpd_ref, dpf_ref, dgd_ref, dgf_ref, dbd_ref, dbf_ref):
        dmv = dm_ref[...]
        sd = _sig(gd_ref[...] + bd_ref[...])
        sf = _sig(gf_ref[...] + bf_ref[...])
        dgd = dmv * pd_ref[...] * (sd * (1.0 - sd))
        dgf = dmv * pf_ref[...] * (sf * (1.0 - sf))
        dpd_ref[...] = (dmv * sd).astype(BF)
        dpf_ref[...] = (dmv * sf).astype(BF)
        dgd_ref[...] = dgd.astype(BF)
        dgf_ref[...] = dgf.astype(BF)

        @pl.when(pl.program_id(0) == 0)
        def _():
            dbd_ref[...] = jnp.zeros_like(dbd_ref)
            dbf_ref[...] = jnp.zeros_like(dbf_ref)

        dbd_ref[...] += jnp.sum(dgd, axis=0, keepdims=True)
        dbf_ref[...] += jnp.sum(dgf, axis=0, keepdims=True)

    ob = jax.ShapeDtypeStruct((t, d), BF)
    ov = jax.ShapeDtypeStruct((1, d), F32)
    return pl.pallas_call(
        body, name="merge_bwd", grid=(t // tr,), in_specs=[row, row, row, gd, gf, vec, vec],
        out_specs=[row, row, row, row, vec, vec], out_shape=[ob, ob, ob, ob, ov, ov],
        compiler_params=_params(),
    )(dm, pd, pf, proj, proj, b_d, b_f)


def _assemble_dproj(dqd, dkd, dvd, dqf, dkf, dvf, dgd, dgf, dlogf, proj, tables, bf_pad, scale):
    t, np_ = proj.shape
    hd = dqd.shape[1]
    d = dgd.shape[1]
    nh = hd // HEAD_DIM
    tr = _tile(t, 256, 16)
    f_blk = np_ // LANE - 1

    def body(dqd_ref, dkd_ref, dvd_ref, dqf_ref, dkf_ref, dvf_ref, dgd_ref, dgf_ref, dlog_ref, fl_ref,
             c_ref, s1_ref, s2_ref, b_ref, o_ref, db_ref):
        c, s1, s2 = c_ref[...], s1_ref[...], s2_ref[...]
        for h in range(nh):
            sl = slice(h * HEAD_DIM, (h + 1) * HEAD_DIM)
            o_ref[:, sl] = (_rope_t(dqd_ref[:, sl], c, s1, s2) * scale).astype(BF)
            o_ref[:, hd + h * HEAD_DIM:hd + (h + 1) * HEAD_DIM] = _rope_t(dkd_ref[:, sl], c, s1, s2).astype(BF)
        o_ref[:, 2 * hd:3 * hd] = dvd_ref[...].astype(BF)
        o_ref[:, 3 * hd:4 * hd] = (dqf_ref[...] * scale).astype(BF)
        o_ref[:, 4 * hd:5 * hd] = dkf_ref[...].astype(BF)
        o_ref[:, 5 * hd:6 * hd] = dvf_ref[...].astype(BF)
        o_ref[:, 6 * hd:6 * hd + d] = dgd_ref[...]
        o_ref[:, 6 * hd + d:6 * hd + 2 * d] = dgf_ref[...]
        z = fl_ref[...] + b_ref[...]
        dfl = dlog_ref[...] * _sig(-z)
        o_ref[:, 6 * hd + 2 * d:] = dfl.astype(BF)

        @pl.when(pl.program_id(0) == 0)
        def _():
            db_ref[...] = jnp.zeros_like(db_ref)

        db_ref[...] += jnp.sum(dfl, axis=0, keepdims=True)

    head = pl.BlockSpec((tr, hd), lambda i: (i, 0))
    row = pl.BlockSpec((tr, d), lambda i: (i, 0))
    lane_row = pl.BlockSpec((tr, LANE), lambda i: (i, 0))
    lane_vec = pl.BlockSpec((1, LANE), lambda i: (0, 0))
    return pl.pallas_call(
        body, name="assemble_dproj", grid=(t // tr,),
        in_specs=[head] * 6 + [row, row, lane_row, pl.BlockSpec((tr, LANE), lambda i: (i, f_blk)),
                               lane_row, lane_row, lane_row, lane_vec],
        out_specs=[pl.BlockSpec((tr, np_), lambda i: (i, 0)), lane_vec],
        out_shape=[jax.ShapeDtypeStruct((t, np_), BF), jax.ShapeDtypeStruct((1, LANE), F32)],
        compiler_params=_params(),
    )(dqd, dkd, dvd, dqf, dkf, dvf, dgd, dgf, dlogf, proj, *tables, bf_pad)


def _to_rows(a, tq):
    h, t = a.shape
    return a.reshape(h, t // tq, 1, tq)


def kernel(x, ffn1_norm, ffn1_w_gate, ffn1_w_up, ffn1_w_down, mix_norm, w_in, b_forget, b_gate_dil, b_gate_fox, w_proj_dil, w_proj_fox, w_out, ffn2_norm, ffn2_w_gate, ffn2_w_up, ffn2_w_down, final_norm, loss_target, m_ffn1_norm, m_ffn1_w_gate, m_ffn1_w_up, m_ffn1_w_down, m_mix_norm, m_w_in, m_b_forget, m_b_gate_dil, m_b_gate_fox, m_w_proj_dil, m_w_proj_fox, m_w_out, m_ffn2_norm, m_ffn2_w_gate, m_ffn2_w_up, m_ffn2_w_down, m_final_norm, v_ffn1_norm, v_ffn1_w_gate, v_ffn1_w_up, v_ffn1_w_down, v_mix_norm, v_w_in, v_b_forget, v_b_gate_dil, v_b_gate_fox, v_w_proj_dil, v_w_proj_fox, v_w_out, v_ffn2_norm, v_ffn2_w_gate, v_ffn2_w_up, v_ffn2_w_down, v_final_norm):
    t, d = x.shape[1], x.shape[2]
    hd = w_proj_dil.shape[1]
    nh = hd // HEAD_DIM
    n_f = b_forget.shape[1]
    cols = w_in.shape[2]
    in_cols = N_DEV * cols
    assert in_cols == 6 * hd + n_f + 2 * d and n_f == nh and n_f <= LANE
    np_ = 6 * hd + 2 * d + LANE
    scale = HEAD_DIM ** -0.5
    tq = _tile(t, 512, LANE)
    assert MAX_WINDOW % tq == 0 and tq % 16 == 0

    x2d = x[0]
    tgt = loss_target[0]

    ag_order = [ffn1_w_gate, ffn1_w_up, ffn1_w_down, w_in, w_proj_dil, w_proj_fox, w_out,
                ffn2_w_gate, ffn2_w_up, ffn2_w_down]
    ag_first, tok = _exchange_start([w[0].astype(BF) for w in ag_order[:2]], True, "ag_start_first", ks=FIRST_LEVEL)
    ag_rest, ag_token = _exchange_start([w[0].astype(BF) for w in ag_order[2:]], True, "ag_start", dep=tok,
                                        ks=FIRST_LEVEL)
    ag = ag_first + ag_rest

    def relay(idx, after, name):
        for i, h in zip(idx, _gather_relay([ag[i] for i in idx], after, name)):
            ag[i] = h

    def gathered(idx, after, name):
        return _gather_wait([ag[i] for i in idx], after, name)

    tables = _rope_tables(t)
    bf_pad = jnp.pad(b_forget, ((0, 0), (0, LANE - n_f)))

    hn1, hn1_t = _rms_fwd(x2d, ffn1_norm, "rms_ffn1", dep=ag_token)
    relay([0], hn1, "ag_relay_ffn1_gate")
    wg1, = gathered([0], hn1, "ag_wait_ffn1_gate")
    g1_f32 = _ffn_gate(hn1, wg1, "ffn1_gate")
    relay([1], g1_f32, "ag_relay_ffn1_up")
    wu1, = gathered([1], g1_f32, "ag_wait_ffn1_up")
    relay([2], wu1, "ag_relay_ffn1_down")
    g1, u1, a1 = _ffn_up_act(hn1, wu1, g1_f32, "ffn1_up_act")
    wd1, = gathered([2], a1, "ag_wait_ffn1_down")
    relay([3], wd1, "ag_relay_w_in")
    x1 = _ffn_down(a1, wd1, x2d, "ffn1_down")

    hm, hm_t = _rms_fwd(x1, mix_norm, "rms_mix")
    win_g, = gathered([3], hm, "ag_wait_w_in")
    relay([4, 5, 6], win_g, "ag_relay_mixer")
    segments = [(0, 6 * hd), (6 * hd + n_f, in_cols), (6 * hd, 6 * hd + n_f)]
    pieces = []
    for lo, hi in segments:
        for j in range(lo // cols, (hi - 1) // cols + 1):
            s, e = max(lo, j * cols), min(hi, (j + 1) * cols)
            pieces.append(win_g[j, :, s - j * cols:e - j * cols])
    win_p = jnp.concatenate(pieces + [jnp.zeros((d, LANE - n_f), BF)], axis=1)
    proj = _mm_nn(hm, win_p, F32, "w_in_fwd")
    qd, kd, vd, qf, kf, vf, logf = _mixer_prep(proj, tables, bf_pad, hd, scale)
    csum = _cumsum_rows(logf, False, "cumsum_logf")
    c_heads = csum[:, :nh].T
    c_row = _to_rows(c_heads, tq)
    c_rep = jnp.broadcast_to(c_heads[:, :, None], (nh, t, LANE))
    dil_bias = _dil_bias_tiles(tq)
    dil_bias_t = dil_bias.transpose(0, 2, 1)
    relay([7, 8, 9], qd, "ag_relay_ffn2")
    yd, lse_d, lse_d_row = _attn_fwd("dil", qd, kd, vd, dil_bias, tq, "attn_dil_fwd")
    yf, lse_f, lse_f_row = _attn_fwd("fox", qf, kf, vf, c_row, tq, "attn_fox_fwd")
    wpd_g, wpf_g = gathered([4, 5], yf, "ag_wait_proj")
    wpd = wpd_g.transpose(1, 0, 2).reshape(hd, d)
    wpf = wpf_g.transpose(1, 0, 2).reshape(hd, d)
    pd, pf, merged = _proj_merge(yd, yf, wpd, wpf, proj, b_gate_dil, b_gate_fox, hd)
    wout_g, = gathered([6], merged, "ag_wait_w_out")
    wout = wout_g.reshape(d, d)
    x2 = _mm_nn(merged, wout, F32, "w_out_fwd", residual=x1, tn_pref=1024)

    hn2, hn2_t = _rms_fwd(x2, ffn2_norm, "rms_ffn2")
    wg2, wu2 = gathered([7, 8], hn2, "ag_wait_ffn2_gate_up")
    g2, u2, a2 = _ffn_gate_up(hn2, wg2, wu2, "ffn2_gate_up")
    wd2, = gathered([9], a2, "ag_wait_ffn2_down")
    x3 = _ffn_down(a2, wd2, x2, "ffn2_down")

    dx3, dx3b, d_final, loss_lanes = _loss_head(x3, final_norm.reshape(1, d), tgt)

    def ffn_bwd(dxb, hn_t, g, u, a, wg, wu, wd, x_in, gain, dres, tag):
        dg, du = _ffn_bwd_hidden(dxb, wd, g, u, tag + "_bwd_hidden")
        dwd = _ffn_dw_down(a, dxb, tag + "_dw_down")
        rs_down, tok = _exchange_start([dwd], False, "rs_start_" + tag + "_down")
        dwg, dwu = _ffn_dw_gate_up(hn_t, dg, du, tag + "_dw_gate_up", dep=tok)
        rs_gu, tok = _exchange_start([dwg, dwu], False, "rs_start_" + tag + "_gate_up")
        dhn = _ffn_bwd_input(dg, du, wg, wu, tag + "_bwd_input", dep=tok)
        dx, dx_bf, dgain = _rms_bwd(dhn, x_in, gain, dres, "rms_" + tag + "_bwd")
        return dx, dx_bf, dgain, rs_gu + rs_down

    dx2, dx2b, d_ffn2_norm, rs_ffn2 = ffn_bwd(dx3b, hn2_t, g2, u2, a2, wg2, wu2, wd2, x2, ffn2_norm, dx3, "ffn2")

    dmerged = _mm_nt(dx2b, wout, F32, "w_out_bwd")
    dwout = _mm_tn(merged, dx2b, BF, "w_out_dw", tn_pref=1024)
    dpd, dpf, dgd, dgf, d_bd, d_bf = _merge_bwd(dmerged, pd, pf, proj, b_gate_dil, b_gate_fox, hd)
    dyd = _mm_nt(dpd, wpd, BF, "proj_dil_bwd")
    dyf = _mm_nt(dpf, wpf, BF, "proj_fox_bwd")
    dwpd = _mm_tn(yd, dpd, BF, "proj_dil_dw", tn_pref=1024)
    dwpf = _mm_tn(yf, dpf, BF, "proj_fox_dw", tn_pref=1024)
    dwpd_c = dwpd.reshape(hd, N_DEV, d // N_DEV).transpose(1, 0, 2)
    dwpf_c = dwpf.reshape(hd, N_DEV, d // N_DEV).transpose(1, 0, 2)
    dwout_c = dwout.reshape(N_DEV, d // N_DEV, d)
    rs_mix, tok = _exchange_start([dwout_c, dwpd_c, dwpf_c], False, "rs_start_mixer")

    dqd, dl_d = _attn_bwd_dq("dil", qd, kd, vd, yd, dyd, lse_d, dil_bias, tq, "attn_dil_dq", dep=tok)
    dkd, dvd = _attn_bwd_dkv("dil", qd, kd, vd, dyd, lse_d_row, dl_d, dil_bias_t, None, tq, "attn_dil_dkv")
    dqf, dl_f = _attn_bwd_dq("fox", qf, kf, vf, yf, dyf, lse_f, c_row, tq, "attn_fox_dq")
    dkf, dvf, dc = _attn_bwd_dkv("fox", qf, kf, vf, dyf, lse_f_row, dl_f, c_rep, c_row, tq, "attn_fox_dkv")
    dc_pad = jnp.pad(dc[:, :, 0, :].reshape(nh, t).T, ((0, 0), (0, LANE - nh)))
    dlogf = _cumsum_rows(dc_pad, True, "revcumsum_dc")
    dproj, d_bforget = _assemble_dproj(dqd, dkd, dvd, dqf, dkf, dvf, dgd, dgf, dlogf, proj, tables, bf_pad, scale)

    dwin_p = _mm_tn(hm_t, dproj, BF, "w_in_dw", tk_pref=DW_ROWS, a_transposed=True)
    def perm_col(c):
        if c < 6 * hd:
            return c
        return c + 2 * d if c < 6 * hd + n_f else c - n_f

    shards = []
    for j in range(N_DEV):
        cuts = sorted({j * cols, (j + 1) * cols} | {c for c in (6 * hd, 6 * hd + n_f) if j * cols < c < (j + 1) * cols})
        shards.append(jnp.concatenate([dwin_p[:, perm_col(lo):perm_col(lo) + hi - lo]
                                       for lo, hi in zip(cuts[:-1], cuts[1:])], axis=1))
    dwin_c = jnp.stack(shards)
    rs_win, tok = _exchange_start([dwin_c], False, "rs_start_w_in")
    dx1, dx1b, d_mix_norm = _mm_nt(dproj, win_p, F32, "w_in_bwd", tn_pref=d, tk_pref=1152,
                                   rms=(x1, mix_norm, dx2), dep=tok)

    grad_x, _, d_ffn1_norm, rs_ffn1 = ffn_bwd(dx1b, hn1_t, g1, u1, a1, wg1, wu1, wd1, x2d, ffn1_norm, dx1, "ffn1")

    def update(handles, names, after, tag):
        recvs = _exchange_wait(handles, False, after, "rs_wait_" + tag)
        res = {}
        for recv, n in zip(recvs, names):
            w, m, v = wmv[n]
            g, delta, m2, v2 = _adam_from_partials(recv, w[0], m[0], v[0], "adam_" + n)
            res[n] = (g[None], delta[None], m2[None], v2[None])
        return res, g

    wmv = {
        "ffn1_w_gate": (ffn1_w_gate, m_ffn1_w_gate, v_ffn1_w_gate),
        "ffn1_w_up": (ffn1_w_up, m_ffn1_w_up, v_ffn1_w_up),
        "ffn1_w_down": (ffn1_w_down, m_ffn1_w_down, v_ffn1_w_down),
        "w_in": (w_in, m_w_in, v_w_in),
        "w_proj_dil": (w_proj_dil, m_w_proj_dil, v_w_proj_dil),
        "w_proj_fox": (w_proj_fox, m_w_proj_fox, v_w_proj_fox),
        "w_out": (w_out, m_w_out, v_w_out),
        "ffn2_w_gate": (ffn2_w_gate, m_ffn2_w_gate, v_ffn2_w_gate),
        "ffn2_w_up": (ffn2_w_up, m_ffn2_w_up, v_ffn2_w_up),
        "ffn2_w_down": (ffn2_w_down, m_ffn2_w_down, v_ffn2_w_down),
    }
    big = {}
    after = grad_x
    for handles, names, tag in [
            (rs_ffn2, ["ffn2_w_gate", "ffn2_w_up", "ffn2_w_down"], "ffn2"),
            (rs_mix, ["w_out", "w_proj_dil", "w_proj_fox"], "mixer"),
            (rs_win, ["w_in"], "w_in"),
            (rs_ffn1, ["ffn1_w_gate", "ffn1_w_up", "ffn1_w_down"], "ffn1")]:
        res, after = update(handles, names, after, tag)
        big.update(res)

    def lanes(a):
        a = a.reshape(1, -1)
        return jnp.pad(a, ((0, 0), (0, d - a.shape[1])))

    small_names = ["ffn1_norm", "mix_norm", "b_gate_dil", "b_gate_fox", "ffn2_norm", "final_norm", "b_forget"]
    small_g = [d_ffn1_norm, d_mix_norm, d_bd, d_bf, d_ffn2_norm, d_final, d_bforget[:, :n_f]]
    small_w = [ffn1_norm, mix_norm, b_gate_dil, b_gate_fox, ffn2_norm, final_norm, b_forget]
    small_m = [m_ffn1_norm, m_mix_norm, m_b_gate_dil, m_b_gate_fox, m_ffn2_norm, m_final_norm, m_b_forget]
    small_v = [v_ffn1_norm, v_mix_norm, v_b_gate_dil, v_b_gate_fox, v_ffn2_norm, v_final_norm, v_b_forget]
    pack = lambda arrs, last: jnp.concatenate([lanes(a) for a in arrs] + [last], axis=0)
    g_all = _allreduce_small(pack(small_g, loss_lanes))
    zero_row = jnp.zeros((1, d), F32)
    one_row = jnp.ones((1, d), F32)
    s_delta, s_m, s_v = _adam_small(g_all, pack(small_w, zero_row), pack(small_m, zero_row), pack(small_v, one_row))
    loss = g_all[len(small_names), 0]

    def unpack(packed, i, like):
        return packed[i, :like.size].reshape(like.shape)

    small = {}
    for i, (n, w) in enumerate(zip(small_names, small_w)):
        small[n] = (unpack(g_all, i, w), unpack(s_delta, i, w), unpack(s_m, i, w), unpack(s_v, i, w))

    order = ["ffn1_norm", "ffn1_w_gate", "ffn1_w_up", "ffn1_w_down", "mix_norm", "w_in", "b_forget", "b_gate_dil",
             "b_gate_fox", "w_proj_dil", "w_proj_fox", "w_out", "ffn2_norm", "ffn2_w_gate", "ffn2_w_up",
             "ffn2_w_down", "final_norm"]
    res = {**big, **small}
    outs = [loss, grad_x[None]]
    for slot in range(4):
        outs += [res[n][slot] for n in order]
    return tuple(outs)
```

```python
import functools

import numpy as np
import jax
import jax.numpy as jnp
from jax import lax
from jax.experimental import pallas as pl
from jax.experimental.pallas import tpu as pltpu

BF = jnp.bfloat16
F32 = jnp.float32
MESH = pl.DeviceIdType.MESH
N_DEV = 8

HEAD_DIM = 128
ROPE_DIM = HEAD_DIM // 4
ROPE_HALF = ROPE_DIM // 2
ROPE_THETA = 500000.0
NORM_EPS = 1e-6
DIL_PATTERNS = ((128, 1), (512, 4), (2048, 16))
MAX_WINDOW = 2048
LANE = 128
NEG = -1e30

ADAM_LR = 0.001
ADAM_B1 = 0.9
ADAM_B2 = 0.999
ADAM_EPS = 1e-08
ADAM_WD = 0.01
ADAM_STEP = 10

VMEM_LIMIT_BYTES = 56 * 1024 * 1024
FFN_ROWS = 1024
DW_ROWS = 1024
ANY = pl.BlockSpec(memory_space=pl.ANY)

NN = (((1,), (0,)), ((), ()))
NT = (((1,), (1,)), ((), ()))
TN = (((0,), (0,)), ((), ()))


def _dot(a, b, dn=NN):
    return lax.dot_general(a, b, dn, preferred_element_type=F32)


def _sig(x):
    return 1.0 / (1.0 + jnp.exp(-x))


def _tile(n, pref, align):
    best = None
    t = align
    while t <= min(n, pref):
        if n % t == 0:
            best = t
        t += align
    return n if best is None else best


def _params():
    return pltpu.CompilerParams(vmem_limit_bytes=VMEM_LIMIT_BYTES)


def _call(body, args, dep=None, **kw):
    if dep is not None:
        n_in = len(args)
        inner = body

        def body(*refs):
            inner(*refs[:n_in], *refs[n_in + 1:])

        kw["in_specs"] = list(kw["in_specs"]) + [ANY]
        args = list(args) + [dep]
    return pl.pallas_call(body, **kw)(*args)


def _peers():
    x, y, c = lax.axis_index("x"), lax.axis_index("y"), lax.axis_index("c")
    me = 4 * x + 2 * y + c
    peers = []
    for k in range(1, N_DEV):
        px = 1 - x if (k >> 2) & 1 else x
        py = 1 - y if (k >> 1) & 1 else y
        pc = 1 - c if k & 1 else c
        peers.append((k, (px, py, pc), 4 * px + 2 * py + pc))
    return me, peers


HBM = pl.BlockSpec(memory_space=pltpu.HBM)
SEM = pl.BlockSpec(memory_space=pltpu.SEMAPHORE)
EFFECT = pltpu.SideEffectType.DATAFLOW_SIDE_EFFECTING


def _exchange_copy(gather, src_ref, land_ref, send_sems, recv_sems, me, k, peer, peer_flat, landing):
    return pltpu.make_async_remote_copy(
        src_ref=src_ref if gather else src_ref.at[peer_flat], dst_ref=land_ref.at[landing],
        send_sem=send_sems.at[k], recv_sem=recv_sems.at[k], device_id=peer, device_id_type=MESH)


ALL_PEERS = (1, 2, 3, 4, 5, 6, 7)
SIBLING = 1
SAME_CORE = (2, 4, 6)
FIRST_LEVEL = (SIBLING,) + SAME_CORE


def _exchange_start(srcs, gather, name, dep=None, ks=ALL_PEERS):
    n = len(srcs)
    extra = [] if dep is None else [dep]

    def body(*refs):
        src_refs, land_refs = refs[:n], refs[n:2 * n]
        refs = refs[2 * n + len(extra):]
        send_refs, recv_refs = refs[:n], refs[n:2 * n]
        token = refs[4 * n]
        me, peers = _peers()
        for i in range(n):
            for k, peer, peer_flat in peers:
                if k in ks:
                    _exchange_copy(gather, src_refs[i], land_refs[i], send_refs[i], recv_refs[i],
                                   me, k, peer, peer_flat, me).start()
        token[...] = jnp.zeros_like(token)

    lands = [lax.empty((N_DEV,) + s.shape[-2:], s.dtype) for s in srcs]
    sems = [pltpu.SemaphoreType.DMA((N_DEV,)) for _ in range(2 * n)]
    out = pl.pallas_call(
        body, name=name,
        out_shape=tuple(sems) + tuple(pltpu.HBM(a.shape, a.dtype) for a in list(srcs) + lands)
        + (jax.ShapeDtypeStruct((8, LANE), F32),),
        in_specs=[HBM] * (2 * n) + [ANY] * len(extra),
        out_specs=tuple([SEM] * (2 * n) + [HBM] * (2 * n) + [pl.BlockSpec(memory_space=pltpu.VMEM)]),
        input_output_aliases={i: 2 * n + i for i in range(2 * n)},
        compiler_params=pltpu.CompilerParams(has_side_effects=EFFECT),
    )(*[pltpu.with_memory_space_constraint(a, pltpu.HBM) for a in list(srcs) + lands], *extra)
    handles = [(out[2 * n + i], out[3 * n + i], out[i], out[n + i]) for i in range(n)]
    return handles, out[4 * n]


def _exchange_wait(handles, gather, after, name):
    n = len(handles)

    def body(*refs):
        src_refs, land_refs = refs[:n], refs[n:2 * n]
        send_refs, recv_refs = refs[2 * n:3 * n], refs[3 * n:4 * n]
        me, peers = _peers()
        for i in range(n):
            for k, peer, peer_flat in peers:
                cp = _exchange_copy(gather, src_refs[i], land_refs[i], send_refs[i], recv_refs[i],
                                    me, k, peer, peer_flat, peer_flat)
                cp.wait_send()
                cp.wait_recv()

    srcs = [h[0] for h in handles]
    lands = [h[1] for h in handles]
    out = pl.pallas_call(
        body, name=name,
        out_shape=tuple(pltpu.HBM(a.shape, a.dtype) for a in srcs + lands),
        in_specs=[HBM] * (2 * n) + [SEM] * (2 * n) + [ANY],
        out_specs=tuple([HBM] * (2 * n)),
        input_output_aliases={i: i for i in range(2 * n)},
        compiler_params=pltpu.CompilerParams(has_side_effects=EFFECT),
    )(*srcs, *lands, *[h[2] for h in handles], *[h[3] for h in handles], after)
    me = 4 * lax.axis_index("x") + 2 * lax.axis_index("y") + lax.axis_index("c")
    filled = []
    for src, land in zip(out[:n], out[n:]):
        own = src[None] if gather else lax.dynamic_slice_in_dim(src, me, 1, axis=0)
        filled.append(lax.dynamic_update_slice_in_dim(land, own, me, axis=0))
    return filled


def _gather_relay(handles, after, name):
    n = len(handles)

    def body(*refs):
        land_refs, recv_refs = refs[:n], refs[n:2 * n]
        refs = refs[2 * n + 1:]
        send2_refs, recv2_refs = refs[n:2 * n], refs[2 * n:3 * n]
        me, peers = _peers()
        sibling = peers[SIBLING - 1][1]
        for i in range(n):
            for k, peer, peer_flat in peers:
                if k in SAME_CORE:
                    block = land_refs[i].at[peer_flat]
                    pltpu.make_async_remote_copy(
                        src_ref=block, dst_ref=block, send_sem=send2_refs[i].at[k], recv_sem=recv_refs[i].at[k],
                        device_id=peer, device_id_type=MESH).wait_recv()
                    pltpu.make_async_remote_copy(
                        src_ref=block, dst_ref=block, send_sem=send2_refs[i].at[k], recv_sem=recv2_refs[i].at[k],
                        device_id=sibling, device_id_type=MESH).start()

    lands = [h[1] for h in handles]
    sems = [pltpu.SemaphoreType.DMA((N_DEV,)) for _ in range(2 * n)]
    out = pl.pallas_call(
        body, name=name,
        out_shape=tuple(pltpu.HBM(a.shape, a.dtype) for a in lands) + tuple(sems),
        in_specs=[HBM] * n + [SEM] * n + [ANY],
        out_specs=tuple([HBM] * n + [SEM] * (2 * n)),
        input_output_aliases={i: i for i in range(n)},
        compiler_params=pltpu.CompilerParams(has_side_effects=EFFECT),
    )(*lands, *[h[3] for h in handles], after)
    return [(h[0], out[i], h[2], h[3], out[n + i], out[2 * n + i]) for i, h in enumerate(handles)]


def _gather_wait(handles, after, name):
    n = len(handles)

    def body(*refs):
        src_refs, land_refs = refs[:n], refs[n:2 * n]
        send_refs, recv_refs = refs[2 * n:3 * n], refs[3 * n:4 * n]
        send2_refs, recv2_refs = refs[4 * n:5 * n], refs[5 * n:6 * n]
        me, peers = _peers()
        _, sibling, sibling_flat = peers[SIBLING - 1]
        for i in range(n):
            for k, peer, peer_flat in peers:
                if k in FIRST_LEVEL:
                    cp = _exchange_copy(True, src_refs[i], land_refs[i], send_refs[i], recv_refs[i],
                                        me, k, peer, peer_flat, peer_flat)
                    cp.wait_send()
                    if k == SIBLING:
                        cp.wait_recv()
                if k in SAME_CORE:
                    mine = land_refs[i].at[peer_flat]
                    theirs = land_refs[i].at[peer_flat ^ SIBLING]
                    cp = pltpu.make_async_remote_copy(
                        src_ref=mine, dst_ref=theirs, send_sem=send2_refs[i].at[k], recv_sem=recv2_refs[i].at[k],
                        device_id=sibling, device_id_type=MESH)
                    cp.wait_send()
                    cp.wait_recv()

    srcs = [h[0] for h in handles]
    lands = [h[1] for h in handles]
    out = pl.pallas_call(
        body, name=name,
        out_shape=tuple(pltpu.HBM(a.shape, a.dtype) for a in srcs + lands),
        in_specs=[HBM] * (2 * n) + [SEM] * (4 * n) + [ANY],
        out_specs=tuple([HBM] * (2 * n)),
        input_output_aliases={i: i for i in range(2 * n)},
        compiler_params=pltpu.CompilerParams(has_side_effects=EFFECT),
    )(*srcs, *lands, *[h[2] for h in handles], *[h[3] for h in handles],
      *[h[4] for h in handles], *[h[5] for h in handles], after)
    me = 4 * lax.axis_index("x") + 2 * lax.axis_index("y") + lax.axis_index("c")
    return [lax.dynamic_update_slice_in_dim(land, src[None], me, axis=0) for src, land in zip(out[:n], out[n:])]


def _allreduce_small(p):
    rows, d = p.shape

    def body(p_ref, o_ref, recv_ref, send_sems, recv_sems):
        me, peers = _peers()
        recv_ref[me] = p_ref[...]
        sends = []
        for k, peer, peer_flat in peers:
            cp = pltpu.make_async_remote_copy(
                src_ref=p_ref, dst_ref=recv_ref.at[me],
                send_sem=send_sems.at[k], recv_sem=recv_sems.at[k],
                device_id=peer, device_id_type=MESH)
            cp.start()
            sends.append(cp)
        for k, peer, peer_flat in peers:
            pltpu.make_async_remote_copy(
                src_ref=p_ref, dst_ref=recv_ref.at[peer_flat],
                send_sem=send_sems.at[k], recv_sem=recv_sems.at[k],
                device_id=peer, device_id_type=MESH).wait_recv()
        for cp in sends:
            cp.wait_send()
        acc = recv_ref[0]
        for s in range(1, N_DEV):
            acc = acc + recv_ref[s]
        is_loss = lax.broadcasted_iota(jnp.int32, (rows, d), 0) == rows - 1
        total = jnp.sum(jnp.where(is_loss, acc, 0.0))
        o_ref[...] = jnp.where(is_loss, total, acc)

    return pl.pallas_call(
        body, name="allreduce_small",
        out_shape=jax.ShapeDtypeStruct((rows, d), F32),
        in_specs=[pl.BlockSpec(memory_space=pltpu.VMEM)],
        out_specs=pl.BlockSpec(memory_space=pltpu.VMEM),
        scratch_shapes=[pltpu.VMEM((N_DEV, rows, d), F32),
                        pltpu.SemaphoreType.DMA((N_DEV,)), pltpu.SemaphoreType.DMA((N_DEV,))],
    )(p)


def _adam_math(w, g, m, v):
    m2 = ADAM_B1 * m + (1.0 - ADAM_B1) * g
    v2 = ADAM_B2 * v + (1.0 - ADAM_B2) * (g * g)
    m_hat = m2 / (1.0 - ADAM_B1 ** ADAM_STEP)
    v_hat = v2 / (1.0 - ADAM_B2 ** ADAM_STEP)
    delta = -ADAM_LR * (m_hat / (jnp.sqrt(v_hat) + ADAM_EPS) + ADAM_WD * w)
    return delta, m2, v2


def _adam_from_partials(parts, w, m, v, name):
    r, c = w.shape
    tr = _tile(r, 256, 16)

    def body(p_ref, w_ref, m_ref, v_ref, g_out, d_out, m_out, v_out):
        g = p_ref[0].astype(F32)
        for s in range(1, N_DEV):
            g = g + p_ref[s].astype(F32)
        delta, m2, v2 = _adam_math(w_ref[...], g, m_ref[...], v_ref[...])
        g_out[...] = g
        d_out[...] = delta
        m_out[...] = m2
        v_out[...] = v2

    blk = pl.BlockSpec((tr, c), lambda i: (i, 0))
    out = jax.ShapeDtypeStruct((r, c), F32)
    return pl.pallas_call(
        body, name=name, grid=(r // tr,),
        in_specs=[pl.BlockSpec((N_DEV, tr, c), lambda i: (0, i, 0)), blk, blk, blk],
        out_specs=[blk, blk, blk, blk], out_shape=[out, out, out, out],
        compiler_params=_params(),
    )(parts, w, m, v)


def _adam_small(g, w, m, v):
    def body(g_ref, w_ref, m_ref, v_ref, d_out, m_out, v_out):
        delta, m2, v2 = _adam_math(w_ref[...], g_ref[...], m_ref[...], v_ref[...])
        d_out[...] = delta
        m_out[...] = m2
        v_out[...] = v2

    out = jax.ShapeDtypeStruct(g.shape, F32)
    return pl.pallas_call(body, name="adam_small", out_shape=[out, out, out])(g, w, m, v)


def _rms_fwd(x, gain, name, dep=None, with_transpose=False):
    t, d = x.shape
    tr = _tile(t, 256, LANE)

    def body(x_ref, g_ref, o_ref, *ot_ref):
        xv = x_ref[...]
        r = lax.rsqrt(jnp.mean(xv * xv, axis=-1, keepdims=True) + NORM_EPS)
        y = xv * r * g_ref[...]
        o_ref[...] = y.astype(BF)
        if with_transpose:
            ot_ref[0][...] = jnp.transpose(y).astype(BF)

    out_specs = [pl.BlockSpec((tr, d), lambda i: (i, 0))]
    out_shape = [jax.ShapeDtypeStruct((t, d), BF)]
    if with_transpose:
        out_specs.append(pl.BlockSpec((d, tr), lambda i: (0, i)))
        out_shape.append(jax.ShapeDtypeStruct((d, t), BF))
    return _call(
        body, [x, gain], dep=dep, name=name, grid=(t // tr,),
        in_specs=[pl.BlockSpec((tr, d), lambda i: (i, 0)), pl.BlockSpec((1, d), lambda i: (0, 0))],
        out_specs=out_specs, out_shape=out_shape, compiler_params=_params(),
    )


def _rms_vjp(xv, gain, dy):
    r = lax.rsqrt(jnp.mean(xv * xv, axis=-1, keepdims=True) + NORM_EPS)
    xhat = xv * r
    dxhat = dy * gain
    dx = r * (dxhat - xhat * jnp.mean(dxhat * xhat, axis=-1, keepdims=True))
    dgain = jnp.sum(dy * xhat, axis=0, keepdims=True)
    return dx, dgain


def _loss_head(x, gain, target):
    t, d = x.shape
    tr = _tile(t, 256, 16)

    def body(x_ref, g_ref, t_ref, dx_ref, dxb_ref, dg_ref, loss_ref):
        xv = x_ref[...]
        gain = g_ref[...]
        r = lax.rsqrt(jnp.mean(xv * xv, axis=-1, keepdims=True) + NORM_EPS)
        err = xv * r * gain - t_ref[...]
        dx, dgain = _rms_vjp(xv, gain, err * (1.0 / d))
        dx_ref[...] = dx
        dxb_ref[...] = dx.astype(BF)

        @pl.when(pl.program_id(0) == 0)
        def _():
            dg_ref[...] = jnp.zeros_like(dg_ref)
            loss_ref[...] = jnp.zeros_like(loss_ref)

        dg_ref[...] += dgain
        loss_ref[...] += jnp.sum(err * err, axis=0, keepdims=True) * (0.5 / d)

    row = pl.BlockSpec((tr, d), lambda i: (i, 0))
    vec = pl.BlockSpec((1, d), lambda i: (0, 0))
    return pl.pallas_call(
        body, name="loss_head", grid=(t // tr,),
        in_specs=[row, vec, row], out_specs=[row, row, vec, vec],
        out_shape=[jax.ShapeDtypeStruct((t, d), F32), jax.ShapeDtypeStruct((t, d), BF),
                   jax.ShapeDtypeStruct((1, d), F32), jax.ShapeDtypeStruct((1, d), F32)],
        compiler_params=_params(),
    )(x, gain, target)


def _mm_nn(a, b, out_dtype, name, residual=None, tm_pref=512, tn_pref=1152):
    m, k = a.shape
    n = b.shape[1]
    tm, tn = _tile(m, tm_pref, 16), _tile(n, tn_pref, LANE)

    def body(*refs):
        if residual is None:
            a_ref, b_ref, o_ref = refs
            o_ref[...] = _dot(a_ref[...], b_ref[...]).astype(out_dtype)
        else:
            a_ref, b_ref, r_ref, o_ref = refs
            o_ref[...] = (r_ref[...] + _dot(a_ref[...], b_ref[...])).astype(out_dtype)

    in_specs = [pl.BlockSpec((tm, k), lambda j, i: (i, 0)), pl.BlockSpec((k, tn), lambda j, i: (0, j))]
    args = [a, b]
    if residual is not None:
        in_specs.append(pl.BlockSpec((tm, tn), lambda j, i: (i, j)))
        args.append(residual)
    return pl.pallas_call(
        body, name=name, grid=(n // tn, m // tm), in_specs=in_specs,
        out_specs=pl.BlockSpec((tm, tn), lambda j, i: (i, j)),
        out_shape=jax.ShapeDtypeStruct((m, n), out_dtype), compiler_params=_params(),
    )(*args)


def _rms_bwd_tail(dy_ref, first, x_ref, g_ref, dres_ref, dx_ref, dxb_ref, dg_ref):
    @pl.when(first)
    def _():
        dg_ref[...] = jnp.zeros_like(dg_ref)

    gain = g_ref[...]
    for r in range(0, dy_ref.shape[0], LANE):
        rows = pl.ds(r, min(LANE, dy_ref.shape[0] - r))
        dx, dgain = _rms_vjp(x_ref[rows, :], gain, dy_ref[rows, :])
        dx = dx + dres_ref[rows, :]
        dx_ref[rows, :] = dx
        dxb_ref[rows, :] = dx.astype(BF)
        dg_ref[...] += dgain


def _mm_nt(a, b, out_dtype, name, tm_pref=512, tn_pref=1024, tk_pref=2048, rms=None, dep=None):
    m, k = a.shape
    n = b.shape[0]
    tm, tn, tk = _tile(m, tm_pref, 16), _tile(n, tn_pref, LANE), _tile(k, tk_pref, LANE)
    nk = k // tk
    assert rms is None or tn == n

    def body(*refs):
        if rms is None:
            a_ref, b_ref, o_ref, acc_ref = refs
        else:
            a_ref, b_ref, x_ref, g_ref, dres_ref, dx_ref, dxb_ref, dg_ref, acc_ref = refs
        kk = pl.program_id(2)

        @pl.when(kk == 0)
        def _():
            acc_ref[...] = jnp.zeros_like(acc_ref)

        acc_ref[...] += _dot(a_ref[...], b_ref[...], NT)

        @pl.when(kk == nk - 1)
        def _():
            if rms is None:
                o_ref[...] = acc_ref[...].astype(out_dtype)
            else:
                _rms_bwd_tail(acc_ref, pl.program_id(1) == 0, x_ref, g_ref, dres_ref, dx_ref, dxb_ref, dg_ref)

    in_specs = [pl.BlockSpec((tm, tk), lambda j, i, kk: (i, kk)), pl.BlockSpec((tn, tk), lambda j, i, kk: (j, kk))]
    row = pl.BlockSpec((tm, tn), lambda j, i, kk: (i, j))
    if rms is None:
        args, out_specs, out_shape = [a, b], row, jax.ShapeDtypeStruct((m, n), out_dtype)
    else:
        vec = pl.BlockSpec((1, n), lambda j, i, kk: (0, 0))
        args, in_specs = [a, b, *rms], in_specs + [row, vec, row]
        out_specs = [row, row, vec]
        out_shape = [jax.ShapeDtypeStruct((m, n), F32), jax.ShapeDtypeStruct((m, n), BF),
                     jax.ShapeDtypeStruct((1, n), F32)]
    return _call(
        body, args, dep=dep, name=name, grid=(n // tn, m // tm, nk), in_specs=in_specs, out_specs=out_specs,
        out_shape=out_shape, scratch_shapes=[pltpu.VMEM((tm, tn), F32)], compiler_params=_params(),
    )


def _mm_tn(a, b, out_dtype, name, tn_pref=1152, tk_pref=512, a_transposed=False):
    (k, t) = a.shape if a_transposed else a.shape[::-1]
    n = b.shape[1]
    tn, tk = _tile(n, tn_pref, LANE), _tile(t, tk_pref, LANE if a_transposed else 16)
    nt = t // tk

    def body(a_ref, b_ref, o_ref, acc_ref):
        tt = pl.program_id(1)

        @pl.when(tt == 0)
        def _():
            acc_ref[...] = jnp.zeros_like(acc_ref)

        acc_ref[...] += _dot(a_ref[...], b_ref[...], NN if a_transposed else TN)

        @pl.when(tt == nt - 1)
        def _():
            o_ref[...] = acc_ref[...].astype(out_dtype)

    if a_transposed:
        a_spec = pl.BlockSpec((k, tk), lambda j, tt: (0, tt))
    else:
        a_spec = pl.BlockSpec((tk, k), lambda j, tt: (tt, 0))
    return pl.pallas_call(
        body, name=name, grid=(n // tn, nt),
        in_specs=[a_spec, pl.BlockSpec((tk, tn), lambda j, tt: (tt, j))],
        out_specs=pl.BlockSpec((k, tn), lambda j, tt: (0, j)),
        out_shape=jax.ShapeDtypeStruct((k, n), out_dtype),
        scratch_shapes=[pltpu.VMEM((k, tn), F32)], compiler_params=_params(),
    )(a, b)


FFN_COLS = 512


def _ffn_tiles(t, fc):
    return _tile(t, FFN_ROWS, 16), _tile(fc, FFN_COLS, LANE)


def _ffn_gate_up(hn, wg_t, wu_t, name):
    t, d = hn.shape
    fc = wg_t.shape[0]
    tm, tn = _ffn_tiles(t, fc)

    def body(h_ref, wg_ref, wu_ref, g_ref, u_ref, a_ref):
        h = h_ref[...]
        g = _dot(h, wg_ref[...], NT)
        u = _dot(h, wu_ref[...], NT)
        g_ref[...] = g.astype(BF)
        u_ref[...] = u.astype(BF)
        a_ref[...] = (g * _sig(g) * u).astype(BF)

    wspec = pl.BlockSpec((tn, d), lambda j, i: (j, 0))
    hid = pl.BlockSpec((tm, tn), lambda j, i: (i, j))
    out = jax.ShapeDtypeStruct((t, fc), BF)
    return pl.pallas_call(
        body, name=name, grid=(fc // tn, t // tm),
        in_specs=[pl.BlockSpec((tm, d), lambda j, i: (i, 0)), wspec, wspec],
        out_specs=[hid, hid, hid], out_shape=[out, out, out], compiler_params=_params(),
    )(hn, wg_t, wu_t)


def _ffn_gate(hn, wg_t, name):
    t, d = hn.shape
    fc = wg_t.shape[0]
    tm, tn = _ffn_tiles(t, fc)

    def body(h_ref, wg_ref, g_ref):
        g_ref[...] = _dot(h_ref[...], wg_ref[...], NT)

    return pl.pallas_call(
        body, name=name, grid=(fc // tn, t // tm),
        in_specs=[pl.BlockSpec((tm, d), lambda j, i: (i, 0)), pl.BlockSpec((tn, d), lambda j, i: (j, 0))],
        out_specs=pl.BlockSpec((tm, tn), lambda j, i: (i, j)),
        out_shape=jax.ShapeDtypeStruct((t, fc), F32), compiler_params=_params(),
    )(hn, wg_t)


def _ffn_up_act(hn, wu_t, g, name):
    t, d = hn.shape
    fc = wu_t.shape[0]
    tm, tn = _ffn_tiles(t, fc)

    def body(h_ref, wu_ref, g_ref, gb_ref, u_ref, a_ref):
        u = _dot(h_ref[...], wu_ref[...], NT)
        gv = g_ref[...]
        gb_ref[...] = gv.astype(BF)
        u_ref[...] = u.astype(BF)
        a_ref[...] = (gv * _sig(gv) * u).astype(BF)

    hid = pl.BlockSpec((tm, tn), lambda j, i: (i, j))
    out = jax.ShapeDtypeStruct((t, fc), BF)
    return pl.pallas_call(
        body, name=name, grid=(fc // tn, t // tm),
        in_specs=[pl.BlockSpec((tm, d), lambda j, i: (i, 0)), pl.BlockSpec((tn, d), lambda j, i: (j, 0)), hid],
        out_specs=[hid, hid, hid], out_shape=[out, out, out], compiler_params=_params(),
    )(hn, wu_t, g)


def _ffn_down(act, wd, xres, name):
    t, fc = act.shape
    d = wd.shape[1]
    tm, tk = _ffn_tiles(t, fc)

    def body(a_ref, w_ref, x_ref, o_ref):
        @pl.when(pl.program_id(1) == 0)
        def _():
            o_ref[...] = x_ref[...]

        o_ref[...] += 0.5 * _dot(a_ref[...], w_ref[...])

    row = pl.BlockSpec((tm, d), lambda i, k: (i, 0))
    return pl.pallas_call(
        body, name=name, grid=(t // tm, fc // tk),
        in_specs=[pl.BlockSpec((tm, tk), lambda i, k: (i, k)), pl.BlockSpec((tk, d), lambda i, k: (k, 0)), row],
        out_specs=row, out_shape=jax.ShapeDtypeStruct((t, d), F32), compiler_params=_params(),
    )(act, wd, xres)


def _ffn_bwd_hidden(dxb, wd, g, u, name):
    t, d = dxb.shape
    fc = wd.shape[0]
    tm, tn = _ffn_tiles(t, fc)

    def body(dx_ref, w_ref, g_ref, u_ref, dg_ref, du_ref):
        dh = 0.5 * _dot(dx_ref[...], w_ref[...], NT)
        gv = g_ref[...].astype(F32)
        uv = u_ref[...].astype(F32)
        s = _sig(gv)
        dg_ref[...] = (dh * uv * (s * (1.0 + gv * (1.0 - s)))).astype(BF)
        du_ref[...] = (dh * (gv * s)).astype(BF)

    hid = pl.BlockSpec((tm, tn), lambda j, i: (i, j))
    out = jax.ShapeDtypeStruct((t, fc), BF)
    return pl.pallas_call(
        body, name=name, grid=(fc // tn, t // tm),
        in_specs=[pl.BlockSpec((tm, d), lambda j, i: (i, 0)), pl.BlockSpec((tn, d), lambda j, i: (j, 0)), hid, hid],
        out_specs=[hid, hid], out_shape=[out, out], compiler_params=_params(),
    )(dxb, wd, g, u)


def _ffn_dw(lhs, rhs, scale, name, dep=None):
    n = len(lhs)
    t, fc = lhs[0].shape
    d = rhs.shape[1]
    tk, tn = _tile(t, DW_ROWS, 16), _tile(fc, FFN_COLS, LANE)
    nt = t // tk

    def body(*refs):
        l_refs, r_ref, o_refs, acc_refs = refs[:n], refs[n], refs[n + 1:2 * n + 1], refs[2 * n + 1:]
        tt = pl.program_id(1)
        r = r_ref[...]
        for l_ref, o_ref, acc_ref in zip(l_refs, o_refs, acc_refs):
            @pl.when(tt == 0)
            def _():
                acc_ref[...] = jnp.zeros_like(acc_ref)

            acc_ref[...] += _dot(l_ref[...], r, TN)

            @pl.when(tt == nt - 1)
            def _():
                o_ref[...] = (scale * acc_ref[...]).astype(BF)

    lspec = pl.BlockSpec((tk, tn), lambda j, tt: (tt, j))
    ospec = pl.BlockSpec((tn, d), lambda j, tt: (j, 0))
    out = jax.ShapeDtypeStruct((fc, d), BF)
    return _call(
        body, [*lhs, rhs], dep=dep, name=name, grid=(fc // tn, nt),
        in_specs=[lspec] * n + [pl.BlockSpec((tk, d), lambda j, tt: (tt, 0))],
        out_specs=[ospec] * n, out_shape=[out] * n,
        scratch_shapes=[pltpu.VMEM((tn, d), F32)] * n, compiler_params=_params(),
    )


def _rms_bwd(dy, x, gain, dres, name):
    t, d = x.shape
    tr = _tile(t, 256, 16)

    def body(dy_ref, x_ref, g_ref, dres_ref, dx_ref, dxb_ref, dg_ref):
        _rms_bwd_tail(dy_ref, pl.program_id(0) == 0, x_ref, g_ref, dres_ref, dx_ref, dxb_ref, dg_ref)

    row = pl.BlockSpec((tr, d), lambda i: (i, 0))
    vec = pl.BlockSpec((1, d), lambda i: (0, 0))
    return pl.pallas_call(
        body, name=name, grid=(t // tr,),
        in_specs=[row, row, vec, row], out_specs=[row, row, vec],
        out_shape=[jax.ShapeDtypeStruct((t, d), F32), jax.ShapeDtypeStruct((t, d), BF),
                   jax.ShapeDtypeStruct((1, d), F32)],
        compiler_params=_params(),
    )(dy, x, gain, dres)


def _ffn_bwd_input(dg, du, wg_t, wu_t, name, dep=None):
    t, fc = dg.shape
    d = wg_t.shape[1]
    tm, tk = _ffn_tiles(t, fc)

    def body(dg_ref, du_ref, wg_ref, wu_ref, o_ref):
        @pl.when(pl.program_id(1) == 0)
        def _():
            o_ref[...] = jnp.zeros_like(o_ref)

        o_ref[...] += _dot(dg_ref[...], wg_ref[...]) + _dot(du_ref[...], wu_ref[...])

    hid = pl.BlockSpec((tm, tk), lambda i, k: (i, k))
    wspec = pl.BlockSpec((tk, d), lambda i, k: (k, 0))
    return _call(
        body, [dg, du, wg_t, wu_t], dep=dep, name=name, grid=(t // tm, fc // tk),
        in_specs=[hid, hid, wspec, wspec],
        out_specs=pl.BlockSpec((tm, d), lambda i, k: (i, 0)),
        out_shape=jax.ShapeDtypeStruct((t, d), F32), compiler_params=_params(),
    )


def _rope_tables(t):
    pos = jnp.arange(t, dtype=F32)
    inv_freq = ROPE_THETA ** (-jnp.arange(0, ROPE_DIM, 2, dtype=F32) / ROPE_DIM)
    ang = pos[:, None] * inv_freq[None, :]
    cos, sin = jnp.cos(ang), jnp.sin(ang)
    rest = HEAD_DIM - ROPE_DIM
    one = jnp.ones((t, rest), F32)
    zero_h = jnp.zeros((t, ROPE_HALF), F32)
    zero_r = jnp.zeros((t, rest), F32)
    c = jnp.concatenate([cos, cos, one], axis=1)
    s1 = jnp.concatenate([-sin, zero_h, zero_r], axis=1)
    s2 = jnp.concatenate([zero_h, sin, zero_r], axis=1)
    return c, s1, s2


def _rope(xh, c, s1, s2):
    return xh * c + pltpu.roll(xh, HEAD_DIM - ROPE_HALF, 1) * s1 + pltpu.roll(xh, ROPE_HALF, 1) * s2


def _rope_t(dh, c, s1, s2):
    return dh * c + pltpu.roll(dh * s1, ROPE_HALF, 1) + pltpu.roll(dh * s2, HEAD_DIM - ROPE_HALF, 1)


def _mixer_prep(proj, tables, bf_pad, hd, scale):
    t, np_ = proj.shape
    tr = _tile(t, 256, 16)
    nh = hd // HEAD_DIM
    nblk = hd // LANE
    f_blk = np_ // LANE - 1

    def body(qd_ref, kd_ref, vd_ref, qf_ref, kf_ref, vf_ref, fl_ref, c_ref, s1_ref, s2_ref, b_ref,
             oqd, okd, ovd, oqf, okf, ovf, olog):
        c, s1, s2 = c_ref[...], s1_ref[...], s2_ref[...]
        for h in range(nh):
            sl = slice(h * HEAD_DIM, (h + 1) * HEAD_DIM)
            oqd[:, sl] = (_rope(qd_ref[:, sl], c, s1, s2) * scale).astype(BF)
            okd[:, sl] = _rope(kd_ref[:, sl], c, s1, s2).astype(BF)
        ovd[...] = vd_ref[...].astype(BF)
        oqf[...] = (qf_ref[...] * scale).astype(BF)
        okf[...] = kf_ref[...].astype(BF)
        ovf[...] = vf_ref[...].astype(BF)
        z = fl_ref[...] + b_ref[...]
        olog[...] = jnp.minimum(z, 0.0) - jnp.log(1.0 + jnp.exp(-jnp.abs(z)))

    def col(kblk):
        return pl.BlockSpec((tr, hd), lambda i, kblk=kblk: (i, kblk))

    lane_row = pl.BlockSpec((tr, LANE), lambda i: (i, 0))
    in_specs = [col(0), col(1), col(2), col(3), col(4), col(5),
                pl.BlockSpec((tr, LANE), lambda i: (i, f_blk)),
                lane_row, lane_row, lane_row, pl.BlockSpec((1, LANE), lambda i: (0, 0))]
    o = pl.BlockSpec((tr, hd), lambda i: (i, 0))
    ob = jax.ShapeDtypeStruct((t, hd), BF)
    del nblk
    return pl.pallas_call(
        body, name="mixer_prep", grid=(t // tr,), in_specs=in_specs,
        out_specs=[o, o, o, o, o, o, lane_row],
        out_shape=[ob, ob, ob, ob, ob, ob, jax.ShapeDtypeStruct((t, LANE), F32)],
        compiler_params=_params(),
    )(proj, proj, proj, proj, proj, proj, proj, *tables, bf_pad)


def _split3(x):
    x1 = x.astype(BF)
    r1 = x - x1.astype(F32)
    x2 = r1.astype(BF)
    x3 = (r1 - x2.astype(F32)).astype(BF)
    return x1, x2, x3


def _cumsum_rows(x, reverse, name):
    t, w = x.shape
    blk = LANE
    nb = t // blk

    def body(x_ref, o_ref):
        r = lax.broadcasted_iota(jnp.int32, (blk, blk), 0)
        c = lax.broadcasted_iota(jnp.int32, (blk, blk), 1)
        tri = jnp.where((c >= r) if reverse else (c <= r), 1.0, 0.0).astype(BF)

        def step(i, carry):
            b = (nb - 1 - i) if reverse else i
            off = pl.multiple_of(b * blk, blk)
            xb = x_ref[pl.ds(off, blk), :]
            x1, x2, x3 = _split3(xb)
            o_ref[pl.ds(off, blk), :] = _dot(tri, x1) + _dot(tri, x2) + _dot(tri, x3) + carry
            return carry + jnp.sum(xb, axis=0, keepdims=True)

        lax.fori_loop(0, nb, step, jnp.zeros((1, w), F32))

    return pl.pallas_call(body, name=name, out_shape=jax.ShapeDtypeStruct((t, w), F32),
                          compiler_params=_params())(x)


ATTN_ROWS = 16


def _dil_bias_tiles(tq):
    nbias = MAX_WINDOW // tq + 1
    b = lax.broadcasted_iota(jnp.int32, (nbias, tq, tq), 0)
    i = lax.broadcasted_iota(jnp.int32, (nbias, tq, tq), 1)
    j = lax.broadcasted_iota(jnp.int32, (nbias, tq, tq), 2)
    delta = b * tq + i - j
    mult = jnp.zeros((nbias, tq, tq), F32)
    for w, dil in DIL_PATTERNS:
        mult = mult + jnp.where((delta >= 0) & (delta <= w) & (delta % dil == 0), 1.0, 0.0)
    return jnp.where(mult > 0.0, jnp.log(jnp.maximum(mult, 1.0)), NEG)


def _rep(x, width):
    return jnp.tile(x, (1, width // LANE))


def _chunks(n_rows, fn):
    for c in range(n_rows // ATTN_ROWS):
        fn(c * ATTN_ROWS)


def _causal(r0, tq, transposed):
    a = lax.broadcasted_iota(jnp.int32, (ATTN_ROWS, tq), 0) + r0
    b = lax.broadcasted_iota(jnp.int32, (ATTN_ROWS, tq), 1)
    return (a <= b) if transposed else (b <= a)


def _rows8(x):
    return jnp.transpose(x)[:8, :]


def _attn_fwd(mode, q, k, v, bias, tq, name):
    t, hd = q.shape
    nh = hd // HEAD_DIM
    nb = t // tq
    wb = MAX_WINDOW // tq
    fox = mode == "fox"

    def body(q_ref, k_ref, v_ref, b_ref, o_ref, lse_ref, lse_row_ref, s_ref, p_ref, m_ref, l_ref, acc_ref):
        qi = pl.program_id(1)
        qb = q_ref[...]
        m_ref[...] = jnp.full_like(m_ref, NEG)
        l_ref[...] = jnp.zeros_like(l_ref)
        acc_ref[...] = jnp.zeros_like(acc_ref)

        def tile(kj, diag):
            off = pl.multiple_of(kj * tq, tq)
            s_ref[...] = _dot(qb, k_ref[pl.ds(off, tq), :], NT)
            if fox:
                brow = b_ref[qi][:, :1] - b_ref[kj]

            def chunk(r0):
                rows = pl.ds(r0, ATTN_ROWS)
                if fox:
                    s = s_ref[rows, :] + brow
                    if diag:
                        s = jnp.where(_causal(r0, tq, False), s, NEG)
                else:
                    s = s_ref[rows, :] + b_ref[qi - kj, rows, :]
                m_old = m_ref[rows, :]
                m_new = jnp.maximum(m_old, jnp.max(s, axis=1, keepdims=True))
                p = jnp.exp(s - _rep(m_new, tq))
                alpha = jnp.exp(m_old - m_new)
                l_ref[rows, :] = alpha * l_ref[rows, :] + jnp.sum(p, axis=1, keepdims=True)
                m_ref[rows, :] = m_new
                acc_ref[rows, :] = alpha * acc_ref[rows, :]
                p_ref[rows, :] = p.astype(BF)

            _chunks(tq, chunk)
            acc_ref[...] += _dot(p_ref[...], v_ref[pl.ds(off, tq), :])

        tile(qi, True)
        if fox:
            lax.fori_loop(0, qi, lambda kj, c: (tile(kj, False), c)[1], 0)
        else:
            lax.fori_loop(1, jnp.minimum(qi, wb) + 1, lambda i, c: (tile(qi - i, False), c)[1], 0)
        o_ref[...] = (acc_ref[...] / l_ref[...]).astype(BF)
        lse = m_ref[...] + jnp.log(l_ref[...])
        lse_ref[...] = lse
        lse_row_ref[...] = _rows8(lse)

    qspec = pl.BlockSpec((tq, HEAD_DIM), lambda h, i: (i, h))
    kvspec = pl.BlockSpec((t, HEAD_DIM), lambda h, i: (0, h))
    repspec = pl.BlockSpec((None, tq, LANE), lambda h, i: (h, i, 0))
    row8spec = pl.BlockSpec((None, None, 8, tq), lambda h, i: (h, i, 0, 0))
    if fox:
        bspec = pl.BlockSpec((None, nb, 1, tq), lambda h, i: (h, 0, 0, 0))
    else:
        bspec = pl.BlockSpec((wb + 1, tq, tq), lambda h, i: (0, 0, 0))
    return pl.pallas_call(
        body, name=name, grid=(nh, nb), in_specs=[qspec, kvspec, kvspec, bspec],
        out_specs=[qspec, repspec, row8spec],
        out_shape=[jax.ShapeDtypeStruct((t, hd), BF), jax.ShapeDtypeStruct((nh, t, LANE), F32),
                   jax.ShapeDtypeStruct((nh, nb, 8, tq), F32)],
        scratch_shapes=[pltpu.VMEM((tq, tq), F32), pltpu.VMEM((tq, tq), BF), pltpu.VMEM((tq, LANE), F32),
                        pltpu.VMEM((tq, LANE), F32), pltpu.VMEM((tq, HEAD_DIM), F32)],
        compiler_params=_params(),
    )(q, k, v, bias)


def _attn_bwd_dq(mode, q, k, v, o, do, lse, bias, tq, name, dep=None):
    t, hd = q.shape
    nh = hd // HEAD_DIM
    nb = t // tq
    wb = MAX_WINDOW // tq
    fox = mode == "fox"

    def body(q_ref, k_ref, v_ref, o_ref, do_ref, lse_ref, b_ref, dq_ref, dl_row_ref,
             s_ref, dp_ref, x_ref, y_ref, acc_ref, acc2_ref, dl_ref):
        qi = pl.program_id(1)
        qb = q_ref[...]
        dob = do_ref[...]
        acc_ref[...] = jnp.zeros_like(acc_ref)
        if fox:
            acc2_ref[...] = jnp.zeros_like(acc2_ref)
            dl_ref[...] = jnp.zeros_like(dl_ref)
        else:
            prod = o_ref[...].astype(F32) * dob.astype(F32)
            dl_ref[...] = jnp.broadcast_to(jnp.sum(prod, axis=1, keepdims=True), (tq, LANE))

        def tile(kj, diag):
            off = pl.multiple_of(kj * tq, tq)
            kb = k_ref[pl.ds(off, tq), :]
            s_ref[...] = _dot(qb, kb, NT)
            dp_ref[...] = _dot(dob, v_ref[pl.ds(off, tq), :], NT)
            if fox:
                brow = b_ref[qi][:, :1] - b_ref[kj]

            def chunk(r0):
                rows = pl.ds(r0, ATTN_ROWS)
                lse_c = _rep(lse_ref[rows, :], tq)
                if fox:
                    s = s_ref[rows, :] + brow
                    if diag:
                        s = jnp.where(_causal(r0, tq, False), s, NEG)
                    p = jnp.exp(s - lse_c)
                    pdp = p * dp_ref[rows, :]
                    dl_ref[rows, :] += jnp.sum(pdp, axis=1, keepdims=True)
                    x_ref[rows, :] = pdp.astype(BF)
                    y_ref[rows, :] = p.astype(BF)
                else:
                    p = jnp.exp(s_ref[rows, :] + b_ref[qi - kj, rows, :] - lse_c)
                    x_ref[rows, :] = (p * (dp_ref[rows, :] - _rep(dl_ref[rows, :], tq))).astype(BF)

            _chunks(tq, chunk)
            acc_ref[...] += _dot(x_ref[...], kb)
            if fox:
                acc2_ref[...] += _dot(y_ref[...], kb)

        tile(qi, True)
        if fox:
            lax.fori_loop(0, qi, lambda kj, c: (tile(kj, False), c)[1], 0)
            dq_ref[...] = acc_ref[...] - dl_ref[...] * acc2_ref[...]
        else:
            lax.fori_loop(1, jnp.minimum(qi, wb) + 1, lambda i, c: (tile(qi - i, False), c)[1], 0)
            dq_ref[...] = acc_ref[...]
        dl_row_ref[...] = _rows8(dl_ref[...])

    qspec = pl.BlockSpec((tq, HEAD_DIM), lambda h, i: (i, h))
    kvspec = pl.BlockSpec((t, HEAD_DIM), lambda h, i: (0, h))
    repspec = pl.BlockSpec((None, tq, LANE), lambda h, i: (h, i, 0))
    row8spec = pl.BlockSpec((None, None, 8, tq), lambda h, i: (h, i, 0, 0))
    if fox:
        bspec = pl.BlockSpec((None, nb, 1, tq), lambda h, i: (h, 0, 0, 0))
    else:
        bspec = pl.BlockSpec((wb + 1, tq, tq), lambda h, i: (0, 0, 0))
    return _call(
        body, [q, k, v, o, do, lse, bias], dep=dep, name=name, grid=(nh, nb),
        in_specs=[qspec, kvspec, kvspec, qspec, qspec, repspec, bspec],
        out_specs=[qspec, row8spec],
        out_shape=[jax.ShapeDtypeStruct((t, hd), F32), jax.ShapeDtypeStruct((nh, nb, 8, tq), F32)],
        scratch_shapes=[pltpu.VMEM((tq, tq), F32), pltpu.VMEM((tq, tq), F32), pltpu.VMEM((tq, tq), BF),
                        pltpu.VMEM((tq, tq), BF), pltpu.VMEM((tq, HEAD_DIM), F32),
                        pltpu.VMEM((tq, HEAD_DIM), F32), pltpu.VMEM((tq, LANE), F32)],
        compiler_params=_params(),
    )


def _attn_bwd_dkv(mode, q, k, v, do, lse_row, dl_row, bias_t, c_row, tq, name):
    t, hd = q.shape
    nh = hd // HEAD_DIM
    nb = t // tq
    wb = MAX_WINDOW // tq
    fox = mode == "fox"

    def body(*refs):
        if fox:
            (q_ref, k_ref, v_ref, do_ref, lse_ref, dl_ref, b_ref, cq_ref, dk_ref, dv_ref, dc_row_ref,
             s_ref, dp_ref, x_ref, y_ref, dc_ref) = refs
        else:
            q_ref, k_ref, v_ref, do_ref, lse_ref, dl_ref, b_ref, dk_ref, dv_ref, s_ref, dp_ref, x_ref, y_ref = refs
        kj = pl.program_id(1)
        kb = k_ref[...]
        vb = v_ref[...]
        dk_ref[...] = jnp.zeros_like(dk_ref)
        dv_ref[...] = jnp.zeros_like(dv_ref)
        if fox:
            dc_ref[...] = jnp.zeros_like(dc_ref)

        def tile(qi, diag):
            off = pl.multiple_of(qi * tq, tq)
            qb = q_ref[pl.ds(off, tq), :]
            dob = do_ref[pl.ds(off, tq), :]
            s_ref[...] = _dot(kb, qb, NT)
            dp_ref[...] = _dot(vb, dob, NT)
            lse_r = lse_ref[qi, 0:1, :]
            dl_r = dl_ref[qi, 0:1, :]
            if fox:
                kbias = cq_ref[qi][:, :1] - b_ref[...]

            def chunk(r0):
                rows = pl.ds(r0, ATTN_ROWS)
                if fox:
                    s = s_ref[rows, :] + _rep(kbias[r0:r0 + ATTN_ROWS, :], tq)
                    if diag:
                        s = jnp.where(_causal(r0, tq, True), s, NEG)
                else:
                    s = s_ref[rows, :] + b_ref[qi - kj, rows, :]
                pt = jnp.exp(s - lse_r)
                dst = pt * (dp_ref[rows, :] - dl_r)
                x_ref[rows, :] = pt.astype(BF)
                y_ref[rows, :] = dst.astype(BF)
                if fox:
                    dc_ref[rows, :] -= jnp.sum(dst, axis=1, keepdims=True)

            _chunks(tq, chunk)
            dv_ref[...] += _dot(x_ref[...], dob)
            dk_ref[...] += _dot(y_ref[...], qb)

        tile(kj, True)
        hi = nb if fox else jnp.minimum(kj + wb + 1, nb)
        lax.fori_loop(kj + 1, hi, lambda qi, c: (tile(qi, False), c)[1], 0)
        if fox:
            dc_row_ref[...] = _rows8(dc_ref[...])

    blkspec = pl.BlockSpec((tq, HEAD_DIM), lambda h, j: (j, h))
    fullspec = pl.BlockSpec((t, HEAD_DIM), lambda h, j: (0, h))
    rows8spec = pl.BlockSpec((None, nb, 8, tq), lambda h, j: (h, 0, 0, 0))
    repspec = pl.BlockSpec((None, tq, LANE), lambda h, j: (h, j, 0))
    in_specs = [fullspec, blkspec, blkspec, fullspec, rows8spec, rows8spec]
    args = [q, k, v, do, lse_row, dl_row, bias_t]
    out_specs = [blkspec, blkspec]
    out_shape = [jax.ShapeDtypeStruct((t, hd), F32), jax.ShapeDtypeStruct((t, hd), F32)]
    scratch = [pltpu.VMEM((tq, tq), F32), pltpu.VMEM((tq, tq), F32), pltpu.VMEM((tq, tq), BF),
               pltpu.VMEM((tq, tq), BF)]
    if fox:
        in_specs += [repspec, pl.BlockSpec((None, nb, 1, tq), lambda h, j: (h, 0, 0, 0))]
        args.append(c_row)
        out_specs.append(pl.BlockSpec((None, None, 8, tq), lambda h, j: (h, j, 0, 0)))
        out_shape.append(jax.ShapeDtypeStruct((nh, nb, 8, tq), F32))
        scratch.append(pltpu.VMEM((tq, LANE), F32))
    else:
        in_specs.append(pl.BlockSpec((wb + 1, tq, tq), lambda h, j: (0, 0, 0)))
    return pl.pallas_call(
        body, name=name, grid=(nh, nb), in_specs=in_specs, out_specs=out_specs, out_shape=out_shape,
        scratch_shapes=scratch, compiler_params=_params(),
    )(*args)


def _gate_specs(t, d, hd, tr):
    row = pl.BlockSpec((tr, d), lambda i: (i, 0))
    vec = pl.BlockSpec((1, d), lambda i: (0, 0))
    base = 6 * hd // d
    gd = pl.BlockSpec((tr, d), lambda i: (i, base))
    gf = pl.BlockSpec((tr, d), lambda i: (i, base + 1))
    return row, vec, gd, gf


def _proj_merge(yd, yf, wpd, wpf, proj, b_d, b_f, hd):
    t = yd.shape[0]
    d = wpd.shape[1]
    tr = _tile(t, 256, 16)
    row, vec, gd, gf = _gate_specs(t, d, hd, tr)

    def body(yd_ref, yf_ref, wd_ref, wf_ref, gd_ref, gf_ref, bd_ref, bf_ref, pd_ref, pf_ref, o_ref):
        pd = _dot(yd_ref[...], wd_ref[...])
        pf = _dot(yf_ref[...], wf_ref[...])
        pd_ref[...] = pd
        pf_ref[...] = pf
        o_ref[...] = (_sig(gd_ref[...] + bd_ref[...]) * pd + _sig(gf_ref[...] + bf_ref[...]) * pf).astype(BF)

    yspec = pl.BlockSpec((tr, hd), lambda i: (i, 0))
    wspec = pl.BlockSpec((hd, d), lambda i: (0, 0))
    f32 = jax.ShapeDtypeStruct((t, d), F32)
    return pl.pallas_call(
        body, name="proj_merge", grid=(t // tr,), in_specs=[yspec, yspec, wspec, wspec, gd, gf, vec, vec],
        out_specs=[row, row, row], out_shape=[f32, f32, jax.ShapeDtypeStruct((t, d), BF)],
        compiler_params=_params(),
    )(yd, yf, wpd, wpf, proj, proj, b_d, b_f)


def _merge_bwd(dm, pd, pf, proj, b_d, b_f, hd):
    t, d = pd.shape
    tr = _tile(t, 256, 16)
    row, vec, gd, gf = _gate_specs(t, d, hd, tr)

    def body(dm_ref, pd_ref, pf_ref, gd_ref, gf_ref, bd_ref, bf_ref,
             dpd_ref, dpf_ref, dgd_ref, dgf_ref, dbd_ref, dbf_ref):
        dmv = dm_ref[...]
        sd = _sig(gd_ref[...] + bd_ref[...])
        sf = _sig(gf_ref[...] + bf_ref[...])
        dgd = dmv * pd_ref[...] * (sd * (1.0 - sd))
        dgf = dmv * pf_ref[...] * (sf * (1.0 - sf))
        dpd_ref[...] = (dmv * sd).astype(BF)
        dpf_ref[...] = (dmv * sf).astype(BF)
        dgd_ref[...] = dgd.astype(BF)
        dgf_ref[...] = dgf.astype(BF)

        @pl.when(pl.program_id(0) == 0)
        def _():
            dbd_ref[...] = jnp.zeros_like(dbd_ref)
            dbf_ref[...] = jnp.zeros_like(dbf_ref)

        dbd_ref[...] += jnp.sum(dgd, axis=0, keepdims=True)
        dbf_ref[...] += jnp.sum(dgf, axis=0, keepdims=True)

    ob = jax.ShapeDtypeStruct((t, d), BF)
    ov = jax.ShapeDtypeStruct((1, d), F32)
    return pl.pallas_call(
        body, name="merge_bwd", grid=(t // tr,), in_specs=[row, row, row, gd, gf, vec, vec],
        out_specs=[row, row, row, row, vec, vec], out_shape=[ob, ob, ob, ob, ov, ov],
        compiler_params=_params(),
    )(dm, pd, pf, proj, proj, b_d, b_f)


def _assemble_dproj(dqd, dkd, dvd, dqf, dkf, dvf, dgd, dgf, dlogf, proj, tables, bf_pad, scale):
    t, np_ = proj.shape
    hd = dqd.shape[1]
    d = dgd.shape[1]
    nh = hd // HEAD_DIM
    tr = _tile(t, 256, 16)
    f_blk = np_ // LANE - 1

    def body(dqd_ref, dkd_ref, dvd_ref, dqf_ref, dkf_ref, dvf_ref, dgd_ref, dgf_ref, dlog_ref, fl_ref,
             c_ref, s1_ref, s2_ref, b_ref, o_ref, db_ref):
        c, s1, s2 = c_ref[...], s1_ref[...], s2_ref[...]
        for h in range(nh):
            sl = slice(h * HEAD_DIM, (h + 1) * HEAD_DIM)
            o_ref[:, sl] = (_rope_t(dqd_ref[:, sl], c, s1, s2) * scale).astype(BF)
            o_ref[:, hd + h * HEAD_DIM:hd + (h + 1) * HEAD_DIM] = _rope_t(dkd_ref[:, sl], c, s1, s2).astype(BF)
        o_ref[:, 2 * hd:3 * hd] = dvd_ref[...].astype(BF)
        o_ref[:, 3 * hd:4 * hd] = (dqf_ref[...] * scale).astype(BF)
        o_ref[:, 4 * hd:5 * hd] = dkf_ref[...].astype(BF)
        o_ref[:, 5 * hd:6 * hd] = dvf_ref[...].astype(BF)
        o_ref[:, 6 * hd:6 * hd + d] = dgd_ref[...]
        o_ref[:, 6 * hd + d:6 * hd + 2 * d] = dgf_ref[...]
        z = fl_ref[...] + b_ref[...]
        dfl = dlog_ref[...] * _sig(-z)
        o_ref[:, 6 * hd + 2 * d:] = dfl.astype(BF)

        @pl.when(pl.program_id(0) == 0)
        def _():
            db_ref[...] = jnp.zeros_like(db_ref)

        db_ref[...] += jnp.sum(dfl, axis=0, keepdims=True)

    head = pl.BlockSpec((tr, hd), lambda i: (i, 0))
    row = pl.BlockSpec((tr, d), lambda i: (i, 0))
    lane_row = pl.BlockSpec((tr, LANE), lambda i: (i, 0))
    lane_vec = pl.BlockSpec((1, LANE), lambda i: (0, 0))
    return pl.pallas_call(
        body, name="assemble_dproj", grid=(t // tr,),
        in_specs=[head] * 6 + [row, row, lane_row, pl.BlockSpec((tr, LANE), lambda i: (i, f_blk)),
                               lane_row, lane_row, lane_row, lane_vec],
        out_specs=[pl.BlockSpec((tr, np_), lambda i: (i, 0)), lane_vec],
        out_shape=[jax.ShapeDtypeStruct((t, np_), BF), jax.ShapeDtypeStruct((1, LANE), F32)],
        compiler_params=_params(),
    )(dqd, dkd, dvd, dqf, dkf, dvf, dgd, dgf, dlogf, proj, *tables, bf_pad)


def _to_rows(a, tq):
    h, t = a.shape
    return a.reshape(h, t // tq, 1, tq)


def kernel(x, ffn1_norm, ffn1_w_gate, ffn1_w_up, ffn1_w_down, mix_norm, w_in, b_forget, b_gate_dil, b_gate_fox, w_proj_dil, w_proj_fox, w_out, ffn2_norm, ffn2_w_gate, ffn2_w_up, ffn2_w_down, final_norm, loss_target, m_ffn1_norm, m_ffn1_w_gate, m_ffn1_w_up, m_ffn1_w_down, m_mix_norm, m_w_in, m_b_forget, m_b_gate_dil, m_b_gate_fox, m_w_proj_dil, m_w_proj_fox, m_w_out, m_ffn2_norm, m_ffn2_w_gate, m_ffn2_w_up, m_ffn2_w_down, m_final_norm, v_ffn1_norm, v_ffn1_w_gate, v_ffn1_w_up, v_ffn1_w_down, v_mix_norm, v_w_in, v_b_forget, v_b_gate_dil, v_b_gate_fox, v_w_proj_dil, v_w_proj_fox, v_w_out, v_ffn2_norm, v_ffn2_w_gate, v_ffn2_w_up, v_ffn2_w_down, v_final_norm):
    t, d = x.shape[1], x.shape[2]
    hd = w_proj_dil.shape[1]
    nh = hd // HEAD_DIM
    n_f = b_forget.shape[1]
    cols = w_in.shape[2]
    in_cols = N_DEV * cols
    assert in_cols == 6 * hd + n_f + 2 * d and n_f == nh and n_f <= LANE
    np_ = 6 * hd + 2 * d + LANE
    scale = HEAD_DIM ** -0.5
    tq = _tile(t, 512, LANE)
    assert MAX_WINDOW % tq == 0 and tq % 16 == 0

    x2d = x[0]
    tgt = loss_target[0]

    def rows(w):
        return jnp.swapaxes(w, 1, 2)

    fc = N_DEV * ffn1_w_down.shape[1]
    ag_order = [rows(ffn1_w_gate), rows(ffn1_w_up), ffn1_w_down, w_in, w_proj_dil, w_proj_fox, w_out,
                rows(ffn2_w_gate), rows(ffn2_w_up), ffn2_w_down]
    ag_first, tok = _exchange_start([w[0].astype(BF) for w in ag_order[:2]], True, "ag_start_first", ks=FIRST_LEVEL)
    ag_rest, ag_token = _exchange_start([w[0].astype(BF) for w in ag_order[2:]], True, "ag_start", dep=tok,
                                        ks=FIRST_LEVEL)
    ag = ag_first + ag_rest

    def relay(idx, after, name):
        for i, h in zip(idx, _gather_relay([ag[i] for i in idx], after, name)):
            ag[i] = h

    def gathered(idx, after, name):
        return _gather_wait([ag[i] for i in idx], after, name)

    def ffn_weight(idx, after, name):
        return [w.reshape(fc, d) for w in gathered(idx, after, name)]

    tables = _rope_tables(t)
    bf_pad = jnp.pad(b_forget, ((0, 0), (0, LANE - n_f)))

    hn1, = _rms_fwd(x2d, ffn1_norm, "rms_ffn1", dep=ag_token)
    relay([0], hn1, "ag_relay_ffn1_gate")
    wg1, = ffn_weight([0], hn1, "ag_wait_ffn1_gate")
    g1_f32 = _ffn_gate(hn1, wg1, "ffn1_gate")
    relay([1], g1_f32, "ag_relay_ffn1_up")
    wu1, = ffn_weight([1], g1_f32, "ag_wait_ffn1_up")
    relay([2], wu1, "ag_relay_ffn1_down")
    g1, u1, a1 = _ffn_up_act(hn1, wu1, g1_f32, "ffn1_up_act")
    wd1, = ffn_weight([2], a1, "ag_wait_ffn1_down")
    relay([3], wd1, "ag_relay_w_in")
    x1 = _ffn_down(a1, wd1, x2d, "ffn1_down")

    hm, hm_t = _rms_fwd(x1, mix_norm, "rms_mix", with_transpose=True)
    win_g, = gathered([3], hm, "ag_wait_w_in")
    relay([4, 5, 6], win_g, "ag_relay_mixer")
    segments = [(0, 6 * hd), (6 * hd + n_f, in_cols), (6 * hd, 6 * hd + n_f)]
    pieces = []
    for lo, hi in segments:
        for j in range(lo // cols, (hi - 1) // cols + 1):
            s, e = max(lo, j * cols), min(hi, (j + 1) * cols)
            pieces.append(win_g[j, :, s - j * cols:e - j * cols])
    win_p = jnp.concatenate(pieces + [jnp.zeros((d, LANE - n_f), BF)], axis=1)
    proj = _mm_nn(hm, win_p, F32, "w_in_fwd")
    qd, kd, vd, qf, kf, vf, logf = _mixer_prep(proj, tables, bf_pad, hd, scale)
    csum = _cumsum_rows(logf, False, "cumsum_logf")
    c_heads = csum[:, :nh].T
    c_row = _to_rows(c_heads, tq)
    c_rep = jnp.broadcast_to(c_heads[:, :, None], (nh, t, LANE))
    dil_bias = _dil_bias_tiles(tq)
    dil_bias_t = dil_bias.transpose(0, 2, 1)
    relay([7, 8, 9], qd, "ag_relay_ffn2")
    yd, lse_d, lse_d_row = _attn_fwd("dil", qd, kd, vd, dil_bias, tq, "attn_dil_fwd")
    yf, lse_f, lse_f_row = _attn_fwd("fox", qf, kf, vf, c_row, tq, "attn_fox_fwd")
    wpd_g, wpf_g = gathered([4, 5], yf, "ag_wait_proj")
    wpd = wpd_g.transpose(1, 0, 2).reshape(hd, d)
    wpf = wpf_g.transpose(1, 0, 2).reshape(hd, d)
    pd, pf, merged = _proj_merge(yd, yf, wpd, wpf, proj, b_gate_dil, b_gate_fox, hd)
    wout_g, = gathered([6], merged, "ag_wait_w_out")
    wout = wout_g.reshape(d, d)
    x2 = _mm_nn(merged, wout, F32, "w_out_fwd", residual=x1, tn_pref=1024)

    hn2, = _rms_fwd(x2, ffn2_norm, "rms_ffn2")
    wg2, wu2 = ffn_weight([7, 8], hn2, "ag_wait_ffn2_gate_up")
    g2, u2, a2 = _ffn_gate_up(hn2, wg2, wu2, "ffn2_gate_up")
    wd2, = ffn_weight([9], a2, "ag_wait_ffn2_down")
    x3 = _ffn_down(a2, wd2, x2, "ffn2_down")

    dx3, dx3b, d_final, loss_lanes = _loss_head(x3, final_norm.reshape(1, d), tgt)

    def ffn_bwd(dxb, hn, g, u, a, wg_t, wu_t, wd, x_in, gain, dres, tag):
        def parts(dw):
            return dw.reshape(N_DEV, fc // N_DEV, d)

        dg, du = _ffn_bwd_hidden(dxb, wd, g, u, tag + "_bwd_hidden")
        dwd, = _ffn_dw([a], dxb, 0.5, tag + "_dw_down")
        rs_down, tok = _exchange_start([parts(dwd)], False, "rs_start_" + tag + "_down")
        dwg_t, dwu_t = _ffn_dw([dg, du], hn, 1.0, tag + "_dw_gate_up", dep=tok)
        rs_gu, tok = _exchange_start([parts(dwg_t), parts(dwu_t)], False, "rs_start_" + tag + "_gate_up")
        dhn = _ffn_bwd_input(dg, du, wg_t, wu_t, tag + "_bwd_input", dep=tok)
        dx, dx_bf, dgain = _rms_bwd(dhn, x_in, gain, dres, "rms_" + tag + "_bwd")
        return dx, dx_bf, dgain, rs_gu + rs_down

    dx2, dx2b, d_ffn2_norm, rs_ffn2 = ffn_bwd(dx3b, hn2, g2, u2, a2, wg2, wu2, wd2, x2, ffn2_norm, dx3, "ffn2")

    dmerged = _mm_nt(dx2b, wout, F32, "w_out_bwd")
    dwout = _mm_tn(merged, dx2b, BF, "w_out_dw", tn_pref=1024)
    dpd, dpf, dgd, dgf, d_bd, d_bf = _merge_bwd(dmerged, pd, pf, proj, b_gate_dil, b_gate_fox, hd)
    dyd = _mm_nt(dpd, wpd, BF, "proj_dil_bwd")
    dyf = _mm_nt(dpf, wpf, BF, "proj_fox_bwd")
    dwpd = _mm_tn(yd, dpd, BF, "proj_dil_dw", tn_pref=1024)
    dwpf = _mm_tn(yf, dpf, BF, "proj_fox_dw", tn_pref=1024)
    dwpd_c = dwpd.reshape(hd, N_DEV, d // N_DEV).transpose(1, 0, 2)
    dwpf_c = dwpf.reshape(hd, N_DEV, d // N_DEV).transpose(1, 0, 2)
    dwout_c = dwout.reshape(N_DEV, d // N_DEV, d)
    rs_mix, tok = _exchange_start([dwout_c, dwpd_c, dwpf_c], False, "rs_start_mixer")

    dqd, dl_d = _attn_bwd_dq("dil", qd, kd, vd, yd, dyd, lse_d, dil_bias, tq, "attn_dil_dq", dep=tok)
    dkd, dvd = _attn_bwd_dkv("dil", qd, kd, vd, dyd, lse_d_row, dl_d, dil_bias_t, None, tq, "attn_dil_dkv")
    dqf, dl_f = _attn_bwd_dq("fox", qf, kf, vf, yf, dyf, lse_f, c_row, tq, "attn_fox_dq")
    dkf, dvf, dc = _attn_bwd_dkv("fox", qf, kf, vf, dyf, lse_f_row, dl_f, c_rep, c_row, tq, "attn_fox_dkv")
    dc_pad = jnp.pad(dc[:, :, 0, :].reshape(nh, t).T, ((0, 0), (0, LANE - nh)))
    dlogf = _cumsum_rows(dc_pad, True, "revcumsum_dc")
    dproj, d_bforget = _assemble_dproj(dqd, dkd, dvd, dqf, dkf, dvf, dgd, dgf, dlogf, proj, tables, bf_pad, scale)

    dwin_p = _mm_tn(hm_t, dproj, BF, "w_in_dw", tk_pref=DW_ROWS, a_transposed=True)
    def perm_col(c):
        if c < 6 * hd:
            return c
        return c + 2 * d if c < 6 * hd + n_f else c - n_f

    shards = []
    for j in range(N_DEV):
        cuts = sorted({j * cols, (j + 1) * cols} | {c for c in (6 * hd, 6 * hd + n_f) if j * cols < c < (j + 1) * cols})
        shards.append(jnp.concatenate([dwin_p[:, perm_col(lo):perm_col(lo) + hi - lo]
                                       for lo, hi in zip(cuts[:-1], cuts[1:])], axis=1))
    dwin_c = jnp.stack(shards)
    rs_win, tok = _exchange_start([dwin_c], False, "rs_start_w_in")
    dx1, dx1b, d_mix_norm = _mm_nt(dproj, win_p, F32, "w_in_bwd", tn_pref=d, tk_pref=1152,
                                   rms=(x1, mix_norm, dx2), dep=tok)

    grad_x, _, d_ffn1_norm, rs_ffn1 = ffn_bwd(dx1b, hn1, g1, u1, a1, wg1, wu1, wd1, x2d, ffn1_norm, dx1, "ffn1")

    def update(handles, names, after, tag):
        recvs = _exchange_wait(handles, False, after, "rs_wait_" + tag)
        res = {}
        for recv, n in zip(recvs, names):
            turn = rows if n.endswith(("w_gate", "w_up")) else (lambda a: a)
            w, m, v = (turn(a)[0] for a in wmv[n])
            res[n] = tuple(turn(o[None]) for o in _adam_from_partials(recv, w, m, v, "adam_" + n))
        return res, res[names[-1]][0]

    wmv = {
        "ffn1_w_gate": (ffn1_w_gate, m_ffn1_w_gate, v_ffn1_w_gate),
        "ffn1_w_up": (ffn1_w_up, m_ffn1_w_up, v_ffn1_w_up),
        "ffn1_w_down": (ffn1_w_down, m_ffn1_w_down, v_ffn1_w_down),
        "w_in": (w_in, m_w_in, v_w_in),
        "w_proj_dil": (w_proj_dil, m_w_proj_dil, v_w_proj_dil),
        "w_proj_fox": (w_proj_fox, m_w_proj_fox, v_w_proj_fox),
        "w_out": (w_out, m_w_out, v_w_out),
        "ffn2_w_gate": (ffn2_w_gate, m_ffn2_w_gate, v_ffn2_w_gate),
        "ffn2_w_up": (ffn2_w_up, m_ffn2_w_up, v_ffn2_w_up),
        "ffn2_w_down": (ffn2_w_down, m_ffn2_w_down, v_ffn2_w_down),
    }
    big = {}
    after = grad_x
    for handles, names, tag in [
            (rs_ffn2, ["ffn2_w_gate", "ffn2_w_up", "ffn2_w_down"], "ffn2"),
            (rs_mix, ["w_out", "w_proj_dil", "w_proj_fox"], "mixer"),
            (rs_win, ["w_in"], "w_in"),
            (rs_ffn1, ["ffn1_w_gate", "ffn1_w_up", "ffn1_w_down"], "ffn1")]:
        res, after = update(handles, names, after, tag)
        big.update(res)

    def lanes(a):
        a = a.reshape(1, -1)
        return jnp.pad(a, ((0, 0), (0, d - a.shape[1])))

    small_names = ["ffn1_norm", "mix_norm", "b_gate_dil", "b_gate_fox", "ffn2_norm", "final_norm", "b_forget"]
    small_g = [d_ffn1_norm, d_mix_norm, d_bd, d_bf, d_ffn2_norm, d_final, d_bforget[:, :n_f]]
    small_w = [ffn1_norm, mix_norm, b_gate_dil, b_gate_fox, ffn2_norm, final_norm, b_forget]
    small_m = [m_ffn1_norm, m_mix_norm, m_b_gate_dil, m_b_gate_fox, m_ffn2_norm, m_final_norm, m_b_forget]
    small_v = [v_ffn1_norm, v_mix_norm, v_b_gate_dil, v_b_gate_fox, v_ffn2_norm, v_final_norm, v_b_forget]
    pack = lambda arrs, last: jnp.concatenate([lanes(a) for a in arrs] + [last], axis=0)
    g_all = _allreduce_small(pack(small_g, loss_lanes))
    zero_row = jnp.zeros((1, d), F32)
    one_row = jnp.ones((1, d), F32)
    s_delta, s_m, s_v = _adam_small(g_all, pack(small_w, zero_row), pack(small_m, zero_row), pack(small_v, one_row))
    loss = g_all[len(small_names), 0]

    def unpack(packed, i, like):
        return packed[i, :like.size].reshape(like.shape)

    small = {}
    for i, (n, w) in enumerate(zip(small_names, small_w)):
        small[n] = (unpack(g_all, i, w), unpack(s_delta, i, w), unpack(s_m, i, w), unpack(s_v, i, w))

    order = ["ffn1_norm", "ffn1_w_gate", "ffn1_w_up", "ffn1_w_down", "mix_norm", "w_in", "b_forget", "b_gate_dil",
             "b_gate_fox", "w_proj_dil", "w_proj_fox", "w_out", "ffn2_norm", "ffn2_w_gate", "ffn2_w_up",
             "ffn2_w_down", "final_norm"]
    res = {**big, **small}
    outs = [loss, grad_x[None]]
    for slot in range(4):
        outs += [res[n][slot] for n in order]
    return tuple(outs)
```

```python
import functools

import numpy as np
import jax
import jax.numpy as jnp
from jax import lax
from jax.experimental import pallas as pl
from jax.experimental.pallas import tpu as pltpu

BF = jnp.bfloat16
F32 = jnp.float32
MESH = pl.DeviceIdType.MESH
N_DEV = 8

HEAD_DIM = 128
ROPE_DIM = HEAD_DIM // 4
ROPE_HALF = ROPE_DIM // 2
ROPE_THETA = 500000.0
NORM_EPS = 1e-6
DIL_PATTERNS = ((128, 1), (512, 4), (2048, 16))
MAX_WINDOW = 2048
LANE = 128
NEG = -1e30

ADAM_LR = 0.001
ADAM_B1 = 0.9
ADAM_B2 = 0.999
ADAM_EPS = 1e-08
ADAM_WD = 0.01
ADAM_STEP = 10

VMEM_LIMIT_BYTES = 56 * 1024 * 1024
FFN_ROWS = 1024
DW_ROWS = 1024
ANY = pl.BlockSpec(memory_space=pl.ANY)

NN = (((1,), (0,)), ((), ()))
NT = (((1,), (1,)), ((), ()))
TN = (((0,), (0,)), ((), ()))


def _dot(a, b, dn=NN):
    return lax.dot_general(a, b, dn, preferred_element_type=F32)


def _sig(x):
    return 1.0 / (1.0 + jnp.exp(-x))


def _tile(n, pref, align):
    best = None
    t = align
    while t <= min(n, pref):
        if n % t == 0:
            best = t
        t += align
    return n if best is None else best


def _params():
    return pltpu.CompilerParams(vmem_limit_bytes=VMEM_LIMIT_BYTES)


def _call(body, args, dep=None, **kw):
    if dep is not None:
        n_in = len(args)
        inner = body

        def body(*refs):
            inner(*refs[:n_in], *refs[n_in + 1:])

        kw["in_specs"] = list(kw["in_specs"]) + [ANY]
        args = list(args) + [dep]
    return pl.pallas_call(body, **kw)(*args)


def _peers():
    x, y, c = lax.axis_index("x"), lax.axis_index("y"), lax.axis_index("c")
    me = 4 * x + 2 * y + c
    peers = []
    for k in range(1, N_DEV):
        px = 1 - x if (k >> 2) & 1 else x
        py = 1 - y if (k >> 1) & 1 else y
        pc = 1 - c if k & 1 else c
        peers.append((k, (px, py, pc), 4 * px + 2 * py + pc))
    return me, peers


HBM = pl.BlockSpec(memory_space=pltpu.HBM)
SEM = pl.BlockSpec(memory_space=pltpu.SEMAPHORE)
EFFECT = pltpu.SideEffectType.DATAFLOW_SIDE_EFFECTING


def _exchange_copy(gather, src_ref, land_ref, send_sems, recv_sems, me, k, peer, peer_flat, landing):
    return pltpu.make_async_remote_copy(
        src_ref=src_ref if gather else src_ref.at[peer_flat], dst_ref=land_ref.at[landing],
        send_sem=send_sems.at[k], recv_sem=recv_sems.at[k], device_id=peer, device_id_type=MESH)


ALL_PEERS = (1, 2, 3, 4, 5, 6, 7)
SIBLING = 1
SAME_CORE = (2, 4, 6)
FIRST_LEVEL = (SIBLING,) + SAME_CORE


def _exchange_start(srcs, gather, name, dep=None, ks=ALL_PEERS):
    n = len(srcs)
    extra = [] if dep is None else [dep]

    def body(*refs):
        src_refs, land_refs = refs[:n], refs[n:2 * n]
        refs = refs[2 * n + len(extra):]
        send_refs, recv_refs = refs[:n], refs[n:2 * n]
        token = refs[4 * n]
        me, peers = _peers()
        for i in range(n):
            for k, peer, peer_flat in peers:
                if k in ks:
                    _exchange_copy(gather, src_refs[i], land_refs[i], send_refs[i], recv_refs[i],
                                   me, k, peer, peer_flat, me).start()
        token[...] = jnp.zeros_like(token)

    lands = [lax.empty((N_DEV,) + s.shape[-2:], s.dtype) for s in srcs]
    sems = [pltpu.SemaphoreType.DMA((N_DEV,)) for _ in range(2 * n)]
    out = pl.pallas_call(
        body, name=name,
        out_shape=tuple(sems) + tuple(pltpu.HBM(a.shape, a.dtype) for a in list(srcs) + lands)
        + (jax.ShapeDtypeStruct((8, LANE), F32),),
        in_specs=[HBM] * (2 * n) + [ANY] * len(extra),
        out_specs=tuple([SEM] * (2 * n) + [HBM] * (2 * n) + [pl.BlockSpec(memory_space=pltpu.VMEM)]),
        input_output_aliases={i: 2 * n + i for i in range(2 * n)},
        compiler_params=pltpu.CompilerParams(has_side_effects=EFFECT),
    )(*[pltpu.with_memory_space_constraint(a, pltpu.HBM) for a in list(srcs) + lands], *extra)
    handles = [(out[2 * n + i], out[3 * n + i], out[i], out[n + i]) for i in range(n)]
    return handles, out[4 * n]


def _exchange_wait(handles, gather, after, name):
    n = len(handles)

    def body(*refs):
        src_refs, land_refs = refs[:n], refs[n:2 * n]
        send_refs, recv_refs = refs[2 * n:3 * n], refs[3 * n:4 * n]
        me, peers = _peers()
        for i in range(n):
            for k, peer, peer_flat in peers:
                cp = _exchange_copy(gather, src_refs[i], land_refs[i], send_refs[i], recv_refs[i],
                                    me, k, peer, peer_flat, peer_flat)
                cp.wait_send()
                cp.wait_recv()

    srcs = [h[0] for h in handles]
    lands = [h[1] for h in handles]
    out = pl.pallas_call(
        body, name=name,
        out_shape=tuple(pltpu.HBM(a.shape, a.dtype) for a in srcs + lands),
        in_specs=[HBM] * (2 * n) + [SEM] * (2 * n) + [ANY],
        out_specs=tuple([HBM] * (2 * n)),
        input_output_aliases={i: i for i in range(2 * n)},
        compiler_params=pltpu.CompilerParams(has_side_effects=EFFECT),
    )(*srcs, *lands, *[h[2] for h in handles], *[h[3] for h in handles], after)
    me = 4 * lax.axis_index("x") + 2 * lax.axis_index("y") + lax.axis_index("c")
    filled = []
    for src, land in zip(out[:n], out[n:]):
        own = src[None] if gather else lax.dynamic_slice_in_dim(src, me, 1, axis=0)
        filled.append(lax.dynamic_update_slice_in_dim(land, own, me, axis=0))
    return filled


def _gather_relay(handles, after, name):
    n = len(handles)

    def body(*refs):
        land_refs, recv_refs = refs[:n], refs[n:2 * n]
        refs = refs[2 * n + 1:]
        send2_refs, recv2_refs = refs[n:2 * n], refs[2 * n:3 * n]
        me, peers = _peers()
        sibling = peers[SIBLING - 1][1]
        for i in range(n):
            for k, peer, peer_flat in peers:
                if k in SAME_CORE:
                    block = land_refs[i].at[peer_flat]
                    pltpu.make_async_remote_copy(
                        src_ref=block, dst_ref=block, send_sem=send2_refs[i].at[k], recv_sem=recv_refs[i].at[k],
                        device_id=peer, device_id_type=MESH).wait_recv()
                    pltpu.make_async_remote_copy(
                        src_ref=block, dst_ref=block, send_sem=send2_refs[i].at[k], recv_sem=recv2_refs[i].at[k],
                        device_id=sibling, device_id_type=MESH).start()

    lands = [h[1] for h in handles]
    sems = [pltpu.SemaphoreType.DMA((N_DEV,)) for _ in range(2 * n)]
    out = pl.pallas_call(
        body, name=name,
        out_shape=tuple(pltpu.HBM(a.shape, a.dtype) for a in lands) + tuple(sems),
        in_specs=[HBM] * n + [SEM] * n + [ANY],
        out_specs=tuple([HBM] * n + [SEM] * (2 * n)),
        input_output_aliases={i: i for i in range(n)},
        compiler_params=pltpu.CompilerParams(has_side_effects=EFFECT),
    )(*lands, *[h[3] for h in handles], after)
    return [(h[0], out[i], h[2], h[3], out[n + i], out[2 * n + i]) for i, h in enumerate(handles)]


def _gather_wait(handles, after, name):
    n = len(handles)

    def body(*refs):
        src_refs, land_refs = refs[:n], refs[n:2 * n]
        send_refs, recv_refs = refs[2 * n:3 * n], refs[3 * n:4 * n]
        send2_refs, recv2_refs = refs[4 * n:5 * n], refs[5 * n:6 * n]
        me, peers = _peers()
        _, sibling, sibling_flat = peers[SIBLING - 1]
        for i in range(n):
            for k, peer, peer_flat in peers:
                if k in FIRST_LEVEL:
                    cp = _exchange_copy(True, src_refs[i], land_refs[i], send_refs[i], recv_refs[i],
                                        me, k, peer, peer_flat, peer_flat)
                    cp.wait_send()
                    if k == SIBLING:
                        cp.wait_recv()
                if k in SAME_CORE:
                    mine = land_refs[i].at[peer_flat]
                    theirs = land_refs[i].at[peer_flat ^ SIBLING]
                    cp = pltpu.make_async_remote_copy(
                        src_ref=mine, dst_ref=theirs, send_sem=send2_refs[i].at[k], recv_sem=recv2_refs[i].at[k],
                        device_id=sibling, device_id_type=MESH)
                    cp.wait_send()
                    cp.wait_recv()

    srcs = [h[0] for h in handles]
    lands = [h[1] for h in handles]
    out = pl.pallas_call(
        body, name=name,
        out_shape=tuple(pltpu.HBM(a.shape, a.dtype) for a in srcs + lands),
        in_specs=[HBM] * (2 * n) + [SEM] * (4 * n) + [ANY],
        out_specs=tuple([HBM] * (2 * n)),
        input_output_aliases={i: i for i in range(2 * n)},
        compiler_params=pltpu.CompilerParams(has_side_effects=EFFECT),
    )(*srcs, *lands, *[h[2] for h in handles], *[h[3] for h in handles],
      *[h[4] for h in handles], *[h[5] for h in handles], after)
    me = 4 * lax.axis_index("x") + 2 * lax.axis_index("y") + lax.axis_index("c")
    return [lax.dynamic_update_slice_in_dim(land, src[None], me, axis=0) for src, land in zip(out[:n], out[n:])]


def _allreduce_small(p):
    rows, d = p.shape

    def body(p_ref, o_ref, recv_ref, send_sems, recv_sems):
        me, peers = _peers()
        recv_ref[me] = p_ref[...]
        sends = []
        for k, peer, peer_flat in peers:
            cp = pltpu.make_async_remote_copy(
                src_ref=p_ref, dst_ref=recv_ref.at[me],
                send_sem=send_sems.at[k], recv_sem=recv_sems.at[k],
                device_id=peer, device_id_type=MESH)
            cp.start()
            sends.append(cp)
        for k, peer, peer_flat in peers:
            pltpu.make_async_remote_copy(
                src_ref=p_ref, dst_ref=recv_ref.at[peer_flat],
                send_sem=send_sems.at[k], recv_sem=recv_sems.at[k],
                device_id=peer, device_id_type=MESH).wait_recv()
        for cp in sends:
            cp.wait_send()
        acc = recv_ref[0]
        for s in range(1, N_DEV):
            acc = acc + recv_ref[s]
        is_loss = lax.broadcasted_iota(jnp.int32, (rows, d), 0) == rows - 1
        total = jnp.sum(jnp.where(is_loss, acc, 0.0))
        o_ref[...] = jnp.where(is_loss, total, acc)

    return pl.pallas_call(
        body, name="allreduce_small",
        out_shape=jax.ShapeDtypeStruct((rows, d), F32),
        in_specs=[pl.BlockSpec(memory_space=pltpu.VMEM)],
        out_specs=pl.BlockSpec(memory_space=pltpu.VMEM),
        scratch_shapes=[pltpu.VMEM((N_DEV, rows, d), F32),
                        pltpu.SemaphoreType.DMA((N_DEV,)), pltpu.SemaphoreType.DMA((N_DEV,))],
    )(p)


def _adam_math(w, g, m, v):
    m2 = ADAM_B1 * m + (1.0 - ADAM_B1) * g
    v2 = ADAM_B2 * v + (1.0 - ADAM_B2) * (g * g)
    m_hat = m2 / (1.0 - ADAM_B1 ** ADAM_STEP)
    v_hat = v2 / (1.0 - ADAM_B2 ** ADAM_STEP)
    delta = -ADAM_LR * (m_hat / (jnp.sqrt(v_hat) + ADAM_EPS) + ADAM_WD * w)
    return delta, m2, v2


def _adam_from_partials(parts, w, m, v, name):
    r, c = w.shape
    tr = _tile(r, 256, 16)

    def body(p_ref, w_ref, m_ref, v_ref, g_out, d_out, m_out, v_out):
        g = p_ref[0].astype(F32)
        for s in range(1, N_DEV):
            g = g + p_ref[s].astype(F32)
        delta, m2, v2 = _adam_math(w_ref[...], g, m_ref[...], v_ref[...])
        g_out[...] = g
        d_out[...] = delta
        m_out[...] = m2
        v_out[...] = v2

    blk = pl.BlockSpec((tr, c), lambda i: (i, 0))
    out = jax.ShapeDtypeStruct((r, c), F32)
    return pl.pallas_call(
        body, name=name, grid=(r // tr,),
        in_specs=[pl.BlockSpec((N_DEV, tr, c), lambda i: (0, i, 0)), blk, blk, blk],
        out_specs=[blk, blk, blk, blk], out_shape=[out, out, out, out],
        compiler_params=_params(),
    )(parts, w, m, v)


def _adam_small(g, w, m, v):
    def body(g_ref, w_ref, m_ref, v_ref, d_out, m_out, v_out):
        delta, m2, v2 = _adam_math(w_ref[...], g_ref[...], m_ref[...], v_ref[...])
        d_out[...] = delta
        m_out[...] = m2
        v_out[...] = v2

    out = jax.ShapeDtypeStruct(g.shape, F32)
    return pl.pallas_call(body, name="adam_small", out_shape=[out, out, out])(g, w, m, v)


def _rms_fwd(x, gain, name, dep=None, with_transpose=False):
    t, d = x.shape
    tr = _tile(t, 256, LANE)

    def body(x_ref, g_ref, o_ref, *ot_ref):
        xv = x_ref[...]
        r = lax.rsqrt(jnp.mean(xv * xv, axis=-1, keepdims=True) + NORM_EPS)
        y = xv * r * g_ref[...]
        o_ref[...] = y.astype(BF)
        if with_transpose:
            ot_ref[0][...] = jnp.transpose(y).astype(BF)

    out_specs = [pl.BlockSpec((tr, d), lambda i: (i, 0))]
    out_shape = [jax.ShapeDtypeStruct((t, d), BF)]
    if with_transpose:
        out_specs.append(pl.BlockSpec((d, tr), lambda i: (0, i)))
        out_shape.append(jax.ShapeDtypeStruct((d, t), BF))
    return _call(
        body, [x, gain], dep=dep, name=name, grid=(t // tr,),
        in_specs=[pl.BlockSpec((tr, d), lambda i: (i, 0)), pl.BlockSpec((1, d), lambda i: (0, 0))],
        out_specs=out_specs, out_shape=out_shape, compiler_params=_params(),
    )


def _rms_vjp(xv, gain, dy):
    r = lax.rsqrt(jnp.mean(xv * xv, axis=-1, keepdims=True) + NORM_EPS)
    xhat = xv * r
    dxhat = dy * gain
    dx = r * (dxhat - xhat * jnp.mean(dxhat * xhat, axis=-1, keepdims=True))
    dgain = jnp.sum(dy * xhat, axis=0, keepdims=True)
    return dx, dgain


def _loss_head(x, gain, target):
    t, d = x.shape
    tr = _tile(t, 256, 16)

    def body(x_ref, g_ref, t_ref, dx_ref, dxb_ref, dg_ref, loss_ref):
        xv = x_ref[...]
        gain = g_ref[...]
        r = lax.rsqrt(jnp.mean(xv * xv, axis=-1, keepdims=True) + NORM_EPS)
        err = xv * r * gain - t_ref[...]
        dx, dgain = _rms_vjp(xv, gain, err * (1.0 / d))
        dx_ref[...] = dx
        dxb_ref[...] = dx.astype(BF)

        @pl.when(pl.program_id(0) == 0)
        def _():
            dg_ref[...] = jnp.zeros_like(dg_ref)
            loss_ref[...] = jnp.zeros_like(loss_ref)

        dg_ref[...] += dgain
        loss_ref[...] += jnp.sum(err * err, axis=0, keepdims=True) * (0.5 / d)

    row = pl.BlockSpec((tr, d), lambda i: (i, 0))
    vec = pl.BlockSpec((1, d), lambda i: (0, 0))
    return pl.pallas_call(
        body, name="loss_head", grid=(t // tr,),
        in_specs=[row, vec, row], out_specs=[row, row, vec, vec],
        out_shape=[jax.ShapeDtypeStruct((t, d), F32), jax.ShapeDtypeStruct((t, d), BF),
                   jax.ShapeDtypeStruct((1, d), F32), jax.ShapeDtypeStruct((1, d), F32)],
        compiler_params=_params(),
    )(x, gain, target)


def _mm_nn(a, b, out_dtype, name, residual=None, tm_pref=512, tn_pref=1152):
    m, k = a.shape
    n = b.shape[1]
    tm, tn = _tile(m, tm_pref, 16), _tile(n, tn_pref, LANE)

    def body(*refs):
        if residual is None:
            a_ref, b_ref, o_ref = refs
            o_ref[...] = _dot(a_ref[...], b_ref[...]).astype(out_dtype)
        else:
            a_ref, b_ref, r_ref, o_ref = refs
            o_ref[...] = (r_ref[...] + _dot(a_ref[...], b_ref[...])).astype(out_dtype)

    in_specs = [pl.BlockSpec((tm, k), lambda j, i: (i, 0)), pl.BlockSpec((k, tn), lambda j, i: (0, j))]
    args = [a, b]
    if residual is not None:
        in_specs.append(pl.BlockSpec((tm, tn), lambda j, i: (i, j)))
        args.append(residual)
    return pl.pallas_call(
        body, name=name, grid=(n // tn, m // tm), in_specs=in_specs,
        out_specs=pl.BlockSpec((tm, tn), lambda j, i: (i, j)),
        out_shape=jax.ShapeDtypeStruct((m, n), out_dtype), compiler_params=_params(),
    )(*args)


def _rms_bwd_tail(dy_ref, first, x_ref, g_ref, dres_ref, dx_ref, dxb_ref, dg_ref):
    @pl.when(first)
    def _():
        dg_ref[...] = jnp.zeros_like(dg_ref)

    gain = g_ref[...]
    for r in range(0, dy_ref.shape[0], LANE):
        rows = pl.ds(r, min(LANE, dy_ref.shape[0] - r))
        dx, dgain = _rms_vjp(x_ref[rows, :], gain, dy_ref[rows, :])
        dx = dx + dres_ref[rows, :]
        dx_ref[rows, :] = dx
        dxb_ref[rows, :] = dx.astype(BF)
        dg_ref[...] += dgain


def _mm_nt(a, b, out_dtype, name, tm_pref=512, tn_pref=1024, tk_pref=2048, rms=None, dep=None):
    m, k = a.shape
    n = b.shape[0]
    tm, tn, tk = _tile(m, tm_pref, 16), _tile(n, tn_pref, LANE), _tile(k, tk_pref, LANE)
    nk = k // tk
    assert rms is None or tn == n

    def body(*refs):
        if rms is None:
            a_ref, b_ref, o_ref, acc_ref = refs
        else:
            a_ref, b_ref, x_ref, g_ref, dres_ref, dx_ref, dxb_ref, dg_ref, acc_ref = refs
        kk = pl.program_id(2)

        @pl.when(kk == 0)
        def _():
            acc_ref[...] = jnp.zeros_like(acc_ref)

        acc_ref[...] += _dot(a_ref[...], b_ref[...], NT)

        @pl.when(kk == nk - 1)
        def _():
            if rms is None:
                o_ref[...] = acc_ref[...].astype(out_dtype)
            else:
                _rms_bwd_tail(acc_ref, pl.program_id(1) == 0, x_ref, g_ref, dres_ref, dx_ref, dxb_ref, dg_ref)

    in_specs = [pl.BlockSpec((tm, tk), lambda j, i, kk: (i, kk)), pl.BlockSpec((tn, tk), lambda j, i, kk: (j, kk))]
    row = pl.BlockSpec((tm, tn), lambda j, i, kk: (i, j))
    if rms is None:
        args, out_specs, out_shape = [a, b], row, jax.ShapeDtypeStruct((m, n), out_dtype)
    else:
        vec = pl.BlockSpec((1, n), lambda j, i, kk: (0, 0))
        args, in_specs = [a, b, *rms], in_specs + [row, vec, row]
        out_specs = [row, row, vec]
        out_shape = [jax.ShapeDtypeStruct((m, n), F32), jax.ShapeDtypeStruct((m, n), BF),
                     jax.ShapeDtypeStruct((1, n), F32)]
    return _call(
        body, args, dep=dep, name=name, grid=(n // tn, m // tm, nk), in_specs=in_specs, out_specs=out_specs,
        out_shape=out_shape, scratch_shapes=[pltpu.VMEM((tm, tn), F32)], compiler_params=_params(),
    )


def _mm_tn(a, b, out_dtype, name, tn_pref=1152, tk_pref=512, a_transposed=False):
    (k, t) = a.shape if a_transposed else a.shape[::-1]
    n = b.shape[1]
    tn, tk = _tile(n, tn_pref, LANE), _tile(t, tk_pref, LANE if a_transposed else 16)
    nt = t // tk

    def body(a_ref, b_ref, o_ref, acc_ref):
        tt = pl.program_id(1)

        @pl.when(tt == 0)
        def _():
            acc_ref[...] = jnp.zeros_like(acc_ref)

        acc_ref[...] += _dot(a_ref[...], b_ref[...], NN if a_transposed else TN)

        @pl.when(tt == nt - 1)
        def _():
            o_ref[...] = acc_ref[...].astype(out_dtype)

    if a_transposed:
        a_spec = pl.BlockSpec((k, tk), lambda j, tt: (0, tt))
    else:
        a_spec = pl.BlockSpec((tk, k), lambda j, tt: (tt, 0))
    return pl.pallas_call(
        body, name=name, grid=(n // tn, nt),
        in_specs=[a_spec, pl.BlockSpec((tk, tn), lambda j, tt: (tt, j))],
        out_specs=pl.BlockSpec((k, tn), lambda j, tt: (0, j)),
        out_shape=jax.ShapeDtypeStruct((k, n), out_dtype),
        scratch_shapes=[pltpu.VMEM((k, tn), F32)], compiler_params=_params(),
    )(a, b)


FFN_COLS = 512


def _ffn_tiles(t, fc):
    return _tile(t, FFN_ROWS, 16), _tile(fc, FFN_COLS, LANE)


def _slabs(tm, rows=256):
    step = rows if tm % rows == 0 else tm
    return [pl.ds(r, step) for r in range(0, tm, step)]


def _ffn_gate_up(hn, wg_t, wu_t, name):
    t, d = hn.shape
    fc = wg_t.shape[0]
    tm, tn = _ffn_tiles(t, fc)

    def body(h_ref, wg_ref, wu_ref, g_ref, u_ref, a_ref):
        for rows in _slabs(tm):
            h = h_ref[rows, :]
            g = _dot(h, wg_ref[...], NT)
            u = _dot(h, wu_ref[...], NT)
            g_ref[rows, :] = g.astype(BF)
            u_ref[rows, :] = u.astype(BF)
            a_ref[rows, :] = (g * _sig(g) * u).astype(BF)

    wspec = pl.BlockSpec((tn, d), lambda j, i: (j, 0))
    hid = pl.BlockSpec((tm, tn), lambda j, i: (i, j))
    out = jax.ShapeDtypeStruct((t, fc), BF)
    return pl.pallas_call(
        body, name=name, grid=(fc // tn, t // tm),
        in_specs=[pl.BlockSpec((tm, d), lambda j, i: (i, 0)), wspec, wspec],
        out_specs=[hid, hid, hid], out_shape=[out, out, out], compiler_params=_params(),
    )(hn, wg_t, wu_t)


def _ffn_gate(hn, wg_t, name):
    t, d = hn.shape
    fc = wg_t.shape[0]
    tm, tn = _ffn_tiles(t, fc)

    def body(h_ref, wg_ref, g_ref):
        g_ref[...] = _dot(h_ref[...], wg_ref[...], NT)

    return pl.pallas_call(
        body, name=name, grid=(fc // tn, t // tm),
        in_specs=[pl.BlockSpec((tm, d), lambda j, i: (i, 0)), pl.BlockSpec((tn, d), lambda j, i: (j, 0))],
        out_specs=pl.BlockSpec((tm, tn), lambda j, i: (i, j)),
        out_shape=jax.ShapeDtypeStruct((t, fc), F32), compiler_params=_params(),
    )(hn, wg_t)


def _ffn_up_act(hn, wu_t, g, name):
    t, d = hn.shape
    fc = wu_t.shape[0]
    tm, tn = _ffn_tiles(t, fc)

    def body(h_ref, wu_ref, g_ref, gb_ref, u_ref, a_ref):
        for rows in _slabs(tm):
            u = _dot(h_ref[rows, :], wu_ref[...], NT)
            gv = g_ref[rows, :]
            gb_ref[rows, :] = gv.astype(BF)
            u_ref[rows, :] = u.astype(BF)
            a_ref[rows, :] = (gv * _sig(gv) * u).astype(BF)

    hid = pl.BlockSpec((tm, tn), lambda j, i: (i, j))
    out = jax.ShapeDtypeStruct((t, fc), BF)
    return pl.pallas_call(
        body, name=name, grid=(fc // tn, t // tm),
        in_specs=[pl.BlockSpec((tm, d), lambda j, i: (i, 0)), pl.BlockSpec((tn, d), lambda j, i: (j, 0)), hid],
        out_specs=[hid, hid, hid], out_shape=[out, out, out], compiler_params=_params(),
    )(hn, wu_t, g)


def _ffn_down(act, wd, xres, name):
    t, fc = act.shape
    d = wd.shape[1]
    tm, tk = _ffn_tiles(t, fc)

    def body(a_ref, w_ref, x_ref, o_ref):
        @pl.when(pl.program_id(1) == 0)
        def _():
            o_ref[...] = x_ref[...]

        o_ref[...] += 0.5 * _dot(a_ref[...], w_ref[...])

    row = pl.BlockSpec((tm, d), lambda i, k: (i, 0))
    return pl.pallas_call(
        body, name=name, grid=(t // tm, fc // tk),
        in_specs=[pl.BlockSpec((tm, tk), lambda i, k: (i, k)), pl.BlockSpec((tk, d), lambda i, k: (k, 0)), row],
        out_specs=row, out_shape=jax.ShapeDtypeStruct((t, d), F32), compiler_params=_params(),
    )(act, wd, xres)


def _ffn_bwd_hidden(dxb, wd, g, u, name):
    t, d = dxb.shape
    fc = wd.shape[0]
    tm, tn = _ffn_tiles(t, fc)

    def body(dx_ref, w_ref, g_ref, u_ref, dg_ref, du_ref):
        for rows in _slabs(tm):
            dh = 0.5 * _dot(dx_ref[rows, :], w_ref[...], NT)
            gv = g_ref[rows, :].astype(F32)
            uv = u_ref[rows, :].astype(F32)
            s = _sig(gv)
            dg_ref[rows, :] = (dh * uv * (s * (1.0 + gv * (1.0 - s)))).astype(BF)
            du_ref[rows, :] = (dh * (gv * s)).astype(BF)

    hid = pl.BlockSpec((tm, tn), lambda j, i: (i, j))
    out = jax.ShapeDtypeStruct((t, fc), BF)
    return pl.pallas_call(
        body, name=name, grid=(fc // tn, t // tm),
        in_specs=[pl.BlockSpec((tm, d), lambda j, i: (i, 0)), pl.BlockSpec((tn, d), lambda j, i: (j, 0)), hid, hid],
        out_specs=[hid, hid], out_shape=[out, out], compiler_params=_params(),
    )(dxb, wd, g, u)


def _ffn_dw(lhs, rhs, scale, name, dep=None):
    n = len(lhs)
    t, fc = lhs[0].shape
    d = rhs.shape[1]
    tk, tn = _tile(t, DW_ROWS, 16), _tile(fc, FFN_COLS, LANE)
    nt = t // tk

    def body(*refs):
        l_refs, r_ref, o_refs, acc_refs = refs[:n], refs[n], refs[n + 1:2 * n + 1], refs[2 * n + 1:]
        tt = pl.program_id(1)
        r = r_ref[...]
        for l_ref, o_ref, acc_ref in zip(l_refs, o_refs, acc_refs):
            @pl.when(tt == 0)
            def _():
                acc_ref[...] = jnp.zeros_like(acc_ref)

            acc_ref[...] += _dot(l_ref[...], r, TN)

            @pl.when(tt == nt - 1)
            def _():
                o_ref[...] = (scale * acc_ref[...]).astype(BF)

    lspec = pl.BlockSpec((tk, tn), lambda j, tt: (tt, j))
    ospec = pl.BlockSpec((tn, d), lambda j, tt: (j, 0))
    out = jax.ShapeDtypeStruct((fc, d), BF)
    return _call(
        body, [*lhs, rhs], dep=dep, name=name, grid=(fc // tn, nt),
        in_specs=[lspec] * n + [pl.BlockSpec((tk, d), lambda j, tt: (tt, 0))],
        out_specs=[ospec] * n, out_shape=[out] * n,
        scratch_shapes=[pltpu.VMEM((tn, d), F32)] * n, compiler_params=_params(),
    )


def _rms_bwd(dy, x, gain, dres, name):
    t, d = x.shape
    tr = _tile(t, 256, 16)

    def body(dy_ref, x_ref, g_ref, dres_ref, dx_ref, dxb_ref, dg_ref):
        _rms_bwd_tail(dy_ref, pl.program_id(0) == 0, x_ref, g_ref, dres_ref, dx_ref, dxb_ref, dg_ref)

    row = pl.BlockSpec((tr, d), lambda i: (i, 0))
    vec = pl.BlockSpec((1, d), lambda i: (0, 0))
    return pl.pallas_call(
        body, name=name, grid=(t // tr,),
        in_specs=[row, row, vec, row], out_specs=[row, row, vec],
        out_shape=[jax.ShapeDtypeStruct((t, d), F32), jax.ShapeDtypeStruct((t, d), BF),
                   jax.ShapeDtypeStruct((1, d), F32)],
        compiler_params=_params(),
    )(dy, x, gain, dres)


def _ffn_bwd_input(dg, du, wg_t, wu_t, name, dep=None):
    t, fc = dg.shape
    d = wg_t.shape[1]
    tm, tk = _ffn_tiles(t, fc)

    def body(dg_ref, du_ref, wg_ref, wu_ref, o_ref):
        @pl.when(pl.program_id(1) == 0)
        def _():
            o_ref[...] = jnp.zeros_like(o_ref)

        o_ref[...] += _dot(dg_ref[...], wg_ref[...]) + _dot(du_ref[...], wu_ref[...])

    hid = pl.BlockSpec((tm, tk), lambda i, k: (i, k))
    wspec = pl.BlockSpec((tk, d), lambda i, k: (k, 0))
    return _call(
        body, [dg, du, wg_t, wu_t], dep=dep, name=name, grid=(t // tm, fc // tk),
        in_specs=[hid, hid, wspec, wspec],
        out_specs=pl.BlockSpec((tm, d), lambda i, k: (i, 0)),
        out_shape=jax.ShapeDtypeStruct((t, d), F32), compiler_params=_params(),
    )


def _rope_tables(t):
    pos = jnp.arange(t, dtype=F32)
    inv_freq = ROPE_THETA ** (-jnp.arange(0, ROPE_DIM, 2, dtype=F32) / ROPE_DIM)
    ang = pos[:, None] * inv_freq[None, :]
    cos, sin = jnp.cos(ang), jnp.sin(ang)
    rest = HEAD_DIM - ROPE_DIM
    one = jnp.ones((t, rest), F32)
    zero_h = jnp.zeros((t, ROPE_HALF), F32)
    zero_r = jnp.zeros((t, rest), F32)
    c = jnp.concatenate([cos, cos, one], axis=1)
    s1 = jnp.concatenate([-sin, zero_h, zero_r], axis=1)
    s2 = jnp.concatenate([zero_h, sin, zero_r], axis=1)
    return c, s1, s2


def _rope(xh, c, s1, s2):
    return xh * c + pltpu.roll(xh, HEAD_DIM - ROPE_HALF, 1) * s1 + pltpu.roll(xh, ROPE_HALF, 1) * s2


def _rope_t(dh, c, s1, s2):
    return dh * c + pltpu.roll(dh * s1, ROPE_HALF, 1) + pltpu.roll(dh * s2, HEAD_DIM - ROPE_HALF, 1)


def _mixer_prep(proj, tables, bf_pad, hd, scale):
    t, np_ = proj.shape
    tr = _tile(t, 256, 16)
    nh = hd // HEAD_DIM
    nblk = hd // LANE
    f_blk = np_ // LANE - 1

    def body(qd_ref, kd_ref, vd_ref, qf_ref, kf_ref, vf_ref, fl_ref, c_ref, s1_ref, s2_ref, b_ref,
             oqd, okd, ovd, oqf, okf, ovf, olog):
        c, s1, s2 = c_ref[...], s1_ref[...], s2_ref[...]
        for h in range(nh):
            sl = slice(h * HEAD_DIM, (h + 1) * HEAD_DIM)
            oqd[:, sl] = (_rope(qd_ref[:, sl], c, s1, s2) * scale).astype(BF)
            okd[:, sl] = _rope(kd_ref[:, sl], c, s1, s2).astype(BF)
        ovd[...] = vd_ref[...].astype(BF)
        oqf[...] = (qf_ref[...] * scale).astype(BF)
        okf[...] = kf_ref[...].astype(BF)
        ovf[...] = vf_ref[...].astype(BF)
        z = fl_ref[...] + b_ref[...]
        olog[...] = jnp.minimum(z, 0.0) - jnp.log(1.0 + jnp.exp(-jnp.abs(z)))

    def col(kblk):
        return pl.BlockSpec((tr, hd), lambda i, kblk=kblk: (i, kblk))

    lane_row = pl.BlockSpec((tr, LANE), lambda i: (i, 0))
    in_specs = [col(0), col(1), col(2), col(3), col(4), col(5),
                pl.BlockSpec((tr, LANE), lambda i: (i, f_blk)),
                lane_row, lane_row, lane_row, pl.BlockSpec((1, LANE), lambda i: (0, 0))]
    o = pl.BlockSpec((tr, hd), lambda i: (i, 0))
    ob = jax.ShapeDtypeStruct((t, hd), BF)
    del nblk
    return pl.pallas_call(
        body, name="mixer_prep", grid=(t // tr,), in_specs=in_specs,
        out_specs=[o, o, o, o, o, o, lane_row],
        out_shape=[ob, ob, ob, ob, ob, ob, jax.ShapeDtypeStruct((t, LANE), F32)],
        compiler_params=_params(),
    )(proj, proj, proj, proj, proj, proj, proj, *tables, bf_pad)


def _split3(x):
    x1 = x.astype(BF)
    r1 = x - x1.astype(F32)
    x2 = r1.astype(BF)
    x3 = (r1 - x2.astype(F32)).astype(BF)
    return x1, x2, x3


def _cumsum_rows(x, reverse, name):
    t, w = x.shape
    blk = LANE
    nb = t // blk

    def body(x_ref, o_ref):
        r = lax.broadcasted_iota(jnp.int32, (blk, blk), 0)
        c = lax.broadcasted_iota(jnp.int32, (blk, blk), 1)
        tri = jnp.where((c >= r) if reverse else (c <= r), 1.0, 0.0).astype(BF)

        def step(i, carry):
            b = (nb - 1 - i) if reverse else i
            off = pl.multiple_of(b * blk, blk)
            xb = x_ref[pl.ds(off, blk), :]
            x1, x2, x3 = _split3(xb)
            o_ref[pl.ds(off, blk), :] = _dot(tri, x1) + _dot(tri, x2) + _dot(tri, x3) + carry
            return carry + jnp.sum(xb, axis=0, keepdims=True)

        lax.fori_loop(0, nb, step, jnp.zeros((1, w), F32))

    return pl.pallas_call(body, name=name, out_shape=jax.ShapeDtypeStruct((t, w), F32),
                          compiler_params=_params())(x)


ATTN_ROWS = 16


def _dil_bias_tiles(tq):
    nbias = MAX_WINDOW // tq + 1
    b = lax.broadcasted_iota(jnp.int32, (nbias, tq, tq), 0)
    i = lax.broadcasted_iota(jnp.int32, (nbias, tq, tq), 1)
    j = lax.broadcasted_iota(jnp.int32, (nbias, tq, tq), 2)
    delta = b * tq + i - j
    mult = jnp.zeros((nbias, tq, tq), F32)
    for w, dil in DIL_PATTERNS:
        mult = mult + jnp.where((delta >= 0) & (delta <= w) & (delta % dil == 0), 1.0, 0.0)
    return jnp.where(mult > 0.0, jnp.log(jnp.maximum(mult, 1.0)), NEG)


def _rep(x, width):
    return jnp.tile(x, (1, width // LANE))


def _chunks(n_rows, fn):
    for c in range(n_rows // ATTN_ROWS):
        fn(c * ATTN_ROWS)


def _causal(r0, tq, transposed):
    a = lax.broadcasted_iota(jnp.int32, (ATTN_ROWS, tq), 0) + r0
    b = lax.broadcasted_iota(jnp.int32, (ATTN_ROWS, tq), 1)
    return (a <= b) if transposed else (b <= a)


def _rows8(x):
    return jnp.transpose(x)[:8, :]


def _attn_fwd(mode, q, k, v, bias, tq, name):
    t, hd = q.shape
    nh = hd // HEAD_DIM
    nb = t // tq
    wb = MAX_WINDOW // tq
    fox = mode == "fox"

    def body(q_ref, k_ref, v_ref, b_ref, o_ref, lse_ref, lse_row_ref, s_ref, p_ref, m_ref, l_ref, acc_ref):
        qi = pl.program_id(1)
        qb = q_ref[...]
        m_ref[...] = jnp.full_like(m_ref, NEG)
        l_ref[...] = jnp.zeros_like(l_ref)
        acc_ref[...] = jnp.zeros_like(acc_ref)

        def tile(kj, diag):
            off = pl.multiple_of(kj * tq, tq)
            s_ref[...] = _dot(qb, k_ref[pl.ds(off, tq), :], NT)
            if fox:
                brow = b_ref[qi][:, :1] - b_ref[kj]

            def chunk(r0):
                rows = pl.ds(r0, ATTN_ROWS)
                if fox:
                    s = s_ref[rows, :] + brow
                    if diag:
                        s = jnp.where(_causal(r0, tq, False), s, NEG)
                else:
                    s = s_ref[rows, :] + b_ref[qi - kj, rows, :]
                m_old = m_ref[rows, :]
                m_new = jnp.maximum(m_old, jnp.max(s, axis=1, keepdims=True))
                p = jnp.exp(s - _rep(m_new, tq))
                alpha = jnp.exp(m_old - m_new)
                l_ref[rows, :] = alpha * l_ref[rows, :] + jnp.sum(p, axis=1, keepdims=True)
                m_ref[rows, :] = m_new
                acc_ref[rows, :] = alpha * acc_ref[rows, :]
                p_ref[rows, :] = p.astype(BF)

            _chunks(tq, chunk)
            acc_ref[...] += _dot(p_ref[...], v_ref[pl.ds(off, tq), :])

        tile(qi, True)
        if fox:
            lax.fori_loop(0, qi, lambda kj, c: (tile(kj, False), c)[1], 0)
        else:
            lax.fori_loop(1, jnp.minimum(qi, wb) + 1, lambda i, c: (tile(qi - i, False), c)[1], 0)
        o_ref[...] = (acc_ref[...] / l_ref[...]).astype(BF)
        lse = m_ref[...] + jnp.log(l_ref[...])
        lse_ref[...] = lse
        lse_row_ref[...] = _rows8(lse)

    qspec = pl.BlockSpec((tq, HEAD_DIM), lambda h, i: (i, h))
    kvspec = pl.BlockSpec((t, HEAD_DIM), lambda h, i: (0, h))
    repspec = pl.BlockSpec((None, tq, LANE), lambda h, i: (h, i, 0))
    row8spec = pl.BlockSpec((None, None, 8, tq), lambda h, i: (h, i, 0, 0))
    if fox:
        bspec = pl.BlockSpec((None, nb, 1, tq), lambda h, i: (h, 0, 0, 0))
    else:
        bspec = pl.BlockSpec((wb + 1, tq, tq), lambda h, i: (0, 0, 0))
    return pl.pallas_call(
        body, name=name, grid=(nh, nb), in_specs=[qspec, kvspec, kvspec, bspec],
        out_specs=[qspec, repspec, row8spec],
        out_shape=[jax.ShapeDtypeStruct((t, hd), BF), jax.ShapeDtypeStruct((nh, t, LANE), F32),
                   jax.ShapeDtypeStruct((nh, nb, 8, tq), F32)],
        scratch_shapes=[pltpu.VMEM((tq, tq), F32), pltpu.VMEM((tq, tq), BF), pltpu.VMEM((tq, LANE), F32),
                        pltpu.VMEM((tq, LANE), F32), pltpu.VMEM((tq, HEAD_DIM), F32)],
        compiler_params=_params(),
    )(q, k, v, bias)


def _attn_bwd_dq(mode, q, k, v, o, do, lse, bias, tq, name, dep=None):
    t, hd = q.shape
    nh = hd // HEAD_DIM
    nb = t // tq
    wb = MAX_WINDOW // tq
    fox = mode == "fox"

    def body(q_ref, k_ref, v_ref, o_ref, do_ref, lse_ref, b_ref, dq_ref, dl_row_ref,
             s_ref, dp_ref, x_ref, y_ref, acc_ref, acc2_ref, dl_ref):
        qi = pl.program_id(1)
        qb = q_ref[...]
        dob = do_ref[...]
        acc_ref[...] = jnp.zeros_like(acc_ref)
        if fox:
            acc2_ref[...] = jnp.zeros_like(acc2_ref)
            dl_ref[...] = jnp.zeros_like(dl_ref)
        else:
            prod = o_ref[...].astype(F32) * dob.astype(F32)
            dl_ref[...] = jnp.broadcast_to(jnp.sum(prod, axis=1, keepdims=True), (tq, LANE))

        def tile(kj, diag):
            off = pl.multiple_of(kj * tq, tq)
            kb = k_ref[pl.ds(off, tq), :]
            s_ref[...] = _dot(qb, kb, NT)
            dp_ref[...] = _dot(dob, v_ref[pl.ds(off, tq), :], NT)
            if fox:
                brow = b_ref[qi][:, :1] - b_ref[kj]

            def chunk(r0):
                rows = pl.ds(r0, ATTN_ROWS)
                lse_c = _rep(lse_ref[rows, :], tq)
                if fox:
                    s = s_ref[rows, :] + brow
                    if diag:
                        s = jnp.where(_causal(r0, tq, False), s, NEG)
                    p = jnp.exp(s - lse_c)
                    pdp = p * dp_ref[rows, :]
                    dl_ref[rows, :] += jnp.sum(pdp, axis=1, keepdims=True)
                    x_ref[rows, :] = pdp.astype(BF)
                    y_ref[rows, :] = p.astype(BF)
                else:
                    p = jnp.exp(s_ref[rows, :] + b_ref[qi - kj, rows, :] - lse_c)
                    x_ref[rows, :] = (p * (dp_ref[rows, :] - _rep(dl_ref[rows, :], tq))).astype(BF)

            _chunks(tq, chunk)
            acc_ref[...] += _dot(x_ref[...], kb)
            if fox:
                acc2_ref[...] += _dot(y_ref[...], kb)

        tile(qi, True)
        if fox:
            lax.fori_loop(0, qi, lambda kj, c: (tile(kj, False), c)[1], 0)
            dq_ref[...] = acc_ref[...] - dl_ref[...] * acc2_ref[...]
        else:
            lax.fori_loop(1, jnp.minimum(qi, wb) + 1, lambda i, c: (tile(qi - i, False), c)[1], 0)
            dq_ref[...] = acc_ref[...]
        dl_row_ref[...] = _rows8(dl_ref[...])

    qspec = pl.BlockSpec((tq, HEAD_DIM), lambda h, i: (i, h))
    kvspec = pl.BlockSpec((t, HEAD_DIM), lambda h, i: (0, h))
    repspec = pl.BlockSpec((None, tq, LANE), lambda h, i: (h, i, 0))
    row8spec = pl.BlockSpec((None, None, 8, tq), lambda h, i: (h, i, 0, 0))
    if fox:
        bspec = pl.BlockSpec((None, nb, 1, tq), lambda h, i: (h, 0, 0, 0))
    else:
        bspec = pl.BlockSpec((wb + 1, tq, tq), lambda h, i: (0, 0, 0))
    return _call(
        body, [q, k, v, o, do, lse, bias], dep=dep, name=name, grid=(nh, nb),
        in_specs=[qspec, kvspec, kvspec, qspec, qspec, repspec, bspec],
        out_specs=[qspec, row8spec],
        out_shape=[jax.ShapeDtypeStruct((t, hd), F32), jax.ShapeDtypeStruct((nh, nb, 8, tq), F32)],
        scratch_shapes=[pltpu.VMEM((tq, tq), F32), pltpu.VMEM((tq, tq), F32), pltpu.VMEM((tq, tq), BF),
                        pltpu.VMEM((tq, tq), BF), pltpu.VMEM((tq, HEAD_DIM), F32),
                        pltpu.VMEM((tq, HEAD_DIM), F32), pltpu.VMEM((tq, LANE), F32)],
        compiler_params=_params(),
    )


def _attn_bwd_dkv(mode, q, k, v, do, lse_row, dl_row, bias_t, c_row, tq, name):
    t, hd = q.shape
    nh = hd // HEAD_DIM
    nb = t // tq
    wb = MAX_WINDOW // tq
    fox = mode == "fox"

    def body(*refs):
        if fox:
            (q_ref, k_ref, v_ref, do_ref, lse_ref, dl_ref, b_ref, cq_ref, dk_ref, dv_ref, dc_row_ref,
             s_ref, dp_ref, x_ref, y_ref, dc_ref) = refs
        else:
            q_ref, k_ref, v_ref, do_ref, lse_ref, dl_ref, b_ref, dk_ref, dv_ref, s_ref, dp_ref, x_ref, y_ref = refs
        kj = pl.program_id(1)
        kb = k_ref[...]
        vb = v_ref[...]
        dk_ref[...] = jnp.zeros_like(dk_ref)
        dv_ref[...] = jnp.zeros_like(dv_ref)
        if fox:
            dc_ref[...] = jnp.zeros_like(dc_ref)

        def tile(qi, diag):
            off = pl.multiple_of(qi * tq, tq)
            qb = q_ref[pl.ds(off, tq), :]
            dob = do_ref[pl.ds(off, tq), :]
            s_ref[...] = _dot(kb, qb, NT)
            dp_ref[...] = _dot(vb, dob, NT)
            lse_r = lse_ref[qi, 0:1, :]
            dl_r = dl_ref[qi, 0:1, :]
            if fox:
                kbias = cq_ref[qi][:, :1] - b_ref[...]

            def chunk(r0):
                rows = pl.ds(r0, ATTN_ROWS)
                if fox:
                    s = s_ref[rows, :] + _rep(kbias[r0:r0 + ATTN_ROWS, :], tq)
                    if diag:
                        s = jnp.where(_causal(r0, tq, True), s, NEG)
                else:
                    s = s_ref[rows, :] + b_ref[qi - kj, rows, :]
                pt = jnp.exp(s - lse_r)
                dst = pt * (dp_ref[rows, :] - dl_r)
                x_ref[rows, :] = pt.astype(BF)
                y_ref[rows, :] = dst.astype(BF)
                if fox:
                    dc_ref[rows, :] -= jnp.sum(dst, axis=1, keepdims=True)

            _chunks(tq, chunk)
            dv_ref[...] += _dot(x_ref[...], dob)
            dk_ref[...] += _dot(y_ref[...], qb)

        tile(kj, True)
        hi = nb if fox else jnp.minimum(kj + wb + 1, nb)
        lax.fori_loop(kj + 1, hi, lambda qi, c: (tile(qi, False), c)[1], 0)
        if fox:
            dc_row_ref[...] = _rows8(dc_ref[...])

    blkspec = pl.BlockSpec((tq, HEAD_DIM), lambda h, j: (j, h))
    fullspec = pl.BlockSpec((t, HEAD_DIM), lambda h, j: (0, h))
    rows8spec = pl.BlockSpec((None, nb, 8, tq), lambda h, j: (h, 0, 0, 0))
    repspec = pl.BlockSpec((None, tq, LANE), lambda h, j: (h, j, 0))
    in_specs = [fullspec, blkspec, blkspec, fullspec, rows8spec, rows8spec]
    args = [q, k, v, do, lse_row, dl_row, bias_t]
    out_specs = [blkspec, blkspec]
    out_shape = [jax.ShapeDtypeStruct((t, hd), F32), jax.ShapeDtypeStruct((t, hd), F32)]
    scratch = [pltpu.VMEM((tq, tq), F32), pltpu.VMEM((tq, tq), F32), pltpu.VMEM((tq, tq), BF),
               pltpu.VMEM((tq, tq), BF)]
    if fox:
        in_specs += [repspec, pl.BlockSpec((None, nb, 1, tq), lambda h, j: (h, 0, 0, 0))]
        args.append(c_row)
        out_specs.append(pl.BlockSpec((None, None, 8, tq), lambda h, j: (h, j, 0, 0)))
        out_shape.append(jax.ShapeDtypeStruct((nh, nb, 8, tq), F32))
        scratch.append(pltpu.VMEM((tq, LANE), F32))
    else:
        in_specs.append(pl.BlockSpec((wb + 1, tq, tq), lambda h, j: (0, 0, 0)))
    return pl.pallas_call(
        body, name=name, grid=(nh, nb), in_specs=in_specs, out_specs=out_specs, out_shape=out_shape,
        scratch_shapes=scratch, compiler_params=_params(),
    )(*args)


def _gate_specs(t, d, hd, tr):
    row = pl.BlockSpec((tr, d), lambda i: (i, 0))
    vec = pl.BlockSpec((1, d), lambda i: (0, 0))
    base = 6 * hd // d
    gd = pl.BlockSpec((tr, d), lambda i: (i, base))
    gf = pl.BlockSpec((tr, d), lambda i: (i, base + 1))
    return row, vec, gd, gf


def _proj_merge(yd, yf, wpd, wpf, proj, b_d, b_f, hd):
    t = yd.shape[0]
    d = wpd.shape[1]
    tr = _tile(t, 256, 16)
    row, vec, gd, gf = _gate_specs(t, d, hd, tr)

    def body(yd_ref, yf_ref, wd_ref, wf_ref, gd_ref, gf_ref, bd_ref, bf_ref, pd_ref, pf_ref, o_ref):
        pd = _dot(yd_ref[...], wd_ref[...])
        pf = _dot(yf_ref[...], wf_ref[...])
        pd_ref[...] = pd
        pf_ref[...] = pf
        o_ref[...] = (_sig(gd_ref[...] + bd_ref[...]) * pd + _sig(gf_ref[...] + bf_ref[...]) * pf).astype(BF)

    yspec = pl.BlockSpec((tr, hd), lambda i: (i, 0))
    wspec = pl.BlockSpec((hd, d), lambda i: (0, 0))
    f32 = jax.ShapeDtypeStruct((t, d), F32)
    return pl.pallas_call(
        body, name="proj_merge", grid=(t // tr,), in_specs=[yspec, yspec, wspec, wspec, gd, gf, vec, vec],
        out_specs=[row, row, row], out_shape=[f32, f32, jax.ShapeDtypeStruct((t, d), BF)],
        compiler_params=_params(),
    )(yd, yf, wpd, wpf, proj, proj, b_d, b_f)


def _merge_bwd(dm, pd, pf, proj, b_d, b_f, hd):
    t, d = pd.shape
    tr = _tile(t, 256, 16)
    row, vec, gd, gf = _gate_specs(t, d, hd, tr)

    def body(dm_ref, pd_ref, pf_ref, gd_ref, gf_ref, bd_ref, bf_ref,
             dpd_ref, dpf_ref, dgd_ref, dgf_ref, dbd_ref, dbf_ref):
        dmv = dm_ref[...]
        sd = _sig(gd_ref[...] + bd_ref[...])
        sf = _sig(gf_ref[...] + bf_ref[...])
        dgd = dmv * pd_ref[...] * (sd * (1.0 - sd))
        dgf = dmv * pf_ref[...] * (sf * (1.0 - sf))
        dpd_ref[...] = (dmv * sd).astype(BF)
        dpf_ref[...] = (dmv * sf).astype(BF)
        dgd_ref[...] = dgd.astype(BF)
        dgf_ref[...] = dgf.astype(BF)

        @pl.when(pl.program_id(0) == 0)
        def _():
            dbd_ref[...] = jnp.zeros_like(dbd_ref)
            dbf_ref[...] = jnp.zeros_like(dbf_ref)

        dbd_ref[...] += jnp.sum(dgd, axis=0, keepdims=True)
        dbf_ref[...] += jnp.sum(dgf, axis=0, keepdims=True)

    ob = jax.ShapeDtypeStruct((t, d), BF)
    ov = jax.ShapeDtypeStruct((1, d), F32)
    return pl.pallas_call(
        body, name="merge_bwd", grid=(t // tr,), in_specs=[row, row, row, gd, gf, vec, vec],
        out_specs=[row, row, row, row, vec, vec], out_shape=[ob, ob, ob, ob, ov, ov],
        compiler_params=_params(),
    )(dm, pd, pf, proj, proj, b_d, b_f)


def _assemble_dproj(dqd, dkd, dvd, dqf, dkf, dvf, dgd, dgf, dlogf, proj, tables, bf_pad, scale):
    t, np_ = proj.shape
    hd = dqd.shape[1]
    d = dgd.shape[1]
    nh = hd // HEAD_DIM
    tr = _tile(t, 256, 16)
    f_blk = np_ // LANE - 1

    def body(dqd_ref, dkd_ref, dvd_ref, dqf_ref, dkf_ref, dvf_ref, dgd_ref, dgf_ref, dlog_ref, fl_ref,
             c_ref, s1_ref, s2_ref, b_ref, o_ref, db_ref):
        c, s1, s2 = c_ref[...], s1_ref[...], s2_ref[...]
        for h in range(nh):
            sl = slice(h * HEAD_DIM, (h + 1) * HEAD_DIM)
            o_ref[:, sl] = (_rope_t(dqd_ref[:, sl], c, s1, s2) * scale).astype(BF)
            o_ref[:, hd + h * HEAD_DIM:hd + (h + 1) * HEAD_DIM] = _rope_t(dkd_ref[:, sl], c, s1, s2).astype(BF)
        o_ref[:, 2 * hd:3 * hd] = dvd_ref[...].astype(BF)
        o_ref[:, 3 * hd:4 * hd] = (dqf_ref[...] * scale).astype(BF)
        o_ref[:, 4 * hd:5 * hd] = dkf_ref[...].astype(BF)
        o_ref[:, 5 * hd:6 * hd] = dvf_ref[...].astype(BF)
        o_ref[:, 6 * hd:6 * hd + d] = dgd_ref[...]
        o_ref[:, 6 * hd + d:6 * hd + 2 * d] = dgf_ref[...]
        z = fl_ref[...] + b_ref[...]
        dfl = dlog_ref[...] * _sig(-z)
        o_ref[:, 6 * hd + 2 * d:] = dfl.astype(BF)

        @pl.when(pl.program_id(0) == 0)
        def _():
            db_ref[...] = jnp.zeros_like(db_ref)

        db_ref[...] += jnp.sum(dfl, axis=0, keepdims=True)

    head = pl.BlockSpec((tr, hd), lambda i: (i, 0))
    row = pl.BlockSpec((tr, d), lambda i: (i, 0))
    lane_row = pl.BlockSpec((tr, LANE), lambda i: (i, 0))
    lane_vec = pl.BlockSpec((1, LANE), lambda i: (0, 0))
    return pl.pallas_call(
        body, name="assemble_dproj", grid=(t // tr,),
        in_specs=[head] * 6 + [row, row, lane_row, pl.BlockSpec((tr, LANE), lambda i: (i, f_blk)),
                               lane_row, lane_row, lane_row, lane_vec],
        out_specs=[pl.BlockSpec((tr, np_), lambda i: (i, 0)), lane_vec],
        out_shape=[jax.ShapeDtypeStruct((t, np_), BF), jax.ShapeDtypeStruct((1, LANE), F32)],
        compiler_params=_params(),
    )(dqd, dkd, dvd, dqf, dkf, dvf, dgd, dgf, dlogf, proj, *tables, bf_pad)


def _to_rows(a, tq):
    h, t = a.shape
    return a.reshape(h, t // tq, 1, tq)


def kernel(x, ffn1_norm, ffn1_w_gate, ffn1_w_up, ffn1_w_down, mix_norm, w_in, b_forget, b_gate_dil, b_gate_fox, w_proj_dil, w_proj_fox, w_out, ffn2_norm, ffn2_w_gate, ffn2_w_up, ffn2_w_down, final_norm, loss_target, m_ffn1_norm, m_ffn1_w_gate, m_ffn1_w_up, m_ffn1_w_down, m_mix_norm, m_w_in, m_b_forget, m_b_gate_dil, m_b_gate_fox, m_w_proj_dil, m_w_proj_fox, m_w_out, m_ffn2_norm, m_ffn2_w_gate, m_ffn2_w_up, m_ffn2_w_down, m_final_norm, v_ffn1_norm, v_ffn1_w_gate, v_ffn1_w_up, v_ffn1_w_down, v_mix_norm, v_w_in, v_b_forget, v_b_gate_dil, v_b_gate_fox, v_w_proj_dil, v_w_proj_fox, v_w_out, v_ffn2_norm, v_ffn2_w_gate, v_ffn2_w_up, v_ffn2_w_down, v_final_norm):
    t, d = x.shape[1], x.shape[2]
    hd = w_proj_dil.shape[1]
    nh = hd // HEAD_DIM
    n_f = b_forget.shape[1]
    cols = w_in.shape[2]
    in_cols = N_DEV * cols
    assert in_cols == 6 * hd + n_f + 2 * d and n_f == nh and n_f <= LANE
    np_ = 6 * hd + 2 * d + LANE
    scale = HEAD_DIM ** -0.5
    tq = _tile(t, 512, LANE)
    assert MAX_WINDOW % tq == 0 and tq % 16 == 0

    x2d = x[0]
    tgt = loss_target[0]

    def rows(w):
        return jnp.swapaxes(w, 1, 2)

    fc = N_DEV * ffn1_w_down.shape[1]
    ag_order = [rows(ffn1_w_gate), rows(ffn1_w_up), ffn1_w_down, w_in, w_proj_dil, w_proj_fox, w_out,
                rows(ffn2_w_gate), rows(ffn2_w_up), ffn2_w_down]
    ag_first, tok = _exchange_start([w[0].astype(BF) for w in ag_order[:2]], True, "ag_start_first", ks=FIRST_LEVEL)
    ag_rest, ag_token = _exchange_start([w[0].astype(BF) for w in ag_order[2:]], True, "ag_start", dep=tok,
                                        ks=FIRST_LEVEL)
    ag = ag_first + ag_rest

    def relay(idx, after, name):
        for i, h in zip(idx, _gather_relay([ag[i] for i in idx], after, name)):
            ag[i] = h

    def gathered(idx, after, name):
        return _gather_wait([ag[i] for i in idx], after, name)

    def ffn_weight(idx, after, name):
        return [w.reshape(fc, d) for w in gathered(idx, after, name)]

    tables = _rope_tables(t)
    bf_pad = jnp.pad(b_forget, ((0, 0), (0, LANE - n_f)))

    hn1, = _rms_fwd(x2d, ffn1_norm, "rms_ffn1", dep=ag_token)
    relay([0], hn1, "ag_relay_ffn1_gate")
    wg1, = ffn_weight([0], hn1, "ag_wait_ffn1_gate")
    g1_f32 = _ffn_gate(hn1, wg1, "ffn1_gate")
    relay([1], g1_f32, "ag_relay_ffn1_up")
    wu1, = ffn_weight([1], g1_f32, "ag_wait_ffn1_up")
    relay([2], wu1, "ag_relay_ffn1_down")
    g1, u1, a1 = _ffn_up_act(hn1, wu1, g1_f32, "ffn1_up_act")
    wd1, = ffn_weight([2], a1, "ag_wait_ffn1_down")
    relay([3], wd1, "ag_relay_w_in")
    x1 = _ffn_down(a1, wd1, x2d, "ffn1_down")

    hm, hm_t = _rms_fwd(x1, mix_norm, "rms_mix", with_transpose=True)
    win_g, = gathered([3], hm, "ag_wait_w_in")
    relay([4, 5, 6], win_g, "ag_relay_mixer")
    segments = [(0, 6 * hd), (6 * hd + n_f, in_cols), (6 * hd, 6 * hd + n_f)]
    pieces = []
    for lo, hi in segments:
        for j in range(lo // cols, (hi - 1) // cols + 1):
            s, e = max(lo, j * cols), min(hi, (j + 1) * cols)
            pieces.append(win_g[j, :, s - j * cols:e - j * cols])
    win_p = jnp.concatenate(pieces + [jnp.zeros((d, LANE - n_f), BF)], axis=1)
    proj = _mm_nn(hm, win_p, F32, "w_in_fwd")
    qd, kd, vd, qf, kf, vf, logf = _mixer_prep(proj, tables, bf_pad, hd, scale)
    csum = _cumsum_rows(logf, False, "cumsum_logf")
    c_heads = csum[:, :nh].T
    c_row = _to_rows(c_heads, tq)
    c_rep = jnp.broadcast_to(c_heads[:, :, None], (nh, t, LANE))
    dil_bias = _dil_bias_tiles(tq)
    dil_bias_t = dil_bias.transpose(0, 2, 1)
    relay([7, 8, 9], qd, "ag_relay_ffn2")
    yd, lse_d, lse_d_row = _attn_fwd("dil", qd, kd, vd, dil_bias, tq, "attn_dil_fwd")
    yf, lse_f, lse_f_row = _attn_fwd("fox", qf, kf, vf, c_row, tq, "attn_fox_fwd")
    wpd_g, wpf_g = gathered([4, 5], yf, "ag_wait_proj")
    wpd = wpd_g.transpose(1, 0, 2).reshape(hd, d)
    wpf = wpf_g.transpose(1, 0, 2).reshape(hd, d)
    pd, pf, merged = _proj_merge(yd, yf, wpd, wpf, proj, b_gate_dil, b_gate_fox, hd)
    wout_g, = gathered([6], merged, "ag_wait_w_out")
    wout = wout_g.reshape(d, d)
    x2 = _mm_nn(merged, wout, F32, "w_out_fwd", residual=x1, tn_pref=1024)

    hn2, = _rms_fwd(x2, ffn2_norm, "rms_ffn2")
    wg2, wu2 = ffn_weight([7, 8], hn2, "ag_wait_ffn2_gate_up")
    g2, u2, a2 = _ffn_gate_up(hn2, wg2, wu2, "ffn2_gate_up")
    wd2, = ffn_weight([9], a2, "ag_wait_ffn2_down")
    x3 = _ffn_down(a2, wd2, x2, "ffn2_down")

    dx3, dx3b, d_final, loss_lanes = _loss_head(x3, final_norm.reshape(1, d), tgt)

    def ffn_bwd(dxb, hn, g, u, a, wg_t, wu_t, wd, x_in, gain, dres, tag):
        def parts(dw):
            return dw.reshape(N_DEV, fc // N_DEV, d)

        dg, du = _ffn_bwd_hidden(dxb, wd, g, u, tag + "_bwd_hidden")
        dwd, = _ffn_dw([a], dxb, 0.5, tag + "_dw_down")
        rs_down, tok = _exchange_start([parts(dwd)], False, "rs_start_" + tag + "_down")
        dwg_t, dwu_t = _ffn_dw([dg, du], hn, 1.0, tag + "_dw_gate_up", dep=tok)
        rs_gu, tok = _exchange_start([parts(dwg_t), parts(dwu_t)], False, "rs_start_" + tag + "_gate_up")
        dhn = _ffn_bwd_input(dg, du, wg_t, wu_t, tag + "_bwd_input", dep=tok)
        dx, dx_bf, dgain = _rms_bwd(dhn, x_in, gain, dres, "rms_" + tag + "_bwd")
        return dx, dx_bf, dgain, rs_gu + rs_down

    dx2, dx2b, d_ffn2_norm, rs_ffn2 = ffn_bwd(dx3b, hn2, g2, u2, a2, wg2, wu2, wd2, x2, ffn2_norm, dx3, "ffn2")

    dmerged = _mm_nt(dx2b, wout, F32, "w_out_bwd")
    dwout = _mm_tn(merged, dx2b, BF, "w_out_dw", tn_pref=1024)
    dpd, dpf, dgd, dgf, d_bd, d_bf = _merge_bwd(dmerged, pd, pf, proj, b_gate_dil, b_gate_fox, hd)
    dyd = _mm_nt(dpd, wpd, BF, "proj_dil_bwd")
    dyf = _mm_nt(dpf, wpf, BF, "proj_fox_bwd")
    dwpd = _mm_tn(yd, dpd, BF, "proj_dil_dw", tn_pref=1024)
    dwpf = _mm_tn(yf, dpf, BF, "proj_fox_dw", tn_pref=1024)
    dwpd_c = dwpd.reshape(hd, N_DEV, d // N_DEV).transpose(1, 0, 2)
    dwpf_c = dwpf.reshape(hd, N_DEV, d // N_DEV).transpose(1, 0, 2)
    dwout_c = dwout.reshape(N_DEV, d // N_DEV, d)
    rs_mix, tok = _exchange_start([dwout_c, dwpd_c, dwpf_c], False, "rs_start_mixer")

    dqd, dl_d = _attn_bwd_dq("dil", qd, kd, vd, yd, dyd, lse_d, dil_bias, tq, "attn_dil_dq", dep=tok)
    dkd, dvd = _attn_bwd_dkv("dil", qd, kd, vd, dyd, lse_d_row, dl_d, dil_bias_t, None, tq, "attn_dil_dkv")
    dqf, dl_f = _attn_bwd_dq("fox", qf, kf, vf, yf, dyf, lse_f, c_row, tq, "attn_fox_dq")
    dkf, dvf, dc = _attn_bwd_dkv("fox", qf, kf, vf, dyf, lse_f_row, dl_f, c_rep, c_row, tq, "attn_fox_dkv")
    dc_pad = jnp.pad(dc[:, :, 0, :].reshape(nh, t).T, ((0, 0), (0, LANE - nh)))
    dlogf = _cumsum_rows(dc_pad, True, "revcumsum_dc")
    dproj, d_bforget = _assemble_dproj(dqd, dkd, dvd, dqf, dkf, dvf, dgd, dgf, dlogf, proj, tables, bf_pad, scale)

    dwin_p = _mm_tn(hm_t, dproj, BF, "w_in_dw", tk_pref=DW_ROWS, a_transposed=True)
    def perm_col(c):
        if c < 6 * hd:
            return c
        return c + 2 * d if c < 6 * hd + n_f else c - n_f

    shards = []
    for j in range(N_DEV):
        cuts = sorted({j * cols, (j + 1) * cols} | {c for c in (6 * hd, 6 * hd + n_f) if j * cols < c < (j + 1) * cols})
        shards.append(jnp.concatenate([dwin_p[:, perm_col(lo):perm_col(lo) + hi - lo]
                                       for lo, hi in zip(cuts[:-1], cuts[1:])], axis=1))
    dwin_c = jnp.stack(shards)
    rs_win, tok = _exchange_start([dwin_c], False, "rs_start_w_in")
    dx1, dx1b, d_mix_norm = _mm_nt(dproj, win_p, F32, "w_in_bwd", tn_pref=d, tk_pref=1152,
                                   rms=(x1, mix_norm, dx2), dep=tok)

    grad_x, _, d_ffn1_norm, rs_ffn1 = ffn_bwd(dx1b, hn1, g1, u1, a1, wg1, wu1, wd1, x2d, ffn1_norm, dx1, "ffn1")

    def update(handles, names, after, tag):
        recvs = _exchange_wait(handles, False, after, "rs_wait_" + tag)
        res = {}
        for recv, n in zip(recvs, names):
            turn = rows if n.endswith(("w_gate", "w_up")) else (lambda a: a)
            w, m, v = (turn(a)[0] for a in wmv[n])
            res[n] = tuple(turn(o[None]) for o in _adam_from_partials(recv, w, m, v, "adam_" + n))
        return res, res[names[-1]][0]

    wmv = {
        "ffn1_w_gate": (ffn1_w_gate, m_ffn1_w_gate, v_ffn1_w_gate),
        "ffn1_w_up": (ffn1_w_up, m_ffn1_w_up, v_ffn1_w_up),
        "ffn1_w_down": (ffn1_w_down, m_ffn1_w_down, v_ffn1_w_down),
        "w_in": (w_in, m_w_in, v_w_in),
        "w_proj_dil": (w_proj_dil, m_w_proj_dil, v_w_proj_dil),
        "w_proj_fox": (w_proj_fox, m_w_proj_fox, v_w_proj_fox),
        "w_out": (w_out, m_w_out, v_w_out),
        "ffn2_w_gate": (ffn2_w_gate, m_ffn2_w_gate, v_ffn2_w_gate),
        "ffn2_w_up": (ffn2_w_up, m_ffn2_w_up, v_ffn2_w_up),
        "ffn2_w_down": (ffn2_w_down, m_ffn2_w_down, v_ffn2_w_down),
    }
    big = {}
    after = grad_x
    for handles, names, tag in [
            (rs_ffn2, ["ffn2_w_gate", "ffn2_w_up", "ffn2_w_down"], "ffn2"),
            (rs_mix, ["w_out", "w_proj_dil", "w_proj_fox"], "mixer"),
            (rs_win, ["w_in"], "w_in"),
            (rs_ffn1, ["ffn1_w_gate", "ffn1_w_up", "ffn1_w_down"], "ffn1")]:
        res, after = update(handles, names, after, tag)
        big.update(res)

    def lanes(a):
        a = a.reshape(1, -1)
        return jnp.pad(a, ((0, 0), (0, d - a.shape[1])))

    small_names = ["ffn1_norm", "mix_norm", "b_gate_dil", "b_gate_fox", "ffn2_norm", "final_norm", "b_forget"]
    small_g = [d_ffn1_norm, d_mix_norm, d_bd, d_bf, d_ffn2_norm, d_final, d_bforget[:, :n_f]]
    small_w = [ffn1_norm, mix_norm, b_gate_dil, b_gate_fox, ffn2_norm, final_norm, b_forget]
    small_m = [m_ffn1_norm, m_mix_norm, m_b_gate_dil, m_b_gate_fox, m_ffn2_norm, m_final_norm, m_b_forget]
    small_v = [v_ffn1_norm, v_mix_norm, v_b_gate_dil, v_b_gate_fox, v_ffn2_norm, v_final_norm, v_b_forget]
    pack = lambda arrs, last: jnp.concatenate([lanes(a) for a in arrs] + [last], axis=0)
    g_all = _allreduce_small(pack(small_g, loss_lanes))
    zero_row = jnp.zeros((1, d), F32)
    one_row = jnp.ones((1, d), F32)
    s_delta, s_m, s_v = _adam_small(g_all, pack(small_w, zero_row), pack(small_m, zero_row), pack(small_v, one_row))
    loss = g_all[len(small_names), 0]

    def unpack(packed, i, like):
        return packed[i, :like.size].reshape(like.shape)

    small = {}
    for i, (n, w) in enumerate(zip(small_names, small_w)):
        small[n] = (unpack(g_all, i, w), unpack(s_delta, i, w), unpack(s_m, i, w), unpack(s_v, i, w))

    order = ["ffn1_norm", "ffn1_w_gate", "ffn1_w_up", "ffn1_w_down", "mix_norm", "w_in", "b_forget", "b_gate_dil",
             "b_gate_fox", "w_proj_dil", "w_proj_fox", "w_out", "ffn2_norm", "ffn2_w_gate", "ffn2_w_up",
             "ffn2_w_down", "final_norm"]
    res = {**big, **small}
    outs = [loss, grad_x[None]]
    for slot in range(4):
        outs += [res[n][slot] for n in order]
    return tuple(outs)
```

```python
import functools

import numpy as np
import jax
import jax.numpy as jnp
from jax import lax
from jax.experimental import pallas as pl
from jax.experimental.pallas import tpu as pltpu

BF = jnp.bfloat16
F32 = jnp.float32
MESH = pl.DeviceIdType.MESH
N_DEV = 8

HEAD_DIM = 128
ROPE_DIM = HEAD_DIM // 4
ROPE_HALF = ROPE_DIM // 2
ROPE_THETA = 500000.0
NORM_EPS = 1e-6
DIL_PATTERNS = ((128, 1), (512, 4), (2048, 16))
MAX_WINDOW = 2048
LANE = 128
NEG = -1e30

ADAM_LR = 0.001
ADAM_B1 = 0.9
ADAM_B2 = 0.999
ADAM_EPS = 1e-08
ADAM_WD = 0.01
ADAM_STEP = 10

VMEM_LIMIT_BYTES = 56 * 1024 * 1024
FFN_ROWS = 1024
DW_ROWS = 1024
ANY = pl.BlockSpec(memory_space=pl.ANY)

NN = (((1,), (0,)), ((), ()))
NT = (((1,), (1,)), ((), ()))
TN = (((0,), (0,)), ((), ()))


def _dot(a, b, dn=NN):
    return lax.dot_general(a, b, dn, preferred_element_type=F32)


def _sig(x):
    return 1.0 / (1.0 + jnp.exp(-x))


def _tile(n, pref, align):
    best = None
    t = align
    while t <= min(n, pref):
        if n % t == 0:
            best = t
        t += align
    return n if best is None else best


def _params():
    return pltpu.CompilerParams(vmem_limit_bytes=VMEM_LIMIT_BYTES)


def _call(body, args, dep=None, **kw):
    if dep is not None:
        n_in = len(args)
        inner = body

        def body(*refs):
            inner(*refs[:n_in], *refs[n_in + 1:])

        kw["in_specs"] = list(kw["in_specs"]) + [ANY]
        args = list(args) + [dep]
    return pl.pallas_call(body, **kw)(*args)


def _peers():
    x, y, c = lax.axis_index("x"), lax.axis_index("y"), lax.axis_index("c")
    me = 4 * x + 2 * y + c
    peers = []
    for k in range(1, N_DEV):
        px = 1 - x if (k >> 2) & 1 else x
        py = 1 - y if (k >> 1) & 1 else y
        pc = 1 - c if k & 1 else c
        peers.append((k, (px, py, pc), 4 * px + 2 * py + pc))
    return me, peers


HBM = pl.BlockSpec(memory_space=pltpu.HBM)
SEM = pl.BlockSpec(memory_space=pltpu.SEMAPHORE)
EFFECT = pltpu.SideEffectType.DATAFLOW_SIDE_EFFECTING


def _exchange_copy(gather, src_ref, land_ref, send_sems, recv_sems, me, k, peer, peer_flat, landing):
    return pltpu.make_async_remote_copy(
        src_ref=src_ref if gather else src_ref.at[peer_flat], dst_ref=land_ref.at[landing],
        send_sem=send_sems.at[k], recv_sem=recv_sems.at[k], device_id=peer, device_id_type=MESH)


ALL_PEERS = (1, 2, 3, 4, 5, 6, 7)
SIBLING = 1
SAME_CORE = (2, 4, 6)
FIRST_LEVEL = (SIBLING,) + SAME_CORE


def _exchange_start(srcs, gather, name, dep=None, ks=ALL_PEERS):
    n = len(srcs)
    extra = [] if dep is None else [dep]

    def body(*refs):
        src_refs, land_refs = refs[:n], refs[n:2 * n]
        refs = refs[2 * n + len(extra):]
        send_refs, recv_refs = refs[:n], refs[n:2 * n]
        token = refs[4 * n]
        me, peers = _peers()
        for i in range(n):
            for k, peer, peer_flat in peers:
                if k in ks:
                    _exchange_copy(gather, src_refs[i], land_refs[i], send_refs[i], recv_refs[i],
                                   me, k, peer, peer_flat, me).start()
        token[...] = jnp.zeros_like(token)

    lands = [lax.empty((N_DEV,) + s.shape[-2:], s.dtype) for s in srcs]
    sems = [pltpu.SemaphoreType.DMA((N_DEV,)) for _ in range(2 * n)]
    out = pl.pallas_call(
        body, name=name,
        out_shape=tuple(sems) + tuple(pltpu.HBM(a.shape, a.dtype) for a in list(srcs) + lands)
        + (jax.ShapeDtypeStruct((8, LANE), F32),),
        in_specs=[HBM] * (2 * n) + [ANY] * len(extra),
        out_specs=tuple([SEM] * (2 * n) + [HBM] * (2 * n) + [pl.BlockSpec(memory_space=pltpu.VMEM)]),
        input_output_aliases={i: 2 * n + i for i in range(2 * n)},
        compiler_params=pltpu.CompilerParams(has_side_effects=EFFECT),
    )(*[pltpu.with_memory_space_constraint(a, pltpu.HBM) for a in list(srcs) + lands], *extra)
    handles = [(out[2 * n + i], out[3 * n + i], out[i], out[n + i]) for i in range(n)]
    return handles, out[4 * n]


def _exchange_wait(handles, gather, after, name):
    n = len(handles)

    def body(*refs):
        src_refs, land_refs = refs[:n], refs[n:2 * n]
        send_refs, recv_refs = refs[2 * n:3 * n], refs[3 * n:4 * n]
        me, peers = _peers()
        for i in range(n):
            for k, peer, peer_flat in peers:
                cp = _exchange_copy(gather, src_refs[i], land_refs[i], send_refs[i], recv_refs[i],
                                    me, k, peer, peer_flat, peer_flat)
                cp.wait_send()
                cp.wait_recv()

    srcs = [h[0] for h in handles]
    lands = [h[1] for h in handles]
    out = pl.pallas_call(
        body, name=name,
        out_shape=tuple(pltpu.HBM(a.shape, a.dtype) for a in srcs + lands),
        in_specs=[HBM] * (2 * n) + [SEM] * (2 * n) + [ANY],
        out_specs=tuple([HBM] * (2 * n)),
        input_output_aliases={i: i for i in range(2 * n)},
        compiler_params=pltpu.CompilerParams(has_side_effects=EFFECT),
    )(*srcs, *lands, *[h[2] for h in handles], *[h[3] for h in handles], after)
    me = 4 * lax.axis_index("x") + 2 * lax.axis_index("y") + lax.axis_index("c")
    filled = []
    for src, land in zip(out[:n], out[n:]):
        own = src[None] if gather else lax.dynamic_slice_in_dim(src, me, 1, axis=0)
        filled.append(lax.dynamic_update_slice_in_dim(land, own, me, axis=0))
    return filled


def _gather_relay(handles, after, name):
    n = len(handles)

    def body(*refs):
        land_refs, recv_refs = refs[:n], refs[n:2 * n]
        refs = refs[2 * n + 1:]
        send2_refs, recv2_refs = refs[n:2 * n], refs[2 * n:3 * n]
        me, peers = _peers()
        sibling = peers[SIBLING - 1][1]
        for i in range(n):
            for k, peer, peer_flat in peers:
                if k in SAME_CORE:
                    block = land_refs[i].at[peer_flat]
                    pltpu.make_async_remote_copy(
                        src_ref=block, dst_ref=block, send_sem=send2_refs[i].at[k], recv_sem=recv_refs[i].at[k],
                        device_id=peer, device_id_type=MESH).wait_recv()
                    pltpu.make_async_remote_copy(
                        src_ref=block, dst_ref=block, send_sem=send2_refs[i].at[k], recv_sem=recv2_refs[i].at[k],
                        device_id=sibling, device_id_type=MESH).start()

    lands = [h[1] for h in handles]
    sems = [pltpu.SemaphoreType.DMA((N_DEV,)) for _ in range(2 * n)]
    out = pl.pallas_call(
        body, name=name,
        out_shape=tuple(pltpu.HBM(a.shape, a.dtype) for a in lands) + tuple(sems),
        in_specs=[HBM] * n + [SEM] * n + [ANY],
        out_specs=tuple([HBM] * n + [SEM] * (2 * n)),
        input_output_aliases={i: i for i in range(n)},
        compiler_params=pltpu.CompilerParams(has_side_effects=EFFECT),
    )(*lands, *[h[3] for h in handles], after)
    return [(h[0], out[i], h[2], h[3], out[n + i], out[2 * n + i]) for i, h in enumerate(handles)]


def _gather_wait(handles, after, name):
    n = len(handles)

    def body(*refs):
        src_refs, land_refs = refs[:n], refs[n:2 * n]
        send_refs, recv_refs = refs[2 * n:3 * n], refs[3 * n:4 * n]
        send2_refs, recv2_refs = refs[4 * n:5 * n], refs[5 * n:6 * n]
        me, peers = _peers()
        _, sibling, sibling_flat = peers[SIBLING - 1]
        for i in range(n):
            for k, peer, peer_flat in peers:
                if k in FIRST_LEVEL:
                    cp = _exchange_copy(True, src_refs[i], land_refs[i], send_refs[i], recv_refs[i],
                                        me, k, peer, peer_flat, peer_flat)
                    cp.wait_send()
                    if k == SIBLING:
                        cp.wait_recv()
                if k in SAME_CORE:
                    mine = land_refs[i].at[peer_flat]
                    theirs = land_refs[i].at[peer_flat ^ SIBLING]
                    cp = pltpu.make_async_remote_copy(
                        src_ref=mine, dst_ref=theirs, send_sem=send2_refs[i].at[k], recv_sem=recv2_refs[i].at[k],
                        device_id=sibling, device_id_type=MESH)
                    cp.wait_send()
                    cp.wait_recv()

    srcs = [h[0] for h in handles]
    lands = [h[1] for h in handles]
    out = pl.pallas_call(
        body, name=name,
        out_shape=tuple(pltpu.HBM(a.shape, a.dtype) for a in srcs + lands),
        in_specs=[HBM] * (2 * n) + [SEM] * (4 * n) + [ANY],
        out_specs=tuple([HBM] * (2 * n)),
        input_output_aliases={i: i for i in range(2 * n)},
        compiler_params=pltpu.CompilerParams(has_side_effects=EFFECT),
    )(*srcs, *lands, *[h[2] for h in handles], *[h[3] for h in handles],
      *[h[4] for h in handles], *[h[5] for h in handles], after)
    me = 4 * lax.axis_index("x") + 2 * lax.axis_index("y") + lax.axis_index("c")
    return [lax.dynamic_update_slice_in_dim(land, src[None], me, axis=0) for src, land in zip(out[:n], out[n:])]


def _allreduce_small(p):
    rows, d = p.shape

    def body(p_ref, o_ref, recv_ref, send_sems, recv_sems):
        me, peers = _peers()
        recv_ref[me] = p_ref[...]
        sends = []
        for k, peer, peer_flat in peers:
            cp = pltpu.make_async_remote_copy(
                src_ref=p_ref, dst_ref=recv_ref.at[me],
                send_sem=send_sems.at[k], recv_sem=recv_sems.at[k],
                device_id=peer, device_id_type=MESH)
            cp.start()
            sends.append(cp)
        for k, peer, peer_flat in peers:
            pltpu.make_async_remote_copy(
                src_ref=p_ref, dst_ref=recv_ref.at[peer_flat],
                send_sem=send_sems.at[k], recv_sem=recv_sems.at[k],
                device_id=peer, device_id_type=MESH).wait_recv()
        for cp in sends:
            cp.wait_send()
        acc = recv_ref[0]
        for s in range(1, N_DEV):
            acc = acc + recv_ref[s]
        is_loss = lax.broadcasted_iota(jnp.int32, (rows, d), 0) == rows - 1
        total = jnp.sum(jnp.where(is_loss, acc, 0.0))
        o_ref[...] = jnp.where(is_loss, total, acc)

    return pl.pallas_call(
        body, name="allreduce_small",
        out_shape=jax.ShapeDtypeStruct((rows, d), F32),
        in_specs=[pl.BlockSpec(memory_space=pltpu.VMEM)],
        out_specs=pl.BlockSpec(memory_space=pltpu.VMEM),
        scratch_shapes=[pltpu.VMEM((N_DEV, rows, d), F32),
                        pltpu.SemaphoreType.DMA((N_DEV,)), pltpu.SemaphoreType.DMA((N_DEV,))],
    )(p)


def _adam_math(w, g, m, v):
    m2 = ADAM_B1 * m + (1.0 - ADAM_B1) * g
    v2 = ADAM_B2 * v + (1.0 - ADAM_B2) * (g * g)
    m_hat = m2 / (1.0 - ADAM_B1 ** ADAM_STEP)
    v_hat = v2 / (1.0 - ADAM_B2 ** ADAM_STEP)
    delta = -ADAM_LR * (m_hat / (jnp.sqrt(v_hat) + ADAM_EPS) + ADAM_WD * w)
    return delta, m2, v2


def _adam_from_partials(parts, w, m, v, name):
    r, c = w.shape
    if r % 16 == 0:
        tr, tc = _tile(r, 256, 16), c
    else:
        tr, tc = r, _tile(c, 256, LANE)

    def body(p_ref, w_ref, m_ref, v_ref, g_out, d_out, m_out, v_out):
        g = p_ref[0].astype(F32)
        for s in range(1, N_DEV):
            g = g + p_ref[s].astype(F32)
        delta, m2, v2 = _adam_math(w_ref[...], g, m_ref[...], v_ref[...])
        g_out[...] = g
        d_out[...] = delta
        m_out[...] = m2
        v_out[...] = v2

    blk = pl.BlockSpec((tr, tc), lambda i, j: (i, j))
    out = jax.ShapeDtypeStruct((r, c), F32)
    return pl.pallas_call(
        body, name=name, grid=(r // tr, c // tc),
        in_specs=[pl.BlockSpec((N_DEV, tr, tc), lambda i, j: (0, i, j)), blk, blk, blk],
        out_specs=[blk, blk, blk, blk], out_shape=[out, out, out, out],
        compiler_params=_params(),
    )(parts, w, m, v)


def _adam_small(g, w, m, v):
    def body(g_ref, w_ref, m_ref, v_ref, d_out, m_out, v_out):
        delta, m2, v2 = _adam_math(w_ref[...], g_ref[...], m_ref[...], v_ref[...])
        d_out[...] = delta
        m_out[...] = m2
        v_out[...] = v2

    out = jax.ShapeDtypeStruct(g.shape, F32)
    return pl.pallas_call(body, name="adam_small", out_shape=[out, out, out])(g, w, m, v)


def _rms_fwd(x, gain, name, dep=None, with_transpose=False):
    t, d = x.shape
    tr = _tile(t, 256, LANE)

    def body(x_ref, g_ref, o_ref, *ot_ref):
        xv = x_ref[...]
        r = lax.rsqrt(jnp.mean(xv * xv, axis=-1, keepdims=True) + NORM_EPS)
        y = xv * r * g_ref[...]
        o_ref[...] = y.astype(BF)
        if with_transpose:
            ot_ref[0][...] = jnp.transpose(y).astype(BF)

    out_specs = [pl.BlockSpec((tr, d), lambda i: (i, 0))]
    out_shape = [jax.ShapeDtypeStruct((t, d), BF)]
    if with_transpose:
        out_specs.append(pl.BlockSpec((d, tr), lambda i: (0, i)))
        out_shape.append(jax.ShapeDtypeStruct((d, t), BF))
    return _call(
        body, [x, gain], dep=dep, name=name, grid=(t // tr,),
        in_specs=[pl.BlockSpec((tr, d), lambda i: (i, 0)), pl.BlockSpec((1, d), lambda i: (0, 0))],
        out_specs=out_specs, out_shape=out_shape, compiler_params=_params(),
    )


def _rms_vjp(xv, gain, dy):
    r = lax.rsqrt(jnp.mean(xv * xv, axis=-1, keepdims=True) + NORM_EPS)
    xhat = xv * r
    dxhat = dy * gain
    dx = r * (dxhat - xhat * jnp.mean(dxhat * xhat, axis=-1, keepdims=True))
    dgain = jnp.sum(dy * xhat, axis=0, keepdims=True)
    return dx, dgain


def _loss_head(x, gain, target):
    t, d = x.shape
    tr = _tile(t, 256, 16)

    def body(x_ref, g_ref, t_ref, dx_ref, dxb_ref, dg_ref, loss_ref):
        xv = x_ref[...]
        gain = g_ref[...]
        r = lax.rsqrt(jnp.mean(xv * xv, axis=-1, keepdims=True) + NORM_EPS)
        err = xv * r * gain - t_ref[...]
        dx, dgain = _rms_vjp(xv, gain, err * (1.0 / d))
        dx_ref[...] = dx
        dxb_ref[...] = dx.astype(BF)

        @pl.when(pl.program_id(0) == 0)
        def _():
            dg_ref[...] = jnp.zeros_like(dg_ref)
            loss_ref[...] = jnp.zeros_like(loss_ref)

        dg_ref[...] += dgain
        loss_ref[...] += jnp.sum(err * err, axis=0, keepdims=True) * (0.5 / d)

    row = pl.BlockSpec((tr, d), lambda i: (i, 0))
    vec = pl.BlockSpec((1, d), lambda i: (0, 0))
    return pl.pallas_call(
        body, name="loss_head", grid=(t // tr,),
        in_specs=[row, vec, row], out_specs=[row, row, vec, vec],
        out_shape=[jax.ShapeDtypeStruct((t, d), F32), jax.ShapeDtypeStruct((t, d), BF),
                   jax.ShapeDtypeStruct((1, d), F32), jax.ShapeDtypeStruct((1, d), F32)],
        compiler_params=_params(),
    )(x, gain, target)


def _mm_nn(a, b, out_dtype, name, residual=None, tm_pref=512, tn_pref=1152, b_rows=False):
    m, k = a.shape
    n = b.shape[0] if b_rows else b.shape[1]
    tm, tn = _tile(m, tm_pref, 16), _tile(n, tn_pref, LANE)
    dn = NT if b_rows else NN

    def body(*refs):
        if residual is None:
            a_ref, b_ref, o_ref = refs
            o_ref[...] = _dot(a_ref[...], b_ref[...], dn).astype(out_dtype)
        else:
            a_ref, b_ref, r_ref, o_ref = refs
            o_ref[...] = (r_ref[...] + _dot(a_ref[...], b_ref[...], dn)).astype(out_dtype)

    b_spec = pl.BlockSpec((tn, k), lambda j, i: (j, 0)) if b_rows else pl.BlockSpec((k, tn), lambda j, i: (0, j))
    in_specs = [pl.BlockSpec((tm, k), lambda j, i: (i, 0)), b_spec]
    args = [a, b]
    if residual is not None:
        in_specs.append(pl.BlockSpec((tm, tn), lambda j, i: (i, j)))
        args.append(residual)
    return pl.pallas_call(
        body, name=name, grid=(n // tn, m // tm), in_specs=in_specs,
        out_specs=pl.BlockSpec((tm, tn), lambda j, i: (i, j)),
        out_shape=jax.ShapeDtypeStruct((m, n), out_dtype), compiler_params=_params(),
    )(*args)


def _rms_bwd_tail(dy_ref, first, x_ref, g_ref, dres_ref, dx_ref, dxb_ref, dg_ref):
    @pl.when(first)
    def _():
        dg_ref[...] = jnp.zeros_like(dg_ref)

    gain = g_ref[...]
    for r in range(0, dy_ref.shape[0], LANE):
        rows = pl.ds(r, min(LANE, dy_ref.shape[0] - r))
        dx, dgain = _rms_vjp(x_ref[rows, :], gain, dy_ref[rows, :])
        dx = dx + dres_ref[rows, :]
        dx_ref[rows, :] = dx
        dxb_ref[rows, :] = dx.astype(BF)
        dg_ref[...] += dgain


def _mm_nt(a, b, out_dtype, name, tm_pref=512, tn_pref=1024, tk_pref=2048, rms=None, dep=None, b_cols=False):
    m, k = a.shape
    n = b.shape[1] if b_cols else b.shape[0]
    tm, tn, tk = _tile(m, tm_pref, 16), _tile(n, tn_pref, LANE), _tile(k, tk_pref, LANE)
    nk = k // tk
    assert rms is None or tn == n

    def body(*refs):
        if rms is None:
            a_ref, b_ref, o_ref, acc_ref = refs
        else:
            a_ref, b_ref, x_ref, g_ref, dres_ref, dx_ref, dxb_ref, dg_ref, acc_ref = refs
        kk = pl.program_id(2)

        @pl.when(kk == 0)
        def _():
            acc_ref[...] = jnp.zeros_like(acc_ref)

        acc_ref[...] += _dot(a_ref[...], b_ref[...], NN if b_cols else NT)

        @pl.when(kk == nk - 1)
        def _():
            if rms is None:
                o_ref[...] = acc_ref[...].astype(out_dtype)
            else:
                _rms_bwd_tail(acc_ref, pl.program_id(1) == 0, x_ref, g_ref, dres_ref, dx_ref, dxb_ref, dg_ref)

    if b_cols:
        b_spec = pl.BlockSpec((tk, tn), lambda j, i, kk: (kk, j))
    else:
        b_spec = pl.BlockSpec((tn, tk), lambda j, i, kk: (j, kk))
    in_specs = [pl.BlockSpec((tm, tk), lambda j, i, kk: (i, kk)), b_spec]
    row = pl.BlockSpec((tm, tn), lambda j, i, kk: (i, j))
    if rms is None:
        args, out_specs, out_shape = [a, b], row, jax.ShapeDtypeStruct((m, n), out_dtype)
    else:
        vec = pl.BlockSpec((1, n), lambda j, i, kk: (0, 0))
        args, in_specs = [a, b, *rms], in_specs + [row, vec, row]
        out_specs = [row, row, vec]
        out_shape = [jax.ShapeDtypeStruct((m, n), F32), jax.ShapeDtypeStruct((m, n), BF),
                     jax.ShapeDtypeStruct((1, n), F32)]
    return _call(
        body, args, dep=dep, name=name, grid=(n // tn, m // tm, nk), in_specs=in_specs, out_specs=out_specs,
        out_shape=out_shape, scratch_shapes=[pltpu.VMEM((tm, tn), F32)], compiler_params=_params(),
    )


def _mm_tn(a, b, out_dtype, name, tn_pref=1152, tk_pref=512, a_transposed=False):
    (k, t) = a.shape if a_transposed else a.shape[::-1]
    n = b.shape[1]
    tn, tk = _tile(n, tn_pref, LANE), _tile(t, tk_pref, LANE if a_transposed else 16)
    nt = t // tk

    def body(a_ref, b_ref, o_ref, acc_ref):
        tt = pl.program_id(1)

        @pl.when(tt == 0)
        def _():
            acc_ref[...] = jnp.zeros_like(acc_ref)

        acc_ref[...] += _dot(a_ref[...], b_ref[...], NN if a_transposed else TN)

        @pl.when(tt == nt - 1)
        def _():
            o_ref[...] = acc_ref[...].astype(out_dtype)

    if a_transposed:
        a_spec = pl.BlockSpec((k, tk), lambda j, tt: (0, tt))
    else:
        a_spec = pl.BlockSpec((tk, k), lambda j, tt: (tt, 0))
    return pl.pallas_call(
        body, name=name, grid=(n // tn, nt),
        in_specs=[a_spec, pl.BlockSpec((tk, tn), lambda j, tt: (tt, j))],
        out_specs=pl.BlockSpec((k, tn), lambda j, tt: (0, j)),
        out_shape=jax.ShapeDtypeStruct((k, n), out_dtype),
        scratch_shapes=[pltpu.VMEM((k, tn), F32)], compiler_params=_params(),
    )(a, b)


FFN_COLS = 512


def _ffn_tiles(t, fc):
    return _tile(t, FFN_ROWS, 16), _tile(fc, FFN_COLS, LANE)


def _slabs(tm, rows=256):
    step = rows if tm % rows == 0 else tm
    return [pl.ds(r, step) for r in range(0, tm, step)]


def _ffn_gate_up(hn, wg_t, wu_t, name):
    t, d = hn.shape
    fc = wg_t.shape[0]
    tm, tn = _ffn_tiles(t, fc)

    def body(h_ref, wg_ref, wu_ref, g_ref, u_ref, a_ref):
        for rows in _slabs(tm):
            h = h_ref[rows, :]
            g = _dot(h, wg_ref[...], NT)
            u = _dot(h, wu_ref[...], NT)
            g_ref[rows, :] = g.astype(BF)
            u_ref[rows, :] = u.astype(BF)
            a_ref[rows, :] = (g * _sig(g) * u).astype(BF)

    wspec = pl.BlockSpec((tn, d), lambda j, i: (j, 0))
    hid = pl.BlockSpec((tm, tn), lambda j, i: (i, j))
    out = jax.ShapeDtypeStruct((t, fc), BF)
    return pl.pallas_call(
        body, name=name, grid=(fc // tn, t // tm),
        in_specs=[pl.BlockSpec((tm, d), lambda j, i: (i, 0)), wspec, wspec],
        out_specs=[hid, hid, hid], out_shape=[out, out, out], compiler_params=_params(),
    )(hn, wg_t, wu_t)


def _ffn_gate(hn, wg_t, name):
    t, d = hn.shape
    fc = wg_t.shape[0]
    tm, tn = _ffn_tiles(t, fc)

    def body(h_ref, wg_ref, g_ref):
        g_ref[...] = _dot(h_ref[...], wg_ref[...], NT)

    return pl.pallas_call(
        body, name=name, grid=(fc // tn, t // tm),
        in_specs=[pl.BlockSpec((tm, d), lambda j, i: (i, 0)), pl.BlockSpec((tn, d), lambda j, i: (j, 0))],
        out_specs=pl.BlockSpec((tm, tn), lambda j, i: (i, j)),
        out_shape=jax.ShapeDtypeStruct((t, fc), F32), compiler_params=_params(),
    )(hn, wg_t)


def _ffn_up_act(hn, wu_t, g, name):
    t, d = hn.shape
    fc = wu_t.shape[0]
    tm, tn = _ffn_tiles(t, fc)

    def body(h_ref, wu_ref, g_ref, gb_ref, u_ref, a_ref):
        for rows in _slabs(tm):
            u = _dot(h_ref[rows, :], wu_ref[...], NT)
            gv = g_ref[rows, :]
            gb_ref[rows, :] = gv.astype(BF)
            u_ref[rows, :] = u.astype(BF)
            a_ref[rows, :] = (gv * _sig(gv) * u).astype(BF)

    hid = pl.BlockSpec((tm, tn), lambda j, i: (i, j))
    out = jax.ShapeDtypeStruct((t, fc), BF)
    return pl.pallas_call(
        body, name=name, grid=(fc // tn, t // tm),
        in_specs=[pl.BlockSpec((tm, d), lambda j, i: (i, 0)), pl.BlockSpec((tn, d), lambda j, i: (j, 0)), hid],
        out_specs=[hid, hid, hid], out_shape=[out, out, out], compiler_params=_params(),
    )(hn, wu_t, g)


def _ffn_down(act, wd, xres, name):
    t, fc = act.shape
    d = wd.shape[1]
    tm, tk = _ffn_tiles(t, fc)

    def body(a_ref, w_ref, x_ref, o_ref):
        @pl.when(pl.program_id(1) == 0)
        def _():
            o_ref[...] = x_ref[...]

        o_ref[...] += 0.5 * _dot(a_ref[...], w_ref[...])

    row = pl.BlockSpec((tm, d), lambda i, k: (i, 0))
    return pl.pallas_call(
        body, name=name, grid=(t // tm, fc // tk),
        in_specs=[pl.BlockSpec((tm, tk), lambda i, k: (i, k)), pl.BlockSpec((tk, d), lambda i, k: (k, 0)), row],
        out_specs=row, out_shape=jax.ShapeDtypeStruct((t, d), F32), compiler_params=_params(),
    )(act, wd, xres)


def _ffn_bwd_hidden(dxb, wd, g, u, name):
    t, d = dxb.shape
    fc = wd.shape[0]
    tm, tn = _ffn_tiles(t, fc)

    def body(dx_ref, w_ref, g_ref, u_ref, dg_ref, du_ref):
        for rows in _slabs(tm):
            dh = 0.5 * _dot(dx_ref[rows, :], w_ref[...], NT)
            gv = g_ref[rows, :].astype(F32)
            uv = u_ref[rows, :].astype(F32)
            s = _sig(gv)
            dg_ref[rows, :] = (dh * uv * (s * (1.0 + gv * (1.0 - s)))).astype(BF)
            du_ref[rows, :] = (dh * (gv * s)).astype(BF)

    hid = pl.BlockSpec((tm, tn), lambda i, j: (i, j))
    out = jax.ShapeDtypeStruct((t, fc), BF)
    return pl.pallas_call(
        body, name=name, grid=(t // tm, fc // tn),
        in_specs=[pl.BlockSpec((tm, d), lambda i, j: (i, 0)), pl.BlockSpec((tn, d), lambda i, j: (j, 0)), hid, hid],
        out_specs=[hid, hid], out_shape=[out, out], compiler_params=_params(),
    )(dxb, wd, g, u)


def _ffn_dw(lhs, rhs, scale, name, dep=None, cols_pref=FFN_COLS):
    n = len(lhs)
    t, fc = lhs[0].shape
    d = rhs.shape[1]
    tk, tn = _tile(t, DW_ROWS, 16), _tile(fc, cols_pref, LANE)
    nt = t // tk

    def body(*refs):
        l_refs, r_ref, o_refs, acc_refs = refs[:n], refs[n], refs[n + 1:2 * n + 1], refs[2 * n + 1:]
        tt = pl.program_id(1)
        r = r_ref[...]
        for l_ref, o_ref, acc_ref in zip(l_refs, o_refs, acc_refs):
            @pl.when(tt == 0)
            def _():
                acc_ref[...] = jnp.zeros_like(acc_ref)

            acc_ref[...] += _dot(l_ref[...], r, TN)

            @pl.when(tt == nt - 1)
            def _():
                o_ref[...] = (scale * acc_ref[...]).astype(BF)

    lspec = pl.BlockSpec((tk, tn), lambda j, tt: (tt, j))
    ospec = pl.BlockSpec((tn, d), lambda j, tt: (j, 0))
    out = jax.ShapeDtypeStruct((fc, d), BF)
    return _call(
        body, [*lhs, rhs], dep=dep, name=name, grid=(fc // tn, nt),
        in_specs=[lspec] * n + [pl.BlockSpec((tk, d), lambda j, tt: (tt, 0))],
        out_specs=[ospec] * n, out_shape=[out] * n,
        scratch_shapes=[pltpu.VMEM((tn, d), F32)] * n, compiler_params=_params(),
    )


def _rms_bwd(dy, x, gain, dres, name):
    t, d = x.shape
    tr = _tile(t, 256, 16)

    def body(dy_ref, x_ref, g_ref, dres_ref, dx_ref, dxb_ref, dg_ref):
        _rms_bwd_tail(dy_ref, pl.program_id(0) == 0, x_ref, g_ref, dres_ref, dx_ref, dxb_ref, dg_ref)

    row = pl.BlockSpec((tr, d), lambda i: (i, 0))
    vec = pl.BlockSpec((1, d), lambda i: (0, 0))
    return pl.pallas_call(
        body, name=name, grid=(t // tr,),
        in_specs=[row, row, vec, row], out_specs=[row, row, vec],
        out_shape=[jax.ShapeDtypeStruct((t, d), F32), jax.ShapeDtypeStruct((t, d), BF),
                   jax.ShapeDtypeStruct((1, d), F32)],
        compiler_params=_params(),
    )(dy, x, gain, dres)


def _ffn_bwd_input(dg, du, wg_t, wu_t, name, dep=None):
    t, fc = dg.shape
    d = wg_t.shape[1]
    tm, tk = _ffn_tiles(t, fc)

    def body(dg_ref, du_ref, wg_ref, wu_ref, o_ref):
        @pl.when(pl.program_id(1) == 0)
        def _():
            o_ref[...] = jnp.zeros_like(o_ref)

        o_ref[...] += _dot(dg_ref[...], wg_ref[...]) + _dot(du_ref[...], wu_ref[...])

    hid = pl.BlockSpec((tm, tk), lambda i, k: (i, k))
    wspec = pl.BlockSpec((tk, d), lambda i, k: (k, 0))
    return _call(
        body, [dg, du, wg_t, wu_t], dep=dep, name=name, grid=(t // tm, fc // tk),
        in_specs=[hid, hid, wspec, wspec],
        out_specs=pl.BlockSpec((tm, d), lambda i, k: (i, 0)),
        out_shape=jax.ShapeDtypeStruct((t, d), F32), compiler_params=_params(),
    )


def _rope_tables(t):
    pos = jnp.arange(t, dtype=F32)
    inv_freq = ROPE_THETA ** (-jnp.arange(0, ROPE_DIM, 2, dtype=F32) / ROPE_DIM)
    ang = pos[:, None] * inv_freq[None, :]
    cos, sin = jnp.cos(ang), jnp.sin(ang)
    rest = HEAD_DIM - ROPE_DIM
    one = jnp.ones((t, rest), F32)
    zero_h = jnp.zeros((t, ROPE_HALF), F32)
    zero_r = jnp.zeros((t, rest), F32)
    c = jnp.concatenate([cos, cos, one], axis=1)
    s1 = jnp.concatenate([-sin, zero_h, zero_r], axis=1)
    s2 = jnp.concatenate([zero_h, sin, zero_r], axis=1)
    return c, s1, s2


def _rope(xh, c, s1, s2):
    return xh * c + pltpu.roll(xh, HEAD_DIM - ROPE_HALF, 1) * s1 + pltpu.roll(xh, ROPE_HALF, 1) * s2


def _rope_t(dh, c, s1, s2):
    return dh * c + pltpu.roll(dh * s1, ROPE_HALF, 1) + pltpu.roll(dh * s2, HEAD_DIM - ROPE_HALF, 1)


def _mixer_prep(proj, tables, bf_pad, hd, scale):
    t, np_ = proj.shape
    tr = _tile(t, 256, 16)
    nh = hd // HEAD_DIM
    nblk = hd // LANE
    f_blk = np_ // LANE - 1

    def body(qd_ref, kd_ref, vd_ref, qf_ref, kf_ref, vf_ref, fl_ref, c_ref, s1_ref, s2_ref, b_ref,
             oqd, okd, ovd, oqf, okf, ovf, olog):
        c, s1, s2 = c_ref[...], s1_ref[...], s2_ref[...]
        for h in range(nh):
            sl = slice(h * HEAD_DIM, (h + 1) * HEAD_DIM)
            oqd[:, sl] = (_rope(qd_ref[:, sl], c, s1, s2) * scale).astype(BF)
            okd[:, sl] = _rope(kd_ref[:, sl], c, s1, s2).astype(BF)
        ovd[...] = vd_ref[...].astype(BF)
        oqf[...] = (qf_ref[...] * scale).astype(BF)
        okf[...] = kf_ref[...].astype(BF)
        ovf[...] = vf_ref[...].astype(BF)
        z = fl_ref[...] + b_ref[...]
        olog[...] = jnp.minimum(z, 0.0) - jnp.log(1.0 + jnp.exp(-jnp.abs(z)))

    def col(kblk):
        return pl.BlockSpec((tr, hd), lambda i, kblk=kblk: (i, kblk))

    lane_row = pl.BlockSpec((tr, LANE), lambda i: (i, 0))
    in_specs = [col(0), col(1), col(2), col(3), col(4), col(5),
                pl.BlockSpec((tr, LANE), lambda i: (i, f_blk)),
                lane_row, lane_row, lane_row, pl.BlockSpec((1, LANE), lambda i: (0, 0))]
    o = pl.BlockSpec((tr, hd), lambda i: (i, 0))
    ob = jax.ShapeDtypeStruct((t, hd), BF)
    del nblk
    return pl.pallas_call(
        body, name="mixer_prep", grid=(t // tr,), in_specs=in_specs,
        out_specs=[o, o, o, o, o, o, lane_row],
        out_shape=[ob, ob, ob, ob, ob, ob, jax.ShapeDtypeStruct((t, LANE), F32)],
        compiler_params=_params(),
    )(proj, proj, proj, proj, proj, proj, proj, *tables, bf_pad)


def _split3(x):
    x1 = x.astype(BF)
    r1 = x - x1.astype(F32)
    x2 = r1.astype(BF)
    x3 = (r1 - x2.astype(F32)).astype(BF)
    return x1, x2, x3


def _cumsum_rows(x, reverse, name):
    t, w = x.shape
    blk = LANE
    nb = t // blk

    def body(x_ref, o_ref):
        r = lax.broadcasted_iota(jnp.int32, (blk, blk), 0)
        c = lax.broadcasted_iota(jnp.int32, (blk, blk), 1)
        tri = jnp.where((c >= r) if reverse else (c <= r), 1.0, 0.0).astype(BF)

        def step(i, carry):
            b = (nb - 1 - i) if reverse else i
            off = pl.multiple_of(b * blk, blk)
            xb = x_ref[pl.ds(off, blk), :]
            x1, x2, x3 = _split3(xb)
            o_ref[pl.ds(off, blk), :] = _dot(tri, x1) + _dot(tri, x2) + _dot(tri, x3) + carry
            return carry + jnp.sum(xb, axis=0, keepdims=True)

        lax.fori_loop(0, nb, step, jnp.zeros((1, w), F32))

    return pl.pallas_call(body, name=name, out_shape=jax.ShapeDtypeStruct((t, w), F32),
                          compiler_params=_params())(x)


ATTN_ROWS = 16


def _dil_bias_tiles(tq):
    nbias = MAX_WINDOW // tq + 1
    b = lax.broadcasted_iota(jnp.int32, (nbias, tq, tq), 0)
    i = lax.broadcasted_iota(jnp.int32, (nbias, tq, tq), 1)
    j = lax.broadcasted_iota(jnp.int32, (nbias, tq, tq), 2)
    delta = b * tq + i - j
    mult = jnp.zeros((nbias, tq, tq), F32)
    for w, dil in DIL_PATTERNS:
        mult = mult + jnp.where((delta >= 0) & (delta <= w) & (delta % dil == 0), 1.0, 0.0)
    return jnp.where(mult > 0.0, jnp.log(jnp.maximum(mult, 1.0)), NEG)


def _rep(x, width):
    return jnp.tile(x, (1, width // LANE))


def _chunks(n_rows, fn):
    for c in range(n_rows // ATTN_ROWS):
        fn(c * ATTN_ROWS)


def _causal(r0, tq, transposed):
    a = lax.broadcasted_iota(jnp.int32, (ATTN_ROWS, tq), 0) + r0
    b = lax.broadcasted_iota(jnp.int32, (ATTN_ROWS, tq), 1)
    return (a <= b) if transposed else (b <= a)


def _rows8(x):
    return jnp.transpose(x)[:8, :]


def _attn_fwd(mode, q, k, v, bias, tq, name):
    t, hd = q.shape
    nh = hd // HEAD_DIM
    nb = t // tq
    wb = MAX_WINDOW // tq
    fox = mode == "fox"

    def body(q_ref, k_ref, v_ref, b_ref, o_ref, lse_ref, lse_row_ref, s_ref, p_ref, m_ref, l_ref, acc_ref):
        qi = pl.program_id(1)
        qb = q_ref[...]
        m_ref[...] = jnp.full_like(m_ref, NEG)
        l_ref[...] = jnp.zeros_like(l_ref)
        acc_ref[...] = jnp.zeros_like(acc_ref)

        def tile(kj, diag):
            off = pl.multiple_of(kj * tq, tq)
            s_ref[...] = _dot(qb, k_ref[pl.ds(off, tq), :], NT)
            if fox:
                brow = b_ref[qi][:, :1] - b_ref[kj]

            def chunk(r0):
                rows = pl.ds(r0, ATTN_ROWS)
                if fox:
                    s = s_ref[rows, :] + brow
                    if diag:
                        s = jnp.where(_causal(r0, tq, False), s, NEG)
                else:
                    s = s_ref[rows, :] + b_ref[qi - kj, rows, :]
                m_old = m_ref[rows, :]
                m_new = jnp.maximum(m_old, jnp.max(s, axis=1, keepdims=True))
                p = jnp.exp(s - _rep(m_new, tq))
                alpha = jnp.exp(m_old - m_new)
                l_ref[rows, :] = alpha * l_ref[rows, :] + jnp.sum(p, axis=1, keepdims=True)
                m_ref[rows, :] = m_new
                acc_ref[rows, :] = alpha * acc_ref[rows, :]
                p_ref[rows, :] = p.astype(BF)

            _chunks(tq, chunk)
            acc_ref[...] += _dot(p_ref[...], v_ref[pl.ds(off, tq), :])

        tile(qi, True)
        if fox:
            lax.fori_loop(0, qi, lambda kj, c: (tile(kj, False), c)[1], 0)
        else:
            lax.fori_loop(1, jnp.minimum(qi, wb) + 1, lambda i, c: (tile(qi - i, False), c)[1], 0)
        o_ref[...] = (acc_ref[...] / l_ref[...]).astype(BF)
        lse = m_ref[...] + jnp.log(l_ref[...])
        lse_ref[...] = lse
        lse_row_ref[...] = _rows8(lse)

    qspec = pl.BlockSpec((tq, HEAD_DIM), lambda h, i: (i, h))
    kvspec = pl.BlockSpec((t, HEAD_DIM), lambda h, i: (0, h))
    repspec = pl.BlockSpec((None, tq, LANE), lambda h, i: (h, i, 0))
    row8spec = pl.BlockSpec((None, None, 8, tq), lambda h, i: (h, i, 0, 0))
    if fox:
        bspec = pl.BlockSpec((None, nb, 1, tq), lambda h, i: (h, 0, 0, 0))
    else:
        bspec = pl.BlockSpec((wb + 1, tq, tq), lambda h, i: (0, 0, 0))
    return pl.pallas_call(
        body, name=name, grid=(nh, nb), in_specs=[qspec, kvspec, kvspec, bspec],
        out_specs=[qspec, repspec, row8spec],
        out_shape=[jax.ShapeDtypeStruct((t, hd), BF), jax.ShapeDtypeStruct((nh, t, LANE), F32),
                   jax.ShapeDtypeStruct((nh, nb, 8, tq), F32)],
        scratch_shapes=[pltpu.VMEM((tq, tq), F32), pltpu.VMEM((tq, tq), BF), pltpu.VMEM((tq, LANE), F32),
                        pltpu.VMEM((tq, LANE), F32), pltpu.VMEM((tq, HEAD_DIM), F32)],
        compiler_params=_params(),
    )(q, k, v, bias)


def _attn_bwd_dq(mode, q, k, v, o, do, lse, bias, tq, name, dep=None):
    t, hd = q.shape
    nh = hd // HEAD_DIM
    nb = t // tq
    wb = MAX_WINDOW // tq
    fox = mode == "fox"

    def body(q_ref, k_ref, v_ref, o_ref, do_ref, lse_ref, b_ref, dq_ref, dl_row_ref,
             s_ref, dp_ref, x_ref, y_ref, acc_ref, acc2_ref, dl_ref):
        qi = pl.program_id(1)
        qb = q_ref[...]
        dob = do_ref[...]
        acc_ref[...] = jnp.zeros_like(acc_ref)
        if fox:
            acc2_ref[...] = jnp.zeros_like(acc2_ref)
            dl_ref[...] = jnp.zeros_like(dl_ref)
        else:
            prod = o_ref[...].astype(F32) * dob.astype(F32)
            dl_ref[...] = jnp.broadcast_to(jnp.sum(prod, axis=1, keepdims=True), (tq, LANE))

        def tile(kj, diag):
            off = pl.multiple_of(kj * tq, tq)
            kb = k_ref[pl.ds(off, tq), :]
            s_ref[...] = _dot(qb, kb, NT)
            dp_ref[...] = _dot(dob, v_ref[pl.ds(off, tq), :], NT)
            if fox:
                brow = b_ref[qi][:, :1] - b_ref[kj]

            def chunk(r0):
                rows = pl.ds(r0, ATTN_ROWS)
                lse_c = _rep(lse_ref[rows, :], tq)
                if fox:
                    s = s_ref[rows, :] + brow
                    if diag:
                        s = jnp.where(_causal(r0, tq, False), s, NEG)
                    p = jnp.exp(s - lse_c)
                    pdp = p * dp_ref[rows, :]
                    dl_ref[rows, :] += jnp.sum(pdp, axis=1, keepdims=True)
                    x_ref[rows, :] = pdp.astype(BF)
                    y_ref[rows, :] = p.astype(BF)
                else:
                    p = jnp.exp(s_ref[rows, :] + b_ref[qi - kj, rows, :] - lse_c)
                    x_ref[rows, :] = (p * (dp_ref[rows, :] - _rep(dl_ref[rows, :], tq))).astype(BF)

            _chunks(tq, chunk)
            acc_ref[...] += _dot(x_ref[...], kb)
            if fox:
                acc2_ref[...] += _dot(y_ref[...], kb)

        tile(qi, True)
        if fox:
            lax.fori_loop(0, qi, lambda kj, c: (tile(kj, False), c)[1], 0)
            dq_ref[...] = acc_ref[...] - dl_ref[...] * acc2_ref[...]
        else:
            lax.fori_loop(1, jnp.minimum(qi, wb) + 1, lambda i, c: (tile(qi - i, False), c)[1], 0)
            dq_ref[...] = acc_ref[...]
        dl_row_ref[...] = _rows8(dl_ref[...])

    qspec = pl.BlockSpec((tq, HEAD_DIM), lambda h, i: (i, h))
    kvspec = pl.BlockSpec((t, HEAD_DIM), lambda h, i: (0, h))
    repspec = pl.BlockSpec((None, tq, LANE), lambda h, i: (h, i, 0))
    row8spec = pl.BlockSpec((None, None, 8, tq), lambda h, i: (h, i, 0, 0))
    if fox:
        bspec = pl.BlockSpec((None, nb, 1, tq), lambda h, i: (h, 0, 0, 0))
    else:
        bspec = pl.BlockSpec((wb + 1, tq, tq), lambda h, i: (0, 0, 0))
    return _call(
        body, [q, k, v, o, do, lse, bias], dep=dep, name=name, grid=(nh, nb),
        in_specs=[qspec, kvspec, kvspec, qspec, qspec, repspec, bspec],
        out_specs=[qspec, row8spec],
        out_shape=[jax.ShapeDtypeStruct((t, hd), F32), jax.ShapeDtypeStruct((nh, nb, 8, tq), F32)],
        scratch_shapes=[pltpu.VMEM((tq, tq), F32), pltpu.VMEM((tq, tq), F32), pltpu.VMEM((tq, tq), BF),
                        pltpu.VMEM((tq, tq), BF), pltpu.VMEM((tq, HEAD_DIM), F32),
                        pltpu.VMEM((tq, HEAD_DIM), F32), pltpu.VMEM((tq, LANE), F32)],
        compiler_params=_params(),
    )


def _attn_bwd_dkv(mode, q, k, v, do, lse_row, dl_row, bias_t, c_row, tq, name):
    t, hd = q.shape
    nh = hd // HEAD_DIM
    nb = t // tq
    wb = MAX_WINDOW // tq
    fox = mode == "fox"

    def body(*refs):
        if fox:
            (q_ref, k_ref, v_ref, do_ref, lse_ref, dl_ref, b_ref, cq_ref, dk_ref, dv_ref, dc_row_ref,
             s_ref, dp_ref, x_ref, y_ref, dc_ref) = refs
        else:
            q_ref, k_ref, v_ref, do_ref, lse_ref, dl_ref, b_ref, dk_ref, dv_ref, s_ref, dp_ref, x_ref, y_ref = refs
        kj = pl.program_id(1)
        kb = k_ref[...]
        vb = v_ref[...]
        dk_ref[...] = jnp.zeros_like(dk_ref)
        dv_ref[...] = jnp.zeros_like(dv_ref)
        if fox:
            dc_ref[...] = jnp.zeros_like(dc_ref)

        def tile(qi, diag):
            off = pl.multiple_of(qi * tq, tq)
            qb = q_ref[pl.ds(off, tq), :]
            dob = do_ref[pl.ds(off, tq), :]
            s_ref[...] = _dot(kb, qb, NT)
            dp_ref[...] = _dot(vb, dob, NT)
            lse_r = lse_ref[qi, 0:1, :]
            dl_r = dl_ref[qi, 0:1, :]
            if fox:
                kbias = cq_ref[qi][:, :1] - b_ref[...]

            def chunk(r0):
                rows = pl.ds(r0, ATTN_ROWS)
                if fox:
                    s = s_ref[rows, :] + _rep(kbias[r0:r0 + ATTN_ROWS, :], tq)
                    if diag:
                        s = jnp.where(_causal(r0, tq, True), s, NEG)
                else:
                    s = s_ref[rows, :] + b_ref[qi - kj, rows, :]
                pt = jnp.exp(s - lse_r)
                dst = pt * (dp_ref[rows, :] - dl_r)
                x_ref[rows, :] = pt.astype(BF)
                y_ref[rows, :] = dst.astype(BF)
                if fox:
                    dc_ref[rows, :] -= jnp.sum(dst, axis=1, keepdims=True)

            _chunks(tq, chunk)
            dv_ref[...] += _dot(x_ref[...], dob)
            dk_ref[...] += _dot(y_ref[...], qb)

        tile(kj, True)
        hi = nb if fox else jnp.minimum(kj + wb + 1, nb)
        lax.fori_loop(kj + 1, hi, lambda qi, c: (tile(qi, False), c)[1], 0)
        if fox:
            dc_row_ref[...] = _rows8(dc_ref[...])

    blkspec = pl.BlockSpec((tq, HEAD_DIM), lambda h, j: (j, h))
    fullspec = pl.BlockSpec((t, HEAD_DIM), lambda h, j: (0, h))
    rows8spec = pl.BlockSpec((None, nb, 8, tq), lambda h, j: (h, 0, 0, 0))
    repspec = pl.BlockSpec((None, tq, LANE), lambda h, j: (h, j, 0))
    in_specs = [fullspec, blkspec, blkspec, fullspec, rows8spec, rows8spec]
    args = [q, k, v, do, lse_row, dl_row, bias_t]
    out_specs = [blkspec, blkspec]
    out_shape = [jax.ShapeDtypeStruct((t, hd), F32), jax.ShapeDtypeStruct((t, hd), F32)]
    scratch = [pltpu.VMEM((tq, tq), F32), pltpu.VMEM((tq, tq), F32), pltpu.VMEM((tq, tq), BF),
               pltpu.VMEM((tq, tq), BF)]
    if fox:
        in_specs += [repspec, pl.BlockSpec((None, nb, 1, tq), lambda h, j: (h, 0, 0, 0))]
        args.append(c_row)
        out_specs.append(pl.BlockSpec((None, None, 8, tq), lambda h, j: (h, j, 0, 0)))
        out_shape.append(jax.ShapeDtypeStruct((nh, nb, 8, tq), F32))
        scratch.append(pltpu.VMEM((tq, LANE), F32))
    else:
        in_specs.append(pl.BlockSpec((wb + 1, tq, tq), lambda h, j: (0, 0, 0)))
    return pl.pallas_call(
        body, name=name, grid=(nh, nb), in_specs=in_specs, out_specs=out_specs, out_shape=out_shape,
        scratch_shapes=scratch, compiler_params=_params(),
    )(*args)


def _gate_specs(t, d, hd, tr):
    row = pl.BlockSpec((tr, d), lambda i: (i, 0))
    vec = pl.BlockSpec((1, d), lambda i: (0, 0))
    base = 6 * hd // d
    gd = pl.BlockSpec((tr, d), lambda i: (i, base))
    gf = pl.BlockSpec((tr, d), lambda i: (i, base + 1))
    return row, vec, gd, gf


def _proj_merge(yd, yf, wpd, wpf, proj, b_d, b_f, hd):
    t = yd.shape[0]
    d = wpd.shape[1]
    tr = _tile(t, 256, 16)
    row, vec, gd, gf = _gate_specs(t, d, hd, tr)

    def body(yd_ref, yf_ref, wd_ref, wf_ref, gd_ref, gf_ref, bd_ref, bf_ref, pd_ref, pf_ref, o_ref):
        pd = _dot(yd_ref[...], wd_ref[...])
        pf = _dot(yf_ref[...], wf_ref[...])
        pd_ref[...] = pd
        pf_ref[...] = pf
        o_ref[...] = (_sig(gd_ref[...] + bd_ref[...]) * pd + _sig(gf_ref[...] + bf_ref[...]) * pf).astype(BF)

    yspec = pl.BlockSpec((tr, hd), lambda i: (i, 0))
    wspec = pl.BlockSpec((hd, d), lambda i: (0, 0))
    f32 = jax.ShapeDtypeStruct((t, d), F32)
    return pl.pallas_call(
        body, name="proj_merge", grid=(t // tr,), in_specs=[yspec, yspec, wspec, wspec, gd, gf, vec, vec],
        out_specs=[row, row, row], out_shape=[f32, f32, jax.ShapeDtypeStruct((t, d), BF)],
        compiler_params=_params(),
    )(yd, yf, wpd, wpf, proj, proj, b_d, b_f)


def _merge_bwd(dm, pd, pf, proj, b_d, b_f, hd):
    t, d = pd.shape
    tr = _tile(t, 256, 16)
    row, vec, gd, gf = _gate_specs(t, d, hd, tr)

    def body(dm_ref, pd_ref, pf_ref, gd_ref, gf_ref, bd_ref, bf_ref,
             dpd_ref, dpf_ref, dgd_ref, dgf_ref, dbd_ref, dbf_ref):
        dmv = dm_ref[...]
        sd = _sig(gd_ref[...] + bd_ref[...])
        sf = _sig(gf_ref[...] + bf_ref[...])
        dgd = dmv * pd_ref[...] * (sd * (1.0 - sd))
        dgf = dmv * pf_ref[...] * (sf * (1.0 - sf))
        dpd_ref[...] = (dmv * sd).astype(BF)
        dpf_ref[...] = (dmv * sf).astype(BF)
        dgd_ref[...] = dgd.astype(BF)
        dgf_ref[...] = dgf.astype(BF)

        @pl.when(pl.program_id(0) == 0)
        def _():
            dbd_ref[...] = jnp.zeros_like(dbd_ref)
            dbf_ref[...] = jnp.zeros_like(dbf_ref)

        dbd_ref[...] += jnp.sum(dgd, axis=0, keepdims=True)
        dbf_ref[...] += jnp.sum(dgf, axis=0, keepdims=True)

    ob = jax.ShapeDtypeStruct((t, d), BF)
    ov = jax.ShapeDtypeStruct((1, d), F32)
    return pl.pallas_call(
        body, name="merge_bwd", grid=(t // tr,), in_specs=[row, row, row, gd, gf, vec, vec],
        out_specs=[row, row, row, row, vec, vec], out_shape=[ob, ob, ob, ob, ov, ov],
        compiler_params=_params(),
    )(dm, pd, pf, proj, proj, b_d, b_f)


def _assemble_dproj(dqd, dkd, dvd, dqf, dkf, dvf, dgd, dgf, dlogf, proj, tables, bf_pad, scale):
    t, np_ = proj.shape
    hd = dqd.shape[1]
    d = dgd.shape[1]
    nh = hd // HEAD_DIM
    tr = _tile(t, 256, 16)
    f_blk = np_ // LANE - 1

    def body(dqd_ref, dkd_ref, dvd_ref, dqf_ref, dkf_ref, dvf_ref, dgd_ref, dgf_ref, dlog_ref, fl_ref,
             c_ref, s1_ref, s2_ref, b_ref, o_ref, db_ref):
        c, s1, s2 = c_ref[...], s1_ref[...], s2_ref[...]
        for h in range(nh):
            sl = slice(h * HEAD_DIM, (h + 1) * HEAD_DIM)
            o_ref[:, sl] = (_rope_t(dqd_ref[:, sl], c, s1, s2) * scale).astype(BF)
            o_ref[:, hd + h * HEAD_DIM:hd + (h + 1) * HEAD_DIM] = _rope_t(dkd_ref[:, sl], c, s1, s2).astype(BF)
        o_ref[:, 2 * hd:3 * hd] = dvd_ref[...].astype(BF)
        o_ref[:, 3 * hd:4 * hd] = (dqf_ref[...] * scale).astype(BF)
        o_ref[:, 4 * hd:5 * hd] = dkf_ref[...].astype(BF)
        o_ref[:, 5 * hd:6 * hd] = dvf_ref[...].astype(BF)
        o_ref[:, 6 * hd:6 * hd + d] = dgd_ref[...]
        o_ref[:, 6 * hd + d:6 * hd + 2 * d] = dgf_ref[...]
        z = fl_ref[...] + b_ref[...]
        dfl = dlog_ref[...] * _sig(-z)
        o_ref[:, 6 * hd + 2 * d:] = dfl.astype(BF)

        @pl.when(pl.program_id(0) == 0)
        def _():
            db_ref[...] = jnp.zeros_like(db_ref)

        db_ref[...] += jnp.sum(dfl, axis=0, keepdims=True)

    head = pl.BlockSpec((tr, hd), lambda i: (i, 0))
    row = pl.BlockSpec((tr, d), lambda i: (i, 0))
    lane_row = pl.BlockSpec((tr, LANE), lambda i: (i, 0))
    lane_vec = pl.BlockSpec((1, LANE), lambda i: (0, 0))
    return pl.pallas_call(
        body, name="assemble_dproj", grid=(t // tr,),
        in_specs=[head] * 6 + [row, row, lane_row, pl.BlockSpec((tr, LANE), lambda i: (i, f_blk)),
                               lane_row, lane_row, lane_row, lane_vec],
        out_specs=[pl.BlockSpec((tr, np_), lambda i: (i, 0)), lane_vec],
        out_shape=[jax.ShapeDtypeStruct((t, np_), BF), jax.ShapeDtypeStruct((1, LANE), F32)],
        compiler_params=_params(),
    )(dqd, dkd, dvd, dqf, dkf, dvf, dgd, dgf, dlogf, proj, *tables, bf_pad)


def _to_rows(a, tq):
    h, t = a.shape
    return a.reshape(h, t // tq, 1, tq)


def kernel(x, ffn1_norm, ffn1_w_gate, ffn1_w_up, ffn1_w_down, mix_norm, w_in, b_forget, b_gate_dil, b_gate_fox, w_proj_dil, w_proj_fox, w_out, ffn2_norm, ffn2_w_gate, ffn2_w_up, ffn2_w_down, final_norm, loss_target, m_ffn1_norm, m_ffn1_w_gate, m_ffn1_w_up, m_ffn1_w_down, m_mix_norm, m_w_in, m_b_forget, m_b_gate_dil, m_b_gate_fox, m_w_proj_dil, m_w_proj_fox, m_w_out, m_ffn2_norm, m_ffn2_w_gate, m_ffn2_w_up, m_ffn2_w_down, m_final_norm, v_ffn1_norm, v_ffn1_w_gate, v_ffn1_w_up, v_ffn1_w_down, v_mix_norm, v_w_in, v_b_forget, v_b_gate_dil, v_b_gate_fox, v_w_proj_dil, v_w_proj_fox, v_w_out, v_ffn2_norm, v_ffn2_w_gate, v_ffn2_w_up, v_ffn2_w_down, v_final_norm):
    t, d = x.shape[1], x.shape[2]
    hd = w_proj_dil.shape[1]
    nh = hd // HEAD_DIM
    n_f = b_forget.shape[1]
    cols = w_in.shape[2]
    in_cols = N_DEV * cols
    assert in_cols == 6 * hd + n_f + 2 * d and n_f == nh and n_f <= LANE
    np_ = 6 * hd + 2 * d + LANE
    scale = HEAD_DIM ** -0.5
    tq = _tile(t, 512, LANE)
    assert MAX_WINDOW % tq == 0 and tq % 16 == 0

    x2d = x[0]
    tgt = loss_target[0]

    def rows(w):
        return jnp.swapaxes(w, 1, 2)

    fc = N_DEV * ffn1_w_down.shape[1]
    ag_order = [rows(ffn1_w_gate), rows(ffn1_w_up), ffn1_w_down, w_in, w_proj_dil, w_proj_fox, w_out,
                rows(ffn2_w_gate), rows(ffn2_w_up), ffn2_w_down]
    ag_first, tok = _exchange_start([w[0].astype(BF) for w in ag_order[:2]], True, "ag_start_first", ks=FIRST_LEVEL)
    ag_rest, ag_token = _exchange_start([w[0].astype(BF) for w in ag_order[2:]], True, "ag_start", dep=tok,
                                        ks=FIRST_LEVEL)
    ag = ag_first + ag_rest

    def relay(idx, after, name):
        for i, h in zip(idx, _gather_relay([ag[i] for i in idx], after, name)):
            ag[i] = h

    def gathered(idx, after, name):
        return _gather_wait([ag[i] for i in idx], after, name)

    def ffn_weight(idx, after, name):
        return [w.reshape(fc, d) for w in gathered(idx, after, name)]

    tables = _rope_tables(t)
    bf_pad = jnp.pad(b_forget, ((0, 0), (0, LANE - n_f)))

    hn1, = _rms_fwd(x2d, ffn1_norm, "rms_ffn1", dep=ag_token)
    relay([0], hn1, "ag_relay_ffn1_gate")
    wg1, = ffn_weight([0], hn1, "ag_wait_ffn1_gate")
    g1_f32 = _ffn_gate(hn1, wg1, "ffn1_gate")
    relay([1], g1_f32, "ag_relay_ffn1_up")
    wu1, = ffn_weight([1], g1_f32, "ag_wait_ffn1_up")
    relay([2], wu1, "ag_relay_ffn1_down")
    g1, u1, a1 = _ffn_up_act(hn1, wu1, g1_f32, "ffn1_up_act")
    wd1, = ffn_weight([2], a1, "ag_wait_ffn1_down")
    relay([3], wd1, "ag_relay_w_in")
    x1 = _ffn_down(a1, wd1, x2d, "ffn1_down")

    hm, hm_t = _rms_fwd(x1, mix_norm, "rms_mix", with_transpose=True)
    win_g, = gathered([3], hm, "ag_wait_w_in")
    relay([4, 5, 6], win_g, "ag_relay_mixer")
    segments = [(0, 6 * hd), (6 * hd + n_f, in_cols), (6 * hd, 6 * hd + n_f)]
    pieces = []
    for lo, hi in segments:
        for j in range(lo // cols, (hi - 1) // cols + 1):
            s, e = max(lo, j * cols), min(hi, (j + 1) * cols)
            pieces.append(win_g[j, :, s - j * cols:e - j * cols])
    win_p = jnp.concatenate(pieces + [jnp.zeros((d, LANE - n_f), BF)], axis=1)
    proj = _mm_nn(hm, win_p, F32, "w_in_fwd")
    qd, kd, vd, qf, kf, vf, logf = _mixer_prep(proj, tables, bf_pad, hd, scale)
    csum = _cumsum_rows(logf, False, "cumsum_logf")
    c_heads = csum[:, :nh].T
    c_row = _to_rows(c_heads, tq)
    c_rep = jnp.broadcast_to(c_heads[:, :, None], (nh, t, LANE))
    dil_bias = _dil_bias_tiles(tq)
    dil_bias_t = dil_bias.transpose(0, 2, 1)
    relay([7, 8, 9], qd, "ag_relay_ffn2")
    yd, lse_d, lse_d_row = _attn_fwd("dil", qd, kd, vd, dil_bias, tq, "attn_dil_fwd")
    yf, lse_f, lse_f_row = _attn_fwd("fox", qf, kf, vf, c_row, tq, "attn_fox_fwd")
    wpd_g, wpf_g = gathered([4, 5], yf, "ag_wait_proj")
    wpd = wpd_g.transpose(1, 0, 2).reshape(hd, d)
    wpf = wpf_g.transpose(1, 0, 2).reshape(hd, d)
    pd, pf, merged = _proj_merge(yd, yf, wpd, wpf, proj, b_gate_dil, b_gate_fox, hd)
    wout_g, = gathered([6], merged, "ag_wait_w_out")
    wout = wout_g.reshape(d, d)
    x2 = _mm_nn(merged, wout, F32, "w_out_fwd", residual=x1, tn_pref=1024)

    hn2, = _rms_fwd(x2, ffn2_norm, "rms_ffn2")
    wg2, wu2 = ffn_weight([7, 8], hn2, "ag_wait_ffn2_gate_up")
    g2, u2, a2 = _ffn_gate_up(hn2, wg2, wu2, "ffn2_gate_up")
    wd2, = ffn_weight([9], a2, "ag_wait_ffn2_down")
    x3 = _ffn_down(a2, wd2, x2, "ffn2_down")

    dx3, dx3b, d_final, loss_lanes = _loss_head(x3, final_norm.reshape(1, d), tgt)

    def ffn_bwd(dxb, hn, g, u, a, wg_t, wu_t, wd, x_in, gain, dres, tag):
        def parts(dw):
            return dw.reshape(N_DEV, fc // N_DEV, d)

        dg, du = _ffn_bwd_hidden(dxb, wd, g, u, tag + "_bwd_hidden")
        dwd, = _ffn_dw([a], dxb, 0.5, tag + "_dw_down")
        rs_down, tok = _exchange_start([parts(dwd)], False, "rs_start_" + tag + "_down")
        dwg_t, dwu_t = _ffn_dw([dg, du], hn, 1.0, tag + "_dw_gate_up", dep=tok)
        rs_gu, tok = _exchange_start([parts(dwg_t), parts(dwu_t)], False, "rs_start_" + tag + "_gate_up")
        dhn = _ffn_bwd_input(dg, du, wg_t, wu_t, tag + "_bwd_input", dep=tok)
        dx, dx_bf, dgain = _rms_bwd(dhn, x_in, gain, dres, "rms_" + tag + "_bwd")
        return dx, dx_bf, dgain, rs_gu + rs_down

    dx2, dx2b, d_ffn2_norm, rs_ffn2 = ffn_bwd(dx3b, hn2, g2, u2, a2, wg2, wu2, wd2, x2, ffn2_norm, dx3, "ffn2")

    dmerged = _mm_nt(dx2b, wout, F32, "w_out_bwd")
    dwout = _mm_tn(merged, dx2b, BF, "w_out_dw", tn_pref=1024)
    dpd, dpf, dgd, dgf, d_bd, d_bf = _merge_bwd(dmerged, pd, pf, proj, b_gate_dil, b_gate_fox, hd)
    dyd = _mm_nt(dpd, wpd, BF, "proj_dil_bwd")
    dyf = _mm_nt(dpf, wpf, BF, "proj_fox_bwd")
    dwpd = _mm_tn(yd, dpd, BF, "proj_dil_dw", tn_pref=1024)
    dwpf = _mm_tn(yf, dpf, BF, "proj_fox_dw", tn_pref=1024)
    dwpd_c = dwpd.reshape(hd, N_DEV, d // N_DEV).transpose(1, 0, 2)
    dwpf_c = dwpf.reshape(hd, N_DEV, d // N_DEV).transpose(1, 0, 2)
    dwout_c = dwout.reshape(N_DEV, d // N_DEV, d)
    rs_mix, tok = _exchange_start([dwout_c, dwpd_c, dwpf_c], False, "rs_start_mixer")

    dqd, dl_d = _attn_bwd_dq("dil", qd, kd, vd, yd, dyd, lse_d, dil_bias, tq, "attn_dil_dq", dep=tok)
    dkd, dvd = _attn_bwd_dkv("dil", qd, kd, vd, dyd, lse_d_row, dl_d, dil_bias_t, None, tq, "attn_dil_dkv")
    dqf, dl_f = _attn_bwd_dq("fox", qf, kf, vf, yf, dyf, lse_f, c_row, tq, "attn_fox_dq")
    dkf, dvf, dc = _attn_bwd_dkv("fox", qf, kf, vf, dyf, lse_f_row, dl_f, c_rep, c_row, tq, "attn_fox_dkv")
    dc_pad = jnp.pad(dc[:, :, 0, :].reshape(nh, t).T, ((0, 0), (0, LANE - nh)))
    dlogf = _cumsum_rows(dc_pad, True, "revcumsum_dc")
    dproj, d_bforget = _assemble_dproj(dqd, dkd, dvd, dqf, dkf, dvf, dgd, dgf, dlogf, proj, tables, bf_pad, scale)

    dwin_p = _mm_tn(hm_t, dproj, BF, "w_in_dw", tk_pref=DW_ROWS, a_transposed=True)

    def perm_col(c):
        if c < 6 * hd:
            return c
        return c + 2 * d if c < 6 * hd + n_f else c - n_f

    shards = []
    for j in range(N_DEV):
        cuts = sorted({j * cols, (j + 1) * cols} | {c for c in (6 * hd, 6 * hd + n_f) if j * cols < c < (j + 1) * cols})
        shards.append(jnp.concatenate([dwin_p[:, perm_col(lo):perm_col(lo) + hi - lo]
                                       for lo, hi in zip(cuts[:-1], cuts[1:])], axis=1))
    dwin_c = jnp.stack(shards)
    rs_win, tok = _exchange_start([dwin_c], False, "rs_start_w_in")
    dx1, dx1b, d_mix_norm = _mm_nt(dproj, win_p, F32, "w_in_bwd", tn_pref=d, tk_pref=1152,
                                   rms=(x1, mix_norm, dx2), dep=tok)

    grad_x, _, d_ffn1_norm, rs_ffn1 = ffn_bwd(dx1b, hn1, g1, u1, a1, wg1, wu1, wd1, x2d, ffn1_norm, dx1, "ffn1")

    def update(handles, names, after, tag):
        recvs = _exchange_wait(handles, False, after, "rs_wait_" + tag)
        res = {}
        for recv, n in zip(recvs, names):
            turn = rows if n.endswith(("w_gate", "w_up")) else (lambda a: a)
            w, m, v = (turn(a)[0] for a in wmv[n])
            res[n] = tuple(turn(o[None]) for o in _adam_from_partials(recv, w, m, v, "adam_" + n))
        return res, res[names[-1]][0]

    wmv = {
        "ffn1_w_gate": (ffn1_w_gate, m_ffn1_w_gate, v_ffn1_w_gate),
        "ffn1_w_up": (ffn1_w_up, m_ffn1_w_up, v_ffn1_w_up),
        "ffn1_w_down": (ffn1_w_down, m_ffn1_w_down, v_ffn1_w_down),
        "w_in": (w_in, m_w_in, v_w_in),
        "w_proj_dil": (w_proj_dil, m_w_proj_dil, v_w_proj_dil),
        "w_proj_fox": (w_proj_fox, m_w_proj_fox, v_w_proj_fox),
        "w_out": (w_out, m_w_out, v_w_out),
        "ffn2_w_gate": (ffn2_w_gate, m_ffn2_w_gate, v_ffn2_w_gate),
        "ffn2_w_up": (ffn2_w_up, m_ffn2_w_up, v_ffn2_w_up),
        "ffn2_w_down": (ffn2_w_down, m_ffn2_w_down, v_ffn2_w_down),
    }
    big = {}
    after = grad_x
    for handles, names, tag in [
            (rs_ffn2, ["ffn2_w_gate", "ffn2_w_up", "ffn2_w_down"], "ffn2"),
            (rs_mix, ["w_out", "w_proj_dil", "w_proj_fox"], "mixer"),
            (rs_win, ["w_in"], "w_in"),
            (rs_ffn1, ["ffn1_w_gate", "ffn1_w_up", "ffn1_w_down"], "ffn1")]:
        res, after = update(handles, names, after, tag)
        big.update(res)

    def lanes(a):
        a = a.reshape(1, -1)
        return jnp.pad(a, ((0, 0), (0, d - a.shape[1])))

    small_names = ["ffn1_norm", "mix_norm", "b_gate_dil", "b_gate_fox", "ffn2_norm", "final_norm", "b_forget"]
    small_g = [d_ffn1_norm, d_mix_norm, d_bd, d_bf, d_ffn2_norm, d_final, d_bforget[:, :n_f]]
    small_w = [ffn1_norm, mix_norm, b_gate_dil, b_gate_fox, ffn2_norm, final_norm, b_forget]
    small_m = [m_ffn1_norm, m_mix_norm, m_b_gate_dil, m_b_gate_fox, m_ffn2_norm, m_final_norm, m_b_forget]
    small_v = [v_ffn1_norm, v_mix_norm, v_b_gate_dil, v_b_gate_fox, v_ffn2_norm, v_final_norm, v_b_forget]
    pack = lambda arrs, last: jnp.concatenate([lanes(a) for a in arrs] + [last], axis=0)
    g_all = _allreduce_small(pack(small_g, loss_lanes))
    zero_row = jnp.zeros((1, d), F32)
    one_row = jnp.ones((1, d), F32)
    s_delta, s_m, s_v = _adam_small(g_all, pack(small_w, zero_row), pack(small_m, zero_row), pack(small_v, one_row))
    loss = g_all[len(small_names), 0]

    def unpack(packed, i, like):
        return packed[i, :like.size].reshape(like.shape)

    small = {}
    for i, (n, w) in enumerate(zip(small_names, small_w)):
        small[n] = (unpack(g_all, i, w), unpack(s_delta, i, w), unpack(s_m, i, w), unpack(s_v, i, w))

    order = ["ffn1_norm", "ffn1_w_gate", "ffn1_w_up", "ffn1_w_down", "mix_norm", "w_in", "b_forget", "b_gate_dil",
             "b_gate_fox", "w_proj_dil", "w_proj_fox", "w_out", "ffn2_norm", "ffn2_w_gate", "ffn2_w_up",
             "ffn2_w_down", "final_norm"]
    res = {**big, **small}
    outs = [loss, grad_x[None]]
    for slot in range(4):
        outs += [res[n][slot] for n in order]
    return tuple(outs)
```

```python
import jax
import jax.numpy as jnp
from jax import lax
from jax.experimental import pallas as pl
from jax.experimental.pallas import tpu as pltpu

BF = jnp.bfloat16
F32 = jnp.float32
MESH = pl.DeviceIdType.MESH
N_DEV = 8

HEAD_DIM = 128
ROPE_DIM = HEAD_DIM // 4
ROPE_HALF = ROPE_DIM // 2
ROPE_THETA = 500000.0
NORM_EPS = 1e-6
DIL_PATTERNS = ((128, 1), (512, 4), (2048, 16))
MAX_WINDOW = 2048
LANE = 128
NEG = -1e30

ADAM_LR = 0.001
ADAM_B1 = 0.9
ADAM_B2 = 0.999
ADAM_EPS = 1e-08
ADAM_WD = 0.01
ADAM_STEP = 10

VMEM_LIMIT_BYTES = 56 * 1024 * 1024
FFN_ROWS = 1024
DW_ROWS = 1024
ANY = pl.BlockSpec(memory_space=pl.ANY)

NN = (((1,), (0,)), ((), ()))
NT = (((1,), (1,)), ((), ()))
TN = (((0,), (0,)), ((), ()))


def _dot(a, b, dn=NN):
    return lax.dot_general(a, b, dn, preferred_element_type=F32)


def _sig(x):
    return 0.5 + 0.5 * jnp.tanh(0.5 * x)


def _tile(n, pref, align):
    best = None
    t = align
    while t <= min(n, pref):
        if n % t == 0:
            best = t
        t += align
    return n if best is None else best


def _params():
    return pltpu.CompilerParams(vmem_limit_bytes=VMEM_LIMIT_BYTES)


def _call(body, args, dep=None, **kw):
    if dep is not None:
        n_in = len(args)
        inner = body

        def body(*refs):
            inner(*refs[:n_in], *refs[n_in + 1:])

        kw["in_specs"] = list(kw["in_specs"]) + [ANY]
        args = list(args) + [dep]
    return pl.pallas_call(body, **kw)(*args)


def _peers():
    x, y, c = lax.axis_index("x"), lax.axis_index("y"), lax.axis_index("c")
    me = 4 * x + 2 * y + c
    peers = []
    for k in range(1, N_DEV):
        px = 1 - x if (k >> 2) & 1 else x
        py = 1 - y if (k >> 1) & 1 else y
        pc = 1 - c if k & 1 else c
        peers.append((k, (px, py, pc), 4 * px + 2 * py + pc))
    return me, peers


HBM = pl.BlockSpec(memory_space=pltpu.HBM)
SEM = pl.BlockSpec(memory_space=pltpu.SEMAPHORE)
EFFECT = pltpu.SideEffectType.DATAFLOW_SIDE_EFFECTING


def _exchange_copy(gather, src_ref, land_ref, send_sems, recv_sems, me, k, peer, peer_flat, landing):
    return pltpu.make_async_remote_copy(
        src_ref=src_ref if gather else src_ref.at[peer_flat], dst_ref=land_ref.at[landing],
        send_sem=send_sems.at[k], recv_sem=recv_sems.at[k], device_id=peer, device_id_type=MESH)


ALL_PEERS = (1, 2, 3, 4, 5, 6, 7)
SIBLING = 1
SAME_CORE = (2, 4, 6)
FIRST_LEVEL = (SIBLING,) + SAME_CORE


def _exchange_start(srcs, gather, name, dep=None, ks=ALL_PEERS):
    n = len(srcs)
    extra = [] if dep is None else [dep]

    def body(*refs):
        src_refs, land_refs = refs[:n], refs[n:2 * n]
        refs = refs[2 * n + len(extra):]
        send_refs, recv_refs = refs[:n], refs[n:2 * n]
        token = refs[4 * n]
        me, peers = _peers()
        for i in range(n):
            for k, peer, peer_flat in peers:
                if k in ks:
                    _exchange_copy(gather, src_refs[i], land_refs[i], send_refs[i], recv_refs[i],
                                   me, k, peer, peer_flat, me).start()
        token[...] = jnp.zeros_like(token)

    lands = [lax.empty((N_DEV,) + s.shape[-2:], s.dtype) for s in srcs]
    sems = [pltpu.SemaphoreType.DMA((N_DEV,)) for _ in range(2 * n)]
    out = pl.pallas_call(
        body, name=name,
        out_shape=tuple(sems) + tuple(pltpu.HBM(a.shape, a.dtype) for a in list(srcs) + lands)
        + (jax.ShapeDtypeStruct((8, LANE), F32),),
        in_specs=[HBM] * (2 * n) + [ANY] * len(extra),
        out_specs=tuple([SEM] * (2 * n) + [HBM] * (2 * n) + [pl.BlockSpec(memory_space=pltpu.VMEM)]),
        input_output_aliases={i: 2 * n + i for i in range(2 * n)},
        compiler_params=pltpu.CompilerParams(has_side_effects=EFFECT),
    )(*[pltpu.with_memory_space_constraint(a, pltpu.HBM) for a in list(srcs) + lands], *extra)
    handles = [(out[2 * n + i], out[3 * n + i], out[i], out[n + i]) for i in range(n)]
    return handles, out[4 * n]


def _exchange_wait(handles, gather, after, name):
    n = len(handles)

    def body(*refs):
        src_refs, land_refs = refs[:n], refs[n:2 * n]
        send_refs, recv_refs = refs[2 * n:3 * n], refs[3 * n:4 * n]
        me, peers = _peers()
        for i in range(n):
            for k, peer, peer_flat in peers:
                cp = _exchange_copy(gather, src_refs[i], land_refs[i], send_refs[i], recv_refs[i],
                                    me, k, peer, peer_flat, peer_flat)
                cp.wait_send()
                cp.wait_recv()

    srcs = [h[0] for h in handles]
    lands = [h[1] for h in handles]
    out = pl.pallas_call(
        body, name=name,
        out_shape=tuple(pltpu.HBM(a.shape, a.dtype) for a in srcs + lands),
        in_specs=[HBM] * (2 * n) + [SEM] * (2 * n) + [ANY],
        out_specs=tuple([HBM] * (2 * n)),
        input_output_aliases={i: i for i in range(2 * n)},
        compiler_params=pltpu.CompilerParams(has_side_effects=EFFECT),
    )(*srcs, *lands, *[h[2] for h in handles], *[h[3] for h in handles], after)
    me = 4 * lax.axis_index("x") + 2 * lax.axis_index("y") + lax.axis_index("c")
    filled = []
    for src, land in zip(out[:n], out[n:]):
        own = src[None] if gather else lax.dynamic_slice_in_dim(src, me, 1, axis=0)
        filled.append(lax.dynamic_update_slice_in_dim(land, own, me, axis=0))
    return filled


def _gather_relay(handles, after, name):
    n = len(handles)

    def body(*refs):
        land_refs, recv_refs = refs[:n], refs[n:2 * n]
        refs = refs[2 * n + 1:]
        send2_refs, recv2_refs = refs[n:2 * n], refs[2 * n:3 * n]
        me, peers = _peers()
        sibling = peers[SIBLING - 1][1]
        for i in range(n):
            for k, peer, peer_flat in peers:
                if k in SAME_CORE:
                    block = land_refs[i].at[peer_flat]
                    pltpu.make_async_remote_copy(
                        src_ref=block, dst_ref=block, send_sem=send2_refs[i].at[k], recv_sem=recv_refs[i].at[k],
                        device_id=peer, device_id_type=MESH).wait_recv()
                    pltpu.make_async_remote_copy(
                        src_ref=block, dst_ref=block, send_sem=send2_refs[i].at[k], recv_sem=recv2_refs[i].at[k],
                        device_id=sibling, device_id_type=MESH).start()

    lands = [h[1] for h in handles]
    sems = [pltpu.SemaphoreType.DMA((N_DEV,)) for _ in range(2 * n)]
    out = pl.pallas_call(
        body, name=name,
        out_shape=tuple(pltpu.HBM(a.shape, a.dtype) for a in lands) + tuple(sems),
        in_specs=[HBM] * n + [SEM] * n + [ANY],
        out_specs=tuple([HBM] * n + [SEM] * (2 * n)),
        input_output_aliases={i: i for i in range(n)},
        compiler_params=pltpu.CompilerParams(has_side_effects=EFFECT),
    )(*lands, *[h[3] for h in handles], after)
    return [(h[0], out[i], h[2], h[3], out[n + i], out[2 * n + i]) for i, h in enumerate(handles)]


def _gather_wait(handles, after, name):
    n = len(handles)

    def body(*refs):
        src_refs, land_refs = refs[:n], refs[n:2 * n]
        send_refs, recv_refs = refs[2 * n:3 * n], refs[3 * n:4 * n]
        send2_refs, recv2_refs = refs[4 * n:5 * n], refs[5 * n:6 * n]
        me, peers = _peers()
        _, sibling, sibling_flat = peers[SIBLING - 1]
        for i in range(n):
            for k, peer, peer_flat in peers:
                if k in FIRST_LEVEL:
                    cp = _exchange_copy(True, src_refs[i], land_refs[i], send_refs[i], recv_refs[i],
                                        me, k, peer, peer_flat, peer_flat)
                    cp.wait_send()
                    if k == SIBLING:
                        cp.wait_recv()
                if k in SAME_CORE:
                    mine = land_refs[i].at[peer_flat]
                    theirs = land_refs[i].at[peer_flat ^ SIBLING]
                    cp = pltpu.make_async_remote_copy(
                        src_ref=mine, dst_ref=theirs, send_sem=send2_refs[i].at[k], recv_sem=recv2_refs[i].at[k],
                        device_id=sibling, device_id_type=MESH)
                    cp.wait_send()
                    cp.wait_recv()

    srcs = [h[0] for h in handles]
    lands = [h[1] for h in handles]
    out = pl.pallas_call(
        body, name=name,
        out_shape=tuple(pltpu.HBM(a.shape, a.dtype) for a in srcs + lands),
        in_specs=[HBM] * (2 * n) + [SEM] * (4 * n) + [ANY],
        out_specs=tuple([HBM] * (2 * n)),
        input_output_aliases={i: i for i in range(2 * n)},
        compiler_params=pltpu.CompilerParams(has_side_effects=EFFECT),
    )(*srcs, *lands, *[h[2] for h in handles], *[h[3] for h in handles],
      *[h[4] for h in handles], *[h[5] for h in handles], after)
    me = 4 * lax.axis_index("x") + 2 * lax.axis_index("y") + lax.axis_index("c")
    return [lax.dynamic_update_slice_in_dim(land, src[None], me, axis=0) for src, land in zip(out[:n], out[n:])]


def _allreduce_small(p):
    rows, d = p.shape

    def body(p_ref, o_ref, recv_ref, send_sems, recv_sems):
        me, peers = _peers()
        recv_ref[me] = p_ref[...]
        sends = []
        for k, peer, peer_flat in peers:
            cp = pltpu.make_async_remote_copy(
                src_ref=p_ref, dst_ref=recv_ref.at[me],
                send_sem=send_sems.at[k], recv_sem=recv_sems.at[k],
                device_id=peer, device_id_type=MESH)
            cp.start()
            sends.append(cp)
        for k, peer, peer_flat in peers:
            pltpu.make_async_remote_copy(
                src_ref=p_ref, dst_ref=recv_ref.at[peer_flat],
                send_sem=send_sems.at[k], recv_sem=recv_sems.at[k],
                device_id=peer, device_id_type=MESH).wait_recv()
        for cp in sends:
            cp.wait_send()
        acc = recv_ref[0]
        for s in range(1, N_DEV):
            acc = acc + recv_ref[s]
        is_loss = lax.broadcasted_iota(jnp.int32, (rows, d), 0) == rows - 1
        total = jnp.sum(jnp.where(is_loss, acc, 0.0))
        o_ref[...] = jnp.where(is_loss, total, acc)

    return pl.pallas_call(
        body, name="allreduce_small",
        out_shape=jax.ShapeDtypeStruct((rows, d), F32),
        in_specs=[pl.BlockSpec(memory_space=pltpu.VMEM)],
        out_specs=pl.BlockSpec(memory_space=pltpu.VMEM),
        scratch_shapes=[pltpu.VMEM((N_DEV, rows, d), F32),
                        pltpu.SemaphoreType.DMA((N_DEV,)), pltpu.SemaphoreType.DMA((N_DEV,))],
    )(p)


def _adam_math(w, g, m, v):
    m2 = ADAM_B1 * m + (1.0 - ADAM_B1) * g
    v2 = ADAM_B2 * v + (1.0 - ADAM_B2) * (g * g)
    m_hat = m2 / (1.0 - ADAM_B1 ** ADAM_STEP)
    v_hat = v2 / (1.0 - ADAM_B2 ** ADAM_STEP)
    delta = -ADAM_LR * (m_hat / (jnp.sqrt(v_hat) + ADAM_EPS) + ADAM_WD * w)
    return delta, m2, v2


def _adam_from_partials(parts, w, m, v, name):
    r, c = w.shape
    tr = _tile(r, 256, 16)

    def body(p_ref, w_ref, m_ref, v_ref, g_out, d_out, m_out, v_out):
        g = p_ref[0].astype(F32)
        for s in range(1, N_DEV):
            g = g + p_ref[s].astype(F32)
        delta, m2, v2 = _adam_math(w_ref[...], g, m_ref[...], v_ref[...])
        g_out[...] = g
        d_out[...] = delta
        m_out[...] = m2
        v_out[...] = v2

    blk = pl.BlockSpec((tr, c), lambda i: (i, 0))
    out = jax.ShapeDtypeStruct((r, c), F32)
    return pl.pallas_call(
        body, name=name, grid=(r // tr,),
        in_specs=[pl.BlockSpec((N_DEV, tr, c), lambda i: (0, i, 0)), blk, blk, blk],
        out_specs=[blk, blk, blk, blk], out_shape=[out, out, out, out],
        compiler_params=_params(),
    )(parts, w, m, v)


def _adam_small(g, w, m, v):
    def body(g_ref, w_ref, m_ref, v_ref, d_out, m_out, v_out):
        delta, m2, v2 = _adam_math(w_ref[...], g_ref[...], m_ref[...], v_ref[...])
        d_out[...] = delta
        m_out[...] = m2
        v_out[...] = v2

    out = jax.ShapeDtypeStruct(g.shape, F32)
    return pl.pallas_call(body, name="adam_small", out_shape=[out, out, out])(g, w, m, v)


def _rms_fwd(x, gain, name, dep=None, with_transpose=False):
    t, d = x.shape
    tr = _tile(t, 256, LANE)

    def body(x_ref, g_ref, o_ref, *ot_ref):
        xv = x_ref[...]
        r = lax.rsqrt(jnp.mean(xv * xv, axis=-1, keepdims=True) + NORM_EPS)
        y = xv * r * g_ref[...]
        o_ref[...] = y.astype(BF)
        if with_transpose:
            ot_ref[0][...] = jnp.transpose(y).astype(BF)

    out_specs = [pl.BlockSpec((tr, d), lambda i: (i, 0))]
    out_shape = [jax.ShapeDtypeStruct((t, d), BF)]
    if with_transpose:
        out_specs.append(pl.BlockSpec((d, tr), lambda i: (0, i)))
        out_shape.append(jax.ShapeDtypeStruct((d, t), BF))
    return _call(
        body, [x, gain], dep=dep, name=name, grid=(t // tr,),
        in_specs=[pl.BlockSpec((tr, d), lambda i: (i, 0)), pl.BlockSpec((1, d), lambda i: (0, 0))],
        out_specs=out_specs, out_shape=out_shape, compiler_params=_params(),
    )


def _rms_vjp(xv, gain, dy):
    r = lax.rsqrt(jnp.mean(xv * xv, axis=-1, keepdims=True) + NORM_EPS)
    xhat = xv * r
    dxhat = dy * gain
    dx = r * (dxhat - xhat * jnp.mean(dxhat * xhat, axis=-1, keepdims=True))
    dgain = jnp.sum(dy * xhat, axis=0, keepdims=True)
    return dx, dgain


def _loss_head(x, gain, target):
    t, d = x.shape
    tr = _tile(t, 256, 16)

    def body(x_ref, g_ref, t_ref, dx_ref, dxb_ref, dg_ref, loss_ref):
        xv = x_ref[...]
        gain = g_ref[...]
        r = lax.rsqrt(jnp.mean(xv * xv, axis=-1, keepdims=True) + NORM_EPS)
        err = xv * r * gain - t_ref[...]
        dx, dgain = _rms_vjp(xv, gain, err * (1.0 / d))
        dx_ref[...] = dx
        dxb_ref[...] = dx.astype(BF)

        @pl.when(pl.program_id(0) == 0)
        def _():
            dg_ref[...] = jnp.zeros_like(dg_ref)
            loss_ref[...] = jnp.zeros_like(loss_ref)

        dg_ref[...] += dgain
        loss_ref[...] += jnp.sum(err * err, axis=0, keepdims=True) * (0.5 / d)

    row = pl.BlockSpec((tr, d), lambda i: (i, 0))
    vec = pl.BlockSpec((1, d), lambda i: (0, 0))
    return pl.pallas_call(
        body, name="loss_head", grid=(t // tr,),
        in_specs=[row, vec, row], out_specs=[row, row, vec, vec],
        out_shape=[jax.ShapeDtypeStruct((t, d), F32), jax.ShapeDtypeStruct((t, d), BF),
                   jax.ShapeDtypeStruct((1, d), F32), jax.ShapeDtypeStruct((1, d), F32)],
        compiler_params=_params(),
    )(x, gain, target)


def _mm_nn(a, b, out_dtype, name, residual=None, tm_pref=512, tn_pref=1152):
    m, k = a.shape
    n = b.shape[1]
    tm, tn = _tile(m, tm_pref, 16), _tile(n, tn_pref, LANE)

    def body(*refs):
        if residual is None:
            a_ref, b_ref, o_ref = refs
            o_ref[...] = _dot(a_ref[...], b_ref[...]).astype(out_dtype)
        else:
            a_ref, b_ref, r_ref, o_ref = refs
            o_ref[...] = (r_ref[...] + _dot(a_ref[...], b_ref[...])).astype(out_dtype)

    in_specs = [pl.BlockSpec((tm, k), lambda j, i: (i, 0)), pl.BlockSpec((k, tn), lambda j, i: (0, j))]
    args = [a, b]
    if residual is not None:
        in_specs.append(pl.BlockSpec((tm, tn), lambda j, i: (i, j)))
        args.append(residual)
    return pl.pallas_call(
        body, name=name, grid=(n // tn, m // tm), in_specs=in_specs,
        out_specs=pl.BlockSpec((tm, tn), lambda j, i: (i, j)),
        out_shape=jax.ShapeDtypeStruct((m, n), out_dtype), compiler_params=_params(),
    )(*args)


def _rms_bwd_tail(dy_ref, first, x_ref, g_ref, dres_ref, dx_ref, dxb_ref, dg_ref):
    @pl.when(first)
    def _():
        dg_ref[...] = jnp.zeros_like(dg_ref)

    gain = g_ref[...]
    for r in range(0, dy_ref.shape[0], LANE):
        rows = pl.ds(r, min(LANE, dy_ref.shape[0] - r))
        dx, dgain = _rms_vjp(x_ref[rows, :], gain, dy_ref[rows, :])
        dx = dx + dres_ref[rows, :]
        dx_ref[rows, :] = dx
        dxb_ref[rows, :] = dx.astype(BF)
        dg_ref[...] += dgain


def _mm_nt(a, b, out_dtype, name, tm_pref=512, tn_pref=1024, tk_pref=2048, rms=None, dep=None):
    m, k = a.shape
    n = b.shape[0]
    tm, tn, tk = _tile(m, tm_pref, 16), _tile(n, tn_pref, LANE), _tile(k, tk_pref, LANE)
    nk = k // tk
    assert rms is None or tn == n

    def body(*refs):
        if rms is None:
            a_ref, b_ref, o_ref, acc_ref = refs
        else:
            a_ref, b_ref, x_ref, g_ref, dres_ref, dx_ref, dxb_ref, dg_ref, acc_ref = refs
        kk = pl.program_id(2)

        @pl.when(kk == 0)
        def _():
            acc_ref[...] = jnp.zeros_like(acc_ref)

        acc_ref[...] += _dot(a_ref[...], b_ref[...], NT)

        @pl.when(kk == nk - 1)
        def _():
            if rms is None:
                o_ref[...] = acc_ref[...].astype(out_dtype)
            else:
                _rms_bwd_tail(acc_ref, pl.program_id(1) == 0, x_ref, g_ref, dres_ref, dx_ref, dxb_ref, dg_ref)

    in_specs = [pl.BlockSpec((tm, tk), lambda j, i, kk: (i, kk)), pl.BlockSpec((tn, tk), lambda j, i, kk: (j, kk))]
    row = pl.BlockSpec((tm, tn), lambda j, i, kk: (i, j))
    if rms is None:
        args, out_specs, out_shape = [a, b], row, jax.ShapeDtypeStruct((m, n), out_dtype)
    else:
        vec = pl.BlockSpec((1, n), lambda j, i, kk: (0, 0))
        args, in_specs = [a, b, *rms], in_specs + [row, vec, row]
        out_specs = [row, row, vec]
        out_shape = [jax.ShapeDtypeStruct((m, n), F32), jax.ShapeDtypeStruct((m, n), BF),
                     jax.ShapeDtypeStruct((1, n), F32)]
    return _call(
        body, args, dep=dep, name=name, grid=(n // tn, m // tm, nk), in_specs=in_specs, out_specs=out_specs,
        out_shape=out_shape, scratch_shapes=[pltpu.VMEM((tm, tn), F32)], compiler_params=_params(),
    )


def _mm_tn(a, b, out_dtype, name, tn_pref=1152, tk_pref=512, a_transposed=False):
    (k, t) = a.shape if a_transposed else a.shape[::-1]
    n = b.shape[1]
    tn, tk = _tile(n, tn_pref, LANE), _tile(t, tk_pref, LANE if a_transposed else 16)
    nt = t // tk

    def body(a_ref, b_ref, o_ref, acc_ref):
        tt = pl.program_id(1)

        @pl.when(tt == 0)
        def _():
            acc_ref[...] = jnp.zeros_like(acc_ref)

        acc_ref[...] += _dot(a_ref[...], b_ref[...], NN if a_transposed else TN)

        @pl.when(tt == nt - 1)
        def _():
            o_ref[...] = acc_ref[...].astype(out_dtype)

    if a_transposed:
        a_spec = pl.BlockSpec((k, tk), lambda j, tt: (0, tt))
    else:
        a_spec = pl.BlockSpec((tk, k), lambda j, tt: (tt, 0))
    return pl.pallas_call(
        body, name=name, grid=(n // tn, nt),
        in_specs=[a_spec, pl.BlockSpec((tk, tn), lambda j, tt: (tt, j))],
        out_specs=pl.BlockSpec((k, tn), lambda j, tt: (0, j)),
        out_shape=jax.ShapeDtypeStruct((k, n), out_dtype),
        scratch_shapes=[pltpu.VMEM((k, tn), F32)], compiler_params=_params(),
    )(a, b)


FFN_COLS = 512


def _ffn_tiles(t, fc):
    return _tile(t, FFN_ROWS, 16), _tile(fc, FFN_COLS, LANE)


def _slabs(tm, rows=256):
    step = rows if tm % rows == 0 else tm
    return [pl.ds(r, step) for r in range(0, tm, step)]


def _ffn_gate_up(hn, wg_t, wu_t, name):
    t, d = hn.shape
    fc = wg_t.shape[0]
    tm, tn = _ffn_tiles(t, fc)

    def body(h_ref, wg_ref, wu_ref, g_ref, u_ref, a_ref):
        for rows in _slabs(tm):
            h = h_ref[rows, :]
            g = _dot(h, wg_ref[...], NT)
            u = _dot(h, wu_ref[...], NT)
            g_ref[rows, :] = g.astype(BF)
            u_ref[rows, :] = u.astype(BF)
            a_ref[rows, :] = (g * _sig(g) * u).astype(BF)

    wspec = pl.BlockSpec((tn, d), lambda j, i: (j, 0))
    hid = pl.BlockSpec((tm, tn), lambda j, i: (i, j))
    out = jax.ShapeDtypeStruct((t, fc), BF)
    return pl.pallas_call(
        body, name=name, grid=(fc // tn, t // tm),
        in_specs=[pl.BlockSpec((tm, d), lambda j, i: (i, 0)), wspec, wspec],
        out_specs=[hid, hid, hid], out_shape=[out, out, out], compiler_params=_params(),
    )(hn, wg_t, wu_t)


def _ffn_gate(hn, wg_t, name):
    t, d = hn.shape
    fc = wg_t.shape[0]
    tm, tn = _ffn_tiles(t, fc)

    def body(h_ref, wg_ref, g_ref):
        g_ref[...] = _dot(h_ref[...], wg_ref[...], NT)

    return pl.pallas_call(
        body, name=name, grid=(fc // tn, t // tm),
        in_specs=[pl.BlockSpec((tm, d), lambda j, i: (i, 0)), pl.BlockSpec((tn, d), lambda j, i: (j, 0))],
        out_specs=pl.BlockSpec((tm, tn), lambda j, i: (i, j)),
        out_shape=jax.ShapeDtypeStruct((t, fc), F32), compiler_params=_params(),
    )(hn, wg_t)


def _ffn_up_act(hn, wu_t, g, name):
    t, d = hn.shape
    fc = wu_t.shape[0]
    tm, tn = _ffn_tiles(t, fc)

    def body(h_ref, wu_ref, g_ref, gb_ref, u_ref, a_ref):
        for rows in _slabs(tm):
            u = _dot(h_ref[rows, :], wu_ref[...], NT)
            gv = g_ref[rows, :]
            gb_ref[rows, :] = gv.astype(BF)
            u_ref[rows, :] = u.astype(BF)
            a_ref[rows, :] = (gv * _sig(gv) * u).astype(BF)

    hid = pl.BlockSpec((tm, tn), lambda j, i: (i, j))
    out = jax.ShapeDtypeStruct((t, fc), BF)
    return pl.pallas_call(
        body, name=name, grid=(fc // tn, t // tm),
        in_specs=[pl.BlockSpec((tm, d), lambda j, i: (i, 0)), pl.BlockSpec((tn, d), lambda j, i: (j, 0)), hid],
        out_specs=[hid, hid, hid], out_shape=[out, out, out], compiler_params=_params(),
    )(hn, wu_t, g)


def _ffn_down(act, wd, xres, name):
    t, fc = act.shape
    d = wd.shape[1]
    tm, tk = _ffn_tiles(t, fc)

    def body(a_ref, w_ref, x_ref, o_ref):
        @pl.when(pl.program_id(1) == 0)
        def _():
            o_ref[...] = x_ref[...]

        o_ref[...] += 0.5 * _dot(a_ref[...], w_ref[...])

    row = pl.BlockSpec((tm, d), lambda i, k: (i, 0))
    return pl.pallas_call(
        body, name=name, grid=(t // tm, fc // tk),
        in_specs=[pl.BlockSpec((tm, tk), lambda i, k: (i, k)), pl.BlockSpec((tk, d), lambda i, k: (k, 0)), row],
        out_specs=row, out_shape=jax.ShapeDtypeStruct((t, d), F32), compiler_params=_params(),
    )(act, wd, xres)


def _ffn_bwd_hidden(dxb, wd, g, u, name):
    t, d = dxb.shape
    fc = wd.shape[0]
    tm, tn = _ffn_tiles(t, fc)

    def body(dx_ref, w_ref, g_ref, u_ref, dg_ref, du_ref):
        for rows in _slabs(tm):
            dh = 0.5 * _dot(dx_ref[rows, :], w_ref[...], NT)
            gv = g_ref[rows, :].astype(F32)
            uv = u_ref[rows, :].astype(F32)
            s = _sig(gv)
            dg_ref[rows, :] = (dh * uv * (s * (1.0 + gv * (1.0 - s)))).astype(BF)
            du_ref[rows, :] = (dh * (gv * s)).astype(BF)

    hid = pl.BlockSpec((tm, tn), lambda i, j: (i, j))
    out = jax.ShapeDtypeStruct((t, fc), BF)
    return pl.pallas_call(
        body, name=name, grid=(t // tm, fc // tn),
        in_specs=[pl.BlockSpec((tm, d), lambda i, j: (i, 0)), pl.BlockSpec((tn, d), lambda i, j: (j, 0)), hid, hid],
        out_specs=[hid, hid], out_shape=[out, out], compiler_params=_params(),
    )(dxb, wd, g, u)


def _ffn_dw(lhs, rhs, scale, name, dep=None):
    n = len(lhs)
    t, fc = lhs[0].shape
    d = rhs.shape[1]
    tk, tn = _tile(t, DW_ROWS, 16), _tile(fc, FFN_COLS, LANE)
    nt = t // tk

    def body(*refs):
        l_refs, r_ref, o_refs, acc_refs = refs[:n], refs[n], refs[n + 1:2 * n + 1], refs[2 * n + 1:]
        tt = pl.program_id(1)
        r = r_ref[...]
        for l_ref, o_ref, acc_ref in zip(l_refs, o_refs, acc_refs):
            @pl.when(tt == 0)
            def _():
                acc_ref[...] = jnp.zeros_like(acc_ref)

            acc_ref[...] += _dot(l_ref[...], r, TN)

            @pl.when(tt == nt - 1)
            def _():
                o_ref[...] = (scale * acc_ref[...]).astype(BF)

    lspec = pl.BlockSpec((tk, tn), lambda j, tt: (tt, j))
    ospec = pl.BlockSpec((tn, d), lambda j, tt: (j, 0))
    out = jax.ShapeDtypeStruct((fc, d), BF)
    return _call(
        body, [*lhs, rhs], dep=dep, name=name, grid=(fc // tn, nt),
        in_specs=[lspec] * n + [pl.BlockSpec((tk, d), lambda j, tt: (tt, 0))],
        out_specs=[ospec] * n, out_shape=[out] * n,
        scratch_shapes=[pltpu.VMEM((tn, d), F32)] * n, compiler_params=_params(),
    )


def _rms_bwd(dy, x, gain, dres, name):
    t, d = x.shape
    tr = _tile(t, 256, 16)

    def body(dy_ref, x_ref, g_ref, dres_ref, dx_ref, dxb_ref, dg_ref):
        _rms_bwd_tail(dy_ref, pl.program_id(0) == 0, x_ref, g_ref, dres_ref, dx_ref, dxb_ref, dg_ref)

    row = pl.BlockSpec((tr, d), lambda i: (i, 0))
    vec = pl.BlockSpec((1, d), lambda i: (0, 0))
    return pl.pallas_call(
        body, name=name, grid=(t // tr,),
        in_specs=[row, row, vec, row], out_specs=[row, row, vec],
        out_shape=[jax.ShapeDtypeStruct((t, d), F32), jax.ShapeDtypeStruct((t, d), BF),
                   jax.ShapeDtypeStruct((1, d), F32)],
        compiler_params=_params(),
    )(dy, x, gain, dres)


def _ffn_bwd_input(dg, du, wg_t, wu_t, name, dep=None):
    t, fc = dg.shape
    d = wg_t.shape[1]
    tm, tk = _ffn_tiles(t, fc)

    def body(dg_ref, du_ref, wg_ref, wu_ref, o_ref):
        @pl.when(pl.program_id(1) == 0)
        def _():
            o_ref[...] = jnp.zeros_like(o_ref)

        o_ref[...] += _dot(dg_ref[...], wg_ref[...]) + _dot(du_ref[...], wu_ref[...])

    hid = pl.BlockSpec((tm, tk), lambda i, k: (i, k))
    wspec = pl.BlockSpec((tk, d), lambda i, k: (k, 0))
    return _call(
        body, [dg, du, wg_t, wu_t], dep=dep, name=name, grid=(t // tm, fc // tk),
        in_specs=[hid, hid, wspec, wspec],
        out_specs=pl.BlockSpec((tm, d), lambda i, k: (i, 0)),
        out_shape=jax.ShapeDtypeStruct((t, d), F32), compiler_params=_params(),
    )


def _rope_tables(t):
    pos = jnp.arange(t, dtype=F32)
    inv_freq = ROPE_THETA ** (-jnp.arange(0, ROPE_DIM, 2, dtype=F32) / ROPE_DIM)
    ang = pos[:, None] * inv_freq[None, :]
    cos, sin = jnp.cos(ang), jnp.sin(ang)
    rest = HEAD_DIM - ROPE_DIM
    one = jnp.ones((t, rest), F32)
    zero_h = jnp.zeros((t, ROPE_HALF), F32)
    zero_r = jnp.zeros((t, rest), F32)
    c = jnp.concatenate([cos, cos, one], axis=1)
    s1 = jnp.concatenate([-sin, zero_h, zero_r], axis=1)
    s2 = jnp.concatenate([zero_h, sin, zero_r], axis=1)
    return c, s1, s2


def _rope(xh, c, s1, s2):
    return xh * c + pltpu.roll(xh, HEAD_DIM - ROPE_HALF, 1) * s1 + pltpu.roll(xh, ROPE_HALF, 1) * s2


def _rope_t(dh, c, s1, s2):
    return dh * c + pltpu.roll(dh * s1, ROPE_HALF, 1) + pltpu.roll(dh * s2, HEAD_DIM - ROPE_HALF, 1)


def _mixer_prep(proj, tables, bf_pad, hd, scale):
    t, np_ = proj.shape
    tr = _tile(t, 256, 16)
    nh = hd // HEAD_DIM
    nblk = hd // LANE
    f_blk = np_ // LANE - 1

    def body(qd_ref, kd_ref, vd_ref, qf_ref, kf_ref, vf_ref, fl_ref, c_ref, s1_ref, s2_ref, b_ref,
             oqd, okd, ovd, oqf, okf, ovf, olog):
        c, s1, s2 = c_ref[...], s1_ref[...], s2_ref[...]
        for h in range(nh):
            sl = slice(h * HEAD_DIM, (h + 1) * HEAD_DIM)
            oqd[:, sl] = (_rope(qd_ref[:, sl], c, s1, s2) * scale).astype(BF)
            okd[:, sl] = _rope(kd_ref[:, sl], c, s1, s2).astype(BF)
        ovd[...] = vd_ref[...].astype(BF)
        oqf[...] = (qf_ref[...] * scale).astype(BF)
        okf[...] = kf_ref[...].astype(BF)
        ovf[...] = vf_ref[...].astype(BF)
        z = fl_ref[...] + b_ref[...]
        olog[...] = jnp.minimum(z, 0.0) - jnp.log(1.0 + jnp.exp(-jnp.abs(z)))

    def col(kblk):
        return pl.BlockSpec((tr, hd), lambda i, kblk=kblk: (i, kblk))

    lane_row = pl.BlockSpec((tr, LANE), lambda i: (i, 0))
    in_specs = [col(0), col(1), col(2), col(3), col(4), col(5),
                pl.BlockSpec((tr, LANE), lambda i: (i, f_blk)),
                lane_row, lane_row, lane_row, pl.BlockSpec((1, LANE), lambda i: (0, 0))]
    o = pl.BlockSpec((tr, hd), lambda i: (i, 0))
    ob = jax.ShapeDtypeStruct((t, hd), BF)
    del nblk
    return pl.pallas_call(
        body, name="mixer_prep", grid=(t // tr,), in_specs=in_specs,
        out_specs=[o, o, o, o, o, o, lane_row],
        out_shape=[ob, ob, ob, ob, ob, ob, jax.ShapeDtypeStruct((t, LANE), F32)],
        compiler_params=_params(),
    )(proj, proj, proj, proj, proj, proj, proj, *tables, bf_pad)


def _split3(x):
    x1 = x.astype(BF)
    r1 = x - x1.astype(F32)
    x2 = r1.astype(BF)
    x3 = (r1 - x2.astype(F32)).astype(BF)
    return x1, x2, x3


def _cumsum_rows(x, reverse, name):
    t, w = x.shape
    blk = LANE
    nb = t // blk

    def body(x_ref, o_ref):
        r = lax.broadcasted_iota(jnp.int32, (blk, blk), 0)
        c = lax.broadcasted_iota(jnp.int32, (blk, blk), 1)
        tri = jnp.where((c >= r) if reverse else (c <= r), 1.0, 0.0).astype(BF)

        def step(i, carry):
            b = (nb - 1 - i) if reverse else i
            off = pl.multiple_of(b * blk, blk)
            xb = x_ref[pl.ds(off, blk), :]
            x1, x2, x3 = _split3(xb)
            o_ref[pl.ds(off, blk), :] = _dot(tri, x1) + _dot(tri, x2) + _dot(tri, x3) + carry
            return carry + jnp.sum(xb, axis=0, keepdims=True)

        lax.fori_loop(0, nb, step, jnp.zeros((1, w), F32))

    return pl.pallas_call(body, name=name, out_shape=jax.ShapeDtypeStruct((t, w), F32),
                          compiler_params=_params())(x)


ATTN_ROWS = 16


def _dil_bias_tiles(tq):
    nbias = MAX_WINDOW // tq + 1
    b = lax.broadcasted_iota(jnp.int32, (nbias, tq, tq), 0)
    i = lax.broadcasted_iota(jnp.int32, (nbias, tq, tq), 1)
    j = lax.broadcasted_iota(jnp.int32, (nbias, tq, tq), 2)
    delta = b * tq + i - j
    mult = jnp.zeros((nbias, tq, tq), F32)
    for w, dil in DIL_PATTERNS:
        mult = mult + jnp.where((delta >= 0) & (delta <= w) & (delta % dil == 0), 1.0, 0.0)
    return jnp.where(mult > 0.0, jnp.log(jnp.maximum(mult, 1.0)), NEG)


def _rep(x, width):
    return jnp.tile(x, (1, width // LANE))


def _chunks(n_rows, fn):
    for c in range(n_rows // ATTN_ROWS):
        fn(c * ATTN_ROWS)


def _causal(r0, tq, transposed):
    a = lax.broadcasted_iota(jnp.int32, (ATTN_ROWS, tq), 0) + r0
    b = lax.broadcasted_iota(jnp.int32, (ATTN_ROWS, tq), 1)
    return (a <= b) if transposed else (b <= a)


def _rows8(x):
    return jnp.transpose(x)[:8, :]


def _attn_fwd(mode, q, k, v, bias, tq, name):
    t, hd = q.shape
    nh = hd // HEAD_DIM
    nb = t // tq
    wb = MAX_WINDOW // tq
    fox = mode == "fox"

    def body(q_ref, k_ref, v_ref, b_ref, o_ref, lse_ref, lse_row_ref, s_ref, p_ref, m_ref, l_ref, acc_ref):
        qi = pl.program_id(1)
        qb = q_ref[...]
        m_ref[...] = jnp.full_like(m_ref, NEG)
        l_ref[...] = jnp.zeros_like(l_ref)
        acc_ref[...] = jnp.zeros_like(acc_ref)

        def tile(kj, diag):
            off = pl.multiple_of(kj * tq, tq)
            s_ref[...] = _dot(qb, k_ref[pl.ds(off, tq), :], NT)
            if fox:
                brow = b_ref[qi][:, :1] - b_ref[kj]

            def chunk(r0):
                rows = pl.ds(r0, ATTN_ROWS)
                if fox:
                    s = s_ref[rows, :] + brow
                    if diag:
                        s = jnp.where(_causal(r0, tq, False), s, NEG)
                else:
                    s = s_ref[rows, :] + b_ref[qi - kj, rows, :]
                m_old = m_ref[rows, :]
                m_new = jnp.maximum(m_old, jnp.max(s, axis=1, keepdims=True))
                p = jnp.exp(s - _rep(m_new, tq))
                alpha = jnp.exp(m_old - m_new)
                l_ref[rows, :] = alpha * l_ref[rows, :] + jnp.sum(p, axis=1, keepdims=True)
                m_ref[rows, :] = m_new
                acc_ref[rows, :] = alpha * acc_ref[rows, :]
                p_ref[rows, :] = p.astype(BF)

            _chunks(tq, chunk)
            acc_ref[...] += _dot(p_ref[...], v_ref[pl.ds(off, tq), :])

        tile(qi, True)
        if fox:
            lax.fori_loop(0, qi, lambda kj, c: (tile(kj, False), c)[1], 0)
        else:
            lax.fori_loop(1, jnp.minimum(qi, wb) + 1, lambda i, c: (tile(qi - i, False), c)[1], 0)
        o_ref[...] = (acc_ref[...] / l_ref[...]).astype(BF)
        lse = m_ref[...] + jnp.log(l_ref[...])
        lse_ref[...] = lse
        lse_row_ref[...] = _rows8(lse)

    qspec = pl.BlockSpec((tq, HEAD_DIM), lambda h, i: (i, h))
    kvspec = pl.BlockSpec((t, HEAD_DIM), lambda h, i: (0, h))
    repspec = pl.BlockSpec((None, tq, LANE), lambda h, i: (h, i, 0))
    row8spec = pl.BlockSpec((None, None, 8, tq), lambda h, i: (h, i, 0, 0))
    if fox:
        bspec = pl.BlockSpec((None, nb, 1, tq), lambda h, i: (h, 0, 0, 0))
    else:
        bspec = pl.BlockSpec((wb + 1, tq, tq), lambda h, i: (0, 0, 0))
    return pl.pallas_call(
        body, name=name, grid=(nh, nb), in_specs=[qspec, kvspec, kvspec, bspec],
        out_specs=[qspec, repspec, row8spec],
        out_shape=[jax.ShapeDtypeStruct((t, hd), BF), jax.ShapeDtypeStruct((nh, t, LANE), F32),
                   jax.ShapeDtypeStruct((nh, nb, 8, tq), F32)],
        scratch_shapes=[pltpu.VMEM((tq, tq), F32), pltpu.VMEM((tq, tq), BF), pltpu.VMEM((tq, LANE), F32),
                        pltpu.VMEM((tq, LANE), F32), pltpu.VMEM((tq, HEAD_DIM), F32)],
        compiler_params=_params(),
    )(q, k, v, bias)


def _attn_bwd_dq(mode, q, k, v, o, do, lse, bias, tq, name, dep=None):
    t, hd = q.shape
    nh = hd // HEAD_DIM
    nb = t // tq
    wb = MAX_WINDOW // tq
    fox = mode == "fox"

    def body(q_ref, k_ref, v_ref, o_ref, do_ref, lse_ref, b_ref, dq_ref, dl_row_ref,
             s_ref, dp_ref, x_ref, y_ref, acc_ref, acc2_ref, dl_ref):
        qi = pl.program_id(1)
        qb = q_ref[...]
        dob = do_ref[...]
        acc_ref[...] = jnp.zeros_like(acc_ref)
        if fox:
            acc2_ref[...] = jnp.zeros_like(acc2_ref)
            dl_ref[...] = jnp.zeros_like(dl_ref)
        else:
            prod = o_ref[...].astype(F32) * dob.astype(F32)
            dl_ref[...] = jnp.broadcast_to(jnp.sum(prod, axis=1, keepdims=True), (tq, LANE))

        def tile(kj, diag):
            off = pl.multiple_of(kj * tq, tq)
            kb = k_ref[pl.ds(off, tq), :]
            s_ref[...] = _dot(qb, kb, NT)
            dp_ref[...] = _dot(dob, v_ref[pl.ds(off, tq), :], NT)
            if fox:
                brow = b_ref[qi][:, :1] - b_ref[kj]

            def chunk(r0):
                rows = pl.ds(r0, ATTN_ROWS)
                lse_c = _rep(lse_ref[rows, :], tq)
                if fox:
                    s = s_ref[rows, :] + brow
                    if diag:
                        s = jnp.where(_causal(r0, tq, False), s, NEG)
                    p = jnp.exp(s - lse_c)
                    pdp = p * dp_ref[rows, :]
                    dl_ref[rows, :] += jnp.sum(pdp, axis=1, keepdims=True)
                    x_ref[rows, :] = pdp.astype(BF)
                    y_ref[rows, :] = p.astype(BF)
                else:
                    p = jnp.exp(s_ref[rows, :] + b_ref[qi - kj, rows, :] - lse_c)
                    x_ref[rows, :] = (p * (dp_ref[rows, :] - _rep(dl_ref[rows, :], tq))).astype(BF)

            _chunks(tq, chunk)
            acc_ref[...] += _dot(x_ref[...], kb)
            if fox:
                acc2_ref[...] += _dot(y_ref[...], kb)

        tile(qi, True)
        if fox:
            lax.fori_loop(0, qi, lambda kj, c: (tile(kj, False), c)[1], 0)
            dq_ref[...] = acc_ref[...] - dl_ref[...] * acc2_ref[...]
        else:
            lax.fori_loop(1, jnp.minimum(qi, wb) + 1, lambda i, c: (tile(qi - i, False), c)[1], 0)
            dq_ref[...] = acc_ref[...]
        dl_row_ref[...] = _rows8(dl_ref[...])

    qspec = pl.BlockSpec((tq, HEAD_DIM), lambda h, i: (i, h))
    kvspec = pl.BlockSpec((t, HEAD_DIM), lambda h, i: (0, h))
    repspec = pl.BlockSpec((None, tq, LANE), lambda h, i: (h, i, 0))
    row8spec = pl.BlockSpec((None, None, 8, tq), lambda h, i: (h, i, 0, 0))
    if fox:
        bspec = pl.BlockSpec((None, nb, 1, tq), lambda h, i: (h, 0, 0, 0))
    else:
        bspec = pl.BlockSpec((wb + 1, tq, tq), lambda h, i: (0, 0, 0))
    return _call(
        body, [q, k, v, o, do, lse, bias], dep=dep, name=name, grid=(nh, nb),
        in_specs=[qspec, kvspec, kvspec, qspec, qspec, repspec, bspec],
        out_specs=[qspec, row8spec],
        out_shape=[jax.ShapeDtypeStruct((t, hd), F32), jax.ShapeDtypeStruct((nh, nb, 8, tq), F32)],
        scratch_shapes=[pltpu.VMEM((tq, tq), F32), pltpu.VMEM((tq, tq), F32), pltpu.VMEM((tq, tq), BF),
                        pltpu.VMEM((tq, tq), BF), pltpu.VMEM((tq, HEAD_DIM), F32),
                        pltpu.VMEM((tq, HEAD_DIM), F32), pltpu.VMEM((tq, LANE), F32)],
        compiler_params=_params(),
    )


def _attn_bwd_dkv(mode, q, k, v, do, lse_row, dl_row, bias_t, c_row, tq, name):
    t, hd = q.shape
    nh = hd // HEAD_DIM
    nb = t // tq
    wb = MAX_WINDOW // tq
    fox = mode == "fox"

    def body(*refs):
        if fox:
            (q_ref, k_ref, v_ref, do_ref, lse_ref, dl_ref, b_ref, cq_ref, dk_ref, dv_ref, dc_row_ref,
             s_ref, dp_ref, x_ref, y_ref, dc_ref) = refs
        else:
            q_ref, k_ref, v_ref, do_ref, lse_ref, dl_ref, b_ref, dk_ref, dv_ref, s_ref, dp_ref, x_ref, y_ref = refs
        kj = pl.program_id(1)
        kb = k_ref[...]
        vb = v_ref[...]
        dk_ref[...] = jnp.zeros_like(dk_ref)
        dv_ref[...] = jnp.zeros_like(dv_ref)
        if fox:
            dc_ref[...] = jnp.zeros_like(dc_ref)

        def tile(qi, diag):
            off = pl.multiple_of(qi * tq, tq)
            qb = q_ref[pl.ds(off, tq), :]
            dob = do_ref[pl.ds(off, tq), :]
            s_ref[...] = _dot(kb, qb, NT)
            dp_ref[...] = _dot(vb, dob, NT)
            lse_r = lse_ref[qi, 0:1, :]
            dl_r = dl_ref[qi, 0:1, :]
            if fox:
                kbias = cq_ref[qi][:, :1] - b_ref[...]

            def chunk(r0):
                rows = pl.ds(r0, ATTN_ROWS)
                if fox:
                    s = s_ref[rows, :] + _rep(kbias[r0:r0 + ATTN_ROWS, :], tq)
                    if diag:
                        s = jnp.where(_causal(r0, tq, True), s, NEG)
                else:
                    s = s_ref[rows, :] + b_ref[qi - kj, rows, :]
                pt = jnp.exp(s - lse_r)
                dst = pt * (dp_ref[rows, :] - dl_r)
                x_ref[rows, :] = pt.astype(BF)
                y_ref[rows, :] = dst.astype(BF)
                if fox:
                    dc_ref[rows, :] -= jnp.sum(dst, axis=1, keepdims=True)

            _chunks(tq, chunk)
            dv_ref[...] += _dot(x_ref[...], dob)
            dk_ref[...] += _dot(y_ref[...], qb)

        tile(kj, True)
        hi = nb if fox else jnp.minimum(kj + wb + 1, nb)
        lax.fori_loop(kj + 1, hi, lambda qi, c: (tile(qi, False), c)[1], 0)
        if fox:
            dc_row_ref[...] = _rows8(dc_ref[...])

    blkspec = pl.BlockSpec((tq, HEAD_DIM), lambda h, j: (j, h))
    fullspec = pl.BlockSpec((t, HEAD_DIM), lambda h, j: (0, h))
    rows8spec = pl.BlockSpec((None, nb, 8, tq), lambda h, j: (h, 0, 0, 0))
    repspec = pl.BlockSpec((None, tq, LANE), lambda h, j: (h, j, 0))
    in_specs = [fullspec, blkspec, blkspec, fullspec, rows8spec, rows8spec]
    args = [q, k, v, do, lse_row, dl_row, bias_t]
    out_specs = [blkspec, blkspec]
    out_shape = [jax.ShapeDtypeStruct((t, hd), F32), jax.ShapeDtypeStruct((t, hd), F32)]
    scratch = [pltpu.VMEM((tq, tq), F32), pltpu.VMEM((tq, tq), F32), pltpu.VMEM((tq, tq), BF),
               pltpu.VMEM((tq, tq), BF)]
    if fox:
        in_specs += [repspec, pl.BlockSpec((None, nb, 1, tq), lambda h, j: (h, 0, 0, 0))]
        args.append(c_row)
        out_specs.append(pl.BlockSpec((None, None, 8, tq), lambda h, j: (h, j, 0, 0)))
        out_shape.append(jax.ShapeDtypeStruct((nh, nb, 8, tq), F32))
        scratch.append(pltpu.VMEM((tq, LANE), F32))
    else:
        in_specs.append(pl.BlockSpec((wb + 1, tq, tq), lambda h, j: (0, 0, 0)))
    return pl.pallas_call(
        body, name=name, grid=(nh, nb), in_specs=in_specs, out_specs=out_specs, out_shape=out_shape,
        scratch_shapes=scratch, compiler_params=_params(),
    )(*args)


def _gate_specs(t, d, hd, tr):
    row = pl.BlockSpec((tr, d), lambda i: (i, 0))
    vec = pl.BlockSpec((1, d), lambda i: (0, 0))
    base = 6 * hd // d
    gd = pl.BlockSpec((tr, d), lambda i: (i, base))
    gf = pl.BlockSpec((tr, d), lambda i: (i, base + 1))
    return row, vec, gd, gf


def _proj_merge(yd, yf, wpd, wpf, proj, b_d, b_f, hd):
    t = yd.shape[0]
    d = wpd.shape[1]
    tr = _tile(t, 256, 16)
    row, vec, gd, gf = _gate_specs(t, d, hd, tr)

    def body(yd_ref, yf_ref, wd_ref, wf_ref, gd_ref, gf_ref, bd_ref, bf_ref, pd_ref, pf_ref, o_ref):
        pd = _dot(yd_ref[...], wd_ref[...])
        pf = _dot(yf_ref[...], wf_ref[...])
        pd_ref[...] = pd
        pf_ref[...] = pf
        o_ref[...] = (_sig(gd_ref[...] + bd_ref[...]) * pd + _sig(gf_ref[...] + bf_ref[...]) * pf).astype(BF)

    yspec = pl.BlockSpec((tr, hd), lambda i: (i, 0))
    wspec = pl.BlockSpec((hd, d), lambda i: (0, 0))
    f32 = jax.ShapeDtypeStruct((t, d), F32)
    return pl.pallas_call(
        body, name="proj_merge", grid=(t // tr,), in_specs=[yspec, yspec, wspec, wspec, gd, gf, vec, vec],
        out_specs=[row, row, row], out_shape=[f32, f32, jax.ShapeDtypeStruct((t, d), BF)],
        compiler_params=_params(),
    )(yd, yf, wpd, wpf, proj, proj, b_d, b_f)


def _merge_bwd(dm, pd, pf, proj, b_d, b_f, hd):
    t, d = pd.shape
    tr = _tile(t, 256, 16)
    row, vec, gd, gf = _gate_specs(t, d, hd, tr)

    def body(dm_ref, pd_ref, pf_ref, gd_ref, gf_ref, bd_ref, bf_ref,
             dpd_ref, dpf_ref, dgd_ref, dgf_ref, dbd_ref, dbf_ref):
        dmv = dm_ref[...]
        sd = _sig(gd_ref[...] + bd_ref[...])
        sf = _sig(gf_ref[...] + bf_ref[...])
        dgd = dmv * pd_ref[...] * (sd * (1.0 - sd))
        dgf = dmv * pf_ref[...] * (sf * (1.0 - sf))
        dpd_ref[...] = (dmv * sd).astype(BF)
        dpf_ref[...] = (dmv * sf).astype(BF)
        dgd_ref[...] = dgd.astype(BF)
        dgf_ref[...] = dgf.astype(BF)

        @pl.when(pl.program_id(0) == 0)
        def _():
            dbd_ref[...] = jnp.zeros_like(dbd_ref)
            dbf_ref[...] = jnp.zeros_like(dbf_ref)

        dbd_ref[...] += jnp.sum(dgd, axis=0, keepdims=True)
        dbf_ref[...] += jnp.sum(dgf, axis=0, keepdims=True)

    ob = jax.ShapeDtypeStruct((t, d), BF)
    ov = jax.ShapeDtypeStruct((1, d), F32)
    return pl.pallas_call(
        body, name="merge_bwd", grid=(t // tr,), in_specs=[row, row, row, gd, gf, vec, vec],
        out_specs=[row, row, row, row, vec, vec], out_shape=[ob, ob, ob, ob, ov, ov],
        compiler_params=_params(),
    )(dm, pd, pf, proj, proj, b_d, b_f)


def _assemble_dproj(dqd, dkd, dvd, dqf, dkf, dvf, dgd, dgf, dlogf, proj, tables, bf_pad, scale):
    t, np_ = proj.shape
    hd = dqd.shape[1]
    d = dgd.shape[1]
    nh = hd // HEAD_DIM
    tr = _tile(t, 256, 16)
    f_blk = np_ // LANE - 1

    def body(dqd_ref, dkd_ref, dvd_ref, dqf_ref, dkf_ref, dvf_ref, dgd_ref, dgf_ref, dlog_ref, fl_ref,
             c_ref, s1_ref, s2_ref, b_ref, o_ref, db_ref):
        c, s1, s2 = c_ref[...], s1_ref[...], s2_ref[...]
        for h in range(nh):
            sl = slice(h * HEAD_DIM, (h + 1) * HEAD_DIM)
            o_ref[:, sl] = (_rope_t(dqd_ref[:, sl], c, s1, s2) * scale).astype(BF)
            o_ref[:, hd + h * HEAD_DIM:hd + (h + 1) * HEAD_DIM] = _rope_t(dkd_ref[:, sl], c, s1, s2).astype(BF)
        o_ref[:, 2 * hd:3 * hd] = dvd_ref[...].astype(BF)
        o_ref[:, 3 * hd:4 * hd] = (dqf_ref[...] * scale).astype(BF)
        o_ref[:, 4 * hd:5 * hd] = dkf_ref[...].astype(BF)
        o_ref[:, 5 * hd:6 * hd] = dvf_ref[...].astype(BF)
        o_ref[:, 6 * hd:6 * hd + d] = dgd_ref[...]
        o_ref[:, 6 * hd + d:6 * hd + 2 * d] = dgf_ref[...]
        z = fl_ref[...] + b_ref[...]
        dfl = dlog_ref[...] * _sig(-z)
        o_ref[:, 6 * hd + 2 * d:] = dfl.astype(BF)

        @pl.when(pl.program_id(0) == 0)
        def _():
            db_ref[...] = jnp.zeros_like(db_ref)

        db_ref[...] += jnp.sum(dfl, axis=0, keepdims=True)

    head = pl.BlockSpec((tr, hd), lambda i: (i, 0))
    row = pl.BlockSpec((tr, d), lambda i: (i, 0))
    lane_row = pl.BlockSpec((tr, LANE), lambda i: (i, 0))
    lane_vec = pl.BlockSpec((1, LANE), lambda i: (0, 0))
    return pl.pallas_call(
        body, name="assemble_dproj", grid=(t // tr,),
        in_specs=[head] * 6 + [row, row, lane_row, pl.BlockSpec((tr, LANE), lambda i: (i, f_blk)),
                               lane_row, lane_row, lane_row, lane_vec],
        out_specs=[pl.BlockSpec((tr, np_), lambda i: (i, 0)), lane_vec],
        out_shape=[jax.ShapeDtypeStruct((t, np_), BF), jax.ShapeDtypeStruct((1, LANE), F32)],
        compiler_params=_params(),
    )(dqd, dkd, dvd, dqf, dkf, dvf, dgd, dgf, dlogf, proj, *tables, bf_pad)


def _to_rows(a, tq):
    h, t = a.shape
    return a.reshape(h, t // tq, 1, tq)


def kernel(x, ffn1_norm, ffn1_w_gate, ffn1_w_up, ffn1_w_down, mix_norm, w_in, b_forget, b_gate_dil, b_gate_fox, w_proj_dil, w_proj_fox, w_out, ffn2_norm, ffn2_w_gate, ffn2_w_up, ffn2_w_down, final_norm, loss_target, m_ffn1_norm, m_ffn1_w_gate, m_ffn1_w_up, m_ffn1_w_down, m_mix_norm, m_w_in, m_b_forget, m_b_gate_dil, m_b_gate_fox, m_w_proj_dil, m_w_proj_fox, m_w_out, m_ffn2_norm, m_ffn2_w_gate, m_ffn2_w_up, m_ffn2_w_down, m_final_norm, v_ffn1_norm, v_ffn1_w_gate, v_ffn1_w_up, v_ffn1_w_down, v_mix_norm, v_w_in, v_b_forget, v_b_gate_dil, v_b_gate_fox, v_w_proj_dil, v_w_proj_fox, v_w_out, v_ffn2_norm, v_ffn2_w_gate, v_ffn2_w_up, v_ffn2_w_down, v_final_norm):
    t, d = x.shape[1], x.shape[2]
    hd = w_proj_dil.shape[1]
    nh = hd // HEAD_DIM
    n_f = b_forget.shape[1]
    cols = w_in.shape[2]
    in_cols = N_DEV * cols
    assert in_cols == 6 * hd + n_f + 2 * d and n_f == nh and n_f <= LANE
    np_ = 6 * hd + 2 * d + LANE
    scale = HEAD_DIM ** -0.5
    tq = _tile(t, 512, LANE)
    assert MAX_WINDOW % tq == 0 and tq % 16 == 0

    x2d = x[0]
    tgt = loss_target[0]

    def rows(w):
        return jnp.swapaxes(w, 1, 2)

    fc = N_DEV * ffn1_w_down.shape[1]
    ag_order = [rows(ffn1_w_gate), rows(ffn1_w_up), ffn1_w_down, w_in, w_proj_dil, w_proj_fox, w_out,
                rows(ffn2_w_gate), rows(ffn2_w_up), ffn2_w_down]
    ag_first, tok = _exchange_start([w[0].astype(BF) for w in ag_order[:2]], True, "ag_start_first", ks=FIRST_LEVEL)
    ag_rest, ag_token = _exchange_start([w[0].astype(BF) for w in ag_order[2:]], True, "ag_start", dep=tok,
                                        ks=FIRST_LEVEL)
    ag = ag_first + ag_rest

    def relay(idx, after, name):
        for i, h in zip(idx, _gather_relay([ag[i] for i in idx], after, name)):
            ag[i] = h

    def gathered(idx, after, name):
        return _gather_wait([ag[i] for i in idx], after, name)

    def ffn_weight(idx, after, name):
        return [w.reshape(fc, d) for w in gathered(idx, after, name)]

    tables = _rope_tables(t)
    bf_pad = jnp.pad(b_forget, ((0, 0), (0, LANE - n_f)))

    hn1, = _rms_fwd(x2d, ffn1_norm, "rms_ffn1", dep=ag_token)
    relay([0], hn1, "ag_relay_ffn1_gate")
    wg1, = ffn_weight([0], hn1, "ag_wait_ffn1_gate")
    g1_f32 = _ffn_gate(hn1, wg1, "ffn1_gate")
    relay([1], g1_f32, "ag_relay_ffn1_up")
    wu1, = ffn_weight([1], g1_f32, "ag_wait_ffn1_up")
    relay([2], wu1, "ag_relay_ffn1_down")
    g1, u1, a1 = _ffn_up_act(hn1, wu1, g1_f32, "ffn1_up_act")
    wd1, = ffn_weight([2], a1, "ag_wait_ffn1_down")
    relay([3], wd1, "ag_relay_w_in")
    x1 = _ffn_down(a1, wd1, x2d, "ffn1_down")

    hm, hm_t = _rms_fwd(x1, mix_norm, "rms_mix", with_transpose=True)
    win_g, = gathered([3], hm, "ag_wait_w_in")
    relay([4, 5, 6], win_g, "ag_relay_mixer")
    segments = [(0, 6 * hd), (6 * hd + n_f, in_cols), (6 * hd, 6 * hd + n_f)]
    pieces = []
    for lo, hi in segments:
        for j in range(lo // cols, (hi - 1) // cols + 1):
            s, e = max(lo, j * cols), min(hi, (j + 1) * cols)
            pieces.append(win_g[j, :, s - j * cols:e - j * cols])
    win_p = jnp.concatenate(pieces + [jnp.zeros((d, LANE - n_f), BF)], axis=1)
    proj = _mm_nn(hm, win_p, F32, "w_in_fwd")
    qd, kd, vd, qf, kf, vf, logf = _mixer_prep(proj, tables, bf_pad, hd, scale)
    csum = _cumsum_rows(logf, False, "cumsum_logf")
    c_heads = csum[:, :nh].T
    c_row = _to_rows(c_heads, tq)
    c_rep = jnp.broadcast_to(c_heads[:, :, None], (nh, t, LANE))
    dil_bias = _dil_bias_tiles(tq)
    dil_bias_t = dil_bias.transpose(0, 2, 1)
    relay([7, 8, 9], qd, "ag_relay_ffn2")
    yd, lse_d, lse_d_row = _attn_fwd("dil", qd, kd, vd, dil_bias, tq, "attn_dil_fwd")
    yf, lse_f, lse_f_row = _attn_fwd("fox", qf, kf, vf, c_row, tq, "attn_fox_fwd")
    wpd_g, wpf_g = gathered([4, 5], yf, "ag_wait_proj")
    wpd = wpd_g.transpose(1, 0, 2).reshape(hd, d)
    wpf = wpf_g.transpose(1, 0, 2).reshape(hd, d)
    pd, pf, merged = _proj_merge(yd, yf, wpd, wpf, proj, b_gate_dil, b_gate_fox, hd)
    wout_g, = gathered([6], merged, "ag_wait_w_out")
    wout = wout_g.reshape(d, d)
    x2 = _mm_nn(merged, wout, F32, "w_out_fwd", residual=x1, tn_pref=1024)

    hn2, = _rms_fwd(x2, ffn2_norm, "rms_ffn2")
    wg2, wu2 = ffn_weight([7, 8], hn2, "ag_wait_ffn2_gate_up")
    g2, u2, a2 = _ffn_gate_up(hn2, wg2, wu2, "ffn2_gate_up")
    wd2, = ffn_weight([9], a2, "ag_wait_ffn2_down")
    x3 = _ffn_down(a2, wd2, x2, "ffn2_down")

    dx3, dx3b, d_final, loss_lanes = _loss_head(x3, final_norm.reshape(1, d), tgt)

    def ffn_bwd(dxb, hn, g, u, a, wg_t, wu_t, wd, x_in, gain, dres, tag):
        def parts(dw):
            return dw.reshape(N_DEV, fc // N_DEV, d)

        dg, du = _ffn_bwd_hidden(dxb, wd, g, u, tag + "_bwd_hidden")
        dwd, = _ffn_dw([a], dxb, 0.5, tag + "_dw_down")
        rs_down, tok = _exchange_start([parts(dwd)], False, "rs_start_" + tag + "_down")
        dwg_t, dwu_t = _ffn_dw([dg, du], hn, 1.0, tag + "_dw_gate_up", dep=tok)
        rs_gu, tok = _exchange_start([parts(dwg_t), parts(dwu_t)], False, "rs_start_" + tag + "_gate_up")
        dhn = _ffn_bwd_input(dg, du, wg_t, wu_t, tag + "_bwd_input", dep=tok)
        dx, dx_bf, dgain = _rms_bwd(dhn, x_in, gain, dres, "rms_" + tag + "_bwd")
        return dx, dx_bf, dgain, rs_gu + rs_down

    dx2, dx2b, d_ffn2_norm, rs_ffn2 = ffn_bwd(dx3b, hn2, g2, u2, a2, wg2, wu2, wd2, x2, ffn2_norm, dx3, "ffn2")

    dmerged = _mm_nt(dx2b, wout, F32, "w_out_bwd")
    dwout = _mm_tn(merged, dx2b, BF, "w_out_dw", tn_pref=1024)
    dpd, dpf, dgd, dgf, d_bd, d_bf = _merge_bwd(dmerged, pd, pf, proj, b_gate_dil, b_gate_fox, hd)
    dyd = _mm_nt(dpd, wpd, BF, "proj_dil_bwd")
    dyf = _mm_nt(dpf, wpf, BF, "proj_fox_bwd")
    dwpd = _mm_tn(yd, dpd, BF, "proj_dil_dw", tn_pref=1024)
    dwpf = _mm_tn(yf, dpf, BF, "proj_fox_dw", tn_pref=1024)
    dwpd_c = dwpd.reshape(hd, N_DEV, d // N_DEV).transpose(1, 0, 2)
    dwpf_c = dwpf.reshape(hd, N_DEV, d // N_DEV).transpose(1, 0, 2)
    dwout_c = dwout.reshape(N_DEV, d // N_DEV, d)
    rs_mix, tok = _exchange_start([dwout_c, dwpd_c, dwpf_c], False, "rs_start_mixer")

    dqd, dl_d = _attn_bwd_dq("dil", qd, kd, vd, yd, dyd, lse_d, dil_bias, tq, "attn_dil_dq", dep=tok)
    dkd, dvd = _attn_bwd_dkv("dil", qd, kd, vd, dyd, lse_d_row, dl_d, dil_bias_t, None, tq, "attn_dil_dkv")
    dqf, dl_f = _attn_bwd_dq("fox", qf, kf, vf, yf, dyf, lse_f, c_row, tq, "attn_fox_dq")
    dkf, dvf, dc = _attn_bwd_dkv("fox", qf, kf, vf, dyf, lse_f_row, dl_f, c_rep, c_row, tq, "attn_fox_dkv")
    dc_pad = jnp.pad(dc[:, :, 0, :].reshape(nh, t).T, ((0, 0), (0, LANE - nh)))
    dlogf = _cumsum_rows(dc_pad, True, "revcumsum_dc")
    dproj, d_bforget = _assemble_dproj(dqd, dkd, dvd, dqf, dkf, dvf, dgd, dgf, dlogf, proj, tables, bf_pad, scale)

    dwin_p = _mm_tn(hm_t, dproj, BF, "w_in_dw", tk_pref=DW_ROWS, a_transposed=True)

    def perm_col(c):
        if c < 6 * hd:
            return c
        return c + 2 * d if c < 6 * hd + n_f else c - n_f

    shards = []
    for j in range(N_DEV):
        cuts = sorted({j * cols, (j + 1) * cols} | {c for c in (6 * hd, 6 * hd + n_f) if j * cols < c < (j + 1) * cols})
        shards.append(jnp.concatenate([dwin_p[:, perm_col(lo):perm_col(lo) + hi - lo]
                                       for lo, hi in zip(cuts[:-1], cuts[1:])], axis=1))
    dwin_c = jnp.stack(shards)
    rs_win, tok = _exchange_start([dwin_c], False, "rs_start_w_in")
    dx1, dx1b, d_mix_norm = _mm_nt(dproj, win_p, F32, "w_in_bwd", tn_pref=d, tk_pref=1152,
                                   rms=(x1, mix_norm, dx2), dep=tok)

    grad_x, _, d_ffn1_norm, rs_ffn1 = ffn_bwd(dx1b, hn1, g1, u1, a1, wg1, wu1, wd1, x2d, ffn1_norm, dx1, "ffn1")

    def update(handles, names, after, tag):
        recvs = _exchange_wait(handles, False, after, "rs_wait_" + tag)
        res = {}
        for recv, n in zip(recvs, names):
            turn = rows if n.endswith(("w_gate", "w_up")) else (lambda a: a)
            w, m, v = (turn(a)[0] for a in wmv[n])
            res[n] = tuple(turn(o[None]) for o in _adam_from_partials(recv, w, m, v, "adam_" + n))
        return res, res[names[-1]][0]

    wmv = {
        "ffn1_w_gate": (ffn1_w_gate, m_ffn1_w_gate, v_ffn1_w_gate),
        "ffn1_w_up": (ffn1_w_up, m_ffn1_w_up, v_ffn1_w_up),
        "ffn1_w_down": (ffn1_w_down, m_ffn1_w_down, v_ffn1_w_down),
        "w_in": (w_in, m_w_in, v_w_in),
        "w_proj_dil": (w_proj_dil, m_w_proj_dil, v_w_proj_dil),
        "w_proj_fox": (w_proj_fox, m_w_proj_fox, v_w_proj_fox),
        "w_out": (w_out, m_w_out, v_w_out),
        "ffn2_w_gate": (ffn2_w_gate, m_ffn2_w_gate, v_ffn2_w_gate),
        "ffn2_w_up": (ffn2_w_up, m_ffn2_w_up, v_ffn2_w_up),
        "ffn2_w_down": (ffn2_w_down, m_ffn2_w_down, v_ffn2_w_down),
    }
    big = {}
    after = grad_x
    for handles, names, tag in [
            (rs_ffn2, ["ffn2_w_gate", "ffn2_w_up", "ffn2_w_down"], "ffn2"),
            (rs_mix, ["w_out", "w_proj_dil", "w_proj_fox"], "mixer"),
            (rs_win, ["w_in"], "w_in"),
            (rs_ffn1, ["ffn1_w_gate", "ffn1_w_up", "ffn1_w_down"], "ffn1")]:
        res, after = update(handles, names, after, tag)
        big.update(res)

    def lanes(a):
        a = a.reshape(1, -1)
        return jnp.pad(a, ((0, 0), (0, d - a.shape[1])))

    small_names = ["ffn1_norm", "mix_norm", "b_gate_dil", "b_gate_fox", "ffn2_norm", "final_norm", "b_forget"]
    small_g = [d_ffn1_norm, d_mix_norm, d_bd, d_bf, d_ffn2_norm, d_final, d_bforget[:, :n_f]]
    small_w = [ffn1_norm, mix_norm, b_gate_dil, b_gate_fox, ffn2_norm, final_norm, b_forget]
    small_m = [m_ffn1_norm, m_mix_norm, m_b_gate_dil, m_b_gate_fox, m_ffn2_norm, m_final_norm, m_b_forget]
    small_v = [v_ffn1_norm, v_mix_norm, v_b_gate_dil, v_b_gate_fox, v_ffn2_norm, v_final_norm, v_b_forget]
    pack = lambda arrs, last: jnp.concatenate([lanes(a) for a in arrs] + [last], axis=0)
    g_all = _allreduce_small(pack(small_g, loss_lanes))
    zero_row = jnp.zeros((1, d), F32)
    one_row = jnp.ones((1, d), F32)
    s_delta, s_m, s_v = _adam_small(g_all, pack(small_w, zero_row), pack(small_m, zero_row), pack(small_v, one_row))
    loss = g_all[len(small_names), 0]

    def unpack(packed, i, like):
        return packed[i, :like.size].reshape(like.shape)

    small = {}
    for i, (n, w) in enumerate(zip(small_names, small_w)):
        small[n] = (unpack(g_all, i, w), unpack(s_delta, i, w), unpack(s_m, i, w), unpack(s_v, i, w))

    order = ["ffn1_norm", "ffn1_w_gate", "ffn1_w_up", "ffn1_w_down", "mix_norm", "w_in", "b_forget", "b_gate_dil",
             "b_gate_fox", "w_proj_dil", "w_proj_fox", "w_out", "ffn2_norm", "ffn2_w_gate", "ffn2_w_up",
             "ffn2_w_down", "final_norm"]
    res = {**big, **small}
    outs = [loss, grad_x[None]]
    for slot in range(4):
        outs += [res[n][slot] for n in order]
    return tuple(outs)
```

```python
import jax
import jax.numpy as jnp
from jax import lax
from jax.experimental import pallas as pl
from jax.experimental.pallas import tpu as pltpu

BF = jnp.bfloat16
F32 = jnp.float32
MESH = pl.DeviceIdType.MESH
N_DEV = 8

HEAD_DIM = 128
ROPE_DIM = HEAD_DIM // 4
ROPE_HALF = ROPE_DIM // 2
ROPE_THETA = 500000.0
NORM_EPS = 1e-6
DIL_PATTERNS = ((128, 1), (512, 4), (2048, 16))
MAX_WINDOW = 2048
LANE = 128
NEG = -1e30

ADAM_LR = 0.001
ADAM_B1 = 0.9
ADAM_B2 = 0.999
ADAM_EPS = 1e-08
ADAM_WD = 0.01
ADAM_STEP = 10

VMEM_LIMIT_BYTES = 56 * 1024 * 1024
FFN_ROWS = 1024
DW_ROWS = 1024
ANY = pl.BlockSpec(memory_space=pl.ANY)

NN = (((1,), (0,)), ((), ()))
NT = (((1,), (1,)), ((), ()))
TN = (((0,), (0,)), ((), ()))


def _dot(a, b, dn=NN):
    return lax.dot_general(a, b, dn, preferred_element_type=F32)


def _sig(x):
    return 0.5 + 0.5 * jnp.tanh(0.5 * x)


def _tile(n, pref, align):
    best = None
    t = align
    while t <= min(n, pref):
        if n % t == 0:
            best = t
        t += align
    return n if best is None else best


def _params():
    return pltpu.CompilerParams(vmem_limit_bytes=VMEM_LIMIT_BYTES)


def _call(body, args, dep=None, **kw):
    if dep is not None:
        n_in = len(args)
        inner = body

        def body(*refs):
            inner(*refs[:n_in], *refs[n_in + 1:])

        kw["in_specs"] = list(kw["in_specs"]) + [ANY]
        args = list(args) + [dep]
    return pl.pallas_call(body, **kw)(*args)


def _peers():
    x, y, c = lax.axis_index("x"), lax.axis_index("y"), lax.axis_index("c")
    me = 4 * x + 2 * y + c
    peers = []
    for k in range(1, N_DEV):
        px = 1 - x if (k >> 2) & 1 else x
        py = 1 - y if (k >> 1) & 1 else y
        pc = 1 - c if k & 1 else c
        peers.append((k, (px, py, pc), 4 * px + 2 * py + pc))
    return me, peers


HBM = pl.BlockSpec(memory_space=pltpu.HBM)
SEM = pl.BlockSpec(memory_space=pltpu.SEMAPHORE)
EFFECT = pltpu.SideEffectType.DATAFLOW_SIDE_EFFECTING


def _exchange_copy(gather, src_ref, land_ref, send_sems, recv_sems, me, k, peer, peer_flat, landing):
    return pltpu.make_async_remote_copy(
        src_ref=src_ref if gather else src_ref.at[peer_flat], dst_ref=land_ref.at[landing],
        send_sem=send_sems.at[k], recv_sem=recv_sems.at[k], device_id=peer, device_id_type=MESH)


ALL_PEERS = (1, 2, 3, 4, 5, 6, 7)
SIBLING = 1
SAME_CORE = (2, 4, 6)
FIRST_LEVEL = (SIBLING,) + SAME_CORE


def _exchange_start(srcs, gather, name, dep=None, ks=ALL_PEERS):
    n = len(srcs)
    extra = [] if dep is None else [dep]

    def body(*refs):
        src_refs, land_refs = refs[:n], refs[n:2 * n]
        refs = refs[2 * n + len(extra):]
        send_refs, recv_refs = refs[:n], refs[n:2 * n]
        token = refs[4 * n]
        me, peers = _peers()
        for i in range(n):
            for k, peer, peer_flat in peers:
                if k in ks:
                    _exchange_copy(gather, src_refs[i], land_refs[i], send_refs[i], recv_refs[i],
                                   me, k, peer, peer_flat, me).start()
        token[...] = jnp.zeros_like(token)

    lands = [lax.empty((N_DEV,) + s.shape[-2:], s.dtype) for s in srcs]
    sems = [pltpu.SemaphoreType.DMA((N_DEV,)) for _ in range(2 * n)]
    out = pl.pallas_call(
        body, name=name,
        out_shape=tuple(sems) + tuple(pltpu.HBM(a.shape, a.dtype) for a in list(srcs) + lands)
        + (jax.ShapeDtypeStruct((8, LANE), F32),),
        in_specs=[HBM] * (2 * n) + [ANY] * len(extra),
        out_specs=tuple([SEM] * (2 * n) + [HBM] * (2 * n) + [pl.BlockSpec(memory_space=pltpu.VMEM)]),
        input_output_aliases={i: 2 * n + i for i in range(2 * n)},
        compiler_params=pltpu.CompilerParams(has_side_effects=EFFECT),
    )(*[pltpu.with_memory_space_constraint(a, pltpu.HBM) for a in list(srcs) + lands], *extra)
    handles = [(out[2 * n + i], out[3 * n + i], out[i], out[n + i]) for i in range(n)]
    return handles, out[4 * n]


def _exchange_wait(handles, gather, after, name):
    n = len(handles)

    def body(*refs):
        src_refs, land_refs = refs[:n], refs[n:2 * n]
        send_refs, recv_refs = refs[2 * n:3 * n], refs[3 * n:4 * n]
        me, peers = _peers()
        for i in range(n):
            for k, peer, peer_flat in peers:
                cp = _exchange_copy(gather, src_refs[i], land_refs[i], send_refs[i], recv_refs[i],
                                    me, k, peer, peer_flat, peer_flat)
                cp.wait_send()
                cp.wait_recv()

    srcs = [h[0] for h in handles]
    lands = [h[1] for h in handles]
    out = pl.pallas_call(
        body, name=name,
        out_shape=tuple(pltpu.HBM(a.shape, a.dtype) for a in srcs + lands),
        in_specs=[HBM] * (2 * n) + [SEM] * (2 * n) + [ANY],
        out_specs=tuple([HBM] * (2 * n)),
        input_output_aliases={i: i for i in range(2 * n)},
        compiler_params=pltpu.CompilerParams(has_side_effects=EFFECT),
    )(*srcs, *lands, *[h[2] for h in handles], *[h[3] for h in handles], after)
    me = 4 * lax.axis_index("x") + 2 * lax.axis_index("y") + lax.axis_index("c")
    filled = []
    for src, land in zip(out[:n], out[n:]):
        own = src[None] if gather else lax.dynamic_slice_in_dim(src, me, 1, axis=0)
        filled.append(lax.dynamic_update_slice_in_dim(land, own, me, axis=0))
    return filled


def _gather_relay(handles, after, name):
    n = len(handles)

    def body(*refs):
        land_refs, recv_refs = refs[:n], refs[n:2 * n]
        refs = refs[2 * n + 1:]
        send2_refs, recv2_refs = refs[n:2 * n], refs[2 * n:3 * n]
        me, peers = _peers()
        sibling = peers[SIBLING - 1][1]
        for i in range(n):
            for k, peer, peer_flat in peers:
                if k in SAME_CORE:
                    block = land_refs[i].at[peer_flat]
                    pltpu.make_async_remote_copy(
                        src_ref=block, dst_ref=block, send_sem=send2_refs[i].at[k], recv_sem=recv_refs[i].at[k],
                        device_id=peer, device_id_type=MESH).wait_recv()
                    pltpu.make_async_remote_copy(
                        src_ref=block, dst_ref=block, send_sem=send2_refs[i].at[k], recv_sem=recv2_refs[i].at[k],
                        device_id=sibling, device_id_type=MESH).start()

    lands = [h[1] for h in handles]
    sems = [pltpu.SemaphoreType.DMA((N_DEV,)) for _ in range(2 * n)]
    out = pl.pallas_call(
        body, name=name,
        out_shape=tuple(pltpu.HBM(a.shape, a.dtype) for a in lands) + tuple(sems),
        in_specs=[HBM] * n + [SEM] * n + [ANY],
        out_specs=tuple([HBM] * n + [SEM] * (2 * n)),
        input_output_aliases={i: i for i in range(n)},
        compiler_params=pltpu.CompilerParams(has_side_effects=EFFECT),
    )(*lands, *[h[3] for h in handles], after)
    return [(h[0], out[i], h[2], h[3], out[n + i], out[2 * n + i]) for i, h in enumerate(handles)]


def _gather_wait(handles, after, name):
    n = len(handles)

    def body(*refs):
        src_refs, land_refs = refs[:n], refs[n:2 * n]
        send_refs, recv_refs = refs[2 * n:3 * n], refs[3 * n:4 * n]
        send2_refs, recv2_refs = refs[4 * n:5 * n], refs[5 * n:6 * n]
        me, peers = _peers()
        _, sibling, sibling_flat = peers[SIBLING - 1]
        for i in range(n):
            for k, peer, peer_flat in peers:
                if k in FIRST_LEVEL:
                    cp = _exchange_copy(True, src_refs[i], land_refs[i], send_refs[i], recv_refs[i],
                                        me, k, peer, peer_flat, peer_flat)
                    cp.wait_send()
                    if k == SIBLING:
                        cp.wait_recv()
                if k in SAME_CORE:
                    mine = land_refs[i].at[peer_flat]
                    theirs = land_refs[i].at[peer_flat ^ SIBLING]
                    cp = pltpu.make_async_remote_copy(
                        src_ref=mine, dst_ref=theirs, send_sem=send2_refs[i].at[k], recv_sem=recv2_refs[i].at[k],
                        device_id=sibling, device_id_type=MESH)
                    cp.wait_send()
                    cp.wait_recv()

    srcs = [h[0] for h in handles]
    lands = [h[1] for h in handles]
    out = pl.pallas_call(
        body, name=name,
        out_shape=tuple(pltpu.HBM(a.shape, a.dtype) for a in srcs + lands),
        in_specs=[HBM] * (2 * n) + [SEM] * (4 * n) + [ANY],
        out_specs=tuple([HBM] * (2 * n)),
        input_output_aliases={i: i for i in range(2 * n)},
        compiler_params=pltpu.CompilerParams(has_side_effects=EFFECT),
    )(*srcs, *lands, *[h[2] for h in handles], *[h[3] for h in handles],
      *[h[4] for h in handles], *[h[5] for h in handles], after)
    me = 4 * lax.axis_index("x") + 2 * lax.axis_index("y") + lax.axis_index("c")
    return [lax.dynamic_update_slice_in_dim(land, src[None], me, axis=0) for src, land in zip(out[:n], out[n:])]


def _allreduce_small(p):
    rows, d = p.shape

    def body(p_ref, o_ref, recv_ref, send_sems, recv_sems):
        me, peers = _peers()
        recv_ref[me] = p_ref[...]
        sends = []
        for k, peer, peer_flat in peers:
            cp = pltpu.make_async_remote_copy(
                src_ref=p_ref, dst_ref=recv_ref.at[me],
                send_sem=send_sems.at[k], recv_sem=recv_sems.at[k],
                device_id=peer, device_id_type=MESH)
            cp.start()
            sends.append(cp)
        for k, peer, peer_flat in peers:
            pltpu.make_async_remote_copy(
                src_ref=p_ref, dst_ref=recv_ref.at[peer_flat],
                send_sem=send_sems.at[k], recv_sem=recv_sems.at[k],
                device_id=peer, device_id_type=MESH).wait_recv()
        for cp in sends:
            cp.wait_send()
        acc = recv_ref[0]
        for s in range(1, N_DEV):
            acc = acc + recv_ref[s]
        is_loss = lax.broadcasted_iota(jnp.int32, (rows, d), 0) == rows - 1
        total = jnp.sum(jnp.where(is_loss, acc, 0.0))
        o_ref[...] = jnp.where(is_loss, total, acc)

    return pl.pallas_call(
        body, name="allreduce_small",
        out_shape=jax.ShapeDtypeStruct((rows, d), F32),
        in_specs=[pl.BlockSpec(memory_space=pltpu.VMEM)],
        out_specs=pl.BlockSpec(memory_space=pltpu.VMEM),
        scratch_shapes=[pltpu.VMEM((N_DEV, rows, d), F32),
                        pltpu.SemaphoreType.DMA((N_DEV,)), pltpu.SemaphoreType.DMA((N_DEV,))],
    )(p)


def _adam_math(w, g, m, v):
    m2 = ADAM_B1 * m + (1.0 - ADAM_B1) * g
    v2 = ADAM_B2 * v + (1.0 - ADAM_B2) * (g * g)
    m_hat = m2 / (1.0 - ADAM_B1 ** ADAM_STEP)
    v_hat = v2 / (1.0 - ADAM_B2 ** ADAM_STEP)
    delta = -ADAM_LR * (m_hat / (jnp.sqrt(v_hat) + ADAM_EPS) + ADAM_WD * w)
    return delta, m2, v2


def _adam_from_partials(parts, w, m, v, name):
    r, c = w.shape
    tr = _tile(r, 256, 16)

    def body(p_ref, w_ref, m_ref, v_ref, g_out, d_out, m_out, v_out):
        g = p_ref[0].astype(F32)
        for s in range(1, N_DEV):
            g = g + p_ref[s].astype(F32)
        delta, m2, v2 = _adam_math(w_ref[...], g, m_ref[...], v_ref[...])
        g_out[...] = g
        d_out[...] = delta
        m_out[...] = m2
        v_out[...] = v2

    blk = pl.BlockSpec((tr, c), lambda i: (i, 0))
    out = jax.ShapeDtypeStruct((r, c), F32)
    return pl.pallas_call(
        body, name=name, grid=(r // tr,),
        in_specs=[pl.BlockSpec((N_DEV, tr, c), lambda i: (0, i, 0)), blk, blk, blk],
        out_specs=[blk, blk, blk, blk], out_shape=[out, out, out, out],
        compiler_params=_params(),
    )(parts, w, m, v)


def _adam_small(g, w, m, v):
    def body(g_ref, w_ref, m_ref, v_ref, d_out, m_out, v_out):
        delta, m2, v2 = _adam_math(w_ref[...], g_ref[...], m_ref[...], v_ref[...])
        d_out[...] = delta
        m_out[...] = m2
        v_out[...] = v2

    out = jax.ShapeDtypeStruct(g.shape, F32)
    return pl.pallas_call(body, name="adam_small", out_shape=[out, out, out])(g, w, m, v)


def _rms_fwd(x, gain, name, dep=None, with_transpose=False):
    t, d = x.shape
    tr = _tile(t, 256, LANE)

    def body(x_ref, g_ref, o_ref, *ot_ref):
        xv = x_ref[...]
        r = lax.rsqrt(jnp.mean(xv * xv, axis=-1, keepdims=True) + NORM_EPS)
        y = xv * r * g_ref[...]
        o_ref[...] = y.astype(BF)
        if with_transpose:
            ot_ref[0][...] = jnp.transpose(y).astype(BF)

    out_specs = [pl.BlockSpec((tr, d), lambda i: (i, 0))]
    out_shape = [jax.ShapeDtypeStruct((t, d), BF)]
    if with_transpose:
        out_specs.append(pl.BlockSpec((d, tr), lambda i: (0, i)))
        out_shape.append(jax.ShapeDtypeStruct((d, t), BF))
    return _call(
        body, [x, gain], dep=dep, name=name, grid=(t // tr,),
        in_specs=[pl.BlockSpec((tr, d), lambda i: (i, 0)), pl.BlockSpec((1, d), lambda i: (0, 0))],
        out_specs=out_specs, out_shape=out_shape, compiler_params=_params(),
    )


def _rms_vjp(xv, gain, dy):
    r = lax.rsqrt(jnp.mean(xv * xv, axis=-1, keepdims=True) + NORM_EPS)
    xhat = xv * r
    dxhat = dy * gain
    dx = r * (dxhat - xhat * jnp.mean(dxhat * xhat, axis=-1, keepdims=True))
    dgain = jnp.sum(dy * xhat, axis=0, keepdims=True)
    return dx, dgain


def _loss_head(x, gain, target):
    t, d = x.shape
    tr = _tile(t, 256, 16)

    def body(x_ref, g_ref, t_ref, dx_ref, dxb_ref, dg_ref, loss_ref):
        xv = x_ref[...]
        gain = g_ref[...]
        r = lax.rsqrt(jnp.mean(xv * xv, axis=-1, keepdims=True) + NORM_EPS)
        err = xv * r * gain - t_ref[...]
        dx, dgain = _rms_vjp(xv, gain, err * (1.0 / d))
        dx_ref[...] = dx
        dxb_ref[...] = dx.astype(BF)

        @pl.when(pl.program_id(0) == 0)
        def _():
            dg_ref[...] = jnp.zeros_like(dg_ref)
            loss_ref[...] = jnp.zeros_like(loss_ref)

        dg_ref[...] += dgain
        loss_ref[...] += jnp.sum(err * err, axis=0, keepdims=True) * (0.5 / d)

    row = pl.BlockSpec((tr, d), lambda i: (i, 0))
    vec = pl.BlockSpec((1, d), lambda i: (0, 0))
    return pl.pallas_call(
        body, name="loss_head", grid=(t // tr,),
        in_specs=[row, vec, row], out_specs=[row, row, vec, vec],
        out_shape=[jax.ShapeDtypeStruct((t, d), F32), jax.ShapeDtypeStruct((t, d), BF),
                   jax.ShapeDtypeStruct((1, d), F32), jax.ShapeDtypeStruct((1, d), F32)],
        compiler_params=_params(),
    )(x, gain, target)


def _mm_nn(a, b, out_dtype, name, residual=None, tm_pref=512, tn_pref=1152):
    m, k = a.shape
    n = b.shape[1]
    tm, tn = _tile(m, tm_pref, 16), _tile(n, tn_pref, LANE)

    def body(*refs):
        if residual is None:
            a_ref, b_ref, o_ref = refs
            o_ref[...] = _dot(a_ref[...], b_ref[...]).astype(out_dtype)
        else:
            a_ref, b_ref, r_ref, o_ref = refs
            o_ref[...] = (r_ref[...] + _dot(a_ref[...], b_ref[...])).astype(out_dtype)

    in_specs = [pl.BlockSpec((tm, k), lambda j, i: (i, 0)), pl.BlockSpec((k, tn), lambda j, i: (0, j))]
    args = [a, b]
    if residual is not None:
        in_specs.append(pl.BlockSpec((tm, tn), lambda j, i: (i, j)))
        args.append(residual)
    return pl.pallas_call(
        body, name=name, grid=(n // tn, m // tm), in_specs=in_specs,
        out_specs=pl.BlockSpec((tm, tn), lambda j, i: (i, j)),
        out_shape=jax.ShapeDtypeStruct((m, n), out_dtype), compiler_params=_params(),
    )(*args)


def _rms_bwd_tail(dy_ref, first, x_ref, g_ref, dres_ref, dx_ref, dxb_ref, dg_ref):
    @pl.when(first)
    def _():
        dg_ref[...] = jnp.zeros_like(dg_ref)

    gain = g_ref[...]
    for r in range(0, dy_ref.shape[0], LANE):
        rows = pl.ds(r, min(LANE, dy_ref.shape[0] - r))
        dx, dgain = _rms_vjp(x_ref[rows, :], gain, dy_ref[rows, :])
        dx = dx + dres_ref[rows, :]
        dx_ref[rows, :] = dx
        dxb_ref[rows, :] = dx.astype(BF)
        dg_ref[...] += dgain


def _mm_nt(a, b, out_dtype, name, tm_pref=512, tn_pref=1024, tk_pref=2048, rms=None, dep=None):
    m, k = a.shape
    n = b.shape[0]
    tm, tn, tk = _tile(m, tm_pref, 16), _tile(n, tn_pref, LANE), _tile(k, tk_pref, LANE)
    nk = k // tk
    assert rms is None or tn == n

    def body(*refs):
        if rms is None:
            a_ref, b_ref, o_ref, acc_ref = refs
        else:
            a_ref, b_ref, x_ref, g_ref, dres_ref, dx_ref, dxb_ref, dg_ref, acc_ref = refs
        kk = pl.program_id(2)

        @pl.when(kk == 0)
        def _():
            acc_ref[...] = jnp.zeros_like(acc_ref)

        acc_ref[...] += _dot(a_ref[...], b_ref[...], NT)

        @pl.when(kk == nk - 1)
        def _():
            if rms is None:
                o_ref[...] = acc_ref[...].astype(out_dtype)
            else:
                _rms_bwd_tail(acc_ref, pl.program_id(1) == 0, x_ref, g_ref, dres_ref, dx_ref, dxb_ref, dg_ref)

    in_specs = [pl.BlockSpec((tm, tk), lambda j, i, kk: (i, kk)), pl.BlockSpec((tn, tk), lambda j, i, kk: (j, kk))]
    row = pl.BlockSpec((tm, tn), lambda j, i, kk: (i, j))
    if rms is None:
        args, out_specs, out_shape = [a, b], row, jax.ShapeDtypeStruct((m, n), out_dtype)
    else:
        vec = pl.BlockSpec((1, n), lambda j, i, kk: (0, 0))
        args, in_specs = [a, b, *rms], in_specs + [row, vec, row]
        out_specs = [row, row, vec]
        out_shape = [jax.ShapeDtypeStruct((m, n), F32), jax.ShapeDtypeStruct((m, n), BF),
                     jax.ShapeDtypeStruct((1, n), F32)]
    return _call(
        body, args, dep=dep, name=name, grid=(n // tn, m // tm, nk), in_specs=in_specs, out_specs=out_specs,
        out_shape=out_shape, scratch_shapes=[pltpu.VMEM((tm, tn), F32)], compiler_params=_params(),
    )


def _mm_tn(a, b, out_dtype, name, tn_pref=1152, tk_pref=512, a_transposed=False):
    (k, t) = a.shape if a_transposed else a.shape[::-1]
    n = b.shape[1]
    tn, tk = _tile(n, tn_pref, LANE), _tile(t, tk_pref, LANE if a_transposed else 16)
    nt = t // tk

    def body(a_ref, b_ref, o_ref, acc_ref):
        tt = pl.program_id(1)

        @pl.when(tt == 0)
        def _():
            acc_ref[...] = jnp.zeros_like(acc_ref)

        acc_ref[...] += _dot(a_ref[...], b_ref[...], NN if a_transposed else TN)

        @pl.when(tt == nt - 1)
        def _():
            o_ref[...] = acc_ref[...].astype(out_dtype)

    if a_transposed:
        a_spec = pl.BlockSpec((k, tk), lambda j, tt: (0, tt))
    else:
        a_spec = pl.BlockSpec((tk, k), lambda j, tt: (tt, 0))
    return pl.pallas_call(
        body, name=name, grid=(n // tn, nt),
        in_specs=[a_spec, pl.BlockSpec((tk, tn), lambda j, tt: (tt, j))],
        out_specs=pl.BlockSpec((k, tn), lambda j, tt: (0, j)),
        out_shape=jax.ShapeDtypeStruct((k, n), out_dtype),
        scratch_shapes=[pltpu.VMEM((k, tn), F32)], compiler_params=_params(),
    )(a, b)


FFN_COLS = 512


FFN_ROWS_WIDE = 2048


def _ffn_tiles(t, fc, rows=FFN_ROWS):
    return _tile(t, rows, 16), _tile(fc, FFN_COLS, LANE)


def _slabs(tm, rows=256):
    step = rows if tm % rows == 0 else tm
    return [pl.ds(r, step) for r in range(0, tm, step)]


def _ffn_gate_up(hn, wg_t, wu_t, name):
    t, d = hn.shape
    fc = wg_t.shape[0]
    tm, tn = _ffn_tiles(t, fc, FFN_ROWS_WIDE)

    def body(h_ref, wg_ref, wu_ref, g_ref, u_ref, a_ref):
        for rows in _slabs(tm):
            h = h_ref[rows, :]
            g = _dot(h, wg_ref[...], NT)
            u = _dot(h, wu_ref[...], NT)
            g_ref[rows, :] = g.astype(BF)
            u_ref[rows, :] = u.astype(BF)
            a_ref[rows, :] = (g * _sig(g) * u).astype(BF)

    wspec = pl.BlockSpec((tn, d), lambda j, i: (j, 0))
    hid = pl.BlockSpec((tm, tn), lambda j, i: (i, j))
    out = jax.ShapeDtypeStruct((t, fc), BF)
    return pl.pallas_call(
        body, name=name, grid=(fc // tn, t // tm),
        in_specs=[pl.BlockSpec((tm, d), lambda j, i: (i, 0)), wspec, wspec],
        out_specs=[hid, hid, hid], out_shape=[out, out, out], compiler_params=_params(),
    )(hn, wg_t, wu_t)


def _ffn_gate(hn, wg_t, name):
    t, d = hn.shape
    fc = wg_t.shape[0]
    tm, tn = _ffn_tiles(t, fc)

    def body(h_ref, wg_ref, g_ref):
        g_ref[...] = _dot(h_ref[...], wg_ref[...], NT)

    return pl.pallas_call(
        body, name=name, grid=(fc // tn, t // tm),
        in_specs=[pl.BlockSpec((tm, d), lambda j, i: (i, 0)), pl.BlockSpec((tn, d), lambda j, i: (j, 0))],
        out_specs=pl.BlockSpec((tm, tn), lambda j, i: (i, j)),
        out_shape=jax.ShapeDtypeStruct((t, fc), F32), compiler_params=_params(),
    )(hn, wg_t)


def _ffn_up_act(hn, wu_t, g, name):
    t, d = hn.shape
    fc = wu_t.shape[0]
    tm, tn = _ffn_tiles(t, fc, FFN_ROWS_WIDE)

    def body(h_ref, wu_ref, g_ref, gb_ref, u_ref, a_ref):
        for rows in _slabs(tm):
            u = _dot(h_ref[rows, :], wu_ref[...], NT)
            gv = g_ref[rows, :]
            gb_ref[rows, :] = gv.astype(BF)
            u_ref[rows, :] = u.astype(BF)
            a_ref[rows, :] = (gv * _sig(gv) * u).astype(BF)

    hid = pl.BlockSpec((tm, tn), lambda j, i: (i, j))
    out = jax.ShapeDtypeStruct((t, fc), BF)
    return pl.pallas_call(
        body, name=name, grid=(fc // tn, t // tm),
        in_specs=[pl.BlockSpec((tm, d), lambda j, i: (i, 0)), pl.BlockSpec((tn, d), lambda j, i: (j, 0)), hid],
        out_specs=[hid, hid, hid], out_shape=[out, out, out], compiler_params=_params(),
    )(hn, wu_t, g)


def _ffn_down(act, wd, xres, name):
    t, fc = act.shape
    d = wd.shape[1]
    tm, tk = _ffn_tiles(t, fc)

    def body(a_ref, w_ref, x_ref, o_ref):
        @pl.when(pl.program_id(1) == 0)
        def _():
            o_ref[...] = x_ref[...]

        o_ref[...] += 0.5 * _dot(a_ref[...], w_ref[...])

    row = pl.BlockSpec((tm, d), lambda i, k: (i, 0))
    return pl.pallas_call(
        body, name=name, grid=(t // tm, fc // tk),
        in_specs=[pl.BlockSpec((tm, tk), lambda i, k: (i, k)), pl.BlockSpec((tk, d), lambda i, k: (k, 0)), row],
        out_specs=row, out_shape=jax.ShapeDtypeStruct((t, d), F32), compiler_params=_params(),
    )(act, wd, xres)


def _ffn_bwd_hidden(dxb, wd, g, u, name):
    t, d = dxb.shape
    fc = wd.shape[0]
    tm, tn = _ffn_tiles(t, fc, FFN_ROWS_WIDE)

    def body(dx_ref, w_ref, g_ref, u_ref, dg_ref, du_ref):
        for rows in _slabs(tm):
            dh = 0.5 * _dot(dx_ref[rows, :], w_ref[...], NT)
            gv = g_ref[rows, :].astype(F32)
            uv = u_ref[rows, :].astype(F32)
            s = _sig(gv)
            dg_ref[rows, :] = (dh * uv * (s * (1.0 + gv * (1.0 - s)))).astype(BF)
            du_ref[rows, :] = (dh * (gv * s)).astype(BF)

    hid = pl.BlockSpec((tm, tn), lambda i, j: (i, j))
    out = jax.ShapeDtypeStruct((t, fc), BF)
    return pl.pallas_call(
        body, name=name, grid=(t // tm, fc // tn),
        in_specs=[pl.BlockSpec((tm, d), lambda i, j: (i, 0)), pl.BlockSpec((tn, d), lambda i, j: (j, 0)), hid, hid],
        out_specs=[hid, hid], out_shape=[out, out], compiler_params=_params(),
    )(dxb, wd, g, u)


def _ffn_dw(lhs, rhs, scale, name, dep=None):
    n = len(lhs)
    t, fc = lhs[0].shape
    d = rhs.shape[1]
    tk, tn = _tile(t, DW_ROWS, 16), _tile(fc, FFN_COLS, LANE)
    nt = t // tk

    def body(*refs):
        l_refs, r_ref, o_refs, acc_refs = refs[:n], refs[n], refs[n + 1:2 * n + 1], refs[2 * n + 1:]
        tt = pl.program_id(1)
        r = r_ref[...]
        for l_ref, o_ref, acc_ref in zip(l_refs, o_refs, acc_refs):
            @pl.when(tt == 0)
            def _():
                acc_ref[...] = jnp.zeros_like(acc_ref)

            acc_ref[...] += _dot(l_ref[...], r, TN)

            @pl.when(tt == nt - 1)
            def _():
                o_ref[...] = (scale * acc_ref[...]).astype(BF)

    lspec = pl.BlockSpec((tk, tn), lambda j, tt: (tt, j))
    ospec = pl.BlockSpec((tn, d), lambda j, tt: (j, 0))
    out = jax.ShapeDtypeStruct((fc, d), BF)
    return _call(
        body, [*lhs, rhs], dep=dep, name=name, grid=(fc // tn, nt),
        in_specs=[lspec] * n + [pl.BlockSpec((tk, d), lambda j, tt: (tt, 0))],
        out_specs=[ospec] * n, out_shape=[out] * n,
        scratch_shapes=[pltpu.VMEM((tn, d), F32)] * n, compiler_params=_params(),
    )


def _rms_bwd(dy, x, gain, dres, name):
    t, d = x.shape
    tr = _tile(t, 256, 16)

    def body(dy_ref, x_ref, g_ref, dres_ref, dx_ref, dxb_ref, dg_ref):
        _rms_bwd_tail(dy_ref, pl.program_id(0) == 0, x_ref, g_ref, dres_ref, dx_ref, dxb_ref, dg_ref)

    row = pl.BlockSpec((tr, d), lambda i: (i, 0))
    vec = pl.BlockSpec((1, d), lambda i: (0, 0))
    return pl.pallas_call(
        body, name=name, grid=(t // tr,),
        in_specs=[row, row, vec, row], out_specs=[row, row, vec],
        out_shape=[jax.ShapeDtypeStruct((t, d), F32), jax.ShapeDtypeStruct((t, d), BF),
                   jax.ShapeDtypeStruct((1, d), F32)],
        compiler_params=_params(),
    )(dy, x, gain, dres)


def _ffn_bwd_input(dg, du, wg_t, wu_t, name, dep=None):
    t, fc = dg.shape
    d = wg_t.shape[1]
    tm, tk = _ffn_tiles(t, fc)

    def body(dg_ref, du_ref, wg_ref, wu_ref, o_ref):
        @pl.when(pl.program_id(1) == 0)
        def _():
            o_ref[...] = jnp.zeros_like(o_ref)

        o_ref[...] += _dot(dg_ref[...], wg_ref[...]) + _dot(du_ref[...], wu_ref[...])

    hid = pl.BlockSpec((tm, tk), lambda i, k: (i, k))
    wspec = pl.BlockSpec((tk, d), lambda i, k: (k, 0))
    return _call(
        body, [dg, du, wg_t, wu_t], dep=dep, name=name, grid=(t // tm, fc // tk),
        in_specs=[hid, hid, wspec, wspec],
        out_specs=pl.BlockSpec((tm, d), lambda i, k: (i, 0)),
        out_shape=jax.ShapeDtypeStruct((t, d), F32), compiler_params=_params(),
    )


def _rope_tables(t):
    pos = jnp.arange(t, dtype=F32)
    inv_freq = ROPE_THETA ** (-jnp.arange(0, ROPE_DIM, 2, dtype=F32) / ROPE_DIM)
    ang = pos[:, None] * inv_freq[None, :]
    cos, sin = jnp.cos(ang), jnp.sin(ang)
    rest = HEAD_DIM - ROPE_DIM
    one = jnp.ones((t, rest), F32)
    zero_h = jnp.zeros((t, ROPE_HALF), F32)
    zero_r = jnp.zeros((t, rest), F32)
    c = jnp.concatenate([cos, cos, one], axis=1)
    s1 = jnp.concatenate([-sin, zero_h, zero_r], axis=1)
    s2 = jnp.concatenate([zero_h, sin, zero_r], axis=1)
    return c, s1, s2


def _rope(xh, c, s1, s2):
    return xh * c + pltpu.roll(xh, HEAD_DIM - ROPE_HALF, 1) * s1 + pltpu.roll(xh, ROPE_HALF, 1) * s2


def _rope_t(dh, c, s1, s2):
    return dh * c + pltpu.roll(dh * s1, ROPE_HALF, 1) + pltpu.roll(dh * s2, HEAD_DIM - ROPE_HALF, 1)


def _mixer_prep(proj, tables, bf_pad, hd, scale):
    t, np_ = proj.shape
    tr = _tile(t, 256, 16)
    nh = hd // HEAD_DIM
    nblk = hd // LANE
    f_blk = np_ // LANE - 1

    def body(qd_ref, kd_ref, vd_ref, qf_ref, kf_ref, vf_ref, fl_ref, c_ref, s1_ref, s2_ref, b_ref,
             oqd, okd, ovd, oqf, okf, ovf, olog):
        c, s1, s2 = c_ref[...], s1_ref[...], s2_ref[...]
        for h in range(nh):
            sl = slice(h * HEAD_DIM, (h + 1) * HEAD_DIM)
            oqd[:, sl] = (_rope(qd_ref[:, sl], c, s1, s2) * scale).astype(BF)
            okd[:, sl] = _rope(kd_ref[:, sl], c, s1, s2).astype(BF)
        ovd[...] = vd_ref[...].astype(BF)
        oqf[...] = (qf_ref[...] * scale).astype(BF)
        okf[...] = kf_ref[...].astype(BF)
        ovf[...] = vf_ref[...].astype(BF)
        z = fl_ref[...] + b_ref[...]
        olog[...] = jnp.minimum(z, 0.0) - jnp.log(1.0 + jnp.exp(-jnp.abs(z)))

    def col(kblk):
        return pl.BlockSpec((tr, hd), lambda i, kblk=kblk: (i, kblk))

    lane_row = pl.BlockSpec((tr, LANE), lambda i: (i, 0))
    in_specs = [col(0), col(1), col(2), col(3), col(4), col(5),
                pl.BlockSpec((tr, LANE), lambda i: (i, f_blk)),
                lane_row, lane_row, lane_row, pl.BlockSpec((1, LANE), lambda i: (0, 0))]
    o = pl.BlockSpec((tr, hd), lambda i: (i, 0))
    ob = jax.ShapeDtypeStruct((t, hd), BF)
    del nblk
    return pl.pallas_call(
        body, name="mixer_prep", grid=(t // tr,), in_specs=in_specs,
        out_specs=[o, o, o, o, o, o, lane_row],
        out_shape=[ob, ob, ob, ob, ob, ob, jax.ShapeDtypeStruct((t, LANE), F32)],
        compiler_params=_params(),
    )(proj, proj, proj, proj, proj, proj, proj, *tables, bf_pad)


def _split3(x):
    x1 = x.astype(BF)
    r1 = x - x1.astype(F32)
    x2 = r1.astype(BF)
    x3 = (r1 - x2.astype(F32)).astype(BF)
    return x1, x2, x3


def _cumsum_rows(x, reverse, name):
    t, w = x.shape
    blk = LANE
    nb = t // blk

    def body(x_ref, o_ref):
        r = lax.broadcasted_iota(jnp.int32, (blk, blk), 0)
        c = lax.broadcasted_iota(jnp.int32, (blk, blk), 1)
        tri = jnp.where((c >= r) if reverse else (c <= r), 1.0, 0.0).astype(BF)

        def step(i, carry):
            b = (nb - 1 - i) if reverse else i
            off = pl.multiple_of(b * blk, blk)
            xb = x_ref[pl.ds(off, blk), :]
            x1, x2, x3 = _split3(xb)
            o_ref[pl.ds(off, blk), :] = _dot(tri, x1) + _dot(tri, x2) + _dot(tri, x3) + carry
            return carry + jnp.sum(xb, axis=0, keepdims=True)

        lax.fori_loop(0, nb, step, jnp.zeros((1, w), F32))

    return pl.pallas_call(body, name=name, out_shape=jax.ShapeDtypeStruct((t, w), F32),
                          compiler_params=_params())(x)


ATTN_ROWS = 16


def _dil_bias_tiles(tq):
    nbias = MAX_WINDOW // tq + 1
    b = lax.broadcasted_iota(jnp.int32, (nbias, tq, tq), 0)
    i = lax.broadcasted_iota(jnp.int32, (nbias, tq, tq), 1)
    j = lax.broadcasted_iota(jnp.int32, (nbias, tq, tq), 2)
    delta = b * tq + i - j
    mult = jnp.zeros((nbias, tq, tq), F32)
    for w, dil in DIL_PATTERNS:
        mult = mult + jnp.where((delta >= 0) & (delta <= w) & (delta % dil == 0), 1.0, 0.0)
    return jnp.where(mult > 0.0, jnp.log(jnp.maximum(mult, 1.0)), NEG)


def _rep(x, width):
    return jnp.tile(x, (1, width // LANE))


def _chunks(n_rows, fn):
    for c in range(n_rows // ATTN_ROWS):
        fn(c * ATTN_ROWS)


def _causal(r0, tq, transposed):
    a = lax.broadcasted_iota(jnp.int32, (ATTN_ROWS, tq), 0) + r0
    b = lax.broadcasted_iota(jnp.int32, (ATTN_ROWS, tq), 1)
    return (a <= b) if transposed else (b <= a)


def _rows8(x):
    return jnp.transpose(x)[:8, :]


def _attn_fwd(mode, q, k, v, bias, tq, name):
    t, hd = q.shape
    nh = hd // HEAD_DIM
    nb = t // tq
    wb = MAX_WINDOW // tq
    fox = mode == "fox"

    def body(q_ref, k_ref, v_ref, b_ref, o_ref, lse_ref, lse_row_ref, s_ref, p_ref, m_ref, l_ref, acc_ref):
        qi = pl.program_id(1)
        qb = q_ref[...]
        m_ref[...] = jnp.full_like(m_ref, NEG)
        l_ref[...] = jnp.zeros_like(l_ref)
        acc_ref[...] = jnp.zeros_like(acc_ref)

        def tile(kj, diag):
            off = pl.multiple_of(kj * tq, tq)
            s_ref[...] = _dot(qb, k_ref[pl.ds(off, tq), :], NT)
            if fox:
                brow = b_ref[qi][:, :1] - b_ref[kj]

            def chunk(r0):
                rows = pl.ds(r0, ATTN_ROWS)
                if fox:
                    s = s_ref[rows, :] + brow
                    if diag:
                        s = jnp.where(_causal(r0, tq, False), s, NEG)
                else:
                    s = s_ref[rows, :] + b_ref[qi - kj, rows, :]
                m_old = m_ref[rows, :]
                m_new = jnp.maximum(m_old, jnp.max(s, axis=1, keepdims=True))
                p = jnp.exp(s - _rep(m_new, tq))
                alpha = jnp.exp(m_old - m_new)
                l_ref[rows, :] = alpha * l_ref[rows, :] + jnp.sum(p, axis=1, keepdims=True)
                m_ref[rows, :] = m_new
                acc_ref[rows, :] = alpha * acc_ref[rows, :]
                p_ref[rows, :] = p.astype(BF)

            _chunks(tq, chunk)
            acc_ref[...] += _dot(p_ref[...], v_ref[pl.ds(off, tq), :])

        tile(qi, True)
        if fox:
            lax.fori_loop(0, qi, lambda kj, c: (tile(kj, False), c)[1], 0)
        else:
            lax.fori_loop(1, jnp.minimum(qi, wb) + 1, lambda i, c: (tile(qi - i, False), c)[1], 0)
        o_ref[...] = (acc_ref[...] / l_ref[...]).astype(BF)
        lse = m_ref[...] + jnp.log(l_ref[...])
        lse_ref[...] = lse
        lse_row_ref[...] = _rows8(lse)

    qspec = pl.BlockSpec((tq, HEAD_DIM), lambda h, i: (i, h))
    kvspec = pl.BlockSpec((t, HEAD_DIM), lambda h, i: (0, h))
    repspec = pl.BlockSpec((None, tq, LANE), lambda h, i: (h, i, 0))
    row8spec = pl.BlockSpec((None, None, 8, tq), lambda h, i: (h, i, 0, 0))
    if fox:
        bspec = pl.BlockSpec((None, nb, 1, tq), lambda h, i: (h, 0, 0, 0))
    else:
        bspec = pl.BlockSpec((wb + 1, tq, tq), lambda h, i: (0, 0, 0))
    return pl.pallas_call(
        body, name=name, grid=(nh, nb), in_specs=[qspec, kvspec, kvspec, bspec],
        out_specs=[qspec, repspec, row8spec],
        out_shape=[jax.ShapeDtypeStruct((t, hd), BF), jax.ShapeDtypeStruct((nh, t, LANE), F32),
                   jax.ShapeDtypeStruct((nh, nb, 8, tq), F32)],
        scratch_shapes=[pltpu.VMEM((tq, tq), F32), pltpu.VMEM((tq, tq), BF), pltpu.VMEM((tq, LANE), F32),
                        pltpu.VMEM((tq, LANE), F32), pltpu.VMEM((tq, HEAD_DIM), F32)],
        compiler_params=_params(),
    )(q, k, v, bias)


def _attn_bwd_dq(mode, q, k, v, o, do, lse, bias, tq, name, dep=None):
    t, hd = q.shape
    nh = hd // HEAD_DIM
    nb = t // tq
    wb = MAX_WINDOW // tq
    fox = mode == "fox"

    def body(q_ref, k_ref, v_ref, o_ref, do_ref, lse_ref, b_ref, dq_ref, dl_row_ref,
             s_ref, dp_ref, x_ref, y_ref, acc_ref, acc2_ref, dl_ref):
        qi = pl.program_id(1)
        qb = q_ref[...]
        dob = do_ref[...]
        acc_ref[...] = jnp.zeros_like(acc_ref)
        if fox:
            acc2_ref[...] = jnp.zeros_like(acc2_ref)
            dl_ref[...] = jnp.zeros_like(dl_ref)
        else:
            prod = o_ref[...].astype(F32) * dob.astype(F32)
            dl_ref[...] = jnp.broadcast_to(jnp.sum(prod, axis=1, keepdims=True), (tq, LANE))

        def tile(kj, diag):
            off = pl.multiple_of(kj * tq, tq)
            kb = k_ref[pl.ds(off, tq), :]
            s_ref[...] = _dot(qb, kb, NT)
            dp_ref[...] = _dot(dob, v_ref[pl.ds(off, tq), :], NT)
            if fox:
                brow = b_ref[qi][:, :1] - b_ref[kj]

            def chunk(r0):
                rows = pl.ds(r0, ATTN_ROWS)
                lse_c = _rep(lse_ref[rows, :], tq)
                if fox:
                    s = s_ref[rows, :] + brow
                    if diag:
                        s = jnp.where(_causal(r0, tq, False), s, NEG)
                    p = jnp.exp(s - lse_c)
                    pdp = p * dp_ref[rows, :]
                    dl_ref[rows, :] += jnp.sum(pdp, axis=1, keepdims=True)
                    x_ref[rows, :] = pdp.astype(BF)
                    y_ref[rows, :] = p.astype(BF)
                else:
                    p = jnp.exp(s_ref[rows, :] + b_ref[qi - kj, rows, :] - lse_c)
                    x_ref[rows, :] = (p * (dp_ref[rows, :] - _rep(dl_ref[rows, :], tq))).astype(BF)

            _chunks(tq, chunk)
            acc_ref[...] += _dot(x_ref[...], kb)
            if fox:
                acc2_ref[...] += _dot(y_ref[...], kb)

        tile(qi, True)
        if fox:
            lax.fori_loop(0, qi, lambda kj, c: (tile(kj, False), c)[1], 0)
            dq_ref[...] = acc_ref[...] - dl_ref[...] * acc2_ref[...]
        else:
            lax.fori_loop(1, jnp.minimum(qi, wb) + 1, lambda i, c: (tile(qi - i, False), c)[1], 0)
            dq_ref[...] = acc_ref[...]
        dl_row_ref[...] = _rows8(dl_ref[...])

    qspec = pl.BlockSpec((tq, HEAD_DIM), lambda h, i: (i, h))
    kvspec = pl.BlockSpec((t, HEAD_DIM), lambda h, i: (0, h))
    repspec = pl.BlockSpec((None, tq, LANE), lambda h, i: (h, i, 0))
    row8spec = pl.BlockSpec((None, None, 8, tq), lambda h, i: (h, i, 0, 0))
    if fox:
        bspec = pl.BlockSpec((None, nb, 1, tq), lambda h, i: (h, 0, 0, 0))
    else:
        bspec = pl.BlockSpec((wb + 1, tq, tq), lambda h, i: (0, 0, 0))
    return _call(
        body, [q, k, v, o, do, lse, bias], dep=dep, name=name, grid=(nh, nb),
        in_specs=[qspec, kvspec, kvspec, qspec, qspec, repspec, bspec],
        out_specs=[qspec, row8spec],
        out_shape=[jax.ShapeDtypeStruct((t, hd), F32), jax.ShapeDtypeStruct((nh, nb, 8, tq), F32)],
        scratch_shapes=[pltpu.VMEM((tq, tq), F32), pltpu.VMEM((tq, tq), F32), pltpu.VMEM((tq, tq), BF),
                        pltpu.VMEM((tq, tq), BF), pltpu.VMEM((tq, HEAD_DIM), F32),
                        pltpu.VMEM((tq, HEAD_DIM), F32), pltpu.VMEM((tq, LANE), F32)],
        compiler_params=_params(),
    )


def _attn_bwd_dkv(mode, q, k, v, do, lse_row, dl_row, bias_t, c_row, tq, name):
    t, hd = q.shape
    nh = hd // HEAD_DIM
    nb = t // tq
    wb = MAX_WINDOW // tq
    fox = mode == "fox"

    def body(*refs):
        if fox:
            (q_ref, k_ref, v_ref, do_ref, lse_ref, dl_ref, b_ref, cq_ref, dk_ref, dv_ref, dc_row_ref,
             s_ref, dp_ref, x_ref, y_ref, dc_ref) = refs
        else:
            q_ref, k_ref, v_ref, do_ref, lse_ref, dl_ref, b_ref, dk_ref, dv_ref, s_ref, dp_ref, x_ref, y_ref = refs
        kj = pl.program_id(1)
        kb = k_ref[...]
        vb = v_ref[...]
        dk_ref[...] = jnp.zeros_like(dk_ref)
        dv_ref[...] = jnp.zeros_like(dv_ref)
        if fox:
            dc_ref[...] = jnp.zeros_like(dc_ref)

        def tile(qi, diag):
            off = pl.multiple_of(qi * tq, tq)
            qb = q_ref[pl.ds(off, tq), :]
            dob = do_ref[pl.ds(off, tq), :]
            s_ref[...] = _dot(kb, qb, NT)
            dp_ref[...] = _dot(vb, dob, NT)
            lse_r = lse_ref[qi, 0:1, :]
            dl_r = dl_ref[qi, 0:1, :]
            if fox:
                kbias = cq_ref[qi][:, :1] - b_ref[...]

            def chunk(r0):
                rows = pl.ds(r0, ATTN_ROWS)
                if fox:
                    s = s_ref[rows, :] + _rep(kbias[r0:r0 + ATTN_ROWS, :], tq)
                    if diag:
                        s = jnp.where(_causal(r0, tq, True), s, NEG)
                else:
                    s = s_ref[rows, :] + b_ref[qi - kj, rows, :]
                pt = jnp.exp(s - lse_r)
                dst = pt * (dp_ref[rows, :] - dl_r)
                x_ref[rows, :] = pt.astype(BF)
                y_ref[rows, :] = dst.astype(BF)
                if fox:
                    dc_ref[rows, :] -= jnp.sum(dst, axis=1, keepdims=True)

            _chunks(tq, chunk)
            dv_ref[...] += _dot(x_ref[...], dob)
            dk_ref[...] += _dot(y_ref[...], qb)

        tile(kj, True)
        hi = nb if fox else jnp.minimum(kj + wb + 1, nb)
        lax.fori_loop(kj + 1, hi, lambda qi, c: (tile(qi, False), c)[1], 0)
        if fox:
            dc_row_ref[...] = _rows8(dc_ref[...])

    blkspec = pl.BlockSpec((tq, HEAD_DIM), lambda h, j: (j, h))
    fullspec = pl.BlockSpec((t, HEAD_DIM), lambda h, j: (0, h))
    rows8spec = pl.BlockSpec((None, nb, 8, tq), lambda h, j: (h, 0, 0, 0))
    repspec = pl.BlockSpec((None, tq, LANE), lambda h, j: (h, j, 0))
    in_specs = [fullspec, blkspec, blkspec, fullspec, rows8spec, rows8spec]
    args = [q, k, v, do, lse_row, dl_row, bias_t]
    out_specs = [blkspec, blkspec]
    out_shape = [jax.ShapeDtypeStruct((t, hd), F32), jax.ShapeDtypeStruct((t, hd), F32)]
    scratch = [pltpu.VMEM((tq, tq), F32), pltpu.VMEM((tq, tq), F32), pltpu.VMEM((tq, tq), BF),
               pltpu.VMEM((tq, tq), BF)]
    if fox:
        in_specs += [repspec, pl.BlockSpec((None, nb, 1, tq), lambda h, j: (h, 0, 0, 0))]
        args.append(c_row)
        out_specs.append(pl.BlockSpec((None, None, 8, tq), lambda h, j: (h, j, 0, 0)))
        out_shape.append(jax.ShapeDtypeStruct((nh, nb, 8, tq), F32))
        scratch.append(pltpu.VMEM((tq, LANE), F32))
    else:
        in_specs.append(pl.BlockSpec((wb + 1, tq, tq), lambda h, j: (0, 0, 0)))
    return pl.pallas_call(
        body, name=name, grid=(nh, nb), in_specs=in_specs, out_specs=out_specs, out_shape=out_shape,
        scratch_shapes=scratch, compiler_params=_params(),
    )(*args)


def _gate_specs(t, d, hd, tr):
    row = pl.BlockSpec((tr, d), lambda i: (i, 0))
    vec = pl.BlockSpec((1, d), lambda i: (0, 0))
    base = 6 * hd // d
    gd = pl.BlockSpec((tr, d), lambda i: (i, base))
    gf = pl.BlockSpec((tr, d), lambda i: (i, base + 1))
    return row, vec, gd, gf


def _proj_merge(yd, yf, wpd, wpf, proj, b_d, b_f, hd):
    t = yd.shape[0]
    d = wpd.shape[1]
    tr = _tile(t, 256, 16)
    row, vec, gd, gf = _gate_specs(t, d, hd, tr)

    def body(yd_ref, yf_ref, wd_ref, wf_ref, gd_ref, gf_ref, bd_ref, bf_ref, pd_ref, pf_ref, o_ref):
        pd = _dot(yd_ref[...], wd_ref[...])
        pf = _dot(yf_ref[...], wf_ref[...])
        pd_ref[...] = pd
        pf_ref[...] = pf
        o_ref[...] = (_sig(gd_ref[...] + bd_ref[...]) * pd + _sig(gf_ref[...] + bf_ref[...]) * pf).astype(BF)

    yspec = pl.BlockSpec((tr, hd), lambda i: (i, 0))
    wspec = pl.BlockSpec((hd, d), lambda i: (0, 0))
    f32 = jax.ShapeDtypeStruct((t, d), F32)
    return pl.pallas_call(
        body, name="proj_merge", grid=(t // tr,), in_specs=[yspec, yspec, wspec, wspec, gd, gf, vec, vec],
        out_specs=[row, row, row], out_shape=[f32, f32, jax.ShapeDtypeStruct((t, d), BF)],
        compiler_params=_params(),
    )(yd, yf, wpd, wpf, proj, proj, b_d, b_f)


def _merge_bwd(dm, pd, pf, proj, b_d, b_f, hd):
    t, d = pd.shape
    tr = _tile(t, 256, 16)
    row, vec, gd, gf = _gate_specs(t, d, hd, tr)

    def body(dm_ref, pd_ref, pf_ref, gd_ref, gf_ref, bd_ref, bf_ref,
             dpd_ref, dpf_ref, dgd_ref, dgf_ref, dbd_ref, dbf_ref):
        dmv = dm_ref[...]
        sd = _sig(gd_ref[...] + bd_ref[...])
        sf = _sig(gf_ref[...] + bf_ref[...])
        dgd = dmv * pd_ref[...] * (sd * (1.0 - sd))
        dgf = dmv * pf_ref[...] * (sf * (1.0 - sf))
        dpd_ref[...] = (dmv * sd).astype(BF)
        dpf_ref[...] = (dmv * sf).astype(BF)
        dgd_ref[...] = dgd.astype(BF)
        dgf_ref[...] = dgf.astype(BF)

        @pl.when(pl.program_id(0) == 0)
        def _():
            dbd_ref[...] = jnp.zeros_like(dbd_ref)
            dbf_ref[...] = jnp.zeros_like(dbf_ref)

        dbd_ref[...] += jnp.sum(dgd, axis=0, keepdims=True)
        dbf_ref[...] += jnp.sum(dgf, axis=0, keepdims=True)

    ob = jax.ShapeDtypeStruct((t, d), BF)
    ov = jax.ShapeDtypeStruct((1, d), F32)
    return pl.pallas_call(
        body, name="merge_bwd", grid=(t // tr,), in_specs=[row, row, row, gd, gf, vec, vec],
        out_specs=[row, row, row, row, vec, vec], out_shape=[ob, ob, ob, ob, ov, ov],
        compiler_params=_params(),
    )(dm, pd, pf, proj, proj, b_d, b_f)


def _assemble_dproj(dqd, dkd, dvd, dqf, dkf, dvf, dgd, dgf, dlogf, proj, tables, bf_pad, scale):
    t, np_ = proj.shape
    hd = dqd.shape[1]
    d = dgd.shape[1]
    nh = hd // HEAD_DIM
    tr = _tile(t, 256, 16)
    f_blk = np_ // LANE - 1

    def body(dqd_ref, dkd_ref, dvd_ref, dqf_ref, dkf_ref, dvf_ref, dgd_ref, dgf_ref, dlog_ref, fl_ref,
             c_ref, s1_ref, s2_ref, b_ref, o_ref, db_ref):
        c, s1, s2 = c_ref[...], s1_ref[...], s2_ref[...]
        for h in range(nh):
            sl = slice(h * HEAD_DIM, (h + 1) * HEAD_DIM)
            o_ref[:, sl] = (_rope_t(dqd_ref[:, sl], c, s1, s2) * scale).astype(BF)
            o_ref[:, hd + h * HEAD_DIM:hd + (h + 1) * HEAD_DIM] = _rope_t(dkd_ref[:, sl], c, s1, s2).astype(BF)
        o_ref[:, 2 * hd:3 * hd] = dvd_ref[...].astype(BF)
        o_ref[:, 3 * hd:4 * hd] = (dqf_ref[...] * scale).astype(BF)
        o_ref[:, 4 * hd:5 * hd] = dkf_ref[...].astype(BF)
        o_ref[:, 5 * hd:6 * hd] = dvf_ref[...].astype(BF)
        o_ref[:, 6 * hd:6 * hd + d] = dgd_ref[...]
        o_ref[:, 6 * hd + d:6 * hd + 2 * d] = dgf_ref[...]
        z = fl_ref[...] + b_ref[...]
        dfl = dlog_ref[...] * _sig(-z)
        o_ref[:, 6 * hd + 2 * d:] = dfl.astype(BF)

        @pl.when(pl.program_id(0) == 0)
        def _():
            db_ref[...] = jnp.zeros_like(db_ref)

        db_ref[...] += jnp.sum(dfl, axis=0, keepdims=True)

    head = pl.BlockSpec((tr, hd), lambda i: (i, 0))
    row = pl.BlockSpec((tr, d), lambda i: (i, 0))
    lane_row = pl.BlockSpec((tr, LANE), lambda i: (i, 0))
    lane_vec = pl.BlockSpec((1, LANE), lambda i: (0, 0))
    return pl.pallas_call(
        body, name="assemble_dproj", grid=(t // tr,),
        in_specs=[head] * 6 + [row, row, lane_row, pl.BlockSpec((tr, LANE), lambda i: (i, f_blk)),
                               lane_row, lane_row, lane_row, lane_vec],
        out_specs=[pl.BlockSpec((tr, np_), lambda i: (i, 0)), lane_vec],
        out_shape=[jax.ShapeDtypeStruct((t, np_), BF), jax.ShapeDtypeStruct((1, LANE), F32)],
        compiler_params=_params(),
    )(dqd, dkd, dvd, dqf, dkf, dvf, dgd, dgf, dlogf, proj, *tables, bf_pad)


def _to_rows(a, tq):
    h, t = a.shape
    return a.reshape(h, t // tq, 1, tq)


def kernel(x, ffn1_norm, ffn1_w_gate, ffn1_w_up, ffn1_w_down, mix_norm, w_in, b_forget, b_gate_dil, b_gate_fox, w_proj_dil, w_proj_fox, w_out, ffn2_norm, ffn2_w_gate, ffn2_w_up, ffn2_w_down, final_norm, loss_target, m_ffn1_norm, m_ffn1_w_gate, m_ffn1_w_up, m_ffn1_w_down, m_mix_norm, m_w_in, m_b_forget, m_b_gate_dil, m_b_gate_fox, m_w_proj_dil, m_w_proj_fox, m_w_out, m_ffn2_norm, m_ffn2_w_gate, m_ffn2_w_up, m_ffn2_w_down, m_final_norm, v_ffn1_norm, v_ffn1_w_gate, v_ffn1_w_up, v_ffn1_w_down, v_mix_norm, v_w_in, v_b_forget, v_b_gate_dil, v_b_gate_fox, v_w_proj_dil, v_w_proj_fox, v_w_out, v_ffn2_norm, v_ffn2_w_gate, v_ffn2_w_up, v_ffn2_w_down, v_final_norm):
    t, d = x.shape[1], x.shape[2]
    hd = w_proj_dil.shape[1]
    nh = hd // HEAD_DIM
    n_f = b_forget.shape[1]
    cols = w_in.shape[2]
    in_cols = N_DEV * cols
    assert in_cols == 6 * hd + n_f + 2 * d and n_f == nh and n_f <= LANE
    np_ = 6 * hd + 2 * d + LANE
    scale = HEAD_DIM ** -0.5
    tq = _tile(t, 512, LANE)
    assert MAX_WINDOW % tq == 0 and tq % 16 == 0

    x2d = x[0]
    tgt = loss_target[0]

    def rows(w):
        return jnp.swapaxes(w, 1, 2)

    fc = N_DEV * ffn1_w_down.shape[1]
    ag_order = [rows(ffn1_w_gate), rows(ffn1_w_up), ffn1_w_down, w_in, w_proj_dil, w_proj_fox, w_out,
                rows(ffn2_w_gate), rows(ffn2_w_up), ffn2_w_down]
    ag_first, tok = _exchange_start([w[0].astype(BF) for w in ag_order[:2]], True, "ag_start_first", ks=FIRST_LEVEL)
    ag_rest, ag_token = _exchange_start([w[0].astype(BF) for w in ag_order[2:]], True, "ag_start", dep=tok,
                                        ks=FIRST_LEVEL)
    ag = ag_first + ag_rest

    def relay(idx, after, name):
        for i, h in zip(idx, _gather_relay([ag[i] for i in idx], after, name)):
            ag[i] = h

    def gathered(idx, after, name):
        return _gather_wait([ag[i] for i in idx], after, name)

    def ffn_weight(idx, after, name):
        return [w.reshape(fc, d) for w in gathered(idx, after, name)]

    tables = _rope_tables(t)
    bf_pad = jnp.pad(b_forget, ((0, 0), (0, LANE - n_f)))

    hn1, = _rms_fwd(x2d, ffn1_norm, "rms_ffn1", dep=ag_token)
    relay([0], hn1, "ag_relay_ffn1_gate")
    wg1, = ffn_weight([0], hn1, "ag_wait_ffn1_gate")
    g1_f32 = _ffn_gate(hn1, wg1, "ffn1_gate")
    relay([1], g1_f32, "ag_relay_ffn1_up")
    wu1, = ffn_weight([1], g1_f32, "ag_wait_ffn1_up")
    relay([2], wu1, "ag_relay_ffn1_down")
    g1, u1, a1 = _ffn_up_act(hn1, wu1, g1_f32, "ffn1_up_act")
    wd1, = ffn_weight([2], a1, "ag_wait_ffn1_down")
    relay([3], wd1, "ag_relay_w_in")
    x1 = _ffn_down(a1, wd1, x2d, "ffn1_down")

    hm, hm_t = _rms_fwd(x1, mix_norm, "rms_mix", with_transpose=True)
    win_g, = gathered([3], hm, "ag_wait_w_in")
    relay([4, 5, 6], win_g, "ag_relay_mixer")
    segments = [(0, 6 * hd), (6 * hd + n_f, in_cols), (6 * hd, 6 * hd + n_f)]
    pieces = []
    for lo, hi in segments:
        for j in range(lo // cols, (hi - 1) // cols + 1):
            s, e = max(lo, j * cols), min(hi, (j + 1) * cols)
            pieces.append(win_g[j, :, s - j * cols:e - j * cols])
    win_p = jnp.concatenate(pieces + [jnp.zeros((d, LANE - n_f), BF)], axis=1)
    proj = _mm_nn(hm, win_p, F32, "w_in_fwd")
    qd, kd, vd, qf, kf, vf, logf = _mixer_prep(proj, tables, bf_pad, hd, scale)
    csum = _cumsum_rows(logf, False, "cumsum_logf")
    c_heads = csum[:, :nh].T
    c_row = _to_rows(c_heads, tq)
    c_rep = jnp.broadcast_to(c_heads[:, :, None], (nh, t, LANE))
    dil_bias = _dil_bias_tiles(tq)
    dil_bias_t = dil_bias.transpose(0, 2, 1)
    relay([7, 8, 9], qd, "ag_relay_ffn2")
    yd, lse_d, lse_d_row = _attn_fwd("dil", qd, kd, vd, dil_bias, tq, "attn_dil_fwd")
    yf, lse_f, lse_f_row = _attn_fwd("fox", qf, kf, vf, c_row, tq, "attn_fox_fwd")
    wpd_g, wpf_g = gathered([4, 5], yf, "ag_wait_proj")
    wpd = wpd_g.transpose(1, 0, 2).reshape(hd, d)
    wpf = wpf_g.transpose(1, 0, 2).reshape(hd, d)
    pd, pf, merged = _proj_merge(yd, yf, wpd, wpf, proj, b_gate_dil, b_gate_fox, hd)
    wout_g, = gathered([6], merged, "ag_wait_w_out")
    wout = wout_g.reshape(d, d)
    x2 = _mm_nn(merged, wout, F32, "w_out_fwd", residual=x1, tn_pref=1024)

    hn2, = _rms_fwd(x2, ffn2_norm, "rms_ffn2")
    wg2, wu2 = ffn_weight([7, 8], hn2, "ag_wait_ffn2_gate_up")
    g2, u2, a2 = _ffn_gate_up(hn2, wg2, wu2, "ffn2_gate_up")
    wd2, = ffn_weight([9], a2, "ag_wait_ffn2_down")
    x3 = _ffn_down(a2, wd2, x2, "ffn2_down")

    dx3, dx3b, d_final, loss_lanes = _loss_head(x3, final_norm.reshape(1, d), tgt)

    def ffn_bwd(dxb, hn, g, u, a, wg_t, wu_t, wd, x_in, gain, dres, tag):
        def parts(dw):
            return dw.reshape(N_DEV, fc // N_DEV, d)

        dg, du = _ffn_bwd_hidden(dxb, wd, g, u, tag + "_bwd_hidden")
        dwd, = _ffn_dw([a], dxb, 0.5, tag + "_dw_down")
        rs_down, tok = _exchange_start([parts(dwd)], False, "rs_start_" + tag + "_down")
        dwg_t, dwu_t = _ffn_dw([dg, du], hn, 1.0, tag + "_dw_gate_up", dep=tok)
        rs_gu, tok = _exchange_start([parts(dwg_t), parts(dwu_t)], False, "rs_start_" + tag + "_gate_up")
        dhn = _ffn_bwd_input(dg, du, wg_t, wu_t, tag + "_bwd_input", dep=tok)
        dx, dx_bf, dgain = _rms_bwd(dhn, x_in, gain, dres, "rms_" + tag + "_bwd")
        return dx, dx_bf, dgain, rs_gu + rs_down

    dx2, dx2b, d_ffn2_norm, rs_ffn2 = ffn_bwd(dx3b, hn2, g2, u2, a2, wg2, wu2, wd2, x2, ffn2_norm, dx3, "ffn2")

    dmerged = _mm_nt(dx2b, wout, F32, "w_out_bwd")
    dwout = _mm_tn(merged, dx2b, BF, "w_out_dw", tn_pref=1024)
    dpd, dpf, dgd, dgf, d_bd, d_bf = _merge_bwd(dmerged, pd, pf, proj, b_gate_dil, b_gate_fox, hd)
    dyd = _mm_nt(dpd, wpd, BF, "proj_dil_bwd")
    dyf = _mm_nt(dpf, wpf, BF, "proj_fox_bwd")
    dwpd = _mm_tn(yd, dpd, BF, "proj_dil_dw", tn_pref=1024)
    dwpf = _mm_tn(yf, dpf, BF, "proj_fox_dw", tn_pref=1024)
    dwpd_c = dwpd.reshape(hd, N_DEV, d // N_DEV).transpose(1, 0, 2)
    dwpf_c = dwpf.reshape(hd, N_DEV, d // N_DEV).transpose(1, 0, 2)
    dwout_c = dwout.reshape(N_DEV, d // N_DEV, d)
    rs_mix, tok = _exchange_start([dwout_c, dwpd_c, dwpf_c], False, "rs_start_mixer")

    dqd, dl_d = _attn_bwd_dq("dil", qd, kd, vd, yd, dyd, lse_d, dil_bias, tq, "attn_dil_dq", dep=tok)
    dkd, dvd = _attn_bwd_dkv("dil", qd, kd, vd, dyd, lse_d_row, dl_d, dil_bias_t, None, tq, "attn_dil_dkv")
    dqf, dl_f = _attn_bwd_dq("fox", qf, kf, vf, yf, dyf, lse_f, c_row, tq, "attn_fox_dq")
    dkf, dvf, dc = _attn_bwd_dkv("fox", qf, kf, vf, dyf, lse_f_row, dl_f, c_rep, c_row, tq, "attn_fox_dkv")
    dc_pad = jnp.pad(dc[:, :, 0, :].reshape(nh, t).T, ((0, 0), (0, LANE - nh)))
    dlogf = _cumsum_rows(dc_pad, True, "revcumsum_dc")
    dproj, d_bforget = _assemble_dproj(dqd, dkd, dvd, dqf, dkf, dvf, dgd, dgf, dlogf, proj, tables, bf_pad, scale)

    dwin_p = _mm_tn(hm_t, dproj, BF, "w_in_dw", tk_pref=DW_ROWS, a_transposed=True)

    def perm_col(c):
        if c < 6 * hd:
            return c
        return c + 2 * d if c < 6 * hd + n_f else c - n_f

    shards = []
    for j in range(N_DEV):
        cuts = sorted({j * cols, (j + 1) * cols} | {c for c in (6 * hd, 6 * hd + n_f) if j * cols < c < (j + 1) * cols})
        shards.append(jnp.concatenate([dwin_p[:, perm_col(lo):perm_col(lo) + hi - lo]
                                       for lo, hi in zip(cuts[:-1], cuts[1:])], axis=1))
    dwin_c = jnp.stack(shards)
    rs_win, tok = _exchange_start([dwin_c], False, "rs_start_w_in")
    dx1, dx1b, d_mix_norm = _mm_nt(dproj, win_p, F32, "w_in_bwd", tn_pref=d, tk_pref=1152,
                                   rms=(x1, mix_norm, dx2), dep=tok)

    grad_x, _, d_ffn1_norm, rs_ffn1 = ffn_bwd(dx1b, hn1, g1, u1, a1, wg1, wu1, wd1, x2d, ffn1_norm, dx1, "ffn1")

    def update(handles, names, after, tag):
        recvs = _exchange_wait(handles, False, after, "rs_wait_" + tag)
        res = {}
        for recv, n in zip(recvs, names):
            turn = rows if n.endswith(("w_gate", "w_up")) else (lambda a: a)
            w, m, v = (turn(a)[0] for a in wmv[n])
            res[n] = tuple(turn(o[None]) for o in _adam_from_partials(recv, w, m, v, "adam_" + n))
        return res, res[names[-1]][0]

    wmv = {
        "ffn1_w_gate": (ffn1_w_gate, m_ffn1_w_gate, v_ffn1_w_gate),
        "ffn1_w_up": (ffn1_w_up, m_ffn1_w_up, v_ffn1_w_up),
        "ffn1_w_down": (ffn1_w_down, m_ffn1_w_down, v_ffn1_w_down),
        "w_in": (w_in, m_w_in, v_w_in),
        "w_proj_dil": (w_proj_dil, m_w_proj_dil, v_w_proj_dil),
        "w_proj_fox": (w_proj_fox, m_w_proj_fox, v_w_proj_fox),
        "w_out": (w_out, m_w_out, v_w_out),
        "ffn2_w_gate": (ffn2_w_gate, m_ffn2_w_gate, v_ffn2_w_gate),
        "ffn2_w_up": (ffn2_w_up, m_ffn2_w_up, v_ffn2_w_up),
        "ffn2_w_down": (ffn2_w_down, m_ffn2_w_down, v_ffn2_w_down),
    }
    big = {}
    after = grad_x
    for handles, names, tag in [
            (rs_ffn2, ["ffn2_w_gate", "ffn2_w_up", "ffn2_w_down"], "ffn2"),
            (rs_mix, ["w_out", "w_proj_dil", "w_proj_fox"], "mixer"),
            (rs_win, ["w_in"], "w_in"),
            (rs_ffn1, ["ffn1_w_gate", "ffn1_w_up", "ffn1_w_down"], "ffn1")]:
        res, after = update(handles, names, after, tag)
        big.update(res)

    def lanes(a):
        a = a.reshape(1, -1)
        return jnp.pad(a, ((0, 0), (0, d - a.shape[1])))

    small_names = ["ffn1_norm", "mix_norm", "b_gate_dil", "b_gate_fox", "ffn2_norm", "final_norm", "b_forget"]
    small_g = [d_ffn1_norm, d_mix_norm, d_bd, d_bf, d_ffn2_norm, d_final, d_bforget[:, :n_f]]
    small_w = [ffn1_norm, mix_norm, b_gate_dil, b_gate_fox, ffn2_norm, final_norm, b_forget]
    small_m = [m_ffn1_norm, m_mix_norm, m_b_gate_dil, m_b_gate_fox, m_ffn2_norm, m_final_norm, m_b_forget]
    small_v = [v_ffn1_norm, v_mix_norm, v_b_gate_dil, v_b_gate_fox, v_ffn2_norm, v_final_norm, v_b_forget]
    pack = lambda arrs, last: jnp.concatenate([lanes(a) for a in arrs] + [last], axis=0)
    g_all = _allreduce_small(pack(small_g, loss_lanes))
    zero_row = jnp.zeros((1, d), F32)
    one_row = jnp.ones((1, d), F32)
    s_delta, s_m, s_v = _adam_small(g_all, pack(small_w, zero_row), pack(small_m, zero_row), pack(small_v, one_row))
    loss = g_all[len(small_names), 0]

    def unpack(packed, i, like):
        return packed[i, :like.size].reshape(like.shape)

    small = {}
    for i, (n, w) in enumerate(zip(small_names, small_w)):
        small[n] = (unpack(g_all, i, w), unpack(s_delta, i, w), unpack(s_m, i, w), unpack(s_v, i, w))

    order = ["ffn1_norm", "ffn1_w_gate", "ffn1_w_up", "ffn1_w_down", "mix_norm", "w_in", "b_forget", "b_gate_dil",
             "b_gate_fox", "w_proj_dil", "w_proj_fox", "w_out", "ffn2_norm", "ffn2_w_gate", "ffn2_w_up",
             "ffn2_w_down", "final_norm"]
    res = {**big, **small}
    outs = [loss, grad_x[None]]
    for slot in range(4):
        outs += [res[n][slot] for n in order]
    return tuple(outs)
```

```python
import jax
import jax.numpy as jnp
from jax import lax
from jax.experimental import pallas as pl
from jax.experimental.pallas import tpu as pltpu

BF = jnp.bfloat16
F32 = jnp.float32
MESH = pl.DeviceIdType.MESH
N_DEV = 8

HEAD_DIM = 128
ROPE_DIM = HEAD_DIM // 4
ROPE_HALF = ROPE_DIM // 2
ROPE_THETA = 500000.0
NORM_EPS = 1e-6
DIL_PATTERNS = ((128, 1), (512, 4), (2048, 16))
MAX_WINDOW = 2048
LANE = 128
NEG = -1e30

ADAM_LR = 0.001
ADAM_B1 = 0.9
ADAM_B2 = 0.999
ADAM_EPS = 1e-08
ADAM_WD = 0.01
ADAM_STEP = 10

VMEM_LIMIT_BYTES = 56 * 1024 * 1024
FFN_ROWS = 1024
DW_ROWS = 2048
ANY = pl.BlockSpec(memory_space=pl.ANY)

NN = (((1,), (0,)), ((), ()))
NT = (((1,), (1,)), ((), ()))
TN = (((0,), (0,)), ((), ()))


def _dot(a, b, dn=NN):
    return lax.dot_general(a, b, dn, preferred_element_type=F32)


def _sig(x):
    return 0.5 + 0.5 * jnp.tanh(0.5 * x)


def _tile(n, pref, align):
    best = None
    t = align
    while t <= min(n, pref):
        if n % t == 0:
            best = t
        t += align
    return n if best is None else best


def _params():
    return pltpu.CompilerParams(vmem_limit_bytes=VMEM_LIMIT_BYTES)


def _call(body, args, dep=None, **kw):
    if dep is not None:
        n_in = len(args)
        inner = body

        def body(*refs):
            inner(*refs[:n_in], *refs[n_in + 1:])

        kw["in_specs"] = list(kw["in_specs"]) + [ANY]
        args = list(args) + [dep]
    return pl.pallas_call(body, **kw)(*args)


def _peers():
    x, y, c = lax.axis_index("x"), lax.axis_index("y"), lax.axis_index("c")
    me = 4 * x + 2 * y + c
    peers = []
    for k in range(1, N_DEV):
        px = 1 - x if (k >> 2) & 1 else x
        py = 1 - y if (k >> 1) & 1 else y
        pc = 1 - c if k & 1 else c
        peers.append((k, (px, py, pc), 4 * px + 2 * py + pc))
    return me, peers


HBM = pl.BlockSpec(memory_space=pltpu.HBM)
SEM = pl.BlockSpec(memory_space=pltpu.SEMAPHORE)
EFFECT = pltpu.SideEffectType.DATAFLOW_SIDE_EFFECTING


def _exchange_copy(gather, src_ref, land_ref, send_sems, recv_sems, me, k, peer, peer_flat, landing):
    return pltpu.make_async_remote_copy(
        src_ref=src_ref if gather else src_ref.at[peer_flat], dst_ref=land_ref.at[landing],
        send_sem=send_sems.at[k], recv_sem=recv_sems.at[k], device_id=peer, device_id_type=MESH)


ALL_PEERS = (1, 2, 3, 4, 5, 6, 7)
SIBLING = 1
SAME_CORE = (2, 4, 6)
FIRST_LEVEL = (SIBLING,) + SAME_CORE


def _exchange_start(srcs, gather, name, dep=None, ks=ALL_PEERS):
    n = len(srcs)
    extra = [] if dep is None else [dep]

    def body(*refs):
        src_refs, land_refs = refs[:n], refs[n:2 * n]
        refs = refs[2 * n + len(extra):]
        send_refs, recv_refs = refs[:n], refs[n:2 * n]
        token = refs[4 * n]
        me, peers = _peers()
        for i in range(n):
            for k, peer, peer_flat in peers:
                if k in ks:
                    _exchange_copy(gather, src_refs[i], land_refs[i], send_refs[i], recv_refs[i],
                                   me, k, peer, peer_flat, me).start()
        token[...] = jnp.zeros_like(token)

    lands = [lax.empty((N_DEV,) + s.shape[-2:], s.dtype) for s in srcs]
    sems = [pltpu.SemaphoreType.DMA((N_DEV,)) for _ in range(2 * n)]
    out = pl.pallas_call(
        body, name=name,
        out_shape=tuple(sems) + tuple(pltpu.HBM(a.shape, a.dtype) for a in list(srcs) + lands)
        + (jax.ShapeDtypeStruct((8, LANE), F32),),
        in_specs=[HBM] * (2 * n) + [ANY] * len(extra),
        out_specs=tuple([SEM] * (2 * n) + [HBM] * (2 * n) + [pl.BlockSpec(memory_space=pltpu.VMEM)]),
        input_output_aliases={i: 2 * n + i for i in range(2 * n)},
        compiler_params=pltpu.CompilerParams(has_side_effects=EFFECT),
    )(*[pltpu.with_memory_space_constraint(a, pltpu.HBM) for a in list(srcs) + lands], *extra)
    handles = [(out[2 * n + i], out[3 * n + i], out[i], out[n + i]) for i in range(n)]
    return handles, out[4 * n]


def _exchange_wait(handles, gather, after, name):
    n = len(handles)

    def body(*refs):
        src_refs, land_refs = refs[:n], refs[n:2 * n]
        send_refs, recv_refs = refs[2 * n:3 * n], refs[3 * n:4 * n]
        me, peers = _peers()
        for i in range(n):
            for k, peer, peer_flat in peers:
                cp = _exchange_copy(gather, src_refs[i], land_refs[i], send_refs[i], recv_refs[i],
                                    me, k, peer, peer_flat, peer_flat)
                cp.wait_send()
                cp.wait_recv()

    srcs = [h[0] for h in handles]
    lands = [h[1] for h in handles]
    out = pl.pallas_call(
        body, name=name,
        out_shape=tuple(pltpu.HBM(a.shape, a.dtype) for a in srcs + lands),
        in_specs=[HBM] * (2 * n) + [SEM] * (2 * n) + [ANY],
        out_specs=tuple([HBM] * (2 * n)),
        input_output_aliases={i: i for i in range(2 * n)},
        compiler_params=pltpu.CompilerParams(has_side_effects=EFFECT),
    )(*srcs, *lands, *[h[2] for h in handles], *[h[3] for h in handles], after)
    me = 4 * lax.axis_index("x") + 2 * lax.axis_index("y") + lax.axis_index("c")
    filled = []
    for src, land in zip(out[:n], out[n:]):
        own = src[None] if gather else lax.dynamic_slice_in_dim(src, me, 1, axis=0)
        filled.append(lax.dynamic_update_slice_in_dim(land, own, me, axis=0))
    return filled


def _gather_relay(handles, after, name):
    n = len(handles)

    def body(*refs):
        land_refs, recv_refs = refs[:n], refs[n:2 * n]
        refs = refs[2 * n + 1:]
        send2_refs, recv2_refs = refs[n:2 * n], refs[2 * n:3 * n]
        me, peers = _peers()
        sibling = peers[SIBLING - 1][1]
        for i in range(n):
            for k, peer, peer_flat in peers:
                if k in SAME_CORE:
                    block = land_refs[i].at[peer_flat]
                    pltpu.make_async_remote_copy(
                        src_ref=block, dst_ref=block, send_sem=send2_refs[i].at[k], recv_sem=recv_refs[i].at[k],
                        device_id=peer, device_id_type=MESH).wait_recv()
                    pltpu.make_async_remote_copy(
                        src_ref=block, dst_ref=block, send_sem=send2_refs[i].at[k], recv_sem=recv2_refs[i].at[k],
                        device_id=sibling, device_id_type=MESH).start()

    lands = [h[1] for h in handles]
    sems = [pltpu.SemaphoreType.DMA((N_DEV,)) for _ in range(2 * n)]
    out = pl.pallas_call(
        body, name=name,
        out_shape=tuple(pltpu.HBM(a.shape, a.dtype) for a in lands) + tuple(sems),
        in_specs=[HBM] * n + [SEM] * n + [ANY],
        out_specs=tuple([HBM] * n + [SEM] * (2 * n)),
        input_output_aliases={i: i for i in range(n)},
        compiler_params=pltpu.CompilerParams(has_side_effects=EFFECT),
    )(*lands, *[h[3] for h in handles], after)
    return [(h[0], out[i], h[2], h[3], out[n + i], out[2 * n + i]) for i, h in enumerate(handles)]


def _gather_wait(handles, after, name):
    n = len(handles)

    def body(*refs):
        src_refs, land_refs = refs[:n], refs[n:2 * n]
        send_refs, recv_refs = refs[2 * n:3 * n], refs[3 * n:4 * n]
        send2_refs, recv2_refs = refs[4 * n:5 * n], refs[5 * n:6 * n]
        me, peers = _peers()
        _, sibling, sibling_flat = peers[SIBLING - 1]
        for i in range(n):
            for k, peer, peer_flat in peers:
                if k in FIRST_LEVEL:
                    cp = _exchange_copy(True, src_refs[i], land_refs[i], send_refs[i], recv_refs[i],
                                        me, k, peer, peer_flat, peer_flat)
                    cp.wait_send()
                    if k == SIBLING:
                        cp.wait_recv()
                if k in SAME_CORE:
                    mine = land_refs[i].at[peer_flat]
                    theirs = land_refs[i].at[peer_flat ^ SIBLING]
                    cp = pltpu.make_async_remote_copy(
                        src_ref=mine, dst_ref=theirs, send_sem=send2_refs[i].at[k], recv_sem=recv2_refs[i].at[k],
                        device_id=sibling, device_id_type=MESH)
                    cp.wait_send()
                    cp.wait_recv()

    srcs = [h[0] for h in handles]
    lands = [h[1] for h in handles]
    out = pl.pallas_call(
        body, name=name,
        out_shape=tuple(pltpu.HBM(a.shape, a.dtype) for a in srcs + lands),
        in_specs=[HBM] * (2 * n) + [SEM] * (4 * n) + [ANY],
        out_specs=tuple([HBM] * (2 * n)),
        input_output_aliases={i: i for i in range(2 * n)},
        compiler_params=pltpu.CompilerParams(has_side_effects=EFFECT),
    )(*srcs, *lands, *[h[2] for h in handles], *[h[3] for h in handles],
      *[h[4] for h in handles], *[h[5] for h in handles], after)
    me = 4 * lax.axis_index("x") + 2 * lax.axis_index("y") + lax.axis_index("c")
    return [lax.dynamic_update_slice_in_dim(land, src[None], me, axis=0) for src, land in zip(out[:n], out[n:])]


def _allreduce_small(p):
    rows, d = p.shape

    def body(p_ref, o_ref, recv_ref, send_sems, recv_sems):
        me, peers = _peers()
        recv_ref[me] = p_ref[...]
        sends = []
        for k, peer, peer_flat in peers:
            cp = pltpu.make_async_remote_copy(
                src_ref=p_ref, dst_ref=recv_ref.at[me],
                send_sem=send_sems.at[k], recv_sem=recv_sems.at[k],
                device_id=peer, device_id_type=MESH)
            cp.start()
            sends.append(cp)
        for k, peer, peer_flat in peers:
            pltpu.make_async_remote_copy(
                src_ref=p_ref, dst_ref=recv_ref.at[peer_flat],
                send_sem=send_sems.at[k], recv_sem=recv_sems.at[k],
                device_id=peer, device_id_type=MESH).wait_recv()
        for cp in sends:
            cp.wait_send()
        acc = recv_ref[0]
        for s in range(1, N_DEV):
            acc = acc + recv_ref[s]
        is_loss = lax.broadcasted_iota(jnp.int32, (rows, d), 0) == rows - 1
        total = jnp.sum(jnp.where(is_loss, acc, 0.0))
        o_ref[...] = jnp.where(is_loss, total, acc)

    return pl.pallas_call(
        body, name="allreduce_small",
        out_shape=jax.ShapeDtypeStruct((rows, d), F32),
        in_specs=[pl.BlockSpec(memory_space=pltpu.VMEM)],
        out_specs=pl.BlockSpec(memory_space=pltpu.VMEM),
        scratch_shapes=[pltpu.VMEM((N_DEV, rows, d), F32),
                        pltpu.SemaphoreType.DMA((N_DEV,)), pltpu.SemaphoreType.DMA((N_DEV,))],
    )(p)


def _adam_math(w, g, m, v):
    m2 = ADAM_B1 * m + (1.0 - ADAM_B1) * g
    v2 = ADAM_B2 * v + (1.0 - ADAM_B2) * (g * g)
    m_hat = m2 / (1.0 - ADAM_B1 ** ADAM_STEP)
    v_hat = v2 / (1.0 - ADAM_B2 ** ADAM_STEP)
    delta = -ADAM_LR * (m_hat / (jnp.sqrt(v_hat) + ADAM_EPS) + ADAM_WD * w)
    return delta, m2, v2


def _adam_from_partials(parts, w, m, v, name):
    r, c = w.shape
    tr = _tile(r, 256, 16)

    def body(p_ref, w_ref, m_ref, v_ref, g_out, d_out, m_out, v_out):
        g = p_ref[0].astype(F32)
        for s in range(1, N_DEV):
            g = g + p_ref[s].astype(F32)
        delta, m2, v2 = _adam_math(w_ref[...], g, m_ref[...], v_ref[...])
        g_out[...] = g
        d_out[...] = delta
        m_out[...] = m2
        v_out[...] = v2

    blk = pl.BlockSpec((tr, c), lambda i: (i, 0))
    out = jax.ShapeDtypeStruct((r, c), F32)
    return pl.pallas_call(
        body, name=name, grid=(r // tr,),
        in_specs=[pl.BlockSpec((N_DEV, tr, c), lambda i: (0, i, 0)), blk, blk, blk],
        out_specs=[blk, blk, blk, blk], out_shape=[out, out, out, out],
        compiler_params=_params(),
    )(parts, w, m, v)


def _adam_small(g, w, m, v):
    def body(g_ref, w_ref, m_ref, v_ref, d_out, m_out, v_out):
        delta, m2, v2 = _adam_math(w_ref[...], g_ref[...], m_ref[...], v_ref[...])
        d_out[...] = delta
        m_out[...] = m2
        v_out[...] = v2

    out = jax.ShapeDtypeStruct(g.shape, F32)
    return pl.pallas_call(body, name="adam_small", out_shape=[out, out, out])(g, w, m, v)


def _rms_fwd(x, gain, name, dep=None, with_transpose=False):
    t, d = x.shape
    tr = _tile(t, 256, LANE)

    def body(x_ref, g_ref, o_ref, *ot_ref):
        xv = x_ref[...]
        r = lax.rsqrt(jnp.mean(xv * xv, axis=-1, keepdims=True) + NORM_EPS)
        y = xv * r * g_ref[...]
        o_ref[...] = y.astype(BF)
        if with_transpose:
            ot_ref[0][...] = jnp.transpose(y).astype(BF)

    out_specs = [pl.BlockSpec((tr, d), lambda i: (i, 0))]
    out_shape = [jax.ShapeDtypeStruct((t, d), BF)]
    if with_transpose:
        out_specs.append(pl.BlockSpec((d, tr), lambda i: (0, i)))
        out_shape.append(jax.ShapeDtypeStruct((d, t), BF))
    return _call(
        body, [x, gain], dep=dep, name=name, grid=(t // tr,),
        in_specs=[pl.BlockSpec((tr, d), lambda i: (i, 0)), pl.BlockSpec((1, d), lambda i: (0, 0))],
        out_specs=out_specs, out_shape=out_shape, compiler_params=_params(),
    )


def _rms_vjp(xv, gain, dy):
    r = lax.rsqrt(jnp.mean(xv * xv, axis=-1, keepdims=True) + NORM_EPS)
    xhat = xv * r
    dxhat = dy * gain
    dx = r * (dxhat - xhat * jnp.mean(dxhat * xhat, axis=-1, keepdims=True))
    dgain = jnp.sum(dy * xhat, axis=0, keepdims=True)
    return dx, dgain


def _loss_head(x, gain, target):
    t, d = x.shape
    tr = _tile(t, 256, 16)

    def body(x_ref, g_ref, t_ref, dx_ref, dxb_ref, dg_ref, loss_ref):
        xv = x_ref[...]
        gain = g_ref[...]
        r = lax.rsqrt(jnp.mean(xv * xv, axis=-1, keepdims=True) + NORM_EPS)
        err = xv * r * gain - t_ref[...]
        dx, dgain = _rms_vjp(xv, gain, err * (1.0 / d))
        dx_ref[...] = dx
        dxb_ref[...] = dx.astype(BF)

        @pl.when(pl.program_id(0) == 0)
        def _():
            dg_ref[...] = jnp.zeros_like(dg_ref)
            loss_ref[...] = jnp.zeros_like(loss_ref)

        dg_ref[...] += dgain
        loss_ref[...] += jnp.sum(err * err, axis=0, keepdims=True) * (0.5 / d)

    row = pl.BlockSpec((tr, d), lambda i: (i, 0))
    vec = pl.BlockSpec((1, d), lambda i: (0, 0))
    return pl.pallas_call(
        body, name="loss_head", grid=(t // tr,),
        in_specs=[row, vec, row], out_specs=[row, row, vec, vec],
        out_shape=[jax.ShapeDtypeStruct((t, d), F32), jax.ShapeDtypeStruct((t, d), BF),
                   jax.ShapeDtypeStruct((1, d), F32), jax.ShapeDtypeStruct((1, d), F32)],
        compiler_params=_params(),
    )(x, gain, target)


def _mm_nn(a, b, out_dtype, name, residual=None, tm_pref=512, tn_pref=1152):
    m, k = a.shape
    n = b.shape[1]
    tm, tn = _tile(m, tm_pref, 16), _tile(n, tn_pref, LANE)

    def body(*refs):
        if residual is None:
            a_ref, b_ref, o_ref = refs
            o_ref[...] = _dot(a_ref[...], b_ref[...]).astype(out_dtype)
        else:
            a_ref, b_ref, r_ref, o_ref = refs
            o_ref[...] = (r_ref[...] + _dot(a_ref[...], b_ref[...])).astype(out_dtype)

    in_specs = [pl.BlockSpec((tm, k), lambda j, i: (i, 0)), pl.BlockSpec((k, tn), lambda j, i: (0, j))]
    args = [a, b]
    if residual is not None:
        in_specs.append(pl.BlockSpec((tm, tn), lambda j, i: (i, j)))
        args.append(residual)
    return pl.pallas_call(
        body, name=name, grid=(n // tn, m // tm), in_specs=in_specs,
        out_specs=pl.BlockSpec((tm, tn), lambda j, i: (i, j)),
        out_shape=jax.ShapeDtypeStruct((m, n), out_dtype), compiler_params=_params(),
    )(*args)


def _rms_bwd_tail(dy_ref, first, x_ref, g_ref, dres_ref, dx_ref, dxb_ref, dg_ref):
    @pl.when(first)
    def _():
        dg_ref[...] = jnp.zeros_like(dg_ref)

    gain = g_ref[...]
    for r in range(0, dy_ref.shape[0], LANE):
        rows = pl.ds(r, min(LANE, dy_ref.shape[0] - r))
        dx, dgain = _rms_vjp(x_ref[rows, :], gain, dy_ref[rows, :])
        dx = dx + dres_ref[rows, :]
        dx_ref[rows, :] = dx
        dxb_ref[rows, :] = dx.astype(BF)
        dg_ref[...] += dgain


def _mm_nt(a, b, out_dtype, name, tm_pref=512, tn_pref=1024, tk_pref=2048, rms=None, dep=None):
    m, k = a.shape
    n = b.shape[0]
    tm, tn, tk = _tile(m, tm_pref, 16), _tile(n, tn_pref, LANE), _tile(k, tk_pref, LANE)
    nk = k // tk
    assert rms is None or tn == n

    def body(*refs):
        if rms is None:
            a_ref, b_ref, o_ref, acc_ref = refs
        else:
            a_ref, b_ref, x_ref, g_ref, dres_ref, dx_ref, dxb_ref, dg_ref, acc_ref = refs
        kk = pl.program_id(2)

        @pl.when(kk == 0)
        def _():
            acc_ref[...] = jnp.zeros_like(acc_ref)

        acc_ref[...] += _dot(a_ref[...], b_ref[...], NT)

        @pl.when(kk == nk - 1)
        def _():
            if rms is None:
                o_ref[...] = acc_ref[...].astype(out_dtype)
            else:
                _rms_bwd_tail(acc_ref, pl.program_id(1) == 0, x_ref, g_ref, dres_ref, dx_ref, dxb_ref, dg_ref)

    in_specs = [pl.BlockSpec((tm, tk), lambda j, i, kk: (i, kk)), pl.BlockSpec((tn, tk), lambda j, i, kk: (j, kk))]
    row = pl.BlockSpec((tm, tn), lambda j, i, kk: (i, j))
    if rms is None:
        args, out_specs, out_shape = [a, b], row, jax.ShapeDtypeStruct((m, n), out_dtype)
    else:
        vec = pl.BlockSpec((1, n), lambda j, i, kk: (0, 0))
        args, in_specs = [a, b, *rms], in_specs + [row, vec, row]
        out_specs = [row, row, vec]
        out_shape = [jax.ShapeDtypeStruct((m, n), F32), jax.ShapeDtypeStruct((m, n), BF),
                     jax.ShapeDtypeStruct((1, n), F32)]
    return _call(
        body, args, dep=dep, name=name, grid=(n // tn, m // tm, nk), in_specs=in_specs, out_specs=out_specs,
        out_shape=out_shape, scratch_shapes=[pltpu.VMEM((tm, tn), F32)], compiler_params=_params(),
    )


def _mm_tn(a, b, out_dtype, name, tn_pref=1152, tk_pref=512, a_transposed=False):
    (k, t) = a.shape if a_transposed else a.shape[::-1]
    n = b.shape[1]
    tn, tk = _tile(n, tn_pref, LANE), _tile(t, tk_pref, LANE if a_transposed else 16)
    nt = t // tk

    def body(a_ref, b_ref, o_ref, acc_ref):
        tt = pl.program_id(1)

        @pl.when(tt == 0)
        def _():
            acc_ref[...] = jnp.zeros_like(acc_ref)

        acc_ref[...] += _dot(a_ref[...], b_ref[...], NN if a_transposed else TN)

        @pl.when(tt == nt - 1)
        def _():
            o_ref[...] = acc_ref[...].astype(out_dtype)

    if a_transposed:
        a_spec = pl.BlockSpec((k, tk), lambda j, tt: (0, tt))
    else:
        a_spec = pl.BlockSpec((tk, k), lambda j, tt: (tt, 0))
    return pl.pallas_call(
        body, name=name, grid=(n // tn, nt),
        in_specs=[a_spec, pl.BlockSpec((tk, tn), lambda j, tt: (tt, j))],
        out_specs=pl.BlockSpec((k, tn), lambda j, tt: (0, j)),
        out_shape=jax.ShapeDtypeStruct((k, n), out_dtype),
        scratch_shapes=[pltpu.VMEM((k, tn), F32)], compiler_params=_params(),
    )(a, b)


FFN_COLS = 512


FFN_ROWS_WIDE = 2048


def _ffn_tiles(t, fc, rows=FFN_ROWS):
    return _tile(t, rows, 16), _tile(fc, FFN_COLS, LANE)


def _slabs(tm, rows=256):
    step = rows if tm % rows == 0 else tm
    return [pl.ds(r, step) for r in range(0, tm, step)]


def _ffn_gate_up(hn, wg_t, wu_t, name):
    t, d = hn.shape
    fc = wg_t.shape[0]
    tm, tn = _ffn_tiles(t, fc, FFN_ROWS_WIDE)

    def body(h_ref, wg_ref, wu_ref, g_ref, u_ref, a_ref):
        for rows in _slabs(tm):
            h = h_ref[rows, :]
            g = _dot(h, wg_ref[...], NT)
            u = _dot(h, wu_ref[...], NT)
            g_ref[rows, :] = g.astype(BF)
            u_ref[rows, :] = u.astype(BF)
            a_ref[rows, :] = (g * _sig(g) * u).astype(BF)

    wspec = pl.BlockSpec((tn, d), lambda j, i: (j, 0))
    hid = pl.BlockSpec((tm, tn), lambda j, i: (i, j))
    out = jax.ShapeDtypeStruct((t, fc), BF)
    return pl.pallas_call(
        body, name=name, grid=(fc // tn, t // tm),
        in_specs=[pl.BlockSpec((tm, d), lambda j, i: (i, 0)), wspec, wspec],
        out_specs=[hid, hid, hid], out_shape=[out, out, out], compiler_params=_params(),
    )(hn, wg_t, wu_t)


def _ffn_gate(hn, wg_t, name):
    t, d = hn.shape
    fc = wg_t.shape[0]
    tm, tn = _ffn_tiles(t, fc)

    def body(h_ref, wg_ref, g_ref):
        g_ref[...] = _dot(h_ref[...], wg_ref[...], NT)

    return pl.pallas_call(
        body, name=name, grid=(fc // tn, t // tm),
        in_specs=[pl.BlockSpec((tm, d), lambda j, i: (i, 0)), pl.BlockSpec((tn, d), lambda j, i: (j, 0))],
        out_specs=pl.BlockSpec((tm, tn), lambda j, i: (i, j)),
        out_shape=jax.ShapeDtypeStruct((t, fc), F32), compiler_params=_params(),
    )(hn, wg_t)


def _ffn_up_act(hn, wu_t, g, name):
    t, d = hn.shape
    fc = wu_t.shape[0]
    tm, tn = _ffn_tiles(t, fc, FFN_ROWS_WIDE)

    def body(h_ref, wu_ref, g_ref, gb_ref, u_ref, a_ref):
        for rows in _slabs(tm):
            u = _dot(h_ref[rows, :], wu_ref[...], NT)
            gv = g_ref[rows, :]
            gb_ref[rows, :] = gv.astype(BF)
            u_ref[rows, :] = u.astype(BF)
            a_ref[rows, :] = (gv * _sig(gv) * u).astype(BF)

    hid = pl.BlockSpec((tm, tn), lambda j, i: (i, j))
    out = jax.ShapeDtypeStruct((t, fc), BF)
    return pl.pallas_call(
        body, name=name, grid=(fc // tn, t // tm),
        in_specs=[pl.BlockSpec((tm, d), lambda j, i: (i, 0)), pl.BlockSpec((tn, d), lambda j, i: (j, 0)), hid],
        out_specs=[hid, hid, hid], out_shape=[out, out, out], compiler_params=_params(),
    )(hn, wu_t, g)


def _ffn_down(act, wd, xres, name):
    t, fc = act.shape
    d = wd.shape[1]
    tm, tk = _ffn_tiles(t, fc)

    def body(a_ref, w_ref, x_ref, o_ref):
        @pl.when(pl.program_id(1) == 0)
        def _():
            o_ref[...] = x_ref[...]

        o_ref[...] += 0.5 * _dot(a_ref[...], w_ref[...])

    row = pl.BlockSpec((tm, d), lambda i, k: (i, 0))
    return pl.pallas_call(
        body, name=name, grid=(t // tm, fc // tk),
        in_specs=[pl.BlockSpec((tm, tk), lambda i, k: (i, k)), pl.BlockSpec((tk, d), lambda i, k: (k, 0)), row],
        out_specs=row, out_shape=jax.ShapeDtypeStruct((t, d), F32), compiler_params=_params(),
    )(act, wd, xres)


def _ffn_bwd_hidden(dxb, wd, g, u, name):
    t, d = dxb.shape
    fc = wd.shape[0]
    tm, tn = _ffn_tiles(t, fc, FFN_ROWS_WIDE)

    def body(dx_ref, w_ref, g_ref, u_ref, dg_ref, du_ref):
        for rows in _slabs(tm):
            dh = 0.5 * _dot(dx_ref[rows, :], w_ref[...], NT)
            gv = g_ref[rows, :].astype(F32)
            uv = u_ref[rows, :].astype(F32)
            s = _sig(gv)
            dg_ref[rows, :] = (dh * uv * (s * (1.0 + gv * (1.0 - s)))).astype(BF)
            du_ref[rows, :] = (dh * (gv * s)).astype(BF)

    hid = pl.BlockSpec((tm, tn), lambda i, j: (i, j))
    out = jax.ShapeDtypeStruct((t, fc), BF)
    return pl.pallas_call(
        body, name=name, grid=(t // tm, fc // tn),
        in_specs=[pl.BlockSpec((tm, d), lambda i, j: (i, 0)), pl.BlockSpec((tn, d), lambda i, j: (j, 0)), hid, hid],
        out_specs=[hid, hid], out_shape=[out, out], compiler_params=_params(),
    )(dxb, wd, g, u)


def _ffn_dw(lhs, rhs, scale, name, dep=None):
    n = len(lhs)
    t, fc = lhs[0].shape
    d = rhs.shape[1]
    tk, tn = _tile(t, DW_ROWS, 16), _tile(fc, FFN_COLS, LANE)
    nt = t // tk

    def body(*refs):
        l_refs, r_ref, o_refs, acc_refs = refs[:n], refs[n], refs[n + 1:2 * n + 1], refs[2 * n + 1:]
        tt = pl.program_id(1)
        r = r_ref[...]
        for l_ref, o_ref, acc_ref in zip(l_refs, o_refs, acc_refs):
            @pl.when(tt == 0)
            def _():
                acc_ref[...] = jnp.zeros_like(acc_ref)

            acc_ref[...] += _dot(l_ref[...], r, TN)

            @pl.when(tt == nt - 1)
            def _():
                o_ref[...] = (scale * acc_ref[...]).astype(BF)

    lspec = pl.BlockSpec((tk, tn), lambda j, tt: (tt, j))
    ospec = pl.BlockSpec((tn, d), lambda j, tt: (j, 0))
    out = jax.ShapeDtypeStruct((fc, d), BF)
    return _call(
        body, [*lhs, rhs], dep=dep, name=name, grid=(fc // tn, nt),
        in_specs=[lspec] * n + [pl.BlockSpec((tk, d), lambda j, tt: (tt, 0))],
        out_specs=[ospec] * n, out_shape=[out] * n,
        scratch_shapes=[pltpu.VMEM((tn, d), F32)] * n, compiler_params=_params(),
    )


def _rms_bwd(dy, x, gain, dres, name):
    t, d = x.shape
    tr = _tile(t, 256, 16)

    def body(dy_ref, x_ref, g_ref, dres_ref, dx_ref, dxb_ref, dg_ref):
        _rms_bwd_tail(dy_ref, pl.program_id(0) == 0, x_ref, g_ref, dres_ref, dx_ref, dxb_ref, dg_ref)

    row = pl.BlockSpec((tr, d), lambda i: (i, 0))
    vec = pl.BlockSpec((1, d), lambda i: (0, 0))
    return pl.pallas_call(
        body, name=name, grid=(t // tr,),
        in_specs=[row, row, vec, row], out_specs=[row, row, vec],
        out_shape=[jax.ShapeDtypeStruct((t, d), F32), jax.ShapeDtypeStruct((t, d), BF),
                   jax.ShapeDtypeStruct((1, d), F32)],
        compiler_params=_params(),
    )(dy, x, gain, dres)


def _ffn_bwd_input(dg, du, wg_t, wu_t, name, dep=None):
    t, fc = dg.shape
    d = wg_t.shape[1]
    tm, tk = _ffn_tiles(t, fc)

    def body(dg_ref, du_ref, wg_ref, wu_ref, o_ref):
        @pl.when(pl.program_id(1) == 0)
        def _():
            o_ref[...] = jnp.zeros_like(o_ref)

        o_ref[...] += _dot(dg_ref[...], wg_ref[...]) + _dot(du_ref[...], wu_ref[...])

    hid = pl.BlockSpec((tm, tk), lambda i, k: (i, k))
    wspec = pl.BlockSpec((tk, d), lambda i, k: (k, 0))
    return _call(
        body, [dg, du, wg_t, wu_t], dep=dep, name=name, grid=(t // tm, fc // tk),
        in_specs=[hid, hid, wspec, wspec],
        out_specs=pl.BlockSpec((tm, d), lambda i, k: (i, 0)),
        out_shape=jax.ShapeDtypeStruct((t, d), F32), compiler_params=_params(),
    )


def _rope_tables(t):
    pos = jnp.arange(t, dtype=F32)
    inv_freq = ROPE_THETA ** (-jnp.arange(0, ROPE_DIM, 2, dtype=F32) / ROPE_DIM)
    ang = pos[:, None] * inv_freq[None, :]
    cos, sin = jnp.cos(ang), jnp.sin(ang)
    rest = HEAD_DIM - ROPE_DIM
    one = jnp.ones((t, rest), F32)
    zero_h = jnp.zeros((t, ROPE_HALF), F32)
    zero_r = jnp.zeros((t, rest), F32)
    c = jnp.concatenate([cos, cos, one], axis=1)
    s1 = jnp.concatenate([-sin, zero_h, zero_r], axis=1)
    s2 = jnp.concatenate([zero_h, sin, zero_r], axis=1)
    return c, s1, s2


def _rope(xh, c, s1, s2):
    return xh * c + pltpu.roll(xh, HEAD_DIM - ROPE_HALF, 1) * s1 + pltpu.roll(xh, ROPE_HALF, 1) * s2


def _rope_t(dh, c, s1, s2):
    return dh * c + pltpu.roll(dh * s1, ROPE_HALF, 1) + pltpu.roll(dh * s2, HEAD_DIM - ROPE_HALF, 1)


def _mixer_prep(proj, tables, bf_pad, hd, scale):
    t, np_ = proj.shape
    tr = _tile(t, 256, 16)
    nh = hd // HEAD_DIM
    nblk = hd // LANE
    f_blk = np_ // LANE - 1

    def body(qd_ref, kd_ref, vd_ref, qf_ref, kf_ref, vf_ref, fl_ref, c_ref, s1_ref, s2_ref, b_ref,
             oqd, okd, ovd, oqf, okf, ovf, olog):
        c, s1, s2 = c_ref[...], s1_ref[...], s2_ref[...]
        for h in range(nh):
            sl = slice(h * HEAD_DIM, (h + 1) * HEAD_DIM)
            oqd[:, sl] = (_rope(qd_ref[:, sl], c, s1, s2) * scale).astype(BF)
            okd[:, sl] = _rope(kd_ref[:, sl], c, s1, s2).astype(BF)
        ovd[...] = vd_ref[...].astype(BF)
        oqf[...] = (qf_ref[...] * scale).astype(BF)
        okf[...] = kf_ref[...].astype(BF)
        ovf[...] = vf_ref[...].astype(BF)
        z = fl_ref[...] + b_ref[...]
        olog[...] = jnp.minimum(z, 0.0) - jnp.log(1.0 + jnp.exp(-jnp.abs(z)))

    def col(kblk):
        return pl.BlockSpec((tr, hd), lambda i, kblk=kblk: (i, kblk))

    lane_row = pl.BlockSpec((tr, LANE), lambda i: (i, 0))
    in_specs = [col(0), col(1), col(2), col(3), col(4), col(5),
                pl.BlockSpec((tr, LANE), lambda i: (i, f_blk)),
                lane_row, lane_row, lane_row, pl.BlockSpec((1, LANE), lambda i: (0, 0))]
    o = pl.BlockSpec((tr, hd), lambda i: (i, 0))
    ob = jax.ShapeDtypeStruct((t, hd), BF)
    del nblk
    return pl.pallas_call(
        body, name="mixer_prep", grid=(t // tr,), in_specs=in_specs,
        out_specs=[o, o, o, o, o, o, lane_row],
        out_shape=[ob, ob, ob, ob, ob, ob, jax.ShapeDtypeStruct((t, LANE), F32)],
        compiler_params=_params(),
    )(proj, proj, proj, proj, proj, proj, proj, *tables, bf_pad)


def _split3(x):
    x1 = x.astype(BF)
    r1 = x - x1.astype(F32)
    x2 = r1.astype(BF)
    x3 = (r1 - x2.astype(F32)).astype(BF)
    return x1, x2, x3


def _cumsum_rows(x, reverse, name):
    t, w = x.shape
    blk = LANE
    nb = t // blk

    def body(x_ref, o_ref):
        r = lax.broadcasted_iota(jnp.int32, (blk, blk), 0)
        c = lax.broadcasted_iota(jnp.int32, (blk, blk), 1)
        tri = jnp.where((c >= r) if reverse else (c <= r), 1.0, 0.0).astype(BF)

        def step(i, carry):
            b = (nb - 1 - i) if reverse else i
            off = pl.multiple_of(b * blk, blk)
            xb = x_ref[pl.ds(off, blk), :]
            x1, x2, x3 = _split3(xb)
            o_ref[pl.ds(off, blk), :] = _dot(tri, x1) + _dot(tri, x2) + _dot(tri, x3) + carry
            return carry + jnp.sum(xb, axis=0, keepdims=True)

        lax.fori_loop(0, nb, step, jnp.zeros((1, w), F32))

    return pl.pallas_call(body, name=name, out_shape=jax.ShapeDtypeStruct((t, w), F32),
                          compiler_params=_params())(x)


ATTN_ROWS = 16


def _dil_bias_tiles(tq):
    nbias = MAX_WINDOW // tq + 1
    b = lax.broadcasted_iota(jnp.int32, (nbias, tq, tq), 0)
    i = lax.broadcasted_iota(jnp.int32, (nbias, tq, tq), 1)
    j = lax.broadcasted_iota(jnp.int32, (nbias, tq, tq), 2)
    delta = b * tq + i - j
    mult = jnp.zeros((nbias, tq, tq), F32)
    for w, dil in DIL_PATTERNS:
        mult = mult + jnp.where((delta >= 0) & (delta <= w) & (delta % dil == 0), 1.0, 0.0)
    return jnp.where(mult > 0.0, jnp.log(jnp.maximum(mult, 1.0)), NEG)


def _rep(x, width):
    return jnp.tile(x, (1, width // LANE))


def _chunks(n_rows, fn):
    for c in range(n_rows // ATTN_ROWS):
        fn(c * ATTN_ROWS)


def _causal(r0, tq, transposed):
    a = lax.broadcasted_iota(jnp.int32, (ATTN_ROWS, tq), 0) + r0
    b = lax.broadcasted_iota(jnp.int32, (ATTN_ROWS, tq), 1)
    return (a <= b) if transposed else (b <= a)


def _rows8(x):
    return jnp.transpose(x)[:8, :]


def _attn_fwd(mode, q, k, v, bias, tq, name):
    t, hd = q.shape
    nh = hd // HEAD_DIM
    nb = t // tq
    wb = MAX_WINDOW // tq
    fox = mode == "fox"

    def body(q_ref, k_ref, v_ref, b_ref, o_ref, lse_ref, lse_row_ref, s_ref, p_ref, m_ref, l_ref, acc_ref):
        qi = pl.program_id(1)
        qb = q_ref[...]
        m_ref[...] = jnp.full_like(m_ref, NEG)
        l_ref[...] = jnp.zeros_like(l_ref)
        acc_ref[...] = jnp.zeros_like(acc_ref)

        def tile(kj, diag):
            off = pl.multiple_of(kj * tq, tq)
            s_ref[...] = _dot(qb, k_ref[pl.ds(off, tq), :], NT)
            if fox:
                brow = b_ref[qi][:, :1] - b_ref[kj]

            def chunk(r0):
                rows = pl.ds(r0, ATTN_ROWS)
                if fox:
                    s = s_ref[rows, :] + brow
                    if diag:
                        s = jnp.where(_causal(r0, tq, False), s, NEG)
                else:
                    s = s_ref[rows, :] + b_ref[qi - kj, rows, :]
                m_old = m_ref[rows, :]
                m_new = jnp.maximum(m_old, jnp.max(s, axis=1, keepdims=True))
                p = jnp.exp(s - _rep(m_new, tq))
                alpha = jnp.exp(m_old - m_new)
                l_ref[rows, :] = alpha * l_ref[rows, :] + jnp.sum(p, axis=1, keepdims=True)
                m_ref[rows, :] = m_new
                acc_ref[rows, :] = alpha * acc_ref[rows, :]
                p_ref[rows, :] = p.astype(BF)

            _chunks(tq, chunk)
            acc_ref[...] += _dot(p_ref[...], v_ref[pl.ds(off, tq), :])

        tile(qi, True)
        if fox:
            lax.fori_loop(0, qi, lambda kj, c: (tile(kj, False), c)[1], 0)
        else:
            lax.fori_loop(1, jnp.minimum(qi, wb) + 1, lambda i, c: (tile(qi - i, False), c)[1], 0)
        o_ref[...] = (acc_ref[...] / l_ref[...]).astype(BF)
        lse = m_ref[...] + jnp.log(l_ref[...])
        lse_ref[...] = lse
        lse_row_ref[...] = _rows8(lse)

    qspec = pl.BlockSpec((tq, HEAD_DIM), lambda h, i: (i, h))
    kvspec = pl.BlockSpec((t, HEAD_DIM), lambda h, i: (0, h))
    repspec = pl.BlockSpec((None, tq, LANE), lambda h, i: (h, i, 0))
    row8spec = pl.BlockSpec((None, None, 8, tq), lambda h, i: (h, i, 0, 0))
    if fox:
        bspec = pl.BlockSpec((None, nb, 1, tq), lambda h, i: (h, 0, 0, 0))
    else:
        bspec = pl.BlockSpec((wb + 1, tq, tq), lambda h, i: (0, 0, 0))
    return pl.pallas_call(
        body, name=name, grid=(nh, nb), in_specs=[qspec, kvspec, kvspec, bspec],
        out_specs=[qspec, repspec, row8spec],
        out_shape=[jax.ShapeDtypeStruct((t, hd), BF), jax.ShapeDtypeStruct((nh, t, LANE), F32),
                   jax.ShapeDtypeStruct((nh, nb, 8, tq), F32)],
        scratch_shapes=[pltpu.VMEM((tq, tq), F32), pltpu.VMEM((tq, tq), BF), pltpu.VMEM((tq, LANE), F32),
                        pltpu.VMEM((tq, LANE), F32), pltpu.VMEM((tq, HEAD_DIM), F32)],
        compiler_params=_params(),
    )(q, k, v, bias)


def _attn_bwd_dq(mode, q, k, v, o, do, lse, bias, tq, name, dep=None):
    t, hd = q.shape
    nh = hd // HEAD_DIM
    nb = t // tq
    wb = MAX_WINDOW // tq
    fox = mode == "fox"

    def body(q_ref, k_ref, v_ref, o_ref, do_ref, lse_ref, b_ref, dq_ref, dl_row_ref,
             s_ref, dp_ref, x_ref, y_ref, acc_ref, acc2_ref, dl_ref):
        qi = pl.program_id(1)
        qb = q_ref[...]
        dob = do_ref[...]
        acc_ref[...] = jnp.zeros_like(acc_ref)
        if fox:
            acc2_ref[...] = jnp.zeros_like(acc2_ref)
            dl_ref[...] = jnp.zeros_like(dl_ref)
        else:
            prod = o_ref[...].astype(F32) * dob.astype(F32)
            dl_ref[...] = jnp.broadcast_to(jnp.sum(prod, axis=1, keepdims=True), (tq, LANE))

        def tile(kj, diag):
            off = pl.multiple_of(kj * tq, tq)
            kb = k_ref[pl.ds(off, tq), :]
            s_ref[...] = _dot(qb, kb, NT)
            dp_ref[...] = _dot(dob, v_ref[pl.ds(off, tq), :], NT)
            if fox:
                brow = b_ref[qi][:, :1] - b_ref[kj]

            def chunk(r0):
                rows = pl.ds(r0, ATTN_ROWS)
                lse_c = _rep(lse_ref[rows, :], tq)
                if fox:
                    s = s_ref[rows, :] + brow
                    if diag:
                        s = jnp.where(_causal(r0, tq, False), s, NEG)
                    p = jnp.exp(s - lse_c)
                    pdp = p * dp_ref[rows, :]
                    dl_ref[rows, :] += jnp.sum(pdp, axis=1, keepdims=True)
                    x_ref[rows, :] = pdp.astype(BF)
                    y_ref[rows, :] = p.astype(BF)
                else:
                    p = jnp.exp(s_ref[rows, :] + b_ref[qi - kj, rows, :] - lse_c)
                    x_ref[rows, :] = (p * (dp_ref[rows, :] - _rep(dl_ref[rows, :], tq))).astype(BF)

            _chunks(tq, chunk)
            acc_ref[...] += _dot(x_ref[...], kb)
            if fox:
                acc2_ref[...] += _dot(y_ref[...], kb)

        tile(qi, True)
        if fox:
            lax.fori_loop(0, qi, lambda kj, c: (tile(kj, False), c)[1], 0)
            dq_ref[...] = acc_ref[...] - dl_ref[...] * acc2_ref[...]
        else:
            lax.fori_loop(1, jnp.minimum(qi, wb) + 1, lambda i, c: (tile(qi - i, False), c)[1], 0)
            dq_ref[...] = acc_ref[...]
        dl_row_ref[...] = _rows8(dl_ref[...])

    qspec = pl.BlockSpec((tq, HEAD_DIM), lambda h, i: (i, h))
    kvspec = pl.BlockSpec((t, HEAD_DIM), lambda h, i: (0, h))
    repspec = pl.BlockSpec((None, tq, LANE), lambda h, i: (h, i, 0))
    row8spec = pl.BlockSpec((None, None, 8, tq), lambda h, i: (h, i, 0, 0))
    if fox:
        bspec = pl.BlockSpec((None, nb, 1, tq), lambda h, i: (h, 0, 0, 0))
    else:
        bspec = pl.BlockSpec((wb + 1, tq, tq), lambda h, i: (0, 0, 0))
    return _call(
        body, [q, k, v, o, do, lse, bias], dep=dep, name=name, grid=(nh, nb),
        in_specs=[qspec, kvspec, kvspec, qspec, qspec, repspec, bspec],
        out_specs=[qspec, row8spec],
        out_shape=[jax.ShapeDtypeStruct((t, hd), F32), jax.ShapeDtypeStruct((nh, nb, 8, tq), F32)],
        scratch_shapes=[pltpu.VMEM((tq, tq), F32), pltpu.VMEM((tq, tq), F32), pltpu.VMEM((tq, tq), BF),
                        pltpu.VMEM((tq, tq), BF), pltpu.VMEM((tq, HEAD_DIM), F32),
                        pltpu.VMEM((tq, HEAD_DIM), F32), pltpu.VMEM((tq, LANE), F32)],
        compiler_params=_params(),
    )


def _attn_bwd_dkv(mode, q, k, v, do, lse_row, dl_row, bias_t, c_row, tq, name):
    t, hd = q.shape
    nh = hd // HEAD_DIM
    nb = t // tq
    wb = MAX_WINDOW // tq
    fox = mode == "fox"

    def body(*refs):
        if fox:
            (q_ref, k_ref, v_ref, do_ref, lse_ref, dl_ref, b_ref, cq_ref, dk_ref, dv_ref, dc_row_ref,
             s_ref, dp_ref, x_ref, y_ref, dc_ref) = refs
        else:
            q_ref, k_ref, v_ref, do_ref, lse_ref, dl_ref, b_ref, dk_ref, dv_ref, s_ref, dp_ref, x_ref, y_ref = refs
        kj = pl.program_id(1)
        kb = k_ref[...]
        vb = v_ref[...]
        dk_ref[...] = jnp.zeros_like(dk_ref)
        dv_ref[...] = jnp.zeros_like(dv_ref)
        if fox:
            dc_ref[...] = jnp.zeros_like(dc_ref)

        def tile(qi, diag):
            off = pl.multiple_of(qi * tq, tq)
            qb = q_ref[pl.ds(off, tq), :]
            dob = do_ref[pl.ds(off, tq), :]
            s_ref[...] = _dot(kb, qb, NT)
            dp_ref[...] = _dot(vb, dob, NT)
            lse_r = lse_ref[qi, 0:1, :]
            dl_r = dl_ref[qi, 0:1, :]
            if fox:
                kbias = cq_ref[qi][:, :1] - b_ref[...]

            def chunk(r0):
                rows = pl.ds(r0, ATTN_ROWS)
                if fox:
                    s = s_ref[rows, :] + _rep(kbias[r0:r0 + ATTN_ROWS, :], tq)
                    if diag:
                        s = jnp.where(_causal(r0, tq, True), s, NEG)
                else:
                    s = s_ref[rows, :] + b_ref[qi - kj, rows, :]
                pt = jnp.exp(s - lse_r)
                dst = pt * (dp_ref[rows, :] - dl_r)
                x_ref[rows, :] = pt.astype(BF)
                y_ref[rows, :] = dst.astype(BF)
                if fox:
                    dc_ref[rows, :] -= jnp.sum(dst, axis=1, keepdims=True)

            _chunks(tq, chunk)
            dv_ref[...] += _dot(x_ref[...], dob)
            dk_ref[...] += _dot(y_ref[...], qb)

        tile(kj, True)
        hi = nb if fox else jnp.minimum(kj + wb + 1, nb)
        lax.fori_loop(kj + 1, hi, lambda qi, c: (tile(qi, False), c)[1], 0)
        if fox:
            dc_row_ref[...] = _rows8(dc_ref[...])

    blkspec = pl.BlockSpec((tq, HEAD_DIM), lambda h, j: (j, h))
    fullspec = pl.BlockSpec((t, HEAD_DIM), lambda h, j: (0, h))
    rows8spec = pl.BlockSpec((None, nb, 8, tq), lambda h, j: (h, 0, 0, 0))
    repspec = pl.BlockSpec((None, tq, LANE), lambda h, j: (h, j, 0))
    in_specs = [fullspec, blkspec, blkspec, fullspec, rows8spec, rows8spec]
    args = [q, k, v, do, lse_row, dl_row, bias_t]
    out_specs = [blkspec, blkspec]
    out_shape = [jax.ShapeDtypeStruct((t, hd), F32), jax.ShapeDtypeStruct((t, hd), F32)]
    scratch = [pltpu.VMEM((tq, tq), F32), pltpu.VMEM((tq, tq), F32), pltpu.VMEM((tq, tq), BF),
               pltpu.VMEM((tq, tq), BF)]
    if fox:
        in_specs += [repspec, pl.BlockSpec((None, nb, 1, tq), lambda h, j: (h, 0, 0, 0))]
        args.append(c_row)
        out_specs.append(pl.BlockSpec((None, None, 8, tq), lambda h, j: (h, j, 0, 0)))
        out_shape.append(jax.ShapeDtypeStruct((nh, nb, 8, tq), F32))
        scratch.append(pltpu.VMEM((tq, LANE), F32))
    else:
        in_specs.append(pl.BlockSpec((wb + 1, tq, tq), lambda h, j: (0, 0, 0)))
    return pl.pallas_call(
        body, name=name, grid=(nh, nb), in_specs=in_specs, out_specs=out_specs, out_shape=out_shape,
        scratch_shapes=scratch, compiler_params=_params(),
    )(*args)


def _gate_specs(t, d, hd, tr):
    row = pl.BlockSpec((tr, d), lambda i: (i, 0))
    vec = pl.BlockSpec((1, d), lambda i: (0, 0))
    base = 6 * hd // d
    gd = pl.BlockSpec((tr, d), lambda i: (i, base))
    gf = pl.BlockSpec((tr, d), lambda i: (i, base + 1))
    return row, vec, gd, gf


def _proj_merge(yd, yf, wpd, wpf, proj, b_d, b_f, hd):
    t = yd.shape[0]
    d = wpd.shape[1]
    tr = _tile(t, 256, 16)
    row, vec, gd, gf = _gate_specs(t, d, hd, tr)

    def body(yd_ref, yf_ref, wd_ref, wf_ref, gd_ref, gf_ref, bd_ref, bf_ref, pd_ref, pf_ref, o_ref):
        pd = _dot(yd_ref[...], wd_ref[...])
        pf = _dot(yf_ref[...], wf_ref[...])
        pd_ref[...] = pd
        pf_ref[...] = pf
        o_ref[...] = (_sig(gd_ref[...] + bd_ref[...]) * pd + _sig(gf_ref[...] + bf_ref[...]) * pf).astype(BF)

    yspec = pl.BlockSpec((tr, hd), lambda i: (i, 0))
    wspec = pl.BlockSpec((hd, d), lambda i: (0, 0))
    f32 = jax.ShapeDtypeStruct((t, d), F32)
    return pl.pallas_call(
        body, name="proj_merge", grid=(t // tr,), in_specs=[yspec, yspec, wspec, wspec, gd, gf, vec, vec],
        out_specs=[row, row, row], out_shape=[f32, f32, jax.ShapeDtypeStruct((t, d), BF)],
        compiler_params=_params(),
    )(yd, yf, wpd, wpf, proj, proj, b_d, b_f)


def _merge_bwd(dm, pd, pf, proj, b_d, b_f, hd):
    t, d = pd.shape
    tr = _tile(t, 256, 16)
    row, vec, gd, gf = _gate_specs(t, d, hd, tr)

    def body(dm_ref, pd_ref, pf_ref, gd_ref, gf_ref, bd_ref, bf_ref,
             dpd_ref, dpf_ref, dgd_ref, dgf_ref, dbd_ref, dbf_ref):
        dmv = dm_ref[...]
        sd = _sig(gd_ref[...] + bd_ref[...])
        sf = _sig(gf_ref[...] + bf_ref[...])
        dgd = dmv * pd_ref[...] * (sd * (1.0 - sd))
        dgf = dmv * pf_ref[...] * (sf * (1.0 - sf))
        dpd_ref[...] = (dmv * sd).astype(BF)
        dpf_ref[...] = (dmv * sf).astype(BF)
        dgd_ref[...] = dgd.astype(BF)
        dgf_ref[...] = dgf.astype(BF)

        @pl.when(pl.program_id(0) == 0)
        def _():
            dbd_ref[...] = jnp.zeros_like(dbd_ref)
            dbf_ref[...] = jnp.zeros_like(dbf_ref)

        dbd_ref[...] += jnp.sum(dgd, axis=0, keepdims=True)
        dbf_ref[...] += jnp.sum(dgf, axis=0, keepdims=True)

    ob = jax.ShapeDtypeStruct((t, d), BF)
    ov = jax.ShapeDtypeStruct((1, d), F32)
    return pl.pallas_call(
        body, name="merge_bwd", grid=(t // tr,), in_specs=[row, row, row, gd, gf, vec, vec],
        out_specs=[row, row, row, row, vec, vec], out_shape=[ob, ob, ob, ob, ov, ov],
        compiler_params=_params(),
    )(dm, pd, pf, proj, proj, b_d, b_f)


def _assemble_dproj(dqd, dkd, dvd, dqf, dkf, dvf, dgd, dgf, dlogf, proj, tables, bf_pad, scale):
    t, np_ = proj.shape
    hd = dqd.shape[1]
    d = dgd.shape[1]
    nh = hd // HEAD_DIM
    tr = _tile(t, 256, 16)
    f_blk = np_ // LANE - 1

    def body(dqd_ref, dkd_ref, dvd_ref, dqf_ref, dkf_ref, dvf_ref, dgd_ref, dgf_ref, dlog_ref, fl_ref,
             c_ref, s1_ref, s2_ref, b_ref, o_ref, db_ref):
        c, s1, s2 = c_ref[...], s1_ref[...], s2_ref[...]
        for h in range(nh):
            sl = slice(h * HEAD_DIM, (h + 1) * HEAD_DIM)
            o_ref[:, sl] = (_rope_t(dqd_ref[:, sl], c, s1, s2) * scale).astype(BF)
            o_ref[:, hd + h * HEAD_DIM:hd + (h + 1) * HEAD_DIM] = _rope_t(dkd_ref[:, sl], c, s1, s2).astype(BF)
        o_ref[:, 2 * hd:3 * hd] = dvd_ref[...].astype(BF)
        o_ref[:, 3 * hd:4 * hd] = (dqf_ref[...] * scale).astype(BF)
        o_ref[:, 4 * hd:5 * hd] = dkf_ref[...].astype(BF)
        o_ref[:, 5 * hd:6 * hd] = dvf_ref[...].astype(BF)
        o_ref[:, 6 * hd:6 * hd + d] = dgd_ref[...]
        o_ref[:, 6 * hd + d:6 * hd + 2 * d] = dgf_ref[...]
        z = fl_ref[...] + b_ref[...]
        dfl = dlog_ref[...] * _sig(-z)
        o_ref[:, 6 * hd + 2 * d:] = dfl.astype(BF)

        @pl.when(pl.program_id(0) == 0)
        def _():
            db_ref[...] = jnp.zeros_like(db_ref)

        db_ref[...] += jnp.sum(dfl, axis=0, keepdims=True)

    head = pl.BlockSpec((tr, hd), lambda i: (i, 0))
    row = pl.BlockSpec((tr, d), lambda i: (i, 0))
    lane_row = pl.BlockSpec((tr, LANE), lambda i: (i, 0))
    lane_vec = pl.BlockSpec((1, LANE), lambda i: (0, 0))
    return pl.pallas_call(
        body, name="assemble_dproj", grid=(t // tr,),
        in_specs=[head] * 6 + [row, row, lane_row, pl.BlockSpec((tr, LANE), lambda i: (i, f_blk)),
                               lane_row, lane_row, lane_row, lane_vec],
        out_specs=[pl.BlockSpec((tr, np_), lambda i: (i, 0)), lane_vec],
        out_shape=[jax.ShapeDtypeStruct((t, np_), BF), jax.ShapeDtypeStruct((1, LANE), F32)],
        compiler_params=_params(),
    )(dqd, dkd, dvd, dqf, dkf, dvf, dgd, dgf, dlogf, proj, *tables, bf_pad)


def _to_rows(a, tq):
    h, t = a.shape
    return a.reshape(h, t // tq, 1, tq)


def kernel(x, ffn1_norm, ffn1_w_gate, ffn1_w_up, ffn1_w_down, mix_norm, w_in, b_forget, b_gate_dil, b_gate_fox, w_proj_dil, w_proj_fox, w_out, ffn2_norm, ffn2_w_gate, ffn2_w_up, ffn2_w_down, final_norm, loss_target, m_ffn1_norm, m_ffn1_w_gate, m_ffn1_w_up, m_ffn1_w_down, m_mix_norm, m_w_in, m_b_forget, m_b_gate_dil, m_b_gate_fox, m_w_proj_dil, m_w_proj_fox, m_w_out, m_ffn2_norm, m_ffn2_w_gate, m_ffn2_w_up, m_ffn2_w_down, m_final_norm, v_ffn1_norm, v_ffn1_w_gate, v_ffn1_w_up, v_ffn1_w_down, v_mix_norm, v_w_in, v_b_forget, v_b_gate_dil, v_b_gate_fox, v_w_proj_dil, v_w_proj_fox, v_w_out, v_ffn2_norm, v_ffn2_w_gate, v_ffn2_w_up, v_ffn2_w_down, v_final_norm):
    t, d = x.shape[1], x.shape[2]
    hd = w_proj_dil.shape[1]
    nh = hd // HEAD_DIM
    n_f = b_forget.shape[1]
    cols = w_in.shape[2]
    in_cols = N_DEV * cols
    assert in_cols == 6 * hd + n_f + 2 * d and n_f == nh and n_f <= LANE
    np_ = 6 * hd + 2 * d + LANE
    scale = HEAD_DIM ** -0.5
    tq = _tile(t, 512, LANE)
    assert MAX_WINDOW % tq == 0 and tq % 16 == 0

    x2d = x[0]
    tgt = loss_target[0]

    def rows(w):
        return jnp.swapaxes(w, 1, 2)

    fc = N_DEV * ffn1_w_down.shape[1]
    ag_order = [rows(ffn1_w_gate), rows(ffn1_w_up), ffn1_w_down, w_in, w_proj_dil, w_proj_fox, w_out,
                rows(ffn2_w_gate), rows(ffn2_w_up), ffn2_w_down]
    ag_first, tok = _exchange_start([w[0].astype(BF) for w in ag_order[:2]], True, "ag_start_first", ks=FIRST_LEVEL)
    ag_rest, ag_token = _exchange_start([w[0].astype(BF) for w in ag_order[2:]], True, "ag_start", dep=tok,
                                        ks=FIRST_LEVEL)
    ag = ag_first + ag_rest

    def relay(idx, after, name):
        for i, h in zip(idx, _gather_relay([ag[i] for i in idx], after, name)):
            ag[i] = h

    def gathered(idx, after, name):
        return _gather_wait([ag[i] for i in idx], after, name)

    def ffn_weight(idx, after, name):
        return [w.reshape(fc, d) for w in gathered(idx, after, name)]

    tables = _rope_tables(t)
    bf_pad = jnp.pad(b_forget, ((0, 0), (0, LANE - n_f)))

    hn1, = _rms_fwd(x2d, ffn1_norm, "rms_ffn1", dep=ag_token)
    relay([0], hn1, "ag_relay_ffn1_gate")
    wg1, = ffn_weight([0], hn1, "ag_wait_ffn1_gate")
    g1_f32 = _ffn_gate(hn1, wg1, "ffn1_gate")
    relay([1], g1_f32, "ag_relay_ffn1_up")
    wu1, = ffn_weight([1], g1_f32, "ag_wait_ffn1_up")
    relay([2], wu1, "ag_relay_ffn1_down")
    g1, u1, a1 = _ffn_up_act(hn1, wu1, g1_f32, "ffn1_up_act")
    wd1, = ffn_weight([2], a1, "ag_wait_ffn1_down")
    relay([3], wd1, "ag_relay_w_in")
    x1 = _ffn_down(a1, wd1, x2d, "ffn1_down")

    hm, hm_t = _rms_fwd(x1, mix_norm, "rms_mix", with_transpose=True)
    win_g, = gathered([3], hm, "ag_wait_w_in")
    relay([4, 5, 6], win_g, "ag_relay_mixer")
    segments = [(0, 6 * hd), (6 * hd + n_f, in_cols), (6 * hd, 6 * hd + n_f)]
    pieces = []
    for lo, hi in segments:
        for j in range(lo // cols, (hi - 1) // cols + 1):
            s, e = max(lo, j * cols), min(hi, (j + 1) * cols)
            pieces.append(win_g[j, :, s - j * cols:e - j * cols])
    win_p = jnp.concatenate(pieces + [jnp.zeros((d, LANE - n_f), BF)], axis=1)
    proj = _mm_nn(hm, win_p, F32, "w_in_fwd")
    qd, kd, vd, qf, kf, vf, logf = _mixer_prep(proj, tables, bf_pad, hd, scale)
    csum = _cumsum_rows(logf, False, "cumsum_logf")
    c_heads = csum[:, :nh].T
    c_row = _to_rows(c_heads, tq)
    c_rep = jnp.broadcast_to(c_heads[:, :, None], (nh, t, LANE))
    dil_bias = _dil_bias_tiles(tq)
    dil_bias_t = dil_bias.transpose(0, 2, 1)
    relay([7, 8, 9], qd, "ag_relay_ffn2")
    yd, lse_d, lse_d_row = _attn_fwd("dil", qd, kd, vd, dil_bias, tq, "attn_dil_fwd")
    yf, lse_f, lse_f_row = _attn_fwd("fox", qf, kf, vf, c_row, tq, "attn_fox_fwd")
    wpd_g, wpf_g = gathered([4, 5], yf, "ag_wait_proj")
    wpd = wpd_g.transpose(1, 0, 2).reshape(hd, d)
    wpf = wpf_g.transpose(1, 0, 2).reshape(hd, d)
    pd, pf, merged = _proj_merge(yd, yf, wpd, wpf, proj, b_gate_dil, b_gate_fox, hd)
    wout_g, = gathered([6], merged, "ag_wait_w_out")
    wout = wout_g.reshape(d, d)
    x2 = _mm_nn(merged, wout, F32, "w_out_fwd", residual=x1, tn_pref=1024)

    hn2, = _rms_fwd(x2, ffn2_norm, "rms_ffn2")
    wg2, wu2 = ffn_weight([7, 8], hn2, "ag_wait_ffn2_gate_up")
    g2, u2, a2 = _ffn_gate_up(hn2, wg2, wu2, "ffn2_gate_up")
    wd2, = ffn_weight([9], a2, "ag_wait_ffn2_down")
    x3 = _ffn_down(a2, wd2, x2, "ffn2_down")

    dx3, dx3b, d_final, loss_lanes = _loss_head(x3, final_norm.reshape(1, d), tgt)

    def ffn_bwd(dxb, hn, g, u, a, wg_t, wu_t, wd, x_in, gain, dres, tag):
        def parts(dw):
            return dw.reshape(N_DEV, fc // N_DEV, d)

        dg, du = _ffn_bwd_hidden(dxb, wd, g, u, tag + "_bwd_hidden")
        dwd, = _ffn_dw([a], dxb, 0.5, tag + "_dw_down")
        rs_down, tok = _exchange_start([parts(dwd)], False, "rs_start_" + tag + "_down")
        dwg_t, dwu_t = _ffn_dw([dg, du], hn, 1.0, tag + "_dw_gate_up", dep=tok)
        rs_gu, tok = _exchange_start([parts(dwg_t), parts(dwu_t)], False, "rs_start_" + tag + "_gate_up")
        dhn = _ffn_bwd_input(dg, du, wg_t, wu_t, tag + "_bwd_input", dep=tok)
        dx, dx_bf, dgain = _rms_bwd(dhn, x_in, gain, dres, "rms_" + tag + "_bwd")
        return dx, dx_bf, dgain, rs_gu + rs_down

    dx2, dx2b, d_ffn2_norm, rs_ffn2 = ffn_bwd(dx3b, hn2, g2, u2, a2, wg2, wu2, wd2, x2, ffn2_norm, dx3, "ffn2")

    dmerged = _mm_nt(dx2b, wout, F32, "w_out_bwd")
    dwout = _mm_tn(merged, dx2b, BF, "w_out_dw", tn_pref=1024)
    dpd, dpf, dgd, dgf, d_bd, d_bf = _merge_bwd(dmerged, pd, pf, proj, b_gate_dil, b_gate_fox, hd)
    dyd = _mm_nt(dpd, wpd, BF, "proj_dil_bwd")
    dyf = _mm_nt(dpf, wpf, BF, "proj_fox_bwd")
    dwpd = _mm_tn(yd, dpd, BF, "proj_dil_dw", tn_pref=1024)
    dwpf = _mm_tn(yf, dpf, BF, "proj_fox_dw", tn_pref=1024)
    dwpd_c = dwpd.reshape(hd, N_DEV, d // N_DEV).transpose(1, 0, 2)
    dwpf_c = dwpf.reshape(hd, N_DEV, d // N_DEV).transpose(1, 0, 2)
    dwout_c = dwout.reshape(N_DEV, d // N_DEV, d)
    rs_mix, tok = _exchange_start([dwout_c, dwpd_c, dwpf_c], False, "rs_start_mixer")

    dqd, dl_d = _attn_bwd_dq("dil", qd, kd, vd, yd, dyd, lse_d, dil_bias, tq, "attn_dil_dq", dep=tok)
    dkd, dvd = _attn_bwd_dkv("dil", qd, kd, vd, dyd, lse_d_row, dl_d, dil_bias_t, None, tq, "attn_dil_dkv")
    dqf, dl_f = _attn_bwd_dq("fox", qf, kf, vf, yf, dyf, lse_f, c_row, tq, "attn_fox_dq")
    dkf, dvf, dc = _attn_bwd_dkv("fox", qf, kf, vf, dyf, lse_f_row, dl_f, c_rep, c_row, tq, "attn_fox_dkv")
    dc_pad = jnp.pad(dc[:, :, 0, :].reshape(nh, t).T, ((0, 0), (0, LANE - nh)))
    dlogf = _cumsum_rows(dc_pad, True, "revcumsum_dc")
    dproj, d_bforget = _assemble_dproj(dqd, dkd, dvd, dqf, dkf, dvf, dgd, dgf, dlogf, proj, tables, bf_pad, scale)

    dwin_p = _mm_tn(hm_t, dproj, BF, "w_in_dw", tk_pref=DW_ROWS, a_transposed=True)

    def perm_col(c):
        if c < 6 * hd:
            return c
        return c + 2 * d if c < 6 * hd + n_f else c - n_f

    shards = []
    for j in range(N_DEV):
        cuts = sorted({j * cols, (j + 1) * cols} | {c for c in (6 * hd, 6 * hd + n_f) if j * cols < c < (j + 1) * cols})
        shards.append(jnp.concatenate([dwin_p[:, perm_col(lo):perm_col(lo) + hi - lo]
                                       for lo, hi in zip(cuts[:-1], cuts[1:])], axis=1))
    dwin_c = jnp.stack(shards)
    rs_win, tok = _exchange_start([dwin_c], False, "rs_start_w_in")
    dx1, dx1b, d_mix_norm = _mm_nt(dproj, win_p, F32, "w_in_bwd", tn_pref=d, tk_pref=1152,
                                   rms=(x1, mix_norm, dx2), dep=tok)

    grad_x, _, d_ffn1_norm, rs_ffn1 = ffn_bwd(dx1b, hn1, g1, u1, a1, wg1, wu1, wd1, x2d, ffn1_norm, dx1, "ffn1")

    def update(handles, names, after, tag):
        recvs = _exchange_wait(handles, False, after, "rs_wait_" + tag)
        res = {}
        for recv, n in zip(recvs, names):
            turn = rows if n.endswith(("w_gate", "w_up")) else (lambda a: a)
            w, m, v = (turn(a)[0] for a in wmv[n])
            res[n] = tuple(turn(o[None]) for o in _adam_from_partials(recv, w, m, v, "adam_" + n))
        return res, res[names[-1]][0]

    wmv = {
        "ffn1_w_gate": (ffn1_w_gate, m_ffn1_w_gate, v_ffn1_w_gate),
        "ffn1_w_up": (ffn1_w_up, m_ffn1_w_up, v_ffn1_w_up),
        "ffn1_w_down": (ffn1_w_down, m_ffn1_w_down, v_ffn1_w_down),
        "w_in": (w_in, m_w_in, v_w_in),
        "w_proj_dil": (w_proj_dil, m_w_proj_dil, v_w_proj_dil),
        "w_proj_fox": (w_proj_fox, m_w_proj_fox, v_w_proj_fox),
        "w_out": (w_out, m_w_out, v_w_out),
        "ffn2_w_gate": (ffn2_w_gate, m_ffn2_w_gate, v_ffn2_w_gate),
        "ffn2_w_up": (ffn2_w_up, m_ffn2_w_up, v_ffn2_w_up),
        "ffn2_w_down": (ffn2_w_down, m_ffn2_w_down, v_ffn2_w_down),
    }
    big = {}
    after = grad_x
    for handles, names, tag in [
            (rs_ffn2, ["ffn2_w_gate", "ffn2_w_up", "ffn2_w_down"], "ffn2"),
            (rs_mix, ["w_out", "w_proj_dil", "w_proj_fox"], "mixer"),
            (rs_win, ["w_in"], "w_in"),
            (rs_ffn1, ["ffn1_w_gate", "ffn1_w_up", "ffn1_w_down"], "ffn1")]:
        res, after = update(handles, names, after, tag)
        big.update(res)

    def lanes(a):
        a = a.reshape(1, -1)
        return jnp.pad(a, ((0, 0), (0, d - a.shape[1])))

    small_names = ["ffn1_norm", "mix_norm", "b_gate_dil", "b_gate_fox", "ffn2_norm", "final_norm", "b_forget"]
    small_g = [d_ffn1_norm, d_mix_norm, d_bd, d_bf, d_ffn2_norm, d_final, d_bforget[:, :n_f]]
    small_w = [ffn1_norm, mix_norm, b_gate_dil, b_gate_fox, ffn2_norm, final_norm, b_forget]
    small_m = [m_ffn1_norm, m_mix_norm, m_b_gate_dil, m_b_gate_fox, m_ffn2_norm, m_final_norm, m_b_forget]
    small_v = [v_ffn1_norm, v_mix_norm, v_b_gate_dil, v_b_gate_fox, v_ffn2_norm, v_final_norm, v_b_forget]
    pack = lambda arrs, last: jnp.concatenate([lanes(a) for a in arrs] + [last], axis=0)
    g_all = _allreduce_small(pack(small_g, loss_lanes))
    zero_row = jnp.zeros((1, d), F32)
    one_row = jnp.ones((1, d), F32)
    s_delta, s_m, s_v = _adam_small(g_all, pack(small_w, zero_row), pack(small_m, zero_row), pack(small_v, one_row))
    loss = g_all[len(small_names), 0]

    def unpack(packed, i, like):
        return packed[i, :like.size].reshape(like.shape)

    small = {}
    for i, (n, w) in enumerate(zip(small_names, small_w)):
        small[n] = (unpack(g_all, i, w), unpack(s_delta, i, w), unpack(s_m, i, w), unpack(s_v, i, w))

    order = ["ffn1_norm", "ffn1_w_gate", "ffn1_w_up", "ffn1_w_down", "mix_norm", "w_in", "b_forget", "b_gate_dil",
             "b_gate_fox", "w_proj_dil", "w_proj_fox", "w_out", "ffn2_norm", "ffn2_w_gate", "ffn2_w_up",
             "ffn2_w_down", "final_norm"]
    res = {**big, **small}
    outs = [loss, grad_x[None]]
    for slot in range(4):
        outs += [res[n][slot] for n in order]
    return tuple(outs)
```

```python
import jax
import jax.numpy as jnp
from jax import lax
from jax.experimental import pallas as pl
from jax.experimental.pallas import tpu as pltpu

BF = jnp.bfloat16
F32 = jnp.float32
MESH = pl.DeviceIdType.MESH
N_DEV = 8

HEAD_DIM = 128
ROPE_DIM = HEAD_DIM // 4
ROPE_HALF = ROPE_DIM // 2
ROPE_THETA = 500000.0
NORM_EPS = 1e-6
DIL_PATTERNS = ((128, 1), (512, 4), (2048, 16))
MAX_WINDOW = 2048
LANE = 128
NEG = -1e30
F_PAD = 512
W_IN_COLS = 1536

ADAM_LR = 0.001
ADAM_B1 = 0.9
ADAM_B2 = 0.999
ADAM_EPS = 1e-08
ADAM_WD = 0.01
ADAM_STEP = 10

VMEM_LIMIT_BYTES = 56 * 1024 * 1024
FFN_ROWS = 1024
DW_ROWS = 2048
ANY = pl.BlockSpec(memory_space=pl.ANY)

NN = (((1,), (0,)), ((), ()))
NT = (((1,), (1,)), ((), ()))
TN = (((0,), (0,)), ((), ()))


def _dot(a, b, dn=NN):
    return lax.dot_general(a, b, dn, preferred_element_type=F32)


def _sig(x):
    return 0.5 + 0.5 * jnp.tanh(0.5 * x)


def _tile(n, pref, align):
    best = None
    t = align
    while t <= min(n, pref):
        if n % t == 0:
            best = t
        t += align
    return n if best is None else best


def _params():
    return pltpu.CompilerParams(vmem_limit_bytes=VMEM_LIMIT_BYTES)


def _call(body, args, dep=None, **kw):
    if dep is not None:
        n_in = len(args)
        inner = body

        def body(*refs):
            inner(*refs[:n_in], *refs[n_in + 1:])

        kw["in_specs"] = list(kw["in_specs"]) + [ANY]
        args = list(args) + [dep]
    return pl.pallas_call(body, **kw)(*args)


def _peers():
    x, y, c = lax.axis_index("x"), lax.axis_index("y"), lax.axis_index("c")
    me = 4 * x + 2 * y + c
    peers = []
    for k in range(1, N_DEV):
        px = 1 - x if (k >> 2) & 1 else x
        py = 1 - y if (k >> 1) & 1 else y
        pc = 1 - c if k & 1 else c
        peers.append((k, (px, py, pc), 4 * px + 2 * py + pc))
    return me, peers


HBM = pl.BlockSpec(memory_space=pltpu.HBM)
SEM = pl.BlockSpec(memory_space=pltpu.SEMAPHORE)
EFFECT = pltpu.SideEffectType.DATAFLOW_SIDE_EFFECTING


def _exchange_copy(gather, src_ref, land_ref, send_sems, recv_sems, me, k, peer, peer_flat, landing):
    return pltpu.make_async_remote_copy(
        src_ref=src_ref if gather else src_ref.at[peer_flat], dst_ref=land_ref.at[landing],
        send_sem=send_sems.at[k], recv_sem=recv_sems.at[k], device_id=peer, device_id_type=MESH)


ALL_PEERS = (1, 2, 3, 4, 5, 6, 7)
SIBLING = 1
SAME_CORE = (2, 4, 6)
FIRST_LEVEL = (SIBLING,) + SAME_CORE


def _exchange_start(srcs, gather, name, dep=None, ks=ALL_PEERS):
    n = len(srcs)
    extra = [] if dep is None else [dep]

    def body(*refs):
        src_refs, land_refs = refs[:n], refs[n:2 * n]
        refs = refs[2 * n + len(extra):]
        send_refs, recv_refs = refs[:n], refs[n:2 * n]
        token = refs[4 * n]
        me, peers = _peers()
        for i in range(n):
            for k, peer, peer_flat in peers:
                if k in ks:
                    _exchange_copy(gather, src_refs[i], land_refs[i], send_refs[i], recv_refs[i],
                                   me, k, peer, peer_flat, me).start()
        token[...] = jnp.zeros_like(token)

    lands = [lax.empty((N_DEV,) + s.shape[-2:], s.dtype) for s in srcs]
    sems = [pltpu.SemaphoreType.DMA((N_DEV,)) for _ in range(2 * n)]
    out = pl.pallas_call(
        body, name=name,
        out_shape=tuple(sems) + tuple(pltpu.HBM(a.shape, a.dtype) for a in list(srcs) + lands)
        + (jax.ShapeDtypeStruct((8, LANE), F32),),
        in_specs=[HBM] * (2 * n) + [ANY] * len(extra),
        out_specs=tuple([SEM] * (2 * n) + [HBM] * (2 * n) + [pl.BlockSpec(memory_space=pltpu.VMEM)]),
        input_output_aliases={i: 2 * n + i for i in range(2 * n)},
        compiler_params=pltpu.CompilerParams(has_side_effects=EFFECT),
    )(*[pltpu.with_memory_space_constraint(a, pltpu.HBM) for a in list(srcs) + lands], *extra)
    handles = [(out[2 * n + i], out[3 * n + i], out[i], out[n + i]) for i in range(n)]
    return handles, out[4 * n]


def _exchange_wait(handles, gather, after, name):
    n = len(handles)

    def body(*refs):
        src_refs, land_refs = refs[:n], refs[n:2 * n]
        send_refs, recv_refs = refs[2 * n:3 * n], refs[3 * n:4 * n]
        me, peers = _peers()
        for i in range(n):
            for k, peer, peer_flat in peers:
                cp = _exchange_copy(gather, src_refs[i], land_refs[i], send_refs[i], recv_refs[i],
                                    me, k, peer, peer_flat, peer_flat)
                cp.wait_send()
                cp.wait_recv()

    srcs = [h[0] for h in handles]
    lands = [h[1] for h in handles]
    out = pl.pallas_call(
        body, name=name,
        out_shape=tuple(pltpu.HBM(a.shape, a.dtype) for a in srcs + lands),
        in_specs=[HBM] * (2 * n) + [SEM] * (2 * n) + [ANY],
        out_specs=tuple([HBM] * (2 * n)),
        input_output_aliases={i: i for i in range(2 * n)},
        compiler_params=pltpu.CompilerParams(has_side_effects=EFFECT),
    )(*srcs, *lands, *[h[2] for h in handles], *[h[3] for h in handles], after)
    me = 4 * lax.axis_index("x") + 2 * lax.axis_index("y") + lax.axis_index("c")
    filled = []
    for src, land in zip(out[:n], out[n:]):
        own = src[None] if gather else lax.dynamic_slice_in_dim(src, me, 1, axis=0)
        filled.append(lax.dynamic_update_slice_in_dim(land, own, me, axis=0))
    return filled


def _gather_relay(handles, after, name):
    n = len(handles)

    def body(*refs):
        land_refs, recv_refs = refs[:n], refs[n:2 * n]
        refs = refs[2 * n + 1:]
        send2_refs, recv2_refs = refs[n:2 * n], refs[2 * n:3 * n]
        me, peers = _peers()
        sibling = peers[SIBLING - 1][1]
        for i in range(n):
            for k, peer, peer_flat in peers:
                if k in SAME_CORE:
                    block = land_refs[i].at[peer_flat]
                    pltpu.make_async_remote_copy(
                        src_ref=block, dst_ref=block, send_sem=send2_refs[i].at[k], recv_sem=recv_refs[i].at[k],
                        device_id=peer, device_id_type=MESH).wait_recv()
                    pltpu.make_async_remote_copy(
                        src_ref=block, dst_ref=block, send_sem=send2_refs[i].at[k], recv_sem=recv2_refs[i].at[k],
                        device_id=sibling, device_id_type=MESH).start()

    lands = [h[1] for h in handles]
    sems = [pltpu.SemaphoreType.DMA((N_DEV,)) for _ in range(2 * n)]
    out = pl.pallas_call(
        body, name=name,
        out_shape=tuple(pltpu.HBM(a.shape, a.dtype) for a in lands) + tuple(sems),
        in_specs=[HBM] * n + [SEM] * n + [ANY],
        out_specs=tuple([HBM] * n + [SEM] * (2 * n)),
        input_output_aliases={i: i for i in range(n)},
        compiler_params=pltpu.CompilerParams(has_side_effects=EFFECT),
    )(*lands, *[h[3] for h in handles], after)
    return [(h[0], out[i], h[2], h[3], out[n + i], out[2 * n + i]) for i, h in enumerate(handles)]


def _gather_wait(handles, after, name):
    n = len(handles)

    def body(*refs):
        src_refs, land_refs = refs[:n], refs[n:2 * n]
        send_refs, recv_refs = refs[2 * n:3 * n], refs[3 * n:4 * n]
        send2_refs, recv2_refs = refs[4 * n:5 * n], refs[5 * n:6 * n]
        me, peers = _peers()
        _, sibling, sibling_flat = peers[SIBLING - 1]
        for i in range(n):
            for k, peer, peer_flat in peers:
                if k in FIRST_LEVEL:
                    cp = _exchange_copy(True, src_refs[i], land_refs[i], send_refs[i], recv_refs[i],
                                        me, k, peer, peer_flat, peer_flat)
                    cp.wait_send()
                    if k == SIBLING:
                        cp.wait_recv()
                if k in SAME_CORE:
                    mine = land_refs[i].at[peer_flat]
                    theirs = land_refs[i].at[peer_flat ^ SIBLING]
                    cp = pltpu.make_async_remote_copy(
                        src_ref=mine, dst_ref=theirs, send_sem=send2_refs[i].at[k], recv_sem=recv2_refs[i].at[k],
                        device_id=sibling, device_id_type=MESH)
                    cp.wait_send()
                    cp.wait_recv()

    srcs = [h[0] for h in handles]
    lands = [h[1] for h in handles]
    out = pl.pallas_call(
        body, name=name,
        out_shape=tuple(pltpu.HBM(a.shape, a.dtype) for a in srcs + lands),
        in_specs=[HBM] * (2 * n) + [SEM] * (4 * n) + [ANY],
        out_specs=tuple([HBM] * (2 * n)),
        input_output_aliases={i: i for i in range(2 * n)},
        compiler_params=pltpu.CompilerParams(has_side_effects=EFFECT),
    )(*srcs, *lands, *[h[2] for h in handles], *[h[3] for h in handles],
      *[h[4] for h in handles], *[h[5] for h in handles], after)
    me = 4 * lax.axis_index("x") + 2 * lax.axis_index("y") + lax.axis_index("c")
    return [lax.dynamic_update_slice_in_dim(land, src[None], me, axis=0) for src, land in zip(out[:n], out[n:])]


def _allreduce_small(p):
    rows, d = p.shape

    def body(p_ref, o_ref, recv_ref, send_sems, recv_sems):
        me, peers = _peers()
        recv_ref[me] = p_ref[...]
        sends = []
        for k, peer, peer_flat in peers:
            cp = pltpu.make_async_remote_copy(
                src_ref=p_ref, dst_ref=recv_ref.at[me],
                send_sem=send_sems.at[k], recv_sem=recv_sems.at[k],
                device_id=peer, device_id_type=MESH)
            cp.start()
            sends.append(cp)
        for k, peer, peer_flat in peers:
            pltpu.make_async_remote_copy(
                src_ref=p_ref, dst_ref=recv_ref.at[peer_flat],
                send_sem=send_sems.at[k], recv_sem=recv_sems.at[k],
                device_id=peer, device_id_type=MESH).wait_recv()
        for cp in sends:
            cp.wait_send()
        acc = recv_ref[0]
        for s in range(1, N_DEV):
            acc = acc + recv_ref[s]
        is_loss = lax.broadcasted_iota(jnp.int32, (rows, d), 0) == rows - 1
        total = jnp.sum(jnp.where(is_loss, acc, 0.0))
        o_ref[...] = jnp.where(is_loss, total, acc)

    return pl.pallas_call(
        body, name="allreduce_small",
        out_shape=jax.ShapeDtypeStruct((rows, d), F32),
        in_specs=[pl.BlockSpec(memory_space=pltpu.VMEM)],
        out_specs=pl.BlockSpec(memory_space=pltpu.VMEM),
        scratch_shapes=[pltpu.VMEM((N_DEV, rows, d), F32),
                        pltpu.SemaphoreType.DMA((N_DEV,)), pltpu.SemaphoreType.DMA((N_DEV,))],
    )(p)


def _adam_math(w, g, m, v):
    m2 = ADAM_B1 * m + (1.0 - ADAM_B1) * g
    v2 = ADAM_B2 * v + (1.0 - ADAM_B2) * (g * g)
    m_hat = m2 / (1.0 - ADAM_B1 ** ADAM_STEP)
    v_hat = v2 / (1.0 - ADAM_B2 ** ADAM_STEP)
    delta = -ADAM_LR * (m_hat / (jnp.sqrt(v_hat) + ADAM_EPS) + ADAM_WD * w)
    return delta, m2, v2


def _adam_from_partials(parts, w, m, v, name):
    r, c = w.shape
    tr = _tile(r, 256, 16)

    def body(p_ref, w_ref, m_ref, v_ref, g_out, d_out, m_out, v_out):
        g = p_ref[0].astype(F32)
        for s in range(1, N_DEV):
            g = g + p_ref[s].astype(F32)
        delta, m2, v2 = _adam_math(w_ref[...], g, m_ref[...], v_ref[...])
        g_out[...] = g
        d_out[...] = delta
        m_out[...] = m2
        v_out[...] = v2

    blk = pl.BlockSpec((tr, c), lambda i: (i, 0))
    out = jax.ShapeDtypeStruct((r, c), F32)
    return pl.pallas_call(
        body, name=name, grid=(r // tr,),
        in_specs=[pl.BlockSpec((N_DEV, tr, c), lambda i: (0, i, 0)), blk, blk, blk],
        out_specs=[blk, blk, blk, blk], out_shape=[out, out, out, out],
        compiler_params=_params(),
    )(parts, w, m, v)


def _adam_small(g, w, m, v):
    def body(g_ref, w_ref, m_ref, v_ref, d_out, m_out, v_out):
        delta, m2, v2 = _adam_math(w_ref[...], g_ref[...], m_ref[...], v_ref[...])
        d_out[...] = delta
        m_out[...] = m2
        v_out[...] = v2

    out = jax.ShapeDtypeStruct(g.shape, F32)
    return pl.pallas_call(body, name="adam_small", out_shape=[out, out, out])(g, w, m, v)


def _rms_fwd(x, gain, name, dep=None, with_transpose=False):
    t, d = x.shape
    tr = _tile(t, 256, LANE)

    def body(x_ref, g_ref, o_ref, *ot_ref):
        xv = x_ref[...]
        r = lax.rsqrt(jnp.mean(xv * xv, axis=-1, keepdims=True) + NORM_EPS)
        y = xv * r * g_ref[...]
        o_ref[...] = y.astype(BF)
        if with_transpose:
            ot_ref[0][...] = jnp.transpose(y).astype(BF)

    out_specs = [pl.BlockSpec((tr, d), lambda i: (i, 0))]
    out_shape = [jax.ShapeDtypeStruct((t, d), BF)]
    if with_transpose:
        out_specs.append(pl.BlockSpec((d, tr), lambda i: (0, i)))
        out_shape.append(jax.ShapeDtypeStruct((d, t), BF))
    return _call(
        body, [x, gain], dep=dep, name=name, grid=(t // tr,),
        in_specs=[pl.BlockSpec((tr, d), lambda i: (i, 0)), pl.BlockSpec((1, d), lambda i: (0, 0))],
        out_specs=out_specs, out_shape=out_shape, compiler_params=_params(),
    )


def _rms_vjp(xv, gain, dy):
    r = lax.rsqrt(jnp.mean(xv * xv, axis=-1, keepdims=True) + NORM_EPS)
    xhat = xv * r
    dxhat = dy * gain
    dx = r * (dxhat - xhat * jnp.mean(dxhat * xhat, axis=-1, keepdims=True))
    dgain = jnp.sum(dy * xhat, axis=0, keepdims=True)
    return dx, dgain


def _loss_head(x, gain, target):
    t, d = x.shape
    tr = _tile(t, 256, 16)

    def body(x_ref, g_ref, t_ref, dx_ref, dxb_ref, dg_ref, loss_ref):
        xv = x_ref[...]
        gain = g_ref[...]
        r = lax.rsqrt(jnp.mean(xv * xv, axis=-1, keepdims=True) + NORM_EPS)
        err = xv * r * gain - t_ref[...]
        dx, dgain = _rms_vjp(xv, gain, err * (1.0 / d))
        dx_ref[...] = dx
        dxb_ref[...] = dx.astype(BF)

        @pl.when(pl.program_id(0) == 0)
        def _():
            dg_ref[...] = jnp.zeros_like(dg_ref)
            loss_ref[...] = jnp.zeros_like(loss_ref)

        dg_ref[...] += dgain
        loss_ref[...] += jnp.sum(err * err, axis=0, keepdims=True) * (0.5 / d)

    row = pl.BlockSpec((tr, d), lambda i: (i, 0))
    vec = pl.BlockSpec((1, d), lambda i: (0, 0))
    return pl.pallas_call(
        body, name="loss_head", grid=(t // tr,),
        in_specs=[row, vec, row], out_specs=[row, row, vec, vec],
        out_shape=[jax.ShapeDtypeStruct((t, d), F32), jax.ShapeDtypeStruct((t, d), BF),
                   jax.ShapeDtypeStruct((1, d), F32), jax.ShapeDtypeStruct((1, d), F32)],
        compiler_params=_params(),
    )(x, gain, target)


def _mm_nn(a, b, out_dtype, name, residual=None, tm_pref=512, tn_pref=1152):
    m, k = a.shape
    n = b.shape[1]
    tm, tn = _tile(m, tm_pref, 16), _tile(n, tn_pref, LANE)

    def body(*refs):
        if residual is None:
            a_ref, b_ref, o_ref = refs
            o_ref[...] = _dot(a_ref[...], b_ref[...]).astype(out_dtype)
        else:
            a_ref, b_ref, r_ref, o_ref = refs
            o_ref[...] = (r_ref[...] + _dot(a_ref[...], b_ref[...])).astype(out_dtype)

    in_specs = [pl.BlockSpec((tm, k), lambda j, i: (i, 0)), pl.BlockSpec((k, tn), lambda j, i: (0, j))]
    args = [a, b]
    if residual is not None:
        in_specs.append(pl.BlockSpec((tm, tn), lambda j, i: (i, j)))
        args.append(residual)
    return pl.pallas_call(
        body, name=name, grid=(n // tn, m // tm), in_specs=in_specs,
        out_specs=pl.BlockSpec((tm, tn), lambda j, i: (i, j)),
        out_shape=jax.ShapeDtypeStruct((m, n), out_dtype), compiler_params=_params(),
    )(*args)


def _rms_bwd_tail(dy_ref, first, x_ref, g_ref, dres_ref, dx_ref, dxb_ref, dg_ref):
    @pl.when(first)
    def _():
        dg_ref[...] = jnp.zeros_like(dg_ref)

    gain = g_ref[...]
    for r in range(0, dy_ref.shape[0], LANE):
        rows = pl.ds(r, min(LANE, dy_ref.shape[0] - r))
        dx, dgain = _rms_vjp(x_ref[rows, :], gain, dy_ref[rows, :])
        dx = dx + dres_ref[rows, :]
        dx_ref[rows, :] = dx
        dxb_ref[rows, :] = dx.astype(BF)
        dg_ref[...] += dgain


def _mm_nt(a, b, out_dtype, name, tm_pref=512, tn_pref=1024, tk_pref=2048, rms=None, dep=None):
    m, k = a.shape
    n = b.shape[0]
    tm, tn, tk = _tile(m, tm_pref, 16), _tile(n, tn_pref, LANE), _tile(k, tk_pref, LANE)
    nk = k // tk
    assert rms is None or tn == n

    def body(*refs):
        if rms is None:
            a_ref, b_ref, o_ref, acc_ref = refs
        else:
            a_ref, b_ref, x_ref, g_ref, dres_ref, dx_ref, dxb_ref, dg_ref, acc_ref = refs
        kk = pl.program_id(2)

        @pl.when(kk == 0)
        def _():
            acc_ref[...] = jnp.zeros_like(acc_ref)

        acc_ref[...] += _dot(a_ref[...], b_ref[...], NT)

        @pl.when(kk == nk - 1)
        def _():
            if rms is None:
                o_ref[...] = acc_ref[...].astype(out_dtype)
            else:
                _rms_bwd_tail(acc_ref, pl.program_id(1) == 0, x_ref, g_ref, dres_ref, dx_ref, dxb_ref, dg_ref)

    in_specs = [pl.BlockSpec((tm, tk), lambda j, i, kk: (i, kk)), pl.BlockSpec((tn, tk), lambda j, i, kk: (j, kk))]
    row = pl.BlockSpec((tm, tn), lambda j, i, kk: (i, j))
    if rms is None:
        args, out_specs, out_shape = [a, b], row, jax.ShapeDtypeStruct((m, n), out_dtype)
    else:
        vec = pl.BlockSpec((1, n), lambda j, i, kk: (0, 0))
        args, in_specs = [a, b, *rms], in_specs + [row, vec, row]
        out_specs = [row, row, vec]
        out_shape = [jax.ShapeDtypeStruct((m, n), F32), jax.ShapeDtypeStruct((m, n), BF),
                     jax.ShapeDtypeStruct((1, n), F32)]
    return _call(
        body, args, dep=dep, name=name, grid=(n // tn, m // tm, nk), in_specs=in_specs, out_specs=out_specs,
        out_shape=out_shape, scratch_shapes=[pltpu.VMEM((tm, tn), F32)], compiler_params=_params(),
    )


def _mm_tn(a, b, out_dtype, name, tn_pref=1152, tk_pref=512, a_transposed=False):
    (k, t) = a.shape if a_transposed else a.shape[::-1]
    n = b.shape[1]
    tn, tk = _tile(n, tn_pref, LANE), _tile(t, tk_pref, LANE if a_transposed else 16)
    nt = t // tk

    def body(a_ref, b_ref, o_ref, acc_ref):
        tt = pl.program_id(1)

        @pl.when(tt == 0)
        def _():
            acc_ref[...] = jnp.zeros_like(acc_ref)

        acc_ref[...] += _dot(a_ref[...], b_ref[...], NN if a_transposed else TN)

        @pl.when(tt == nt - 1)
        def _():
            o_ref[...] = acc_ref[...].astype(out_dtype)

    if a_transposed:
        a_spec = pl.BlockSpec((k, tk), lambda j, tt: (0, tt))
    else:
        a_spec = pl.BlockSpec((tk, k), lambda j, tt: (tt, 0))
    return pl.pallas_call(
        body, name=name, grid=(n // tn, nt),
        in_specs=[a_spec, pl.BlockSpec((tk, tn), lambda j, tt: (tt, j))],
        out_specs=pl.BlockSpec((k, tn), lambda j, tt: (0, j)),
        out_shape=jax.ShapeDtypeStruct((k, n), out_dtype),
        scratch_shapes=[pltpu.VMEM((k, tn), F32)], compiler_params=_params(),
    )(a, b)


FFN_COLS = 512


FFN_ROWS_WIDE = 2048


def _ffn_tiles(t, fc, rows=FFN_ROWS):
    return _tile(t, rows, 16), _tile(fc, FFN_COLS, LANE)


def _slabs(tm, rows=256):
    step = rows if tm % rows == 0 else tm
    return [pl.ds(r, step) for r in range(0, tm, step)]


def _ffn_gate_up(hn, wg_t, wu_t, name):
    t, d = hn.shape
    fc = wg_t.shape[0]
    tm, tn = _ffn_tiles(t, fc, FFN_ROWS_WIDE)

    def body(h_ref, wg_ref, wu_ref, g_ref, u_ref, a_ref):
        for rows in _slabs(tm):
            h = h_ref[rows, :]
            g = _dot(h, wg_ref[...], NT)
            u = _dot(h, wu_ref[...], NT)
            g_ref[rows, :] = g.astype(BF)
            u_ref[rows, :] = u.astype(BF)
            a_ref[rows, :] = (g * _sig(g) * u).astype(BF)

    wspec = pl.BlockSpec((tn, d), lambda j, i: (j, 0))
    hid = pl.BlockSpec((tm, tn), lambda j, i: (i, j))
    out = jax.ShapeDtypeStruct((t, fc), BF)
    return pl.pallas_call(
        body, name=name, grid=(fc // tn, t // tm),
        in_specs=[pl.BlockSpec((tm, d), lambda j, i: (i, 0)), wspec, wspec],
        out_specs=[hid, hid, hid], out_shape=[out, out, out], compiler_params=_params(),
    )(hn, wg_t, wu_t)


def _ffn_gate(hn, wg_t, name):
    t, d = hn.shape
    fc = wg_t.shape[0]
    tm, tn = _ffn_tiles(t, fc)

    def body(h_ref, wg_ref, g_ref):
        g_ref[...] = _dot(h_ref[...], wg_ref[...], NT)

    return pl.pallas_call(
        body, name=name, grid=(fc // tn, t // tm),
        in_specs=[pl.BlockSpec((tm, d), lambda j, i: (i, 0)), pl.BlockSpec((tn, d), lambda j, i: (j, 0))],
        out_specs=pl.BlockSpec((tm, tn), lambda j, i: (i, j)),
        out_shape=jax.ShapeDtypeStruct((t, fc), F32), compiler_params=_params(),
    )(hn, wg_t)


def _ffn_up_act(hn, wu_t, g, name):
    t, d = hn.shape
    fc = wu_t.shape[0]
    tm, tn = _ffn_tiles(t, fc, FFN_ROWS_WIDE)

    def body(h_ref, wu_ref, g_ref, gb_ref, u_ref, a_ref):
        for rows in _slabs(tm):
            u = _dot(h_ref[rows, :], wu_ref[...], NT)
            gv = g_ref[rows, :]
            gb_ref[rows, :] = gv.astype(BF)
            u_ref[rows, :] = u.astype(BF)
            a_ref[rows, :] = (gv * _sig(gv) * u).astype(BF)

    hid = pl.BlockSpec((tm, tn), lambda j, i: (i, j))
    out = jax.ShapeDtypeStruct((t, fc), BF)
    return pl.pallas_call(
        body, name=name, grid=(fc // tn, t // tm),
        in_specs=[pl.BlockSpec((tm, d), lambda j, i: (i, 0)), pl.BlockSpec((tn, d), lambda j, i: (j, 0)), hid],
        out_specs=[hid, hid, hid], out_shape=[out, out, out], compiler_params=_params(),
    )(hn, wu_t, g)


def _ffn_down(act, wd, xres, name):
    t, fc = act.shape
    d = wd.shape[1]
    tm, tk = _ffn_tiles(t, fc)

    def body(a_ref, w_ref, x_ref, o_ref):
        @pl.when(pl.program_id(1) == 0)
        def _():
            o_ref[...] = x_ref[...]

        o_ref[...] += 0.5 * _dot(a_ref[...], w_ref[...])

    row = pl.BlockSpec((tm, d), lambda i, k: (i, 0))
    return pl.pallas_call(
        body, name=name, grid=(t // tm, fc // tk),
        in_specs=[pl.BlockSpec((tm, tk), lambda i, k: (i, k)), pl.BlockSpec((tk, d), lambda i, k: (k, 0)), row],
        out_specs=row, out_shape=jax.ShapeDtypeStruct((t, d), F32), compiler_params=_params(),
    )(act, wd, xres)


def _ffn_bwd_hidden(dxb, wd, g, u, name):
    t, d = dxb.shape
    fc = wd.shape[0]
    tm, tn = _ffn_tiles(t, fc, FFN_ROWS_WIDE)

    def body(dx_ref, w_ref, g_ref, u_ref, dg_ref, du_ref):
        for rows in _slabs(tm):
            dh = 0.5 * _dot(dx_ref[rows, :], w_ref[...], NT)
            gv = g_ref[rows, :].astype(F32)
            uv = u_ref[rows, :].astype(F32)
            s = _sig(gv)
            dg_ref[rows, :] = (dh * uv * (s * (1.0 + gv * (1.0 - s)))).astype(BF)
            du_ref[rows, :] = (dh * (gv * s)).astype(BF)

    hid = pl.BlockSpec((tm, tn), lambda i, j: (i, j))
    out = jax.ShapeDtypeStruct((t, fc), BF)
    return pl.pallas_call(
        body, name=name, grid=(t // tm, fc // tn),
        in_specs=[pl.BlockSpec((tm, d), lambda i, j: (i, 0)), pl.BlockSpec((tn, d), lambda i, j: (j, 0)), hid, hid],
        out_specs=[hid, hid], out_shape=[out, out], compiler_params=_params(),
    )(dxb, wd, g, u)


def _ffn_dw(lhs, rhs, scale, name, dep=None):
    n = len(lhs)
    t, fc = lhs[0].shape
    d = rhs.shape[1]
    tk, tn = _tile(t, DW_ROWS, 16), _tile(fc, FFN_COLS, LANE)
    nt = t // tk

    def body(*refs):
        l_refs, r_ref, o_refs, acc_refs = refs[:n], refs[n], refs[n + 1:2 * n + 1], refs[2 * n + 1:]
        tt = pl.program_id(1)
        r = r_ref[...]
        for l_ref, o_ref, acc_ref in zip(l_refs, o_refs, acc_refs):
            @pl.when(tt == 0)
            def _():
                acc_ref[...] = jnp.zeros_like(acc_ref)

            acc_ref[...] += _dot(l_ref[...], r, TN)

            @pl.when(tt == nt - 1)
            def _():
                o_ref[...] = (scale * acc_ref[...]).astype(BF)

    lspec = pl.BlockSpec((tk, tn), lambda j, tt: (tt, j))
    ospec = pl.BlockSpec((tn, d), lambda j, tt: (j, 0))
    out = jax.ShapeDtypeStruct((fc, d), BF)
    return _call(
        body, [*lhs, rhs], dep=dep, name=name, grid=(fc // tn, nt),
        in_specs=[lspec] * n + [pl.BlockSpec((tk, d), lambda j, tt: (tt, 0))],
        out_specs=[ospec] * n, out_shape=[out] * n,
        scratch_shapes=[pltpu.VMEM((tn, d), F32)] * n, compiler_params=_params(),
    )


def _rms_bwd(dy, x, gain, dres, name):
    t, d = x.shape
    tr = _tile(t, 256, 16)

    def body(dy_ref, x_ref, g_ref, dres_ref, dx_ref, dxb_ref, dg_ref):
        _rms_bwd_tail(dy_ref, pl.program_id(0) == 0, x_ref, g_ref, dres_ref, dx_ref, dxb_ref, dg_ref)

    row = pl.BlockSpec((tr, d), lambda i: (i, 0))
    vec = pl.BlockSpec((1, d), lambda i: (0, 0))
    return pl.pallas_call(
        body, name=name, grid=(t // tr,),
        in_specs=[row, row, vec, row], out_specs=[row, row, vec],
        out_shape=[jax.ShapeDtypeStruct((t, d), F32), jax.ShapeDtypeStruct((t, d), BF),
                   jax.ShapeDtypeStruct((1, d), F32)],
        compiler_params=_params(),
    )(dy, x, gain, dres)


def _ffn_bwd_input(dg, du, wg_t, wu_t, name, dep=None):
    t, fc = dg.shape
    d = wg_t.shape[1]
    tm, tk = _ffn_tiles(t, fc)

    def body(dg_ref, du_ref, wg_ref, wu_ref, o_ref):
        @pl.when(pl.program_id(1) == 0)
        def _():
            o_ref[...] = jnp.zeros_like(o_ref)

        o_ref[...] += _dot(dg_ref[...], wg_ref[...]) + _dot(du_ref[...], wu_ref[...])

    hid = pl.BlockSpec((tm, tk), lambda i, k: (i, k))
    wspec = pl.BlockSpec((tk, d), lambda i, k: (k, 0))
    return _call(
        body, [dg, du, wg_t, wu_t], dep=dep, name=name, grid=(t // tm, fc // tk),
        in_specs=[hid, hid, wspec, wspec],
        out_specs=pl.BlockSpec((tm, d), lambda i, k: (i, 0)),
        out_shape=jax.ShapeDtypeStruct((t, d), F32), compiler_params=_params(),
    )


def _rope_tables(t):
    pos = jnp.arange(t, dtype=F32)
    inv_freq = ROPE_THETA ** (-jnp.arange(0, ROPE_DIM, 2, dtype=F32) / ROPE_DIM)
    ang = pos[:, None] * inv_freq[None, :]
    cos, sin = jnp.cos(ang), jnp.sin(ang)
    rest = HEAD_DIM - ROPE_DIM
    one = jnp.ones((t, rest), F32)
    zero_h = jnp.zeros((t, ROPE_HALF), F32)
    zero_r = jnp.zeros((t, rest), F32)
    c = jnp.concatenate([cos, cos, one], axis=1)
    s1 = jnp.concatenate([-sin, zero_h, zero_r], axis=1)
    s2 = jnp.concatenate([zero_h, sin, zero_r], axis=1)
    return c, s1, s2


def _rope(xh, c, s1, s2):
    return xh * c + pltpu.roll(xh, HEAD_DIM - ROPE_HALF, 1) * s1 + pltpu.roll(xh, ROPE_HALF, 1) * s2


def _rope_t(dh, c, s1, s2):
    return dh * c + pltpu.roll(dh * s1, ROPE_HALF, 1) + pltpu.roll(dh * s2, HEAD_DIM - ROPE_HALF, 1)


def _mixer_prep(proj, tables, bf_pad, hd, scale):
    t, np_ = proj.shape
    tr = _tile(t, 256, 16)
    nh = hd // HEAD_DIM
    nblk = hd // LANE
    f_blk = (np_ - F_PAD) // LANE

    def body(qd_ref, kd_ref, vd_ref, qf_ref, kf_ref, vf_ref, fl_ref, c_ref, s1_ref, s2_ref, b_ref,
             oqd, okd, ovd, oqf, okf, ovf, olog):
        c, s1, s2 = c_ref[...], s1_ref[...], s2_ref[...]
        for h in range(nh):
            sl = slice(h * HEAD_DIM, (h + 1) * HEAD_DIM)
            oqd[:, sl] = (_rope(qd_ref[:, sl], c, s1, s2) * scale).astype(BF)
            okd[:, sl] = _rope(kd_ref[:, sl], c, s1, s2).astype(BF)
        ovd[...] = vd_ref[...].astype(BF)
        oqf[...] = (qf_ref[...] * scale).astype(BF)
        okf[...] = kf_ref[...].astype(BF)
        ovf[...] = vf_ref[...].astype(BF)
        z = fl_ref[...] + b_ref[...]
        olog[...] = jnp.minimum(z, 0.0) - jnp.log(1.0 + jnp.exp(-jnp.abs(z)))

    def col(kblk):
        return pl.BlockSpec((tr, hd), lambda i, kblk=kblk: (i, kblk))

    lane_row = pl.BlockSpec((tr, LANE), lambda i: (i, 0))
    in_specs = [col(0), col(1), col(2), col(3), col(4), col(5),
                pl.BlockSpec((tr, LANE), lambda i: (i, f_blk)),
                lane_row, lane_row, lane_row, pl.BlockSpec((1, LANE), lambda i: (0, 0))]
    o = pl.BlockSpec((tr, hd), lambda i: (i, 0))
    ob = jax.ShapeDtypeStruct((t, hd), BF)
    del nblk
    return pl.pallas_call(
        body, name="mixer_prep", grid=(t // tr,), in_specs=in_specs,
        out_specs=[o, o, o, o, o, o, lane_row],
        out_shape=[ob, ob, ob, ob, ob, ob, jax.ShapeDtypeStruct((t, LANE), F32)],
        compiler_params=_params(),
    )(proj, proj, proj, proj, proj, proj, proj, *tables, bf_pad)


def _split3(x):
    x1 = x.astype(BF)
    r1 = x - x1.astype(F32)
    x2 = r1.astype(BF)
    x3 = (r1 - x2.astype(F32)).astype(BF)
    return x1, x2, x3


def _cumsum_rows(x, reverse, name):
    t, w = x.shape
    blk = LANE
    nb = t // blk

    def body(x_ref, o_ref):
        r = lax.broadcasted_iota(jnp.int32, (blk, blk), 0)
        c = lax.broadcasted_iota(jnp.int32, (blk, blk), 1)
        tri = jnp.where((c >= r) if reverse else (c <= r), 1.0, 0.0).astype(BF)

        def step(i, carry):
            b = (nb - 1 - i) if reverse else i
            off = pl.multiple_of(b * blk, blk)
            xb = x_ref[pl.ds(off, blk), :]
            x1, x2, x3 = _split3(xb)
            o_ref[pl.ds(off, blk), :] = _dot(tri, x1) + _dot(tri, x2) + _dot(tri, x3) + carry
            return carry + jnp.sum(xb, axis=0, keepdims=True)

        lax.fori_loop(0, nb, step, jnp.zeros((1, w), F32))

    return pl.pallas_call(body, name=name, out_shape=jax.ShapeDtypeStruct((t, w), F32),
                          compiler_params=_params())(x)


ATTN_ROWS = 16


def _dil_bias_tiles(tq):
    nbias = MAX_WINDOW // tq + 1
    b = lax.broadcasted_iota(jnp.int32, (nbias, tq, tq), 0)
    i = lax.broadcasted_iota(jnp.int32, (nbias, tq, tq), 1)
    j = lax.broadcasted_iota(jnp.int32, (nbias, tq, tq), 2)
    delta = b * tq + i - j
    mult = jnp.zeros((nbias, tq, tq), F32)
    for w, dil in DIL_PATTERNS:
        mult = mult + jnp.where((delta >= 0) & (delta <= w) & (delta % dil == 0), 1.0, 0.0)
    return jnp.where(mult > 0.0, jnp.log(jnp.maximum(mult, 1.0)), NEG)


def _rep(x, width):
    return jnp.tile(x, (1, width // LANE))


def _chunks(n_rows, fn):
    for c in range(n_rows // ATTN_ROWS):
        fn(c * ATTN_ROWS)


def _causal(r0, tq, transposed):
    a = lax.broadcasted_iota(jnp.int32, (ATTN_ROWS, tq), 0) + r0
    b = lax.broadcasted_iota(jnp.int32, (ATTN_ROWS, tq), 1)
    return (a <= b) if transposed else (b <= a)


def _rows8(x):
    return jnp.transpose(x)[:8, :]


def _attn_fwd(mode, q, k, v, bias, tq, name):
    t, hd = q.shape
    nh = hd // HEAD_DIM
    nb = t // tq
    wb = MAX_WINDOW // tq
    fox = mode == "fox"

    def body(q_ref, k_ref, v_ref, b_ref, o_ref, lse_ref, lse_row_ref, s_ref, p_ref, m_ref, l_ref, acc_ref):
        qi = pl.program_id(1)
        qb = q_ref[...]
        m_ref[...] = jnp.full_like(m_ref, NEG)
        l_ref[...] = jnp.zeros_like(l_ref)
        acc_ref[...] = jnp.zeros_like(acc_ref)

        def tile(kj, diag):
            off = pl.multiple_of(kj * tq, tq)
            s_ref[...] = _dot(qb, k_ref[pl.ds(off, tq), :], NT)
            if fox:
                brow = b_ref[qi][:, :1] - b_ref[kj]

            def chunk(r0):
                rows = pl.ds(r0, ATTN_ROWS)
                if fox:
                    s = s_ref[rows, :] + brow
                    if diag:
                        s = jnp.where(_causal(r0, tq, False), s, NEG)
                else:
                    s = s_ref[rows, :] + b_ref[qi - kj, rows, :]
                m_old = m_ref[rows, :]
                m_new = jnp.maximum(m_old, jnp.max(s, axis=1, keepdims=True))
                p = jnp.exp(s - _rep(m_new, tq))
                alpha = jnp.exp(m_old - m_new)
                l_ref[rows, :] = alpha * l_ref[rows, :] + jnp.sum(p, axis=1, keepdims=True)
                m_ref[rows, :] = m_new
                acc_ref[rows, :] = alpha * acc_ref[rows, :]
                p_ref[rows, :] = p.astype(BF)

            _chunks(tq, chunk)
            acc_ref[...] += _dot(p_ref[...], v_ref[pl.ds(off, tq), :])

        tile(qi, True)
        if fox:
            lax.fori_loop(0, qi, lambda kj, c: (tile(kj, False), c)[1], 0)
        else:
            lax.fori_loop(1, jnp.minimum(qi, wb) + 1, lambda i, c: (tile(qi - i, False), c)[1], 0)
        o_ref[...] = (acc_ref[...] / l_ref[...]).astype(BF)
        lse = m_ref[...] + jnp.log(l_ref[...])
        lse_ref[...] = lse
        lse_row_ref[...] = _rows8(lse)

    qspec = pl.BlockSpec((tq, HEAD_DIM), lambda h, i: (i, h))
    kvspec = pl.BlockSpec((t, HEAD_DIM), lambda h, i: (0, h))
    repspec = pl.BlockSpec((None, tq, LANE), lambda h, i: (h, i, 0))
    row8spec = pl.BlockSpec((None, None, 8, tq), lambda h, i: (h, i, 0, 0))
    if fox:
        bspec = pl.BlockSpec((None, nb, 1, tq), lambda h, i: (h, 0, 0, 0))
    else:
        bspec = pl.BlockSpec((wb + 1, tq, tq), lambda h, i: (0, 0, 0))
    return pl.pallas_call(
        body, name=name, grid=(nh, nb), in_specs=[qspec, kvspec, kvspec, bspec],
        out_specs=[qspec, repspec, row8spec],
        out_shape=[jax.ShapeDtypeStruct((t, hd), BF), jax.ShapeDtypeStruct((nh, t, LANE), F32),
                   jax.ShapeDtypeStruct((nh, nb, 8, tq), F32)],
        scratch_shapes=[pltpu.VMEM((tq, tq), F32), pltpu.VMEM((tq, tq), BF), pltpu.VMEM((tq, LANE), F32),
                        pltpu.VMEM((tq, LANE), F32), pltpu.VMEM((tq, HEAD_DIM), F32)],
        compiler_params=_params(),
    )(q, k, v, bias)


def _attn_bwd_dq(mode, q, k, v, o, do, lse, bias, tq, name, dep=None):
    t, hd = q.shape
    nh = hd // HEAD_DIM
    nb = t // tq
    wb = MAX_WINDOW // tq
    fox = mode == "fox"

    def body(q_ref, k_ref, v_ref, o_ref, do_ref, lse_ref, b_ref, dq_ref, dl_row_ref,
             s_ref, dp_ref, x_ref, y_ref, acc_ref, acc2_ref, dl_ref):
        qi = pl.program_id(1)
        qb = q_ref[...]
        dob = do_ref[...]
        acc_ref[...] = jnp.zeros_like(acc_ref)
        if fox:
            acc2_ref[...] = jnp.zeros_like(acc2_ref)
            dl_ref[...] = jnp.zeros_like(dl_ref)
        else:
            prod = o_ref[...].astype(F32) * dob.astype(F32)
            dl_ref[...] = jnp.broadcast_to(jnp.sum(prod, axis=1, keepdims=True), (tq, LANE))

        def tile(kj, diag):
            off = pl.multiple_of(kj * tq, tq)
            kb = k_ref[pl.ds(off, tq), :]
            s_ref[...] = _dot(qb, kb, NT)
            dp_ref[...] = _dot(dob, v_ref[pl.ds(off, tq), :], NT)
            if fox:
                brow = b_ref[qi][:, :1] - b_ref[kj]

            def chunk(r0):
                rows = pl.ds(r0, ATTN_ROWS)
                lse_c = _rep(lse_ref[rows, :], tq)
                if fox:
                    s = s_ref[rows, :] + brow
                    if diag:
                        s = jnp.where(_causal(r0, tq, False), s, NEG)
                    p = jnp.exp(s - lse_c)
                    pdp = p * dp_ref[rows, :]
                    dl_ref[rows, :] += jnp.sum(pdp, axis=1, keepdims=True)
                    x_ref[rows, :] = pdp.astype(BF)
                    y_ref[rows, :] = p.astype(BF)
                else:
                    p = jnp.exp(s_ref[rows, :] + b_ref[qi - kj, rows, :] - lse_c)
                    x_ref[rows, :] = (p * (dp_ref[rows, :] - _rep(dl_ref[rows, :], tq))).astype(BF)

            _chunks(tq, chunk)
            acc_ref[...] += _dot(x_ref[...], kb)
            if fox:
                acc2_ref[...] += _dot(y_ref[...], kb)

        tile(qi, True)
        if fox:
            lax.fori_loop(0, qi, lambda kj, c: (tile(kj, False), c)[1], 0)
            dq_ref[...] = acc_ref[...] - dl_ref[...] * acc2_ref[...]
        else:
            lax.fori_loop(1, jnp.minimum(qi, wb) + 1, lambda i, c: (tile(qi - i, False), c)[1], 0)
            dq_ref[...] = acc_ref[...]
        dl_row_ref[...] = _rows8(dl_ref[...])

    qspec = pl.BlockSpec((tq, HEAD_DIM), lambda h, i: (i, h))
    kvspec = pl.BlockSpec((t, HEAD_DIM), lambda h, i: (0, h))
    repspec = pl.BlockSpec((None, tq, LANE), lambda h, i: (h, i, 0))
    row8spec = pl.BlockSpec((None, None, 8, tq), lambda h, i: (h, i, 0, 0))
    if fox:
        bspec = pl.BlockSpec((None, nb, 1, tq), lambda h, i: (h, 0, 0, 0))
    else:
        bspec = pl.BlockSpec((wb + 1, tq, tq), lambda h, i: (0, 0, 0))
    return _call(
        body, [q, k, v, o, do, lse, bias], dep=dep, name=name, grid=(nh, nb),
        in_specs=[qspec, kvspec, kvspec, qspec, qspec, repspec, bspec],
        out_specs=[qspec, row8spec],
        out_shape=[jax.ShapeDtypeStruct((t, hd), F32), jax.ShapeDtypeStruct((nh, nb, 8, tq), F32)],
        scratch_shapes=[pltpu.VMEM((tq, tq), F32), pltpu.VMEM((tq, tq), F32), pltpu.VMEM((tq, tq), BF),
                        pltpu.VMEM((tq, tq), BF), pltpu.VMEM((tq, HEAD_DIM), F32),
                        pltpu.VMEM((tq, HEAD_DIM), F32), pltpu.VMEM((tq, LANE), F32)],
        compiler_params=_params(),
    )


def _attn_bwd_dkv(mode, q, k, v, do, lse_row, dl_row, bias_t, c_row, tq, name):
    t, hd = q.shape
    nh = hd // HEAD_DIM
    nb = t // tq
    wb = MAX_WINDOW // tq
    fox = mode == "fox"

    def body(*refs):
        if fox:
            (q_ref, k_ref, v_ref, do_ref, lse_ref, dl_ref, b_ref, cq_ref, dk_ref, dv_ref, dc_row_ref,
             s_ref, dp_ref, x_ref, y_ref, dc_ref) = refs
        else:
            q_ref, k_ref, v_ref, do_ref, lse_ref, dl_ref, b_ref, dk_ref, dv_ref, s_ref, dp_ref, x_ref, y_ref = refs
        kj = pl.program_id(1)
        kb = k_ref[...]
        vb = v_ref[...]
        dk_ref[...] = jnp.zeros_like(dk_ref)
        dv_ref[...] = jnp.zeros_like(dv_ref)
        if fox:
            dc_ref[...] = jnp.zeros_like(dc_ref)

        def tile(qi, diag):
            off = pl.multiple_of(qi * tq, tq)
            qb = q_ref[pl.ds(off, tq), :]
            dob = do_ref[pl.ds(off, tq), :]
            s_ref[...] = _dot(kb, qb, NT)
            dp_ref[...] = _dot(vb, dob, NT)
            lse_r = lse_ref[qi, 0:1, :]
            dl_r = dl_ref[qi, 0:1, :]
            if fox:
                kbias = cq_ref[qi][:, :1] - b_ref[...]

            def chunk(r0):
                rows = pl.ds(r0, ATTN_ROWS)
                if fox:
                    s = s_ref[rows, :] + _rep(kbias[r0:r0 + ATTN_ROWS, :], tq)
                    if diag:
                        s = jnp.where(_causal(r0, tq, True), s, NEG)
                else:
                    s = s_ref[rows, :] + b_ref[qi - kj, rows, :]
                pt = jnp.exp(s - lse_r)
                dst = pt * (dp_ref[rows, :] - dl_r)
                x_ref[rows, :] = pt.astype(BF)
                y_ref[rows, :] = dst.astype(BF)
                if fox:
                    dc_ref[rows, :] -= jnp.sum(dst, axis=1, keepdims=True)

            _chunks(tq, chunk)
            dv_ref[...] += _dot(x_ref[...], dob)
            dk_ref[...] += _dot(y_ref[...], qb)

        tile(kj, True)
        hi = nb if fox else jnp.minimum(kj + wb + 1, nb)
        lax.fori_loop(kj + 1, hi, lambda qi, c: (tile(qi, False), c)[1], 0)
        if fox:
            dc_row_ref[...] = _rows8(dc_ref[...])

    blkspec = pl.BlockSpec((tq, HEAD_DIM), lambda h, j: (j, h))
    fullspec = pl.BlockSpec((t, HEAD_DIM), lambda h, j: (0, h))
    rows8spec = pl.BlockSpec((None, nb, 8, tq), lambda h, j: (h, 0, 0, 0))
    repspec = pl.BlockSpec((None, tq, LANE), lambda h, j: (h, j, 0))
    in_specs = [fullspec, blkspec, blkspec, fullspec, rows8spec, rows8spec]
    args = [q, k, v, do, lse_row, dl_row, bias_t]
    out_specs = [blkspec, blkspec]
    out_shape = [jax.ShapeDtypeStruct((t, hd), F32), jax.ShapeDtypeStruct((t, hd), F32)]
    scratch = [pltpu.VMEM((tq, tq), F32), pltpu.VMEM((tq, tq), F32), pltpu.VMEM((tq, tq), BF),
               pltpu.VMEM((tq, tq), BF)]
    if fox:
        in_specs += [repspec, pl.BlockSpec((None, nb, 1, tq), lambda h, j: (h, 0, 0, 0))]
        args.append(c_row)
        out_specs.append(pl.BlockSpec((None, None, 8, tq), lambda h, j: (h, j, 0, 0)))
        out_shape.append(jax.ShapeDtypeStruct((nh, nb, 8, tq), F32))
        scratch.append(pltpu.VMEM((tq, LANE), F32))
    else:
        in_specs.append(pl.BlockSpec((wb + 1, tq, tq), lambda h, j: (0, 0, 0)))
    return pl.pallas_call(
        body, name=name, grid=(nh, nb), in_specs=in_specs, out_specs=out_specs, out_shape=out_shape,
        scratch_shapes=scratch, compiler_params=_params(),
    )(*args)


def _gate_specs(t, d, hd, tr):
    row = pl.BlockSpec((tr, d), lambda i: (i, 0))
    vec = pl.BlockSpec((1, d), lambda i: (0, 0))
    base = 6 * hd // d
    gd = pl.BlockSpec((tr, d), lambda i: (i, base))
    gf = pl.BlockSpec((tr, d), lambda i: (i, base + 1))
    return row, vec, gd, gf


def _proj_merge(yd, yf, wpd, wpf, proj, b_d, b_f, hd):
    t = yd.shape[0]
    d = wpd.shape[1]
    tr = _tile(t, 256, 16)
    row, vec, gd, gf = _gate_specs(t, d, hd, tr)

    def body(yd_ref, yf_ref, wd_ref, wf_ref, gd_ref, gf_ref, bd_ref, bf_ref, pd_ref, pf_ref, o_ref):
        pd = _dot(yd_ref[...], wd_ref[...])
        pf = _dot(yf_ref[...], wf_ref[...])
        pd_ref[...] = pd
        pf_ref[...] = pf
        o_ref[...] = (_sig(gd_ref[...] + bd_ref[...]) * pd + _sig(gf_ref[...] + bf_ref[...]) * pf).astype(BF)

    yspec = pl.BlockSpec((tr, hd), lambda i: (i, 0))
    wspec = pl.BlockSpec((hd, d), lambda i: (0, 0))
    f32 = jax.ShapeDtypeStruct((t, d), F32)
    return pl.pallas_call(
        body, name="proj_merge", grid=(t // tr,), in_specs=[yspec, yspec, wspec, wspec, gd, gf, vec, vec],
        out_specs=[row, row, row], out_shape=[f32, f32, jax.ShapeDtypeStruct((t, d), BF)],
        compiler_params=_params(),
    )(yd, yf, wpd, wpf, proj, proj, b_d, b_f)


def _merge_bwd(dm, pd, pf, proj, b_d, b_f, hd):
    t, d = pd.shape
    tr = _tile(t, 256, 16)
    row, vec, gd, gf = _gate_specs(t, d, hd, tr)

    def body(dm_ref, pd_ref, pf_ref, gd_ref, gf_ref, bd_ref, bf_ref,
             dpd_ref, dpf_ref, dgd_ref, dgf_ref, dbd_ref, dbf_ref):
        dmv = dm_ref[...]
        sd = _sig(gd_ref[...] + bd_ref[...])
        sf = _sig(gf_ref[...] + bf_ref[...])
        dgd = dmv * pd_ref[...] * (sd * (1.0 - sd))
        dgf = dmv * pf_ref[...] * (sf * (1.0 - sf))
        dpd_ref[...] = (dmv * sd).astype(BF)
        dpf_ref[...] = (dmv * sf).astype(BF)
        dgd_ref[...] = dgd.astype(BF)
        dgf_ref[...] = dgf.astype(BF)

        @pl.when(pl.program_id(0) == 0)
        def _():
            dbd_ref[...] = jnp.zeros_like(dbd_ref)
            dbf_ref[...] = jnp.zeros_like(dbf_ref)

        dbd_ref[...] += jnp.sum(dgd, axis=0, keepdims=True)
        dbf_ref[...] += jnp.sum(dgf, axis=0, keepdims=True)

    ob = jax.ShapeDtypeStruct((t, d), BF)
    ov = jax.ShapeDtypeStruct((1, d), F32)
    return pl.pallas_call(
        body, name="merge_bwd", grid=(t // tr,), in_specs=[row, row, row, gd, gf, vec, vec],
        out_specs=[row, row, row, row, vec, vec], out_shape=[ob, ob, ob, ob, ov, ov],
        compiler_params=_params(),
    )(dm, pd, pf, proj, proj, b_d, b_f)


def _assemble_dproj(dqd, dkd, dvd, dqf, dkf, dvf, dgd, dgf, dlogf, proj, tables, bf_pad, scale):
    t, np_ = proj.shape
    hd = dqd.shape[1]
    d = dgd.shape[1]
    nh = hd // HEAD_DIM
    tr = _tile(t, 256, 16)
    f_blk = (np_ - F_PAD) // LANE

    def body(dqd_ref, dkd_ref, dvd_ref, dqf_ref, dkf_ref, dvf_ref, dgd_ref, dgf_ref, dlog_ref, fl_ref,
             c_ref, s1_ref, s2_ref, b_ref, o_ref, db_ref):
        c, s1, s2 = c_ref[...], s1_ref[...], s2_ref[...]
        for h in range(nh):
            sl = slice(h * HEAD_DIM, (h + 1) * HEAD_DIM)
            o_ref[:, sl] = (_rope_t(dqd_ref[:, sl], c, s1, s2) * scale).astype(BF)
            o_ref[:, hd + h * HEAD_DIM:hd + (h + 1) * HEAD_DIM] = _rope_t(dkd_ref[:, sl], c, s1, s2).astype(BF)
        o_ref[:, 2 * hd:3 * hd] = dvd_ref[...].astype(BF)
        o_ref[:, 3 * hd:4 * hd] = (dqf_ref[...] * scale).astype(BF)
        o_ref[:, 4 * hd:5 * hd] = dkf_ref[...].astype(BF)
        o_ref[:, 5 * hd:6 * hd] = dvf_ref[...].astype(BF)
        o_ref[:, 6 * hd:6 * hd + d] = dgd_ref[...]
        o_ref[:, 6 * hd + d:6 * hd + 2 * d] = dgf_ref[...]
        z = fl_ref[...] + b_ref[...]
        dfl = dlog_ref[...] * _sig(-z)
        o_ref[:, 6 * hd + 2 * d:6 * hd + 2 * d + LANE] = dfl.astype(BF)
        o_ref[:, 6 * hd + 2 * d + LANE:] = jnp.zeros((tr, F_PAD - LANE), BF)

        @pl.when(pl.program_id(0) == 0)
        def _():
            db_ref[...] = jnp.zeros_like(db_ref)

        db_ref[...] += jnp.sum(dfl, axis=0, keepdims=True)

    head = pl.BlockSpec((tr, hd), lambda i: (i, 0))
    row = pl.BlockSpec((tr, d), lambda i: (i, 0))
    lane_row = pl.BlockSpec((tr, LANE), lambda i: (i, 0))
    lane_vec = pl.BlockSpec((1, LANE), lambda i: (0, 0))
    return pl.pallas_call(
        body, name="assemble_dproj", grid=(t // tr,),
        in_specs=[head] * 6 + [row, row, lane_row, pl.BlockSpec((tr, LANE), lambda i: (i, f_blk)),
                               lane_row, lane_row, lane_row, lane_vec],
        out_specs=[pl.BlockSpec((tr, np_), lambda i: (i, 0)), lane_vec],
        out_shape=[jax.ShapeDtypeStruct((t, np_), BF), jax.ShapeDtypeStruct((1, LANE), F32)],
        compiler_params=_params(),
    )(dqd, dkd, dvd, dqf, dkf, dvf, dgd, dgf, dlogf, proj, *tables, bf_pad)


def _to_rows(a, tq):
    h, t = a.shape
    return a.reshape(h, t // tq, 1, tq)


def kernel(x, ffn1_norm, ffn1_w_gate, ffn1_w_up, ffn1_w_down, mix_norm, w_in, b_forget, b_gate_dil, b_gate_fox, w_proj_dil, w_proj_fox, w_out, ffn2_norm, ffn2_w_gate, ffn2_w_up, ffn2_w_down, final_norm, loss_target, m_ffn1_norm, m_ffn1_w_gate, m_ffn1_w_up, m_ffn1_w_down, m_mix_norm, m_w_in, m_b_forget, m_b_gate_dil, m_b_gate_fox, m_w_proj_dil, m_w_proj_fox, m_w_out, m_ffn2_norm, m_ffn2_w_gate, m_ffn2_w_up, m_ffn2_w_down, m_final_norm, v_ffn1_norm, v_ffn1_w_gate, v_ffn1_w_up, v_ffn1_w_down, v_mix_norm, v_w_in, v_b_forget, v_b_gate_dil, v_b_gate_fox, v_w_proj_dil, v_w_proj_fox, v_w_out, v_ffn2_norm, v_ffn2_w_gate, v_ffn2_w_up, v_ffn2_w_down, v_final_norm):
    t, d = x.shape[1], x.shape[2]
    hd = w_proj_dil.shape[1]
    nh = hd // HEAD_DIM
    n_f = b_forget.shape[1]
    cols = w_in.shape[2]
    in_cols = N_DEV * cols
    assert in_cols == 6 * hd + n_f + 2 * d and n_f == nh and n_f <= LANE
    np_ = 6 * hd + 2 * d + F_PAD
    scale = HEAD_DIM ** -0.5
    tq = _tile(t, 512, LANE)
    assert MAX_WINDOW % tq == 0 and tq % 16 == 0

    x2d = x[0]
    tgt = loss_target[0]

    def rows(w):
        return jnp.swapaxes(w, 1, 2)

    fc = N_DEV * ffn1_w_down.shape[1]
    ag_order = [rows(ffn1_w_gate), rows(ffn1_w_up), ffn1_w_down, w_in, w_proj_dil, w_proj_fox, w_out,
                rows(ffn2_w_gate), rows(ffn2_w_up), ffn2_w_down]
    ag_first, tok = _exchange_start([w[0].astype(BF) for w in ag_order[:2]], True, "ag_start_first", ks=FIRST_LEVEL)
    ag_rest, ag_token = _exchange_start([w[0].astype(BF) for w in ag_order[2:]], True, "ag_start", dep=tok,
                                        ks=FIRST_LEVEL)
    ag = ag_first + ag_rest

    def relay(idx, after, name):
        for i, h in zip(idx, _gather_relay([ag[i] for i in idx], after, name)):
            ag[i] = h

    def gathered(idx, after, name):
        return _gather_wait([ag[i] for i in idx], after, name)

    def ffn_weight(idx, after, name):
        return [w.reshape(fc, d) for w in gathered(idx, after, name)]

    tables = _rope_tables(t)
    bf_pad = jnp.pad(b_forget, ((0, 0), (0, LANE - n_f)))

    hn1, = _rms_fwd(x2d, ffn1_norm, "rms_ffn1", dep=ag_token)
    relay([0], hn1, "ag_relay_ffn1_gate")
    wg1, = ffn_weight([0], hn1, "ag_wait_ffn1_gate")
    g1_f32 = _ffn_gate(hn1, wg1, "ffn1_gate")
    relay([1], g1_f32, "ag_relay_ffn1_up")
    wu1, = ffn_weight([1], g1_f32, "ag_wait_ffn1_up")
    relay([2], wu1, "ag_relay_ffn1_down")
    g1, u1, a1 = _ffn_up_act(hn1, wu1, g1_f32, "ffn1_up_act")
    wd1, = ffn_weight([2], a1, "ag_wait_ffn1_down")
    relay([3], wd1, "ag_relay_w_in")
    x1 = _ffn_down(a1, wd1, x2d, "ffn1_down")

    hm, hm_t = _rms_fwd(x1, mix_norm, "rms_mix", with_transpose=True)
    win_g, = gathered([3], hm, "ag_wait_w_in")
    relay([4, 5, 6], win_g, "ag_relay_mixer")
    segments = [(0, 6 * hd), (6 * hd + n_f, in_cols), (6 * hd, 6 * hd + n_f)]
    pieces = []
    for lo, hi in segments:
        for j in range(lo // cols, (hi - 1) // cols + 1):
            s, e = max(lo, j * cols), min(hi, (j + 1) * cols)
            pieces.append(win_g[j, :, s - j * cols:e - j * cols])
    win_p = jnp.concatenate(pieces + [jnp.zeros((d, F_PAD - n_f), BF)], axis=1)
    proj = _mm_nn(hm, win_p, F32, "w_in_fwd", tn_pref=W_IN_COLS)
    qd, kd, vd, qf, kf, vf, logf = _mixer_prep(proj, tables, bf_pad, hd, scale)
    csum = _cumsum_rows(logf, False, "cumsum_logf")
    c_heads = csum[:, :nh].T
    c_row = _to_rows(c_heads, tq)
    c_rep = jnp.broadcast_to(c_heads[:, :, None], (nh, t, LANE))
    dil_bias = _dil_bias_tiles(tq)
    dil_bias_t = dil_bias.transpose(0, 2, 1)
    relay([7, 8, 9], qd, "ag_relay_ffn2")
    yd, lse_d, lse_d_row = _attn_fwd("dil", qd, kd, vd, dil_bias, tq, "attn_dil_fwd")
    yf, lse_f, lse_f_row = _attn_fwd("fox", qf, kf, vf, c_row, tq, "attn_fox_fwd")
    wpd_g, wpf_g = gathered([4, 5], yf, "ag_wait_proj")
    wpd = wpd_g.transpose(1, 0, 2).reshape(hd, d)
    wpf = wpf_g.transpose(1, 0, 2).reshape(hd, d)
    pd, pf, merged = _proj_merge(yd, yf, wpd, wpf, proj, b_gate_dil, b_gate_fox, hd)
    wout_g, = gathered([6], merged, "ag_wait_w_out")
    wout = wout_g.reshape(d, d)
    x2 = _mm_nn(merged, wout, F32, "w_out_fwd", residual=x1, tn_pref=1024)

    hn2, = _rms_fwd(x2, ffn2_norm, "rms_ffn2")
    wg2, wu2 = ffn_weight([7, 8], hn2, "ag_wait_ffn2_gate_up")
    g2, u2, a2 = _ffn_gate_up(hn2, wg2, wu2, "ffn2_gate_up")
    wd2, = ffn_weight([9], a2, "ag_wait_ffn2_down")
    x3 = _ffn_down(a2, wd2, x2, "ffn2_down")

    dx3, dx3b, d_final, loss_lanes = _loss_head(x3, final_norm.reshape(1, d), tgt)

    def ffn_bwd(dxb, hn, g, u, a, wg_t, wu_t, wd, x_in, gain, dres, tag):
        def parts(dw):
            return dw.reshape(N_DEV, fc // N_DEV, d)

        dg, du = _ffn_bwd_hidden(dxb, wd, g, u, tag + "_bwd_hidden")
        dwd, = _ffn_dw([a], dxb, 0.5, tag + "_dw_down")
        rs_down, tok = _exchange_start([parts(dwd)], False, "rs_start_" + tag + "_down")
        dwg_t, dwu_t = _ffn_dw([dg, du], hn, 1.0, tag + "_dw_gate_up", dep=tok)
        rs_gu, tok = _exchange_start([parts(dwg_t), parts(dwu_t)], False, "rs_start_" + tag + "_gate_up")
        dhn = _ffn_bwd_input(dg, du, wg_t, wu_t, tag + "_bwd_input", dep=tok)
        dx, dx_bf, dgain = _rms_bwd(dhn, x_in, gain, dres, "rms_" + tag + "_bwd")
        return dx, dx_bf, dgain, rs_gu + rs_down

    dx2, dx2b, d_ffn2_norm, rs_ffn2 = ffn_bwd(dx3b, hn2, g2, u2, a2, wg2, wu2, wd2, x2, ffn2_norm, dx3, "ffn2")

    dmerged = _mm_nt(dx2b, wout, F32, "w_out_bwd")
    dwout = _mm_tn(merged, dx2b, BF, "w_out_dw", tn_pref=1024, tk_pref=DW_ROWS)
    dpd, dpf, dgd, dgf, d_bd, d_bf = _merge_bwd(dmerged, pd, pf, proj, b_gate_dil, b_gate_fox, hd)
    dyd = _mm_nt(dpd, wpd, BF, "proj_dil_bwd")
    dyf = _mm_nt(dpf, wpf, BF, "proj_fox_bwd")
    dwpd = _mm_tn(yd, dpd, BF, "proj_dil_dw", tn_pref=1024, tk_pref=DW_ROWS)
    dwpf = _mm_tn(yf, dpf, BF, "proj_fox_dw", tn_pref=1024, tk_pref=DW_ROWS)
    dwpd_c = dwpd.reshape(hd, N_DEV, d // N_DEV).transpose(1, 0, 2)
    dwpf_c = dwpf.reshape(hd, N_DEV, d // N_DEV).transpose(1, 0, 2)
    dwout_c = dwout.reshape(N_DEV, d // N_DEV, d)
    rs_mix, tok = _exchange_start([dwout_c, dwpd_c, dwpf_c], False, "rs_start_mixer")

    dqd, dl_d = _attn_bwd_dq("dil", qd, kd, vd, yd, dyd, lse_d, dil_bias, tq, "attn_dil_dq", dep=tok)
    dkd, dvd = _attn_bwd_dkv("dil", qd, kd, vd, dyd, lse_d_row, dl_d, dil_bias_t, None, tq, "attn_dil_dkv")
    dqf, dl_f = _attn_bwd_dq("fox", qf, kf, vf, yf, dyf, lse_f, c_row, tq, "attn_fox_dq")
    dkf, dvf, dc = _attn_bwd_dkv("fox", qf, kf, vf, dyf, lse_f_row, dl_f, c_rep, c_row, tq, "attn_fox_dkv")
    dc_pad = jnp.pad(dc[:, :, 0, :].reshape(nh, t).T, ((0, 0), (0, LANE - nh)))
    dlogf = _cumsum_rows(dc_pad, True, "revcumsum_dc")
    dproj, d_bforget = _assemble_dproj(dqd, dkd, dvd, dqf, dkf, dvf, dgd, dgf, dlogf, proj, tables, bf_pad, scale)

    dwin_p = _mm_tn(hm_t, dproj, BF, "w_in_dw", tn_pref=W_IN_COLS // 2, tk_pref=DW_ROWS, a_transposed=True)

    def perm_col(c):
        if c < 6 * hd:
            return c
        return c + 2 * d if c < 6 * hd + n_f else c - n_f

    shards = []
    for j in range(N_DEV):
        cuts = sorted({j * cols, (j + 1) * cols} | {c for c in (6 * hd, 6 * hd + n_f) if j * cols < c < (j + 1) * cols})
        shards.append(jnp.concatenate([dwin_p[:, perm_col(lo):perm_col(lo) + hi - lo]
                                       for lo, hi in zip(cuts[:-1], cuts[1:])], axis=1))
    dwin_c = jnp.stack(shards)
    rs_win, tok = _exchange_start([dwin_c], False, "rs_start_w_in")
    dx1, dx1b, d_mix_norm = _mm_nt(dproj, win_p, F32, "w_in_bwd", tn_pref=d, tk_pref=W_IN_COLS // 2,
                                   rms=(x1, mix_norm, dx2), dep=tok)

    grad_x, _, d_ffn1_norm, rs_ffn1 = ffn_bwd(dx1b, hn1, g1, u1, a1, wg1, wu1, wd1, x2d, ffn1_norm, dx1, "ffn1")

    def update(handles, names, after, tag):
        recvs = _exchange_wait(handles, False, after, "rs_wait_" + tag)
        res = {}
        for recv, n in zip(recvs, names):
            turn = rows if n.endswith(("w_gate", "w_up")) else (lambda a: a)
            w, m, v = (turn(a)[0] for a in wmv[n])
            res[n] = tuple(turn(o[None]) for o in _adam_from_partials(recv, w, m, v, "adam_" + n))
        return res, res[names[-1]][0]

    wmv = {
        "ffn1_w_gate": (ffn1_w_gate, m_ffn1_w_gate, v_ffn1_w_gate),
        "ffn1_w_up": (ffn1_w_up, m_ffn1_w_up, v_ffn1_w_up),
        "ffn1_w_down": (ffn1_w_down, m_ffn1_w_down, v_ffn1_w_down),
        "w_in": (w_in, m_w_in, v_w_in),
        "w_proj_dil": (w_proj_dil, m_w_proj_dil, v_w_proj_dil),
        "w_proj_fox": (w_proj_fox, m_w_proj_fox, v_w_proj_fox),
        "w_out": (w_out, m_w_out, v_w_out),
        "ffn2_w_gate": (ffn2_w_gate, m_ffn2_w_gate, v_ffn2_w_gate),
        "ffn2_w_up": (ffn2_w_up, m_ffn2_w_up, v_ffn2_w_up),
        "ffn2_w_down": (ffn2_w_down, m_ffn2_w_down, v_ffn2_w_down),
    }
    big = {}
    after = grad_x
    for handles, names, tag in [
            (rs_ffn2, ["ffn2_w_gate", "ffn2_w_up", "ffn2_w_down"], "ffn2"),
            (rs_mix, ["w_out", "w_proj_dil", "w_proj_fox"], "mixer"),
            (rs_win, ["w_in"], "w_in"),
            (rs_ffn1, ["ffn1_w_gate", "ffn1_w_up", "ffn1_w_down"], "ffn1")]:
        res, after = update(handles, names, after, tag)
        big.update(res)

    def lanes(a):
        a = a.reshape(1, -1)
        return jnp.pad(a, ((0, 0), (0, d - a.shape[1])))

    small_names = ["ffn1_norm", "mix_norm", "b_gate_dil", "b_gate_fox", "ffn2_norm", "final_norm", "b_forget"]
    small_g = [d_ffn1_norm, d_mix_norm, d_bd, d_bf, d_ffn2_norm, d_final, d_bforget[:, :n_f]]
    small_w = [ffn1_norm, mix_norm, b_gate_dil, b_gate_fox, ffn2_norm, final_norm, b_forget]
    small_m = [m_ffn1_norm, m_mix_norm, m_b_gate_dil, m_b_gate_fox, m_ffn2_norm, m_final_norm, m_b_forget]
    small_v = [v_ffn1_norm, v_mix_norm, v_b_gate_dil, v_b_gate_fox, v_ffn2_norm, v_final_norm, v_b_forget]
    pack = lambda arrs, last: jnp.concatenate([lanes(a) for a in arrs] + [last], axis=0)
    g_all = _allreduce_small(pack(small_g, loss_lanes))
    zero_row = jnp.zeros((1, d), F32)
    one_row = jnp.ones((1, d), F32)
    s_delta, s_m, s_v = _adam_small(g_all, pack(small_w, zero_row), pack(small_m, zero_row), pack(small_v, one_row))
    loss = g_all[len(small_names), 0]

    def unpack(packed, i, like):
        return packed[i, :like.size].reshape(like.shape)

    small = {}
    for i, (n, w) in enumerate(zip(small_names, small_w)):
        small[n] = (unpack(g_all, i, w), unpack(s_delta, i, w), unpack(s_m, i, w), unpack(s_v, i, w))

    order = ["ffn1_norm", "ffn1_w_gate", "ffn1_w_up", "ffn1_w_down", "mix_norm", "w_in", "b_forget", "b_gate_dil",
             "b_gate_fox", "w_proj_dil", "w_proj_fox", "w_out", "ffn2_norm", "ffn2_w_gate", "ffn2_w_up",
             "ffn2_w_down", "final_norm"]
    res = {**big, **small}
    outs = [loss, grad_x[None]]
    for slot in range(4):
        outs += [res[n][slot] for n in order]
    return tuple(outs)
```

```python
import jax
import jax.numpy as jnp
from jax import lax
from jax.experimental import pallas as pl
from jax.experimental.pallas import tpu as pltpu

BF = jnp.bfloat16
F32 = jnp.float32
MESH = pl.DeviceIdType.MESH
N_DEV = 8

HEAD_DIM = 128
ROPE_DIM = HEAD_DIM // 4
ROPE_HALF = ROPE_DIM // 2
ROPE_THETA = 500000.0
NORM_EPS = 1e-6
DIL_PATTERNS = ((128, 1), (512, 4), (2048, 16))
MAX_WINDOW = 2048
LANE = 128
NEG = -1e30
F_PAD = 512
W_IN_COLS = 1536

ADAM_LR = 0.001
ADAM_B1 = 0.9
ADAM_B2 = 0.999
ADAM_EPS = 1e-08
ADAM_WD = 0.01
ADAM_STEP = 10

VMEM_LIMIT_BYTES = 56 * 1024 * 1024
FFN_ROWS = 1024
DW_ROWS = 2048
ANY = pl.BlockSpec(memory_space=pl.ANY)

NN = (((1,), (0,)), ((), ()))
NT = (((1,), (1,)), ((), ()))
TN = (((0,), (0,)), ((), ()))


def _dot(a, b, dn=NN):
    return lax.dot_general(a, b, dn, preferred_element_type=F32)


def _sig(x):
    return 0.5 + 0.5 * jnp.tanh(0.5 * x)


def _tile(n, pref, align):
    best = None
    t = align
    while t <= min(n, pref):
        if n % t == 0:
            best = t
        t += align
    return n if best is None else best


def _params():
    return pltpu.CompilerParams(vmem_limit_bytes=VMEM_LIMIT_BYTES)


def _call(body, args, dep=None, **kw):
    if dep is not None:
        n_in = len(args)
        inner = body

        def body(*refs):
            inner(*refs[:n_in], *refs[n_in + 1:])

        kw["in_specs"] = list(kw["in_specs"]) + [ANY]
        args = list(args) + [dep]
    return pl.pallas_call(body, **kw)(*args)


def _peers():
    x, y, c = lax.axis_index("x"), lax.axis_index("y"), lax.axis_index("c")
    me = 4 * x + 2 * y + c
    peers = []
    for k in range(1, N_DEV):
        px = 1 - x if (k >> 2) & 1 else x
        py = 1 - y if (k >> 1) & 1 else y
        pc = 1 - c if k & 1 else c
        peers.append((k, (px, py, pc), 4 * px + 2 * py + pc))
    return me, peers


HBM = pl.BlockSpec(memory_space=pltpu.HBM)
SEM = pl.BlockSpec(memory_space=pltpu.SEMAPHORE)
EFFECT = pltpu.SideEffectType.DATAFLOW_SIDE_EFFECTING


def _exchange_copy(gather, src_ref, land_ref, send_sems, recv_sems, me, k, peer, peer_flat, landing):
    return pltpu.make_async_remote_copy(
        src_ref=src_ref if gather else src_ref.at[peer_flat], dst_ref=land_ref.at[landing],
        send_sem=send_sems.at[k], recv_sem=recv_sems.at[k], device_id=peer, device_id_type=MESH)


ALL_PEERS = (1, 2, 3, 4, 5, 6, 7)
SIBLING = 1
SAME_CORE = (2, 4, 6)
FIRST_LEVEL = (SIBLING,) + SAME_CORE


def _exchange_start(srcs, gather, name, dep=None, ks=ALL_PEERS):
    n = len(srcs)
    extra = [] if dep is None else [dep]

    def body(*refs):
        src_refs, land_refs = refs[:n], refs[n:2 * n]
        refs = refs[2 * n + len(extra):]
        send_refs, recv_refs = refs[:n], refs[n:2 * n]
        token = refs[4 * n]
        me, peers = _peers()
        for i in range(n):
            for k, peer, peer_flat in peers:
                if k in ks:
                    _exchange_copy(gather, src_refs[i], land_refs[i], send_refs[i], recv_refs[i],
                                   me, k, peer, peer_flat, me).start()
        token[...] = jnp.zeros_like(token)

    lands = [lax.empty((N_DEV,) + s.shape[-2:], s.dtype) for s in srcs]
    sems = [pltpu.SemaphoreType.DMA((N_DEV,)) for _ in range(2 * n)]
    out = pl.pallas_call(
        body, name=name,
        out_shape=tuple(sems) + tuple(pltpu.HBM(a.shape, a.dtype) for a in list(srcs) + lands)
        + (jax.ShapeDtypeStruct((8, LANE), F32),),
        in_specs=[HBM] * (2 * n) + [ANY] * len(extra),
        out_specs=tuple([SEM] * (2 * n) + [HBM] * (2 * n) + [pl.BlockSpec(memory_space=pltpu.VMEM)]),
        input_output_aliases={i: 2 * n + i for i in range(2 * n)},
        compiler_params=pltpu.CompilerParams(has_side_effects=EFFECT),
    )(*[pltpu.with_memory_space_constraint(a, pltpu.HBM) for a in list(srcs) + lands], *extra)
    handles = [(out[2 * n + i], out[3 * n + i], out[i], out[n + i]) for i in range(n)]
    return handles, out[4 * n]


def _exchange_wait(handles, gather, after, name):
    n = len(handles)

    def body(*refs):
        src_refs, land_refs = refs[:n], refs[n:2 * n]
        send_refs, recv_refs = refs[2 * n:3 * n], refs[3 * n:4 * n]
        me, peers = _peers()
        for i in range(n):
            for k, peer, peer_flat in peers:
                cp = _exchange_copy(gather, src_refs[i], land_refs[i], send_refs[i], recv_refs[i],
                                    me, k, peer, peer_flat, peer_flat)
                cp.wait_send()
                cp.wait_recv()

    srcs = [h[0] for h in handles]
    lands = [h[1] for h in handles]
    out = pl.pallas_call(
        body, name=name,
        out_shape=tuple(pltpu.HBM(a.shape, a.dtype) for a in srcs + lands),
        in_specs=[HBM] * (2 * n) + [SEM] * (2 * n) + [ANY],
        out_specs=tuple([HBM] * (2 * n)),
        input_output_aliases={i: i for i in range(2 * n)},
        compiler_params=pltpu.CompilerParams(has_side_effects=EFFECT),
    )(*srcs, *lands, *[h[2] for h in handles], *[h[3] for h in handles], after)
    me = 4 * lax.axis_index("x") + 2 * lax.axis_index("y") + lax.axis_index("c")
    filled = []
    for src, land in zip(out[:n], out[n:]):
        own = src[None] if gather else lax.dynamic_slice_in_dim(src, me, 1, axis=0)
        filled.append(lax.dynamic_update_slice_in_dim(land, own, me, axis=0))
    return filled


def _gather_relay(handles, after, name):
    n = len(handles)

    def body(*refs):
        land_refs, recv_refs = refs[:n], refs[n:2 * n]
        refs = refs[2 * n + 1:]
        send2_refs, recv2_refs = refs[n:2 * n], refs[2 * n:3 * n]
        me, peers = _peers()
        sibling = peers[SIBLING - 1][1]
        for i in range(n):
            for k, peer, peer_flat in peers:
                if k in SAME_CORE:
                    block = land_refs[i].at[peer_flat]
                    pltpu.make_async_remote_copy(
                        src_ref=block, dst_ref=block, send_sem=send2_refs[i].at[k], recv_sem=recv_refs[i].at[k],
                        device_id=peer, device_id_type=MESH).wait_recv()
                    pltpu.make_async_remote_copy(
                        src_ref=block, dst_ref=block, send_sem=send2_refs[i].at[k], recv_sem=recv2_refs[i].at[k],
                        device_id=sibling, device_id_type=MESH).start()

    lands = [h[1] for h in handles]
    sems = [pltpu.SemaphoreType.DMA((N_DEV,)) for _ in range(2 * n)]
    out = pl.pallas_call(
        body, name=name,
        out_shape=tuple(pltpu.HBM(a.shape, a.dtype) for a in lands) + tuple(sems),
        in_specs=[HBM] * n + [SEM] * n + [ANY],
        out_specs=tuple([HBM] * n + [SEM] * (2 * n)),
        input_output_aliases={i: i for i in range(n)},
        compiler_params=pltpu.CompilerParams(has_side_effects=EFFECT),
    )(*lands, *[h[3] for h in handles], after)
    return [(h[0], out[i], h[2], h[3], out[n + i], out[2 * n + i]) for i, h in enumerate(handles)]


def _gather_wait(handles, after, name):
    n = len(handles)

    def body(*refs):
        src_refs, land_refs = refs[:n], refs[n:2 * n]
        send_refs, recv_refs = refs[2 * n:3 * n], refs[3 * n:4 * n]
        send2_refs, recv2_refs = refs[4 * n:5 * n], refs[5 * n:6 * n]
        me, peers = _peers()
        _, sibling, sibling_flat = peers[SIBLING - 1]
        for i in range(n):
            for k, peer, peer_flat in peers:
                if k in FIRST_LEVEL:
                    cp = _exchange_copy(True, src_refs[i], land_refs[i], send_refs[i], recv_refs[i],
                                        me, k, peer, peer_flat, peer_flat)
                    cp.wait_send()
                    if k == SIBLING:
                        cp.wait_recv()
                if k in SAME_CORE:
                    mine = land_refs[i].at[peer_flat]
                    theirs = land_refs[i].at[peer_flat ^ SIBLING]
                    cp = pltpu.make_async_remote_copy(
                        src_ref=mine, dst_ref=theirs, send_sem=send2_refs[i].at[k], recv_sem=recv2_refs[i].at[k],
                        device_id=sibling, device_id_type=MESH)
                    cp.wait_send()
                    cp.wait_recv()

    srcs = [h[0] for h in handles]
    lands = [h[1] for h in handles]
    out = pl.pallas_call(
        body, name=name,
        out_shape=tuple(pltpu.HBM(a.shape, a.dtype) for a in srcs + lands),
        in_specs=[HBM] * (2 * n) + [SEM] * (4 * n) + [ANY],
        out_specs=tuple([HBM] * (2 * n)),
        input_output_aliases={i: i for i in range(2 * n)},
        compiler_params=pltpu.CompilerParams(has_side_effects=EFFECT),
    )(*srcs, *lands, *[h[2] for h in handles], *[h[3] for h in handles],
      *[h[4] for h in handles], *[h[5] for h in handles], after)
    me = 4 * lax.axis_index("x") + 2 * lax.axis_index("y") + lax.axis_index("c")
    return [lax.dynamic_update_slice_in_dim(land, src[None], me, axis=0) for src, land in zip(out[:n], out[n:])]


def _allreduce_small(p):
    rows, d = p.shape

    def body(p_ref, o_ref, recv_ref, send_sems, recv_sems):
        me, peers = _peers()
        recv_ref[me] = p_ref[...]
        sends = []
        for k, peer, peer_flat in peers:
            cp = pltpu.make_async_remote_copy(
                src_ref=p_ref, dst_ref=recv_ref.at[me],
                send_sem=send_sems.at[k], recv_sem=recv_sems.at[k],
                device_id=peer, device_id_type=MESH)
            cp.start()
            sends.append(cp)
        for k, peer, peer_flat in peers:
            pltpu.make_async_remote_copy(
                src_ref=p_ref, dst_ref=recv_ref.at[peer_flat],
                send_sem=send_sems.at[k], recv_sem=recv_sems.at[k],
                device_id=peer, device_id_type=MESH).wait_recv()
        for cp in sends:
            cp.wait_send()
        acc = recv_ref[0]
        for s in range(1, N_DEV):
            acc = acc + recv_ref[s]
        is_loss = lax.broadcasted_iota(jnp.int32, (rows, d), 0) == rows - 1
        total = jnp.sum(jnp.where(is_loss, acc, 0.0))
        o_ref[...] = jnp.where(is_loss, total, acc)

    return pl.pallas_call(
        body, name="allreduce_small",
        out_shape=jax.ShapeDtypeStruct((rows, d), F32),
        in_specs=[pl.BlockSpec(memory_space=pltpu.VMEM)],
        out_specs=pl.BlockSpec(memory_space=pltpu.VMEM),
        scratch_shapes=[pltpu.VMEM((N_DEV, rows, d), F32),
                        pltpu.SemaphoreType.DMA((N_DEV,)), pltpu.SemaphoreType.DMA((N_DEV,))],
    )(p)


def _adam_math(w, g, m, v):
    m2 = ADAM_B1 * m + (1.0 - ADAM_B1) * g
    v2 = ADAM_B2 * v + (1.0 - ADAM_B2) * (g * g)
    m_hat = m2 / (1.0 - ADAM_B1 ** ADAM_STEP)
    v_hat = v2 / (1.0 - ADAM_B2 ** ADAM_STEP)
    delta = -ADAM_LR * (m_hat / (jnp.sqrt(v_hat) + ADAM_EPS) + ADAM_WD * w)
    return delta, m2, v2


def _adam_from_partials(parts, w, m, v, name):
    r, c = w.shape
    tr = _tile(r, 256, 16)

    def body(p_ref, w_ref, m_ref, v_ref, g_out, d_out, m_out, v_out):
        g = p_ref[0].astype(F32)
        for s in range(1, N_DEV):
            g = g + p_ref[s].astype(F32)
        delta, m2, v2 = _adam_math(w_ref[...], g, m_ref[...], v_ref[...])
        g_out[...] = g
        d_out[...] = delta
        m_out[...] = m2
        v_out[...] = v2

    blk = pl.BlockSpec((tr, c), lambda i: (i, 0))
    out = jax.ShapeDtypeStruct((r, c), F32)
    return pl.pallas_call(
        body, name=name, grid=(r // tr,),
        in_specs=[pl.BlockSpec((N_DEV, tr, c), lambda i: (0, i, 0)), blk, blk, blk],
        out_specs=[blk, blk, blk, blk], out_shape=[out, out, out, out],
        compiler_params=_params(),
    )(parts, w, m, v)


def _adam_small(g, w, m, v):
    def body(g_ref, w_ref, m_ref, v_ref, d_out, m_out, v_out):
        delta, m2, v2 = _adam_math(w_ref[...], g_ref[...], m_ref[...], v_ref[...])
        d_out[...] = delta
        m_out[...] = m2
        v_out[...] = v2

    out = jax.ShapeDtypeStruct(g.shape, F32)
    return pl.pallas_call(body, name="adam_small", out_shape=[out, out, out])(g, w, m, v)


def _rms_fwd(x, gain, name, dep=None, with_transpose=False):
    t, d = x.shape
    tr = _tile(t, 256, LANE)

    def body(x_ref, g_ref, o_ref, *ot_ref):
        xv = x_ref[...]
        r = lax.rsqrt(jnp.mean(xv * xv, axis=-1, keepdims=True) + NORM_EPS)
        y = xv * r * g_ref[...]
        o_ref[...] = y.astype(BF)
        if with_transpose:
            ot_ref[0][...] = jnp.transpose(y).astype(BF)

    out_specs = [pl.BlockSpec((tr, d), lambda i: (i, 0))]
    out_shape = [jax.ShapeDtypeStruct((t, d), BF)]
    if with_transpose:
        out_specs.append(pl.BlockSpec((d, tr), lambda i: (0, i)))
        out_shape.append(jax.ShapeDtypeStruct((d, t), BF))
    return _call(
        body, [x, gain], dep=dep, name=name, grid=(t // tr,),
        in_specs=[pl.BlockSpec((tr, d), lambda i: (i, 0)), pl.BlockSpec((1, d), lambda i: (0, 0))],
        out_specs=out_specs, out_shape=out_shape, compiler_params=_params(),
    )


def _rms_vjp(xv, gain, dy):
    r = lax.rsqrt(jnp.mean(xv * xv, axis=-1, keepdims=True) + NORM_EPS)
    xhat = xv * r
    dxhat = dy * gain
    dx = r * (dxhat - xhat * jnp.mean(dxhat * xhat, axis=-1, keepdims=True))
    dgain = jnp.sum(dy * xhat, axis=0, keepdims=True)
    return dx, dgain


def _loss_head(x, gain, target):
    t, d = x.shape
    tr = _tile(t, 256, 16)

    def body(x_ref, g_ref, t_ref, dx_ref, dxb_ref, dg_ref, loss_ref):
        xv = x_ref[...]
        gain = g_ref[...]
        r = lax.rsqrt(jnp.mean(xv * xv, axis=-1, keepdims=True) + NORM_EPS)
        err = xv * r * gain - t_ref[...]
        dx, dgain = _rms_vjp(xv, gain, err * (1.0 / d))
        dx_ref[...] = dx
        dxb_ref[...] = dx.astype(BF)

        @pl.when(pl.program_id(0) == 0)
        def _():
            dg_ref[...] = jnp.zeros_like(dg_ref)
            loss_ref[...] = jnp.zeros_like(loss_ref)

        dg_ref[...] += dgain
        loss_ref[...] += jnp.sum(err * err, axis=0, keepdims=True) * (0.5 / d)

    row = pl.BlockSpec((tr, d), lambda i: (i, 0))
    vec = pl.BlockSpec((1, d), lambda i: (0, 0))
    return pl.pallas_call(
        body, name="loss_head", grid=(t // tr,),
        in_specs=[row, vec, row], out_specs=[row, row, vec, vec],
        out_shape=[jax.ShapeDtypeStruct((t, d), F32), jax.ShapeDtypeStruct((t, d), BF),
                   jax.ShapeDtypeStruct((1, d), F32), jax.ShapeDtypeStruct((1, d), F32)],
        compiler_params=_params(),
    )(x, gain, target)


def _mm_nn(a, b, out_dtype, name, residual=None, tm_pref=512, tn_pref=1152):
    m, k = a.shape
    n = b.shape[1]
    tm, tn = _tile(m, tm_pref, 16), _tile(n, tn_pref, LANE)

    def body(*refs):
        if residual is None:
            a_ref, b_ref, o_ref = refs
            o_ref[...] = _dot(a_ref[...], b_ref[...]).astype(out_dtype)
        else:
            a_ref, b_ref, r_ref, o_ref = refs
            o_ref[...] = (r_ref[...] + _dot(a_ref[...], b_ref[...])).astype(out_dtype)

    in_specs = [pl.BlockSpec((tm, k), lambda j, i: (i, 0)), pl.BlockSpec((k, tn), lambda j, i: (0, j))]
    args = [a, b]
    if residual is not None:
        in_specs.append(pl.BlockSpec((tm, tn), lambda j, i: (i, j)))
        args.append(residual)
    return pl.pallas_call(
        body, name=name, grid=(n // tn, m // tm), in_specs=in_specs,
        out_specs=pl.BlockSpec((tm, tn), lambda j, i: (i, j)),
        out_shape=jax.ShapeDtypeStruct((m, n), out_dtype), compiler_params=_params(),
    )(*args)


def _rms_bwd_tail(dy_ref, first, x_ref, g_ref, dres_ref, dx_ref, dxb_ref, dg_ref):
    @pl.when(first)
    def _():
        dg_ref[...] = jnp.zeros_like(dg_ref)

    gain = g_ref[...]
    for r in range(0, dy_ref.shape[0], LANE):
        rows = pl.ds(r, min(LANE, dy_ref.shape[0] - r))
        dx, dgain = _rms_vjp(x_ref[rows, :], gain, dy_ref[rows, :])
        dx = dx + dres_ref[rows, :]
        dx_ref[rows, :] = dx
        dxb_ref[rows, :] = dx.astype(BF)
        dg_ref[...] += dgain


def _mm_nt(a, b, out_dtype, name, tm_pref=512, tn_pref=1024, tk_pref=2048, rms=None, dep=None):
    m, k = a.shape
    n = b.shape[0]
    tm, tn, tk = _tile(m, tm_pref, 16), _tile(n, tn_pref, LANE), _tile(k, tk_pref, LANE)
    nk = k // tk
    assert rms is None or tn == n

    def body(*refs):
        if rms is None:
            a_ref, b_ref, o_ref, acc_ref = refs
        else:
            a_ref, b_ref, x_ref, g_ref, dres_ref, dx_ref, dxb_ref, dg_ref, acc_ref = refs
        kk = pl.program_id(2)

        @pl.when(kk == 0)
        def _():
            acc_ref[...] = jnp.zeros_like(acc_ref)

        acc_ref[...] += _dot(a_ref[...], b_ref[...], NT)

        @pl.when(kk == nk - 1)
        def _():
            if rms is None:
                o_ref[...] = acc_ref[...].astype(out_dtype)
            else:
                _rms_bwd_tail(acc_ref, pl.program_id(1) == 0, x_ref, g_ref, dres_ref, dx_ref, dxb_ref, dg_ref)

    in_specs = [pl.BlockSpec((tm, tk), lambda j, i, kk: (i, kk)), pl.BlockSpec((tn, tk), lambda j, i, kk: (j, kk))]
    row = pl.BlockSpec((tm, tn), lambda j, i, kk: (i, j))
    if rms is None:
        args, out_specs, out_shape = [a, b], row, jax.ShapeDtypeStruct((m, n), out_dtype)
    else:
        vec = pl.BlockSpec((1, n), lambda j, i, kk: (0, 0))
        row_once = pl.BlockSpec((tm, tn), lambda j, i, kk: (i, j), pipeline_mode=pl.Buffered(1))
        args, in_specs = [a, b, *rms], in_specs + [row_once, vec, row_once]
        out_specs = [row, row, vec]
        out_shape = [jax.ShapeDtypeStruct((m, n), F32), jax.ShapeDtypeStruct((m, n), BF),
                     jax.ShapeDtypeStruct((1, n), F32)]
    return _call(
        body, args, dep=dep, name=name, grid=(n // tn, m // tm, nk), in_specs=in_specs, out_specs=out_specs,
        out_shape=out_shape, scratch_shapes=[pltpu.VMEM((tm, tn), F32)], compiler_params=_params(),
    )


def _mm_tn(a, b, out_dtype, name, tn_pref=1152, tk_pref=512, a_transposed=False):
    (k, t) = a.shape if a_transposed else a.shape[::-1]
    n = b.shape[1]
    tn, tk = _tile(n, tn_pref, LANE), _tile(t, tk_pref, LANE if a_transposed else 16)
    nt = t // tk

    def body(a_ref, b_ref, o_ref, acc_ref):
        tt = pl.program_id(1)

        @pl.when(tt == 0)
        def _():
            acc_ref[...] = jnp.zeros_like(acc_ref)

        acc_ref[...] += _dot(a_ref[...], b_ref[...], NN if a_transposed else TN)

        @pl.when(tt == nt - 1)
        def _():
            o_ref[...] = acc_ref[...].astype(out_dtype)

    if a_transposed:
        a_spec = pl.BlockSpec((k, tk), lambda j, tt: (0, tt))
    else:
        a_spec = pl.BlockSpec((tk, k), lambda j, tt: (tt, 0))
    return pl.pallas_call(
        body, name=name, grid=(n // tn, nt),
        in_specs=[a_spec, pl.BlockSpec((tk, tn), lambda j, tt: (tt, j))],
        out_specs=pl.BlockSpec((k, tn), lambda j, tt: (0, j)),
        out_shape=jax.ShapeDtypeStruct((k, n), out_dtype),
        scratch_shapes=[pltpu.VMEM((k, tn), F32)], compiler_params=_params(),
    )(a, b)


FFN_COLS = 512


FFN_ROWS_WIDE = 2048


def _ffn_tiles(t, fc, rows=FFN_ROWS):
    return _tile(t, rows, 16), _tile(fc, FFN_COLS, LANE)


def _slabs(tm, rows=256):
    step = rows if tm % rows == 0 else tm
    return [pl.ds(r, step) for r in range(0, tm, step)]


def _ffn_gate_up(hn, wg_t, wu_t, name):
    t, d = hn.shape
    fc = wg_t.shape[0]
    tm, tn = _ffn_tiles(t, fc, FFN_ROWS_WIDE)

    def body(h_ref, wg_ref, wu_ref, g_ref, u_ref, a_ref):
        for rows in _slabs(tm):
            h = h_ref[rows, :]
            g = _dot(h, wg_ref[...], NT)
            u = _dot(h, wu_ref[...], NT)
            g_ref[rows, :] = g.astype(BF)
            u_ref[rows, :] = u.astype(BF)
            a_ref[rows, :] = (g * _sig(g) * u).astype(BF)

    wspec = pl.BlockSpec((tn, d), lambda j, i: (j, 0))
    hid = pl.BlockSpec((tm, tn), lambda j, i: (i, j))
    out = jax.ShapeDtypeStruct((t, fc), BF)
    return pl.pallas_call(
        body, name=name, grid=(fc // tn, t // tm),
        in_specs=[pl.BlockSpec((tm, d), lambda j, i: (i, 0)), wspec, wspec],
        out_specs=[hid, hid, hid], out_shape=[out, out, out], compiler_params=_params(),
    )(hn, wg_t, wu_t)


def _ffn_gate(hn, wg_t, name):
    t, d = hn.shape
    fc = wg_t.shape[0]
    tm, tn = _ffn_tiles(t, fc)

    def body(h_ref, wg_ref, g_ref):
        g_ref[...] = _dot(h_ref[...], wg_ref[...], NT)

    return pl.pallas_call(
        body, name=name, grid=(fc // tn, t // tm),
        in_specs=[pl.BlockSpec((tm, d), lambda j, i: (i, 0)), pl.BlockSpec((tn, d), lambda j, i: (j, 0))],
        out_specs=pl.BlockSpec((tm, tn), lambda j, i: (i, j)),
        out_shape=jax.ShapeDtypeStruct((t, fc), F32), compiler_params=_params(),
    )(hn, wg_t)


def _ffn_up_act(hn, wu_t, g, name):
    t, d = hn.shape
    fc = wu_t.shape[0]
    tm, tn = _ffn_tiles(t, fc, FFN_ROWS_WIDE)

    def body(h_ref, wu_ref, g_ref, gb_ref, u_ref, a_ref):
        for rows in _slabs(tm):
            u = _dot(h_ref[rows, :], wu_ref[...], NT)
            gv = g_ref[rows, :]
            gb_ref[rows, :] = gv.astype(BF)
            u_ref[rows, :] = u.astype(BF)
            a_ref[rows, :] = (gv * _sig(gv) * u).astype(BF)

    hid = pl.BlockSpec((tm, tn), lambda j, i: (i, j))
    out = jax.ShapeDtypeStruct((t, fc), BF)
    return pl.pallas_call(
        body, name=name, grid=(fc // tn, t // tm),
        in_specs=[pl.BlockSpec((tm, d), lambda j, i: (i, 0)), pl.BlockSpec((tn, d), lambda j, i: (j, 0)), hid],
        out_specs=[hid, hid, hid], out_shape=[out, out, out], compiler_params=_params(),
    )(hn, wu_t, g)


def _ffn_down(act, wd, xres, name):
    t, fc = act.shape
    d = wd.shape[1]
    tm, tk = _ffn_tiles(t, fc)

    def body(a_ref, w_ref, x_ref, o_ref):
        @pl.when(pl.program_id(1) == 0)
        def _():
            o_ref[...] = x_ref[...]

        o_ref[...] += 0.5 * _dot(a_ref[...], w_ref[...])

    row = pl.BlockSpec((tm, d), lambda i, k: (i, 0))
    return pl.pallas_call(
        body, name=name, grid=(t // tm, fc // tk),
        in_specs=[pl.BlockSpec((tm, tk), lambda i, k: (i, k)), pl.BlockSpec((tk, d), lambda i, k: (k, 0)), row],
        out_specs=row, out_shape=jax.ShapeDtypeStruct((t, d), F32), compiler_params=_params(),
    )(act, wd, xres)


def _ffn_bwd_hidden(dxb, wd, g, u, name):
    t, d = dxb.shape
    fc = wd.shape[0]
    tm, tn = _ffn_tiles(t, fc, FFN_ROWS_WIDE)

    def body(dx_ref, w_ref, g_ref, u_ref, dg_ref, du_ref):
        for rows in _slabs(tm):
            dh = 0.5 * _dot(dx_ref[rows, :], w_ref[...], NT)
            gv = g_ref[rows, :].astype(F32)
            uv = u_ref[rows, :].astype(F32)
            s = _sig(gv)
            dg_ref[rows, :] = (dh * uv * (s * (1.0 + gv * (1.0 - s)))).astype(BF)
            du_ref[rows, :] = (dh * (gv * s)).astype(BF)

    hid = pl.BlockSpec((tm, tn), lambda i, j: (i, j))
    out = jax.ShapeDtypeStruct((t, fc), BF)
    return pl.pallas_call(
        body, name=name, grid=(t // tm, fc // tn),
        in_specs=[pl.BlockSpec((tm, d), lambda i, j: (i, 0)), pl.BlockSpec((tn, d), lambda i, j: (j, 0)), hid, hid],
        out_specs=[hid, hid], out_shape=[out, out], compiler_params=_params(),
    )(dxb, wd, g, u)


def _ffn_dw(lhs, rhs, scale, name, dep=None):
    n = len(lhs)
    t, fc = lhs[0].shape
    d = rhs.shape[1]
    tk, tn = _tile(t, DW_ROWS, 16), _tile(fc, FFN_COLS, LANE)
    nt = t // tk

    def body(*refs):
        l_refs, r_ref, o_refs, acc_refs = refs[:n], refs[n], refs[n + 1:2 * n + 1], refs[2 * n + 1:]
        tt = pl.program_id(1)
        r = r_ref[...]
        for l_ref, o_ref, acc_ref in zip(l_refs, o_refs, acc_refs):
            @pl.when(tt == 0)
            def _():
                acc_ref[...] = jnp.zeros_like(acc_ref)

            acc_ref[...] += _dot(l_ref[...], r, TN)

            @pl.when(tt == nt - 1)
            def _():
                o_ref[...] = (scale * acc_ref[...]).astype(BF)

    lspec = pl.BlockSpec((tk, tn), lambda j, tt: (tt, j))
    ospec = pl.BlockSpec((tn, d), lambda j, tt: (j, 0))
    out = jax.ShapeDtypeStruct((fc, d), BF)
    return _call(
        body, [*lhs, rhs], dep=dep, name=name, grid=(fc // tn, nt),
        in_specs=[lspec] * n + [pl.BlockSpec((tk, d), lambda j, tt: (tt, 0))],
        out_specs=[ospec] * n, out_shape=[out] * n,
        scratch_shapes=[pltpu.VMEM((tn, d), F32)] * n, compiler_params=_params(),
    )


def _rms_bwd(dy, x, gain, dres, name):
    t, d = x.shape
    tr = _tile(t, 256, 16)

    def body(dy_ref, x_ref, g_ref, dres_ref, dx_ref, dxb_ref, dg_ref):
        _rms_bwd_tail(dy_ref, pl.program_id(0) == 0, x_ref, g_ref, dres_ref, dx_ref, dxb_ref, dg_ref)

    row = pl.BlockSpec((tr, d), lambda i: (i, 0))
    vec = pl.BlockSpec((1, d), lambda i: (0, 0))
    return pl.pallas_call(
        body, name=name, grid=(t // tr,),
        in_specs=[row, row, vec, row], out_specs=[row, row, vec],
        out_shape=[jax.ShapeDtypeStruct((t, d), F32), jax.ShapeDtypeStruct((t, d), BF),
                   jax.ShapeDtypeStruct((1, d), F32)],
        compiler_params=_params(),
    )(dy, x, gain, dres)


def _ffn_bwd_input(dg, du, wg_t, wu_t, name, dep=None):
    t, fc = dg.shape
    d = wg_t.shape[1]
    tm, tk = _ffn_tiles(t, fc)

    def body(dg_ref, du_ref, wg_ref, wu_ref, o_ref):
        @pl.when(pl.program_id(1) == 0)
        def _():
            o_ref[...] = jnp.zeros_like(o_ref)

        o_ref[...] += _dot(dg_ref[...], wg_ref[...]) + _dot(du_ref[...], wu_ref[...])

    hid = pl.BlockSpec((tm, tk), lambda i, k: (i, k))
    wspec = pl.BlockSpec((tk, d), lambda i, k: (k, 0))
    return _call(
        body, [dg, du, wg_t, wu_t], dep=dep, name=name, grid=(t // tm, fc // tk),
        in_specs=[hid, hid, wspec, wspec],
        out_specs=pl.BlockSpec((tm, d), lambda i, k: (i, 0)),
        out_shape=jax.ShapeDtypeStruct((t, d), F32), compiler_params=_params(),
    )


def _rope_tables(t):
    pos = jnp.arange(t, dtype=F32)
    inv_freq = ROPE_THETA ** (-jnp.arange(0, ROPE_DIM, 2, dtype=F32) / ROPE_DIM)
    ang = pos[:, None] * inv_freq[None, :]
    cos, sin = jnp.cos(ang), jnp.sin(ang)
    rest = HEAD_DIM - ROPE_DIM
    one = jnp.ones((t, rest), F32)
    zero_h = jnp.zeros((t, ROPE_HALF), F32)
    zero_r = jnp.zeros((t, rest), F32)
    c = jnp.concatenate([cos, cos, one], axis=1)
    s1 = jnp.concatenate([-sin, zero_h, zero_r], axis=1)
    s2 = jnp.concatenate([zero_h, sin, zero_r], axis=1)
    return c, s1, s2


def _rope(xh, c, s1, s2):
    return xh * c + pltpu.roll(xh, HEAD_DIM - ROPE_HALF, 1) * s1 + pltpu.roll(xh, ROPE_HALF, 1) * s2


def _rope_t(dh, c, s1, s2):
    return dh * c + pltpu.roll(dh * s1, ROPE_HALF, 1) + pltpu.roll(dh * s2, HEAD_DIM - ROPE_HALF, 1)


def _mixer_prep(proj, tables, bf_pad, hd, scale):
    t, np_ = proj.shape
    tr = _tile(t, 256, 16)
    nh = hd // HEAD_DIM
    nblk = hd // LANE
    f_blk = (np_ - F_PAD) // LANE

    def body(qd_ref, kd_ref, vd_ref, qf_ref, kf_ref, vf_ref, fl_ref, c_ref, s1_ref, s2_ref, b_ref,
             oqd, okd, ovd, oqf, okf, ovf, olog):
        c, s1, s2 = c_ref[...], s1_ref[...], s2_ref[...]
        for h in range(nh):
            sl = slice(h * HEAD_DIM, (h + 1) * HEAD_DIM)
            oqd[:, sl] = (_rope(qd_ref[:, sl], c, s1, s2) * scale).astype(BF)
            okd[:, sl] = _rope(kd_ref[:, sl], c, s1, s2).astype(BF)
        ovd[...] = vd_ref[...].astype(BF)
        oqf[...] = (qf_ref[...] * scale).astype(BF)
        okf[...] = kf_ref[...].astype(BF)
        ovf[...] = vf_ref[...].astype(BF)
        z = fl_ref[...] + b_ref[...]
        olog[...] = jnp.minimum(z, 0.0) - jnp.log(1.0 + jnp.exp(-jnp.abs(z)))

    def col(kblk):
        return pl.BlockSpec((tr, hd), lambda i, kblk=kblk: (i, kblk))

    lane_row = pl.BlockSpec((tr, LANE), lambda i: (i, 0))
    in_specs = [col(0), col(1), col(2), col(3), col(4), col(5),
                pl.BlockSpec((tr, LANE), lambda i: (i, f_blk)),
                lane_row, lane_row, lane_row, pl.BlockSpec((1, LANE), lambda i: (0, 0))]
    o = pl.BlockSpec((tr, hd), lambda i: (i, 0))
    ob = jax.ShapeDtypeStruct((t, hd), BF)
    del nblk
    return pl.pallas_call(
        body, name="mixer_prep", grid=(t // tr,), in_specs=in_specs,
        out_specs=[o, o, o, o, o, o, lane_row],
        out_shape=[ob, ob, ob, ob, ob, ob, jax.ShapeDtypeStruct((t, LANE), F32)],
        compiler_params=_params(),
    )(proj, proj, proj, proj, proj, proj, proj, *tables, bf_pad)


def _split3(x):
    x1 = x.astype(BF)
    r1 = x - x1.astype(F32)
    x2 = r1.astype(BF)
    x3 = (r1 - x2.astype(F32)).astype(BF)
    return x1, x2, x3


def _cumsum_rows(x, reverse, name):
    t, w = x.shape
    blk = LANE
    nb = t // blk

    def body(x_ref, o_ref):
        r = lax.broadcasted_iota(jnp.int32, (blk, blk), 0)
        c = lax.broadcasted_iota(jnp.int32, (blk, blk), 1)
        tri = jnp.where((c >= r) if reverse else (c <= r), 1.0, 0.0).astype(BF)

        def step(i, carry):
            b = (nb - 1 - i) if reverse else i
            off = pl.multiple_of(b * blk, blk)
            xb = x_ref[pl.ds(off, blk), :]
            x1, x2, x3 = _split3(xb)
            o_ref[pl.ds(off, blk), :] = _dot(tri, x1) + _dot(tri, x2) + _dot(tri, x3) + carry
            return carry + jnp.sum(xb, axis=0, keepdims=True)

        lax.fori_loop(0, nb, step, jnp.zeros((1, w), F32))

    return pl.pallas_call(body, name=name, out_shape=jax.ShapeDtypeStruct((t, w), F32),
                          compiler_params=_params())(x)


ATTN_ROWS = 16


def _dil_bias_tiles(tq):
    nbias = MAX_WINDOW // tq + 1
    b = lax.broadcasted_iota(jnp.int32, (nbias, tq, tq), 0)
    i = lax.broadcasted_iota(jnp.int32, (nbias, tq, tq), 1)
    j = lax.broadcasted_iota(jnp.int32, (nbias, tq, tq), 2)
    delta = b * tq + i - j
    mult = jnp.zeros((nbias, tq, tq), F32)
    for w, dil in DIL_PATTERNS:
        mult = mult + jnp.where((delta >= 0) & (delta <= w) & (delta % dil == 0), 1.0, 0.0)
    return jnp.where(mult > 0.0, jnp.log(jnp.maximum(mult, 1.0)), NEG)


def _rep(x, width):
    return jnp.tile(x, (1, width // LANE))


def _chunks(n_rows, fn):
    for c in range(n_rows // ATTN_ROWS):
        fn(c * ATTN_ROWS)


def _causal(r0, tq, transposed):
    a = lax.broadcasted_iota(jnp.int32, (ATTN_ROWS, tq), 0) + r0
    b = lax.broadcasted_iota(jnp.int32, (ATTN_ROWS, tq), 1)
    return (a <= b) if transposed else (b <= a)


def _rows8(x):
    return jnp.transpose(x)[:8, :]


def _attn_fwd(mode, q, k, v, bias, tq, name):
    t, hd = q.shape
    nh = hd // HEAD_DIM
    nb = t // tq
    wb = MAX_WINDOW // tq
    fox = mode == "fox"

    def body(q_ref, k_ref, v_ref, b_ref, o_ref, lse_ref, lse_row_ref, s_ref, p_ref, m_ref, l_ref, acc_ref):
        qi = pl.program_id(1)
        qb = q_ref[...]
        m_ref[...] = jnp.full_like(m_ref, NEG)
        l_ref[...] = jnp.zeros_like(l_ref)
        acc_ref[...] = jnp.zeros_like(acc_ref)

        def tile(kj, diag):
            off = pl.multiple_of(kj * tq, tq)
            s_ref[...] = _dot(qb, k_ref[pl.ds(off, tq), :], NT)
            if fox:
                brow = b_ref[qi][:, :1] - b_ref[kj]

            def chunk(r0):
                rows = pl.ds(r0, ATTN_ROWS)
                if fox:
                    s = s_ref[rows, :] + brow
                    if diag:
                        s = jnp.where(_causal(r0, tq, False), s, NEG)
                else:
                    s = s_ref[rows, :] + b_ref[qi - kj, rows, :]
                m_old = m_ref[rows, :]
                m_new = jnp.maximum(m_old, jnp.max(s, axis=1, keepdims=True))
                p = jnp.exp(s - _rep(m_new, tq))
                alpha = jnp.exp(m_old - m_new)
                l_ref[rows, :] = alpha * l_ref[rows, :] + jnp.sum(p, axis=1, keepdims=True)
                m_ref[rows, :] = m_new
                acc_ref[rows, :] = alpha * acc_ref[rows, :]
                p_ref[rows, :] = p.astype(BF)

            _chunks(tq, chunk)
            acc_ref[...] += _dot(p_ref[...], v_ref[pl.ds(off, tq), :])

        tile(qi, True)
        if fox:
            lax.fori_loop(0, qi, lambda kj, c: (tile(kj, False), c)[1], 0)
        else:
            lax.fori_loop(1, jnp.minimum(qi, wb) + 1, lambda i, c: (tile(qi - i, False), c)[1], 0)
        o_ref[...] = (acc_ref[...] / l_ref[...]).astype(BF)
        lse = m_ref[...] + jnp.log(l_ref[...])
        lse_ref[...] = lse
        lse_row_ref[...] = _rows8(lse)

    qspec = pl.BlockSpec((tq, HEAD_DIM), lambda h, i: (i, h))
    kvspec = pl.BlockSpec((t, HEAD_DIM), lambda h, i: (0, h))
    repspec = pl.BlockSpec((None, tq, LANE), lambda h, i: (h, i, 0))
    row8spec = pl.BlockSpec((None, None, 8, tq), lambda h, i: (h, i, 0, 0))
    if fox:
        bspec = pl.BlockSpec((None, nb, 1, tq), lambda h, i: (h, 0, 0, 0))
    else:
        bspec = pl.BlockSpec((wb + 1, tq, tq), lambda h, i: (0, 0, 0))
    return pl.pallas_call(
        body, name=name, grid=(nh, nb), in_specs=[qspec, kvspec, kvspec, bspec],
        out_specs=[qspec, repspec, row8spec],
        out_shape=[jax.ShapeDtypeStruct((t, hd), BF), jax.ShapeDtypeStruct((nh, t, LANE), F32),
                   jax.ShapeDtypeStruct((nh, nb, 8, tq), F32)],
        scratch_shapes=[pltpu.VMEM((tq, tq), F32), pltpu.VMEM((tq, tq), BF), pltpu.VMEM((tq, LANE), F32),
                        pltpu.VMEM((tq, LANE), F32), pltpu.VMEM((tq, HEAD_DIM), F32)],
        compiler_params=_params(),
    )(q, k, v, bias)


def _attn_bwd_dq(mode, q, k, v, o, do, lse, bias, tq, name, dep=None):
    t, hd = q.shape
    nh = hd // HEAD_DIM
    nb = t // tq
    wb = MAX_WINDOW // tq
    fox = mode == "fox"

    def body(q_ref, k_ref, v_ref, o_ref, do_ref, lse_ref, b_ref, dq_ref, dl_row_ref,
             s_ref, dp_ref, x_ref, y_ref, acc_ref, acc2_ref, dl_ref):
        qi = pl.program_id(1)
        qb = q_ref[...]
        dob = do_ref[...]
        acc_ref[...] = jnp.zeros_like(acc_ref)
        if fox:
            acc2_ref[...] = jnp.zeros_like(acc2_ref)
            dl_ref[...] = jnp.zeros_like(dl_ref)
        else:
            prod = o_ref[...].astype(F32) * dob.astype(F32)
            dl_ref[...] = jnp.broadcast_to(jnp.sum(prod, axis=1, keepdims=True), (tq, LANE))

        def tile(kj, diag):
            off = pl.multiple_of(kj * tq, tq)
            kb = k_ref[pl.ds(off, tq), :]
            s_ref[...] = _dot(qb, kb, NT)
            dp_ref[...] = _dot(dob, v_ref[pl.ds(off, tq), :], NT)
            if fox:
                brow = b_ref[qi][:, :1] - b_ref[kj]

            def chunk(r0):
                rows = pl.ds(r0, ATTN_ROWS)
                lse_c = _rep(lse_ref[rows, :], tq)
                if fox:
                    s = s_ref[rows, :] + brow
                    if diag:
                        s = jnp.where(_causal(r0, tq, False), s, NEG)
                    p = jnp.exp(s - lse_c)
                    pdp = p * dp_ref[rows, :]
                    dl_ref[rows, :] += jnp.sum(pdp, axis=1, keepdims=True)
                    x_ref[rows, :] = pdp.astype(BF)
                    y_ref[rows, :] = p.astype(BF)
                else:
                    p = jnp.exp(s_ref[rows, :] + b_ref[qi - kj, rows, :] - lse_c)
                    x_ref[rows, :] = (p * (dp_ref[rows, :] - _rep(dl_ref[rows, :], tq))).astype(BF)

            _chunks(tq, chunk)
            acc_ref[...] += _dot(x_ref[...], kb)
            if fox:
                acc2_ref[...] += _dot(y_ref[...], kb)

        tile(qi, True)
        if fox:
            lax.fori_loop(0, qi, lambda kj, c: (tile(kj, False), c)[1], 0)
            dq_ref[...] = acc_ref[...] - dl_ref[...] * acc2_ref[...]
        else:
            lax.fori_loop(1, jnp.minimum(qi, wb) + 1, lambda i, c: (tile(qi - i, False), c)[1], 0)
            dq_ref[...] = acc_ref[...]
        dl_row_ref[...] = _rows8(dl_ref[...])

    qspec = pl.BlockSpec((tq, HEAD_DIM), lambda h, i: (i, h))
    kvspec = pl.BlockSpec((t, HEAD_DIM), lambda h, i: (0, h))
    repspec = pl.BlockSpec((None, tq, LANE), lambda h, i: (h, i, 0))
    row8spec = pl.BlockSpec((None, None, 8, tq), lambda h, i: (h, i, 0, 0))
    if fox:
        bspec = pl.BlockSpec((None, nb, 1, tq), lambda h, i: (h, 0, 0, 0))
    else:
        bspec = pl.BlockSpec((wb + 1, tq, tq), lambda h, i: (0, 0, 0))
    return _call(
        body, [q, k, v, o, do, lse, bias], dep=dep, name=name, grid=(nh, nb),
        in_specs=[qspec, kvspec, kvspec, qspec, qspec, repspec, bspec],
        out_specs=[qspec, row8spec],
        out_shape=[jax.ShapeDtypeStruct((t, hd), F32), jax.ShapeDtypeStruct((nh, nb, 8, tq), F32)],
        scratch_shapes=[pltpu.VMEM((tq, tq), F32), pltpu.VMEM((tq, tq), F32), pltpu.VMEM((tq, tq), BF),
                        pltpu.VMEM((tq, tq), BF), pltpu.VMEM((tq, HEAD_DIM), F32),
                        pltpu.VMEM((tq, HEAD_DIM), F32), pltpu.VMEM((tq, LANE), F32)],
        compiler_params=_params(),
    )


def _attn_bwd_dkv(mode, q, k, v, do, lse_row, dl_row, bias_t, c_row, tq, name):
    t, hd = q.shape
    nh = hd // HEAD_DIM
    nb = t // tq
    wb = MAX_WINDOW // tq
    fox = mode == "fox"

    def body(*refs):
        if fox:
            (q_ref, k_ref, v_ref, do_ref, lse_ref, dl_ref, b_ref, cq_ref, dk_ref, dv_ref, dc_row_ref,
             s_ref, dp_ref, x_ref, y_ref, dc_ref) = refs
        else:
            q_ref, k_ref, v_ref, do_ref, lse_ref, dl_ref, b_ref, dk_ref, dv_ref, s_ref, dp_ref, x_ref, y_ref = refs
        kj = pl.program_id(1)
        kb = k_ref[...]
        vb = v_ref[...]
        dk_ref[...] = jnp.zeros_like(dk_ref)
        dv_ref[...] = jnp.zeros_like(dv_ref)
        if fox:
            dc_ref[...] = jnp.zeros_like(dc_ref)

        def tile(qi, diag):
            off = pl.multiple_of(qi * tq, tq)
            qb = q_ref[pl.ds(off, tq), :]
            dob = do_ref[pl.ds(off, tq), :]
            s_ref[...] = _dot(kb, qb, NT)
            dp_ref[...] = _dot(vb, dob, NT)
            lse_r = lse_ref[qi, 0:1, :]
            dl_r = dl_ref[qi, 0:1, :]
            if fox:
                kbias = cq_ref[qi][:, :1] - b_ref[...]

            def chunk(r0):
                rows = pl.ds(r0, ATTN_ROWS)
                if fox:
                    s = s_ref[rows, :] + _rep(kbias[r0:r0 + ATTN_ROWS, :], tq)
                    if diag:
                        s = jnp.where(_causal(r0, tq, True), s, NEG)
                else:
                    s = s_ref[rows, :] + b_ref[qi - kj, rows, :]
                pt = jnp.exp(s - lse_r)
                dst = pt * (dp_ref[rows, :] - dl_r)
                x_ref[rows, :] = pt.astype(BF)
                y_ref[rows, :] = dst.astype(BF)
                if fox:
                    dc_ref[rows, :] -= jnp.sum(dst, axis=1, keepdims=True)

            _chunks(tq, chunk)
            dv_ref[...] += _dot(x_ref[...], dob)
            dk_ref[...] += _dot(y_ref[...], qb)

        tile(kj, True)
        hi = nb if fox else jnp.minimum(kj + wb + 1, nb)
        lax.fori_loop(kj + 1, hi, lambda qi, c: (tile(qi, False), c)[1], 0)
        if fox:
            dc_row_ref[...] = _rows8(dc_ref[...])

    blkspec = pl.BlockSpec((tq, HEAD_DIM), lambda h, j: (j, h))
    fullspec = pl.BlockSpec((t, HEAD_DIM), lambda h, j: (0, h))
    rows8spec = pl.BlockSpec((None, nb, 8, tq), lambda h, j: (h, 0, 0, 0))
    repspec = pl.BlockSpec((None, tq, LANE), lambda h, j: (h, j, 0))
    in_specs = [fullspec, blkspec, blkspec, fullspec, rows8spec, rows8spec]
    args = [q, k, v, do, lse_row, dl_row, bias_t]
    out_specs = [blkspec, blkspec]
    out_shape = [jax.ShapeDtypeStruct((t, hd), F32), jax.ShapeDtypeStruct((t, hd), F32)]
    scratch = [pltpu.VMEM((tq, tq), F32), pltpu.VMEM((tq, tq), F32), pltpu.VMEM((tq, tq), BF),
               pltpu.VMEM((tq, tq), BF)]
    if fox:
        in_specs += [repspec, pl.BlockSpec((None, nb, 1, tq), lambda h, j: (h, 0, 0, 0))]
        args.append(c_row)
        out_specs.append(pl.BlockSpec((None, None, 8, tq), lambda h, j: (h, j, 0, 0)))
        out_shape.append(jax.ShapeDtypeStruct((nh, nb, 8, tq), F32))
        scratch.append(pltpu.VMEM((tq, LANE), F32))
    else:
        in_specs.append(pl.BlockSpec((wb + 1, tq, tq), lambda h, j: (0, 0, 0)))
    return pl.pallas_call(
        body, name=name, grid=(nh, nb), in_specs=in_specs, out_specs=out_specs, out_shape=out_shape,
        scratch_shapes=scratch, compiler_params=_params(),
    )(*args)


def _gate_specs(t, d, hd, tr):
    row = pl.BlockSpec((tr, d), lambda i: (i, 0))
    vec = pl.BlockSpec((1, d), lambda i: (0, 0))
    base = 6 * hd // d
    gd = pl.BlockSpec((tr, d), lambda i: (i, base))
    gf = pl.BlockSpec((tr, d), lambda i: (i, base + 1))
    return row, vec, gd, gf


def _proj_merge(yd, yf, wpd, wpf, proj, b_d, b_f, hd):
    t = yd.shape[0]
    d = wpd.shape[1]
    tr = _tile(t, 256, 16)
    row, vec, gd, gf = _gate_specs(t, d, hd, tr)

    def body(yd_ref, yf_ref, wd_ref, wf_ref, gd_ref, gf_ref, bd_ref, bf_ref, pd_ref, pf_ref, o_ref):
        pd = _dot(yd_ref[...], wd_ref[...])
        pf = _dot(yf_ref[...], wf_ref[...])
        pd_ref[...] = pd
        pf_ref[...] = pf
        o_ref[...] = (_sig(gd_ref[...] + bd_ref[...]) * pd + _sig(gf_ref[...] + bf_ref[...]) * pf).astype(BF)

    yspec = pl.BlockSpec((tr, hd), lambda i: (i, 0))
    wspec = pl.BlockSpec((hd, d), lambda i: (0, 0))
    f32 = jax.ShapeDtypeStruct((t, d), F32)
    return pl.pallas_call(
        body, name="proj_merge", grid=(t // tr,), in_specs=[yspec, yspec, wspec, wspec, gd, gf, vec, vec],
        out_specs=[row, row, row], out_shape=[f32, f32, jax.ShapeDtypeStruct((t, d), BF)],
        compiler_params=_params(),
    )(yd, yf, wpd, wpf, proj, proj, b_d, b_f)


def _merge_bwd(dm, pd, pf, proj, b_d, b_f, hd):
    t, d = pd.shape
    tr = _tile(t, 256, 16)
    row, vec, gd, gf = _gate_specs(t, d, hd, tr)

    def body(dm_ref, pd_ref, pf_ref, gd_ref, gf_ref, bd_ref, bf_ref,
             dpd_ref, dpf_ref, dgd_ref, dgf_ref, dbd_ref, dbf_ref):
        dmv = dm_ref[...]
        sd = _sig(gd_ref[...] + bd_ref[...])
        sf = _sig(gf_ref[...] + bf_ref[...])
        dgd = dmv * pd_ref[...] * (sd * (1.0 - sd))
        dgf = dmv * pf_ref[...] * (sf * (1.0 - sf))
        dpd_ref[...] = (dmv * sd).astype(BF)
        dpf_ref[...] = (dmv * sf).astype(BF)
        dgd_ref[...] = dgd.astype(BF)
        dgf_ref[...] = dgf.astype(BF)

        @pl.when(pl.program_id(0) == 0)
        def _():
            dbd_ref[...] = jnp.zeros_like(dbd_ref)
            dbf_ref[...] = jnp.zeros_like(dbf_ref)

        dbd_ref[...] += jnp.sum(dgd, axis=0, keepdims=True)
        dbf_ref[...] += jnp.sum(dgf, axis=0, keepdims=True)

    ob = jax.ShapeDtypeStruct((t, d), BF)
    ov = jax.ShapeDtypeStruct((1, d), F32)
    return pl.pallas_call(
        body, name="merge_bwd", grid=(t // tr,), in_specs=[row, row, row, gd, gf, vec, vec],
        out_specs=[row, row, row, row, vec, vec], out_shape=[ob, ob, ob, ob, ov, ov],
        compiler_params=_params(),
    )(dm, pd, pf, proj, proj, b_d, b_f)


def _assemble_dproj(dqd, dkd, dvd, dqf, dkf, dvf, dgd, dgf, dlogf, proj, tables, bf_pad, scale):
    t, np_ = proj.shape
    hd = dqd.shape[1]
    d = dgd.shape[1]
    nh = hd // HEAD_DIM
    tr = _tile(t, 256, 16)
    f_blk = (np_ - F_PAD) // LANE

    def body(dqd_ref, dkd_ref, dvd_ref, dqf_ref, dkf_ref, dvf_ref, dgd_ref, dgf_ref, dlog_ref, fl_ref,
             c_ref, s1_ref, s2_ref, b_ref, o_ref, db_ref):
        c, s1, s2 = c_ref[...], s1_ref[...], s2_ref[...]
        for h in range(nh):
            sl = slice(h * HEAD_DIM, (h + 1) * HEAD_DIM)
            o_ref[:, sl] = (_rope_t(dqd_ref[:, sl], c, s1, s2) * scale).astype(BF)
            o_ref[:, hd + h * HEAD_DIM:hd + (h + 1) * HEAD_DIM] = _rope_t(dkd_ref[:, sl], c, s1, s2).astype(BF)
        o_ref[:, 2 * hd:3 * hd] = dvd_ref[...].astype(BF)
        o_ref[:, 3 * hd:4 * hd] = (dqf_ref[...] * scale).astype(BF)
        o_ref[:, 4 * hd:5 * hd] = dkf_ref[...].astype(BF)
        o_ref[:, 5 * hd:6 * hd] = dvf_ref[...].astype(BF)
        o_ref[:, 6 * hd:6 * hd + d] = dgd_ref[...]
        o_ref[:, 6 * hd + d:6 * hd + 2 * d] = dgf_ref[...]
        z = fl_ref[...] + b_ref[...]
        dfl = dlog_ref[...] * _sig(-z)
        o_ref[:, 6 * hd + 2 * d:6 * hd + 2 * d + LANE] = dfl.astype(BF)
        o_ref[:, 6 * hd + 2 * d + LANE:] = jnp.zeros((tr, F_PAD - LANE), BF)

        @pl.when(pl.program_id(0) == 0)
        def _():
            db_ref[...] = jnp.zeros_like(db_ref)

        db_ref[...] += jnp.sum(dfl, axis=0, keepdims=True)

    head = pl.BlockSpec((tr, hd), lambda i: (i, 0))
    row = pl.BlockSpec((tr, d), lambda i: (i, 0))
    lane_row = pl.BlockSpec((tr, LANE), lambda i: (i, 0))
    lane_vec = pl.BlockSpec((1, LANE), lambda i: (0, 0))
    return pl.pallas_call(
        body, name="assemble_dproj", grid=(t // tr,),
        in_specs=[head] * 6 + [row, row, lane_row, pl.BlockSpec((tr, LANE), lambda i: (i, f_blk)),
                               lane_row, lane_row, lane_row, lane_vec],
        out_specs=[pl.BlockSpec((tr, np_), lambda i: (i, 0)), lane_vec],
        out_shape=[jax.ShapeDtypeStruct((t, np_), BF), jax.ShapeDtypeStruct((1, LANE), F32)],
        compiler_params=_params(),
    )(dqd, dkd, dvd, dqf, dkf, dvf, dgd, dgf, dlogf, proj, *tables, bf_pad)


def _to_rows(a, tq):
    h, t = a.shape
    return a.reshape(h, t // tq, 1, tq)


def kernel(x, ffn1_norm, ffn1_w_gate, ffn1_w_up, ffn1_w_down, mix_norm, w_in, b_forget, b_gate_dil, b_gate_fox, w_proj_dil, w_proj_fox, w_out, ffn2_norm, ffn2_w_gate, ffn2_w_up, ffn2_w_down, final_norm, loss_target, m_ffn1_norm, m_ffn1_w_gate, m_ffn1_w_up, m_ffn1_w_down, m_mix_norm, m_w_in, m_b_forget, m_b_gate_dil, m_b_gate_fox, m_w_proj_dil, m_w_proj_fox, m_w_out, m_ffn2_norm, m_ffn2_w_gate, m_ffn2_w_up, m_ffn2_w_down, m_final_norm, v_ffn1_norm, v_ffn1_w_gate, v_ffn1_w_up, v_ffn1_w_down, v_mix_norm, v_w_in, v_b_forget, v_b_gate_dil, v_b_gate_fox, v_w_proj_dil, v_w_proj_fox, v_w_out, v_ffn2_norm, v_ffn2_w_gate, v_ffn2_w_up, v_ffn2_w_down, v_final_norm):
    t, d = x.shape[1], x.shape[2]
    hd = w_proj_dil.shape[1]
    nh = hd // HEAD_DIM
    n_f = b_forget.shape[1]
    cols = w_in.shape[2]
    in_cols = N_DEV * cols
    assert in_cols == 6 * hd + n_f + 2 * d and n_f == nh and n_f <= LANE
    np_ = 6 * hd + 2 * d + F_PAD
    scale = HEAD_DIM ** -0.5
    tq = _tile(t, 512, LANE)
    assert MAX_WINDOW % tq == 0 and tq % 16 == 0

    x2d = x[0]
    tgt = loss_target[0]

    def rows(w):
        return jnp.swapaxes(w, 1, 2)

    fc = N_DEV * ffn1_w_down.shape[1]
    ag_order = [rows(ffn1_w_gate), rows(ffn1_w_up), ffn1_w_down, w_in, w_proj_dil, w_proj_fox, w_out,
                rows(ffn2_w_gate), rows(ffn2_w_up), ffn2_w_down]
    ag_first, tok = _exchange_start([w[0].astype(BF) for w in ag_order[:2]], True, "ag_start_first", ks=FIRST_LEVEL)
    ag_rest, ag_token = _exchange_start([w[0].astype(BF) for w in ag_order[2:]], True, "ag_start", dep=tok,
                                        ks=FIRST_LEVEL)
    ag = ag_first + ag_rest

    def relay(idx, after, name):
        for i, h in zip(idx, _gather_relay([ag[i] for i in idx], after, name)):
            ag[i] = h

    def gathered(idx, after, name):
        return _gather_wait([ag[i] for i in idx], after, name)

    def ffn_weight(idx, after, name):
        return [w.reshape(fc, d) for w in gathered(idx, after, name)]

    tables = _rope_tables(t)
    bf_pad = jnp.pad(b_forget, ((0, 0), (0, LANE - n_f)))

    hn1, = _rms_fwd(x2d, ffn1_norm, "rms_ffn1", dep=ag_token)
    relay([0], hn1, "ag_relay_ffn1_gate")
    wg1, = ffn_weight([0], hn1, "ag_wait_ffn1_gate")
    g1_f32 = _ffn_gate(hn1, wg1, "ffn1_gate")
    relay([1], g1_f32, "ag_relay_ffn1_up")
    wu1, = ffn_weight([1], g1_f32, "ag_wait_ffn1_up")
    relay([2], wu1, "ag_relay_ffn1_down")
    g1, u1, a1 = _ffn_up_act(hn1, wu1, g1_f32, "ffn1_up_act")
    wd1, = ffn_weight([2], a1, "ag_wait_ffn1_down")
    relay([3], wd1, "ag_relay_w_in")
    x1 = _ffn_down(a1, wd1, x2d, "ffn1_down")

    hm, hm_t = _rms_fwd(x1, mix_norm, "rms_mix", with_transpose=True)
    win_g, = gathered([3], hm, "ag_wait_w_in")
    relay([4, 5, 6], win_g, "ag_relay_mixer")
    segments = [(0, 6 * hd), (6 * hd + n_f, in_cols), (6 * hd, 6 * hd + n_f)]
    pieces = []
    for lo, hi in segments:
        for j in range(lo // cols, (hi - 1) // cols + 1):
            s, e = max(lo, j * cols), min(hi, (j + 1) * cols)
            pieces.append(win_g[j, :, s - j * cols:e - j * cols])
    win_p = jnp.concatenate(pieces + [jnp.zeros((d, F_PAD - n_f), BF)], axis=1)
    proj = _mm_nn(hm, win_p, F32, "w_in_fwd", tn_pref=W_IN_COLS)
    qd, kd, vd, qf, kf, vf, logf = _mixer_prep(proj, tables, bf_pad, hd, scale)
    csum = _cumsum_rows(logf, False, "cumsum_logf")
    c_heads = csum[:, :nh].T
    c_row = _to_rows(c_heads, tq)
    c_rep = jnp.broadcast_to(c_heads[:, :, None], (nh, t, LANE))
    dil_bias = _dil_bias_tiles(tq)
    dil_bias_t = dil_bias.transpose(0, 2, 1)
    relay([7, 8, 9], qd, "ag_relay_ffn2")
    yd, lse_d, lse_d_row = _attn_fwd("dil", qd, kd, vd, dil_bias, tq, "attn_dil_fwd")
    yf, lse_f, lse_f_row = _attn_fwd("fox", qf, kf, vf, c_row, tq, "attn_fox_fwd")
    wpd_g, wpf_g = gathered([4, 5], yf, "ag_wait_proj")
    wpd = wpd_g.transpose(1, 0, 2).reshape(hd, d)
    wpf = wpf_g.transpose(1, 0, 2).reshape(hd, d)
    pd, pf, merged = _proj_merge(yd, yf, wpd, wpf, proj, b_gate_dil, b_gate_fox, hd)
    wout_g, = gathered([6], merged, "ag_wait_w_out")
    wout = wout_g.reshape(d, d)
    x2 = _mm_nn(merged, wout, F32, "w_out_fwd", residual=x1, tn_pref=1024)

    hn2, = _rms_fwd(x2, ffn2_norm, "rms_ffn2")
    wg2, wu2 = ffn_weight([7, 8], hn2, "ag_wait_ffn2_gate_up")
    g2, u2, a2 = _ffn_gate_up(hn2, wg2, wu2, "ffn2_gate_up")
    wd2, = ffn_weight([9], a2, "ag_wait_ffn2_down")
    x3 = _ffn_down(a2, wd2, x2, "ffn2_down")

    dx3, dx3b, d_final, loss_lanes = _loss_head(x3, final_norm.reshape(1, d), tgt)

    def ffn_bwd(dxb, hn, g, u, a, wg_t, wu_t, wd, x_in, gain, dres, tag):
        def parts(dw):
            return dw.reshape(N_DEV, fc // N_DEV, d)

        dg, du = _ffn_bwd_hidden(dxb, wd, g, u, tag + "_bwd_hidden")
        dwd, = _ffn_dw([a], dxb, 0.5, tag + "_dw_down")
        rs_down, tok = _exchange_start([parts(dwd)], False, "rs_start_" + tag + "_down")
        dwg_t, dwu_t = _ffn_dw([dg, du], hn, 1.0, tag + "_dw_gate_up", dep=tok)
        rs_gu, tok = _exchange_start([parts(dwg_t), parts(dwu_t)], False, "rs_start_" + tag + "_gate_up")
        dhn = _ffn_bwd_input(dg, du, wg_t, wu_t, tag + "_bwd_input", dep=tok)
        dx, dx_bf, dgain = _rms_bwd(dhn, x_in, gain, dres, "rms_" + tag + "_bwd")
        return dx, dx_bf, dgain, rs_gu + rs_down

    dx2, dx2b, d_ffn2_norm, rs_ffn2 = ffn_bwd(dx3b, hn2, g2, u2, a2, wg2, wu2, wd2, x2, ffn2_norm, dx3, "ffn2")

    dmerged = _mm_nt(dx2b, wout, F32, "w_out_bwd")
    dwout = _mm_tn(merged, dx2b, BF, "w_out_dw", tn_pref=1024, tk_pref=DW_ROWS)
    dpd, dpf, dgd, dgf, d_bd, d_bf = _merge_bwd(dmerged, pd, pf, proj, b_gate_dil, b_gate_fox, hd)
    dyd = _mm_nt(dpd, wpd, BF, "proj_dil_bwd")
    dyf = _mm_nt(dpf, wpf, BF, "proj_fox_bwd")
    dwpd = _mm_tn(yd, dpd, BF, "proj_dil_dw", tn_pref=1024, tk_pref=DW_ROWS)
    dwpf = _mm_tn(yf, dpf, BF, "proj_fox_dw", tn_pref=1024, tk_pref=DW_ROWS)
    dwpd_c = dwpd.reshape(hd, N_DEV, d // N_DEV).transpose(1, 0, 2)
    dwpf_c = dwpf.reshape(hd, N_DEV, d // N_DEV).transpose(1, 0, 2)
    dwout_c = dwout.reshape(N_DEV, d // N_DEV, d)
    rs_mix, tok = _exchange_start([dwout_c, dwpd_c, dwpf_c], False, "rs_start_mixer")

    dqd, dl_d = _attn_bwd_dq("dil", qd, kd, vd, yd, dyd, lse_d, dil_bias, tq, "attn_dil_dq", dep=tok)
    dkd, dvd = _attn_bwd_dkv("dil", qd, kd, vd, dyd, lse_d_row, dl_d, dil_bias_t, None, tq, "attn_dil_dkv")
    dqf, dl_f = _attn_bwd_dq("fox", qf, kf, vf, yf, dyf, lse_f, c_row, tq, "attn_fox_dq")
    dkf, dvf, dc = _attn_bwd_dkv("fox", qf, kf, vf, dyf, lse_f_row, dl_f, c_rep, c_row, tq, "attn_fox_dkv")
    dc_pad = jnp.pad(dc[:, :, 0, :].reshape(nh, t).T, ((0, 0), (0, LANE - nh)))
    dlogf = _cumsum_rows(dc_pad, True, "revcumsum_dc")
    dproj, d_bforget = _assemble_dproj(dqd, dkd, dvd, dqf, dkf, dvf, dgd, dgf, dlogf, proj, tables, bf_pad, scale)

    dwin_p = _mm_tn(hm_t, dproj, BF, "w_in_dw", tn_pref=W_IN_COLS // 2, tk_pref=DW_ROWS, a_transposed=True)

    def perm_col(c):
        if c < 6 * hd:
            return c
        return c + 2 * d if c < 6 * hd + n_f else c - n_f

    shards = []
    for j in range(N_DEV):
        cuts = sorted({j * cols, (j + 1) * cols} | {c for c in (6 * hd, 6 * hd + n_f) if j * cols < c < (j + 1) * cols})
        shards.append(jnp.concatenate([dwin_p[:, perm_col(lo):perm_col(lo) + hi - lo]
                                       for lo, hi in zip(cuts[:-1], cuts[1:])], axis=1))
    dwin_c = jnp.stack(shards)
    rs_win, tok = _exchange_start([dwin_c], False, "rs_start_w_in")
    dx1, dx1b, d_mix_norm = _mm_nt(dproj, win_p, F32, "w_in_bwd", tn_pref=d, tk_pref=W_IN_COLS,
                                   rms=(x1, mix_norm, dx2), dep=tok)

    grad_x, _, d_ffn1_norm, rs_ffn1 = ffn_bwd(dx1b, hn1, g1, u1, a1, wg1, wu1, wd1, x2d, ffn1_norm, dx1, "ffn1")

    def update(handles, names, after, tag):
        recvs = _exchange_wait(handles, False, after, "rs_wait_" + tag)
        res = {}
        for recv, n in zip(recvs, names):
            turn = rows if n.endswith(("w_gate", "w_up")) else (lambda a: a)
            w, m, v = (turn(a)[0] for a in wmv[n])
            res[n] = tuple(turn(o[None]) for o in _adam_from_partials(recv, w, m, v, "adam_" + n))
        return res, res[names[-1]][0]

    wmv = {
        "ffn1_w_gate": (ffn1_w_gate, m_ffn1_w_gate, v_ffn1_w_gate),
        "ffn1_w_up": (ffn1_w_up, m_ffn1_w_up, v_ffn1_w_up),
        "ffn1_w_down": (ffn1_w_down, m_ffn1_w_down, v_ffn1_w_down),
        "w_in": (w_in, m_w_in, v_w_in),
        "w_proj_dil": (w_proj_dil, m_w_proj_dil, v_w_proj_dil),
        "w_proj_fox": (w_proj_fox, m_w_proj_fox, v_w_proj_fox),
        "w_out": (w_out, m_w_out, v_w_out),
        "ffn2_w_gate": (ffn2_w_gate, m_ffn2_w_gate, v_ffn2_w_gate),
        "ffn2_w_up": (ffn2_w_up, m_ffn2_w_up, v_ffn2_w_up),
        "ffn2_w_down": (ffn2_w_down, m_ffn2_w_down, v_ffn2_w_down),
    }
    big = {}
    after = grad_x
    for handles, names, tag in [
            (rs_ffn2, ["ffn2_w_gate", "ffn2_w_up", "ffn2_w_down"], "ffn2"),
            (rs_mix, ["w_out", "w_proj_dil", "w_proj_fox"], "mixer"),
            (rs_win, ["w_in"], "w_in"),
            (rs_ffn1, ["ffn1_w_gate", "ffn1_w_up", "ffn1_w_down"], "ffn1")]:
        res, after = update(handles, names, after, tag)
        big.update(res)

    def lanes(a):
        a = a.reshape(1, -1)
        return jnp.pad(a, ((0, 0), (0, d - a.shape[1])))

    small_names = ["ffn1_norm", "mix_norm", "b_gate_dil", "b_gate_fox", "ffn2_norm", "final_norm", "b_forget"]
    small_g = [d_ffn1_norm, d_mix_norm, d_bd, d_bf, d_ffn2_norm, d_final, d_bforget[:, :n_f]]
    small_w = [ffn1_norm, mix_norm, b_gate_dil, b_gate_fox, ffn2_norm, final_norm, b_forget]
    small_m = [m_ffn1_norm, m_mix_norm, m_b_gate_dil, m_b_gate_fox, m_ffn2_norm, m_final_norm, m_b_forget]
    small_v = [v_ffn1_norm, v_mix_norm, v_b_gate_dil, v_b_gate_fox, v_ffn2_norm, v_final_norm, v_b_forget]
    pack = lambda arrs, last: jnp.concatenate([lanes(a) for a in arrs] + [last], axis=0)
    g_all = _allreduce_small(pack(small_g, loss_lanes))
    zero_row = jnp.zeros((1, d), F32)
    one_row = jnp.ones((1, d), F32)
    s_delta, s_m, s_v = _adam_small(g_all, pack(small_w, zero_row), pack(small_m, zero_row), pack(small_v, one_row))
    loss = g_all[len(small_names), 0]

    def unpack(packed, i, like):
        return packed[i, :like.size].reshape(like.shape)

    small = {}
    for i, (n, w) in enumerate(zip(small_names, small_w)):
        small[n] = (unpack(g_all, i, w), unpack(s_delta, i, w), unpack(s_m, i, w), unpack(s_v, i, w))

    order = ["ffn1_norm", "ffn1_w_gate", "ffn1_w_up", "ffn1_w_down", "mix_norm", "w_in", "b_forget", "b_gate_dil",
             "b_gate_fox", "w_proj_dil", "w_proj_fox", "w_out", "ffn2_norm", "ffn2_w_gate", "ffn2_w_up",
             "ffn2_w_down", "final_norm"]
    res = {**big, **small}
    outs = [loss, grad_x[None]]
    for slot in range(4):
        outs += [res[n][slot] for n in order]
    return tuple(outs)
```

```python
import jax
import jax.numpy as jnp
from jax import lax
from jax.experimental import pallas as pl
from jax.experimental.pallas import tpu as pltpu

BF = jnp.bfloat16
F32 = jnp.float32
MESH = pl.DeviceIdType.MESH
N_DEV = 8

HEAD_DIM = 128
ROPE_DIM = HEAD_DIM // 4
ROPE_HALF = ROPE_DIM // 2
ROPE_THETA = 500000.0
NORM_EPS = 1e-6
DIL_PATTERNS = ((128, 1), (512, 4), (2048, 16))
MAX_WINDOW = 2048
LANE = 128
NEG = -1e30
F_PAD = 512
W_IN_COLS = 1536

ADAM_LR = 0.001
ADAM_B1 = 0.9
ADAM_B2 = 0.999
ADAM_EPS = 1e-08
ADAM_WD = 0.01
ADAM_STEP = 10

VMEM_LIMIT_BYTES = 56 * 1024 * 1024
FFN_ROWS = 1024
DW_ROWS = 2048
ANY = pl.BlockSpec(memory_space=pl.ANY)

NN = (((1,), (0,)), ((), ()))
NT = (((1,), (1,)), ((), ()))
TN = (((0,), (0,)), ((), ()))


def _dot(a, b, dn=NN):
    return lax.dot_general(a, b, dn, preferred_element_type=F32)


def _sig(x):
    return 0.5 + 0.5 * jnp.tanh(0.5 * x)


def _tile(n, pref, align):
    best = None
    t = align
    while t <= min(n, pref):
        if n % t == 0:
            best = t
        t += align
    return n if best is None else best


def _params():
    return pltpu.CompilerParams(vmem_limit_bytes=VMEM_LIMIT_BYTES)


def _call(body, args, dep=None, **kw):
    if dep is not None:
        n_in = len(args)
        inner = body

        def body(*refs):
            inner(*refs[:n_in], *refs[n_in + 1:])

        kw["in_specs"] = list(kw["in_specs"]) + [ANY]
        args = list(args) + [dep]
    return pl.pallas_call(body, **kw)(*args)


def _peers():
    x, y, c = lax.axis_index("x"), lax.axis_index("y"), lax.axis_index("c")
    me = 4 * x + 2 * y + c
    peers = []
    for k in range(1, N_DEV):
        px = 1 - x if (k >> 2) & 1 else x
        py = 1 - y if (k >> 1) & 1 else y
        pc = 1 - c if k & 1 else c
        peers.append((k, (px, py, pc), 4 * px + 2 * py + pc))
    return me, peers


HBM = pl.BlockSpec(memory_space=pltpu.HBM)
SEM = pl.BlockSpec(memory_space=pltpu.SEMAPHORE)
EFFECT = pltpu.SideEffectType.DATAFLOW_SIDE_EFFECTING


def _exchange_copy(gather, src_ref, land_ref, send_sems, recv_sems, me, k, peer, peer_flat, landing):
    return pltpu.make_async_remote_copy(
        src_ref=src_ref if gather else src_ref.at[peer_flat], dst_ref=land_ref.at[landing],
        send_sem=send_sems.at[k], recv_sem=recv_sems.at[k], device_id=peer, device_id_type=MESH)


ALL_PEERS = (1, 2, 3, 4, 5, 6, 7)
SIBLING = 1
SAME_CORE = (2, 4, 6)
FIRST_LEVEL = (SIBLING,) + SAME_CORE


def _exchange_start(srcs, gather, name, dep=None, ks=ALL_PEERS):
    n = len(srcs)
    extra = [] if dep is None else [dep]

    def body(*refs):
        src_refs, land_refs = refs[:n], refs[n:2 * n]
        refs = refs[2 * n + len(extra):]
        send_refs, recv_refs = refs[:n], refs[n:2 * n]
        token = refs[4 * n]
        me, peers = _peers()
        for i in range(n):
            for k, peer, peer_flat in peers:
                if k in ks:
                    _exchange_copy(gather, src_refs[i], land_refs[i], send_refs[i], recv_refs[i],
                                   me, k, peer, peer_flat, me).start()
        token[...] = jnp.zeros_like(token)

    lands = [lax.empty((N_DEV,) + s.shape[-2:], s.dtype) for s in srcs]
    sems = [pltpu.SemaphoreType.DMA((N_DEV,)) for _ in range(2 * n)]
    out = pl.pallas_call(
        body, name=name,
        out_shape=tuple(sems) + tuple(pltpu.HBM(a.shape, a.dtype) for a in list(srcs) + lands)
        + (jax.ShapeDtypeStruct((8, LANE), F32),),
        in_specs=[HBM] * (2 * n) + [ANY] * len(extra),
        out_specs=tuple([SEM] * (2 * n) + [HBM] * (2 * n) + [pl.BlockSpec(memory_space=pltpu.VMEM)]),
        input_output_aliases={i: 2 * n + i for i in range(2 * n)},
        compiler_params=pltpu.CompilerParams(has_side_effects=EFFECT),
    )(*[pltpu.with_memory_space_constraint(a, pltpu.HBM) for a in list(srcs) + lands], *extra)
    handles = [(out[2 * n + i], out[3 * n + i], out[i], out[n + i]) for i in range(n)]
    return handles, out[4 * n]


def _exchange_wait(handles, gather, after, name):
    n = len(handles)

    def body(*refs):
        src_refs, land_refs = refs[:n], refs[n:2 * n]
        send_refs, recv_refs = refs[2 * n:3 * n], refs[3 * n:4 * n]
        me, peers = _peers()
        for i in range(n):
            for k, peer, peer_flat in peers:
                cp = _exchange_copy(gather, src_refs[i], land_refs[i], send_refs[i], recv_refs[i],
                                    me, k, peer, peer_flat, peer_flat)
                cp.wait_send()
                cp.wait_recv()

    srcs = [h[0] for h in handles]
    lands = [h[1] for h in handles]
    out = pl.pallas_call(
        body, name=name,
        out_shape=tuple(pltpu.HBM(a.shape, a.dtype) for a in srcs + lands),
        in_specs=[HBM] * (2 * n) + [SEM] * (2 * n) + [ANY],
        out_specs=tuple([HBM] * (2 * n)),
        input_output_aliases={i: i for i in range(2 * n)},
        compiler_params=pltpu.CompilerParams(has_side_effects=EFFECT),
    )(*srcs, *lands, *[h[2] for h in handles], *[h[3] for h in handles], after)
    me = 4 * lax.axis_index("x") + 2 * lax.axis_index("y") + lax.axis_index("c")
    filled = []
    for src, land in zip(out[:n], out[n:]):
        own = src[None] if gather else lax.dynamic_slice_in_dim(src, me, 1, axis=0)
        filled.append(lax.dynamic_update_slice_in_dim(land, own, me, axis=0))
    return filled


def _gather_relay(handles, after, name):
    n = len(handles)

    def body(*refs):
        land_refs, recv_refs = refs[:n], refs[n:2 * n]
        refs = refs[2 * n + 1:]
        send2_refs, recv2_refs = refs[n:2 * n], refs[2 * n:3 * n]
        me, peers = _peers()
        sibling = peers[SIBLING - 1][1]
        for i in range(n):
            for k, peer, peer_flat in peers:
                if k in SAME_CORE:
                    block = land_refs[i].at[peer_flat]
                    pltpu.make_async_remote_copy(
                        src_ref=block, dst_ref=block, send_sem=send2_refs[i].at[k], recv_sem=recv_refs[i].at[k],
                        device_id=peer, device_id_type=MESH).wait_recv()
                    pltpu.make_async_remote_copy(
                        src_ref=block, dst_ref=block, send_sem=send2_refs[i].at[k], recv_sem=recv2_refs[i].at[k],
                        device_id=sibling, device_id_type=MESH).start()

    lands = [h[1] for h in handles]
    sems = [pltpu.SemaphoreType.DMA((N_DEV,)) for _ in range(2 * n)]
    out = pl.pallas_call(
        body, name=name,
        out_shape=tuple(pltpu.HBM(a.shape, a.dtype) for a in lands) + tuple(sems),
        in_specs=[HBM] * n + [SEM] * n + [ANY],
        out_specs=tuple([HBM] * n + [SEM] * (2 * n)),
        input_output_aliases={i: i for i in range(n)},
        compiler_params=pltpu.CompilerParams(has_side_effects=EFFECT),
    )(*lands, *[h[3] for h in handles], after)
    return [(h[0], out[i], h[2], h[3], out[n + i], out[2 * n + i]) for i, h in enumerate(handles)]


def _gather_wait(handles, after, name):
    n = len(handles)

    def body(*refs):
        src_refs, land_refs = refs[:n], refs[n:2 * n]
        send_refs, recv_refs = refs[2 * n:3 * n], refs[3 * n:4 * n]
        send2_refs, recv2_refs = refs[4 * n:5 * n], refs[5 * n:6 * n]
        me, peers = _peers()
        _, sibling, sibling_flat = peers[SIBLING - 1]
        for i in range(n):
            for k, peer, peer_flat in peers:
                if k in FIRST_LEVEL:
                    cp = _exchange_copy(True, src_refs[i], land_refs[i], send_refs[i], recv_refs[i],
                                        me, k, peer, peer_flat, peer_flat)
                    cp.wait_send()
                    if k == SIBLING:
                        cp.wait_recv()
                if k in SAME_CORE:
                    mine = land_refs[i].at[peer_flat]
                    theirs = land_refs[i].at[peer_flat ^ SIBLING]
                    cp = pltpu.make_async_remote_copy(
                        src_ref=mine, dst_ref=theirs, send_sem=send2_refs[i].at[k], recv_sem=recv2_refs[i].at[k],
                        device_id=sibling, device_id_type=MESH)
                    cp.wait_send()
                    cp.wait_recv()

    srcs = [h[0] for h in handles]
    lands = [h[1] for h in handles]
    out = pl.pallas_call(
        body, name=name,
        out_shape=tuple(pltpu.HBM(a.shape, a.dtype) for a in srcs + lands),
        in_specs=[HBM] * (2 * n) + [SEM] * (4 * n) + [ANY],
        out_specs=tuple([HBM] * (2 * n)),
        input_output_aliases={i: i for i in range(2 * n)},
        compiler_params=pltpu.CompilerParams(has_side_effects=EFFECT),
    )(*srcs, *lands, *[h[2] for h in handles], *[h[3] for h in handles],
      *[h[4] for h in handles], *[h[5] for h in handles], after)
    me = 4 * lax.axis_index("x") + 2 * lax.axis_index("y") + lax.axis_index("c")
    return [lax.dynamic_update_slice_in_dim(land, src[None], me, axis=0) for src, land in zip(out[:n], out[n:])]


def _allreduce_small(p):
    rows, d = p.shape

    def body(p_ref, o_ref, recv_ref, send_sems, recv_sems):
        me, peers = _peers()
        recv_ref[me] = p_ref[...]
        sends = []
        for k, peer, peer_flat in peers:
            cp = pltpu.make_async_remote_copy(
                src_ref=p_ref, dst_ref=recv_ref.at[me],
                send_sem=send_sems.at[k], recv_sem=recv_sems.at[k],
                device_id=peer, device_id_type=MESH)
            cp.start()
            sends.append(cp)
        for k, peer, peer_flat in peers:
            pltpu.make_async_remote_copy(
                src_ref=p_ref, dst_ref=recv_ref.at[peer_flat],
                send_sem=send_sems.at[k], recv_sem=recv_sems.at[k],
                device_id=peer, device_id_type=MESH).wait_recv()
        for cp in sends:
            cp.wait_send()
        acc = recv_ref[0]
        for s in range(1, N_DEV):
            acc = acc + recv_ref[s]
        is_loss = lax.broadcasted_iota(jnp.int32, (rows, d), 0) == rows - 1
        total = jnp.sum(jnp.where(is_loss, acc, 0.0))
        o_ref[...] = jnp.where(is_loss, total, acc)

    return pl.pallas_call(
        body, name="allreduce_small",
        out_shape=jax.ShapeDtypeStruct((rows, d), F32),
        in_specs=[pl.BlockSpec(memory_space=pltpu.VMEM)],
        out_specs=pl.BlockSpec(memory_space=pltpu.VMEM),
        scratch_shapes=[pltpu.VMEM((N_DEV, rows, d), F32),
                        pltpu.SemaphoreType.DMA((N_DEV,)), pltpu.SemaphoreType.DMA((N_DEV,))],
    )(p)


def _adam_math(w, g, m, v):
    m2 = ADAM_B1 * m + (1.0 - ADAM_B1) * g
    v2 = ADAM_B2 * v + (1.0 - ADAM_B2) * (g * g)
    m_hat = m2 / (1.0 - ADAM_B1 ** ADAM_STEP)
    v_hat = v2 / (1.0 - ADAM_B2 ** ADAM_STEP)
    delta = -ADAM_LR * (m_hat / (jnp.sqrt(v_hat) + ADAM_EPS) + ADAM_WD * w)
    return delta, m2, v2


def _adam_from_partials(parts, w, m, v, name):
    r, c = w.shape
    tr = _tile(r, 256, 16)

    def body(p_ref, w_ref, m_ref, v_ref, g_out, d_out, m_out, v_out):
        g = p_ref[0].astype(F32)
        for s in range(1, N_DEV):
            g = g + p_ref[s].astype(F32)
        delta, m2, v2 = _adam_math(w_ref[...], g, m_ref[...], v_ref[...])
        g_out[...] = g
        d_out[...] = delta
        m_out[...] = m2
        v_out[...] = v2

    blk = pl.BlockSpec((tr, c), lambda i: (i, 0))
    out = jax.ShapeDtypeStruct((r, c), F32)
    return pl.pallas_call(
        body, name=name, grid=(r // tr,),
        in_specs=[pl.BlockSpec((N_DEV, tr, c), lambda i: (0, i, 0)), blk, blk, blk],
        out_specs=[blk, blk, blk, blk], out_shape=[out, out, out, out],
        compiler_params=_params(),
    )(parts, w, m, v)


def _adam_small(g, w, m, v):
    def body(g_ref, w_ref, m_ref, v_ref, d_out, m_out, v_out):
        delta, m2, v2 = _adam_math(w_ref[...], g_ref[...], m_ref[...], v_ref[...])
        d_out[...] = delta
        m_out[...] = m2
        v_out[...] = v2

    out = jax.ShapeDtypeStruct(g.shape, F32)
    return pl.pallas_call(body, name="adam_small", out_shape=[out, out, out])(g, w, m, v)


def _rms_fwd(x, gain, name, dep=None, with_transpose=False):
    t, d = x.shape
    tr = _tile(t, 256, LANE)

    def body(x_ref, g_ref, o_ref, *ot_ref):
        xv = x_ref[...]
        r = lax.rsqrt(jnp.mean(xv * xv, axis=-1, keepdims=True) + NORM_EPS)
        y = xv * r * g_ref[...]
        o_ref[...] = y.astype(BF)
        if with_transpose:
            ot_ref[0][...] = jnp.transpose(y).astype(BF)

    out_specs = [pl.BlockSpec((tr, d), lambda i: (i, 0))]
    out_shape = [jax.ShapeDtypeStruct((t, d), BF)]
    if with_transpose:
        out_specs.append(pl.BlockSpec((d, tr), lambda i: (0, i)))
        out_shape.append(jax.ShapeDtypeStruct((d, t), BF))
    return _call(
        body, [x, gain], dep=dep, name=name, grid=(t // tr,),
        in_specs=[pl.BlockSpec((tr, d), lambda i: (i, 0)), pl.BlockSpec((1, d), lambda i: (0, 0))],
        out_specs=out_specs, out_shape=out_shape, compiler_params=_params(),
    )


def _rms_vjp(xv, gain, dy):
    r = lax.rsqrt(jnp.mean(xv * xv, axis=-1, keepdims=True) + NORM_EPS)
    xhat = xv * r
    dxhat = dy * gain
    dx = r * (dxhat - xhat * jnp.mean(dxhat * xhat, axis=-1, keepdims=True))
    dgain = jnp.sum(dy * xhat, axis=0, keepdims=True)
    return dx, dgain


def _loss_head(x, gain, target):
    t, d = x.shape
    tr = _tile(t, 256, 16)

    def body(x_ref, g_ref, t_ref, dx_ref, dxb_ref, dg_ref, loss_ref):
        xv = x_ref[...]
        gain = g_ref[...]
        r = lax.rsqrt(jnp.mean(xv * xv, axis=-1, keepdims=True) + NORM_EPS)
        err = xv * r * gain - t_ref[...]
        dx, dgain = _rms_vjp(xv, gain, err * (1.0 / d))
        dx_ref[...] = dx
        dxb_ref[...] = dx.astype(BF)

        @pl.when(pl.program_id(0) == 0)
        def _():
            dg_ref[...] = jnp.zeros_like(dg_ref)
            loss_ref[...] = jnp.zeros_like(loss_ref)

        dg_ref[...] += dgain
        loss_ref[...] += jnp.sum(err * err, axis=0, keepdims=True) * (0.5 / d)

    row = pl.BlockSpec((tr, d), lambda i: (i, 0))
    vec = pl.BlockSpec((1, d), lambda i: (0, 0))
    return pl.pallas_call(
        body, name="loss_head", grid=(t // tr,),
        in_specs=[row, vec, row], out_specs=[row, row, vec, vec],
        out_shape=[jax.ShapeDtypeStruct((t, d), F32), jax.ShapeDtypeStruct((t, d), BF),
                   jax.ShapeDtypeStruct((1, d), F32), jax.ShapeDtypeStruct((1, d), F32)],
        compiler_params=_params(),
    )(x, gain, target)


def _mm_nn(a, b, out_dtype, name, residual=None, tm_pref=512, tn_pref=1152):
    m, k = a.shape
    n = b.shape[1]
    tm, tn = _tile(m, tm_pref, 16), _tile(n, tn_pref, LANE)

    def body(*refs):
        if residual is None:
            a_ref, b_ref, o_ref = refs
            o_ref[...] = _dot(a_ref[...], b_ref[...]).astype(out_dtype)
        else:
            a_ref, b_ref, r_ref, o_ref = refs
            o_ref[...] = (r_ref[...] + _dot(a_ref[...], b_ref[...])).astype(out_dtype)

    in_specs = [pl.BlockSpec((tm, k), lambda j, i: (i, 0)), pl.BlockSpec((k, tn), lambda j, i: (0, j))]
    args = [a, b]
    if residual is not None:
        in_specs.append(pl.BlockSpec((tm, tn), lambda j, i: (i, j)))
        args.append(residual)
    return pl.pallas_call(
        body, name=name, grid=(n // tn, m // tm), in_specs=in_specs,
        out_specs=pl.BlockSpec((tm, tn), lambda j, i: (i, j)),
        out_shape=jax.ShapeDtypeStruct((m, n), out_dtype), compiler_params=_params(),
    )(*args)


def _rms_bwd_tail(dy_ref, first, x_ref, g_ref, dres_ref, dx_ref, dxb_ref, dg_ref):
    @pl.when(first)
    def _():
        dg_ref[...] = jnp.zeros_like(dg_ref)

    gain = g_ref[...]
    for r in range(0, dy_ref.shape[0], LANE):
        rows = pl.ds(r, min(LANE, dy_ref.shape[0] - r))
        dx, dgain = _rms_vjp(x_ref[rows, :], gain, dy_ref[rows, :])
        dx = dx + dres_ref[rows, :]
        dx_ref[rows, :] = dx
        dxb_ref[rows, :] = dx.astype(BF)
        dg_ref[...] += dgain


def _mm_nt(a, b, out_dtype, name, tm_pref=512, tn_pref=1024, tk_pref=2048, rms=None, dep=None):
    m, k = a.shape
    n = b.shape[0]
    tm, tn, tk = _tile(m, tm_pref, 16), _tile(n, tn_pref, LANE), _tile(k, tk_pref, LANE)
    nk = k // tk
    assert rms is None or tn == n

    def body(*refs):
        if rms is None:
            a_ref, b_ref, o_ref, acc_ref = refs
        else:
            a_ref, b_ref, x_ref, g_ref, dres_ref, dx_ref, dxb_ref, dg_ref, acc_ref = refs
        kk = pl.program_id(2)

        @pl.when(kk == 0)
        def _():
            acc_ref[...] = jnp.zeros_like(acc_ref)

        acc_ref[...] += _dot(a_ref[...], b_ref[...], NT)

        @pl.when(kk == nk - 1)
        def _():
            if rms is None:
                o_ref[...] = acc_ref[...].astype(out_dtype)
            else:
                _rms_bwd_tail(acc_ref, pl.program_id(1) == 0, x_ref, g_ref, dres_ref, dx_ref, dxb_ref, dg_ref)

    in_specs = [pl.BlockSpec((tm, tk), lambda j, i, kk: (i, kk)), pl.BlockSpec((tn, tk), lambda j, i, kk: (j, kk))]
    row = pl.BlockSpec((tm, tn), lambda j, i, kk: (i, j))
    if rms is None:
        args, out_specs, out_shape = [a, b], row, jax.ShapeDtypeStruct((m, n), out_dtype)
    else:
        vec = pl.BlockSpec((1, n), lambda j, i, kk: (0, 0))
        row_once = pl.BlockSpec((tm, tn), lambda j, i, kk: (i, j), pipeline_mode=pl.Buffered(1))
        args, in_specs = [a, b, *rms], in_specs + [row_once, vec, row_once]
        out_specs = [row, row, vec]
        out_shape = [jax.ShapeDtypeStruct((m, n), F32), jax.ShapeDtypeStruct((m, n), BF),
                     jax.ShapeDtypeStruct((1, n), F32)]
    return _call(
        body, args, dep=dep, name=name, grid=(n // tn, m // tm, nk), in_specs=in_specs, out_specs=out_specs,
        out_shape=out_shape, scratch_shapes=[pltpu.VMEM((tm, tn), F32)], compiler_params=_params(),
    )


def _mm_tn(a, b, out_dtype, name, tn_pref=1152, tk_pref=512, a_transposed=False):
    (k, t) = a.shape if a_transposed else a.shape[::-1]
    n = b.shape[1]
    tn, tk = _tile(n, tn_pref, LANE), _tile(t, tk_pref, LANE if a_transposed else 16)
    nt = t // tk

    def body(a_ref, b_ref, o_ref, acc_ref):
        tt = pl.program_id(1)

        @pl.when(tt == 0)
        def _():
            acc_ref[...] = jnp.zeros_like(acc_ref)

        acc_ref[...] += _dot(a_ref[...], b_ref[...], NN if a_transposed else TN)

        @pl.when(tt == nt - 1)
        def _():
            o_ref[...] = acc_ref[...].astype(out_dtype)

    if a_transposed:
        a_spec = pl.BlockSpec((k, tk), lambda j, tt: (0, tt))
    else:
        a_spec = pl.BlockSpec((tk, k), lambda j, tt: (tt, 0))
    return pl.pallas_call(
        body, name=name, grid=(n // tn, nt),
        in_specs=[a_spec, pl.BlockSpec((tk, tn), lambda j, tt: (tt, j))],
        out_specs=pl.BlockSpec((k, tn), lambda j, tt: (0, j)),
        out_shape=jax.ShapeDtypeStruct((k, n), out_dtype),
        scratch_shapes=[pltpu.VMEM((k, tn), F32)], compiler_params=_params(),
    )(a, b)


FFN_COLS = 512


FFN_ROWS_WIDE = 2048


def _ffn_tiles(t, fc, rows=FFN_ROWS):
    return _tile(t, rows, 16), _tile(fc, FFN_COLS, LANE)


def _slabs(tm, rows=256):
    step = rows if tm % rows == 0 else tm
    return [pl.ds(r, step) for r in range(0, tm, step)]


def _ffn_gate_up(hn, wg_t, wu_t, name):
    t, d = hn.shape
    fc = wg_t.shape[0]
    tm, tn = _ffn_tiles(t, fc, FFN_ROWS_WIDE)

    def body(h_ref, wg_ref, wu_ref, g_ref, u_ref, a_ref):
        for rows in _slabs(tm):
            h = h_ref[rows, :]
            g = _dot(h, wg_ref[...], NT)
            u = _dot(h, wu_ref[...], NT)
            g_ref[rows, :] = g.astype(BF)
            u_ref[rows, :] = u.astype(BF)
            a_ref[rows, :] = (g * _sig(g) * u).astype(BF)

    wspec = pl.BlockSpec((tn, d), lambda j, i: (j, 0))
    hid = pl.BlockSpec((tm, tn), lambda j, i: (i, j))
    out = jax.ShapeDtypeStruct((t, fc), BF)
    return pl.pallas_call(
        body, name=name, grid=(fc // tn, t // tm),
        in_specs=[pl.BlockSpec((tm, d), lambda j, i: (i, 0)), wspec, wspec],
        out_specs=[hid, hid, hid], out_shape=[out, out, out], compiler_params=_params(),
    )(hn, wg_t, wu_t)


def _ffn_gate(hn, wg_t, name):
    t, d = hn.shape
    fc = wg_t.shape[0]
    tm, tn = _ffn_tiles(t, fc)

    def body(h_ref, wg_ref, g_ref):
        g_ref[...] = _dot(h_ref[...], wg_ref[...], NT)

    return pl.pallas_call(
        body, name=name, grid=(fc // tn, t // tm),
        in_specs=[pl.BlockSpec((tm, d), lambda j, i: (i, 0)), pl.BlockSpec((tn, d), lambda j, i: (j, 0))],
        out_specs=pl.BlockSpec((tm, tn), lambda j, i: (i, j)),
        out_shape=jax.ShapeDtypeStruct((t, fc), F32), compiler_params=_params(),
    )(hn, wg_t)


def _ffn_up_act(hn, wu_t, g, name):
    t, d = hn.shape
    fc = wu_t.shape[0]
    tm, tn = _ffn_tiles(t, fc, FFN_ROWS_WIDE)

    def body(h_ref, wu_ref, g_ref, gb_ref, u_ref, a_ref):
        for rows in _slabs(tm):
            u = _dot(h_ref[rows, :], wu_ref[...], NT)
            gv = g_ref[rows, :]
            gb_ref[rows, :] = gv.astype(BF)
            u_ref[rows, :] = u.astype(BF)
            a_ref[rows, :] = (gv * _sig(gv) * u).astype(BF)

    hid = pl.BlockSpec((tm, tn), lambda j, i: (i, j))
    out = jax.ShapeDtypeStruct((t, fc), BF)
    return pl.pallas_call(
        body, name=name, grid=(fc // tn, t // tm),
        in_specs=[pl.BlockSpec((tm, d), lambda j, i: (i, 0)), pl.BlockSpec((tn, d), lambda j, i: (j, 0)), hid],
        out_specs=[hid, hid, hid], out_shape=[out, out, out], compiler_params=_params(),
    )(hn, wu_t, g)


def _ffn_down(act, wd, xres, name):
    t, fc = act.shape
    d = wd.shape[1]
    tm, tk = _ffn_tiles(t, fc)

    def body(a_ref, w_ref, x_ref, o_ref):
        @pl.when(pl.program_id(1) == 0)
        def _():
            o_ref[...] = x_ref[...]

        o_ref[...] += 0.5 * _dot(a_ref[...], w_ref[...])

    row = pl.BlockSpec((tm, d), lambda i, k: (i, 0))
    return pl.pallas_call(
        body, name=name, grid=(t // tm, fc // tk),
        in_specs=[pl.BlockSpec((tm, tk), lambda i, k: (i, k)), pl.BlockSpec((tk, d), lambda i, k: (k, 0)), row],
        out_specs=row, out_shape=jax.ShapeDtypeStruct((t, d), F32), compiler_params=_params(),
    )(act, wd, xres)


def _ffn_bwd_hidden(dxb, wd, g, u, name):
    t, d = dxb.shape
    fc = wd.shape[0]
    tm, tn = _ffn_tiles(t, fc, FFN_ROWS_WIDE)

    def body(dx_ref, w_ref, g_ref, u_ref, dg_ref, du_ref):
        for rows in _slabs(tm):
            dh = 0.5 * _dot(dx_ref[rows, :], w_ref[...], NT)
            gv = g_ref[rows, :].astype(F32)
            uv = u_ref[rows, :].astype(F32)
            s = _sig(gv)
            dg_ref[rows, :] = (dh * uv * (s * (1.0 + gv * (1.0 - s)))).astype(BF)
            du_ref[rows, :] = (dh * (gv * s)).astype(BF)

    hid = pl.BlockSpec((tm, tn), lambda i, j: (i, j))
    out = jax.ShapeDtypeStruct((t, fc), BF)
    return pl.pallas_call(
        body, name=name, grid=(t // tm, fc // tn),
        in_specs=[pl.BlockSpec((tm, d), lambda i, j: (i, 0)), pl.BlockSpec((tn, d), lambda i, j: (j, 0)), hid, hid],
        out_specs=[hid, hid], out_shape=[out, out], compiler_params=_params(),
    )(dxb, wd, g, u)


def _ffn_dw(lhs, rhs, scale, name, dep=None):
    n = len(lhs)
    t, fc = lhs[0].shape
    d = rhs.shape[1]
    tk, tn = _tile(t, DW_ROWS, 16), _tile(fc, FFN_COLS, LANE)
    nt = t // tk

    def body(*refs):
        l_refs, r_ref, o_refs, acc_refs = refs[:n], refs[n], refs[n + 1:2 * n + 1], refs[2 * n + 1:]
        tt = pl.program_id(1)
        r = r_ref[...]
        for l_ref, o_ref, acc_ref in zip(l_refs, o_refs, acc_refs):
            @pl.when(tt == 0)
            def _():
                acc_ref[...] = jnp.zeros_like(acc_ref)

            acc_ref[...] += _dot(l_ref[...], r, TN)

            @pl.when(tt == nt - 1)
            def _():
                o_ref[...] = (scale * acc_ref[...]).astype(BF)

    lspec = pl.BlockSpec((tk, tn), lambda j, tt: (tt, j))
    ospec = pl.BlockSpec((tn, d), lambda j, tt: (j, 0))
    out = jax.ShapeDtypeStruct((fc, d), BF)
    return _call(
        body, [*lhs, rhs], dep=dep, name=name, grid=(fc // tn, nt),
        in_specs=[lspec] * n + [pl.BlockSpec((tk, d), lambda j, tt: (tt, 0))],
        out_specs=[ospec] * n, out_shape=[out] * n,
        scratch_shapes=[pltpu.VMEM((tn, d), F32)] * n, compiler_params=_params(),
    )


def _rms_bwd(dy, x, gain, dres, name):
    t, d = x.shape
    tr = _tile(t, 256, 16)

    def body(dy_ref, x_ref, g_ref, dres_ref, dx_ref, dxb_ref, dg_ref):
        _rms_bwd_tail(dy_ref, pl.program_id(0) == 0, x_ref, g_ref, dres_ref, dx_ref, dxb_ref, dg_ref)

    row = pl.BlockSpec((tr, d), lambda i: (i, 0))
    vec = pl.BlockSpec((1, d), lambda i: (0, 0))
    return pl.pallas_call(
        body, name=name, grid=(t // tr,),
        in_specs=[row, row, vec, row], out_specs=[row, row, vec],
        out_shape=[jax.ShapeDtypeStruct((t, d), F32), jax.ShapeDtypeStruct((t, d), BF),
                   jax.ShapeDtypeStruct((1, d), F32)],
        compiler_params=_params(),
    )(dy, x, gain, dres)


def _ffn_bwd_input(dg, du, wg_t, wu_t, name, dep=None):
    t, fc = dg.shape
    d = wg_t.shape[1]
    tm, tk = _ffn_tiles(t, fc)

    def body(dg_ref, du_ref, wg_ref, wu_ref, o_ref):
        @pl.when(pl.program_id(1) == 0)
        def _():
            o_ref[...] = jnp.zeros_like(o_ref)

        o_ref[...] += _dot(dg_ref[...], wg_ref[...]) + _dot(du_ref[...], wu_ref[...])

    hid = pl.BlockSpec((tm, tk), lambda i, k: (i, k))
    wspec = pl.BlockSpec((tk, d), lambda i, k: (k, 0))
    return _call(
        body, [dg, du, wg_t, wu_t], dep=dep, name=name, grid=(t // tm, fc // tk),
        in_specs=[hid, hid, wspec, wspec],
        out_specs=pl.BlockSpec((tm, d), lambda i, k: (i, 0)),
        out_shape=jax.ShapeDtypeStruct((t, d), F32), compiler_params=_params(),
    )


def _rope_tables(t):
    pos = jnp.arange(t, dtype=F32)
    inv_freq = ROPE_THETA ** (-jnp.arange(0, ROPE_DIM, 2, dtype=F32) / ROPE_DIM)
    ang = pos[:, None] * inv_freq[None, :]
    cos, sin = jnp.cos(ang), jnp.sin(ang)
    rest = HEAD_DIM - ROPE_DIM
    one = jnp.ones((t, rest), F32)
    zero_h = jnp.zeros((t, ROPE_HALF), F32)
    zero_r = jnp.zeros((t, rest), F32)
    c = jnp.concatenate([cos, cos, one], axis=1)
    s1 = jnp.concatenate([-sin, zero_h, zero_r], axis=1)
    s2 = jnp.concatenate([zero_h, sin, zero_r], axis=1)
    return c, s1, s2


def _rope(xh, c, s1, s2):
    return xh * c + pltpu.roll(xh, HEAD_DIM - ROPE_HALF, 1) * s1 + pltpu.roll(xh, ROPE_HALF, 1) * s2


def _rope_t(dh, c, s1, s2):
    return dh * c + pltpu.roll(dh * s1, ROPE_HALF, 1) + pltpu.roll(dh * s2, HEAD_DIM - ROPE_HALF, 1)


def _mixer_prep(proj, tables, bf_pad, hd, scale):
    t, np_ = proj.shape
    tr = _tile(t, 256, 16)
    nh = hd // HEAD_DIM
    nblk = hd // LANE
    f_blk = (np_ - F_PAD) // LANE

    def body(qd_ref, kd_ref, vd_ref, qf_ref, kf_ref, vf_ref, fl_ref, c_ref, s1_ref, s2_ref, b_ref,
             oqd, okd, ovd, oqf, okf, ovf, olog):
        c, s1, s2 = c_ref[...], s1_ref[...], s2_ref[...]
        for h in range(nh):
            sl = slice(h * HEAD_DIM, (h + 1) * HEAD_DIM)
            oqd[:, sl] = (_rope(qd_ref[:, sl], c, s1, s2) * scale).astype(BF)
            okd[:, sl] = _rope(kd_ref[:, sl], c, s1, s2).astype(BF)
        ovd[...] = vd_ref[...].astype(BF)
        oqf[...] = (qf_ref[...] * scale).astype(BF)
        okf[...] = kf_ref[...].astype(BF)
        ovf[...] = vf_ref[...].astype(BF)
        z = fl_ref[...] + b_ref[...]
        olog[...] = jnp.minimum(z, 0.0) - jnp.log(1.0 + jnp.exp(-jnp.abs(z)))

    def col(kblk):
        return pl.BlockSpec((tr, hd), lambda i, kblk=kblk: (i, kblk))

    lane_row = pl.BlockSpec((tr, LANE), lambda i: (i, 0))
    in_specs = [col(0), col(1), col(2), col(3), col(4), col(5),
                pl.BlockSpec((tr, LANE), lambda i: (i, f_blk)),
                lane_row, lane_row, lane_row, pl.BlockSpec((1, LANE), lambda i: (0, 0))]
    o = pl.BlockSpec((tr, hd), lambda i: (i, 0))
    ob = jax.ShapeDtypeStruct((t, hd), BF)
    del nblk
    return pl.pallas_call(
        body, name="mixer_prep", grid=(t // tr,), in_specs=in_specs,
        out_specs=[o, o, o, o, o, o, lane_row],
        out_shape=[ob, ob, ob, ob, ob, ob, jax.ShapeDtypeStruct((t, LANE), F32)],
        compiler_params=_params(),
    )(proj, proj, proj, proj, proj, proj, proj, *tables, bf_pad)


def _split3(x):
    x1 = x.astype(BF)
    r1 = x - x1.astype(F32)
    x2 = r1.astype(BF)
    x3 = (r1 - x2.astype(F32)).astype(BF)
    return x1, x2, x3


def _cumsum_rows(x, reverse, name):
    t, w = x.shape
    blk = LANE
    nb = t // blk

    def body(x_ref, o_ref):
        r = lax.broadcasted_iota(jnp.int32, (blk, blk), 0)
        c = lax.broadcasted_iota(jnp.int32, (blk, blk), 1)
        tri = jnp.where((c >= r) if reverse else (c <= r), 1.0, 0.0).astype(BF)

        def step(i, carry):
            b = (nb - 1 - i) if reverse else i
            off = pl.multiple_of(b * blk, blk)
            xb = x_ref[pl.ds(off, blk), :]
            x1, x2, x3 = _split3(xb)
            o_ref[pl.ds(off, blk), :] = _dot(tri, x1) + _dot(tri, x2) + _dot(tri, x3) + carry
            return carry + jnp.sum(xb, axis=0, keepdims=True)

        lax.fori_loop(0, nb, step, jnp.zeros((1, w), F32))

    return pl.pallas_call(body, name=name, out_shape=jax.ShapeDtypeStruct((t, w), F32),
                          compiler_params=_params())(x)


ATTN_ROWS = 16


def _dil_bias_tiles(tq):
    nbias = MAX_WINDOW // tq + 1
    b = lax.broadcasted_iota(jnp.int32, (nbias, tq, tq), 0)
    i = lax.broadcasted_iota(jnp.int32, (nbias, tq, tq), 1)
    j = lax.broadcasted_iota(jnp.int32, (nbias, tq, tq), 2)
    delta = b * tq + i - j
    mult = jnp.zeros((nbias, tq, tq), F32)
    for w, dil in DIL_PATTERNS:
        mult = mult + jnp.where((delta >= 0) & (delta <= w) & (delta % dil == 0), 1.0, 0.0)
    return jnp.where(mult > 0.0, jnp.log(jnp.maximum(mult, 1.0)), NEG)


def _rep(x, width):
    return jnp.tile(x, (1, width // LANE))


def _chunks(n_rows, fn):
    for c in range(n_rows // ATTN_ROWS):
        fn(c * ATTN_ROWS)


def _causal(r0, tq, transposed):
    a = lax.broadcasted_iota(jnp.int32, (ATTN_ROWS, tq), 0) + r0
    b = lax.broadcasted_iota(jnp.int32, (ATTN_ROWS, tq), 1)
    return (a <= b) if transposed else (b <= a)


def _rows8(x):
    return jnp.transpose(x)[:8, :]


def _attn_fwd(mode, q, k, v, bias, tq, name):
    t, hd = q.shape
    nh = hd // HEAD_DIM
    nb = t // tq
    wb = MAX_WINDOW // tq
    fox = mode == "fox"

    def body(q_ref, k_ref, v_ref, b_ref, o_ref, lse_ref, lse_row_ref, s_ref, p_ref, m_ref, l_ref, acc_ref):
        qi = pl.program_id(1)
        qb = q_ref[...]
        m_ref[...] = jnp.full_like(m_ref, NEG)
        l_ref[...] = jnp.zeros_like(l_ref)
        acc_ref[...] = jnp.zeros_like(acc_ref)

        def tile(kj, diag):
            off = pl.multiple_of(kj * tq, tq)
            s_ref[...] = _dot(qb, k_ref[pl.ds(off, tq), :], NT)
            if fox:
                brow = b_ref[qi][:, :1] - b_ref[kj]

            def chunk(r0):
                rows = pl.ds(r0, ATTN_ROWS)
                if fox:
                    s = s_ref[rows, :] + brow
                    if diag:
                        s = jnp.where(_causal(r0, tq, False), s, NEG)
                else:
                    s = s_ref[rows, :] + b_ref[qi - kj, rows, :]
                m_old = m_ref[rows, :]
                m_new = jnp.maximum(m_old, jnp.max(s, axis=1, keepdims=True))
                p = jnp.exp(s - _rep(m_new, tq))
                alpha = jnp.exp(m_old - m_new)
                l_ref[rows, :] = alpha * l_ref[rows, :] + jnp.sum(p, axis=1, keepdims=True)
                m_ref[rows, :] = m_new
                acc_ref[rows, :] = alpha * acc_ref[rows, :]
                p_ref[rows, :] = p.astype(BF)

            _chunks(tq, chunk)
            acc_ref[...] += _dot(p_ref[...], v_ref[pl.ds(off, tq), :])

        tile(qi, True)
        if fox:
            lax.fori_loop(0, qi, lambda kj, c: (tile(kj, False), c)[1], 0)
        else:
            lax.fori_loop(1, jnp.minimum(qi, wb) + 1, lambda i, c: (tile(qi - i, False), c)[1], 0)
        o_ref[...] = (acc_ref[...] / l_ref[...]).astype(BF)
        lse = m_ref[...] + jnp.log(l_ref[...])
        lse_ref[...] = lse
        lse_row_ref[...] = _rows8(lse)

    qspec = pl.BlockSpec((tq, HEAD_DIM), lambda h, i: (i, h))
    kvspec = pl.BlockSpec((t, HEAD_DIM), lambda h, i: (0, h))
    repspec = pl.BlockSpec((None, tq, LANE), lambda h, i: (h, i, 0))
    row8spec = pl.BlockSpec((None, None, 8, tq), lambda h, i: (h, i, 0, 0))
    if fox:
        bspec = pl.BlockSpec((None, nb, 1, tq), lambda h, i: (h, 0, 0, 0))
    else:
        bspec = pl.BlockSpec((wb + 1, tq, tq), lambda h, i: (0, 0, 0))
    return pl.pallas_call(
        body, name=name, grid=(nh, nb), in_specs=[qspec, kvspec, kvspec, bspec],
        out_specs=[qspec, repspec, row8spec],
        out_shape=[jax.ShapeDtypeStruct((t, hd), BF), jax.ShapeDtypeStruct((nh, t, LANE), F32),
                   jax.ShapeDtypeStruct((nh, nb, 8, tq), F32)],
        scratch_shapes=[pltpu.VMEM((tq, tq), F32), pltpu.VMEM((tq, tq), BF), pltpu.VMEM((tq, LANE), F32),
                        pltpu.VMEM((tq, LANE), F32), pltpu.VMEM((tq, HEAD_DIM), F32)],
        compiler_params=_params(),
    )(q, k, v, bias)


def _attn_bwd_dq(mode, q, k, v, o, do, lse, bias, tq, name, dep=None):
    t, hd = q.shape
    nh = hd // HEAD_DIM
    nb = t // tq
    wb = MAX_WINDOW // tq
    fox = mode == "fox"

    def body(q_ref, k_ref, v_ref, o_ref, do_ref, lse_ref, b_ref, dq_ref, dl_row_ref,
             s_ref, dp_ref, x_ref, y_ref, acc_ref, acc2_ref, dl_ref):
        qi = pl.program_id(1)
        qb = q_ref[...]
        dob = do_ref[...]
        acc_ref[...] = jnp.zeros_like(acc_ref)
        if fox:
            acc2_ref[...] = jnp.zeros_like(acc2_ref)
            dl_ref[...] = jnp.zeros_like(dl_ref)
        else:
            prod = o_ref[...].astype(F32) * dob.astype(F32)
            dl_ref[...] = jnp.broadcast_to(jnp.sum(prod, axis=1, keepdims=True), (tq, LANE))

        def tile(kj, diag):
            off = pl.multiple_of(kj * tq, tq)
            kb = k_ref[pl.ds(off, tq), :]
            s_ref[...] = _dot(qb, kb, NT)
            dp_ref[...] = _dot(dob, v_ref[pl.ds(off, tq), :], NT)
            if fox:
                brow = b_ref[qi][:, :1] - b_ref[kj]

            def chunk(r0):
                rows = pl.ds(r0, ATTN_ROWS)
                lse_c = _rep(lse_ref[rows, :], tq)
                if fox:
                    s = s_ref[rows, :] + brow
                    if diag:
                        s = jnp.where(_causal(r0, tq, False), s, NEG)
                    p = jnp.exp(s - lse_c)
                    pdp = p * dp_ref[rows, :]
                    dl_ref[rows, :] += jnp.sum(pdp, axis=1, keepdims=True)
                    x_ref[rows, :] = pdp.astype(BF)
                    y_ref[rows, :] = p.astype(BF)
                else:
                    p = jnp.exp(s_ref[rows, :] + b_ref[qi - kj, rows, :] - lse_c)
                    x_ref[rows, :] = (p * (dp_ref[rows, :] - _rep(dl_ref[rows, :], tq))).astype(BF)

            _chunks(tq, chunk)
            acc_ref[...] += _dot(x_ref[...], kb)
            if fox:
                acc2_ref[...] += _dot(y_ref[...], kb)

        tile(qi, True)
        if fox:
            lax.fori_loop(0, qi, lambda kj, c: (tile(kj, False), c)[1], 0)
            dq_ref[...] = acc_ref[...] - dl_ref[...] * acc2_ref[...]
        else:
            lax.fori_loop(1, jnp.minimum(qi, wb) + 1, lambda i, c: (tile(qi - i, False), c)[1], 0)
            dq_ref[...] = acc_ref[...]
        dl_row_ref[...] = _rows8(dl_ref[...])

    qspec = pl.BlockSpec((tq, HEAD_DIM), lambda h, i: (i, h))
    kvspec = pl.BlockSpec((t, HEAD_DIM), lambda h, i: (0, h))
    repspec = pl.BlockSpec((None, tq, LANE), lambda h, i: (h, i, 0))
    row8spec = pl.BlockSpec((None, None, 8, tq), lambda h, i: (h, i, 0, 0))
    if fox:
        bspec = pl.BlockSpec((None, nb, 1, tq), lambda h, i: (h, 0, 0, 0))
    else:
        bspec = pl.BlockSpec((wb + 1, tq, tq), lambda h, i: (0, 0, 0))
    return _call(
        body, [q, k, v, o, do, lse, bias], dep=dep, name=name, grid=(nh, nb),
        in_specs=[qspec, kvspec, kvspec, qspec, qspec, repspec, bspec],
        out_specs=[qspec, row8spec],
        out_shape=[jax.ShapeDtypeStruct((t, hd), F32), jax.ShapeDtypeStruct((nh, nb, 8, tq), F32)],
        scratch_shapes=[pltpu.VMEM((tq, tq), F32), pltpu.VMEM((tq, tq), F32), pltpu.VMEM((tq, tq), BF),
                        pltpu.VMEM((tq, tq), BF), pltpu.VMEM((tq, HEAD_DIM), F32),
                        pltpu.VMEM((tq, HEAD_DIM), F32), pltpu.VMEM((tq, LANE), F32)],
        compiler_params=_params(),
    )


def _attn_bwd_dkv(mode, q, k, v, do, lse_row, dl_row, bias_t, c_row, tq, name):
    t, hd = q.shape
    nh = hd // HEAD_DIM
    nb = t // tq
    wb = MAX_WINDOW // tq
    fox = mode == "fox"

    def body(*refs):
        if fox:
            (q_ref, k_ref, v_ref, do_ref, lse_ref, dl_ref, b_ref, cq_ref, dk_ref, dv_ref, dc_row_ref,
             s_ref, dp_ref, x_ref, y_ref, dc_ref) = refs
        else:
            q_ref, k_ref, v_ref, do_ref, lse_ref, dl_ref, b_ref, dk_ref, dv_ref, s_ref, dp_ref, x_ref, y_ref = refs
        kj = pl.program_id(1)
        kb = k_ref[...]
        vb = v_ref[...]
        dk_ref[...] = jnp.zeros_like(dk_ref)
        dv_ref[...] = jnp.zeros_like(dv_ref)
        if fox:
            dc_ref[...] = jnp.zeros_like(dc_ref)

        def tile(qi, diag):
            off = pl.multiple_of(qi * tq, tq)
            qb = q_ref[pl.ds(off, tq), :]
            dob = do_ref[pl.ds(off, tq), :]
            s_ref[...] = _dot(kb, qb, NT)
            dp_ref[...] = _dot(vb, dob, NT)
            lse_r = lse_ref[qi, 0:1, :]
            dl_r = dl_ref[qi, 0:1, :]
            if fox:
                kbias = cq_ref[qi][:, :1] - b_ref[...]

            def chunk(r0):
                rows = pl.ds(r0, ATTN_ROWS)
                if fox:
                    s = s_ref[rows, :] + _rep(kbias[r0:r0 + ATTN_ROWS, :], tq)
                    if diag:
                        s = jnp.where(_causal(r0, tq, True), s, NEG)
                else:
                    s = s_ref[rows, :] + b_ref[qi - kj, rows, :]
                pt = jnp.exp(s - lse_r)
                dst = pt * (dp_ref[rows, :] - dl_r)
                x_ref[rows, :] = pt.astype(BF)
                y_ref[rows, :] = dst.astype(BF)
                if fox:
                    dc_ref[rows, :] -= jnp.sum(dst, axis=1, keepdims=True)

            _chunks(tq, chunk)
            dv_ref[...] += _dot(x_ref[...], dob)
            dk_ref[...] += _dot(y_ref[...], qb)

        tile(kj, True)
        hi = nb if fox else jnp.minimum(kj + wb + 1, nb)
        lax.fori_loop(kj + 1, hi, lambda qi, c: (tile(qi, False), c)[1], 0)
        if fox:
            dc_row_ref[...] = _rows8(dc_ref[...])

    blkspec = pl.BlockSpec((tq, HEAD_DIM), lambda h, j: (j, h))
    fullspec = pl.BlockSpec((t, HEAD_DIM), lambda h, j: (0, h))
    rows8spec = pl.BlockSpec((None, nb, 8, tq), lambda h, j: (h, 0, 0, 0))
    repspec = pl.BlockSpec((None, tq, LANE), lambda h, j: (h, j, 0))
    in_specs = [fullspec, blkspec, blkspec, fullspec, rows8spec, rows8spec]
    args = [q, k, v, do, lse_row, dl_row, bias_t]
    out_specs = [blkspec, blkspec]
    out_shape = [jax.ShapeDtypeStruct((t, hd), F32), jax.ShapeDtypeStruct((t, hd), F32)]
    scratch = [pltpu.VMEM((tq, tq), F32), pltpu.VMEM((tq, tq), F32), pltpu.VMEM((tq, tq), BF),
               pltpu.VMEM((tq, tq), BF)]
    if fox:
        in_specs += [repspec, pl.BlockSpec((None, nb, 1, tq), lambda h, j: (h, 0, 0, 0))]
        args.append(c_row)
        out_specs.append(pl.BlockSpec((None, None, 8, tq), lambda h, j: (h, j, 0, 0)))
        out_shape.append(jax.ShapeDtypeStruct((nh, nb, 8, tq), F32))
        scratch.append(pltpu.VMEM((tq, LANE), F32))
    else:
        in_specs.append(pl.BlockSpec((wb + 1, tq, tq), lambda h, j: (0, 0, 0)))
    return pl.pallas_call(
        body, name=name, grid=(nh, nb), in_specs=in_specs, out_specs=out_specs, out_shape=out_shape,
        scratch_shapes=scratch, compiler_params=_params(),
    )(*args)


def _gate_specs(t, d, hd, tr):
    row = pl.BlockSpec((tr, d), lambda i: (i, 0))
    vec = pl.BlockSpec((1, d), lambda i: (0, 0))
    base = 6 * hd // d
    gd = pl.BlockSpec((tr, d), lambda i: (i, base))
    gf = pl.BlockSpec((tr, d), lambda i: (i, base + 1))
    return row, vec, gd, gf


def _proj_merge(yd, yf, wpd, wpf, proj, b_d, b_f, hd):
    t = yd.shape[0]
    d = wpd.shape[1]
    tr = _tile(t, 256, 16)
    row, vec, gd, gf = _gate_specs(t, d, hd, tr)

    def body(yd_ref, yf_ref, wd_ref, wf_ref, gd_ref, gf_ref, bd_ref, bf_ref, pd_ref, pf_ref, o_ref):
        pd = _dot(yd_ref[...], wd_ref[...])
        pf = _dot(yf_ref[...], wf_ref[...])
        pd_ref[...] = pd
        pf_ref[...] = pf
        o_ref[...] = (_sig(gd_ref[...] + bd_ref[...]) * pd + _sig(gf_ref[...] + bf_ref[...]) * pf).astype(BF)

    yspec = pl.BlockSpec((tr, hd), lambda i: (i, 0))
    wspec = pl.BlockSpec((hd, d), lambda i: (0, 0))
    f32 = jax.ShapeDtypeStruct((t, d), F32)
    return pl.pallas_call(
        body, name="proj_merge", grid=(t // tr,), in_specs=[yspec, yspec, wspec, wspec, gd, gf, vec, vec],
        out_specs=[row, row, row], out_shape=[f32, f32, jax.ShapeDtypeStruct((t, d), BF)],
        compiler_params=_params(),
    )(yd, yf, wpd, wpf, proj, proj, b_d, b_f)


def _merge_bwd(dm, pd, pf, proj, b_d, b_f, hd):
    t, d = pd.shape
    tr = _tile(t, 256, 16)
    row, vec, gd, gf = _gate_specs(t, d, hd, tr)

    def body(dm_ref, pd_ref, pf_ref, gd_ref, gf_ref, bd_ref, bf_ref,
             dpd_ref, dpf_ref, dgd_ref, dgf_ref, dbd_ref, dbf_ref):
        dmv = dm_ref[...]
        sd = _sig(gd_ref[...] + bd_ref[...])
        sf = _sig(gf_ref[...] + bf_ref[...])
        dgd = dmv * pd_ref[...] * (sd * (1.0 - sd))
        dgf = dmv * pf_ref[...] * (sf * (1.0 - sf))
        dpd_ref[...] = (dmv * sd).astype(BF)
        dpf_ref[...] = (dmv * sf).astype(BF)
        dgd_ref[...] = dgd.astype(BF)
        dgf_ref[...] = dgf.astype(BF)

        @pl.when(pl.program_id(0) == 0)
        def _():
            dbd_ref[...] = jnp.zeros_like(dbd_ref)
            dbf_ref[...] = jnp.zeros_like(dbf_ref)

        dbd_ref[...] += jnp.sum(dgd, axis=0, keepdims=True)
        dbf_ref[...] += jnp.sum(dgf, axis=0, keepdims=True)

    ob = jax.ShapeDtypeStruct((t, d), BF)
    ov = jax.ShapeDtypeStruct((1, d), F32)
    return pl.pallas_call(
        body, name="merge_bwd", grid=(t // tr,), in_specs=[row, row, row, gd, gf, vec, vec],
        out_specs=[row, row, row, row, vec, vec], out_shape=[ob, ob, ob, ob, ov, ov],
        compiler_params=_params(),
    )(dm, pd, pf, proj, proj, b_d, b_f)


def _assemble_dproj(dqd, dkd, dvd, dqf, dkf, dvf, dgd, dgf, dlogf, proj, tables, bf_pad, scale):
    t, np_ = proj.shape
    hd = dqd.shape[1]
    d = dgd.shape[1]
    nh = hd // HEAD_DIM
    tr = _tile(t, 256, 16)
    f_blk = (np_ - F_PAD) // LANE

    def body(dqd_ref, dkd_ref, dvd_ref, dqf_ref, dkf_ref, dvf_ref, dgd_ref, dgf_ref, dlog_ref, fl_ref,
             c_ref, s1_ref, s2_ref, b_ref, o_ref, db_ref):
        c, s1, s2 = c_ref[...], s1_ref[...], s2_ref[...]
        for h in range(nh):
            sl = slice(h * HEAD_DIM, (h + 1) * HEAD_DIM)
            o_ref[:, sl] = (_rope_t(dqd_ref[:, sl], c, s1, s2) * scale).astype(BF)
            o_ref[:, hd + h * HEAD_DIM:hd + (h + 1) * HEAD_DIM] = _rope_t(dkd_ref[:, sl], c, s1, s2).astype(BF)
        o_ref[:, 2 * hd:3 * hd] = dvd_ref[...].astype(BF)
        o_ref[:, 3 * hd:4 * hd] = (dqf_ref[...] * scale).astype(BF)
        o_ref[:, 4 * hd:5 * hd] = dkf_ref[...].astype(BF)
        o_ref[:, 5 * hd:6 * hd] = dvf_ref[...].astype(BF)
        o_ref[:, 6 * hd:6 * hd + d] = dgd_ref[...]
        o_ref[:, 6 * hd + d:6 * hd + 2 * d] = dgf_ref[...]
        z = fl_ref[...] + b_ref[...]
        dfl = dlog_ref[...] * _sig(-z)
        o_ref[:, 6 * hd + 2 * d:6 * hd + 2 * d + LANE] = dfl.astype(BF)
        o_ref[:, 6 * hd + 2 * d + LANE:] = jnp.zeros((tr, F_PAD - LANE), BF)

        @pl.when(pl.program_id(0) == 0)
        def _():
            db_ref[...] = jnp.zeros_like(db_ref)

        db_ref[...] += jnp.sum(dfl, axis=0, keepdims=True)

    head = pl.BlockSpec((tr, hd), lambda i: (i, 0))
    row = pl.BlockSpec((tr, d), lambda i: (i, 0))
    lane_row = pl.BlockSpec((tr, LANE), lambda i: (i, 0))
    lane_vec = pl.BlockSpec((1, LANE), lambda i: (0, 0))
    return pl.pallas_call(
        body, name="assemble_dproj", grid=(t // tr,),
        in_specs=[head] * 6 + [row, row, lane_row, pl.BlockSpec((tr, LANE), lambda i: (i, f_blk)),
                               lane_row, lane_row, lane_row, lane_vec],
        out_specs=[pl.BlockSpec((tr, np_), lambda i: (i, 0)), lane_vec],
        out_shape=[jax.ShapeDtypeStruct((t, np_), BF), jax.ShapeDtypeStruct((1, LANE), F32)],
        compiler_params=_params(),
    )(dqd, dkd, dvd, dqf, dkf, dvf, dgd, dgf, dlogf, proj, *tables, bf_pad)


def _to_rows(a, tq):
    h, t = a.shape
    return a.reshape(h, t // tq, 1, tq)


def kernel(x, ffn1_norm, ffn1_w_gate, ffn1_w_up, ffn1_w_down, mix_norm, w_in, b_forget, b_gate_dil, b_gate_fox, w_proj_dil, w_proj_fox, w_out, ffn2_norm, ffn2_w_gate, ffn2_w_up, ffn2_w_down, final_norm, loss_target, m_ffn1_norm, m_ffn1_w_gate, m_ffn1_w_up, m_ffn1_w_down, m_mix_norm, m_w_in, m_b_forget, m_b_gate_dil, m_b_gate_fox, m_w_proj_dil, m_w_proj_fox, m_w_out, m_ffn2_norm, m_ffn2_w_gate, m_ffn2_w_up, m_ffn2_w_down, m_final_norm, v_ffn1_norm, v_ffn1_w_gate, v_ffn1_w_up, v_ffn1_w_down, v_mix_norm, v_w_in, v_b_forget, v_b_gate_dil, v_b_gate_fox, v_w_proj_dil, v_w_proj_fox, v_w_out, v_ffn2_norm, v_ffn2_w_gate, v_ffn2_w_up, v_ffn2_w_down, v_final_norm):
    t, d = x.shape[1], x.shape[2]
    hd = w_proj_dil.shape[1]
    nh = hd // HEAD_DIM
    n_f = b_forget.shape[1]
    cols = w_in.shape[2]
    in_cols = N_DEV * cols
    assert in_cols == 6 * hd + n_f + 2 * d and n_f == nh and n_f <= LANE
    np_ = 6 * hd + 2 * d + F_PAD
    scale = HEAD_DIM ** -0.5
    tq = _tile(t, 512, LANE)
    assert MAX_WINDOW % tq == 0 and tq % 16 == 0

    x2d = x[0]
    tgt = loss_target[0]

    def rows(w):
        return jnp.swapaxes(w, 1, 2)

    fc = N_DEV * ffn1_w_down.shape[1]
    ag_order = [rows(ffn1_w_gate), rows(ffn1_w_up), ffn1_w_down, w_in, w_proj_dil, w_proj_fox, w_out,
                rows(ffn2_w_gate), rows(ffn2_w_up), ffn2_w_down]
    ag_first, tok = _exchange_start([w[0].astype(BF) for w in ag_order[:2]], True, "ag_start_first", ks=FIRST_LEVEL)
    ag_rest, ag_token = _exchange_start([w[0].astype(BF) for w in ag_order[2:]], True, "ag_start", dep=tok,
                                        ks=FIRST_LEVEL)
    ag = ag_first + ag_rest

    def relay(idx, after, name):
        for i, h in zip(idx, _gather_relay([ag[i] for i in idx], after, name)):
            ag[i] = h

    def gathered(idx, after, name):
        return _gather_wait([ag[i] for i in idx], after, name)

    def ffn_weight(idx, after, name):
        return [w.reshape(fc, d) for w in gathered(idx, after, name)]

    tables = _rope_tables(t)
    bf_pad = jnp.pad(b_forget, ((0, 0), (0, LANE - n_f)))

    hn1, = _rms_fwd(x2d, ffn1_norm, "rms_ffn1", dep=ag_token)
    relay([0], hn1, "ag_relay_ffn1_gate")
    wg1, = ffn_weight([0], hn1, "ag_wait_ffn1_gate")
    g1_f32 = _ffn_gate(hn1, wg1, "ffn1_gate")
    relay([1], g1_f32, "ag_relay_ffn1_up")
    wu1, = ffn_weight([1], g1_f32, "ag_wait_ffn1_up")
    relay([2], wu1, "ag_relay_ffn1_down")
    g1, u1, a1 = _ffn_up_act(hn1, wu1, g1_f32, "ffn1_up_act")
    wd1, = ffn_weight([2], a1, "ag_wait_ffn1_down")
    relay([3], wd1, "ag_relay_w_in")
    x1 = _ffn_down(a1, wd1, x2d, "ffn1_down")

    hm, hm_t = _rms_fwd(x1, mix_norm, "rms_mix", with_transpose=True)
    win_g, = gathered([3], hm, "ag_wait_w_in")
    relay([4, 5, 6], win_g, "ag_relay_mixer")
    segments = [(0, 6 * hd), (6 * hd + n_f, in_cols), (6 * hd, 6 * hd + n_f)]
    pieces = []
    for lo, hi in segments:
        for j in range(lo // cols, (hi - 1) // cols + 1):
            s, e = max(lo, j * cols), min(hi, (j + 1) * cols)
            pieces.append(win_g[j, :, s - j * cols:e - j * cols])
    win_p = jnp.concatenate(pieces + [jnp.zeros((d, F_PAD - n_f), BF)], axis=1)
    proj = _mm_nn(hm, win_p, F32, "w_in_fwd", tn_pref=W_IN_COLS)
    qd, kd, vd, qf, kf, vf, logf = _mixer_prep(proj, tables, bf_pad, hd, scale)
    csum = _cumsum_rows(logf, False, "cumsum_logf")
    c_heads = csum[:, :nh].T
    c_row = _to_rows(c_heads, tq)
    c_rep = jnp.broadcast_to(c_heads[:, :, None], (nh, t, LANE))
    dil_bias = _dil_bias_tiles(tq)
    dil_bias_t = dil_bias.transpose(0, 2, 1)
    relay([7, 8, 9], qd, "ag_relay_ffn2")
    yd, lse_d, lse_d_row = _attn_fwd("dil", qd, kd, vd, dil_bias, tq, "attn_dil_fwd")
    yf, lse_f, lse_f_row = _attn_fwd("fox", qf, kf, vf, c_row, tq, "attn_fox_fwd")
    wpd_g, wpf_g = gathered([4, 5], yf, "ag_wait_proj")
    wpd = wpd_g.transpose(1, 0, 2).reshape(hd, d)
    wpf = wpf_g.transpose(1, 0, 2).reshape(hd, d)
    pd, pf, merged = _proj_merge(yd, yf, wpd, wpf, proj, b_gate_dil, b_gate_fox, hd)
    wout_g, = gathered([6], merged, "ag_wait_w_out")
    wout = wout_g.reshape(d, d)
    x2 = _mm_nn(merged, wout, F32, "w_out_fwd", residual=x1, tn_pref=1024)

    hn2, = _rms_fwd(x2, ffn2_norm, "rms_ffn2")
    wg2, wu2 = ffn_weight([7, 8], hn2, "ag_wait_ffn2_gate_up")
    g2, u2, a2 = _ffn_gate_up(hn2, wg2, wu2, "ffn2_gate_up")
    wd2, = ffn_weight([9], a2, "ag_wait_ffn2_down")
    x3 = _ffn_down(a2, wd2, x2, "ffn2_down")

    dx3, dx3b, d_final, loss_lanes = _loss_head(x3, final_norm.reshape(1, d), tgt)

    def ffn_bwd(dxb, hn, g, u, a, wg_t, wu_t, wd, x_in, gain, dres, tag):
        def parts(dw):
            return dw.reshape(N_DEV, fc // N_DEV, d)

        dg, du = _ffn_bwd_hidden(dxb, wd, g, u, tag + "_bwd_hidden")
        dwd, = _ffn_dw([a], dxb, 0.5, tag + "_dw_down")
        rs_down, tok = _exchange_start([parts(dwd)], False, "rs_start_" + tag + "_down")
        dwg_t, dwu_t = _ffn_dw([dg, du], hn, 1.0, tag + "_dw_gate_up", dep=tok)
        rs_gu, tok = _exchange_start([parts(dwg_t), parts(dwu_t)], False, "rs_start_" + tag + "_gate_up")
        dhn = _ffn_bwd_input(dg, du, wg_t, wu_t, tag + "_bwd_input", dep=tok)
        dx, dx_bf, dgain = _rms_bwd(dhn, x_in, gain, dres, "rms_" + tag + "_bwd")
        return dx, dx_bf, dgain, rs_gu + rs_down

    dx2, dx2b, d_ffn2_norm, rs_ffn2 = ffn_bwd(dx3b, hn2, g2, u2, a2, wg2, wu2, wd2, x2, ffn2_norm, dx3, "ffn2")

    dmerged = _mm_nt(dx2b, wout, F32, "w_out_bwd")
    dwout = _mm_tn(merged, dx2b, BF, "w_out_dw", tn_pref=1024, tk_pref=DW_ROWS)
    dpd, dpf, dgd, dgf, d_bd, d_bf = _merge_bwd(dmerged, pd, pf, proj, b_gate_dil, b_gate_fox, hd)
    dyd = _mm_nt(dpd, wpd, BF, "proj_dil_bwd")
    dyf = _mm_nt(dpf, wpf, BF, "proj_fox_bwd")
    dwpd = _mm_tn(yd, dpd, BF, "proj_dil_dw", tn_pref=1024, tk_pref=DW_ROWS)
    dwpf = _mm_tn(yf, dpf, BF, "proj_fox_dw", tn_pref=1024, tk_pref=DW_ROWS)
    dwpd_c = dwpd.reshape(hd, N_DEV, d // N_DEV).transpose(1, 0, 2)
    dwpf_c = dwpf.reshape(hd, N_DEV, d // N_DEV).transpose(1, 0, 2)
    dwout_c = dwout.reshape(N_DEV, d // N_DEV, d)
    rs_mix, tok = _exchange_start([dwout_c, dwpd_c, dwpf_c], False, "rs_start_mixer")

    dqd, dl_d = _attn_bwd_dq("dil", qd, kd, vd, yd, dyd, lse_d, dil_bias, tq, "attn_dil_dq", dep=tok)
    dkd, dvd = _attn_bwd_dkv("dil", qd, kd, vd, dyd, lse_d_row, dl_d, dil_bias_t, None, tq, "attn_dil_dkv")
    dqf, dl_f = _attn_bwd_dq("fox", qf, kf, vf, yf, dyf, lse_f, c_row, tq, "attn_fox_dq")
    dkf, dvf, dc = _attn_bwd_dkv("fox", qf, kf, vf, dyf, lse_f_row, dl_f, c_rep, c_row, tq, "attn_fox_dkv")
    dc_pad = jnp.pad(dc[:, :, 0, :].reshape(nh, t).T, ((0, 0), (0, LANE - nh)))
    dlogf = _cumsum_rows(dc_pad, True, "revcumsum_dc")
    dproj, d_bforget = _assemble_dproj(dqd, dkd, dvd, dqf, dkf, dvf, dgd, dgf, dlogf, proj, tables, bf_pad, scale)

    dwin_p = _mm_tn(hm_t, dproj, BF, "w_in_dw", tn_pref=W_IN_COLS // 2, tk_pref=DW_ROWS, a_transposed=True)

    def perm_col(c):
        if c < 6 * hd:
            return c
        return c + 2 * d if c < 6 * hd + n_f else c - n_f

    shards = []
    for j in range(N_DEV):
        cuts = sorted({j * cols, (j + 1) * cols} | {c for c in (6 * hd, 6 * hd + n_f) if j * cols < c < (j + 1) * cols})
        shards.append(jnp.concatenate([dwin_p[:, perm_col(lo):perm_col(lo) + hi - lo]
                                       for lo, hi in zip(cuts[:-1], cuts[1:])], axis=1))
    dwin_c = jnp.stack(shards)
    rs_win, tok = _exchange_start([dwin_c], False, "rs_start_w_in")
    dhm = _mm_nt(dproj, win_p, F32, "w_in_bwd", tm_pref=1024, tn_pref=d, tk_pref=W_IN_COLS, dep=tok)
    dx1, dx1b, d_mix_norm = _rms_bwd(dhm, x1, mix_norm, dx2, "rms_mix_bwd")

    grad_x, _, d_ffn1_norm, rs_ffn1 = ffn_bwd(dx1b, hn1, g1, u1, a1, wg1, wu1, wd1, x2d, ffn1_norm, dx1, "ffn1")

    def update(handles, names, after, tag):
        recvs = _exchange_wait(handles, False, after, "rs_wait_" + tag)
        res = {}
        for recv, n in zip(recvs, names):
            turn = rows if n.endswith(("w_gate", "w_up")) else (lambda a: a)
            w, m, v = (turn(a)[0] for a in wmv[n])
            res[n] = tuple(turn(o[None]) for o in _adam_from_partials(recv, w, m, v, "adam_" + n))
        return res, res[names[-1]][0]

    wmv = {
        "ffn1_w_gate": (ffn1_w_gate, m_ffn1_w_gate, v_ffn1_w_gate),
        "ffn1_w_up": (ffn1_w_up, m_ffn1_w_up, v_ffn1_w_up),
        "ffn1_w_down": (ffn1_w_down, m_ffn1_w_down, v_ffn1_w_down),
        "w_in": (w_in, m_w_in, v_w_in),
        "w_proj_dil": (w_proj_dil, m_w_proj_dil, v_w_proj_dil),
        "w_proj_fox": (w_proj_fox, m_w_proj_fox, v_w_proj_fox),
        "w_out": (w_out, m_w_out, v_w_out),
        "ffn2_w_gate": (ffn2_w_gate, m_ffn2_w_gate, v_ffn2_w_gate),
        "ffn2_w_up": (ffn2_w_up, m_ffn2_w_up, v_ffn2_w_up),
        "ffn2_w_down": (ffn2_w_down, m_ffn2_w_down, v_ffn2_w_down),
    }
    big = {}
    after = grad_x
    for handles, names, tag in [
            (rs_ffn2, ["ffn2_w_gate", "ffn2_w_up", "ffn2_w_down"], "ffn2"),
            (rs_mix, ["w_out", "w_proj_dil", "w_proj_fox"], "mixer"),
            (rs_win, ["w_in"], "w_in"),
            (rs_ffn1, ["ffn1_w_gate", "ffn1_w_up", "ffn1_w_down"], "ffn1")]:
        res, after = update(handles, names, after, tag)
        big.update(res)

    def lanes(a):
        a = a.reshape(1, -1)
        return jnp.pad(a, ((0, 0), (0, d - a.shape[1])))

    small_names = ["ffn1_norm", "mix_norm", "b_gate_dil", "b_gate_fox", "ffn2_norm", "final_norm", "b_forget"]
    small_g = [d_ffn1_norm, d_mix_norm, d_bd, d_bf, d_ffn2_norm, d_final, d_bforget[:, :n_f]]
    small_w = [ffn1_norm, mix_norm, b_gate_dil, b_gate_fox, ffn2_norm, final_norm, b_forget]
    small_m = [m_ffn1_norm, m_mix_norm, m_b_gate_dil, m_b_gate_fox, m_ffn2_norm, m_final_norm, m_b_forget]
    small_v = [v_ffn1_norm, v_mix_norm, v_b_gate_dil, v_b_gate_fox, v_ffn2_norm, v_final_norm, v_b_forget]
    pack = lambda arrs, last: jnp.concatenate([lanes(a) for a in arrs] + [last], axis=0)
    g_all = _allreduce_small(pack(small_g, loss_lanes))
    zero_row = jnp.zeros((1, d), F32)
    one_row = jnp.ones((1, d), F32)
    s_delta, s_m, s_v = _adam_small(g_all, pack(small_w, zero_row), pack(small_m, zero_row), pack(small_v, one_row))
    loss = g_all[len(small_names), 0]

    def unpack(packed, i, like):
        return packed[i, :like.size].reshape(like.shape)

    small = {}
    for i, (n, w) in enumerate(zip(small_names, small_w)):
        small[n] = (unpack(g_all, i, w), unpack(s_delta, i, w), unpack(s_m, i, w), unpack(s_v, i, w))

    order = ["ffn1_norm", "ffn1_w_gate", "ffn1_w_up", "ffn1_w_down", "mix_norm", "w_in", "b_forget", "b_gate_dil",
             "b_gate_fox", "w_proj_dil", "w_proj_fox", "w_out", "ffn2_norm", "ffn2_w_gate", "ffn2_w_up",
             "ffn2_w_down", "final_norm"]
    res = {**big, **small}
    outs = [loss, grad_x[None]]
    for slot in range(4):
        outs += [res[n][slot] for n in order]
    return tuple(outs)
```

```python
import jax
import jax.numpy as jnp
from jax import lax
from jax.experimental import pallas as pl
from jax.experimental.pallas import tpu as pltpu

BF = jnp.bfloat16
F32 = jnp.float32
MESH = pl.DeviceIdType.MESH
N_DEV = 8

HEAD_DIM = 128
ROPE_DIM = HEAD_DIM // 4
ROPE_HALF = ROPE_DIM // 2
ROPE_THETA = 500000.0
NORM_EPS = 1e-6
DIL_PATTERNS = ((128, 1), (512, 4), (2048, 16))
MAX_WINDOW = 2048
LANE = 128
NEG = -1e30
F_PAD = 512
W_IN_COLS = 1536

ADAM_LR = 0.001
ADAM_B1 = 0.9
ADAM_B2 = 0.999
ADAM_EPS = 1e-08
ADAM_WD = 0.01
ADAM_STEP = 10

VMEM_LIMIT_BYTES = 56 * 1024 * 1024
FFN_ROWS = 1024
DW_ROWS = 2048
ANY = pl.BlockSpec(memory_space=pl.ANY)

NN = (((1,), (0,)), ((), ()))
NT = (((1,), (1,)), ((), ()))
TN = (((0,), (0,)), ((), ()))


def _dot(a, b, dn=NN):
    return lax.dot_general(a, b, dn, preferred_element_type=F32)


def _sig(x):
    return 0.5 + 0.5 * jnp.tanh(0.5 * x)


def _tile(n, pref, align):
    best = None
    t = align
    while t <= min(n, pref):
        if n % t == 0:
            best = t
        t += align
    return n if best is None else best


def _params():
    return pltpu.CompilerParams(vmem_limit_bytes=VMEM_LIMIT_BYTES)


def _call(body, args, dep=None, **kw):
    if dep is not None:
        n_in = len(args)
        inner = body

        def body(*refs):
            inner(*refs[:n_in], *refs[n_in + 1:])

        kw["in_specs"] = list(kw["in_specs"]) + [ANY]
        args = list(args) + [dep]
    return pl.pallas_call(body, **kw)(*args)


def _peers():
    x, y, c = lax.axis_index("x"), lax.axis_index("y"), lax.axis_index("c")
    me = 4 * x + 2 * y + c
    peers = []
    for k in range(1, N_DEV):
        px = 1 - x if (k >> 2) & 1 else x
        py = 1 - y if (k >> 1) & 1 else y
        pc = 1 - c if k & 1 else c
        peers.append((k, (px, py, pc), 4 * px + 2 * py + pc))
    return me, peers


HBM = pl.BlockSpec(memory_space=pltpu.HBM)
SEM = pl.BlockSpec(memory_space=pltpu.SEMAPHORE)
EFFECT = pltpu.SideEffectType.DATAFLOW_SIDE_EFFECTING


def _exchange_copy(gather, src_ref, land_ref, send_sems, recv_sems, me, k, peer, peer_flat, landing):
    return pltpu.make_async_remote_copy(
        src_ref=src_ref if gather else src_ref.at[peer_flat], dst_ref=land_ref.at[landing],
        send_sem=send_sems.at[k], recv_sem=recv_sems.at[k], device_id=peer, device_id_type=MESH)


ALL_PEERS = (1, 2, 3, 4, 5, 6, 7)
SIBLING = 1
SAME_CORE = (2, 4, 6)
FIRST_LEVEL = (SIBLING,) + SAME_CORE


def _exchange_start(srcs, gather, name, dep=None, ks=ALL_PEERS):
    n = len(srcs)
    extra = [] if dep is None else [dep]

    def body(*refs):
        src_refs, land_refs = refs[:n], refs[n:2 * n]
        refs = refs[2 * n + len(extra):]
        send_refs, recv_refs = refs[:n], refs[n:2 * n]
        token = refs[4 * n]
        me, peers = _peers()
        for i in range(n):
            for k, peer, peer_flat in peers:
                if k in ks:
                    _exchange_copy(gather, src_refs[i], land_refs[i], send_refs[i], recv_refs[i],
                                   me, k, peer, peer_flat, me).start()
        token[...] = jnp.zeros_like(token)

    lands = [lax.empty((N_DEV,) + s.shape[-2:], s.dtype) for s in srcs]
    sems = [pltpu.SemaphoreType.DMA((N_DEV,)) for _ in range(2 * n)]
    out = pl.pallas_call(
        body, name=name,
        out_shape=tuple(sems) + tuple(pltpu.HBM(a.shape, a.dtype) for a in list(srcs) + lands)
        + (jax.ShapeDtypeStruct((8, LANE), F32),),
        in_specs=[HBM] * (2 * n) + [ANY] * len(extra),
        out_specs=tuple([SEM] * (2 * n) + [HBM] * (2 * n) + [pl.BlockSpec(memory_space=pltpu.VMEM)]),
        input_output_aliases={i: 2 * n + i for i in range(2 * n)},
        compiler_params=pltpu.CompilerParams(has_side_effects=EFFECT),
    )(*[pltpu.with_memory_space_constraint(a, pltpu.HBM) for a in list(srcs) + lands], *extra)
    handles = [(out[2 * n + i], out[3 * n + i], out[i], out[n + i]) for i in range(n)]
    return handles, out[4 * n]


def _exchange_wait(handles, gather, after, name):
    n = len(handles)

    def body(*refs):
        src_refs, land_refs = refs[:n], refs[n:2 * n]
        send_refs, recv_refs = refs[2 * n:3 * n], refs[3 * n:4 * n]
        me, peers = _peers()
        for i in range(n):
            for k, peer, peer_flat in peers:
                cp = _exchange_copy(gather, src_refs[i], land_refs[i], send_refs[i], recv_refs[i],
                                    me, k, peer, peer_flat, peer_flat)
                cp.wait_send()
                cp.wait_recv()

    srcs = [h[0] for h in handles]
    lands = [h[1] for h in handles]
    out = pl.pallas_call(
        body, name=name,
        out_shape=tuple(pltpu.HBM(a.shape, a.dtype) for a in srcs + lands),
        in_specs=[HBM] * (2 * n) + [SEM] * (2 * n) + [ANY],
        out_specs=tuple([HBM] * (2 * n)),
        input_output_aliases={i: i for i in range(2 * n)},
        compiler_params=pltpu.CompilerParams(has_side_effects=EFFECT),
    )(*srcs, *lands, *[h[2] for h in handles], *[h[3] for h in handles], after)
    me = 4 * lax.axis_index("x") + 2 * lax.axis_index("y") + lax.axis_index("c")
    filled = []
    for src, land in zip(out[:n], out[n:]):
        own = src[None] if gather else lax.dynamic_slice_in_dim(src, me, 1, axis=0)
        filled.append(lax.dynamic_update_slice_in_dim(land, own, me, axis=0))
    return filled


def _gather_relay(handles, after, name):
    n = len(handles)

    def body(*refs):
        land_refs, recv_refs = refs[:n], refs[n:2 * n]
        refs = refs[2 * n + 1:]
        send2_refs, recv2_refs = refs[n:2 * n], refs[2 * n:3 * n]
        me, peers = _peers()
        sibling = peers[SIBLING - 1][1]
        for i in range(n):
            for k, peer, peer_flat in peers:
                if k in SAME_CORE:
                    block = land_refs[i].at[peer_flat]
                    pltpu.make_async_remote_copy(
                        src_ref=block, dst_ref=block, send_sem=send2_refs[i].at[k], recv_sem=recv_refs[i].at[k],
                        device_id=peer, device_id_type=MESH).wait_recv()
                    pltpu.make_async_remote_copy(
                        src_ref=block, dst_ref=block, send_sem=send2_refs[i].at[k], recv_sem=recv2_refs[i].at[k],
                        device_id=sibling, device_id_type=MESH).start()

    lands = [h[1] for h in handles]
    sems = [pltpu.SemaphoreType.DMA((N_DEV,)) for _ in range(2 * n)]
    out = pl.pallas_call(
        body, name=name,
        out_shape=tuple(pltpu.HBM(a.shape, a.dtype) for a in lands) + tuple(sems),
        in_specs=[HBM] * n + [SEM] * n + [ANY],
        out_specs=tuple([HBM] * n + [SEM] * (2 * n)),
        input_output_aliases={i: i for i in range(n)},
        compiler_params=pltpu.CompilerParams(has_side_effects=EFFECT),
    )(*lands, *[h[3] for h in handles], after)
    return [(h[0], out[i], h[2], h[3], out[n + i], out[2 * n + i]) for i, h in enumerate(handles)]


def _gather_wait(handles, after, name):
    n = len(handles)

    def body(*refs):
        src_refs, land_refs = refs[:n], refs[n:2 * n]
        send_refs, recv_refs = refs[2 * n:3 * n], refs[3 * n:4 * n]
        send2_refs, recv2_refs = refs[4 * n:5 * n], refs[5 * n:6 * n]
        me, peers = _peers()
        _, sibling, sibling_flat = peers[SIBLING - 1]
        for i in range(n):
            for k, peer, peer_flat in peers:
                if k in FIRST_LEVEL:
                    cp = _exchange_copy(True, src_refs[i], land_refs[i], send_refs[i], recv_refs[i],
                                        me, k, peer, peer_flat, peer_flat)
                    cp.wait_send()
                    if k == SIBLING:
                        cp.wait_recv()
                if k in SAME_CORE:
                    mine = land_refs[i].at[peer_flat]
                    theirs = land_refs[i].at[peer_flat ^ SIBLING]
                    cp = pltpu.make_async_remote_copy(
                        src_ref=mine, dst_ref=theirs, send_sem=send2_refs[i].at[k], recv_sem=recv2_refs[i].at[k],
                        device_id=sibling, device_id_type=MESH)
                    cp.wait_send()
                    cp.wait_recv()

    srcs = [h[0] for h in handles]
    lands = [h[1] for h in handles]
    out = pl.pallas_call(
        body, name=name,
        out_shape=tuple(pltpu.HBM(a.shape, a.dtype) for a in srcs + lands),
        in_specs=[HBM] * (2 * n) + [SEM] * (4 * n) + [ANY],
        out_specs=tuple([HBM] * (2 * n)),
        input_output_aliases={i: i for i in range(2 * n)},
        compiler_params=pltpu.CompilerParams(has_side_effects=EFFECT),
    )(*srcs, *lands, *[h[2] for h in handles], *[h[3] for h in handles],
      *[h[4] for h in handles], *[h[5] for h in handles], after)
    me = 4 * lax.axis_index("x") + 2 * lax.axis_index("y") + lax.axis_index("c")
    return [lax.dynamic_update_slice_in_dim(land, src[None], me, axis=0) for src, land in zip(out[:n], out[n:])]


def _allreduce_small(p):
    rows, d = p.shape

    def body(p_ref, o_ref, recv_ref, send_sems, recv_sems):
        me, peers = _peers()
        recv_ref[me] = p_ref[...]
        sends = []
        for k, peer, peer_flat in peers:
            cp = pltpu.make_async_remote_copy(
                src_ref=p_ref, dst_ref=recv_ref.at[me],
                send_sem=send_sems.at[k], recv_sem=recv_sems.at[k],
                device_id=peer, device_id_type=MESH)
            cp.start()
            sends.append(cp)
        for k, peer, peer_flat in peers:
            pltpu.make_async_remote_copy(
                src_ref=p_ref, dst_ref=recv_ref.at[peer_flat],
                send_sem=send_sems.at[k], recv_sem=recv_sems.at[k],
                device_id=peer, device_id_type=MESH).wait_recv()
        for cp in sends:
            cp.wait_send()
        acc = recv_ref[0]
        for s in range(1, N_DEV):
            acc = acc + recv_ref[s]
        is_loss = lax.broadcasted_iota(jnp.int32, (rows, d), 0) == rows - 1
        total = jnp.sum(jnp.where(is_loss, acc, 0.0))
        o_ref[...] = jnp.where(is_loss, total, acc)

    return pl.pallas_call(
        body, name="allreduce_small",
        out_shape=jax.ShapeDtypeStruct((rows, d), F32),
        in_specs=[pl.BlockSpec(memory_space=pltpu.VMEM)],
        out_specs=pl.BlockSpec(memory_space=pltpu.VMEM),
        scratch_shapes=[pltpu.VMEM((N_DEV, rows, d), F32),
                        pltpu.SemaphoreType.DMA((N_DEV,)), pltpu.SemaphoreType.DMA((N_DEV,))],
    )(p)


def _adam_math(w, g, m, v):
    m2 = ADAM_B1 * m + (1.0 - ADAM_B1) * g
    v2 = ADAM_B2 * v + (1.0 - ADAM_B2) * (g * g)
    m_hat = m2 / (1.0 - ADAM_B1 ** ADAM_STEP)
    v_hat = v2 / (1.0 - ADAM_B2 ** ADAM_STEP)
    delta = -ADAM_LR * (m_hat / (jnp.sqrt(v_hat) + ADAM_EPS) + ADAM_WD * w)
    return delta, m2, v2


def _adam_from_partials(parts, w, m, v, name):
    r, c = w.shape
    tr = _tile(r, 256, 16)

    def body(p_ref, w_ref, m_ref, v_ref, g_out, d_out, m_out, v_out):
        g = p_ref[0].astype(F32)
        for s in range(1, N_DEV):
            g = g + p_ref[s].astype(F32)
        delta, m2, v2 = _adam_math(w_ref[...], g, m_ref[...], v_ref[...])
        g_out[...] = g
        d_out[...] = delta
        m_out[...] = m2
        v_out[...] = v2

    blk = pl.BlockSpec((tr, c), lambda i: (i, 0))
    out = jax.ShapeDtypeStruct((r, c), F32)
    return pl.pallas_call(
        body, name=name, grid=(r // tr,),
        in_specs=[pl.BlockSpec((N_DEV, tr, c), lambda i: (0, i, 0)), blk, blk, blk],
        out_specs=[blk, blk, blk, blk], out_shape=[out, out, out, out],
        compiler_params=_params(),
    )(parts, w, m, v)


def _adam_small(g, w, m, v):
    def body(g_ref, w_ref, m_ref, v_ref, d_out, m_out, v_out):
        delta, m2, v2 = _adam_math(w_ref[...], g_ref[...], m_ref[...], v_ref[...])
        d_out[...] = delta
        m_out[...] = m2
        v_out[...] = v2

    out = jax.ShapeDtypeStruct(g.shape, F32)
    return pl.pallas_call(body, name="adam_small", out_shape=[out, out, out])(g, w, m, v)


def _rms_fwd(x, gain, name, dep=None, with_transpose=False):
    t, d = x.shape
    tr = _tile(t, 256, LANE)

    def body(x_ref, g_ref, o_ref, *ot_ref):
        xv = x_ref[...]
        r = lax.rsqrt(jnp.mean(xv * xv, axis=-1, keepdims=True) + NORM_EPS)
        y = xv * r * g_ref[...]
        o_ref[...] = y.astype(BF)
        if with_transpose:
            ot_ref[0][...] = jnp.transpose(y).astype(BF)

    out_specs = [pl.BlockSpec((tr, d), lambda i: (i, 0))]
    out_shape = [jax.ShapeDtypeStruct((t, d), BF)]
    if with_transpose:
        out_specs.append(pl.BlockSpec((d, tr), lambda i: (0, i)))
        out_shape.append(jax.ShapeDtypeStruct((d, t), BF))
    return _call(
        body, [x, gain], dep=dep, name=name, grid=(t // tr,),
        in_specs=[pl.BlockSpec((tr, d), lambda i: (i, 0)), pl.BlockSpec((1, d), lambda i: (0, 0))],
        out_specs=out_specs, out_shape=out_shape, compiler_params=_params(),
    )


def _rms_vjp(xv, gain, dy):
    r = lax.rsqrt(jnp.mean(xv * xv, axis=-1, keepdims=True) + NORM_EPS)
    xhat = xv * r
    dxhat = dy * gain
    dx = r * (dxhat - xhat * jnp.mean(dxhat * xhat, axis=-1, keepdims=True))
    dgain = jnp.sum(dy * xhat, axis=0, keepdims=True)
    return dx, dgain


def _loss_head(x, gain, target):
    t, d = x.shape
    tr = _tile(t, 256, 16)

    def body(x_ref, g_ref, t_ref, dx_ref, dxb_ref, dg_ref, loss_ref):
        xv = x_ref[...]
        gain = g_ref[...]
        r = lax.rsqrt(jnp.mean(xv * xv, axis=-1, keepdims=True) + NORM_EPS)
        err = xv * r * gain - t_ref[...]
        dx, dgain = _rms_vjp(xv, gain, err * (1.0 / d))
        dx_ref[...] = dx
        dxb_ref[...] = dx.astype(BF)

        @pl.when(pl.program_id(0) == 0)
        def _():
            dg_ref[...] = jnp.zeros_like(dg_ref)
            loss_ref[...] = jnp.zeros_like(loss_ref)

        dg_ref[...] += dgain
        loss_ref[...] += jnp.sum(err * err, axis=0, keepdims=True) * (0.5 / d)

    row = pl.BlockSpec((tr, d), lambda i: (i, 0))
    vec = pl.BlockSpec((1, d), lambda i: (0, 0))
    return pl.pallas_call(
        body, name="loss_head", grid=(t // tr,),
        in_specs=[row, vec, row], out_specs=[row, row, vec, vec],
        out_shape=[jax.ShapeDtypeStruct((t, d), F32), jax.ShapeDtypeStruct((t, d), BF),
                   jax.ShapeDtypeStruct((1, d), F32), jax.ShapeDtypeStruct((1, d), F32)],
        compiler_params=_params(),
    )(x, gain, target)


def _mm_nn(a, b, out_dtype, name, residual=None, tm_pref=512, tn_pref=1152):
    m, k = a.shape
    n = b.shape[1]
    tm, tn = _tile(m, tm_pref, 16), _tile(n, tn_pref, LANE)

    def body(*refs):
        if residual is None:
            a_ref, b_ref, o_ref = refs
            o_ref[...] = _dot(a_ref[...], b_ref[...]).astype(out_dtype)
        else:
            a_ref, b_ref, r_ref, o_ref = refs
            o_ref[...] = (r_ref[...] + _dot(a_ref[...], b_ref[...])).astype(out_dtype)

    in_specs = [pl.BlockSpec((tm, k), lambda j, i: (i, 0)), pl.BlockSpec((k, tn), lambda j, i: (0, j))]
    args = [a, b]
    if residual is not None:
        in_specs.append(pl.BlockSpec((tm, tn), lambda j, i: (i, j)))
        args.append(residual)
    return pl.pallas_call(
        body, name=name, grid=(n // tn, m // tm), in_specs=in_specs,
        out_specs=pl.BlockSpec((tm, tn), lambda j, i: (i, j)),
        out_shape=jax.ShapeDtypeStruct((m, n), out_dtype), compiler_params=_params(),
    )(*args)


def _rms_bwd_tail(dy_ref, first, x_ref, g_ref, dres_ref, dx_ref, dxb_ref, dg_ref):
    @pl.when(first)
    def _():
        dg_ref[...] = jnp.zeros_like(dg_ref)

    gain = g_ref[...]
    for r in range(0, dy_ref.shape[0], LANE):
        rows = pl.ds(r, min(LANE, dy_ref.shape[0] - r))
        dx, dgain = _rms_vjp(x_ref[rows, :], gain, dy_ref[rows, :])
        dx = dx + dres_ref[rows, :]
        dx_ref[rows, :] = dx
        dxb_ref[rows, :] = dx.astype(BF)
        dg_ref[...] += dgain


def _mm_nt(a, b, out_dtype, name, tm_pref=512, tn_pref=1024, tk_pref=2048, dep=None):
    m, k = a.shape
    n = b.shape[0]
    tm, tn, tk = _tile(m, tm_pref, 16), _tile(n, tn_pref, LANE), _tile(k, tk_pref, LANE)
    nk = k // tk

    def body(a_ref, b_ref, o_ref, acc_ref):
        kk = pl.program_id(2)

        @pl.when(kk == 0)
        def _():
            acc_ref[...] = jnp.zeros_like(acc_ref)

        acc_ref[...] += _dot(a_ref[...], b_ref[...], NT)

        @pl.when(kk == nk - 1)
        def _():
            o_ref[...] = acc_ref[...].astype(out_dtype)

    return _call(
        body, [a, b], dep=dep, name=name, grid=(n // tn, m // tm, nk),
        in_specs=[pl.BlockSpec((tm, tk), lambda j, i, kk: (i, kk)),
                  pl.BlockSpec((tn, tk), lambda j, i, kk: (j, kk))],
        out_specs=pl.BlockSpec((tm, tn), lambda j, i, kk: (i, j)),
        out_shape=jax.ShapeDtypeStruct((m, n), out_dtype),
        scratch_shapes=[pltpu.VMEM((tm, tn), F32)], compiler_params=_params(),
    )


def _mm_tn(a, b, out_dtype, name, tn_pref=1152, tk_pref=512, a_transposed=False):
    (k, t) = a.shape if a_transposed else a.shape[::-1]
    n = b.shape[1]
    tn, tk = _tile(n, tn_pref, LANE), _tile(t, tk_pref, LANE if a_transposed else 16)
    nt = t // tk

    def body(a_ref, b_ref, o_ref, acc_ref):
        tt = pl.program_id(1)

        @pl.when(tt == 0)
        def _():
            acc_ref[...] = jnp.zeros_like(acc_ref)

        acc_ref[...] += _dot(a_ref[...], b_ref[...], NN if a_transposed else TN)

        @pl.when(tt == nt - 1)
        def _():
            o_ref[...] = acc_ref[...].astype(out_dtype)

    if a_transposed:
        a_spec = pl.BlockSpec((k, tk), lambda j, tt: (0, tt))
    else:
        a_spec = pl.BlockSpec((tk, k), lambda j, tt: (tt, 0))
    return pl.pallas_call(
        body, name=name, grid=(n // tn, nt),
        in_specs=[a_spec, pl.BlockSpec((tk, tn), lambda j, tt: (tt, j))],
        out_specs=pl.BlockSpec((k, tn), lambda j, tt: (0, j)),
        out_shape=jax.ShapeDtypeStruct((k, n), out_dtype),
        scratch_shapes=[pltpu.VMEM((k, tn), F32)], compiler_params=_params(),
    )(a, b)


FFN_COLS = 512


FFN_ROWS_WIDE = 2048


def _ffn_tiles(t, fc, rows=FFN_ROWS):
    return _tile(t, rows, 16), _tile(fc, FFN_COLS, LANE)


def _slabs(tm, rows=256):
    step = rows if tm % rows == 0 else tm
    return [pl.ds(r, step) for r in range(0, tm, step)]


def _ffn_gate_up(hn, wg_t, wu_t, name):
    t, d = hn.shape
    fc = wg_t.shape[0]
    tm, tn = _ffn_tiles(t, fc, FFN_ROWS_WIDE)

    def body(h_ref, wg_ref, wu_ref, g_ref, u_ref, a_ref):
        for rows in _slabs(tm):
            h = h_ref[rows, :]
            g = _dot(h, wg_ref[...], NT)
            u = _dot(h, wu_ref[...], NT)
            g_ref[rows, :] = g.astype(BF)
            u_ref[rows, :] = u.astype(BF)
            a_ref[rows, :] = (g * _sig(g) * u).astype(BF)

    wspec = pl.BlockSpec((tn, d), lambda j, i: (j, 0))
    hid = pl.BlockSpec((tm, tn), lambda j, i: (i, j))
    out = jax.ShapeDtypeStruct((t, fc), BF)
    return pl.pallas_call(
        body, name=name, grid=(fc // tn, t // tm),
        in_specs=[pl.BlockSpec((tm, d), lambda j, i: (i, 0)), wspec, wspec],
        out_specs=[hid, hid, hid], out_shape=[out, out, out], compiler_params=_params(),
    )(hn, wg_t, wu_t)


def _ffn_gate(hn, wg_t, name):
    t, d = hn.shape
    fc = wg_t.shape[0]
    tm, tn = _ffn_tiles(t, fc)

    def body(h_ref, wg_ref, g_ref):
        g_ref[...] = _dot(h_ref[...], wg_ref[...], NT)

    return pl.pallas_call(
        body, name=name, grid=(fc // tn, t // tm),
        in_specs=[pl.BlockSpec((tm, d), lambda j, i: (i, 0)), pl.BlockSpec((tn, d), lambda j, i: (j, 0))],
        out_specs=pl.BlockSpec((tm, tn), lambda j, i: (i, j)),
        out_shape=jax.ShapeDtypeStruct((t, fc), F32), compiler_params=_params(),
    )(hn, wg_t)


def _ffn_up_act(hn, wu_t, g, name):
    t, d = hn.shape
    fc = wu_t.shape[0]
    tm, tn = _ffn_tiles(t, fc, FFN_ROWS_WIDE)

    def body(h_ref, wu_ref, g_ref, gb_ref, u_ref, a_ref):
        for rows in _slabs(tm):
            u = _dot(h_ref[rows, :], wu_ref[...], NT)
            gv = g_ref[rows, :]
            gb_ref[rows, :] = gv.astype(BF)
            u_ref[rows, :] = u.astype(BF)
            a_ref[rows, :] = (gv * _sig(gv) * u).astype(BF)

    hid = pl.BlockSpec((tm, tn), lambda j, i: (i, j))
    out = jax.ShapeDtypeStruct((t, fc), BF)
    return pl.pallas_call(
        body, name=name, grid=(fc // tn, t // tm),
        in_specs=[pl.BlockSpec((tm, d), lambda j, i: (i, 0)), pl.BlockSpec((tn, d), lambda j, i: (j, 0)), hid],
        out_specs=[hid, hid, hid], out_shape=[out, out, out], compiler_params=_params(),
    )(hn, wu_t, g)


def _ffn_down(act, wd, xres, name):
    t, fc = act.shape
    d = wd.shape[1]
    tm, tk = _ffn_tiles(t, fc)

    def body(a_ref, w_ref, x_ref, o_ref):
        @pl.when(pl.program_id(1) == 0)
        def _():
            o_ref[...] = x_ref[...]

        o_ref[...] += 0.5 * _dot(a_ref[...], w_ref[...])

    row = pl.BlockSpec((tm, d), lambda i, k: (i, 0))
    return pl.pallas_call(
        body, name=name, grid=(t // tm, fc // tk),
        in_specs=[pl.BlockSpec((tm, tk), lambda i, k: (i, k)), pl.BlockSpec((tk, d), lambda i, k: (k, 0)), row],
        out_specs=row, out_shape=jax.ShapeDtypeStruct((t, d), F32), compiler_params=_params(),
    )(act, wd, xres)


def _ffn_bwd_hidden(dxb, wd, g, u, name):
    t, d = dxb.shape
    fc = wd.shape[0]
    tm, tn = _ffn_tiles(t, fc, FFN_ROWS_WIDE)

    def body(dx_ref, w_ref, g_ref, u_ref, dg_ref, du_ref):
        for rows in _slabs(tm):
            dh = 0.5 * _dot(dx_ref[rows, :], w_ref[...], NT)
            gv = g_ref[rows, :].astype(F32)
            uv = u_ref[rows, :].astype(F32)
            s = _sig(gv)
            dg_ref[rows, :] = (dh * uv * (s * (1.0 + gv * (1.0 - s)))).astype(BF)
            du_ref[rows, :] = (dh * (gv * s)).astype(BF)

    hid = pl.BlockSpec((tm, tn), lambda i, j: (i, j))
    out = jax.ShapeDtypeStruct((t, fc), BF)
    return pl.pallas_call(
        body, name=name, grid=(t // tm, fc // tn),
        in_specs=[pl.BlockSpec((tm, d), lambda i, j: (i, 0)), pl.BlockSpec((tn, d), lambda i, j: (j, 0)), hid, hid],
        out_specs=[hid, hid], out_shape=[out, out], compiler_params=_params(),
    )(dxb, wd, g, u)


def _ffn_dw(lhs, rhs, scale, name, dep=None):
    n = len(lhs)
    t, fc = lhs[0].shape
    d = rhs.shape[1]
    tk, tn = _tile(t, DW_ROWS, 16), _tile(fc, FFN_COLS, LANE)
    nt = t // tk

    def body(*refs):
        l_refs, r_ref, o_refs, acc_refs = refs[:n], refs[n], refs[n + 1:2 * n + 1], refs[2 * n + 1:]
        tt = pl.program_id(1)
        r = r_ref[...]
        for l_ref, o_ref, acc_ref in zip(l_refs, o_refs, acc_refs):
            @pl.when(tt == 0)
            def _():
                acc_ref[...] = jnp.zeros_like(acc_ref)

            acc_ref[...] += _dot(l_ref[...], r, TN)

            @pl.when(tt == nt - 1)
            def _():
                o_ref[...] = (scale * acc_ref[...]).astype(BF)

    lspec = pl.BlockSpec((tk, tn), lambda j, tt: (tt, j))
    ospec = pl.BlockSpec((tn, d), lambda j, tt: (j, 0))
    out = jax.ShapeDtypeStruct((fc, d), BF)
    return _call(
        body, [*lhs, rhs], dep=dep, name=name, grid=(fc // tn, nt),
        in_specs=[lspec] * n + [pl.BlockSpec((tk, d), lambda j, tt: (tt, 0))],
        out_specs=[ospec] * n, out_shape=[out] * n,
        scratch_shapes=[pltpu.VMEM((tn, d), F32)] * n, compiler_params=_params(),
    )


def _rms_bwd(dy, x, gain, dres, name):
    t, d = x.shape
    tr = _tile(t, 256, 16)

    def body(dy_ref, x_ref, g_ref, dres_ref, dx_ref, dxb_ref, dg_ref):
        _rms_bwd_tail(dy_ref, pl.program_id(0) == 0, x_ref, g_ref, dres_ref, dx_ref, dxb_ref, dg_ref)

    row = pl.BlockSpec((tr, d), lambda i: (i, 0))
    vec = pl.BlockSpec((1, d), lambda i: (0, 0))
    return pl.pallas_call(
        body, name=name, grid=(t // tr,),
        in_specs=[row, row, vec, row], out_specs=[row, row, vec],
        out_shape=[jax.ShapeDtypeStruct((t, d), F32), jax.ShapeDtypeStruct((t, d), BF),
                   jax.ShapeDtypeStruct((1, d), F32)],
        compiler_params=_params(),
    )(dy, x, gain, dres)


def _ffn_bwd_input(dg, du, wg_t, wu_t, name, dep=None):
    t, fc = dg.shape
    d = wg_t.shape[1]
    tm, tk = _ffn_tiles(t, fc)

    def body(dg_ref, du_ref, wg_ref, wu_ref, o_ref):
        @pl.when(pl.program_id(1) == 0)
        def _():
            o_ref[...] = jnp.zeros_like(o_ref)

        o_ref[...] += _dot(dg_ref[...], wg_ref[...]) + _dot(du_ref[...], wu_ref[...])

    hid = pl.BlockSpec((tm, tk), lambda i, k: (i, k))
    wspec = pl.BlockSpec((tk, d), lambda i, k: (k, 0))
    return _call(
        body, [dg, du, wg_t, wu_t], dep=dep, name=name, grid=(t // tm, fc // tk),
        in_specs=[hid, hid, wspec, wspec],
        out_specs=pl.BlockSpec((tm, d), lambda i, k: (i, 0)),
        out_shape=jax.ShapeDtypeStruct((t, d), F32), compiler_params=_params(),
    )


def _rope_tables(t):
    pos = jnp.arange(t, dtype=F32)
    inv_freq = ROPE_THETA ** (-jnp.arange(0, ROPE_DIM, 2, dtype=F32) / ROPE_DIM)
    ang = pos[:, None] * inv_freq[None, :]
    cos, sin = jnp.cos(ang), jnp.sin(ang)
    rest = HEAD_DIM - ROPE_DIM
    one = jnp.ones((t, rest), F32)
    zero_h = jnp.zeros((t, ROPE_HALF), F32)
    zero_r = jnp.zeros((t, rest), F32)
    c = jnp.concatenate([cos, cos, one], axis=1)
    s1 = jnp.concatenate([-sin, zero_h, zero_r], axis=1)
    s2 = jnp.concatenate([zero_h, sin, zero_r], axis=1)
    return c, s1, s2


def _rope(xh, c, s1, s2):
    return xh * c + pltpu.roll(xh, HEAD_DIM - ROPE_HALF, 1) * s1 + pltpu.roll(xh, ROPE_HALF, 1) * s2


def _rope_t(dh, c, s1, s2):
    return dh * c + pltpu.roll(dh * s1, ROPE_HALF, 1) + pltpu.roll(dh * s2, HEAD_DIM - ROPE_HALF, 1)


def _mixer_prep(proj, tables, bf_pad, hd, scale):
    t, np_ = proj.shape
    tr = _tile(t, 256, 16)
    nh = hd // HEAD_DIM
    nblk = hd // LANE
    f_blk = (np_ - F_PAD) // LANE

    def body(qd_ref, kd_ref, vd_ref, qf_ref, kf_ref, vf_ref, fl_ref, c_ref, s1_ref, s2_ref, b_ref,
             oqd, okd, ovd, oqf, okf, ovf, olog):
        c, s1, s2 = c_ref[...], s1_ref[...], s2_ref[...]
        for h in range(nh):
            sl = slice(h * HEAD_DIM, (h + 1) * HEAD_DIM)
            oqd[:, sl] = (_rope(qd_ref[:, sl], c, s1, s2) * scale).astype(BF)
            okd[:, sl] = _rope(kd_ref[:, sl], c, s1, s2).astype(BF)
        ovd[...] = vd_ref[...].astype(BF)
        oqf[...] = (qf_ref[...] * scale).astype(BF)
        okf[...] = kf_ref[...].astype(BF)
        ovf[...] = vf_ref[...].astype(BF)
        z = fl_ref[...] + b_ref[...]
        olog[...] = jnp.minimum(z, 0.0) - jnp.log(1.0 + jnp.exp(-jnp.abs(z)))

    def col(kblk):
        return pl.BlockSpec((tr, hd), lambda i, kblk=kblk: (i, kblk))

    lane_row = pl.BlockSpec((tr, LANE), lambda i: (i, 0))
    in_specs = [col(0), col(1), col(2), col(3), col(4), col(5),
                pl.BlockSpec((tr, LANE), lambda i: (i, f_blk)),
                lane_row, lane_row, lane_row, pl.BlockSpec((1, LANE), lambda i: (0, 0))]
    o = pl.BlockSpec((tr, hd), lambda i: (i, 0))
    ob = jax.ShapeDtypeStruct((t, hd), BF)
    del nblk
    return pl.pallas_call(
        body, name="mixer_prep", grid=(t // tr,), in_specs=in_specs,
        out_specs=[o, o, o, o, o, o, lane_row],
        out_shape=[ob, ob, ob, ob, ob, ob, jax.ShapeDtypeStruct((t, LANE), F32)],
        compiler_params=_params(),
    )(proj, proj, proj, proj, proj, proj, proj, *tables, bf_pad)


def _split3(x):
    x1 = x.astype(BF)
    r1 = x - x1.astype(F32)
    x2 = r1.astype(BF)
    x3 = (r1 - x2.astype(F32)).astype(BF)
    return x1, x2, x3


def _cumsum_rows(x, reverse, name):
    t, w = x.shape
    blk = LANE
    nb = t // blk

    def body(x_ref, o_ref):
        r = lax.broadcasted_iota(jnp.int32, (blk, blk), 0)
        c = lax.broadcasted_iota(jnp.int32, (blk, blk), 1)
        tri = jnp.where((c >= r) if reverse else (c <= r), 1.0, 0.0).astype(BF)

        def step(i, carry):
            b = (nb - 1 - i) if reverse else i
            off = pl.multiple_of(b * blk, blk)
            xb = x_ref[pl.ds(off, blk), :]
            x1, x2, x3 = _split3(xb)
            o_ref[pl.ds(off, blk), :] = _dot(tri, x1) + _dot(tri, x2) + _dot(tri, x3) + carry
            return carry + jnp.sum(xb, axis=0, keepdims=True)

        lax.fori_loop(0, nb, step, jnp.zeros((1, w), F32))

    return pl.pallas_call(body, name=name, out_shape=jax.ShapeDtypeStruct((t, w), F32),
                          compiler_params=_params())(x)


ATTN_ROWS = 16


def _dil_bias_tiles(tq):
    nbias = MAX_WINDOW // tq + 1
    b = lax.broadcasted_iota(jnp.int32, (nbias, tq, tq), 0)
    i = lax.broadcasted_iota(jnp.int32, (nbias, tq, tq), 1)
    j = lax.broadcasted_iota(jnp.int32, (nbias, tq, tq), 2)
    delta = b * tq + i - j
    mult = jnp.zeros((nbias, tq, tq), F32)
    for w, dil in DIL_PATTERNS:
        mult = mult + jnp.where((delta >= 0) & (delta <= w) & (delta % dil == 0), 1.0, 0.0)
    return jnp.where(mult > 0.0, jnp.log(jnp.maximum(mult, 1.0)), NEG)


def _rep(x, width):
    return jnp.tile(x, (1, width // LANE))


def _chunks(n_rows, fn):
    for c in range(n_rows // ATTN_ROWS):
        fn(c * ATTN_ROWS)


def _causal(r0, tq, transposed):
    a = lax.broadcasted_iota(jnp.int32, (ATTN_ROWS, tq), 0) + r0
    b = lax.broadcasted_iota(jnp.int32, (ATTN_ROWS, tq), 1)
    return (a <= b) if transposed else (b <= a)


def _rows8(x):
    return jnp.transpose(x)[:8, :]


def _attn_fwd(mode, q, k, v, bias, tq, name):
    t, hd = q.shape
    nh = hd // HEAD_DIM
    nb = t // tq
    wb = MAX_WINDOW // tq
    fox = mode == "fox"

    def body(q_ref, k_ref, v_ref, b_ref, o_ref, lse_ref, lse_row_ref, s_ref, p_ref, m_ref, l_ref, acc_ref):
        qi = pl.program_id(1)
        qb = q_ref[...]
        m_ref[...] = jnp.full_like(m_ref, NEG)
        l_ref[...] = jnp.zeros_like(l_ref)
        acc_ref[...] = jnp.zeros_like(acc_ref)

        def tile(kj, diag):
            off = pl.multiple_of(kj * tq, tq)
            s_ref[...] = _dot(qb, k_ref[pl.ds(off, tq), :], NT)
            if fox:
                brow = b_ref[qi][:, :1] - b_ref[kj]

            def chunk(r0):
                rows = pl.ds(r0, ATTN_ROWS)
                if fox:
                    s = s_ref[rows, :] + brow
                    if diag:
                        s = jnp.where(_causal(r0, tq, False), s, NEG)
                else:
                    s = s_ref[rows, :] + b_ref[qi - kj, rows, :]
                m_old = m_ref[rows, :]
                m_new = jnp.maximum(m_old, jnp.max(s, axis=1, keepdims=True))
                p = jnp.exp(s - _rep(m_new, tq))
                alpha = jnp.exp(m_old - m_new)
                l_ref[rows, :] = alpha * l_ref[rows, :] + jnp.sum(p, axis=1, keepdims=True)
                m_ref[rows, :] = m_new
                acc_ref[rows, :] = alpha * acc_ref[rows, :]
                p_ref[rows, :] = p.astype(BF)

            _chunks(tq, chunk)
            acc_ref[...] += _dot(p_ref[...], v_ref[pl.ds(off, tq), :])

        tile(qi, True)
        if fox:
            lax.fori_loop(0, qi, lambda kj, c: (tile(kj, False), c)[1], 0)
        else:
            lax.fori_loop(1, jnp.minimum(qi, wb) + 1, lambda i, c: (tile(qi - i, False), c)[1], 0)
        o_ref[...] = (acc_ref[...] / l_ref[...]).astype(BF)
        lse = m_ref[...] + jnp.log(l_ref[...])
        lse_ref[...] = lse
        lse_row_ref[...] = _rows8(lse)

    qspec = pl.BlockSpec((tq, HEAD_DIM), lambda h, i: (i, h))
    kvspec = pl.BlockSpec((t, HEAD_DIM), lambda h, i: (0, h))
    repspec = pl.BlockSpec((None, tq, LANE), lambda h, i: (h, i, 0))
    row8spec = pl.BlockSpec((None, None, 8, tq), lambda h, i: (h, i, 0, 0))
    if fox:
        bspec = pl.BlockSpec((None, nb, 1, tq), lambda h, i: (h, 0, 0, 0))
    else:
        bspec = pl.BlockSpec((wb + 1, tq, tq), lambda h, i: (0, 0, 0))
    return pl.pallas_call(
        body, name=name, grid=(nh, nb), in_specs=[qspec, kvspec, kvspec, bspec],
        out_specs=[qspec, repspec, row8spec],
        out_shape=[jax.ShapeDtypeStruct((t, hd), BF), jax.ShapeDtypeStruct((nh, t, LANE), F32),
                   jax.ShapeDtypeStruct((nh, nb, 8, tq), F32)],
        scratch_shapes=[pltpu.VMEM((tq, tq), F32), pltpu.VMEM((tq, tq), BF), pltpu.VMEM((tq, LANE), F32),
                        pltpu.VMEM((tq, LANE), F32), pltpu.VMEM((tq, HEAD_DIM), F32)],
        compiler_params=_params(),
    )(q, k, v, bias)


def _attn_bwd_dq(mode, q, k, v, o, do, lse, bias, tq, name, dep=None):
    t, hd = q.shape
    nh = hd // HEAD_DIM
    nb = t // tq
    wb = MAX_WINDOW // tq
    fox = mode == "fox"

    def body(q_ref, k_ref, v_ref, o_ref, do_ref, lse_ref, b_ref, dq_ref, dl_row_ref,
             s_ref, dp_ref, x_ref, y_ref, acc_ref, acc2_ref, dl_ref):
        qi = pl.program_id(1)
        qb = q_ref[...]
        dob = do_ref[...]
        acc_ref[...] = jnp.zeros_like(acc_ref)
        if fox:
            acc2_ref[...] = jnp.zeros_like(acc2_ref)
            dl_ref[...] = jnp.zeros_like(dl_ref)
        else:
            prod = o_ref[...].astype(F32) * dob.astype(F32)
            dl_ref[...] = jnp.broadcast_to(jnp.sum(prod, axis=1, keepdims=True), (tq, LANE))

        def tile(kj, diag):
            off = pl.multiple_of(kj * tq, tq)
            kb = k_ref[pl.ds(off, tq), :]
            s_ref[...] = _dot(qb, kb, NT)
            dp_ref[...] = _dot(dob, v_ref[pl.ds(off, tq), :], NT)
            if fox:
                brow = b_ref[qi][:, :1] - b_ref[kj]

            def chunk(r0):
                rows = pl.ds(r0, ATTN_ROWS)
                lse_c = _rep(lse_ref[rows, :], tq)
                if fox:
                    s = s_ref[rows, :] + brow
                    if diag:
                        s = jnp.where(_causal(r0, tq, False), s, NEG)
                    p = jnp.exp(s - lse_c)
                    pdp = p * dp_ref[rows, :]
                    dl_ref[rows, :] += jnp.sum(pdp, axis=1, keepdims=True)
                    x_ref[rows, :] = pdp.astype(BF)
                    y_ref[rows, :] = p.astype(BF)
                else:
                    p = jnp.exp(s_ref[rows, :] + b_ref[qi - kj, rows, :] - lse_c)
                    x_ref[rows, :] = (p * (dp_ref[rows, :] - _rep(dl_ref[rows, :], tq))).astype(BF)

            _chunks(tq, chunk)
            acc_ref[...] += _dot(x_ref[...], kb)
            if fox:
                acc2_ref[...] += _dot(y_ref[...], kb)

        tile(qi, True)
        if fox:
            lax.fori_loop(0, qi, lambda kj, c: (tile(kj, False), c)[1], 0)
            dq_ref[...] = acc_ref[...] - dl_ref[...] * acc2_ref[...]
        else:
            lax.fori_loop(1, jnp.minimum(qi, wb) + 1, lambda i, c: (tile(qi - i, False), c)[1], 0)
            dq_ref[...] = acc_ref[...]
        dl_row_ref[...] = _rows8(dl_ref[...])

    qspec = pl.BlockSpec((tq, HEAD_DIM), lambda h, i: (i, h))
    kvspec = pl.BlockSpec((t, HEAD_DIM), lambda h, i: (0, h))
    repspec = pl.BlockSpec((None, tq, LANE), lambda h, i: (h, i, 0))
    row8spec = pl.BlockSpec((None, None, 8, tq), lambda h, i: (h, i, 0, 0))
    if fox:
        bspec = pl.BlockSpec((None, nb, 1, tq), lambda h, i: (h, 0, 0, 0))
    else:
        bspec = pl.BlockSpec((wb + 1, tq, tq), lambda h, i: (0, 0, 0))
    return _call(
        body, [q, k, v, o, do, lse, bias], dep=dep, name=name, grid=(nh, nb),
        in_specs=[qspec, kvspec, kvspec, qspec, qspec, repspec, bspec],
        out_specs=[qspec, row8spec],
        out_shape=[jax.ShapeDtypeStruct((t, hd), F32), jax.ShapeDtypeStruct((nh, nb, 8, tq), F32)],
        scratch_shapes=[pltpu.VMEM((tq, tq), F32), pltpu.VMEM((tq, tq), F32), pltpu.VMEM((tq, tq), BF),
                        pltpu.VMEM((tq, tq), BF), pltpu.VMEM((tq, HEAD_DIM), F32),
                        pltpu.VMEM((tq, HEAD_DIM), F32), pltpu.VMEM((tq, LANE), F32)],
        compiler_params=_params(),
    )


def _attn_bwd_dkv(mode, q, k, v, do, lse_row, dl_row, bias_t, c_row, tq, name):
    t, hd = q.shape
    nh = hd // HEAD_DIM
    nb = t // tq
    wb = MAX_WINDOW // tq
    fox = mode == "fox"

    def body(*refs):
        if fox:
            (q_ref, k_ref, v_ref, do_ref, lse_ref, dl_ref, b_ref, cq_ref, dk_ref, dv_ref, dc_row_ref,
             s_ref, dp_ref, x_ref, y_ref, dc_ref) = refs
        else:
            q_ref, k_ref, v_ref, do_ref, lse_ref, dl_ref, b_ref, dk_ref, dv_ref, s_ref, dp_ref, x_ref, y_ref = refs
        kj = pl.program_id(1)
        kb = k_ref[...]
        vb = v_ref[...]
        dk_ref[...] = jnp.zeros_like(dk_ref)
        dv_ref[...] = jnp.zeros_like(dv_ref)
        if fox:
            dc_ref[...] = jnp.zeros_like(dc_ref)

        def tile(qi, diag):
            off = pl.multiple_of(qi * tq, tq)
            qb = q_ref[pl.ds(off, tq), :]
            dob = do_ref[pl.ds(off, tq), :]
            s_ref[...] = _dot(kb, qb, NT)
            dp_ref[...] = _dot(vb, dob, NT)
            lse_r = lse_ref[qi, 0:1, :]
            dl_r = dl_ref[qi, 0:1, :]
            if fox:
                kbias = cq_ref[qi][:, :1] - b_ref[...]

            def chunk(r0):
                rows = pl.ds(r0, ATTN_ROWS)
                if fox:
                    s = s_ref[rows, :] + _rep(kbias[r0:r0 + ATTN_ROWS, :], tq)
                    if diag:
                        s = jnp.where(_causal(r0, tq, True), s, NEG)
                else:
                    s = s_ref[rows, :] + b_ref[qi - kj, rows, :]
                pt = jnp.exp(s - lse_r)
                dst = pt * (dp_ref[rows, :] - dl_r)
                x_ref[rows, :] = pt.astype(BF)
                y_ref[rows, :] = dst.astype(BF)
                if fox:
                    dc_ref[rows, :] -= jnp.sum(dst, axis=1, keepdims=True)

            _chunks(tq, chunk)
            dv_ref[...] += _dot(x_ref[...], dob)
            dk_ref[...] += _dot(y_ref[...], qb)

        tile(kj, True)
        hi = nb if fox else jnp.minimum(kj + wb + 1, nb)
        lax.fori_loop(kj + 1, hi, lambda qi, c: (tile(qi, False), c)[1], 0)
        if fox:
            dc_row_ref[...] = _rows8(dc_ref[...])

    blkspec = pl.BlockSpec((tq, HEAD_DIM), lambda h, j: (j, h))
    fullspec = pl.BlockSpec((t, HEAD_DIM), lambda h, j: (0, h))
    rows8spec = pl.BlockSpec((None, nb, 8, tq), lambda h, j: (h, 0, 0, 0))
    repspec = pl.BlockSpec((None, tq, LANE), lambda h, j: (h, j, 0))
    in_specs = [fullspec, blkspec, blkspec, fullspec, rows8spec, rows8spec]
    args = [q, k, v, do, lse_row, dl_row, bias_t]
    out_specs = [blkspec, blkspec]
    out_shape = [jax.ShapeDtypeStruct((t, hd), F32), jax.ShapeDtypeStruct((t, hd), F32)]
    scratch = [pltpu.VMEM((tq, tq), F32), pltpu.VMEM((tq, tq), F32), pltpu.VMEM((tq, tq), BF),
               pltpu.VMEM((tq, tq), BF)]
    if fox:
        in_specs += [repspec, pl.BlockSpec((None, nb, 1, tq), lambda h, j: (h, 0, 0, 0))]
        args.append(c_row)
        out_specs.append(pl.BlockSpec((None, None, 8, tq), lambda h, j: (h, j, 0, 0)))
        out_shape.append(jax.ShapeDtypeStruct((nh, nb, 8, tq), F32))
        scratch.append(pltpu.VMEM((tq, LANE), F32))
    else:
        in_specs.append(pl.BlockSpec((wb + 1, tq, tq), lambda h, j: (0, 0, 0)))
    return pl.pallas_call(
        body, name=name, grid=(nh, nb), in_specs=in_specs, out_specs=out_specs, out_shape=out_shape,
        scratch_shapes=scratch, compiler_params=_params(),
    )(*args)


def _gate_specs(t, d, hd, tr):
    row = pl.BlockSpec((tr, d), lambda i: (i, 0))
    vec = pl.BlockSpec((1, d), lambda i: (0, 0))
    base = 6 * hd // d
    gd = pl.BlockSpec((tr, d), lambda i: (i, base))
    gf = pl.BlockSpec((tr, d), lambda i: (i, base + 1))
    return row, vec, gd, gf


def _proj_merge(yd, yf, wpd, wpf, proj, b_d, b_f, hd):
    t = yd.shape[0]
    d = wpd.shape[1]
    tr = _tile(t, 256, 16)
    row, vec, gd, gf = _gate_specs(t, d, hd, tr)

    def body(yd_ref, yf_ref, wd_ref, wf_ref, gd_ref, gf_ref, bd_ref, bf_ref, pd_ref, pf_ref, o_ref):
        pd = _dot(yd_ref[...], wd_ref[...])
        pf = _dot(yf_ref[...], wf_ref[...])
        pd_ref[...] = pd
        pf_ref[...] = pf
        o_ref[...] = (_sig(gd_ref[...] + bd_ref[...]) * pd + _sig(gf_ref[...] + bf_ref[...]) * pf).astype(BF)

    yspec = pl.BlockSpec((tr, hd), lambda i: (i, 0))
    wspec = pl.BlockSpec((hd, d), lambda i: (0, 0))
    f32 = jax.ShapeDtypeStruct((t, d), F32)
    return pl.pallas_call(
        body, name="proj_merge", grid=(t // tr,), in_specs=[yspec, yspec, wspec, wspec, gd, gf, vec, vec],
        out_specs=[row, row, row], out_shape=[f32, f32, jax.ShapeDtypeStruct((t, d), BF)],
        compiler_params=_params(),
    )(yd, yf, wpd, wpf, proj, proj, b_d, b_f)


def _merge_bwd(dm, pd, pf, proj, b_d, b_f, hd):
    t, d = pd.shape
    tr = _tile(t, 256, 16)
    row, vec, gd, gf = _gate_specs(t, d, hd, tr)

    def body(dm_ref, pd_ref, pf_ref, gd_ref, gf_ref, bd_ref, bf_ref,
             dpd_ref, dpf_ref, dgd_ref, dgf_ref, dbd_ref, dbf_ref):
        dmv = dm_ref[...]
        sd = _sig(gd_ref[...] + bd_ref[...])
        sf = _sig(gf_ref[...] + bf_ref[...])
        dgd = dmv * pd_ref[...] * (sd * (1.0 - sd))
        dgf = dmv * pf_ref[...] * (sf * (1.0 - sf))
        dpd_ref[...] = (dmv * sd).astype(BF)
        dpf_ref[...] = (dmv * sf).astype(BF)
        dgd_ref[...] = dgd.astype(BF)
        dgf_ref[...] = dgf.astype(BF)

        @pl.when(pl.program_id(0) == 0)
        def _():
            dbd_ref[...] = jnp.zeros_like(dbd_ref)
            dbf_ref[...] = jnp.zeros_like(dbf_ref)

        dbd_ref[...] += jnp.sum(dgd, axis=0, keepdims=True)
        dbf_ref[...] += jnp.sum(dgf, axis=0, keepdims=True)

    ob = jax.ShapeDtypeStruct((t, d), BF)
    ov = jax.ShapeDtypeStruct((1, d), F32)
    return pl.pallas_call(
        body, name="merge_bwd", grid=(t // tr,), in_specs=[row, row, row, gd, gf, vec, vec],
        out_specs=[row, row, row, row, vec, vec], out_shape=[ob, ob, ob, ob, ov, ov],
        compiler_params=_params(),
    )(dm, pd, pf, proj, proj, b_d, b_f)


def _assemble_dproj(dqd, dkd, dvd, dqf, dkf, dvf, dgd, dgf, dlogf, proj, tables, bf_pad, scale):
    t, np_ = proj.shape
    hd = dqd.shape[1]
    d = dgd.shape[1]
    nh = hd // HEAD_DIM
    tr = _tile(t, 256, 16)
    f_blk = (np_ - F_PAD) // LANE

    def body(dqd_ref, dkd_ref, dvd_ref, dqf_ref, dkf_ref, dvf_ref, dgd_ref, dgf_ref, dlog_ref, fl_ref,
             c_ref, s1_ref, s2_ref, b_ref, o_ref, db_ref):
        c, s1, s2 = c_ref[...], s1_ref[...], s2_ref[...]
        for h in range(nh):
            sl = slice(h * HEAD_DIM, (h + 1) * HEAD_DIM)
            o_ref[:, sl] = (_rope_t(dqd_ref[:, sl], c, s1, s2) * scale).astype(BF)
            o_ref[:, hd + h * HEAD_DIM:hd + (h + 1) * HEAD_DIM] = _rope_t(dkd_ref[:, sl], c, s1, s2).astype(BF)
        o_ref[:, 2 * hd:3 * hd] = dvd_ref[...].astype(BF)
        o_ref[:, 3 * hd:4 * hd] = (dqf_ref[...] * scale).astype(BF)
        o_ref[:, 4 * hd:5 * hd] = dkf_ref[...].astype(BF)
        o_ref[:, 5 * hd:6 * hd] = dvf_ref[...].astype(BF)
        o_ref[:, 6 * hd:6 * hd + d] = dgd_ref[...]
        o_ref[:, 6 * hd + d:6 * hd + 2 * d] = dgf_ref[...]
        z = fl_ref[...] + b_ref[...]
        dfl = dlog_ref[...] * _sig(-z)
        o_ref[:, 6 * hd + 2 * d:6 * hd + 2 * d + LANE] = dfl.astype(BF)
        o_ref[:, 6 * hd + 2 * d + LANE:] = jnp.zeros((tr, F_PAD - LANE), BF)

        @pl.when(pl.program_id(0) == 0)
        def _():
            db_ref[...] = jnp.zeros_like(db_ref)

        db_ref[...] += jnp.sum(dfl, axis=0, keepdims=True)

    head = pl.BlockSpec((tr, hd), lambda i: (i, 0))
    row = pl.BlockSpec((tr, d), lambda i: (i, 0))
    lane_row = pl.BlockSpec((tr, LANE), lambda i: (i, 0))
    lane_vec = pl.BlockSpec((1, LANE), lambda i: (0, 0))
    return pl.pallas_call(
        body, name="assemble_dproj", grid=(t // tr,),
        in_specs=[head] * 6 + [row, row, lane_row, pl.BlockSpec((tr, LANE), lambda i: (i, f_blk)),
                               lane_row, lane_row, lane_row, lane_vec],
        out_specs=[pl.BlockSpec((tr, np_), lambda i: (i, 0)), lane_vec],
        out_shape=[jax.ShapeDtypeStruct((t, np_), BF), jax.ShapeDtypeStruct((1, LANE), F32)],
        compiler_params=_params(),
    )(dqd, dkd, dvd, dqf, dkf, dvf, dgd, dgf, dlogf, proj, *tables, bf_pad)


def _to_rows(a, tq):
    h, t = a.shape
    return a.reshape(h, t // tq, 1, tq)


def kernel(x, ffn1_norm, ffn1_w_gate, ffn1_w_up, ffn1_w_down, mix_norm, w_in, b_forget, b_gate_dil, b_gate_fox, w_proj_dil, w_proj_fox, w_out, ffn2_norm, ffn2_w_gate, ffn2_w_up, ffn2_w_down, final_norm, loss_target, m_ffn1_norm, m_ffn1_w_gate, m_ffn1_w_up, m_ffn1_w_down, m_mix_norm, m_w_in, m_b_forget, m_b_gate_dil, m_b_gate_fox, m_w_proj_dil, m_w_proj_fox, m_w_out, m_ffn2_norm, m_ffn2_w_gate, m_ffn2_w_up, m_ffn2_w_down, m_final_norm, v_ffn1_norm, v_ffn1_w_gate, v_ffn1_w_up, v_ffn1_w_down, v_mix_norm, v_w_in, v_b_forget, v_b_gate_dil, v_b_gate_fox, v_w_proj_dil, v_w_proj_fox, v_w_out, v_ffn2_norm, v_ffn2_w_gate, v_ffn2_w_up, v_ffn2_w_down, v_final_norm):
    t, d = x.shape[1], x.shape[2]
    hd = w_proj_dil.shape[1]
    nh = hd // HEAD_DIM
    n_f = b_forget.shape[1]
    cols = w_in.shape[2]
    in_cols = N_DEV * cols
    assert in_cols == 6 * hd + n_f + 2 * d and n_f == nh and n_f <= LANE
    np_ = 6 * hd + 2 * d + F_PAD
    scale = HEAD_DIM ** -0.5
    tq = _tile(t, 512, LANE)
    assert MAX_WINDOW % tq == 0 and tq % 16 == 0

    x2d = x[0]
    tgt = loss_target[0]

    def rows(w):
        return jnp.swapaxes(w, 1, 2)

    fc = N_DEV * ffn1_w_down.shape[1]
    ag_order = [rows(ffn1_w_gate), rows(ffn1_w_up), ffn1_w_down, w_in, w_proj_dil, w_proj_fox, w_out,
                rows(ffn2_w_gate), rows(ffn2_w_up), ffn2_w_down]
    ag_first, tok = _exchange_start([w[0].astype(BF) for w in ag_order[:2]], True, "ag_start_first", ks=FIRST_LEVEL)
    ag_rest, ag_token = _exchange_start([w[0].astype(BF) for w in ag_order[2:]], True, "ag_start", dep=tok,
                                        ks=FIRST_LEVEL)
    ag = ag_first + ag_rest

    def relay(idx, after, name):
        for i, h in zip(idx, _gather_relay([ag[i] for i in idx], after, name)):
            ag[i] = h

    def gathered(idx, after, name):
        return _gather_wait([ag[i] for i in idx], after, name)

    def ffn_weight(idx, after, name):
        return [w.reshape(fc, d) for w in gathered(idx, after, name)]

    tables = _rope_tables(t)
    bf_pad = jnp.pad(b_forget, ((0, 0), (0, LANE - n_f)))

    hn1, = _rms_fwd(x2d, ffn1_norm, "rms_ffn1", dep=ag_token)
    relay([0], hn1, "ag_relay_ffn1_gate")
    wg1, = ffn_weight([0], hn1, "ag_wait_ffn1_gate")
    g1_f32 = _ffn_gate(hn1, wg1, "ffn1_gate")
    relay([1], g1_f32, "ag_relay_ffn1_up")
    wu1, = ffn_weight([1], g1_f32, "ag_wait_ffn1_up")
    relay([2], wu1, "ag_relay_ffn1_down")
    g1, u1, a1 = _ffn_up_act(hn1, wu1, g1_f32, "ffn1_up_act")
    wd1, = ffn_weight([2], a1, "ag_wait_ffn1_down")
    relay([3], wd1, "ag_relay_w_in")
    x1 = _ffn_down(a1, wd1, x2d, "ffn1_down")

    hm, hm_t = _rms_fwd(x1, mix_norm, "rms_mix", with_transpose=True)
    win_g, = gathered([3], hm, "ag_wait_w_in")
    relay([4, 5, 6], win_g, "ag_relay_mixer")
    segments = [(0, 6 * hd), (6 * hd + n_f, in_cols), (6 * hd, 6 * hd + n_f)]
    pieces = []
    for lo, hi in segments:
        for j in range(lo // cols, (hi - 1) // cols + 1):
            s, e = max(lo, j * cols), min(hi, (j + 1) * cols)
            pieces.append(win_g[j, :, s - j * cols:e - j * cols])
    win_p = jnp.concatenate(pieces + [jnp.zeros((d, F_PAD - n_f), BF)], axis=1)
    proj = _mm_nn(hm, win_p, F32, "w_in_fwd", tm_pref=1024, tn_pref=W_IN_COLS)
    qd, kd, vd, qf, kf, vf, logf = _mixer_prep(proj, tables, bf_pad, hd, scale)
    csum = _cumsum_rows(logf, False, "cumsum_logf")
    c_heads = csum[:, :nh].T
    c_row = _to_rows(c_heads, tq)
    c_rep = jnp.broadcast_to(c_heads[:, :, None], (nh, t, LANE))
    dil_bias = _dil_bias_tiles(tq)
    dil_bias_t = dil_bias.transpose(0, 2, 1)
    relay([7, 8, 9], qd, "ag_relay_ffn2")
    yd, lse_d, lse_d_row = _attn_fwd("dil", qd, kd, vd, dil_bias, tq, "attn_dil_fwd")
    yf, lse_f, lse_f_row = _attn_fwd("fox", qf, kf, vf, c_row, tq, "attn_fox_fwd")
    wpd_g, wpf_g = gathered([4, 5], yf, "ag_wait_proj")
    wpd = wpd_g.transpose(1, 0, 2).reshape(hd, d)
    wpf = wpf_g.transpose(1, 0, 2).reshape(hd, d)
    pd, pf, merged = _proj_merge(yd, yf, wpd, wpf, proj, b_gate_dil, b_gate_fox, hd)
    wout_g, = gathered([6], merged, "ag_wait_w_out")
    wout = wout_g.reshape(d, d)
    x2 = _mm_nn(merged, wout, F32, "w_out_fwd", residual=x1, tn_pref=1024)

    hn2, = _rms_fwd(x2, ffn2_norm, "rms_ffn2")
    wg2, wu2 = ffn_weight([7, 8], hn2, "ag_wait_ffn2_gate_up")
    g2, u2, a2 = _ffn_gate_up(hn2, wg2, wu2, "ffn2_gate_up")
    wd2, = ffn_weight([9], a2, "ag_wait_ffn2_down")
    x3 = _ffn_down(a2, wd2, x2, "ffn2_down")

    dx3, dx3b, d_final, loss_lanes = _loss_head(x3, final_norm.reshape(1, d), tgt)

    def ffn_bwd(dxb, hn, g, u, a, wg_t, wu_t, wd, x_in, gain, dres, tag):
        def parts(dw):
            return dw.reshape(N_DEV, fc // N_DEV, d)

        dg, du = _ffn_bwd_hidden(dxb, wd, g, u, tag + "_bwd_hidden")
        dwd, = _ffn_dw([a], dxb, 0.5, tag + "_dw_down")
        rs_down, tok = _exchange_start([parts(dwd)], False, "rs_start_" + tag + "_down")
        dwg_t, dwu_t = _ffn_dw([dg, du], hn, 1.0, tag + "_dw_gate_up", dep=tok)
        rs_gu, tok = _exchange_start([parts(dwg_t), parts(dwu_t)], False, "rs_start_" + tag + "_gate_up")
        dhn = _ffn_bwd_input(dg, du, wg_t, wu_t, tag + "_bwd_input", dep=tok)
        dx, dx_bf, dgain = _rms_bwd(dhn, x_in, gain, dres, "rms_" + tag + "_bwd")
        return dx, dx_bf, dgain, rs_gu + rs_down

    dx2, dx2b, d_ffn2_norm, rs_ffn2 = ffn_bwd(dx3b, hn2, g2, u2, a2, wg2, wu2, wd2, x2, ffn2_norm, dx3, "ffn2")

    dmerged = _mm_nt(dx2b, wout, F32, "w_out_bwd")
    dwout = _mm_tn(merged, dx2b, BF, "w_out_dw", tn_pref=1024, tk_pref=DW_ROWS)
    dpd, dpf, dgd, dgf, d_bd, d_bf = _merge_bwd(dmerged, pd, pf, proj, b_gate_dil, b_gate_fox, hd)
    dyd = _mm_nt(dpd, wpd, BF, "proj_dil_bwd")
    dyf = _mm_nt(dpf, wpf, BF, "proj_fox_bwd")
    dwpd = _mm_tn(yd, dpd, BF, "proj_dil_dw", tn_pref=1024, tk_pref=DW_ROWS)
    dwpf = _mm_tn(yf, dpf, BF, "proj_fox_dw", tn_pref=1024, tk_pref=DW_ROWS)
    dwpd_c = dwpd.reshape(hd, N_DEV, d // N_DEV).transpose(1, 0, 2)
    dwpf_c = dwpf.reshape(hd, N_DEV, d // N_DEV).transpose(1, 0, 2)
    dwout_c = dwout.reshape(N_DEV, d // N_DEV, d)
    rs_mix, tok = _exchange_start([dwout_c, dwpd_c, dwpf_c], False, "rs_start_mixer")

    dqd, dl_d = _attn_bwd_dq("dil", qd, kd, vd, yd, dyd, lse_d, dil_bias, tq, "attn_dil_dq", dep=tok)
    dkd, dvd = _attn_bwd_dkv("dil", qd, kd, vd, dyd, lse_d_row, dl_d, dil_bias_t, None, tq, "attn_dil_dkv")
    dqf, dl_f = _attn_bwd_dq("fox", qf, kf, vf, yf, dyf, lse_f, c_row, tq, "attn_fox_dq")
    dkf, dvf, dc = _attn_bwd_dkv("fox", qf, kf, vf, dyf, lse_f_row, dl_f, c_rep, c_row, tq, "attn_fox_dkv")
    dc_pad = jnp.pad(dc[:, :, 0, :].reshape(nh, t).T, ((0, 0), (0, LANE - nh)))
    dlogf = _cumsum_rows(dc_pad, True, "revcumsum_dc")
    dproj, d_bforget = _assemble_dproj(dqd, dkd, dvd, dqf, dkf, dvf, dgd, dgf, dlogf, proj, tables, bf_pad, scale)

    dwin_p = _mm_tn(hm_t, dproj, BF, "w_in_dw", tn_pref=W_IN_COLS // 2, tk_pref=DW_ROWS, a_transposed=True)

    def perm_col(c):
        if c < 6 * hd:
            return c
        return c + 2 * d if c < 6 * hd + n_f else c - n_f

    shards = []
    for j in range(N_DEV):
        cuts = sorted({j * cols, (j + 1) * cols} | {c for c in (6 * hd, 6 * hd + n_f) if j * cols < c < (j + 1) * cols})
        shards.append(jnp.concatenate([dwin_p[:, perm_col(lo):perm_col(lo) + hi - lo]
                                       for lo, hi in zip(cuts[:-1], cuts[1:])], axis=1))
    dwin_c = jnp.stack(shards)
    rs_win, tok = _exchange_start([dwin_c], False, "rs_start_w_in")
    dhm = _mm_nt(dproj, win_p, F32, "w_in_bwd", tm_pref=1024, tn_pref=d, tk_pref=W_IN_COLS, dep=tok)
    dx1, dx1b, d_mix_norm = _rms_bwd(dhm, x1, mix_norm, dx2, "rms_mix_bwd")

    grad_x, _, d_ffn1_norm, rs_ffn1 = ffn_bwd(dx1b, hn1, g1, u1, a1, wg1, wu1, wd1, x2d, ffn1_norm, dx1, "ffn1")

    def update(handles, names, after, tag):
        recvs = _exchange_wait(handles, False, after, "rs_wait_" + tag)
        res = {}
        for recv, n in zip(recvs, names):
            turn = rows if n.endswith(("w_gate", "w_up")) else (lambda a: a)
            w, m, v = (turn(a)[0] for a in wmv[n])
            res[n] = tuple(turn(o[None]) for o in _adam_from_partials(recv, w, m, v, "adam_" + n))
        return res, res[names[-1]][0]

    wmv = {
        "ffn1_w_gate": (ffn1_w_gate, m_ffn1_w_gate, v_ffn1_w_gate),
        "ffn1_w_up": (ffn1_w_up, m_ffn1_w_up, v_ffn1_w_up),
        "ffn1_w_down": (ffn1_w_down, m_ffn1_w_down, v_ffn1_w_down),
        "w_in": (w_in, m_w_in, v_w_in),
        "w_proj_dil": (w_proj_dil, m_w_proj_dil, v_w_proj_dil),
        "w_proj_fox": (w_proj_fox, m_w_proj_fox, v_w_proj_fox),
        "w_out": (w_out, m_w_out, v_w_out),
        "ffn2_w_gate": (ffn2_w_gate, m_ffn2_w_gate, v_ffn2_w_gate),
        "ffn2_w_up": (ffn2_w_up, m_ffn2_w_up, v_ffn2_w_up),
        "ffn2_w_down": (ffn2_w_down, m_ffn2_w_down, v_ffn2_w_down),
    }
    big = {}
    after = grad_x
    for handles, names, tag in [
            (rs_ffn2, ["ffn2_w_gate", "ffn2_w_up", "ffn2_w_down"], "ffn2"),
            (rs_mix, ["w_out", "w_proj_dil", "w_proj_fox"], "mixer"),
            (rs_win, ["w_in"], "w_in"),
            (rs_ffn1, ["ffn1_w_gate", "ffn1_w_up", "ffn1_w_down"], "ffn1")]:
        res, after = update(handles, names, after, tag)
        big.update(res)

    def lanes(a):
        a = a.reshape(1, -1)
        return jnp.pad(a, ((0, 0), (0, d - a.shape[1])))

    small_names = ["ffn1_norm", "mix_norm", "b_gate_dil", "b_gate_fox", "ffn2_norm", "final_norm", "b_forget"]
    small_g = [d_ffn1_norm, d_mix_norm, d_bd, d_bf, d_ffn2_norm, d_final, d_bforget[:, :n_f]]
    small_w = [ffn1_norm, mix_norm, b_gate_dil, b_gate_fox, ffn2_norm, final_norm, b_forget]
    small_m = [m_ffn1_norm, m_mix_norm, m_b_gate_dil, m_b_gate_fox, m_ffn2_norm, m_final_norm, m_b_forget]
    small_v = [v_ffn1_norm, v_mix_norm, v_b_gate_dil, v_b_gate_fox, v_ffn2_norm, v_final_norm, v_b_forget]
    pack = lambda arrs, last: jnp.concatenate([lanes(a) for a in arrs] + [last], axis=0)
    g_all = _allreduce_small(pack(small_g, loss_lanes))
    zero_row = jnp.zeros((1, d), F32)
    one_row = jnp.ones((1, d), F32)
    s_delta, s_m, s_v = _adam_small(g_all, pack(small_w, zero_row), pack(small_m, zero_row), pack(small_v, one_row))
    loss = g_all[len(small_names), 0]

    def unpack(packed, i, like):
        return packed[i, :like.size].reshape(like.shape)

    small = {}
    for i, (n, w) in enumerate(zip(small_names, small_w)):
        small[n] = (unpack(g_all, i, w), unpack(s_delta, i, w), unpack(s_m, i, w), unpack(s_v, i, w))

    order = ["ffn1_norm", "ffn1_w_gate", "ffn1_w_up", "ffn1_w_down", "mix_norm", "w_in", "b_forget", "b_gate_dil",
             "b_gate_fox", "w_proj_dil", "w_proj_fox", "w_out", "ffn2_norm", "ffn2_w_gate", "ffn2_w_up",
             "ffn2_w_down", "final_norm"]
    res = {**big, **small}
    outs = [loss, grad_x[None]]
    for slot in range(4):
        outs += [res[n][slot] for n in order]
    return tuple(outs)
```

```python
import jax
import jax.numpy as jnp
from jax import lax
from jax.experimental import pallas as pl
from jax.experimental.pallas import tpu as pltpu

BF = jnp.bfloat16
F32 = jnp.float32
MESH = pl.DeviceIdType.MESH
N_DEV = 8

HEAD_DIM = 128
ROPE_DIM = HEAD_DIM // 4
ROPE_HALF = ROPE_DIM // 2
ROPE_THETA = 500000.0
NORM_EPS = 1e-6
DIL_PATTERNS = ((128, 1), (512, 4), (2048, 16))
MAX_WINDOW = 2048
LANE = 128
NEG = -1e30
F_PAD = 512
W_IN_COLS = 1536

ADAM_LR = 0.001
ADAM_B1 = 0.9
ADAM_B2 = 0.999
ADAM_EPS = 1e-08
ADAM_WD = 0.01
ADAM_STEP = 10

VMEM_LIMIT_BYTES = 56 * 1024 * 1024
FFN_ROWS = 1024
DW_ROWS = 2048
ANY = pl.BlockSpec(memory_space=pl.ANY)

NN = (((1,), (0,)), ((), ()))
NT = (((1,), (1,)), ((), ()))
TN = (((0,), (0,)), ((), ()))


def _dot(a, b, dn=NN):
    return lax.dot_general(a, b, dn, preferred_element_type=F32)


def _sig(x):
    return 0.5 + 0.5 * jnp.tanh(0.5 * x)


def _tile(n, pref, align):
    best = None
    t = align
    while t <= min(n, pref):
        if n % t == 0:
            best = t
        t += align
    return n if best is None else best


def _params():
    return pltpu.CompilerParams(vmem_limit_bytes=VMEM_LIMIT_BYTES)


def _call(body, args, dep=None, **kw):
    if dep is not None:
        n_in = len(args)
        inner = body

        def body(*refs):
            inner(*refs[:n_in], *refs[n_in + 1:])

        kw["in_specs"] = list(kw["in_specs"]) + [ANY]
        args = list(args) + [dep]
    return pl.pallas_call(body, **kw)(*args)


def _peers():
    x, y, c = lax.axis_index("x"), lax.axis_index("y"), lax.axis_index("c")
    me = 4 * x + 2 * y + c
    peers = []
    for k in range(1, N_DEV):
        px = 1 - x if (k >> 2) & 1 else x
        py = 1 - y if (k >> 1) & 1 else y
        pc = 1 - c if k & 1 else c
        peers.append((k, (px, py, pc), 4 * px + 2 * py + pc))
    return me, peers


HBM = pl.BlockSpec(memory_space=pltpu.HBM)
SEM = pl.BlockSpec(memory_space=pltpu.SEMAPHORE)
EFFECT = pltpu.SideEffectType.DATAFLOW_SIDE_EFFECTING


def _exchange_copy(gather, src_ref, land_ref, send_sems, recv_sems, me, k, peer, peer_flat, landing):
    return pltpu.make_async_remote_copy(
        src_ref=src_ref if gather else src_ref.at[peer_flat], dst_ref=land_ref.at[landing],
        send_sem=send_sems.at[k], recv_sem=recv_sems.at[k], device_id=peer, device_id_type=MESH)


ALL_PEERS = (1, 2, 3, 4, 5, 6, 7)
SIBLING = 1
SAME_CORE = (2, 4, 6)
FIRST_LEVEL = (SIBLING,) + SAME_CORE


def _exchange_start(srcs, gather, name, dep=None, ks=ALL_PEERS):
    n = len(srcs)
    extra = [] if dep is None else [dep]

    def body(*refs):
        src_refs, land_refs = refs[:n], refs[n:2 * n]
        refs = refs[2 * n + len(extra):]
        send_refs, recv_refs = refs[:n], refs[n:2 * n]
        token = refs[4 * n]
        me, peers = _peers()
        for i in range(n):
            for k, peer, peer_flat in peers:
                if k in ks:
                    _exchange_copy(gather, src_refs[i], land_refs[i], send_refs[i], recv_refs[i],
                                   me, k, peer, peer_flat, me).start()
        token[...] = jnp.zeros_like(token)

    lands = [lax.empty((N_DEV,) + s.shape[-2:], s.dtype) for s in srcs]
    sems = [pltpu.SemaphoreType.DMA((N_DEV,)) for _ in range(2 * n)]
    out = pl.pallas_call(
        body, name=name,
        out_shape=tuple(sems) + tuple(pltpu.HBM(a.shape, a.dtype) for a in list(srcs) + lands)
        + (jax.ShapeDtypeStruct((8, LANE), F32),),
        in_specs=[HBM] * (2 * n) + [ANY] * len(extra),
        out_specs=tuple([SEM] * (2 * n) + [HBM] * (2 * n) + [pl.BlockSpec(memory_space=pltpu.VMEM)]),
        input_output_aliases={i: 2 * n + i for i in range(2 * n)},
        compiler_params=pltpu.CompilerParams(has_side_effects=EFFECT),
    )(*[pltpu.with_memory_space_constraint(a, pltpu.HBM) for a in list(srcs) + lands], *extra)
    handles = [(out[2 * n + i], out[3 * n + i], out[i], out[n + i]) for i in range(n)]
    return handles, out[4 * n]


def _exchange_wait(handles, gather, after, name):
    n = len(handles)

    def body(*refs):
        src_refs, land_refs = refs[:n], refs[n:2 * n]
        send_refs, recv_refs = refs[2 * n:3 * n], refs[3 * n:4 * n]
        me, peers = _peers()
        for i in range(n):
            for k, peer, peer_flat in peers:
                cp = _exchange_copy(gather, src_refs[i], land_refs[i], send_refs[i], recv_refs[i],
                                    me, k, peer, peer_flat, peer_flat)
                cp.wait_send()
                cp.wait_recv()

    srcs = [h[0] for h in handles]
    lands = [h[1] for h in handles]
    out = pl.pallas_call(
        body, name=name,
        out_shape=tuple(pltpu.HBM(a.shape, a.dtype) for a in srcs + lands),
        in_specs=[HBM] * (2 * n) + [SEM] * (2 * n) + [ANY],
        out_specs=tuple([HBM] * (2 * n)),
        input_output_aliases={i: i for i in range(2 * n)},
        compiler_params=pltpu.CompilerParams(has_side_effects=EFFECT),
    )(*srcs, *lands, *[h[2] for h in handles], *[h[3] for h in handles], after)
    me = 4 * lax.axis_index("x") + 2 * lax.axis_index("y") + lax.axis_index("c")
    filled = []
    for src, land in zip(out[:n], out[n:]):
        own = src[None] if gather else lax.dynamic_slice_in_dim(src, me, 1, axis=0)
        filled.append(lax.dynamic_update_slice_in_dim(land, own, me, axis=0))
    return filled


def _gather_relay(handles, after, name):
    n = len(handles)

    def body(*refs):
        land_refs, recv_refs = refs[:n], refs[n:2 * n]
        refs = refs[2 * n + 1:]
        send2_refs, recv2_refs = refs[n:2 * n], refs[2 * n:3 * n]
        me, peers = _peers()
        sibling = peers[SIBLING - 1][1]
        for i in range(n):
            for k, peer, peer_flat in peers:
                if k in SAME_CORE:
                    block = land_refs[i].at[peer_flat]
                    pltpu.make_async_remote_copy(
                        src_ref=block, dst_ref=block, send_sem=send2_refs[i].at[k], recv_sem=recv_refs[i].at[k],
                        device_id=peer, device_id_type=MESH).wait_recv()
                    pltpu.make_async_remote_copy(
                        src_ref=block, dst_ref=block, send_sem=send2_refs[i].at[k], recv_sem=recv2_refs[i].at[k],
                        device_id=sibling, device_id_type=MESH).start()

    lands = [h[1] for h in handles]
    sems = [pltpu.SemaphoreType.DMA((N_DEV,)) for _ in range(2 * n)]
    out = pl.pallas_call(
        body, name=name,
        out_shape=tuple(pltpu.HBM(a.shape, a.dtype) for a in lands) + tuple(sems),
        in_specs=[HBM] * n + [SEM] * n + [ANY],
        out_specs=tuple([HBM] * n + [SEM] * (2 * n)),
        input_output_aliases={i: i for i in range(n)},
        compiler_params=pltpu.CompilerParams(has_side_effects=EFFECT),
    )(*lands, *[h[3] for h in handles], after)
    return [(h[0], out[i], h[2], h[3], out[n + i], out[2 * n + i]) for i, h in enumerate(handles)]


def _gather_wait(handles, after, name):
    n = len(handles)

    def body(*refs):
        src_refs, land_refs = refs[:n], refs[n:2 * n]
        send_refs, recv_refs = refs[2 * n:3 * n], refs[3 * n:4 * n]
        send2_refs, recv2_refs = refs[4 * n:5 * n], refs[5 * n:6 * n]
        me, peers = _peers()
        _, sibling, sibling_flat = peers[SIBLING - 1]
        for i in range(n):
            for k, peer, peer_flat in peers:
                if k in FIRST_LEVEL:
                    cp = _exchange_copy(True, src_refs[i], land_refs[i], send_refs[i], recv_refs[i],
                                        me, k, peer, peer_flat, peer_flat)
                    cp.wait_send()
                    if k == SIBLING:
                        cp.wait_recv()
                if k in SAME_CORE:
                    mine = land_refs[i].at[peer_flat]
                    theirs = land_refs[i].at[peer_flat ^ SIBLING]
                    cp = pltpu.make_async_remote_copy(
                        src_ref=mine, dst_ref=theirs, send_sem=send2_refs[i].at[k], recv_sem=recv2_refs[i].at[k],
                        device_id=sibling, device_id_type=MESH)
                    cp.wait_send()
                    cp.wait_recv()

    srcs = [h[0] for h in handles]
    lands = [h[1] for h in handles]
    out = pl.pallas_call(
        body, name=name,
        out_shape=tuple(pltpu.HBM(a.shape, a.dtype) for a in srcs + lands),
        in_specs=[HBM] * (2 * n) + [SEM] * (4 * n) + [ANY],
        out_specs=tuple([HBM] * (2 * n)),
        input_output_aliases={i: i for i in range(2 * n)},
        compiler_params=pltpu.CompilerParams(has_side_effects=EFFECT),
    )(*srcs, *lands, *[h[2] for h in handles], *[h[3] for h in handles],
      *[h[4] for h in handles], *[h[5] for h in handles], after)
    me = 4 * lax.axis_index("x") + 2 * lax.axis_index("y") + lax.axis_index("c")
    return [lax.dynamic_update_slice_in_dim(land, src[None], me, axis=0) for src, land in zip(out[:n], out[n:])]


def _allreduce_small(p):
    rows, d = p.shape

    def body(p_ref, o_ref, recv_ref, send_sems, recv_sems):
        me, peers = _peers()
        recv_ref[me] = p_ref[...]
        sends = []
        for k, peer, peer_flat in peers:
            cp = pltpu.make_async_remote_copy(
                src_ref=p_ref, dst_ref=recv_ref.at[me],
                send_sem=send_sems.at[k], recv_sem=recv_sems.at[k],
                device_id=peer, device_id_type=MESH)
            cp.start()
            sends.append(cp)
        for k, peer, peer_flat in peers:
            pltpu.make_async_remote_copy(
                src_ref=p_ref, dst_ref=recv_ref.at[peer_flat],
                send_sem=send_sems.at[k], recv_sem=recv_sems.at[k],
                device_id=peer, device_id_type=MESH).wait_recv()
        for cp in sends:
            cp.wait_send()
        acc = recv_ref[0]
        for s in range(1, N_DEV):
            acc = acc + recv_ref[s]
        is_loss = lax.broadcasted_iota(jnp.int32, (rows, d), 0) == rows - 1
        total = jnp.sum(jnp.where(is_loss, acc, 0.0))
        o_ref[...] = jnp.where(is_loss, total, acc)

    return pl.pallas_call(
        body, name="allreduce_small",
        out_shape=jax.ShapeDtypeStruct((rows, d), F32),
        in_specs=[pl.BlockSpec(memory_space=pltpu.VMEM)],
        out_specs=pl.BlockSpec(memory_space=pltpu.VMEM),
        scratch_shapes=[pltpu.VMEM((N_DEV, rows, d), F32),
                        pltpu.SemaphoreType.DMA((N_DEV,)), pltpu.SemaphoreType.DMA((N_DEV,))],
    )(p)


def _adam_math(w, g, m, v):
    m2 = ADAM_B1 * m + (1.0 - ADAM_B1) * g
    v2 = ADAM_B2 * v + (1.0 - ADAM_B2) * (g * g)
    m_hat = m2 / (1.0 - ADAM_B1 ** ADAM_STEP)
    v_hat = v2 / (1.0 - ADAM_B2 ** ADAM_STEP)
    delta = -ADAM_LR * (m_hat / (jnp.sqrt(v_hat) + ADAM_EPS) + ADAM_WD * w)
    return delta, m2, v2


def _adam_from_partials(parts, w, m, v, name):
    r, c = w.shape
    tr = _tile(r, 256, 16)

    def body(p_ref, w_ref, m_ref, v_ref, g_out, d_out, m_out, v_out):
        g = p_ref[0].astype(F32)
        for s in range(1, N_DEV):
            g = g + p_ref[s].astype(F32)
        delta, m2, v2 = _adam_math(w_ref[...], g, m_ref[...], v_ref[...])
        g_out[...] = g
        d_out[...] = delta
        m_out[...] = m2
        v_out[...] = v2

    blk = pl.BlockSpec((tr, c), lambda i: (i, 0))
    out = jax.ShapeDtypeStruct((r, c), F32)
    return pl.pallas_call(
        body, name=name, grid=(r // tr,),
        in_specs=[pl.BlockSpec((N_DEV, tr, c), lambda i: (0, i, 0)), blk, blk, blk],
        out_specs=[blk, blk, blk, blk], out_shape=[out, out, out, out],
        compiler_params=_params(),
    )(parts, w, m, v)


def _adam_small(g, w, m, v):
    def body(g_ref, w_ref, m_ref, v_ref, d_out, m_out, v_out):
        delta, m2, v2 = _adam_math(w_ref[...], g_ref[...], m_ref[...], v_ref[...])
        d_out[...] = delta
        m_out[...] = m2
        v_out[...] = v2

    out = jax.ShapeDtypeStruct(g.shape, F32)
    return pl.pallas_call(body, name="adam_small", out_shape=[out, out, out])(g, w, m, v)


def _rms_fwd(x, gain, name, dep=None, with_transpose=False):
    t, d = x.shape
    tr = _tile(t, 256, LANE)

    def body(x_ref, g_ref, o_ref, *ot_ref):
        xv = x_ref[...]
        r = lax.rsqrt(jnp.mean(xv * xv, axis=-1, keepdims=True) + NORM_EPS)
        y = xv * r * g_ref[...]
        o_ref[...] = y.astype(BF)
        if with_transpose:
            ot_ref[0][...] = jnp.transpose(y).astype(BF)

    out_specs = [pl.BlockSpec((tr, d), lambda i: (i, 0))]
    out_shape = [jax.ShapeDtypeStruct((t, d), BF)]
    if with_transpose:
        out_specs.append(pl.BlockSpec((d, tr), lambda i: (0, i)))
        out_shape.append(jax.ShapeDtypeStruct((d, t), BF))
    return _call(
        body, [x, gain], dep=dep, name=name, grid=(t // tr,),
        in_specs=[pl.BlockSpec((tr, d), lambda i: (i, 0)), pl.BlockSpec((1, d), lambda i: (0, 0))],
        out_specs=out_specs, out_shape=out_shape, compiler_params=_params(),
    )


def _rms_vjp(xv, gain, dy):
    r = lax.rsqrt(jnp.mean(xv * xv, axis=-1, keepdims=True) + NORM_EPS)
    xhat = xv * r
    dxhat = dy * gain
    dx = r * (dxhat - xhat * jnp.mean(dxhat * xhat, axis=-1, keepdims=True))
    dgain = jnp.sum(dy * xhat, axis=0, keepdims=True)
    return dx, dgain


def _loss_head(x, gain, target):
    t, d = x.shape
    tr = _tile(t, 256, 16)

    def body(x_ref, g_ref, t_ref, dx_ref, dxb_ref, dg_ref, loss_ref):
        xv = x_ref[...]
        gain = g_ref[...]
        r = lax.rsqrt(jnp.mean(xv * xv, axis=-1, keepdims=True) + NORM_EPS)
        err = xv * r * gain - t_ref[...]
        dx, dgain = _rms_vjp(xv, gain, err * (1.0 / d))
        dx_ref[...] = dx
        dxb_ref[...] = dx.astype(BF)

        @pl.when(pl.program_id(0) == 0)
        def _():
            dg_ref[...] = jnp.zeros_like(dg_ref)
            loss_ref[...] = jnp.zeros_like(loss_ref)

        dg_ref[...] += dgain
        loss_ref[...] += jnp.sum(err * err, axis=0, keepdims=True) * (0.5 / d)

    row = pl.BlockSpec((tr, d), lambda i: (i, 0))
    vec = pl.BlockSpec((1, d), lambda i: (0, 0))
    return pl.pallas_call(
        body, name="loss_head", grid=(t // tr,),
        in_specs=[row, vec, row], out_specs=[row, row, vec, vec],
        out_shape=[jax.ShapeDtypeStruct((t, d), F32), jax.ShapeDtypeStruct((t, d), BF),
                   jax.ShapeDtypeStruct((1, d), F32), jax.ShapeDtypeStruct((1, d), F32)],
        compiler_params=_params(),
    )(x, gain, target)


def _mm_nn(a, b, out_dtype, name, residual=None, tm_pref=512, tn_pref=1152):
    m, k = a.shape
    n = b.shape[1]
    tm, tn = _tile(m, tm_pref, 16), _tile(n, tn_pref, LANE)

    def body(*refs):
        if residual is None:
            a_ref, b_ref, o_ref = refs
            o_ref[...] = _dot(a_ref[...], b_ref[...]).astype(out_dtype)
        else:
            a_ref, b_ref, r_ref, o_ref = refs
            o_ref[...] = (r_ref[...] + _dot(a_ref[...], b_ref[...])).astype(out_dtype)

    in_specs = [pl.BlockSpec((tm, k), lambda j, i: (i, 0)), pl.BlockSpec((k, tn), lambda j, i: (0, j))]
    args = [a, b]
    if residual is not None:
        in_specs.append(pl.BlockSpec((tm, tn), lambda j, i: (i, j)))
        args.append(residual)
    return pl.pallas_call(
        body, name=name, grid=(n // tn, m // tm), in_specs=in_specs,
        out_specs=pl.BlockSpec((tm, tn), lambda j, i: (i, j)),
        out_shape=jax.ShapeDtypeStruct((m, n), out_dtype), compiler_params=_params(),
    )(*args)


def _rms_bwd_tail(dy_ref, first, x_ref, g_ref, dres_ref, dx_ref, dxb_ref, dg_ref):
    @pl.when(first)
    def _():
        dg_ref[...] = jnp.zeros_like(dg_ref)

    gain = g_ref[...]
    for r in range(0, dy_ref.shape[0], LANE):
        rows = pl.ds(r, min(LANE, dy_ref.shape[0] - r))
        dx, dgain = _rms_vjp(x_ref[rows, :], gain, dy_ref[rows, :])
        dx = dx + dres_ref[rows, :]
        dx_ref[rows, :] = dx
        dxb_ref[rows, :] = dx.astype(BF)
        dg_ref[...] += dgain


def _mm_nt(a, b, out_dtype, name, tm_pref=512, tn_pref=1024, tk_pref=2048, dep=None):
    m, k = a.shape
    n = b.shape[0]
    tm, tn, tk = _tile(m, tm_pref, 16), _tile(n, tn_pref, LANE), _tile(k, tk_pref, LANE)
    nk = k // tk

    def body(a_ref, b_ref, o_ref, acc_ref):
        kk = pl.program_id(2)

        @pl.when(kk == 0)
        def _():
            acc_ref[...] = jnp.zeros_like(acc_ref)

        acc_ref[...] += _dot(a_ref[...], b_ref[...], NT)

        @pl.when(kk == nk - 1)
        def _():
            o_ref[...] = acc_ref[...].astype(out_dtype)

    return _call(
        body, [a, b], dep=dep, name=name, grid=(n // tn, m // tm, nk),
        in_specs=[pl.BlockSpec((tm, tk), lambda j, i, kk: (i, kk)),
                  pl.BlockSpec((tn, tk), lambda j, i, kk: (j, kk))],
        out_specs=pl.BlockSpec((tm, tn), lambda j, i, kk: (i, j)),
        out_shape=jax.ShapeDtypeStruct((m, n), out_dtype),
        scratch_shapes=[pltpu.VMEM((tm, tn), F32)], compiler_params=_params(),
    )


def _mm_tn(a, b, out_dtype, name, tn_pref=1152, tk_pref=512, a_transposed=False):
    (k, t) = a.shape if a_transposed else a.shape[::-1]
    n = b.shape[1]
    tn, tk = _tile(n, tn_pref, LANE), _tile(t, tk_pref, LANE if a_transposed else 16)
    nt = t // tk

    def body(a_ref, b_ref, o_ref, acc_ref):
        tt = pl.program_id(1)

        @pl.when(tt == 0)
        def _():
            acc_ref[...] = jnp.zeros_like(acc_ref)

        acc_ref[...] += _dot(a_ref[...], b_ref[...], NN if a_transposed else TN)

        @pl.when(tt == nt - 1)
        def _():
            o_ref[...] = acc_ref[...].astype(out_dtype)

    if a_transposed:
        a_spec = pl.BlockSpec((k, tk), lambda j, tt: (0, tt))
    else:
        a_spec = pl.BlockSpec((tk, k), lambda j, tt: (tt, 0))
    return pl.pallas_call(
        body, name=name, grid=(n // tn, nt),
        in_specs=[a_spec, pl.BlockSpec((tk, tn), lambda j, tt: (tt, j))],
        out_specs=pl.BlockSpec((k, tn), lambda j, tt: (0, j)),
        out_shape=jax.ShapeDtypeStruct((k, n), out_dtype),
        scratch_shapes=[pltpu.VMEM((k, tn), F32)], compiler_params=_params(),
    )(a, b)


FFN_COLS = 512


FFN_ROWS_WIDE = 2048


def _ffn_tiles(t, fc, rows=FFN_ROWS):
    return _tile(t, rows, 16), _tile(fc, FFN_COLS, LANE)


def _slabs(tm, rows=256):
    step = rows if tm % rows == 0 else tm
    return [pl.ds(r, step) for r in range(0, tm, step)]


def _ffn_gate_up(hn, wg_t, wu_t, name):
    t, d = hn.shape
    fc = wg_t.shape[0]
    tm, tn = _ffn_tiles(t, fc, FFN_ROWS_WIDE)

    def body(h_ref, wg_ref, wu_ref, g_ref, u_ref, a_ref):
        for rows in _slabs(tm):
            h = h_ref[rows, :]
            g = _dot(h, wg_ref[...], NT)
            u = _dot(h, wu_ref[...], NT)
            g_ref[rows, :] = g.astype(BF)
            u_ref[rows, :] = u.astype(BF)
            a_ref[rows, :] = (g * _sig(g) * u).astype(BF)

    wspec = pl.BlockSpec((tn, d), lambda j, i: (j, 0))
    hid = pl.BlockSpec((tm, tn), lambda j, i: (i, j))
    out = jax.ShapeDtypeStruct((t, fc), BF)
    return pl.pallas_call(
        body, name=name, grid=(fc // tn, t // tm),
        in_specs=[pl.BlockSpec((tm, d), lambda j, i: (i, 0)), wspec, wspec],
        out_specs=[hid, hid, hid], out_shape=[out, out, out], compiler_params=_params(),
    )(hn, wg_t, wu_t)


def _ffn_gate(hn, wg_t, name):
    t, d = hn.shape
    fc = wg_t.shape[0]
    tm, tn = _ffn_tiles(t, fc, FFN_ROWS_WIDE)

    def body(h_ref, wg_ref, g_ref):
        g_ref[...] = _dot(h_ref[...], wg_ref[...], NT)

    return pl.pallas_call(
        body, name=name, grid=(fc // tn, t // tm),
        in_specs=[pl.BlockSpec((tm, d), lambda j, i: (i, 0)), pl.BlockSpec((tn, d), lambda j, i: (j, 0))],
        out_specs=pl.BlockSpec((tm, tn), lambda j, i: (i, j)),
        out_shape=jax.ShapeDtypeStruct((t, fc), F32), compiler_params=_params(),
    )(hn, wg_t)


def _ffn_up_act(hn, wu_t, g, name):
    t, d = hn.shape
    fc = wu_t.shape[0]
    tm, tn = _ffn_tiles(t, fc, FFN_ROWS_WIDE)

    def body(h_ref, wu_ref, g_ref, gb_ref, u_ref, a_ref):
        for rows in _slabs(tm):
            u = _dot(h_ref[rows, :], wu_ref[...], NT)
            gv = g_ref[rows, :]
            gb_ref[rows, :] = gv.astype(BF)
            u_ref[rows, :] = u.astype(BF)
            a_ref[rows, :] = (gv * _sig(gv) * u).astype(BF)

    hid = pl.BlockSpec((tm, tn), lambda j, i: (i, j))
    out = jax.ShapeDtypeStruct((t, fc), BF)
    return pl.pallas_call(
        body, name=name, grid=(fc // tn, t // tm),
        in_specs=[pl.BlockSpec((tm, d), lambda j, i: (i, 0)), pl.BlockSpec((tn, d), lambda j, i: (j, 0)), hid],
        out_specs=[hid, hid, hid], out_shape=[out, out, out], compiler_params=_params(),
    )(hn, wu_t, g)


def _ffn_down(act, wd, xres, name):
    t, fc = act.shape
    d = wd.shape[1]
    tm, tk = _ffn_tiles(t, fc)

    def body(a_ref, w_ref, x_ref, o_ref):
        @pl.when(pl.program_id(1) == 0)
        def _():
            o_ref[...] = x_ref[...]

        o_ref[...] += 0.5 * _dot(a_ref[...], w_ref[...])

    row = pl.BlockSpec((tm, d), lambda i, k: (i, 0))
    return pl.pallas_call(
        body, name=name, grid=(t // tm, fc // tk),
        in_specs=[pl.BlockSpec((tm, tk), lambda i, k: (i, k)), pl.BlockSpec((tk, d), lambda i, k: (k, 0)), row],
        out_specs=row, out_shape=jax.ShapeDtypeStruct((t, d), F32), compiler_params=_params(),
    )(act, wd, xres)


def _ffn_bwd_hidden(dxb, wd, g, u, name):
    t, d = dxb.shape
    fc = wd.shape[0]
    tm, tn = _ffn_tiles(t, fc, FFN_ROWS_WIDE)

    def body(dx_ref, w_ref, g_ref, u_ref, dg_ref, du_ref):
        for rows in _slabs(tm):
            dh = 0.5 * _dot(dx_ref[rows, :], w_ref[...], NT)
            gv = g_ref[rows, :].astype(F32)
            uv = u_ref[rows, :].astype(F32)
            s = _sig(gv)
            dg_ref[rows, :] = (dh * uv * (s * (1.0 + gv * (1.0 - s)))).astype(BF)
            du_ref[rows, :] = (dh * (gv * s)).astype(BF)

    hid = pl.BlockSpec((tm, tn), lambda i, j: (i, j))
    out = jax.ShapeDtypeStruct((t, fc), BF)
    return pl.pallas_call(
        body, name=name, grid=(t // tm, fc // tn),
        in_specs=[pl.BlockSpec((tm, d), lambda i, j: (i, 0)), pl.BlockSpec((tn, d), lambda i, j: (j, 0)), hid, hid],
        out_specs=[hid, hid], out_shape=[out, out], compiler_params=_params(),
    )(dxb, wd, g, u)


def _ffn_dw(lhs, rhs, scale, name, dep=None):
    n = len(lhs)
    t, fc = lhs[0].shape
    d = rhs.shape[1]
    tk, tn = _tile(t, DW_ROWS, 16), _tile(fc, FFN_COLS, LANE)
    nt = t // tk

    def body(*refs):
        l_refs, r_ref, o_refs, acc_refs = refs[:n], refs[n], refs[n + 1:2 * n + 1], refs[2 * n + 1:]
        tt = pl.program_id(1)
        r = r_ref[...]
        for l_ref, o_ref, acc_ref in zip(l_refs, o_refs, acc_refs):
            @pl.when(tt == 0)
            def _():
                acc_ref[...] = jnp.zeros_like(acc_ref)

            acc_ref[...] += _dot(l_ref[...], r, TN)

            @pl.when(tt == nt - 1)
            def _():
                o_ref[...] = (scale * acc_ref[...]).astype(BF)

    lspec = pl.BlockSpec((tk, tn), lambda j, tt: (tt, j))
    ospec = pl.BlockSpec((tn, d), lambda j, tt: (j, 0))
    out = jax.ShapeDtypeStruct((fc, d), BF)
    return _call(
        body, [*lhs, rhs], dep=dep, name=name, grid=(fc // tn, nt),
        in_specs=[lspec] * n + [pl.BlockSpec((tk, d), lambda j, tt: (tt, 0))],
        out_specs=[ospec] * n, out_shape=[out] * n,
        scratch_shapes=[pltpu.VMEM((tn, d), F32)] * n, compiler_params=_params(),
    )


def _rms_bwd(dy, x, gain, dres, name):
    t, d = x.shape
    tr = _tile(t, 256, 16)

    def body(dy_ref, x_ref, g_ref, dres_ref, dx_ref, dxb_ref, dg_ref):
        _rms_bwd_tail(dy_ref, pl.program_id(0) == 0, x_ref, g_ref, dres_ref, dx_ref, dxb_ref, dg_ref)

    row = pl.BlockSpec((tr, d), lambda i: (i, 0))
    vec = pl.BlockSpec((1, d), lambda i: (0, 0))
    return pl.pallas_call(
        body, name=name, grid=(t // tr,),
        in_specs=[row, row, vec, row], out_specs=[row, row, vec],
        out_shape=[jax.ShapeDtypeStruct((t, d), F32), jax.ShapeDtypeStruct((t, d), BF),
                   jax.ShapeDtypeStruct((1, d), F32)],
        compiler_params=_params(),
    )(dy, x, gain, dres)


def _ffn_bwd_input(dg, du, wg_t, wu_t, name, dep=None):
    t, fc = dg.shape
    d = wg_t.shape[1]
    tm, tk = _ffn_tiles(t, fc)

    def body(dg_ref, du_ref, wg_ref, wu_ref, o_ref):
        @pl.when(pl.program_id(1) == 0)
        def _():
            o_ref[...] = jnp.zeros_like(o_ref)

        o_ref[...] += _dot(dg_ref[...], wg_ref[...]) + _dot(du_ref[...], wu_ref[...])

    hid = pl.BlockSpec((tm, tk), lambda i, k: (i, k))
    wspec = pl.BlockSpec((tk, d), lambda i, k: (k, 0))
    return _call(
        body, [dg, du, wg_t, wu_t], dep=dep, name=name, grid=(t // tm, fc // tk),
        in_specs=[hid, hid, wspec, wspec],
        out_specs=pl.BlockSpec((tm, d), lambda i, k: (i, 0)),
        out_shape=jax.ShapeDtypeStruct((t, d), F32), compiler_params=_params(),
    )


def _rope_tables(t):
    pos = jnp.arange(t, dtype=F32)
    inv_freq = ROPE_THETA ** (-jnp.arange(0, ROPE_DIM, 2, dtype=F32) / ROPE_DIM)
    ang = pos[:, None] * inv_freq[None, :]
    cos, sin = jnp.cos(ang), jnp.sin(ang)
    rest = HEAD_DIM - ROPE_DIM
    one = jnp.ones((t, rest), F32)
    zero_h = jnp.zeros((t, ROPE_HALF), F32)
    zero_r = jnp.zeros((t, rest), F32)
    c = jnp.concatenate([cos, cos, one], axis=1)
    s1 = jnp.concatenate([-sin, zero_h, zero_r], axis=1)
    s2 = jnp.concatenate([zero_h, sin, zero_r], axis=1)
    return c, s1, s2


def _rope(xh, c, s1, s2):
    return xh * c + pltpu.roll(xh, HEAD_DIM - ROPE_HALF, 1) * s1 + pltpu.roll(xh, ROPE_HALF, 1) * s2


def _rope_t(dh, c, s1, s2):
    return dh * c + pltpu.roll(dh * s1, ROPE_HALF, 1) + pltpu.roll(dh * s2, HEAD_DIM - ROPE_HALF, 1)


def _mixer_prep(proj, tables, bf_pad, hd, scale):
    t, np_ = proj.shape
    tr = _tile(t, 256, 16)
    nh = hd // HEAD_DIM
    nblk = hd // LANE
    f_blk = (np_ - F_PAD) // LANE

    def body(qd_ref, kd_ref, vd_ref, qf_ref, kf_ref, vf_ref, fl_ref, c_ref, s1_ref, s2_ref, b_ref,
             oqd, okd, ovd, oqf, okf, ovf, olog):
        c, s1, s2 = c_ref[...], s1_ref[...], s2_ref[...]
        for h in range(nh):
            sl = slice(h * HEAD_DIM, (h + 1) * HEAD_DIM)
            oqd[:, sl] = (_rope(qd_ref[:, sl], c, s1, s2) * scale).astype(BF)
            okd[:, sl] = _rope(kd_ref[:, sl], c, s1, s2).astype(BF)
        ovd[...] = vd_ref[...].astype(BF)
        oqf[...] = (qf_ref[...] * scale).astype(BF)
        okf[...] = kf_ref[...].astype(BF)
        ovf[...] = vf_ref[...].astype(BF)
        z = fl_ref[...] + b_ref[...]
        olog[...] = jnp.minimum(z, 0.0) - jnp.log(1.0 + jnp.exp(-jnp.abs(z)))

    def col(kblk):
        return pl.BlockSpec((tr, hd), lambda i, kblk=kblk: (i, kblk))

    lane_row = pl.BlockSpec((tr, LANE), lambda i: (i, 0))
    in_specs = [col(0), col(1), col(2), col(3), col(4), col(5),
                pl.BlockSpec((tr, LANE), lambda i: (i, f_blk)),
                lane_row, lane_row, lane_row, pl.BlockSpec((1, LANE), lambda i: (0, 0))]
    o = pl.BlockSpec((tr, hd), lambda i: (i, 0))
    ob = jax.ShapeDtypeStruct((t, hd), BF)
    del nblk
    return pl.pallas_call(
        body, name="mixer_prep", grid=(t // tr,), in_specs=in_specs,
        out_specs=[o, o, o, o, o, o, lane_row],
        out_shape=[ob, ob, ob, ob, ob, ob, jax.ShapeDtypeStruct((t, LANE), F32)],
        compiler_params=_params(),
    )(proj, proj, proj, proj, proj, proj, proj, *tables, bf_pad)


def _split3(x):
    x1 = x.astype(BF)
    r1 = x - x1.astype(F32)
    x2 = r1.astype(BF)
    x3 = (r1 - x2.astype(F32)).astype(BF)
    return x1, x2, x3


def _cumsum_rows(x, reverse, name):
    t, w = x.shape
    blk = LANE
    nb = t // blk

    def body(x_ref, o_ref):
        r = lax.broadcasted_iota(jnp.int32, (blk, blk), 0)
        c = lax.broadcasted_iota(jnp.int32, (blk, blk), 1)
        tri = jnp.where((c >= r) if reverse else (c <= r), 1.0, 0.0).astype(BF)

        def step(i, carry):
            b = (nb - 1 - i) if reverse else i
            off = pl.multiple_of(b * blk, blk)
            xb = x_ref[pl.ds(off, blk), :]
            x1, x2, x3 = _split3(xb)
            o_ref[pl.ds(off, blk), :] = _dot(tri, x1) + _dot(tri, x2) + _dot(tri, x3) + carry
            return carry + jnp.sum(xb, axis=0, keepdims=True)

        lax.fori_loop(0, nb, step, jnp.zeros((1, w), F32))

    return pl.pallas_call(body, name=name, out_shape=jax.ShapeDtypeStruct((t, w), F32),
                          compiler_params=_params())(x)


ATTN_ROWS = 16


def _dil_bias_tiles(tq):
    nbias = MAX_WINDOW // tq + 1
    b = lax.broadcasted_iota(jnp.int32, (nbias, tq, tq), 0)
    i = lax.broadcasted_iota(jnp.int32, (nbias, tq, tq), 1)
    j = lax.broadcasted_iota(jnp.int32, (nbias, tq, tq), 2)
    delta = b * tq + i - j
    mult = jnp.zeros((nbias, tq, tq), F32)
    for w, dil in DIL_PATTERNS:
        mult = mult + jnp.where((delta >= 0) & (delta <= w) & (delta % dil == 0), 1.0, 0.0)
    return jnp.where(mult > 0.0, jnp.log(jnp.maximum(mult, 1.0)), NEG)


def _rep(x, width):
    return jnp.tile(x, (1, width // LANE))


def _chunks(n_rows, fn):
    for c in range(n_rows // ATTN_ROWS):
        fn(c * ATTN_ROWS)


def _causal(r0, tq, transposed):
    a = lax.broadcasted_iota(jnp.int32, (ATTN_ROWS, tq), 0) + r0
    b = lax.broadcasted_iota(jnp.int32, (ATTN_ROWS, tq), 1)
    return (a <= b) if transposed else (b <= a)


def _rows8(x):
    return jnp.transpose(x)[:8, :]


def _attn_fwd(mode, q, k, v, bias, tq, name):
    t, hd = q.shape
    nh = hd // HEAD_DIM
    nb = t // tq
    wb = MAX_WINDOW // tq
    fox = mode == "fox"

    def body(q_ref, k_ref, v_ref, b_ref, o_ref, lse_ref, lse_row_ref, s_ref, p_ref, m_ref, l_ref, acc_ref):
        qi = pl.program_id(1)
        qb = q_ref[...]
        m_ref[...] = jnp.full_like(m_ref, NEG)
        l_ref[...] = jnp.zeros_like(l_ref)
        acc_ref[...] = jnp.zeros_like(acc_ref)

        def tile(kj, diag):
            off = pl.multiple_of(kj * tq, tq)
            s_ref[...] = _dot(qb, k_ref[pl.ds(off, tq), :], NT)
            if fox:
                brow = b_ref[qi][:, :1] - b_ref[kj]

            def chunk(r0):
                rows = pl.ds(r0, ATTN_ROWS)
                if fox:
                    s = s_ref[rows, :] + brow
                    if diag:
                        s = jnp.where(_causal(r0, tq, False), s, NEG)
                else:
                    s = s_ref[rows, :] + b_ref[qi - kj, rows, :]
                m_old = m_ref[rows, :]
                m_new = jnp.maximum(m_old, jnp.max(s, axis=1, keepdims=True))
                p = jnp.exp(s - _rep(m_new, tq))
                alpha = jnp.exp(m_old - m_new)
                l_ref[rows, :] = alpha * l_ref[rows, :] + jnp.sum(p, axis=1, keepdims=True)
                m_ref[rows, :] = m_new
                acc_ref[rows, :] = alpha * acc_ref[rows, :]
                p_ref[rows, :] = p.astype(BF)

            _chunks(tq, chunk)
            acc_ref[...] += _dot(p_ref[...], v_ref[pl.ds(off, tq), :])

        tile(qi, True)
        if fox:
            lax.fori_loop(0, qi, lambda kj, c: (tile(kj, False), c)[1], 0)
        else:
            lax.fori_loop(1, jnp.minimum(qi, wb) + 1, lambda i, c: (tile(qi - i, False), c)[1], 0)
        o_ref[...] = (acc_ref[...] / l_ref[...]).astype(BF)
        lse = m_ref[...] + jnp.log(l_ref[...])
        lse_ref[...] = lse
        lse_row_ref[...] = _rows8(lse)

    qspec = pl.BlockSpec((tq, HEAD_DIM), lambda h, i: (i, h))
    kvspec = pl.BlockSpec((t, HEAD_DIM), lambda h, i: (0, h))
    repspec = pl.BlockSpec((None, tq, LANE), lambda h, i: (h, i, 0))
    row8spec = pl.BlockSpec((None, None, 8, tq), lambda h, i: (h, i, 0, 0))
    if fox:
        bspec = pl.BlockSpec((None, nb, 1, tq), lambda h, i: (h, 0, 0, 0))
    else:
        bspec = pl.BlockSpec((wb + 1, tq, tq), lambda h, i: (0, 0, 0))
    return pl.pallas_call(
        body, name=name, grid=(nh, nb), in_specs=[qspec, kvspec, kvspec, bspec],
        out_specs=[qspec, repspec, row8spec],
        out_shape=[jax.ShapeDtypeStruct((t, hd), BF), jax.ShapeDtypeStruct((nh, t, LANE), F32),
                   jax.ShapeDtypeStruct((nh, nb, 8, tq), F32)],
        scratch_shapes=[pltpu.VMEM((tq, tq), F32), pltpu.VMEM((tq, tq), BF), pltpu.VMEM((tq, LANE), F32),
                        pltpu.VMEM((tq, LANE), F32), pltpu.VMEM((tq, HEAD_DIM), F32)],
        compiler_params=_params(),
    )(q, k, v, bias)


def _attn_bwd_dq(mode, q, k, v, o, do, lse, bias, tq, name, dep=None):
    t, hd = q.shape
    nh = hd // HEAD_DIM
    nb = t // tq
    wb = MAX_WINDOW // tq
    fox = mode == "fox"

    def body(q_ref, k_ref, v_ref, o_ref, do_ref, lse_ref, b_ref, dq_ref, dl_row_ref,
             s_ref, dp_ref, x_ref, y_ref, acc_ref, acc2_ref, dl_ref):
        qi = pl.program_id(1)
        qb = q_ref[...]
        dob = do_ref[...]
        acc_ref[...] = jnp.zeros_like(acc_ref)
        if fox:
            acc2_ref[...] = jnp.zeros_like(acc2_ref)
            dl_ref[...] = jnp.zeros_like(dl_ref)
        else:
            prod = o_ref[...].astype(F32) * dob.astype(F32)
            dl_ref[...] = jnp.broadcast_to(jnp.sum(prod, axis=1, keepdims=True), (tq, LANE))

        def tile(kj, diag):
            off = pl.multiple_of(kj * tq, tq)
            kb = k_ref[pl.ds(off, tq), :]
            s_ref[...] = _dot(qb, kb, NT)
            dp_ref[...] = _dot(dob, v_ref[pl.ds(off, tq), :], NT)
            if fox:
                brow = b_ref[qi][:, :1] - b_ref[kj]

            def chunk(r0):
                rows = pl.ds(r0, ATTN_ROWS)
                lse_c = _rep(lse_ref[rows, :], tq)
                if fox:
                    s = s_ref[rows, :] + brow
                    if diag:
                        s = jnp.where(_causal(r0, tq, False), s, NEG)
                    p = jnp.exp(s - lse_c)
                    pdp = p * dp_ref[rows, :]
                    dl_ref[rows, :] += jnp.sum(pdp, axis=1, keepdims=True)
                    x_ref[rows, :] = pdp.astype(BF)
                    y_ref[rows, :] = p.astype(BF)
                else:
                    p = jnp.exp(s_ref[rows, :] + b_ref[qi - kj, rows, :] - lse_c)
                    x_ref[rows, :] = (p * (dp_ref[rows, :] - _rep(dl_ref[rows, :], tq))).astype(BF)

            _chunks(tq, chunk)
            acc_ref[...] += _dot(x_ref[...], kb)
            if fox:
                acc2_ref[...] += _dot(y_ref[...], kb)

        tile(qi, True)
        if fox:
            lax.fori_loop(0, qi, lambda kj, c: (tile(kj, False), c)[1], 0)
            dq_ref[...] = acc_ref[...] - dl_ref[...] * acc2_ref[...]
        else:
            lax.fori_loop(1, jnp.minimum(qi, wb) + 1, lambda i, c: (tile(qi - i, False), c)[1], 0)
            dq_ref[...] = acc_ref[...]
        dl_row_ref[...] = _rows8(dl_ref[...])

    qspec = pl.BlockSpec((tq, HEAD_DIM), lambda h, i: (i, h))
    kvspec = pl.BlockSpec((t, HEAD_DIM), lambda h, i: (0, h))
    repspec = pl.BlockSpec((None, tq, LANE), lambda h, i: (h, i, 0))
    row8spec = pl.BlockSpec((None, None, 8, tq), lambda h, i: (h, i, 0, 0))
    if fox:
        bspec = pl.BlockSpec((None, nb, 1, tq), lambda h, i: (h, 0, 0, 0))
    else:
        bspec = pl.BlockSpec((wb + 1, tq, tq), lambda h, i: (0, 0, 0))
    return _call(
        body, [q, k, v, o, do, lse, bias], dep=dep, name=name, grid=(nh, nb),
        in_specs=[qspec, kvspec, kvspec, qspec, qspec, repspec, bspec],
        out_specs=[qspec, row8spec],
        out_shape=[jax.ShapeDtypeStruct((t, hd), F32), jax.ShapeDtypeStruct((nh, nb, 8, tq), F32)],
        scratch_shapes=[pltpu.VMEM((tq, tq), F32), pltpu.VMEM((tq, tq), F32), pltpu.VMEM((tq, tq), BF),
                        pltpu.VMEM((tq, tq), BF), pltpu.VMEM((tq, HEAD_DIM), F32),
                        pltpu.VMEM((tq, HEAD_DIM), F32), pltpu.VMEM((tq, LANE), F32)],
        compiler_params=_params(),
    )


def _attn_bwd_dkv(mode, q, k, v, do, lse_row, dl_row, bias_t, c_row, tq, name):
    t, hd = q.shape
    nh = hd // HEAD_DIM
    nb = t // tq
    wb = MAX_WINDOW // tq
    fox = mode == "fox"

    def body(*refs):
        if fox:
            (q_ref, k_ref, v_ref, do_ref, lse_ref, dl_ref, b_ref, cq_ref, dk_ref, dv_ref, dc_row_ref,
             s_ref, dp_ref, x_ref, y_ref, dc_ref) = refs
        else:
            q_ref, k_ref, v_ref, do_ref, lse_ref, dl_ref, b_ref, dk_ref, dv_ref, s_ref, dp_ref, x_ref, y_ref = refs
        kj = pl.program_id(1)
        kb = k_ref[...]
        vb = v_ref[...]
        dk_ref[...] = jnp.zeros_like(dk_ref)
        dv_ref[...] = jnp.zeros_like(dv_ref)
        if fox:
            dc_ref[...] = jnp.zeros_like(dc_ref)

        def tile(qi, diag):
            off = pl.multiple_of(qi * tq, tq)
            qb = q_ref[pl.ds(off, tq), :]
            dob = do_ref[pl.ds(off, tq), :]
            s_ref[...] = _dot(kb, qb, NT)
            dp_ref[...] = _dot(vb, dob, NT)
            lse_r = lse_ref[qi, 0:1, :]
            dl_r = dl_ref[qi, 0:1, :]
            if fox:
                kbias = cq_ref[qi][:, :1] - b_ref[...]

            def chunk(r0):
                rows = pl.ds(r0, ATTN_ROWS)
                if fox:
                    s = s_ref[rows, :] + _rep(kbias[r0:r0 + ATTN_ROWS, :], tq)
                    if diag:
                        s = jnp.where(_causal(r0, tq, True), s, NEG)
                else:
                    s = s_ref[rows, :] + b_ref[qi - kj, rows, :]
                pt = jnp.exp(s - lse_r)
                dst = pt * (dp_ref[rows, :] - dl_r)
                x_ref[rows, :] = pt.astype(BF)
                y_ref[rows, :] = dst.astype(BF)
                if fox:
                    dc_ref[rows, :] -= jnp.sum(dst, axis=1, keepdims=True)

            _chunks(tq, chunk)
            dv_ref[...] += _dot(x_ref[...], dob)
            dk_ref[...] += _dot(y_ref[...], qb)

        tile(kj, True)
        hi = nb if fox else jnp.minimum(kj + wb + 1, nb)
        lax.fori_loop(kj + 1, hi, lambda qi, c: (tile(qi, False), c)[1], 0)
        if fox:
            dc_row_ref[...] = _rows8(dc_ref[...])

    blkspec = pl.BlockSpec((tq, HEAD_DIM), lambda h, j: (j, h))
    fullspec = pl.BlockSpec((t, HEAD_DIM), lambda h, j: (0, h))
    rows8spec = pl.BlockSpec((None, nb, 8, tq), lambda h, j: (h, 0, 0, 0))
    repspec = pl.BlockSpec((None, tq, LANE), lambda h, j: (h, j, 0))
    in_specs = [fullspec, blkspec, blkspec, fullspec, rows8spec, rows8spec]
    args = [q, k, v, do, lse_row, dl_row, bias_t]
    out_specs = [blkspec, blkspec]
    out_shape = [jax.ShapeDtypeStruct((t, hd), F32), jax.ShapeDtypeStruct((t, hd), F32)]
    scratch = [pltpu.VMEM((tq, tq), F32), pltpu.VMEM((tq, tq), F32), pltpu.VMEM((tq, tq), BF),
               pltpu.VMEM((tq, tq), BF)]
    if fox:
        in_specs += [repspec, pl.BlockSpec((None, nb, 1, tq), lambda h, j: (h, 0, 0, 0))]
        args.append(c_row)
        out_specs.append(pl.BlockSpec((None, None, 8, tq), lambda h, j: (h, j, 0, 0)))
        out_shape.append(jax.ShapeDtypeStruct((nh, nb, 8, tq), F32))
        scratch.append(pltpu.VMEM((tq, LANE), F32))
    else:
        in_specs.append(pl.BlockSpec((wb + 1, tq, tq), lambda h, j: (0, 0, 0)))
    return pl.pallas_call(
        body, name=name, grid=(nh, nb), in_specs=in_specs, out_specs=out_specs, out_shape=out_shape,
        scratch_shapes=scratch, compiler_params=_params(),
    )(*args)


def _gate_specs(t, d, hd, tr):
    row = pl.BlockSpec((tr, d), lambda i: (i, 0))
    vec = pl.BlockSpec((1, d), lambda i: (0, 0))
    base = 6 * hd // d
    gd = pl.BlockSpec((tr, d), lambda i: (i, base))
    gf = pl.BlockSpec((tr, d), lambda i: (i, base + 1))
    return row, vec, gd, gf


def _proj_merge(yd, yf, wpd, wpf, proj, b_d, b_f, hd):
    t = yd.shape[0]
    d = wpd.shape[1]
    tr = _tile(t, 256, 16)
    row, vec, gd, gf = _gate_specs(t, d, hd, tr)

    def body(yd_ref, yf_ref, wd_ref, wf_ref, gd_ref, gf_ref, bd_ref, bf_ref, pd_ref, pf_ref, o_ref):
        pd = _dot(yd_ref[...], wd_ref[...])
        pf = _dot(yf_ref[...], wf_ref[...])
        pd_ref[...] = pd
        pf_ref[...] = pf
        o_ref[...] = (_sig(gd_ref[...] + bd_ref[...]) * pd + _sig(gf_ref[...] + bf_ref[...]) * pf).astype(BF)

    yspec = pl.BlockSpec((tr, hd), lambda i: (i, 0))
    wspec = pl.BlockSpec((hd, d), lambda i: (0, 0))
    f32 = jax.ShapeDtypeStruct((t, d), F32)
    return pl.pallas_call(
        body, name="proj_merge", grid=(t // tr,), in_specs=[yspec, yspec, wspec, wspec, gd, gf, vec, vec],
        out_specs=[row, row, row], out_shape=[f32, f32, jax.ShapeDtypeStruct((t, d), BF)],
        compiler_params=_params(),
    )(yd, yf, wpd, wpf, proj, proj, b_d, b_f)


def _merge_bwd(dm, pd, pf, proj, b_d, b_f, hd):
    t, d = pd.shape
    tr = _tile(t, 256, 16)
    row, vec, gd, gf = _gate_specs(t, d, hd, tr)

    def body(dm_ref, pd_ref, pf_ref, gd_ref, gf_ref, bd_ref, bf_ref,
             dpd_ref, dpf_ref, dgd_ref, dgf_ref, dbd_ref, dbf_ref):
        dmv = dm_ref[...]
        sd = _sig(gd_ref[...] + bd_ref[...])
        sf = _sig(gf_ref[...] + bf_ref[...])
        dgd = dmv * pd_ref[...] * (sd * (1.0 - sd))
        dgf = dmv * pf_ref[...] * (sf * (1.0 - sf))
        dpd_ref[...] = (dmv * sd).astype(BF)
        dpf_ref[...] = (dmv * sf).astype(BF)
        dgd_ref[...] = dgd.astype(BF)
        dgf_ref[...] = dgf.astype(BF)

        @pl.when(pl.program_id(0) == 0)
        def _():
            dbd_ref[...] = jnp.zeros_like(dbd_ref)
            dbf_ref[...] = jnp.zeros_like(dbf_ref)

        dbd_ref[...] += jnp.sum(dgd, axis=0, keepdims=True)
        dbf_ref[...] += jnp.sum(dgf, axis=0, keepdims=True)

    ob = jax.ShapeDtypeStruct((t, d), BF)
    ov = jax.ShapeDtypeStruct((1, d), F32)
    return pl.pallas_call(
        body, name="merge_bwd", grid=(t // tr,), in_specs=[row, row, row, gd, gf, vec, vec],
        out_specs=[row, row, row, row, vec, vec], out_shape=[ob, ob, ob, ob, ov, ov],
        compiler_params=_params(),
    )(dm, pd, pf, proj, proj, b_d, b_f)


def _assemble_dproj(dqd, dkd, dvd, dqf, dkf, dvf, dgd, dgf, dlogf, proj, tables, bf_pad, scale):
    t, np_ = proj.shape
    hd = dqd.shape[1]
    d = dgd.shape[1]
    nh = hd // HEAD_DIM
    tr = _tile(t, 256, 16)
    f_blk = (np_ - F_PAD) // LANE

    def body(dqd_ref, dkd_ref, dvd_ref, dqf_ref, dkf_ref, dvf_ref, dgd_ref, dgf_ref, dlog_ref, fl_ref,
             c_ref, s1_ref, s2_ref, b_ref, o_ref, db_ref):
        c, s1, s2 = c_ref[...], s1_ref[...], s2_ref[...]
        for h in range(nh):
            sl = slice(h * HEAD_DIM, (h + 1) * HEAD_DIM)
            o_ref[:, sl] = (_rope_t(dqd_ref[:, sl], c, s1, s2) * scale).astype(BF)
            o_ref[:, hd + h * HEAD_DIM:hd + (h + 1) * HEAD_DIM] = _rope_t(dkd_ref[:, sl], c, s1, s2).astype(BF)
        o_ref[:, 2 * hd:3 * hd] = dvd_ref[...].astype(BF)
        o_ref[:, 3 * hd:4 * hd] = (dqf_ref[...] * scale).astype(BF)
        o_ref[:, 4 * hd:5 * hd] = dkf_ref[...].astype(BF)
        o_ref[:, 5 * hd:6 * hd] = dvf_ref[...].astype(BF)
        o_ref[:, 6 * hd:6 * hd + d] = dgd_ref[...]
        o_ref[:, 6 * hd + d:6 * hd + 2 * d] = dgf_ref[...]
        z = fl_ref[...] + b_ref[...]
        dfl = dlog_ref[...] * _sig(-z)
        o_ref[:, 6 * hd + 2 * d:6 * hd + 2 * d + LANE] = dfl.astype(BF)
        o_ref[:, 6 * hd + 2 * d + LANE:] = jnp.zeros((tr, F_PAD - LANE), BF)

        @pl.when(pl.program_id(0) == 0)
        def _():
            db_ref[...] = jnp.zeros_like(db_ref)

        db_ref[...] += jnp.sum(dfl, axis=0, keepdims=True)

    head = pl.BlockSpec((tr, hd), lambda i: (i, 0))
    row = pl.BlockSpec((tr, d), lambda i: (i, 0))
    lane_row = pl.BlockSpec((tr, LANE), lambda i: (i, 0))
    lane_vec = pl.BlockSpec((1, LANE), lambda i: (0, 0))
    return pl.pallas_call(
        body, name="assemble_dproj", grid=(t // tr,),
        in_specs=[head] * 6 + [row, row, lane_row, pl.BlockSpec((tr, LANE), lambda i: (i, f_blk)),
                               lane_row, lane_row, lane_row, lane_vec],
        out_specs=[pl.BlockSpec((tr, np_), lambda i: (i, 0)), lane_vec],
        out_shape=[jax.ShapeDtypeStruct((t, np_), BF), jax.ShapeDtypeStruct((1, LANE), F32)],
        compiler_params=_params(),
    )(dqd, dkd, dvd, dqf, dkf, dvf, dgd, dgf, dlogf, proj, *tables, bf_pad)


def _to_rows(a, tq):
    h, t = a.shape
    return a.reshape(h, t // tq, 1, tq)


def kernel(x, ffn1_norm, ffn1_w_gate, ffn1_w_up, ffn1_w_down, mix_norm, w_in, b_forget, b_gate_dil, b_gate_fox, w_proj_dil, w_proj_fox, w_out, ffn2_norm, ffn2_w_gate, ffn2_w_up, ffn2_w_down, final_norm, loss_target, m_ffn1_norm, m_ffn1_w_gate, m_ffn1_w_up, m_ffn1_w_down, m_mix_norm, m_w_in, m_b_forget, m_b_gate_dil, m_b_gate_fox, m_w_proj_dil, m_w_proj_fox, m_w_out, m_ffn2_norm, m_ffn2_w_gate, m_ffn2_w_up, m_ffn2_w_down, m_final_norm, v_ffn1_norm, v_ffn1_w_gate, v_ffn1_w_up, v_ffn1_w_down, v_mix_norm, v_w_in, v_b_forget, v_b_gate_dil, v_b_gate_fox, v_w_proj_dil, v_w_proj_fox, v_w_out, v_ffn2_norm, v_ffn2_w_gate, v_ffn2_w_up, v_ffn2_w_down, v_final_norm):
    t, d = x.shape[1], x.shape[2]
    hd = w_proj_dil.shape[1]
    nh = hd // HEAD_DIM
    n_f = b_forget.shape[1]
    cols = w_in.shape[2]
    in_cols = N_DEV * cols
    assert in_cols == 6 * hd + n_f + 2 * d and n_f == nh and n_f <= LANE
    np_ = 6 * hd + 2 * d + F_PAD
    scale = HEAD_DIM ** -0.5
    tq = _tile(t, 512, LANE)
    assert MAX_WINDOW % tq == 0 and tq % 16 == 0

    x2d = x[0]
    tgt = loss_target[0]

    def rows(w):
        return jnp.swapaxes(w, 1, 2)

    fc = N_DEV * ffn1_w_down.shape[1]
    ag_order = [rows(ffn1_w_gate), rows(ffn1_w_up), ffn1_w_down, w_in, w_proj_dil, w_proj_fox, w_out,
                rows(ffn2_w_gate), rows(ffn2_w_up), ffn2_w_down]
    ag_first, tok = _exchange_start([w[0].astype(BF) for w in ag_order[:2]], True, "ag_start_first", ks=FIRST_LEVEL)
    ag_rest, ag_token = _exchange_start([w[0].astype(BF) for w in ag_order[2:]], True, "ag_start", dep=tok,
                                        ks=FIRST_LEVEL)
    ag = ag_first + ag_rest

    def relay(idx, after, name):
        for i, h in zip(idx, _gather_relay([ag[i] for i in idx], after, name)):
            ag[i] = h

    def gathered(idx, after, name):
        return _gather_wait([ag[i] for i in idx], after, name)

    def ffn_weight(idx, after, name):
        return [w.reshape(fc, d) for w in gathered(idx, after, name)]

    tables = _rope_tables(t)
    bf_pad = jnp.pad(b_forget, ((0, 0), (0, LANE - n_f)))

    hn1, = _rms_fwd(x2d, ffn1_norm, "rms_ffn1", dep=ag_token)
    relay([0], hn1, "ag_relay_ffn1_gate")
    wg1, = ffn_weight([0], hn1, "ag_wait_ffn1_gate")
    g1_f32 = _ffn_gate(hn1, wg1, "ffn1_gate")
    relay([1], g1_f32, "ag_relay_ffn1_up")
    wu1, = ffn_weight([1], g1_f32, "ag_wait_ffn1_up")
    relay([2], wu1, "ag_relay_ffn1_down")
    g1, u1, a1 = _ffn_up_act(hn1, wu1, g1_f32, "ffn1_up_act")
    wd1, = ffn_weight([2], a1, "ag_wait_ffn1_down")
    relay([3], wd1, "ag_relay_w_in")
    x1 = _ffn_down(a1, wd1, x2d, "ffn1_down")

    hm, hm_t = _rms_fwd(x1, mix_norm, "rms_mix", with_transpose=True)
    win_g, = gathered([3], hm, "ag_wait_w_in")
    relay([4, 5, 6], win_g, "ag_relay_mixer")
    segments = [(0, 6 * hd), (6 * hd + n_f, in_cols), (6 * hd, 6 * hd + n_f)]
    pieces = []
    for lo, hi in segments:
        for j in range(lo // cols, (hi - 1) // cols + 1):
            s, e = max(lo, j * cols), min(hi, (j + 1) * cols)
            pieces.append(win_g[j, :, s - j * cols:e - j * cols])
    win_p = jnp.concatenate(pieces + [jnp.zeros((d, F_PAD - n_f), BF)], axis=1)
    proj = _mm_nn(hm, win_p, F32, "w_in_fwd", tm_pref=1024, tn_pref=W_IN_COLS)
    qd, kd, vd, qf, kf, vf, logf = _mixer_prep(proj, tables, bf_pad, hd, scale)
    csum = _cumsum_rows(logf, False, "cumsum_logf")
    c_heads = csum[:, :nh].T
    c_row = _to_rows(c_heads, tq)
    c_rep = jnp.broadcast_to(c_heads[:, :, None], (nh, t, LANE))
    dil_bias = _dil_bias_tiles(tq)
    dil_bias_t = dil_bias.transpose(0, 2, 1)
    relay([7, 8, 9], qd, "ag_relay_ffn2")
    yd, lse_d, lse_d_row = _attn_fwd("dil", qd, kd, vd, dil_bias, tq, "attn_dil_fwd")
    yf, lse_f, lse_f_row = _attn_fwd("fox", qf, kf, vf, c_row, tq, "attn_fox_fwd")
    wpd_g, wpf_g = gathered([4, 5], yf, "ag_wait_proj")
    wpd = wpd_g.transpose(1, 0, 2).reshape(hd, d)
    wpf = wpf_g.transpose(1, 0, 2).reshape(hd, d)
    pd, pf, merged = _proj_merge(yd, yf, wpd, wpf, proj, b_gate_dil, b_gate_fox, hd)
    wout_g, = gathered([6], merged, "ag_wait_w_out")
    wout = wout_g.reshape(d, d)
    x2 = _mm_nn(merged, wout, F32, "w_out_fwd", residual=x1, tn_pref=1024)

    hn2, = _rms_fwd(x2, ffn2_norm, "rms_ffn2")
    wg2, wu2 = ffn_weight([7, 8], hn2, "ag_wait_ffn2_gate_up")
    g2, u2, a2 = _ffn_gate_up(hn2, wg2, wu2, "ffn2_gate_up")
    wd2, = ffn_weight([9], a2, "ag_wait_ffn2_down")
    x3 = _ffn_down(a2, wd2, x2, "ffn2_down")

    dx3, dx3b, d_final, loss_lanes = _loss_head(x3, final_norm.reshape(1, d), tgt)

    def ffn_bwd(dxb, hn, g, u, a, wg_t, wu_t, wd, x_in, gain, dres, tag):
        def parts(dw):
            return dw.reshape(N_DEV, fc // N_DEV, d)

        dg, du = _ffn_bwd_hidden(dxb, wd, g, u, tag + "_bwd_hidden")
        dwd, = _ffn_dw([a], dxb, 0.5, tag + "_dw_down")
        rs_down, tok = _exchange_start([parts(dwd)], False, "rs_start_" + tag + "_down")
        dwg_t, dwu_t = _ffn_dw([dg, du], hn, 1.0, tag + "_dw_gate_up", dep=tok)
        rs_gu, tok = _exchange_start([parts(dwg_t), parts(dwu_t)], False, "rs_start_" + tag + "_gate_up")
        dhn = _ffn_bwd_input(dg, du, wg_t, wu_t, tag + "_bwd_input", dep=tok)
        dx, dx_bf, dgain = _rms_bwd(dhn, x_in, gain, dres, "rms_" + tag + "_bwd")
        return dx, dx_bf, dgain, rs_gu + rs_down

    dx2, dx2b, d_ffn2_norm, rs_ffn2 = ffn_bwd(dx3b, hn2, g2, u2, a2, wg2, wu2, wd2, x2, ffn2_norm, dx3, "ffn2")

    dmerged = _mm_nt(dx2b, wout, F32, "w_out_bwd")
    dwout = _mm_tn(merged, dx2b, BF, "w_out_dw", tn_pref=1024, tk_pref=DW_ROWS)
    dpd, dpf, dgd, dgf, d_bd, d_bf = _merge_bwd(dmerged, pd, pf, proj, b_gate_dil, b_gate_fox, hd)
    dyd = _mm_nt(dpd, wpd, BF, "proj_dil_bwd")
    dyf = _mm_nt(dpf, wpf, BF, "proj_fox_bwd")
    dwpd = _mm_tn(yd, dpd, BF, "proj_dil_dw", tn_pref=1024, tk_pref=DW_ROWS)
    dwpf = _mm_tn(yf, dpf, BF, "proj_fox_dw", tn_pref=1024, tk_pref=DW_ROWS)
    dwpd_c = dwpd.reshape(hd, N_DEV, d // N_DEV).transpose(1, 0, 2)
    dwpf_c = dwpf.reshape(hd, N_DEV, d // N_DEV).transpose(1, 0, 2)
    dwout_c = dwout.reshape(N_DEV, d // N_DEV, d)
    rs_mix, tok = _exchange_start([dwout_c, dwpd_c, dwpf_c], False, "rs_start_mixer")

    dqd, dl_d = _attn_bwd_dq("dil", qd, kd, vd, yd, dyd, lse_d, dil_bias, tq, "attn_dil_dq", dep=tok)
    dkd, dvd = _attn_bwd_dkv("dil", qd, kd, vd, dyd, lse_d_row, dl_d, dil_bias_t, None, tq, "attn_dil_dkv")
    dqf, dl_f = _attn_bwd_dq("fox", qf, kf, vf, yf, dyf, lse_f, c_row, tq, "attn_fox_dq")
    dkf, dvf, dc = _attn_bwd_dkv("fox", qf, kf, vf, dyf, lse_f_row, dl_f, c_rep, c_row, tq, "attn_fox_dkv")
    dc_pad = jnp.pad(dc[:, :, 0, :].reshape(nh, t).T, ((0, 0), (0, LANE - nh)))
    dlogf = _cumsum_rows(dc_pad, True, "revcumsum_dc")
    dproj, d_bforget = _assemble_dproj(dqd, dkd, dvd, dqf, dkf, dvf, dgd, dgf, dlogf, proj, tables, bf_pad, scale)

    dwin_p = _mm_tn(hm_t, dproj, BF, "w_in_dw", tn_pref=W_IN_COLS // 2, tk_pref=DW_ROWS, a_transposed=True)

    def perm_col(c):
        if c < 6 * hd:
            return c
        return c + 2 * d if c < 6 * hd + n_f else c - n_f

    shards = []
    for j in range(N_DEV):
        cuts = sorted({j * cols, (j + 1) * cols} | {c for c in (6 * hd, 6 * hd + n_f) if j * cols < c < (j + 1) * cols})
        shards.append(jnp.concatenate([dwin_p[:, perm_col(lo):perm_col(lo) + hi - lo]
                                       for lo, hi in zip(cuts[:-1], cuts[1:])], axis=1))
    dwin_c = jnp.stack(shards)
    rs_win, tok = _exchange_start([dwin_c], False, "rs_start_w_in")
    dhm = _mm_nt(dproj, win_p, F32, "w_in_bwd", tm_pref=1024, tn_pref=d, tk_pref=W_IN_COLS, dep=tok)
    dx1, dx1b, d_mix_norm = _rms_bwd(dhm, x1, mix_norm, dx2, "rms_mix_bwd")

    grad_x, _, d_ffn1_norm, rs_ffn1 = ffn_bwd(dx1b, hn1, g1, u1, a1, wg1, wu1, wd1, x2d, ffn1_norm, dx1, "ffn1")

    def update(handles, names, after, tag):
        recvs = _exchange_wait(handles, False, after, "rs_wait_" + tag)
        res = {}
        for recv, n in zip(recvs, names):
            turn = rows if n.endswith(("w_gate", "w_up")) else (lambda a: a)
            w, m, v = (turn(a)[0] for a in wmv[n])
            res[n] = tuple(turn(o[None]) for o in _adam_from_partials(recv, w, m, v, "adam_" + n))
        return res, res[names[-1]][0]

    wmv = {
        "ffn1_w_gate": (ffn1_w_gate, m_ffn1_w_gate, v_ffn1_w_gate),
        "ffn1_w_up": (ffn1_w_up, m_ffn1_w_up, v_ffn1_w_up),
        "ffn1_w_down": (ffn1_w_down, m_ffn1_w_down, v_ffn1_w_down),
        "w_in": (w_in, m_w_in, v_w_in),
        "w_proj_dil": (w_proj_dil, m_w_proj_dil, v_w_proj_dil),
        "w_proj_fox": (w_proj_fox, m_w_proj_fox, v_w_proj_fox),
        "w_out": (w_out, m_w_out, v_w_out),
        "ffn2_w_gate": (ffn2_w_gate, m_ffn2_w_gate, v_ffn2_w_gate),
        "ffn2_w_up": (ffn2_w_up, m_ffn2_w_up, v_ffn2_w_up),
        "ffn2_w_down": (ffn2_w_down, m_ffn2_w_down, v_ffn2_w_down),
    }
    big = {}
    after = grad_x
    for handles, names, tag in [
            (rs_ffn2, ["ffn2_w_gate", "ffn2_w_up", "ffn2_w_down"], "ffn2"),
            (rs_mix, ["w_out", "w_proj_dil", "w_proj_fox"], "mixer"),
            (rs_win, ["w_in"], "w_in"),
            (rs_ffn1, ["ffn1_w_gate", "ffn1_w_up", "ffn1_w_down"], "ffn1")]:
        res, after = update(handles, names, after, tag)
        big.update(res)

    def lanes(a):
        a = a.reshape(1, -1)
        return jnp.pad(a, ((0, 0), (0, d - a.shape[1])))

    small_names = ["ffn1_norm", "mix_norm", "b_gate_dil", "b_gate_fox", "ffn2_norm", "final_norm", "b_forget"]
    small_g = [d_ffn1_norm, d_mix_norm, d_bd, d_bf, d_ffn2_norm, d_final, d_bforget[:, :n_f]]
    small_w = [ffn1_norm, mix_norm, b_gate_dil, b_gate_fox, ffn2_norm, final_norm, b_forget]
    small_m = [m_ffn1_norm, m_mix_norm, m_b_gate_dil, m_b_gate_fox, m_ffn2_norm, m_final_norm, m_b_forget]
    small_v = [v_ffn1_norm, v_mix_norm, v_b_gate_dil, v_b_gate_fox, v_ffn2_norm, v_final_norm, v_b_forget]
    pack = lambda arrs, last: jnp.concatenate([lanes(a) for a in arrs] + [last], axis=0)
    g_all = _allreduce_small(pack(small_g, loss_lanes))
    zero_row = jnp.zeros((1, d), F32)
    one_row = jnp.ones((1, d), F32)
    s_delta, s_m, s_v = _adam_small(g_all, pack(small_w, zero_row), pack(small_m, zero_row), pack(small_v, one_row))
    loss = g_all[len(small_names), 0]

    def unpack(packed, i, like):
        return packed[i, :like.size].reshape(like.shape)

    small = {}
    for i, (n, w) in enumerate(zip(small_names, small_w)):
        small[n] = (unpack(g_all, i, w), unpack(s_delta, i, w), unpack(s_m, i, w), unpack(s_v, i, w))

    order = ["ffn1_norm", "ffn1_w_gate", "ffn1_w_up", "ffn1_w_down", "mix_norm", "w_in", "b_forget", "b_gate_dil",
             "b_gate_fox", "w_proj_dil", "w_proj_fox", "w_out", "ffn2_norm", "ffn2_w_gate", "ffn2_w_up",
             "ffn2_w_down", "final_norm"]
    res = {**big, **small}
    outs = [loss, grad_x[None]]
    for slot in range(4):
        outs += [res[n][slot] for n in order]
    return tuple(outs)
```
